```python
import jax
import jax.numpy as jnp
from jax import lax
import numpy as np


D_MODEL = 2048
BATCH = 8
SEQ = 4096
DEPTH = 4

A_WIDTH = D_MODEL // 2
B_WIDTH = D_MODEL - A_WIDTH
A_GROUPS = 8
A_GROUP_DIM = A_WIDTH // A_GROUPS
CHUNK = 128
CONV_WIDTH = 3
ATTN_HEAD_DIM = 128
ATTN_HEADS = D_MODEL // ATTN_HEAD_DIM
DILATED_BRANCHES = ((128, 1), (512, 4), (2048, 16))
ATTN_BLOCK = 128
ROPE_THETA = 10000.0
FFN_DIM = 4 * D_MODEL
N_EVEN = (DEPTH + 1) // 2
N_ODD = DEPTH // 2
RMS_EPS = 1e-6
LN_EPS = 1e-5

kernel_name = 'hybrid_gmlp_shortconv_dilated_attn_trunk'


def rmsnorm(x, g):
    xf = x.astype(jnp.float32)
    y = xf * lax.rsqrt(jnp.mean(xf * xf, axis=-1, keepdims=True) + RMS_EPS)
    return (y * g.astype(jnp.float32)).astype(x.dtype)


def rope(t, positions):
    dh = t.shape[-1]
    half = dh // 2
    inv_freq = ROPE_THETA ** (-jnp.arange(half, dtype=jnp.float32) * 2.0 / dh)
    ang = positions.astype(jnp.float32)[:, None] * inv_freq[None, :]
    cos = jnp.cos(ang)[None, :, None, :]
    sin = jnp.sin(ang)[None, :, None, :]
    t1 = t[..., :half].astype(jnp.float32)
    t2 = t[..., half:].astype(jnp.float32)
    out = jnp.concatenate([t1 * cos - t2 * sin, t2 * cos + t1 * sin], axis=-1)
    return out.astype(t.dtype)


def gmlp_shortconv_mixer(h, w_in, w_s, b_s, conv_w, w_out):
    bsz, s, _ = h.shape
    proj = h @ w_in
    a_u, a_v, g_b, g_c, b_x = jnp.split(
        proj, [A_WIDTH, 2 * A_WIDTH, 2 * A_WIDTH + B_WIDTH, 2 * A_WIDTH + 2 * B_WIDTH], axis=-1)

    a_u = jax.nn.gelu(a_u)
    a_v = jax.nn.gelu(a_v)
    vf = a_v.astype(jnp.float32)
    mu = jnp.mean(vf, axis=-1, keepdims=True)
    var = jnp.mean(jnp.square(vf - mu), axis=-1, keepdims=True)
    vn = ((vf - mu) * lax.rsqrt(var + LN_EPS)).astype(h.dtype)
    n_chunks = s // CHUNK
    vn = vn.reshape(bsz, n_chunks, CHUNK, A_GROUPS, A_GROUP_DIM)
    causal = jnp.tril(jnp.ones((CHUNK, CHUNK), dtype=bool))
    w_causal = jnp.where(causal[None], w_s, 0)
    mixed = jnp.einsum('gts,bnsgc->bntgc', w_causal, vn) + b_s.T[None, None, :, :, None]
    a_out = a_u * mixed.reshape(bsz, s, A_WIDTH)

    z = g_c * b_x
    zp = jnp.pad(z, ((0, 0), (CONV_WIDTH - 1, 0), (0, 0)))
    y = conv_w[0] * zp[:, 0:s]
    for tap in range(1, CONV_WIDTH):
        y = y + conv_w[tap] * zp[:, tap:tap + s]
    b_out = g_b * y

    return jnp.concatenate([a_out, b_out], axis=-1) @ w_out


def dilated_branch(q, k, v, window, dilation):
    bsz, s, nh, dh = q.shape
    n_steps = window // dilation
    blk = ATTN_BLOCK
    L = s // dilation
    nb = -(-L // blk)
    Lp = nb * blk

    def to_blocks(t):
        t = t.reshape(bsz, L, dilation, nh, dh).transpose(0, 2, 3, 1, 4)
        t = jnp.pad(t, ((0, 0), (0, 0), (0, 0), (0, Lp - L), (0, 0)))
        return t.reshape(bsz, dilation, nh, nb, blk, dh)

    def with_prev_block(t):
        prev = jnp.pad(t, ((0, 0), (0, 0), (0, 0), (1, 0), (0, 0), (0, 0)))[:, :, :, :-1]
        return jnp.concatenate([prev, t], axis=4)

    qb = to_blocks(q)
    kk = with_prev_block(to_blocks(k))
    vv = with_prev_block(to_blocks(v))

    scores = jnp.einsum('brhnqc,brhnkc->brhnqk', qb, kk).astype(jnp.float32) * (dh ** -0.5)
    qi = jnp.arange(blk)[:, None]
    ki = jnp.arange(2 * blk)[None, :]
    step = qi + blk - ki
    band = (step >= 0) & (step <= n_steps)
    key_idx = jnp.arange(nb)[:, None, None] * blk - blk + ki[None]
    mask = band[None] & (key_idx >= 0)
    scores = jnp.where(mask, scores, -jnp.inf)
    m = jnp.max(scores, axis=-1, keepdims=True)
    p = jnp.exp(scores - m)
    denom = jnp.sum(p, axis=-1, keepdims=True)
    o = jnp.einsum('brhnqk,brhnkc->brhnqc', (p / denom).astype(v.dtype), vv)
    lse = (m + jnp.log(denom))[..., 0]

    o = o.reshape(bsz, dilation, nh, Lp, dh)[:, :, :, :L].transpose(0, 3, 1, 2, 4).reshape(bsz, s, nh, dh)
    lse = lse.reshape(bsz, dilation, nh, Lp)[..., :L].transpose(0, 3, 1, 2).reshape(bsz, s, nh)
    return o, lse


def dilated_attention_mixer(h, w_qkv, w_o, positions):
    bsz, s, _ = h.shape
    qkv = h @ w_qkv
    q, k, v = jnp.split(qkv, 3, axis=-1)
    q = rope(q.reshape(bsz, s, ATTN_HEADS, ATTN_HEAD_DIM), positions)
    k = rope(k.reshape(bsz, s, ATTN_HEADS, ATTN_HEAD_DIM), positions)
    v = v.reshape(bsz, s, ATTN_HEADS, ATTN_HEAD_DIM)
    outs = []
    lses = []
    for window, dilation in DILATED_BRANCHES:
        o_i, lse_i = dilated_branch(q, k, v, window, dilation)
        outs.append(o_i.astype(jnp.float32))
        lses.append(lse_i)
    alpha = jax.nn.softmax(jnp.stack(lses, axis=0), axis=0)
    o = jnp.einsum('ibsh,ibshc->bshc', alpha, jnp.stack(outs, axis=0))
    return o.astype(h.dtype).reshape(bsz, s, ATTN_HEADS * ATTN_HEAD_DIM) @ w_o


def squared_relu_mlp(h, w_up, w_down):
    return jnp.square(jax.nn.relu(h @ w_up)) @ w_down


def _fwd_setup_inputs(seed: int = 0) -> dict:
    key = jax.random.key(seed)
    ks = jax.random.split(key, 14)
    f32 = jnp.float32
    d = D_MODEL
    x = jax.random.normal(ks[0], (BATCH, SEQ, d), f32)
    norm_mix_pre = 1.0 + 0.05 * jax.random.normal(ks[1], (DEPTH, d), f32)
    norm_mix_post = 1.0 + 0.05 * jax.random.normal(ks[2], (DEPTH, d), f32)
    norm_mlp_pre = 1.0 + 0.05 * jax.random.normal(ks[3], (DEPTH, d), f32)
    norm_mlp_post = 1.0 + 0.05 * jax.random.normal(ks[4], (DEPTH, d), f32)
    in_cols = 2 * A_WIDTH + 3 * B_WIDTH
    w_in_ab = jax.random.normal(ks[5], (N_EVEN, d, in_cols), f32) * d ** -0.5
    w_spatial = jax.random.normal(ks[6], (N_EVEN, A_GROUPS, CHUNK, CHUNK), f32) * (0.5 * CHUNK ** -0.5)
    b_spatial = 1.0 + 0.1 * jax.random.normal(ks[7], (N_EVEN, A_GROUPS, CHUNK), f32)
    conv_w = jax.random.normal(ks[8], (N_EVEN, CONV_WIDTH, B_WIDTH), f32) * CONV_WIDTH ** -0.5
    w_out_ab = jax.random.normal(ks[9], (N_EVEN, A_WIDTH + B_WIDTH, d), f32) * (A_WIDTH + B_WIDTH) ** -0.5
    attn_width = ATTN_HEADS * ATTN_HEAD_DIM
    w_qkv = jax.random.normal(ks[10], (N_ODD, d, 3 * attn_width), f32) * d ** -0.5
    w_o = jax.random.normal(ks[11], (N_ODD, attn_width, d), f32) * attn_width ** -0.5
    w_up = jax.random.normal(ks[12], (DEPTH, d, FFN_DIM), f32) * d ** -0.5
    w_down = jax.random.normal(ks[13], (DEPTH, FFN_DIM, d), f32) * FFN_DIM ** -0.5
    return {'x': x, 'norm_mix_pre': norm_mix_pre, 'norm_mix_post': norm_mix_post,
            'norm_mlp_pre': norm_mlp_pre, 'norm_mlp_post': norm_mlp_post,
            'w_in_ab': w_in_ab, 'w_spatial': w_spatial, 'b_spatial': b_spatial,
            'conv_w': conv_w, 'w_out_ab': w_out_ab, 'w_qkv': w_qkv, 'w_o': w_o,
            'w_up': w_up, 'w_down': w_down}


def _fwd_reference(x, norm_mix_pre, norm_mix_post, norm_mlp_pre, norm_mlp_post,
              w_in_ab, w_spatial, b_spatial, conv_w, w_out_ab, w_qkv, w_o,
              w_up, w_down):
    positions = jnp.arange(x.shape[1], dtype=jnp.int32)
    h = x
    for layer in range(DEPTH):
        hn = rmsnorm(h, norm_mix_pre[layer])
        if layer % 2 == 0:
            e = layer // 2
            mix = gmlp_shortconv_mixer(hn, w_in_ab[e], w_spatial[e], b_spatial[e], conv_w[e], w_out_ab[e])
        else:
            o = layer // 2
            mix = dilated_attention_mixer(hn, w_qkv[o], w_o[o], positions)
        h = h + rmsnorm(mix, norm_mix_post[layer])
        f = squared_relu_mlp(rmsnorm(h, norm_mlp_pre[layer]), w_up[layer], w_down[layer])
        h = h + rmsnorm(f, norm_mlp_post[layer])
    return h


import jax as _jax
import jax.numpy as _jnp

TWIN_FORMAT = 'train_step'
FWD_PARAMS = ['x', 'norm_mix_pre', 'norm_mix_post', 'norm_mlp_pre', 'norm_mlp_post', 'w_in_ab', 'w_spatial', 'b_spatial', 'conv_w', 'w_out_ab', 'w_qkv', 'w_o', 'w_up', 'w_down']
TWIN_WEIGHTS = ['norm_mix_pre', 'norm_mix_post', 'norm_mlp_pre', 'norm_mlp_post', 'w_in_ab', 'w_spatial', 'b_spatial', 'conv_w', 'w_out_ab', 'w_qkv', 'w_o', 'w_up', 'w_down']
TWIN_DIFF_INPUT = 'x'
TWIN_INPUTS = ['x', 'norm_mix_pre', 'norm_mix_post', 'norm_mlp_pre', 'norm_mlp_post', 'w_in_ab', 'w_spatial', 'b_spatial', 'conv_w', 'w_out_ab', 'w_qkv', 'w_o', 'w_up', 'w_down', 'loss_target', 'm_norm_mix_pre', 'm_norm_mix_post', 'm_norm_mlp_pre', 'm_norm_mlp_post', 'm_w_in_ab', 'm_w_spatial', 'm_b_spatial', 'm_conv_w', 'm_w_out_ab', 'm_w_qkv', 'm_w_o', 'm_w_up', 'm_w_down', 'v_norm_mix_pre', 'v_norm_mix_post', 'v_norm_mlp_pre', 'v_norm_mlp_post', 'v_w_in_ab', 'v_w_spatial', 'v_b_spatial', 'v_conv_w', 'v_w_out_ab', 'v_w_qkv', 'v_w_o', 'v_w_up', 'v_w_down']
TWIN_OUTPUTS = ['loss', 'grad_x', 'grad_norm_mix_pre', 'grad_norm_mix_post', 'grad_norm_mlp_pre', 'grad_norm_mlp_post', 'grad_w_in_ab', 'grad_w_spatial', 'grad_b_spatial', 'grad_conv_w', 'grad_w_out_ab', 'grad_w_qkv', 'grad_w_o', 'grad_w_up', 'grad_w_down', 'delta_norm_mix_pre', 'delta_norm_mix_post', 'delta_norm_mlp_pre', 'delta_norm_mlp_post', 'delta_w_in_ab', 'delta_w_spatial', 'delta_b_spatial', 'delta_conv_w', 'delta_w_out_ab', 'delta_w_qkv', 'delta_w_o', 'delta_w_up', 'delta_w_down', 'new_m_norm_mix_pre', 'new_m_norm_mix_post', 'new_m_norm_mlp_pre', 'new_m_norm_mlp_post', 'new_m_w_in_ab', 'new_m_w_spatial', 'new_m_b_spatial', 'new_m_conv_w', 'new_m_w_out_ab', 'new_m_w_qkv', 'new_m_w_o', 'new_m_w_up', 'new_m_w_down', 'new_v_norm_mix_pre', 'new_v_norm_mix_post', 'new_v_norm_mlp_pre', 'new_v_norm_mlp_post', 'new_v_w_in_ab', 'new_v_w_spatial', 'new_v_b_spatial', 'new_v_conv_w', 'new_v_w_out_ab', 'new_v_w_qkv', 'new_v_w_o', 'new_v_w_up', 'new_v_w_down']
TWIN_LEAF_KINDS = {'loss': 'loss', 'grad_x': 'grad_x', 'grad_norm_mix_pre': 'grad_w', 'grad_norm_mix_post': 'grad_w', 'grad_norm_mlp_pre': 'grad_w', 'grad_norm_mlp_post': 'grad_w', 'grad_w_in_ab': 'grad_w', 'grad_w_spatial': 'grad_w', 'grad_b_spatial': 'grad_w', 'grad_conv_w': 'grad_w', 'grad_w_out_ab': 'grad_w', 'grad_w_qkv': 'grad_w', 'grad_w_o': 'grad_w', 'grad_w_up': 'grad_w', 'grad_w_down': 'grad_w', 'delta_norm_mix_pre': 'delta_w', 'delta_norm_mix_post': 'delta_w', 'delta_norm_mlp_pre': 'delta_w', 'delta_norm_mlp_post': 'delta_w', 'delta_w_in_ab': 'delta_w', 'delta_w_spatial': 'delta_w', 'delta_b_spatial': 'delta_w', 'delta_conv_w': 'delta_w', 'delta_w_out_ab': 'delta_w', 'delta_w_qkv': 'delta_w', 'delta_w_o': 'delta_w', 'delta_w_up': 'delta_w', 'delta_w_down': 'delta_w', 'new_m_norm_mix_pre': 'new_m', 'new_m_norm_mix_post': 'new_m', 'new_m_norm_mlp_pre': 'new_m', 'new_m_norm_mlp_post': 'new_m', 'new_m_w_in_ab': 'new_m', 'new_m_w_spatial': 'new_m', 'new_m_b_spatial': 'new_m', 'new_m_conv_w': 'new_m', 'new_m_w_out_ab': 'new_m', 'new_m_w_qkv': 'new_m', 'new_m_w_o': 'new_m', 'new_m_w_up': 'new_m', 'new_m_w_down': 'new_m', 'new_v_norm_mix_pre': 'new_v', 'new_v_norm_mix_post': 'new_v', 'new_v_norm_mlp_pre': 'new_v', 'new_v_norm_mlp_post': 'new_v', 'new_v_w_in_ab': 'new_v', 'new_v_w_spatial': 'new_v', 'new_v_b_spatial': 'new_v', 'new_v_conv_w': 'new_v', 'new_v_w_out_ab': 'new_v', 'new_v_w_qkv': 'new_v', 'new_v_w_o': 'new_v', 'new_v_w_up': 'new_v', 'new_v_w_down': 'new_v'}


def _forward(args):
    return _fwd_reference(*[args[k] for k in FWD_PARAMS])


def _output_shape():
    def fwd():
        inp = _fwd_setup_inputs(0)
        return _fwd_reference(*[inp[k] for k in FWD_PARAMS])
    out = _jax.eval_shape(fwd)
    return out.shape, out.dtype

N_MICROBATCH = 1
ADAM_LR = 0.001
ADAM_B1 = 0.9
ADAM_B2 = 0.999
ADAM_EPS = 1e-08
ADAM_WD = 0.01
ADAM_STEP = 10
PER_EXAMPLE_BATCH_AXIS = {'x': 0, 'loss_target': 0}
SHARED_INPUTS = []
_WEIGHT_DTYPES = {'norm_mix_pre': _jnp.float32, 'norm_mix_post': _jnp.float32, 'norm_mlp_pre': _jnp.float32, 'norm_mlp_post': _jnp.float32, 'w_in_ab': _jnp.float32, 'w_spatial': _jnp.float32, 'b_spatial': _jnp.float32, 'conv_w': _jnp.float32, 'w_out_ab': _jnp.float32, 'w_qkv': _jnp.float32, 'w_o': _jnp.float32, 'w_up': _jnp.float32, 'w_down': _jnp.float32}
MOMENT_SCALE = {'norm_mix_pre': 1.858397e+01, 'norm_mix_post': 2.848757e+01, 'norm_mlp_pre': 1.080177e+01, 'norm_mlp_post': 2.605856e+01, 'w_in_ab': 4.738439e+00, 'w_spatial': 5.154657e-01, 'b_spatial': 1.401991e+00, 'conv_w': 4.156735e+00, 'w_out_ab': 1.517167e+01, 'w_qkv': 1.447046e+01, 'w_o': 2.462455e+01, 'w_up': 5.419507e+00, 'w_down': 2.012820e+01}


def _to_microbatches(a, axis):
    t = _jnp.moveaxis(a, axis, 0)
    t = t.reshape((N_MICROBATCH, t.shape[0] // N_MICROBATCH) + t.shape[1:])
    return _jnp.moveaxis(t, 1, axis + 1)


def setup_inputs(seed: int = 0) -> dict:
    inp = _fwd_setup_inputs(seed)
    key = _jax.random.fold_in(_jax.random.key(seed), 7919)
    shape, _ = _output_shape()
    out = dict(inp)
    out["loss_target"] = _jax.random.normal(_jax.random.fold_in(key, 0), shape, _jnp.float32)
    for i, name in enumerate(TWIN_WEIGHTS):
        w = inp[name].astype(_jnp.float32)
        if MOMENT_SCALE is None:
            s = _jnp.sqrt(_jnp.mean(_jnp.square(w)) + 1e-30)
        else:
            s = MOMENT_SCALE[name]
        km, kv = _jax.random.split(_jax.random.fold_in(key, i + 1))
        out[name] = w
        out["m_" + name] = s * _jax.random.normal(km, w.shape, _jnp.float32)
        out["v_" + name] = (s * s) * _jax.random.uniform(kv, w.shape, _jnp.float32, 0.5, 1.5)
    if N_MICROBATCH > 1:
        for name, axis in PER_EXAMPLE_BATCH_AXIS.items():
            out[name] = _to_microbatches(out[name], axis)
    return {'x': out['x'], 'norm_mix_pre': out['norm_mix_pre'], 'norm_mix_post': out['norm_mix_post'], 'norm_mlp_pre': out['norm_mlp_pre'], 'norm_mlp_post': out['norm_mlp_post'], 'w_in_ab': out['w_in_ab'], 'w_spatial': out['w_spatial'], 'b_spatial': out['b_spatial'], 'conv_w': out['conv_w'], 'w_out_ab': out['w_out_ab'], 'w_qkv': out['w_qkv'], 'w_o': out['w_o'], 'w_up': out['w_up'], 'w_down': out['w_down'], 'loss_target': out['loss_target'], 'm_norm_mix_pre': out['m_norm_mix_pre'], 'm_norm_mix_post': out['m_norm_mix_post'], 'm_norm_mlp_pre': out['m_norm_mlp_pre'], 'm_norm_mlp_post': out['m_norm_mlp_post'], 'm_w_in_ab': out['m_w_in_ab'], 'm_w_spatial': out['m_w_spatial'], 'm_b_spatial': out['m_b_spatial'], 'm_conv_w': out['m_conv_w'], 'm_w_out_ab': out['m_w_out_ab'], 'm_w_qkv': out['m_w_qkv'], 'm_w_o': out['m_w_o'], 'm_w_up': out['m_w_up'], 'm_w_down': out['m_w_down'], 'v_norm_mix_pre': out['v_norm_mix_pre'], 'v_norm_mix_post': out['v_norm_mix_post'], 'v_norm_mlp_pre': out['v_norm_mlp_pre'], 'v_norm_mlp_post': out['v_norm_mlp_post'], 'v_w_in_ab': out['v_w_in_ab'], 'v_w_spatial': out['v_w_spatial'], 'v_b_spatial': out['v_b_spatial'], 'v_conv_w': out['v_conv_w'], 'v_w_out_ab': out['v_w_out_ab'], 'v_w_qkv': out['v_w_qkv'], 'v_w_o': out['v_w_o'], 'v_w_up': out['v_w_up'], 'v_w_down': out['v_w_down']}


def _loss(weights, diff, rest, loss_target):
    with _jax.named_scope("forward"):
        args = {**rest, TWIN_DIFF_INPUT: diff, **{k: w.astype(_WEIGHT_DTYPES[k]) for k, w in weights.items()}}
        y = _forward(args)
    with _jax.named_scope("loss_head"):
        err = _jnp.square(y.astype(_jnp.float32) - loss_target)
        return 0.5 * _jnp.sum(_jnp.mean(err, axis=-1)) if err.ndim else 0.5 * err


def _adamw(w, g, m, v):
    m = ADAM_B1 * m + (1.0 - ADAM_B1) * g
    v = ADAM_B2 * v + (1.0 - ADAM_B2) * _jnp.square(g)
    m_hat = m / (1.0 - ADAM_B1 ** ADAM_STEP)
    v_hat = v / (1.0 - ADAM_B2 ** ADAM_STEP)
    delta = -ADAM_LR * (m_hat / (_jnp.sqrt(v_hat) + ADAM_EPS) + ADAM_WD * w)
    return delta, m, v


def reference(x, norm_mix_pre, norm_mix_post, norm_mlp_pre, norm_mlp_post, w_in_ab, w_spatial, b_spatial, conv_w, w_out_ab, w_qkv, w_o, w_up, w_down, loss_target, m_norm_mix_pre, m_norm_mix_post, m_norm_mlp_pre, m_norm_mlp_post, m_w_in_ab, m_w_spatial, m_b_spatial, m_conv_w, m_w_out_ab, m_w_qkv, m_w_o, m_w_up, m_w_down, v_norm_mix_pre, v_norm_mix_post, v_norm_mlp_pre, v_norm_mlp_post, v_w_in_ab, v_w_spatial, v_b_spatial, v_conv_w, v_w_out_ab, v_w_qkv, v_w_o, v_w_up, v_w_down):
    given = dict(x=x, norm_mix_pre=norm_mix_pre, norm_mix_post=norm_mix_post, norm_mlp_pre=norm_mlp_pre, norm_mlp_post=norm_mlp_post, w_in_ab=w_in_ab, w_spatial=w_spatial, b_spatial=b_spatial, conv_w=conv_w, w_out_ab=w_out_ab, w_qkv=w_qkv, w_o=w_o, w_up=w_up, w_down=w_down, loss_target=loss_target, m_norm_mix_pre=m_norm_mix_pre, m_norm_mix_post=m_norm_mix_post, m_norm_mlp_pre=m_norm_mlp_pre, m_norm_mlp_post=m_norm_mlp_post, m_w_in_ab=m_w_in_ab, m_w_spatial=m_w_spatial, m_b_spatial=m_b_spatial, m_conv_w=m_conv_w, m_w_out_ab=m_w_out_ab, m_w_qkv=m_w_qkv, m_w_o=m_w_o, m_w_up=m_w_up, m_w_down=m_w_down, v_norm_mix_pre=v_norm_mix_pre, v_norm_mix_post=v_norm_mix_post, v_norm_mlp_pre=v_norm_mlp_pre, v_norm_mlp_post=v_norm_mlp_post, v_w_in_ab=v_w_in_ab, v_w_spatial=v_w_spatial, v_b_spatial=v_b_spatial, v_conv_w=v_conv_w, v_w_out_ab=v_w_out_ab, v_w_qkv=v_w_qkv, v_w_o=v_w_o, v_w_up=v_w_up, v_w_down=v_w_down)
    weights = {n: given[n] for n in TWIN_WEIGHTS}
    shared = {n: given[n] for n in SHARED_INPUTS}
    per_example = {n: given[n] for n in ['x']}
    grad_fn = _jax.value_and_grad(_loss, argnums=(0, 1))

    def one_microbatch(ex, loss_target):
        ex = dict(ex)
        diff = ex.pop(TWIN_DIFF_INPUT)
        return grad_fn(weights, diff, {**shared, **ex}, loss_target)

    if N_MICROBATCH == 1:
        loss, (grad_w, grad_x) = one_microbatch(per_example, given["loss_target"])
    else:
        def body(carry, xs):
            loss_sum, grad_sum = carry
            l_k, (gw_k, gx_k) = one_microbatch(xs[0], xs[1])
            with _jax.named_scope("update"):
                return (loss_sum + l_k, _jax.tree.map(_jnp.add, grad_sum, gw_k)), gx_k

        init = (_jnp.zeros((), _jnp.float32), _jax.tree.map(_jnp.zeros_like, weights))
        (loss, grad_w), grad_x = _jax.lax.scan(body, init, (per_example, given["loss_target"]))
    with _jax.named_scope("update"):
        delta_w, new_m, new_v = {}, {}, {}
        for n in TWIN_WEIGHTS:
            delta_w[n], new_m[n], new_v[n] = _adamw(weights[n], grad_w[n], given["m_" + n], given["v_" + n])
    return (loss, grad_x, *[grad_w[n] for n in TWIN_WEIGHTS], *[delta_w[n] for n in TWIN_WEIGHTS],
            *[new_m[n] for n in TWIN_WEIGHTS], *[new_v[n] for n in TWIN_WEIGHTS])
```

```python
import functools

import jax
import jax.numpy as jnp
from jax import lax
from jax.experimental import pallas as pl
from jax.experimental.pallas import tpu as pltpu

F32 = jnp.float32
BF16 = jnp.bfloat16
MESH = pl.DeviceIdType.MESH
ANY = pl.BlockSpec(memory_space=pl.ANY)

NDEV = 8
NCHIP = 4
RMS_EPS = 1e-6
LN_EPS = 1e-5
CHUNK = 128
HEAD = 128
ATT_TILE = 2048
DILATIONS = (1, 4, 16)
ROPE_THETA = 10000.0
CONV_TAPS = 3
HALO = 8
GELU_C = 0.7978845608028654
GELU_A = 0.044715
ADAM_LR, ADAM_B1, ADAM_B2, ADAM_EPS, ADAM_WD, ADAM_STEP = 0.001, 0.9, 0.999, 1e-08, 0.01, 10
VMEM_MM = 52 << 20
VMEM_EW = 40 << 20


def _cp(sem=None, vmem=VMEM_EW):
    if sem is None:
        return pltpu.CompilerParams(vmem_limit_bytes=vmem)
    return pltpu.CompilerParams(dimension_semantics=sem, vmem_limit_bytes=vmem)


def _tile(n, want):
    return want if n % want == 0 else n


def _all_gather(name, shards):
    n = len(shards)

    def body(*refs):
        ins, outs = refs[:n], refs[n:2 * n]
        send_sems, recv_sems, local_sems = refs[2 * n:]
        x, y, c = lax.axis_index("x"), lax.axis_index("y"), lax.axis_index("c")
        me, sibling = (x, y, c), (x, y, 1 - c)
        chips = [(1 - x, y), (x, 1 - y), (1 - x, 1 - y)]

        def slot(p):
            return 4 * p[0] + 2 * p[1] + p[2]

        def copy(i, k, block, to, src=None):
            dst = outs[i].at[slot(block)]
            return pltpu.make_async_remote_copy(
                src_ref=dst if src is None else src, dst_ref=dst,
                send_sem=send_sems.at[i, k], recv_sem=recv_sems.at[i, k],
                device_id=to, device_id_type=MESH)

        mine = [pltpu.make_async_copy(ins[i], outs[i].at[slot(me)], local_sems.at[i]) for i in range(n)]
        for cp in mine:
            cp.start()
        first = []
        for i in range(n):
            first.append(copy(i, 0, me, sibling, src=ins[i]))
            for j, chip in enumerate(chips):
                first.append(copy(i, 1 + j, me, (*chip, c), src=ins[i]))
        for cp in first:
            cp.start()
        passed = []
        for j, chip in enumerate(chips):
            for i in range(n):
                copy(i, 1 + j, (*chip, c), me).wait_recv()
                fwd = copy(i, 4 + j, (*chip, c), sibling)
                fwd.start()
                passed.append(fwd)
        for i in range(n):
            copy(i, 0, sibling, me).wait_recv()
            for j, chip in enumerate(chips):
                copy(i, 4 + j, (*chip, 1 - c), me).wait_recv()
        for cp in first + passed:
            cp.wait_send()
        for cp in mine:
            cp.wait()

    return pl.pallas_call(
        body, name=name,
        out_shape=[jax.ShapeDtypeStruct((NDEV,) + s.shape, s.dtype) for s in shards],
        in_specs=[ANY] * n, out_specs=[ANY] * n,
        scratch_shapes=[pltpu.SemaphoreType.DMA((n, 7)), pltpu.SemaphoreType.DMA((n, 7)),
                        pltpu.SemaphoreType.DMA((n,))],
    )(*shards)


def _rs_swap(name, grads):
    n = len(grads)

    def body(*refs):
        ins, outs = refs[:n], refs[n:2 * n]
        send_sems, recv_sems = refs[2 * n:]
        x, y, c = lax.axis_index("x"), lax.axis_index("y"), lax.axis_index("c")
        copies = []
        for i in range(n):
            for k in range(NCHIP):
                copies.append(pltpu.make_async_remote_copy(
                    src_ref=ins[i].at[2 * k + (1 - c)], dst_ref=outs[i].at[k],
                    send_sem=send_sems.at[i, k], recv_sem=recv_sems.at[i, k],
                    device_id=(x, y, 1 - c), device_id_type=MESH))
        for cp in copies:
            cp.start()
        for cp in copies:
            cp.wait()

    return pl.pallas_call(
        body, name=name,
        out_shape=[jax.ShapeDtypeStruct((NCHIP,) + g.shape[1:], g.dtype) for g in grads],
        in_specs=[ANY] * n, out_specs=[ANY] * n,
        scratch_shapes=[pltpu.SemaphoreType.DMA((n, NCHIP)), pltpu.SemaphoreType.DMA((n, NCHIP))],
    )(*grads)


def _rs_scatter(name, pairs):
    n = len(pairs)

    def body(*refs):
        ins, outs = refs[:n], refs[n:2 * n]
        send_sems, recv_sems, local_sems = refs[2 * n:]
        x, y, c = lax.axis_index("x"), lax.axis_index("y"), lax.axis_index("c")
        my_chip = 2 * x + y
        chips = [(1 - x, y), (x, 1 - y), (1 - x, 1 - y)]
        mine = [pltpu.make_async_copy(ins[i].at[my_chip], outs[i].at[my_chip], local_sems.at[i]) for i in range(n)]
        for cp in mine:
            cp.start()
        copies = []
        for i in range(n):
            for j, chip in enumerate(chips):
                copies.append(pltpu.make_async_remote_copy(
                    src_ref=ins[i].at[2 * chip[0] + chip[1]], dst_ref=outs[i].at[my_chip],
                    send_sem=send_sems.at[i, j], recv_sem=recv_sems.at[i, j],
                    device_id=(*chip, c), device_id_type=MESH))
        for cp in copies:
            cp.start()
        for cp in copies:
            cp.wait()
        for cp in mine:
            cp.wait()

    return pl.pallas_call(
        body, name=name,
        out_shape=[jax.ShapeDtypeStruct(p.shape, p.dtype) for p in pairs],
        in_specs=[ANY] * n, out_specs=[ANY] * n,
        scratch_shapes=[pltpu.SemaphoreType.DMA((n, 3)), pltpu.SemaphoreType.DMA((n, 3)),
                        pltpu.SemaphoreType.DMA((n,))],
    )(*pairs)


def _pair_sum(name, grad, got, core):
    _, rows, cols = grad.shape
    tr = _tile(rows, 256)

    def body(core_ref, g_ref, r_ref, o_ref):
        o_ref[...] = (g_ref[...].astype(F32) + r_ref[...].astype(F32)).astype(o_ref.dtype)

    return pl.pallas_call(
        body, name=name,
        grid_spec=pltpu.PrefetchScalarGridSpec(
            num_scalar_prefetch=1, grid=(NCHIP, rows // tr),
            in_specs=[pl.BlockSpec((None, tr, cols), lambda k, i, core_ref: (2 * k + core_ref[0], i, 0)),
                      pl.BlockSpec((None, tr, cols), lambda k, i, core_ref: (k, i, 0))],
            out_specs=pl.BlockSpec((None, tr, cols), lambda k, i, core_ref: (k, i, 0))),
        out_shape=jax.ShapeDtypeStruct((NCHIP, rows, cols), grad.dtype),
        compiler_params=_cp(("parallel", "parallel")),
    )(core, grad, got)


def _adam_math(w, g, m, v):
    m = ADAM_B1 * m + (1.0 - ADAM_B1) * g
    v = ADAM_B2 * v + (1.0 - ADAM_B2) * (g * g)
    m_hat = m / (1.0 - ADAM_B1 ** ADAM_STEP)
    v_hat = v / (1.0 - ADAM_B2 ** ADAM_STEP)
    delta = -ADAM_LR * (m_hat / (jnp.sqrt(v_hat) + ADAM_EPS) + ADAM_WD * w)
    return delta, m, v


def _adamw(name, parts, w, m, v):
    nparts, rows, cols = parts.shape
    tr = _tile(rows, 256)

    def body(p_ref, w_ref, m_ref, v_ref, g_out, d_out, m_out, v_out):
        g = p_ref[0].astype(F32)
        for k in range(1, nparts):
            g = g + p_ref[k].astype(F32)
        delta, mn, vn = _adam_math(w_ref[...], g, m_ref[...], v_ref[...])
        g_out[...] = g
        d_out[...] = delta
        m_out[...] = mn
        v_out[...] = vn

    row = pl.BlockSpec((tr, cols), lambda i: (i, 0))
    return pl.pallas_call(
        body, name=name, grid=(rows // tr,),
        in_specs=[pl.BlockSpec((nparts, tr, cols), lambda i: (0, i, 0)), row, row, row],
        out_specs=[row] * 4,
        out_shape=[jax.ShapeDtypeStruct((rows, cols), F32)] * 4,
        compiler_params=_cp(("parallel",)),
    )(parts, w, m, v)


def _norm_fwd(name, h, z=None, g_post=None, g_pre=None):
    rows, d = h.shape
    tm = _tile(rows, 256)
    has_post, has_pre = z is not None, g_pre is not None

    def body(*refs):
        it = iter(refs)
        hv = next(it)[...]
        if has_post:
            zv, gp = next(it)[...], next(it)[...]
        if has_pre:
            gq = next(it)[...]
        if has_post:
            r = lax.rsqrt(jnp.mean(zv * zv, axis=-1, keepdims=True) + RMS_EPS)
            hv = hv + (zv * r) * gp
            next(it)[...] = hv
        if has_pre:
            r = lax.rsqrt(jnp.mean(hv * hv, axis=-1, keepdims=True) + RMS_EPS)
            next(it)[...] = ((hv * r) * gq).astype(BF16)

    row = pl.BlockSpec((tm, d), lambda i: (i, 0))
    vec = pl.BlockSpec((1, d), lambda i: (0, 0))
    ins, in_specs, out_shape, out_specs = [h], [row], [], []
    if has_post:
        ins += [z, g_post]
        in_specs += [row, vec]
        out_shape.append(jax.ShapeDtypeStruct((rows, d), F32))
        out_specs.append(row)
    if has_pre:
        ins.append(g_pre)
        in_specs.append(vec)
        out_shape.append(jax.ShapeDtypeStruct((rows, d), BF16))
        out_specs.append(row)
    return pl.pallas_call(body, name=name, grid=(rows // tm,), in_specs=in_specs, out_specs=out_specs,
                          out_shape=out_shape, compiler_params=_cp(("parallel",)))(*ins)


def _rms_bwd_rows(x, g, dy):
    r = lax.rsqrt(jnp.mean(x * x, axis=-1, keepdims=True) + RMS_EPS)
    xn = x * r
    dg = jnp.sum(dy * xn, axis=0, keepdims=True)
    dxn = dy * g
    dx = r * (dxn - xn * jnp.mean(dxn * xn, axis=-1, keepdims=True))
    return dx, dg


def _norm_bwd(name, d_out, pre=None, post=None):
    rows, d = d_out.shape
    tm = _tile(rows, 256)
    has_pre, has_post = pre is not None, post is not None

    def body(*refs):
        it = iter(refs)
        dres = next(it)[...]
        if has_pre:
            dy, xp, gq = next(it)[...], next(it)[...], next(it)[...]
        if has_post:
            zv, gp = next(it)[...], next(it)[...]
        first = pl.program_id(0) == 0
        if has_pre:
            dx, dg = _rms_bwd_rows(xp, gq, dy)
            dres = dres + dx
            next(it)[...] = dres
            dg_ref = next(it)

            @pl.when(first)
            def _():
                dg_ref[...] = jnp.zeros_like(dg_ref)
            dg_ref[...] += dg
        if has_post:
            dz, dg2 = _rms_bwd_rows(zv, gp, dres)
            next(it)[...] = dz.astype(BF16)
            dg2_ref = next(it)

            @pl.when(first)
            def _():
                dg2_ref[...] = jnp.zeros_like(dg2_ref)
            dg2_ref[...] += dg2

    row = pl.BlockSpec((tm, d), lambda i: (i, 0))
    vec = pl.BlockSpec((1, d), lambda i: (0, 0))
    ins, in_specs, out_shape, out_specs = [d_out], [row], [], []
    if has_pre:
        ins += list(pre)
        in_specs += [row, row, vec]
        out_shape += [jax.ShapeDtypeStruct((rows, d), F32), jax.ShapeDtypeStruct((1, d), F32)]
        out_specs += [row, vec]
    if has_post:
        ins += list(post)
        in_specs += [row, vec]
        out_shape += [jax.ShapeDtypeStruct((rows, d), BF16), jax.ShapeDtypeStruct((1, d), F32)]
        out_specs += [row, vec]
    return pl.pallas_call(body, name=name, grid=(rows // tm,), in_specs=in_specs, out_specs=out_specs,
                          out_shape=out_shape, compiler_params=_cp(("arbitrary",)))(*ins)


def _loss_grad(name, y, target):
    rows, d = y.shape
    tm = _tile(rows, 256)

    def body(y_ref, t_ref, dy_ref, loss_ref):
        err = y_ref[...] - t_ref[...]
        dy_ref[...] = err * (1.0 / d)

        @pl.when(pl.program_id(0) == 0)
        def _():
            loss_ref[...] = jnp.zeros_like(loss_ref)
        loss_ref[...] += jnp.full(loss_ref.shape, (0.5 / d) * jnp.sum(err * err), F32)

    row = pl.BlockSpec((tm, d), lambda i: (i, 0))
    return pl.pallas_call(
        body, name=name, grid=(rows // tm,), in_specs=[row, row],
        out_specs=[row, pl.BlockSpec((1, 128), lambda i: (0, 0))],
        out_shape=[jax.ShapeDtypeStruct((rows, d), F32), jax.ShapeDtypeStruct((1, 128), F32)],
        compiler_params=_cp(("arbitrary",)))(y, target)


NT_DIMS = (((1,), (1,)), ((), ()))
TN_DIMS = (((0,), (0,)), ((), ()))


def _mm_nn_blk(name, a, wblk, relu2=False):
    m, k = a.shape
    nb = wblk.shape[2]
    tm = _tile(m, 512)

    def body(a_ref, w_ref, *outs):
        r = jnp.dot(a_ref[...], w_ref[...], preferred_element_type=F32)
        outs[0][...] = r
        if relu2:
            rr = jnp.maximum(r, 0.0)
            outs[1][...] = (rr * rr).astype(BF16)

    out_shape = [jax.ShapeDtypeStruct((m, NDEV * nb), F32)]
    if relu2:
        out_shape.append(jax.ShapeDtypeStruct((m, NDEV * nb), BF16))
    return pl.pallas_call(
        body, name=name, grid=(NDEV, m // tm),
        in_specs=[pl.BlockSpec((tm, k), lambda d, i: (i, 0)), pl.BlockSpec((None, k, nb), lambda d, i: (d, 0, 0))],
        out_specs=[pl.BlockSpec((tm, nb), lambda d, i: (i, d))] * len(out_shape),
        out_shape=out_shape, compiler_params=_cp(("parallel", "parallel"), VMEM_MM))(a, wblk)


def _mm_nn(name, a, w):
    m, kb = a.shape
    n = w.shape[1]
    tm, tk = _tile(m, 512), _tile(kb, 2048)

    def body(a_ref, w_ref, o_ref):
        r = jnp.dot(a_ref[...], w_ref[...], preferred_element_type=F32)
        step = pl.program_id(1)

        @pl.when(step == 0)
        def _():
            o_ref[...] = r

        @pl.when(step > 0)
        def _():
            o_ref[...] += r

    return pl.pallas_call(
        body, name=name, grid=(m // tm, kb // tk),
        in_specs=[pl.BlockSpec((tm, tk), lambda i, s: (i, s)), pl.BlockSpec((tk, n), lambda i, s: (s, 0))],
        out_specs=pl.BlockSpec((tm, n), lambda i, s: (i, 0)),
        out_shape=jax.ShapeDtypeStruct((m, n), F32),
        compiler_params=_cp(("parallel", "arbitrary"), VMEM_MM))(a, w)


def _mm_nt_rows(name, dy, w, up=None):
    m, n = dy.shape
    kw = w.shape[0]
    tm, tkw = _tile(m, 512), _tile(kw, 1024)

    def body(dy_ref, w_ref, *rest):
        r = lax.dot_general(dy_ref[...], w_ref[...], NT_DIMS, preferred_element_type=F32)
        if up is None:
            rest[0][...] = r
        else:
            rest[1][...] = (r * (2.0 * jnp.maximum(rest[0][...], 0.0))).astype(BF16)

    ins = [dy, w]
    in_specs = [pl.BlockSpec((tm, n), lambda j, i: (i, 0)), pl.BlockSpec((tkw, n), lambda j, i: (j, 0))]
    if up is not None:
        ins.append(up)
        in_specs.append(pl.BlockSpec((tm, tkw), lambda j, i: (i, j)))
    return pl.pallas_call(
        body, name=name, grid=(kw // tkw, m // tm), in_specs=in_specs,
        out_specs=pl.BlockSpec((tm, tkw), lambda j, i: (i, j)),
        out_shape=jax.ShapeDtypeStruct((m, kw), F32 if up is None else BF16),
        compiler_params=_cp(("parallel", "parallel"), VMEM_MM))(*ins)


def _mm_nt_blk(name, dy, wblk):
    m = dy.shape[0]
    _, kw, nb = wblk.shape
    tm, per = _tile(m, 512), 2

    def body(dy_ref, w_ref, o_ref):
        r = lax.dot_general(dy_ref[:, :nb], w_ref[0], NT_DIMS, preferred_element_type=F32)
        for t in range(1, per):
            r = r + lax.dot_general(dy_ref[:, t * nb:(t + 1) * nb], w_ref[t], NT_DIMS, preferred_element_type=F32)
        step = pl.program_id(1)

        @pl.when(step == 0)
        def _():
            o_ref[...] = r

        @pl.when(step > 0)
        def _():
            o_ref[...] += r

    return pl.pallas_call(
        body, name=name, grid=(m // tm, NDEV // per),
        in_specs=[pl.BlockSpec((tm, per * nb), lambda i, s: (i, s)),
                  pl.BlockSpec((per, kw, nb), lambda i, s: (s, 0, 0))],
        out_specs=pl.BlockSpec((tm, kw), lambda i, s: (i, 0)),
        out_shape=jax.ShapeDtypeStruct((m, kw), F32),
        compiler_params=_cp(("parallel", "arbitrary"), VMEM_MM))(dy, wblk)


def _mm_tn(name, x, dy, nb=None):
    t, mx = x.shape
    n = dy.shape[1]
    tmx = _tile(mx, 512)
    tn = nb if nb is not None else _tile(n, 1024)

    def body(x_ref, dy_ref, o_ref):
        o_ref[...] = lax.dot_general(x_ref[...], dy_ref[...], TN_DIMS, preferred_element_type=F32).astype(BF16)

    if nb is None:
        out_shape = jax.ShapeDtypeStruct((mx, n), BF16)
        out_spec = pl.BlockSpec((tmx, tn), lambda j, i: (i, j))
    else:
        out_shape = jax.ShapeDtypeStruct((NDEV, mx, nb), BF16)
        out_spec = pl.BlockSpec((None, tmx, nb), lambda j, i: (j, i, 0))
    return pl.pallas_call(
        body, name=name, grid=(n // tn, mx // tmx),
        in_specs=[pl.BlockSpec((t, tmx), lambda j, i: (0, i)), pl.BlockSpec((t, tn), lambda j, i: (0, j))],
        out_specs=out_spec, out_shape=out_shape,
        compiler_params=_cp(("parallel", "parallel"), VMEM_MM))(x, dy)


def _gelu(x):
    return 0.5 * x * (1.0 + jnp.tanh(GELU_C * (x + GELU_A * (x * x * x))))


def _gelu_grad(x):
    t = jnp.tanh(GELU_C * (x + GELU_A * (x * x * x)))
    return 0.5 * (1.0 + t) + 0.5 * x * (1.0 - t * t) * (GELU_C * (1.0 + 3.0 * GELU_A * (x * x)))


def _layernorm(a):
    mu = jnp.mean(a, axis=-1, keepdims=True)
    ac = a - mu
    rstd = lax.rsqrt(jnp.mean(ac * ac, axis=-1, keepdims=True) + LN_EPS)
    return ac * rstd, rstd


def _shift_rows(z, halo, k):
    zr = pltpu.roll(z, k, 0)
    hr = pltpu.roll(halo, k, 0)
    row = lax.broadcasted_iota(jnp.int32, hr.shape, 0)
    top = jnp.where(row < k, hr, zr[:HALO])
    return jnp.concatenate([top, zr[HALO:]], axis=0)


def _shift_rows_up(z, halo, k):
    rows = z.shape[0]
    zr = pltpu.roll(z, rows - k, 0)
    hr = pltpu.roll(halo, HALO - k, 0)
    row = lax.broadcasted_iota(jnp.int32, hr.shape, 0)
    bot = jnp.where(row >= HALO - k, hr, zr[rows - HALO:])
    return jnp.concatenate([zr[:rows - HALO], bot], axis=0)


def _causal_mask():
    t = lax.broadcasted_iota(jnp.int32, (CHUNK, CHUNK), 0)
    s = lax.broadcasted_iota(jnp.int32, (CHUNK, CHUNK), 1)
    return s <= t


def _gate_specs(tm, rows, width):
    per = tm // HALO
    last = rows // HALO - 1
    cur = pl.BlockSpec((tm, width), lambda i: (i, 0))
    prev = pl.BlockSpec((HALO, width), lambda i: (jnp.maximum(i * per - 1, 0), 0))
    nxt = pl.BlockSpec((HALO, width), lambda i: (jnp.minimum((i + 1) * per, last), 0))
    return cur, prev, nxt


def _gate_fwd(name, proj, w_s, b_st, cw):
    rows, width = proj.shape
    w = width // 5
    groups = w // CHUNK
    tm = _tile(rows, 256)
    cur, prev, _ = _gate_specs(tm, rows, width)

    def body(p_ref, h_ref, ws_ref, b_ref, cw_ref, o_ref):
        mask = _causal_mask()
        au = _gelu(p_ref[:, 0:w])
        vn, _ = _layernorm(_gelu(p_ref[:, w:2 * w]))
        vn = vn.astype(BF16)
        for g in range(groups):
            wc = jnp.where(mask, ws_ref[g], 0.0).astype(BF16)
            cols = slice(g * CHUNK, (g + 1) * CHUNK)
            for ch in range(tm // CHUNK):
                rws = slice(ch * CHUNK, (ch + 1) * CHUNK)
                mixed = jnp.dot(wc, vn[rws, cols], preferred_element_type=F32) + b_ref[:, g:g + 1]
                o_ref[rws, cols] = (au[rws, cols] * mixed).astype(BF16)
        z = p_ref[:, 3 * w:4 * w] * p_ref[:, 4 * w:5 * w]
        zh = h_ref[:, 3 * w:4 * w] * h_ref[:, 4 * w:5 * w]
        zh = jnp.where(pl.program_id(0) == 0, 0.0, zh)
        y = cw_ref[0:1, :] * _shift_rows(z, zh, 2) + cw_ref[1:2, :] * _shift_rows(z, zh, 1) + cw_ref[2:3, :] * z
        o_ref[:, w:2 * w] = (p_ref[:, 2 * w:3 * w] * y).astype(BF16)

    full = lambda a: pl.BlockSpec(a.shape, lambda i: (0,) * a.ndim)
    return pl.pallas_call(
        body, name=name, grid=(rows // tm,),
        in_specs=[cur, prev, full(w_s), full(b_st), full(cw)],
        out_specs=pl.BlockSpec((tm, 2 * w), lambda i: (i, 0)),
        out_shape=jax.ShapeDtypeStruct((rows, 2 * w), BF16),
        compiler_params=_cp(("parallel",)))(proj, proj, w_s, b_st, cw)


def _gate_bwd(name, proj, d_ab, w_s, b_st, cw):
    rows, width = proj.shape
    w = width // 5
    groups = w // CHUNK
    tm = _tile(rows, 256)
    cur, prev, nxt = _gate_specs(tm, rows, width)
    dcur, _, dnxt = _gate_specs(tm, rows, 2 * w)

    def body(p_ref, ph_ref, pn_ref, d_ref, dn_ref, ws_ref, b_ref, cw_ref, o_ref, dws_ref, dbs_ref, dcw_ref):
        i = pl.program_id(0)

        @pl.when(i == 0)
        def _():
            dws_ref[...] = jnp.zeros_like(dws_ref)
            dbs_ref[...] = jnp.zeros_like(dbs_ref)
            dcw_ref[...] = jnp.zeros_like(dcw_ref)

        mask = _causal_mask()
        u, v = p_ref[:, 0:w], p_ref[:, w:2 * w]
        au, av = _gelu(u), _gelu(v)
        vn, rstd = _layernorm(av)
        vnb = vn.astype(BF16)
        d_a = d_ref[:, 0:w]
        d_mixed = (d_a * au).astype(BF16)
        ones = jnp.ones((HALO, CHUNK), BF16)
        d_vn_cols = []
        d_au_cols = []
        for g in range(groups):
            wc = jnp.where(mask, ws_ref[g], 0.0).astype(BF16)
            cols = slice(g * CHUNK, (g + 1) * CHUNK)
            dw = jnp.zeros((CHUNK, CHUNK), F32)
            db = jnp.zeros((HALO, CHUNK), F32)
            d_vn_rows, d_au_rows = [], []
            for ch in range(tm // CHUNK):
                rws = slice(ch * CHUNK, (ch + 1) * CHUNK)
                mixed = jnp.dot(wc, vnb[rws, cols], preferred_element_type=F32) + b_ref[:, g:g + 1]
                d_au_rows.append(d_a[rws, cols] * mixed)
                dm = d_mixed[rws, cols]
                dw = dw + lax.dot_general(dm, vnb[rws, cols], NT_DIMS, preferred_element_type=F32)
                db = db + lax.dot_general(ones, dm, NT_DIMS, preferred_element_type=F32)
                d_vn_rows.append(lax.dot_general(wc, dm, TN_DIMS, preferred_element_type=F32))
            dws_ref[g] += jnp.where(mask, dw, 0.0)
            dbs_ref[g:g + 1, :] += db[0:1, :]
            d_vn_cols.append(jnp.concatenate(d_vn_rows, axis=0))
            d_au_cols.append(jnp.concatenate(d_au_rows, axis=0))
        d_vn = jnp.concatenate(d_vn_cols, axis=1)
        d_au = jnp.concatenate(d_au_cols, axis=1)
        d_av = rstd * (d_vn - jnp.mean(d_vn, axis=-1, keepdims=True)
                       - vn * jnp.mean(d_vn * vn, axis=-1, keepdims=True))
        o_ref[:, 0:w] = (d_au * _gelu_grad(u)).astype(BF16)
        o_ref[:, w:2 * w] = (d_av * _gelu_grad(v)).astype(BF16)

        gb, gc, bx = p_ref[:, 2 * w:3 * w], p_ref[:, 3 * w:4 * w], p_ref[:, 4 * w:5 * w]
        z = gc * bx
        zh = jnp.where(i == 0, 0.0, ph_ref[:, 3 * w:4 * w] * ph_ref[:, 4 * w:5 * w])
        z1, z2 = _shift_rows(z, zh, 1), _shift_rows(z, zh, 2)
        d_b = d_ref[:, w:2 * w]
        y = cw_ref[0:1, :] * z2 + cw_ref[1:2, :] * z1 + cw_ref[2:3, :] * z
        dy = d_b * gb
        dyn = jnp.where(i == pl.num_programs(0) - 1, 0.0, dn_ref[:, w:2 * w] * pn_ref[:, 2 * w:3 * w])
        dz = (cw_ref[2:3, :] * dy + cw_ref[1:2, :] * _shift_rows_up(dy, dyn, 1)
              + cw_ref[0:1, :] * _shift_rows_up(dy, dyn, 2))
        dcw_ref[0:1, :] += jnp.sum(dy * z2, axis=0, keepdims=True)
        dcw_ref[1:2, :] += jnp.sum(dy * z1, axis=0, keepdims=True)
        dcw_ref[2:3, :] += jnp.sum(dy * z, axis=0, keepdims=True)
        o_ref[:, 2 * w:3 * w] = (d_b * y).astype(BF16)
        o_ref[:, 3 * w:4 * w] = (dz * bx).astype(BF16)
        o_ref[:, 4 * w:5 * w] = (dz * gc).astype(BF16)

    full = lambda a: pl.BlockSpec(a.shape, lambda i: (0,) * a.ndim)
    acc = lambda shape: pl.BlockSpec(shape, lambda i: (0,) * len(shape))
    return pl.pallas_call(
        body, name=name, grid=(rows // tm,),
        in_specs=[cur, prev, nxt, dcur, dnxt, full(w_s), full(b_st), full(cw)],
        out_specs=[pl.BlockSpec((tm, width), lambda i: (i, 0)), acc((groups, CHUNK, CHUNK)),
                   acc((groups, CHUNK)), acc((HALO, w))],
        out_shape=[jax.ShapeDtypeStruct((rows, width), BF16), jax.ShapeDtypeStruct((groups, CHUNK, CHUNK), F32),
                   jax.ShapeDtypeStruct((groups, CHUNK), F32), jax.ShapeDtypeStruct((HALO, w), F32)],
        compiler_params=_cp(("arbitrary",), VMEM_MM))(proj, proj, proj, d_ab, d_ab, w_s, b_st, cw)


def _rope(t, cosf, sins):
    return t * cosf + pltpu.roll(t, HEAD // 2, 1) * sins


def _rope_bwd(dt, cosf, sins):
    return dt * cosf + pltpu.roll(dt * sins, HEAD // 2, 1)


def _attn_units(n, visit):
    for b, d in enumerate(DILATIONS):
        blocks = ATT_TILE // (CHUNK * d)
        for r in range(d):
            if blocks == 1:
                visit(b, d, r, 0)
            else:
                def step(j, carry, b=b, d=d, r=r):
                    visit(b, d, r, j)
                    return carry
                lax.fori_loop(0, blocks, step, 0)


def _unit_rows(ref, d, r, j, nblk, offset=0):
    base = offset + j * (CHUNK * d)
    if not isinstance(base, int):
        base = pl.multiple_of(base, CHUNK)
    return ref.at[pl.ds(base, nblk * CHUNK * d)], pl.ds(r, nblk * CHUNK, stride=d)


def _band(n, j):
    qi = lax.broadcasted_iota(jnp.int32, (CHUNK, 2 * CHUNK), 0)
    ki = lax.broadcasted_iota(jnp.int32, (CHUNK, 2 * CHUNK), 1)
    no_prev = jnp.logical_and(n == 0, j == 0)
    lo = jnp.where(no_prev, CHUNK, 0)
    return (ki >= qi) & (ki <= qi + CHUNK) & (ki >= lo)


def _attn_in_specs(heads):
    blk = (ATT_TILE, HEAD)
    prev = lambda n: jnp.maximum(n - 1, 0)
    return [
        pl.BlockSpec(blk, lambda h, n: (n, h)),
        pl.BlockSpec(blk, lambda h, n: (n, heads + h)),
        pl.BlockSpec(blk, lambda h, n: (prev(n), heads + h)),
        pl.BlockSpec(blk, lambda h, n: (n, 2 * heads + h)),
        pl.BlockSpec(blk, lambda h, n: (prev(n), 2 * heads + h)),
        pl.BlockSpec(blk, lambda h, n: (n, 0)),
        pl.BlockSpec(blk, lambda h, n: (n, 0)),
        pl.BlockSpec(blk, lambda h, n: (prev(n), 0)),
        pl.BlockSpec(blk, lambda h, n: (prev(n), 0)),
    ]


def _attn_load(q_ref, kc_ref, kp_ref, vc_ref, vp_ref, cc_ref, sc_ref, cp_ref, sp_ref, qr, kcat, vcat):
    qr[...] = _rope(q_ref[...], cc_ref[...], sc_ref[...])
    kcat[pl.ds(0, ATT_TILE), :] = _rope(kp_ref[...], cp_ref[...], sp_ref[...])
    kcat[pl.ds(ATT_TILE, ATT_TILE), :] = _rope(kc_ref[...], cc_ref[...], sc_ref[...])
    vcat[pl.ds(0, ATT_TILE), :] = vp_ref[...]
    vcat[pl.ds(ATT_TILE, ATT_TILE), :] = vc_ref[...]


def _attn_fwd(name, qkv, cosf, sins):
    t = qkv.shape[0]
    heads = qkv.shape[1] // (3 * HEAD)
    scale = HEAD ** -0.5
    nbr = len(DILATIONS)

    def body(q_ref, kc_ref, kp_ref, vc_ref, vp_ref, cc_ref, sc_ref, cp_ref, sp_ref, o_ref, lse_ref,
             qr, kcat, vcat, obr, lbr):
        n = pl.program_id(1)
        _attn_load(q_ref, kc_ref, kp_ref, vc_ref, vp_ref, cc_ref, sc_ref, cp_ref, sp_ref, qr, kcat, vcat)

        def visit(b, d, r, j):
            qv, qs = _unit_rows(qr, d, r, j, 1)
            kv, ks = _unit_rows(kcat, d, r, j, 2, ATT_TILE - CHUNK * d)
            vv, _ = _unit_rows(vcat, d, r, j, 2, ATT_TILE - CHUNK * d)
            s = lax.dot_general(qv[qs, :].astype(BF16), kv[ks, :].astype(BF16), NT_DIMS,
                                preferred_element_type=F32) * scale
            s = jnp.where(_band(n, j), s, -jnp.inf)
            mx = jnp.max(s, axis=-1, keepdims=True)
            p = jnp.exp(s - mx)
            den = jnp.sum(p, axis=-1, keepdims=True)
            o = jnp.dot((p / den).astype(BF16), vv[ks, :].astype(BF16), preferred_element_type=F32)
            ov, _ = _unit_rows(obr.at[b], d, r, j, 1)
            lv, _ = _unit_rows(lbr.at[b], d, r, j, 1)
            ov[qs, :] = o
            lv[qs, :] = jnp.broadcast_to(mx + jnp.log(den), (CHUNK, HEAD))

        _attn_units(n, visit)
        ls = [lbr[b] for b in range(nbr)]
        top = functools.reduce(jnp.maximum, ls)
        ws = [jnp.exp(l - top) for l in ls]
        tot = functools.reduce(jnp.add, ws)
        o = (ws[0] / tot) * obr[0]
        for b in range(1, nbr):
            o = o + (ws[b] / tot) * obr[b]
        o_ref[...] = o.astype(BF16)
        lse_ref[...] = top + jnp.log(tot)

    blk = (ATT_TILE, HEAD)
    tile = pl.BlockSpec(blk, lambda h, n: (n, h))
    return pl.pallas_call(
        body, name=name, grid=(heads, t // ATT_TILE), in_specs=_attn_in_specs(heads),
        out_specs=[tile, tile],
        out_shape=[jax.ShapeDtypeStruct((t, heads * HEAD), BF16), jax.ShapeDtypeStruct((t, heads * HEAD), F32)],
        scratch_shapes=[pltpu.VMEM(blk, F32), pltpu.VMEM((2 * ATT_TILE, HEAD), F32), pltpu.VMEM((2 * ATT_TILE, HEAD), F32),
                        pltpu.VMEM((nbr,) + blk, F32), pltpu.VMEM((nbr,) + blk, F32)],
        compiler_params=_cp(("parallel", "parallel"), VMEM_MM),
    )(qkv, qkv, qkv, qkv, qkv, cosf, sins, cosf, sins)


def _attn_bwd(name, qkv, cosf, sins, d_o, o, lse):
    t = qkv.shape[0]
    heads = qkv.shape[1] // (3 * HEAD)
    scale = HEAD ** -0.5

    def body(q_ref, kc_ref, kp_ref, vc_ref, vp_ref, cc_ref, sc_ref, cp_ref, sp_ref, do_ref, o_ref, lse_ref,
             dq_ref, dko_ref, dkp_ref, dvo_ref, dvp_ref, qr, kcat, vcat, dq_acc, dk_acc, dv_acc, delta):
        n = pl.program_id(1)
        _attn_load(q_ref, kc_ref, kp_ref, vc_ref, vp_ref, cc_ref, sc_ref, cp_ref, sp_ref, qr, kcat, vcat)
        dq_acc[...] = jnp.zeros_like(dq_acc)
        dk_acc[...] = jnp.zeros_like(dk_acc)
        dv_acc[...] = jnp.zeros_like(dv_acc)
        delta[...] = jnp.broadcast_to(
            jnp.sum(do_ref[...] * o_ref[...].astype(F32), axis=-1, keepdims=True), delta.shape)

        def visit(b, d, r, j):
            qv, qs = _unit_rows(qr, d, r, j, 1)
            kv, ks = _unit_rows(kcat, d, r, j, 2, ATT_TILE - CHUNK * d)
            vv, _ = _unit_rows(vcat, d, r, j, 2, ATT_TILE - CHUNK * d)
            dov, _ = _unit_rows(do_ref, d, r, j, 1)
            lv, _ = _unit_rows(lse_ref, d, r, j, 1)
            dlv, _ = _unit_rows(delta, d, r, j, 1)
            q, k = qv[qs, :].astype(BF16), kv[ks, :].astype(BF16)
            do = dov[qs, :].astype(BF16)
            s = lax.dot_general(q, k, NT_DIMS, preferred_element_type=F32) * scale
            s = jnp.where(_band(n, j), s, -jnp.inf)
            p = jnp.exp(s - lv[qs, :][:, 0:1])
            dp = lax.dot_general(do, vv[ks, :].astype(BF16), NT_DIMS, preferred_element_type=F32)
            ds = (p * (dp - dlv[qs, :][:, 0:1]) * scale).astype(BF16)
            dqv, _ = _unit_rows(dq_acc, d, r, j, 1)
            dkv, _ = _unit_rows(dk_acc, d, r, j, 2, ATT_TILE - CHUNK * d)
            dvv, _ = _unit_rows(dv_acc, d, r, j, 2, ATT_TILE - CHUNK * d)
            dqv[qs, :] += jnp.dot(ds, k, preferred_element_type=F32)
            dkv[ks, :] += lax.dot_general(ds, q, TN_DIMS, preferred_element_type=F32)
            dvv[ks, :] += lax.dot_general(p.astype(BF16), do, TN_DIMS, preferred_element_type=F32)

        _attn_units(n, visit)
        dq_ref[...] = _rope_bwd(dq_acc[...], cc_ref[...], sc_ref[...])
        dkp_ref[...] = _rope_bwd(dk_acc[pl.ds(0, ATT_TILE), :], cp_ref[...], sp_ref[...])
        dko_ref[...] = _rope_bwd(dk_acc[pl.ds(ATT_TILE, ATT_TILE), :], cc_ref[...], sc_ref[...])
        dvp_ref[...] = dv_acc[pl.ds(0, ATT_TILE), :]
        dvo_ref[...] = dv_acc[pl.ds(ATT_TILE, ATT_TILE), :]

    blk = (ATT_TILE, HEAD)
    tile = pl.BlockSpec(blk, lambda h, n: (n, h))
    big = pltpu.VMEM((2 * ATT_TILE, HEAD), F32)
    return pl.pallas_call(
        body, name=name, grid=(heads, t // ATT_TILE), in_specs=_attn_in_specs(heads) + [tile, tile, tile],
        out_specs=[tile] * 5,
        out_shape=[jax.ShapeDtypeStruct((t, heads * HEAD), F32)] * 5,
        scratch_shapes=[pltpu.VMEM(blk, F32), big, big, pltpu.VMEM(blk, F32), big, big, pltpu.VMEM(blk, F32)],
        compiler_params=_cp(("parallel", "parallel"), 60 << 20),
    )(qkv, qkv, qkv, qkv, qkv, cosf, sins, cosf, sins, d_o, o, lse)


def _attn_merge(name, dq, dk_own, dk_prev, dv_own, dv_prev):
    t, hd = dq.shape
    nt = t // ATT_TILE
    tw = _tile(hd, 512)

    def body(dq_ref, dko_ref, dkn_ref, dvo_ref, dvn_ref, o_ref):
        last = pl.program_id(0) == nt - 1
        part = pl.program_id(1)

        @pl.when(part == 0)
        def _():
            o_ref[...] = dq_ref[...].astype(BF16)

        @pl.when(part == 1)
        def _():
            o_ref[...] = (dko_ref[...] + jnp.where(last, 0.0, dkn_ref[...])).astype(BF16)

        @pl.when(part == 2)
        def _():
            o_ref[...] = (dvo_ref[...] + jnp.where(last, 0.0, dvn_ref[...])).astype(BF16)

    own = pl.BlockSpec((ATT_TILE, tw), lambda n, p, c: (n, c))
    nxt = pl.BlockSpec((ATT_TILE, tw), lambda n, p, c: (jnp.minimum(n + 1, nt - 1), c))
    per = hd // tw
    return pl.pallas_call(
        body, name=name, grid=(nt, 3, per), in_specs=[own, own, nxt, own, nxt],
        out_specs=pl.BlockSpec((ATT_TILE, tw), lambda n, p, c: (n, p * per + c)),
        out_shape=jax.ShapeDtypeStruct((t, 3 * hd), BF16),
        compiler_params=_cp(("parallel", "parallel", "parallel"), VMEM_MM))(dq, dk_own, dk_prev, dv_own, dv_prev)


def _sum_parts(name, parts):
    nparts, rows, cols = parts.shape
    tr = _tile(rows, 256)

    def body(p_ref, o_ref):
        s = p_ref[0]
        for k in range(1, nparts):
            s = s + p_ref[k]
        o_ref[...] = s

    return pl.pallas_call(
        body, name=name, grid=(rows // tr,),
        in_specs=[pl.BlockSpec((nparts, tr, cols), lambda i: (0, i, 0))],
        out_specs=pl.BlockSpec((tr, cols), lambda i: (i, 0)),
        out_shape=jax.ShapeDtypeStruct((rows, cols), F32),
        compiler_params=_cp(("parallel",)))(parts)


def _rows128(a, pad_to=8):
    flat = a.reshape(-1)
    rows = -(-flat.shape[0] // 128)
    rows = -(-rows // pad_to) * pad_to
    flat = jnp.pad(flat, (0, rows * 128 - flat.shape[0]))
    return flat.reshape(rows, 128)


def _pack(arrays):
    return jnp.concatenate([_rows128(a) for a in arrays], axis=0)


def _unpack(packed, like):
    out, at = [], 0
    for a in like:
        size = 1
        for s in a.shape:
            size *= s
        rows = -(-(-(-size // 128)) // 8) * 8
        out.append(packed[at:at + rows].reshape(-1)[:size].reshape(a.shape))
        at += rows
    return out


def kernel(x, norm_mix_pre, norm_mix_post, norm_mlp_pre, norm_mlp_post, w_in_ab, w_spatial, b_spatial, conv_w, w_out_ab, w_qkv, w_o, w_up, w_down, loss_target, m_norm_mix_pre, m_norm_mix_post, m_norm_mlp_pre, m_norm_mlp_post, m_w_in_ab, m_w_spatial, m_b_spatial, m_conv_w, m_w_out_ab, m_w_qkv, m_w_o, m_w_up, m_w_down, v_norm_mix_pre, v_norm_mix_post, v_norm_mlp_pre, v_norm_mlp_post, v_w_in_ab, v_w_spatial, v_b_spatial, v_conv_w, v_w_out_ab, v_w_qkv, v_w_o, v_w_up, v_w_down):
    depth = norm_mix_pre.shape[0]
    seq, dm = x.shape[1], x.shape[2]
    h0 = x.reshape(seq, dm)
    target = loss_target.reshape(seq, dm)
    ax, ay, ac = lax.axis_index("x"), lax.axis_index("y"), lax.axis_index("c")
    my_block = 4 * ax + 2 * ay + ac
    core = jnp.reshape(ac, (1,)).astype(jnp.int32)

    half = HEAD // 2
    inv_freq = ROPE_THETA ** (-jnp.arange(half, dtype=F32) * 2.0 / HEAD)
    ang = jnp.arange(seq, dtype=jnp.int32).astype(F32)[:, None] * inv_freq[None, :]
    cosf = jnp.concatenate([jnp.cos(ang), jnp.cos(ang)], axis=-1)
    sins = jnp.concatenate([-jnp.sin(ang), jnp.sin(ang)], axis=-1)

    big = {"w_in_ab": w_in_ab, "w_out_ab": w_out_ab, "w_qkv": w_qkv, "w_o": w_o, "w_up": w_up, "w_down": w_down}
    shard_list, keys = [], []
    for nm, wt in big.items():
        for l in range(wt.shape[0]):
            shard_list.append(wt[l].astype(BF16))
            keys.append((nm, l))
    n_even = w_in_ab.shape[0]
    cw_rows = jnp.pad(conv_w.reshape(n_even * CONV_TAPS, conv_w.shape[2]), ((0, HALO - (n_even * CONV_TAPS) % HALO), (0, 0)))
    gathered = _all_gather("ag_weights", shard_list + [cw_rows])
    wg = dict(zip(keys, gathered[:-1]))
    cw_all = gathered[-1][:, :n_even * CONV_TAPS].reshape(NDEV, n_even, CONV_TAPS, -1)
    cw_all = jnp.transpose(cw_all, (1, 2, 0, 3)).reshape(n_even, CONV_TAPS, -1)
    cw_full = [jnp.pad(cw_all[e], ((0, HALO - CONV_TAPS), (0, 0))) for e in range(n_even)]

    def rows_nat(blk):
        return blk.reshape(blk.shape[0] * blk.shape[1], blk.shape[2])

    saved = []
    hn = _norm_fwd("norm_first", h0, g_pre=norm_mix_pre[0][None])[0]
    h = h0
    for l in range(depth):
        s = {"h_in": h, "hn1": hn}
        if l % 2 == 0:
            e = l // 2
            proj = _mm_nn_blk(f"fwd_in_{l}", hn, wg[("w_in_ab", e)])[0]
            ab = _gate_fwd(f"gate_fwd_{l}", proj, w_spatial[e], b_spatial[e].T, cw_full[e])
            mix = _mm_nn(f"fwd_out_{l}", ab, rows_nat(wg[("w_out_ab", e)]))
            s.update(proj=proj, ab=ab)
        else:
            o_ = l // 2
            qkv = _mm_nn_blk(f"fwd_qkv_{l}", hn, wg[("w_qkv", o_)])[0]
            att, lse = _attn_fwd(f"attn_fwd_{l}", qkv, cosf, sins)
            mix = _mm_nn(f"fwd_o_{l}", att, rows_nat(wg[("w_o", o_)]))
            s.update(qkv=qkv, att=att, lse=lse)
        h1, hn2 = _norm_fwd(f"norm_mid_{l}", h, mix, norm_mix_post[l][None], norm_mlp_pre[l][None])
        up, act = _mm_nn_blk(f"fwd_up_{l}", hn2, wg[("w_up", l)], relu2=True)
        f = _mm_nn(f"fwd_down_{l}", act, rows_nat(wg[("w_down", l)]))
        s.update(mix=mix, h1=h1, hn2=hn2, up=up, act=act, f=f)
        if l + 1 < depth:
            h, hn = _norm_fwd(f"norm_end_{l}", h1, f, norm_mlp_post[l][None], norm_mix_pre[l + 1][None])
        else:
            h = _norm_fwd(f"norm_end_{l}", h1, f, norm_mlp_post[l][None])[0]
        saved.append(s)

    d_h, loss_row = _loss_grad("loss", h, target)
    grads = {}
    dg = {nm: [None] * depth for nm in ("norm_mix_pre", "norm_mix_post", "norm_mlp_pre", "norm_mlp_post")}
    d_ws, d_bs, d_cw = [None] * n_even, [None] * n_even, [None] * n_even
    d_hn_next = None
    for l in reversed(range(depth)):
        s = saved[l]
        if l == depth - 1:
            d_f, dg["norm_mlp_post"][l] = _norm_bwd(f"nb_end_{l}", d_h, post=(s["f"], norm_mlp_post[l][None]))
        else:
            d_h, dg["norm_mix_pre"][l + 1], d_f, dg["norm_mlp_post"][l] = _norm_bwd(
                f"nb_end_{l}", d_h, pre=(d_hn_next, saved[l + 1]["h_in"], norm_mix_pre[l + 1][None]),
                post=(s["f"], norm_mlp_post[l][None]))
        wd = rows_nat(wg[("w_down", l)])
        d_up = _mm_nt_rows(f"bwd_down_{l}", d_f, wd, up=s["up"])
        g_down = _mm_tn(f"gw_down_{l}", s["act"], d_f)
        grads[("w_down", l)] = g_down.reshape(NDEV, g_down.shape[0] // NDEV, g_down.shape[1])
        grads[("w_up", l)] = _mm_tn(f"gw_up_{l}", s["hn2"], d_up, nb=w_up.shape[2])
        d_hn2 = _mm_nt_blk(f"bwd_up_{l}", d_up, wg[("w_up", l)])
        d_h, dg["norm_mlp_pre"][l], d_mix, dg["norm_mix_post"][l] = _norm_bwd(
            f"nb_mid_{l}", d_h, pre=(d_hn2, s["h1"], norm_mlp_pre[l][None]),
            post=(s["mix"], norm_mix_post[l][None]))
        if l % 2 == 0:
            e = l // 2
            wo = rows_nat(wg[("w_out_ab", e)])
            d_ab = _mm_nt_rows(f"bwd_out_{l}", d_mix, wo)
            g_out = _mm_tn(f"gw_out_{l}", s["ab"], d_mix)
            grads[("w_out_ab", e)] = g_out.reshape(NDEV, g_out.shape[0] // NDEV, g_out.shape[1])
            d_proj, d_ws[e], d_bs[e], d_cw[e] = _gate_bwd(
                f"gate_bwd_{l}", s["proj"], d_ab, w_spatial[e], b_spatial[e].T, cw_full[e])
            grads[("w_in_ab", e)] = _mm_tn(f"gw_in_{l}", s["hn1"], d_proj, nb=w_in_ab.shape[2])
            d_hn_next = _mm_nt_blk(f"bwd_in_{l}", d_proj, wg[("w_in_ab", e)])
        else:
            o_ = l // 2
            wo = rows_nat(wg[("w_o", o_)])
            d_att = _mm_nt_rows(f"bwd_o_{l}", d_mix, wo)
            g_o = _mm_tn(f"gw_o_{l}", s["att"], d_mix)
            grads[("w_o", o_)] = g_o.reshape(NDEV, g_o.shape[0] // NDEV, g_o.shape[1])
            parts = _attn_bwd(f"attn_bwd_{l}", s["qkv"], cosf, sins, d_att, s["att"], s["lse"])
            d_qkv = _attn_merge(f"attn_merge_{l}", *parts)
            grads[("w_qkv", o_)] = _mm_tn(f"gw_qkv_{l}", s["hn1"], d_qkv, nb=w_qkv.shape[2])
            d_hn_next = _mm_nt_blk(f"bwd_qkv_{l}", d_qkv, wg[("w_qkv", o_)])
    grad_x, dg["norm_mix_pre"][0] = _norm_bwd("nb_first", d_h, pre=(d_hn_next, h0, norm_mix_pre[0][None]))

    glist = [grads[k] for k in keys]
    flat = [g.reshape(NDEV, -1, g.shape[-1]) for g in glist]
    got = _rs_swap("rs_swap", flat)
    pairs = [_pair_sum(f"pair_{nm}_{l}", g, r, core) for (nm, l), g, r in zip(keys, flat, got)]
    parts = _rs_scatter("rs_scatter", pairs)
    moments = {"w_in_ab": (m_w_in_ab, v_w_in_ab), "w_out_ab": (m_w_out_ab, v_w_out_ab), "w_qkv": (m_w_qkv, v_w_qkv),
               "w_o": (m_w_o, v_w_o), "w_up": (m_w_up, v_w_up), "w_down": (m_w_down, v_w_down)}
    res = {nm: [] for nm in big}
    for (nm, l), p in zip(keys, parts):
        res[nm].append(_adamw(f"adamw_{nm}_{l}", p, big[nm][l], moments[nm][0][l], moments[nm][1][l]))
    out_big = {nm: [jnp.stack([r[i] for r in res[nm]]) for i in range(4)] for nm in big}

    small_g = ([jnp.concatenate(dg[nm], axis=0) for nm in dg]
               + [jnp.stack(d_ws), jnp.stack(d_bs), jnp.stack([c[:CONV_TAPS] for c in d_cw]), loss_row])
    packed = _pack(small_g)
    summed = _sum_parts("sum_small", _all_gather("ag_small", [packed])[0])
    g_nmp, g_nmo, g_nlp, g_nlo, g_ws, g_bs, g_cw_all, loss_sum = _unpack(summed, small_g)
    loss = loss_sum[0, 0]
    cwb = conv_w.shape[2]
    g_cw = lax.dynamic_slice_in_dim(g_cw_all, my_block * cwb, cwb, axis=2)
    small_w = [norm_mix_pre, norm_mix_post, norm_mlp_pre, norm_mlp_post, w_spatial, b_spatial, conv_w]
    small_m = [m_norm_mix_pre, m_norm_mix_post, m_norm_mlp_pre, m_norm_mlp_post, m_w_spatial, m_b_spatial, m_conv_w]
    small_v = [v_norm_mix_pre, v_norm_mix_post, v_norm_mlp_pre, v_norm_mlp_post, v_w_spatial, v_b_spatial, v_conv_w]
    small_grad = [g_nmp, g_nmo, g_nlp, g_nlo, g_ws, g_bs, g_cw]
    upd = _adamw("adamw_small", _pack(small_grad)[None], _pack(small_w), _pack(small_m), _pack(small_v))
    sg, sd, sm, sv = [_unpack(u, small_w) for u in upd]

    def outs(i_small, i_big):
        return (i_small[0], i_small[1], i_small[2], i_small[3], i_big["w_in_ab"], i_small[4], i_small[5], i_small[6],
                i_big["w_out_ab"], i_big["w_qkv"], i_big["w_o"], i_big["w_up"], i_big["w_down"])

    pick = lambda i: {nm: out_big[nm][i] for nm in big}
    return (loss, grad_x.reshape(x.shape), *outs(sg, pick(0)), *outs(sd, pick(1)), *outs(sm, pick(2)),
            *outs(sv, pick(3)))
```

```python
import functools

import jax
import jax.numpy as jnp
from jax import lax
from jax.experimental import pallas as pl
from jax.experimental.pallas import tpu as pltpu

F32 = jnp.float32
BF16 = jnp.bfloat16
MESH = pl.DeviceIdType.MESH
ANY = pl.BlockSpec(memory_space=pl.ANY)

NDEV = 8
NCHIP = 4
RMS_EPS = 1e-6
LN_EPS = 1e-5
CHUNK = 128
HEAD = 128
ATT_TILE = 2048
ATT_UNROLL = 4
DILATIONS = (1, 4, 16)
ROPE_THETA = 10000.0
CONV_TAPS = 3
HALO = 8
GELU_C = 0.7978845608028654
GELU_A = 0.044715
ADAM_LR, ADAM_B1, ADAM_B2, ADAM_EPS, ADAM_WD, ADAM_STEP = 0.001, 0.9, 0.999, 1e-08, 0.01, 10
VMEM_MM = 52 << 20
VMEM_EW = 40 << 20


def _cp(sem=None, vmem=VMEM_EW):
    if sem is None:
        return pltpu.CompilerParams(vmem_limit_bytes=vmem)
    return pltpu.CompilerParams(dimension_semantics=sem, vmem_limit_bytes=vmem)


def _tile(n, want):
    return want if n % want == 0 else n


def _all_gather(name, shards):
    n = len(shards)

    def body(*refs):
        ins, outs = refs[:n], refs[n:2 * n]
        send_sems, recv_sems, local_sems = refs[2 * n:]
        x, y, c = lax.axis_index("x"), lax.axis_index("y"), lax.axis_index("c")
        me, sibling = (x, y, c), (x, y, 1 - c)
        chips = [(1 - x, y), (x, 1 - y), (1 - x, 1 - y)]

        def slot(p):
            return 4 * p[0] + 2 * p[1] + p[2]

        def copy(i, k, block, to, src=None):
            dst = outs[i].at[slot(block)]
            return pltpu.make_async_remote_copy(
                src_ref=dst if src is None else src, dst_ref=dst,
                send_sem=send_sems.at[i, k], recv_sem=recv_sems.at[i, k],
                device_id=to, device_id_type=MESH)

        mine = [pltpu.make_async_copy(ins[i], outs[i].at[slot(me)], local_sems.at[i]) for i in range(n)]
        for cp in mine:
            cp.start()
        first = []
        for i in range(n):
            first.append(copy(i, 0, me, sibling, src=ins[i]))
            for j, chip in enumerate(chips):
                first.append(copy(i, 1 + j, me, (*chip, c), src=ins[i]))
        for cp in first:
            cp.start()
        passed = []
        for j, chip in enumerate(chips):
            for i in range(n):
                copy(i, 1 + j, (*chip, c), me).wait_recv()
                fwd = copy(i, 4 + j, (*chip, c), sibling)
                fwd.start()
                passed.append(fwd)
        for i in range(n):
            copy(i, 0, sibling, me).wait_recv()
            for j, chip in enumerate(chips):
                copy(i, 4 + j, (*chip, 1 - c), me).wait_recv()
        for cp in first + passed:
            cp.wait_send()
        for cp in mine:
            cp.wait()

    return pl.pallas_call(
        body, name=name,
        out_shape=[jax.ShapeDtypeStruct((NDEV,) + s.shape, s.dtype) for s in shards],
        in_specs=[ANY] * n, out_specs=[ANY] * n,
        scratch_shapes=[pltpu.SemaphoreType.DMA((n, 7)), pltpu.SemaphoreType.DMA((n, 7)),
                        pltpu.SemaphoreType.DMA((n,))],
    )(*shards)


def _rs_swap(name, grads):
    n = len(grads)

    def body(*refs):
        ins, outs = refs[:n], refs[n:2 * n]
        send_sems, recv_sems = refs[2 * n:]
        x, y, c = lax.axis_index("x"), lax.axis_index("y"), lax.axis_index("c")
        copies = []
        for i in range(n):
            for k in range(NCHIP):
                copies.append(pltpu.make_async_remote_copy(
                    src_ref=ins[i].at[2 * k + (1 - c)], dst_ref=outs[i].at[k],
                    send_sem=send_sems.at[i, k], recv_sem=recv_sems.at[i, k],
                    device_id=(x, y, 1 - c), device_id_type=MESH))
        for cp in copies:
            cp.start()
        for cp in copies:
            cp.wait()

    return pl.pallas_call(
        body, name=name,
        out_shape=[jax.ShapeDtypeStruct((NCHIP,) + g.shape[1:], g.dtype) for g in grads],
        in_specs=[ANY] * n, out_specs=[ANY] * n,
        scratch_shapes=[pltpu.SemaphoreType.DMA((n, NCHIP)), pltpu.SemaphoreType.DMA((n, NCHIP))],
    )(*grads)


def _rs_scatter(name, pairs):
    n = len(pairs)

    def body(*refs):
        ins, outs = refs[:n], refs[n:2 * n]
        send_sems, recv_sems, local_sems = refs[2 * n:]
        x, y, c = lax.axis_index("x"), lax.axis_index("y"), lax.axis_index("c")
        my_chip = 2 * x + y
        chips = [(1 - x, y), (x, 1 - y), (1 - x, 1 - y)]
        mine = [pltpu.make_async_copy(ins[i].at[my_chip], outs[i].at[my_chip], local_sems.at[i]) for i in range(n)]
        for cp in mine:
            cp.start()
        copies = []
        for i in range(n):
            for j, chip in enumerate(chips):
                copies.append(pltpu.make_async_remote_copy(
                    src_ref=ins[i].at[2 * chip[0] + chip[1]], dst_ref=outs[i].at[my_chip],
                    send_sem=send_sems.at[i, j], recv_sem=recv_sems.at[i, j],
                    device_id=(*chip, c), device_id_type=MESH))
        for cp in copies:
            cp.start()
        for cp in copies:
            cp.wait()
        for cp in mine:
            cp.wait()

    return pl.pallas_call(
        body, name=name,
        out_shape=[jax.ShapeDtypeStruct(p.shape, p.dtype) for p in pairs],
        in_specs=[ANY] * n, out_specs=[ANY] * n,
        scratch_shapes=[pltpu.SemaphoreType.DMA((n, 3)), pltpu.SemaphoreType.DMA((n, 3)),
                        pltpu.SemaphoreType.DMA((n,))],
    )(*pairs)


def _pair_sum(name, grad, got, core):
    _, rows, cols = grad.shape
    tr = _tile(rows, 256)

    def body(core_ref, g_ref, r_ref, o_ref):
        o_ref[...] = (g_ref[...].astype(F32) + r_ref[...].astype(F32)).astype(o_ref.dtype)

    return pl.pallas_call(
        body, name=name,
        grid_spec=pltpu.PrefetchScalarGridSpec(
            num_scalar_prefetch=1, grid=(NCHIP, rows // tr),
            in_specs=[pl.BlockSpec((None, tr, cols), lambda k, i, core_ref: (2 * k + core_ref[0], i, 0)),
                      pl.BlockSpec((None, tr, cols), lambda k, i, core_ref: (k, i, 0))],
            out_specs=pl.BlockSpec((None, tr, cols), lambda k, i, core_ref: (k, i, 0))),
        out_shape=jax.ShapeDtypeStruct((NCHIP, rows, cols), grad.dtype),
        compiler_params=_cp(("parallel", "parallel")),
    )(core, grad, got)


def _adam_math(w, g, m, v):
    m = ADAM_B1 * m + (1.0 - ADAM_B1) * g
    v = ADAM_B2 * v + (1.0 - ADAM_B2) * (g * g)
    m_hat = m / (1.0 - ADAM_B1 ** ADAM_STEP)
    v_hat = v / (1.0 - ADAM_B2 ** ADAM_STEP)
    delta = -ADAM_LR * (m_hat / (jnp.sqrt(v_hat) + ADAM_EPS) + ADAM_WD * w)
    return delta, m, v


def _adamw(name, parts, w, m, v):
    nparts, rows, cols = parts.shape
    tr = _tile(rows, 256)

    def body(p_ref, w_ref, m_ref, v_ref, g_out, d_out, m_out, v_out):
        g = p_ref[0].astype(F32)
        for k in range(1, nparts):
            g = g + p_ref[k].astype(F32)
        delta, mn, vn = _adam_math(w_ref[...], g, m_ref[...], v_ref[...])
        g_out[...] = g
        d_out[...] = delta
        m_out[...] = mn
        v_out[...] = vn

    row = pl.BlockSpec((tr, cols), lambda i: (i, 0))
    return pl.pallas_call(
        body, name=name, grid=(rows // tr,),
        in_specs=[pl.BlockSpec((nparts, tr, cols), lambda i: (0, i, 0)), row, row, row],
        out_specs=[row] * 4,
        out_shape=[jax.ShapeDtypeStruct((rows, cols), F32)] * 4,
        compiler_params=_cp(("parallel",)),
    )(parts, w, m, v)


def _adamw_layers(name, parts, w, m, v):
    layers, rows, cols = w.shape
    nparts = parts[0].shape[0]
    tr = rows
    while tr % 2 == 0 and tr > 8 and nparts * tr * cols * 2 > (1 << 20):
        tr //= 2

    def body(*refs):
        p_refs = refs[:layers]
        w_ref, m_ref, v_ref, g_out, d_out, m_out, v_out = refs[layers:]
        layer = pl.program_id(0)
        for k in range(layers):
            @pl.when(layer == k)
            def _(k=k):
                g = p_refs[k][0].astype(F32)
                for s in range(1, nparts):
                    g = g + p_refs[k][s].astype(F32)
                delta, mn, vn = _adam_math(w_ref[...], g, m_ref[...], v_ref[...])
                g_out[...] = g
                d_out[...] = delta
                m_out[...] = mn
                v_out[...] = vn

    def part_spec(k):
        return pl.BlockSpec((nparts, tr, cols), lambda l, i: (0, jnp.where(l == k, i, 0), 0))

    row = pl.BlockSpec((None, tr, cols), lambda l, i: (l, i, 0))
    return pl.pallas_call(
        body, name=name, grid=(layers, rows // tr),
        in_specs=[part_spec(k) for k in range(layers)] + [row, row, row],
        out_specs=[row] * 4,
        out_shape=[jax.ShapeDtypeStruct((layers, rows, cols), F32)] * 4,
        compiler_params=_cp(("arbitrary", "arbitrary")),
    )(*parts, w, m, v)


def _cast_layer(name, w, layer):
    _, rows, cols = w.shape
    tr = _tile(rows, 256)

    def body(w_ref, o_ref):
        o_ref[...] = w_ref[...].astype(BF16)

    return pl.pallas_call(
        body, name=name, grid=(rows // tr,),
        in_specs=[pl.BlockSpec((None, tr, cols), lambda i: (layer, i, 0))],
        out_specs=pl.BlockSpec((tr, cols), lambda i: (i, 0)),
        out_shape=jax.ShapeDtypeStruct((rows, cols), BF16),
        compiler_params=_cp(("parallel",)))(w)


def _norm_fwd(name, h, z=None, g_post=None, g_pre=None):
    rows, d = h.shape
    tm = _tile(rows, 256)
    has_post, has_pre = z is not None, g_pre is not None

    def body(*refs):
        it = iter(refs)
        hv = next(it)[...]
        if has_post:
            zv, gp = next(it)[...], next(it)[...]
        if has_pre:
            gq = next(it)[...]
        if has_post:
            r = lax.rsqrt(jnp.mean(zv * zv, axis=-1, keepdims=True) + RMS_EPS)
            hv = hv + (zv * r) * gp
            next(it)[...] = hv
        if has_pre:
            r = lax.rsqrt(jnp.mean(hv * hv, axis=-1, keepdims=True) + RMS_EPS)
            next(it)[...] = ((hv * r) * gq).astype(BF16)

    row = pl.BlockSpec((tm, d), lambda i: (i, 0))
    vec = pl.BlockSpec((1, d), lambda i: (0, 0))
    ins, in_specs, out_shape, out_specs = [h], [row], [], []
    if has_post:
        ins += [z, g_post]
        in_specs += [row, vec]
        out_shape.append(jax.ShapeDtypeStruct((rows, d), F32))
        out_specs.append(row)
    if has_pre:
        ins.append(g_pre)
        in_specs.append(vec)
        out_shape.append(jax.ShapeDtypeStruct((rows, d), BF16))
        out_specs.append(row)
    return pl.pallas_call(body, name=name, grid=(rows // tm,), in_specs=in_specs, out_specs=out_specs,
                          out_shape=out_shape, compiler_params=_cp(("parallel",)))(*ins)


def _rms_bwd_rows(x, g, dy):
    r = lax.rsqrt(jnp.mean(x * x, axis=-1, keepdims=True) + RMS_EPS)
    xn = x * r
    dg = jnp.sum(dy * xn, axis=0, keepdims=True)
    dxn = dy * g
    dx = r * (dxn - xn * jnp.mean(dxn * xn, axis=-1, keepdims=True))
    return dx, dg


def _norm_bwd(name, d_out, pre=None, post=None):
    rows, d = d_out.shape
    tm = _tile(rows, 256)
    has_pre, has_post = pre is not None, post is not None

    def body(*refs):
        it = iter(refs)
        dres = next(it)[...]
        if has_pre:
            dy, xp, gq = next(it)[...], next(it)[...], next(it)[...]
        if has_post:
            zv, gp = next(it)[...], next(it)[...]
        first = pl.program_id(0) == 0
        if has_pre:
            dx, dg = _rms_bwd_rows(xp, gq, dy)
            dres = dres + dx
            next(it)[...] = dres
            dg_ref = next(it)

            @pl.when(first)
            def _():
                dg_ref[...] = jnp.zeros_like(dg_ref)
            dg_ref[...] += dg
        if has_post:
            dz, dg2 = _rms_bwd_rows(zv, gp, dres)
            next(it)[...] = dz.astype(BF16)
            dg2_ref = next(it)

            @pl.when(first)
            def _():
                dg2_ref[...] = jnp.zeros_like(dg2_ref)
            dg2_ref[...] += dg2

    row = pl.BlockSpec((tm, d), lambda i: (i, 0))
    vec = pl.BlockSpec((1, d), lambda i: (0, 0))
    ins, in_specs, out_shape, out_specs = [d_out], [row], [], []
    if has_pre:
        ins += list(pre)
        in_specs += [row, row, vec]
        out_shape += [jax.ShapeDtypeStruct((rows, d), F32), jax.ShapeDtypeStruct((1, d), F32)]
        out_specs += [row, vec]
    if has_post:
        ins += list(post)
        in_specs += [row, vec]
        out_shape += [jax.ShapeDtypeStruct((rows, d), BF16), jax.ShapeDtypeStruct((1, d), F32)]
        out_specs += [row, vec]
    return pl.pallas_call(body, name=name, grid=(rows // tm,), in_specs=in_specs, out_specs=out_specs,
                          out_shape=out_shape, compiler_params=_cp(("arbitrary",)))(*ins)


def _loss_grad(name, y, target):
    rows, d = y.shape
    tm = _tile(rows, 256)

    def body(y_ref, t_ref, dy_ref, loss_ref):
        err = y_ref[...] - t_ref[...]
        dy_ref[...] = err * (1.0 / d)

        @pl.when(pl.program_id(0) == 0)
        def _():
            loss_ref[...] = jnp.zeros_like(loss_ref)
        loss_ref[...] += jnp.full(loss_ref.shape, (0.5 / d) * jnp.sum(err * err), F32)

    row = pl.BlockSpec((tm, d), lambda i: (i, 0))
    return pl.pallas_call(
        body, name=name, grid=(rows // tm,), in_specs=[row, row],
        out_specs=[row, pl.BlockSpec((1, 128), lambda i: (0, 0))],
        out_shape=[jax.ShapeDtypeStruct((rows, d), F32), jax.ShapeDtypeStruct((1, 128), F32)],
        compiler_params=_cp(("arbitrary",)))(y, target)


NT_DIMS = (((1,), (1,)), ((), ()))
TN_DIMS = (((0,), (0,)), ((), ()))


def _mm_nn_blk(name, a, wblk, relu2=False):
    m, k = a.shape
    nb = wblk.shape[2]
    tm = _tile(m, 512)

    def body(a_ref, w_ref, *outs):
        r = jnp.dot(a_ref[...], w_ref[...], preferred_element_type=F32)
        outs[0][...] = r
        if relu2:
            rr = jnp.maximum(r, 0.0)
            outs[1][...] = (rr * rr).astype(BF16)

    out_shape = [jax.ShapeDtypeStruct((m, NDEV * nb), F32)]
    if relu2:
        out_shape.append(jax.ShapeDtypeStruct((m, NDEV * nb), BF16))
    return pl.pallas_call(
        body, name=name, grid=(NDEV, m // tm),
        in_specs=[pl.BlockSpec((tm, k), lambda d, i: (i, 0)), pl.BlockSpec((None, k, nb), lambda d, i: (d, 0, 0))],
        out_specs=[pl.BlockSpec((tm, nb), lambda d, i: (i, d))] * len(out_shape),
        out_shape=out_shape, compiler_params=_cp(("parallel", "parallel"), VMEM_MM))(a, wblk)


def _mm_nn(name, a, w):
    m, kb = a.shape
    n = w.shape[1]
    tm, tk = _tile(m, 512), _tile(kb, 2048)

    def body(a_ref, w_ref, o_ref):
        r = jnp.dot(a_ref[...], w_ref[...], preferred_element_type=F32)
        step = pl.program_id(1)

        @pl.when(step == 0)
        def _():
            o_ref[...] = r

        @pl.when(step > 0)
        def _():
            o_ref[...] += r

    return pl.pallas_call(
        body, name=name, grid=(m // tm, kb // tk),
        in_specs=[pl.BlockSpec((tm, tk), lambda i, s: (i, s)), pl.BlockSpec((tk, n), lambda i, s: (s, 0))],
        out_specs=pl.BlockSpec((tm, n), lambda i, s: (i, 0)),
        out_shape=jax.ShapeDtypeStruct((m, n), F32),
        compiler_params=_cp(("parallel", "arbitrary"), VMEM_MM))(a, w)


def _mm_nt_rows(name, dy, w, up=None):
    m, n = dy.shape
    kw = w.shape[0]
    tm, tkw = _tile(m, 512), _tile(kw, 1024)

    def body(dy_ref, w_ref, *rest):
        r = lax.dot_general(dy_ref[...], w_ref[...], NT_DIMS, preferred_element_type=F32)
        if up is None:
            rest[0][...] = r
        else:
            rest[1][...] = (r * (2.0 * jnp.maximum(rest[0][...], 0.0))).astype(BF16)

    ins = [dy, w]
    in_specs = [pl.BlockSpec((tm, n), lambda j, i: (i, 0)), pl.BlockSpec((tkw, n), lambda j, i: (j, 0))]
    if up is not None:
        ins.append(up)
        in_specs.append(pl.BlockSpec((tm, tkw), lambda j, i: (i, j)))
    return pl.pallas_call(
        body, name=name, grid=(kw // tkw, m // tm), in_specs=in_specs,
        out_specs=pl.BlockSpec((tm, tkw), lambda j, i: (i, j)),
        out_shape=jax.ShapeDtypeStruct((m, kw), F32 if up is None else BF16),
        compiler_params=_cp(("parallel", "parallel"), VMEM_MM))(*ins)


def _mm_nt_blk(name, dy, wblk):
    m = dy.shape[0]
    _, kw, nb = wblk.shape
    tm, per = _tile(m, 512), 2

    def body(dy_ref, w_ref, o_ref):
        r = lax.dot_general(dy_ref[:, :nb], w_ref[0], NT_DIMS, preferred_element_type=F32)
        for t in range(1, per):
            r = r + lax.dot_general(dy_ref[:, t * nb:(t + 1) * nb], w_ref[t], NT_DIMS, preferred_element_type=F32)
        step = pl.program_id(1)

        @pl.when(step == 0)
        def _():
            o_ref[...] = r

        @pl.when(step > 0)
        def _():
            o_ref[...] += r

    return pl.pallas_call(
        body, name=name, grid=(m // tm, NDEV // per),
        in_specs=[pl.BlockSpec((tm, per * nb), lambda i, s: (i, s)),
                  pl.BlockSpec((per, kw, nb), lambda i, s: (s, 0, 0))],
        out_specs=pl.BlockSpec((tm, kw), lambda i, s: (i, 0)),
        out_shape=jax.ShapeDtypeStruct((m, kw), F32),
        compiler_params=_cp(("parallel", "arbitrary"), VMEM_MM))(dy, wblk)


def _mm_tn(name, x, dy, nb=None):
    t, mx = x.shape
    n = dy.shape[1]
    tmx = _tile(mx, 512)
    tn = nb if nb is not None else _tile(n, 1024)

    def body(x_ref, dy_ref, o_ref):
        o_ref[...] = lax.dot_general(x_ref[...], dy_ref[...], TN_DIMS, preferred_element_type=F32).astype(BF16)

    if nb is None:
        out_shape = jax.ShapeDtypeStruct((mx, n), BF16)
        out_spec = pl.BlockSpec((tmx, tn), lambda j, i: (i, j))
    else:
        out_shape = jax.ShapeDtypeStruct((NDEV, mx, nb), BF16)
        out_spec = pl.BlockSpec((None, tmx, nb), lambda j, i: (j, i, 0))
    return pl.pallas_call(
        body, name=name, grid=(n // tn, mx // tmx),
        in_specs=[pl.BlockSpec((t, tmx), lambda j, i: (0, i)), pl.BlockSpec((t, tn), lambda j, i: (0, j))],
        out_specs=out_spec, out_shape=out_shape,
        compiler_params=_cp(("parallel", "parallel"), VMEM_MM))(x, dy)


def _gelu(x):
    return 0.5 * x * (1.0 + jnp.tanh(GELU_C * (x + GELU_A * (x * x * x))))


def _gelu_grad(x):
    t = jnp.tanh(GELU_C * (x + GELU_A * (x * x * x)))
    return 0.5 * (1.0 + t) + 0.5 * x * (1.0 - t * t) * (GELU_C * (1.0 + 3.0 * GELU_A * (x * x)))


def _layernorm(a):
    mu = jnp.mean(a, axis=-1, keepdims=True)
    ac = a - mu
    rstd = lax.rsqrt(jnp.mean(ac * ac, axis=-1, keepdims=True) + LN_EPS)
    return ac * rstd, rstd


def _shift_rows(z, halo, k):
    zr = pltpu.roll(z, k, 0)
    hr = pltpu.roll(halo, k, 0)
    row = lax.broadcasted_iota(jnp.int32, hr.shape, 0)
    top = jnp.where(row < k, hr, zr[:HALO])
    return jnp.concatenate([top, zr[HALO:]], axis=0)


def _shift_rows_up(z, halo, k):
    rows = z.shape[0]
    zr = pltpu.roll(z, rows - k, 0)
    hr = pltpu.roll(halo, HALO - k, 0)
    row = lax.broadcasted_iota(jnp.int32, hr.shape, 0)
    bot = jnp.where(row >= HALO - k, hr, zr[rows - HALO:])
    return jnp.concatenate([zr[:rows - HALO], bot], axis=0)


def _causal_mask():
    t = lax.broadcasted_iota(jnp.int32, (CHUNK, CHUNK), 0)
    s = lax.broadcasted_iota(jnp.int32, (CHUNK, CHUNK), 1)
    return s <= t


def _gate_specs(tm, rows, width):
    per = tm // HALO
    last = rows // HALO - 1
    cur = pl.BlockSpec((tm, width), lambda i: (i, 0))
    prev = pl.BlockSpec((HALO, width), lambda i: (jnp.maximum(i * per - 1, 0), 0))
    nxt = pl.BlockSpec((HALO, width), lambda i: (jnp.minimum((i + 1) * per, last), 0))
    return cur, prev, nxt


def _gate_fwd(name, proj, w_s, b_st, cw):
    rows, width = proj.shape
    w = width // 5
    groups = w // CHUNK
    tm = _tile(rows, 256)
    cur, prev, _ = _gate_specs(tm, rows, width)

    def body(p_ref, h_ref, ws_ref, b_ref, cw_ref, o_ref):
        mask = _causal_mask()
        au = _gelu(p_ref[:, 0:w])
        vn, _ = _layernorm(_gelu(p_ref[:, w:2 * w]))
        vn = vn.astype(BF16)
        for g in range(groups):
            wc = jnp.where(mask, ws_ref[g], 0.0).astype(BF16)
            cols = slice(g * CHUNK, (g + 1) * CHUNK)
            for ch in range(tm // CHUNK):
                rws = slice(ch * CHUNK, (ch + 1) * CHUNK)
                mixed = jnp.dot(wc, vn[rws, cols], preferred_element_type=F32) + b_ref[:, g:g + 1]
                o_ref[rws, cols] = (au[rws, cols] * mixed).astype(BF16)
        z = p_ref[:, 3 * w:4 * w] * p_ref[:, 4 * w:5 * w]
        zh = h_ref[:, 3 * w:4 * w] * h_ref[:, 4 * w:5 * w]
        zh = jnp.where(pl.program_id(0) == 0, 0.0, zh)
        y = cw_ref[0:1, :] * _shift_rows(z, zh, 2) + cw_ref[1:2, :] * _shift_rows(z, zh, 1) + cw_ref[2:3, :] * z
        o_ref[:, w:2 * w] = (p_ref[:, 2 * w:3 * w] * y).astype(BF16)

    full = lambda a: pl.BlockSpec(a.shape, lambda i: (0,) * a.ndim)
    return pl.pallas_call(
        body, name=name, grid=(rows // tm,),
        in_specs=[cur, prev, full(w_s), full(b_st), full(cw)],
        out_specs=pl.BlockSpec((tm, 2 * w), lambda i: (i, 0)),
        out_shape=jax.ShapeDtypeStruct((rows, 2 * w), BF16),
        compiler_params=_cp(("parallel",)))(proj, proj, w_s, b_st, cw)


def _gate_bwd(name, proj, d_ab, w_s, b_st, cw):
    rows, width = proj.shape
    w = width // 5
    groups = w // CHUNK
    tm = _tile(rows, 256)
    cur, prev, nxt = _gate_specs(tm, rows, width)
    dcur, _, dnxt = _gate_specs(tm, rows, 2 * w)

    def body(p_ref, ph_ref, pn_ref, d_ref, dn_ref, ws_ref, b_ref, cw_ref, o_ref, dws_ref, dbs_ref, dcw_ref):
        i = pl.program_id(0)

        @pl.when(i == 0)
        def _():
            dws_ref[...] = jnp.zeros_like(dws_ref)
            dbs_ref[...] = jnp.zeros_like(dbs_ref)
            dcw_ref[...] = jnp.zeros_like(dcw_ref)

        mask = _causal_mask()
        u, v = p_ref[:, 0:w], p_ref[:, w:2 * w]
        au, av = _gelu(u), _gelu(v)
        vn, rstd = _layernorm(av)
        vnb = vn.astype(BF16)
        d_a = d_ref[:, 0:w]
        d_mixed = (d_a * au).astype(BF16)
        ones = jnp.ones((HALO, CHUNK), BF16)
        d_vn_cols = []
        d_au_cols = []
        for g in range(groups):
            wc = jnp.where(mask, ws_ref[g], 0.0).astype(BF16)
            cols = slice(g * CHUNK, (g + 1) * CHUNK)
            dw = jnp.zeros((CHUNK, CHUNK), F32)
            db = jnp.zeros((HALO, CHUNK), F32)
            d_vn_rows, d_au_rows = [], []
            for ch in range(tm // CHUNK):
                rws = slice(ch * CHUNK, (ch + 1) * CHUNK)
                mixed = jnp.dot(wc, vnb[rws, cols], preferred_element_type=F32) + b_ref[:, g:g + 1]
                d_au_rows.append(d_a[rws, cols] * mixed)
                dm = d_mixed[rws, cols]
                dw = dw + lax.dot_general(dm, vnb[rws, cols], NT_DIMS, preferred_element_type=F32)
                db = db + lax.dot_general(ones, dm, NT_DIMS, preferred_element_type=F32)
                d_vn_rows.append(lax.dot_general(wc, dm, TN_DIMS, preferred_element_type=F32))
            dws_ref[g] += jnp.where(mask, dw, 0.0)
            dbs_ref[g:g + 1, :] += db[0:1, :]
            d_vn_cols.append(jnp.concatenate(d_vn_rows, axis=0))
            d_au_cols.append(jnp.concatenate(d_au_rows, axis=0))
        d_vn = jnp.concatenate(d_vn_cols, axis=1)
        d_au = jnp.concatenate(d_au_cols, axis=1)
        d_av = rstd * (d_vn - jnp.mean(d_vn, axis=-1, keepdims=True)
                       - vn * jnp.mean(d_vn * vn, axis=-1, keepdims=True))
        o_ref[:, 0:w] = (d_au * _gelu_grad(u)).astype(BF16)
        o_ref[:, w:2 * w] = (d_av * _gelu_grad(v)).astype(BF16)

        gb, gc, bx = p_ref[:, 2 * w:3 * w], p_ref[:, 3 * w:4 * w], p_ref[:, 4 * w:5 * w]
        z = gc * bx
        zh = jnp.where(i == 0, 0.0, ph_ref[:, 3 * w:4 * w] * ph_ref[:, 4 * w:5 * w])
        z1, z2 = _shift_rows(z, zh, 1), _shift_rows(z, zh, 2)
        d_b = d_ref[:, w:2 * w]
        y = cw_ref[0:1, :] * z2 + cw_ref[1:2, :] * z1 + cw_ref[2:3, :] * z
        dy = d_b * gb
        dyn = jnp.where(i == pl.num_programs(0) - 1, 0.0, dn_ref[:, w:2 * w] * pn_ref[:, 2 * w:3 * w])
        dz = (cw_ref[2:3, :] * dy + cw_ref[1:2, :] * _shift_rows_up(dy, dyn, 1)
              + cw_ref[0:1, :] * _shift_rows_up(dy, dyn, 2))
        dcw_ref[0:1, :] += jnp.sum(dy * z2, axis=0, keepdims=True)
        dcw_ref[1:2, :] += jnp.sum(dy * z1, axis=0, keepdims=True)
        dcw_ref[2:3, :] += jnp.sum(dy * z, axis=0, keepdims=True)
        o_ref[:, 2 * w:3 * w] = (d_b * y).astype(BF16)
        o_ref[:, 3 * w:4 * w] = (dz * bx).astype(BF16)
        o_ref[:, 4 * w:5 * w] = (dz * gc).astype(BF16)

    full = lambda a: pl.BlockSpec(a.shape, lambda i: (0,) * a.ndim)
    acc = lambda shape: pl.BlockSpec(shape, lambda i: (0,) * len(shape))
    return pl.pallas_call(
        body, name=name, grid=(rows // tm,),
        in_specs=[cur, prev, nxt, dcur, dnxt, full(w_s), full(b_st), full(cw)],
        out_specs=[pl.BlockSpec((tm, width), lambda i: (i, 0)), acc((groups, CHUNK, CHUNK)),
                   acc((groups, CHUNK)), acc((HALO, w))],
        out_shape=[jax.ShapeDtypeStruct((rows, width), BF16), jax.ShapeDtypeStruct((groups, CHUNK, CHUNK), F32),
                   jax.ShapeDtypeStruct((groups, CHUNK), F32), jax.ShapeDtypeStruct((HALO, w), F32)],
        compiler_params=_cp(("arbitrary",), VMEM_MM))(proj, proj, proj, d_ab, d_ab, w_s, b_st, cw)


def _rope(t, cosf, sins):
    return t * cosf + pltpu.roll(t, HEAD // 2, 1) * sins


def _rope_bwd(dt, cosf, sins):
    return dt * cosf + pltpu.roll(dt * sins, HEAD // 2, 1)


def _attn_units(visit):
    for b, d in enumerate(DILATIONS):
        blocks = ATT_TILE // (CHUNK * d)
        for r in range(d):
            if blocks <= ATT_UNROLL:
                for j in range(blocks):
                    visit(b, d, r, j)
            else:
                def step(jj, carry, b=b, d=d, r=r):
                    for u in range(ATT_UNROLL):
                        visit(b, d, r, jj * ATT_UNROLL + u)
                    return carry
                lax.fori_loop(0, blocks // ATT_UNROLL, step, 0)


def _unit_rows(ref, d, r, j, nblk, offset=0):
    base = offset + j * (CHUNK * d)
    if not isinstance(base, int):
        base = pl.multiple_of(base, CHUNK)
    return ref.at[pl.ds(base, nblk * CHUNK * d)], pl.ds(r, nblk * CHUNK, stride=d)


def _band_bias(bias):
    qi = lax.broadcasted_iota(jnp.int32, (CHUNK, 2 * CHUNK), 0)
    ki = lax.broadcasted_iota(jnp.int32, (CHUNK, 2 * CHUNK), 1)
    band = (ki >= qi) & (ki <= qi + CHUNK)
    bias[0] = jnp.where(band, 0.0, -jnp.inf)
    bias[1] = jnp.where(band & (ki >= CHUNK), 0.0, -jnp.inf)


def _unit_bias(bias, n, j):
    if isinstance(j, int) and j != 0:
        return bias[0]
    return bias[jnp.where(jnp.logical_and(n == 0, j == 0), 1, 0)]


def _attn_in_specs(heads):
    blk = (ATT_TILE, HEAD)
    prev = lambda n: jnp.maximum(n - 1, 0)
    return [
        pl.BlockSpec(blk, lambda h, n: (n, h)),
        pl.BlockSpec(blk, lambda h, n: (n, heads + h)),
        pl.BlockSpec(blk, lambda h, n: (prev(n), heads + h)),
        pl.BlockSpec(blk, lambda h, n: (n, 2 * heads + h)),
        pl.BlockSpec(blk, lambda h, n: (prev(n), 2 * heads + h)),
        pl.BlockSpec(blk, lambda h, n: (n, 0)),
        pl.BlockSpec(blk, lambda h, n: (n, 0)),
        pl.BlockSpec(blk, lambda h, n: (prev(n), 0)),
        pl.BlockSpec(blk, lambda h, n: (prev(n), 0)),
    ]


def _attn_load(q_ref, kc_ref, kp_ref, vc_ref, vp_ref, cc_ref, sc_ref, cp_ref, sp_ref, qr, kcat, vcat):
    qr[...] = _rope(q_ref[...], cc_ref[...], sc_ref[...])
    kcat[pl.ds(0, ATT_TILE), :] = _rope(kp_ref[...], cp_ref[...], sp_ref[...])
    kcat[pl.ds(ATT_TILE, ATT_TILE), :] = _rope(kc_ref[...], cc_ref[...], sc_ref[...])
    vcat[pl.ds(0, ATT_TILE), :] = vp_ref[...]
    vcat[pl.ds(ATT_TILE, ATT_TILE), :] = vc_ref[...]


def _attn_fwd(name, qkv, cosf, sins):
    t = qkv.shape[0]
    heads = qkv.shape[1] // (3 * HEAD)
    scale = HEAD ** -0.5
    nbr = len(DILATIONS)

    def body(q_ref, kc_ref, kp_ref, vc_ref, vp_ref, cc_ref, sc_ref, cp_ref, sp_ref, o_ref, lse_ref,
             qr, kcat, vcat, obr, lbr, bias):
        n = pl.program_id(1)
        _attn_load(q_ref, kc_ref, kp_ref, vc_ref, vp_ref, cc_ref, sc_ref, cp_ref, sp_ref, qr, kcat, vcat)
        _band_bias(bias)

        def visit(b, d, r, j):
            qv, qs = _unit_rows(qr, d, r, j, 1)
            kv, ks = _unit_rows(kcat, d, r, j, 2, ATT_TILE - CHUNK * d)
            vv, _ = _unit_rows(vcat, d, r, j, 2, ATT_TILE - CHUNK * d)
            s = lax.dot_general(qv[qs, :].astype(BF16), kv[ks, :].astype(BF16), NT_DIMS,
                                preferred_element_type=F32) * scale + _unit_bias(bias, n, j)
            mx = jnp.max(s, axis=-1, keepdims=True)
            p = jnp.exp(s - mx)
            den = jnp.sum(p, axis=-1, keepdims=True)
            o = jnp.dot((p * (1.0 / den)).astype(BF16), vv[ks, :].astype(BF16), preferred_element_type=F32)
            ov, _ = _unit_rows(obr.at[b], d, r, j, 1)
            lv, _ = _unit_rows(lbr.at[b], d, r, j, 1)
            ov[qs, :] = o
            lv[qs, :] = jnp.broadcast_to(mx + jnp.log(den), (CHUNK, HEAD))

        _attn_units(visit)
        ls = [lbr[b] for b in range(nbr)]
        top = functools.reduce(jnp.maximum, ls)
        ws = [jnp.exp(l - top) for l in ls]
        tot = functools.reduce(jnp.add, ws)
        o = (ws[0] / tot) * obr[0]
        for b in range(1, nbr):
            o = o + (ws[b] / tot) * obr[b]
        o_ref[...] = o.astype(BF16)
        lse_ref[...] = top + jnp.log(tot)

    blk = (ATT_TILE, HEAD)
    tile = pl.BlockSpec(blk, lambda h, n: (n, h))
    return pl.pallas_call(
        body, name=name, grid=(heads, t // ATT_TILE), in_specs=_attn_in_specs(heads),
        out_specs=[tile, tile],
        out_shape=[jax.ShapeDtypeStruct((t, heads * HEAD), BF16), jax.ShapeDtypeStruct((t, heads * HEAD), F32)],
        scratch_shapes=[pltpu.VMEM(blk, F32), pltpu.VMEM((2 * ATT_TILE, HEAD), F32), pltpu.VMEM((2 * ATT_TILE, HEAD), F32),
                        pltpu.VMEM((nbr,) + blk, F32), pltpu.VMEM((nbr,) + blk, F32),
                        pltpu.VMEM((2, CHUNK, 2 * CHUNK), F32)],
        compiler_params=_cp(("parallel", "parallel"), VMEM_MM),
    )(qkv, qkv, qkv, qkv, qkv, cosf, sins, cosf, sins)


def _attn_bwd(name, qkv, cosf, sins, d_o, o, lse):
    t = qkv.shape[0]
    heads = qkv.shape[1] // (3 * HEAD)
    scale = HEAD ** -0.5

    def body(q_ref, kc_ref, kp_ref, vc_ref, vp_ref, cc_ref, sc_ref, cp_ref, sp_ref, do_ref, o_ref, lse_ref,
             dq_ref, dko_ref, dkp_ref, dvo_ref, dvp_ref, qr, kcat, vcat, dq_acc, dk_acc, dv_acc, delta, bias):
        n = pl.program_id(1)
        _attn_load(q_ref, kc_ref, kp_ref, vc_ref, vp_ref, cc_ref, sc_ref, cp_ref, sp_ref, qr, kcat, vcat)
        _band_bias(bias)
        dq_acc[...] = jnp.zeros_like(dq_acc)
        dk_acc[...] = jnp.zeros_like(dk_acc)
        dv_acc[...] = jnp.zeros_like(dv_acc)
        delta[...] = jnp.broadcast_to(
            jnp.sum(do_ref[...] * o_ref[...].astype(F32), axis=-1, keepdims=True), delta.shape)

        def visit(b, d, r, j):
            qv, qs = _unit_rows(qr, d, r, j, 1)
            kv, ks = _unit_rows(kcat, d, r, j, 2, ATT_TILE - CHUNK * d)
            vv, _ = _unit_rows(vcat, d, r, j, 2, ATT_TILE - CHUNK * d)
            dov, _ = _unit_rows(do_ref, d, r, j, 1)
            lv, _ = _unit_rows(lse_ref, d, r, j, 1)
            dlv, _ = _unit_rows(delta, d, r, j, 1)
            q, k = qv[qs, :].astype(BF16), kv[ks, :].astype(BF16)
            do = dov[qs, :].astype(BF16)
            s = lax.dot_general(q, k, NT_DIMS, preferred_element_type=F32) * scale + _unit_bias(bias, n, j)
            p = jnp.exp(s - lv[qs, :][:, 0:1])
            dp = lax.dot_general(do, vv[ks, :].astype(BF16), NT_DIMS, preferred_element_type=F32)
            ds = (p * (dp - dlv[qs, :][:, 0:1]) * scale).astype(BF16)
            dqv, _ = _unit_rows(dq_acc, d, r, j, 1)
            dkv, _ = _unit_rows(dk_acc, d, r, j, 2, ATT_TILE - CHUNK * d)
            dvv, _ = _unit_rows(dv_acc, d, r, j, 2, ATT_TILE - CHUNK * d)
            dqv[qs, :] += jnp.dot(ds, k, preferred_element_type=F32)
            dkv[ks, :] += lax.dot_general(ds, q, TN_DIMS, preferred_element_type=F32)
            dvv[ks, :] += lax.dot_general(p.astype(BF16), do, TN_DIMS, preferred_element_type=F32)

        _attn_units(visit)
        dq_ref[...] = _rope_bwd(dq_acc[...], cc_ref[...], sc_ref[...])
        dkp_ref[...] = _rope_bwd(dk_acc[pl.ds(0, ATT_TILE), :], cp_ref[...], sp_ref[...])
        dko_ref[...] = _rope_bwd(dk_acc[pl.ds(ATT_TILE, ATT_TILE), :], cc_ref[...], sc_ref[...])
        dvp_ref[...] = dv_acc[pl.ds(0, ATT_TILE), :]
        dvo_ref[...] = dv_acc[pl.ds(ATT_TILE, ATT_TILE), :]

    blk = (ATT_TILE, HEAD)
    tile = pl.BlockSpec(blk, lambda h, n: (n, h))
    big = pltpu.VMEM((2 * ATT_TILE, HEAD), F32)
    return pl.pallas_call(
        body, name=name, grid=(heads, t // ATT_TILE), in_specs=_attn_in_specs(heads) + [tile, tile, tile],
        out_specs=[tile] * 5,
        out_shape=[jax.ShapeDtypeStruct((t, heads * HEAD), F32)] * 5,
        scratch_shapes=[pltpu.VMEM(blk, F32), big, big, pltpu.VMEM(blk, F32), big, big, pltpu.VMEM(blk, F32),
                        pltpu.VMEM((2, CHUNK, 2 * CHUNK), F32)],
        compiler_params=_cp(("parallel", "parallel"), 60 << 20),
    )(qkv, qkv, qkv, qkv, qkv, cosf, sins, cosf, sins, d_o, o, lse)


def _attn_merge(name, dq, dk_own, dk_prev, dv_own, dv_prev):
    t, hd = dq.shape
    nt = t // ATT_TILE
    tw = _tile(hd, 512)

    def body(dq_ref, dko_ref, dkn_ref, dvo_ref, dvn_ref, o_ref):
        last = pl.program_id(0) == nt - 1
        part = pl.program_id(1)

        @pl.when(part == 0)
        def _():
            o_ref[...] = dq_ref[...].astype(BF16)

        @pl.when(part == 1)
        def _():
            o_ref[...] = (dko_ref[...] + jnp.where(last, 0.0, dkn_ref[...])).astype(BF16)

        @pl.when(part == 2)
        def _():
            o_ref[...] = (dvo_ref[...] + jnp.where(last, 0.0, dvn_ref[...])).astype(BF16)

    def own(part):
        return pl.BlockSpec((ATT_TILE, tw), lambda n, p, c: (jnp.where(p == part, n, 0), jnp.where(p == part, c, 0)))

    def nxt(part):
        return pl.BlockSpec((ATT_TILE, tw), lambda n, p, c: (jnp.where(p == part, jnp.minimum(n + 1, nt - 1), 0),
                                                            jnp.where(p == part, c, 0)))

    per = hd // tw
    return pl.pallas_call(
        body, name=name, grid=(nt, 3, per), in_specs=[own(0), own(1), nxt(1), own(2), nxt(2)],
        out_specs=pl.BlockSpec((ATT_TILE, tw), lambda n, p, c: (n, p * per + c)),
        out_shape=jax.ShapeDtypeStruct((t, 3 * hd), BF16),
        compiler_params=_cp(("parallel", "parallel", "parallel"), VMEM_MM))(dq, dk_own, dk_prev, dv_own, dv_prev)


def _sum_parts(name, parts):
    nparts, rows, cols = parts.shape
    tr = _tile(rows, 256)

    def body(p_ref, o_ref):
        s = p_ref[0]
        for k in range(1, nparts):
            s = s + p_ref[k]
        o_ref[...] = s

    return pl.pallas_call(
        body, name=name, grid=(rows // tr,),
        in_specs=[pl.BlockSpec((nparts, tr, cols), lambda i: (0, i, 0))],
        out_specs=pl.BlockSpec((tr, cols), lambda i: (i, 0)),
        out_shape=jax.ShapeDtypeStruct((rows, cols), F32),
        compiler_params=_cp(("parallel",)))(parts)


def _rows128(a, pad_to=8):
    flat = a.reshape(-1)
    rows = -(-flat.shape[0] // 128)
    rows = -(-rows // pad_to) * pad_to
    flat = jnp.pad(flat, (0, rows * 128 - flat.shape[0]))
    return flat.reshape(rows, 128)


def _pack(arrays):
    return jnp.concatenate([_rows128(a) for a in arrays], axis=0)


def _unpack(packed, like):
    out, at = [], 0
    for a in like:
        size = 1
        for s in a.shape:
            size *= s
        rows = -(-(-(-size // 128)) // 8) * 8
        out.append(packed[at:at + rows].reshape(-1)[:size].reshape(a.shape))
        at += rows
    return out


def kernel(x, norm_mix_pre, norm_mix_post, norm_mlp_pre, norm_mlp_post, w_in_ab, w_spatial, b_spatial, conv_w, w_out_ab, w_qkv, w_o, w_up, w_down, loss_target, m_norm_mix_pre, m_norm_mix_post, m_norm_mlp_pre, m_norm_mlp_post, m_w_in_ab, m_w_spatial, m_b_spatial, m_conv_w, m_w_out_ab, m_w_qkv, m_w_o, m_w_up, m_w_down, v_norm_mix_pre, v_norm_mix_post, v_norm_mlp_pre, v_norm_mlp_post, v_w_in_ab, v_w_spatial, v_b_spatial, v_conv_w, v_w_out_ab, v_w_qkv, v_w_o, v_w_up, v_w_down):
    depth = norm_mix_pre.shape[0]
    seq, dm = x.shape[1], x.shape[2]
    h0 = x.reshape(seq, dm)
    target = loss_target.reshape(seq, dm)
    ax, ay, ac = lax.axis_index("x"), lax.axis_index("y"), lax.axis_index("c")
    my_block = 4 * ax + 2 * ay + ac
    core = jnp.reshape(ac, (1,)).astype(jnp.int32)

    half = HEAD // 2
    inv_freq = ROPE_THETA ** (-jnp.arange(half, dtype=F32) * 2.0 / HEAD)
    ang = jnp.arange(seq, dtype=jnp.int32).astype(F32)[:, None] * inv_freq[None, :]
    cosf = jnp.concatenate([jnp.cos(ang), jnp.cos(ang)], axis=-1)
    sins = jnp.concatenate([-jnp.sin(ang), jnp.sin(ang)], axis=-1)

    big = {"w_in_ab": w_in_ab, "w_out_ab": w_out_ab, "w_qkv": w_qkv, "w_o": w_o, "w_up": w_up, "w_down": w_down}
    shard_list, keys = [], []
    for nm, wt in big.items():
        for l in range(wt.shape[0]):
            shard_list.append(_cast_layer(f"cast_{nm}_{l}", wt, l))
            keys.append((nm, l))
    n_even = w_in_ab.shape[0]
    cw_rows = jnp.pad(conv_w.reshape(n_even * CONV_TAPS, conv_w.shape[2]), ((0, HALO - (n_even * CONV_TAPS) % HALO), (0, 0)))
    gathered = _all_gather("ag_weights", shard_list + [cw_rows])
    wg = dict(zip(keys, gathered[:-1]))
    cw_all = gathered[-1][:, :n_even * CONV_TAPS].reshape(NDEV, n_even, CONV_TAPS, -1)
    cw_all = jnp.transpose(cw_all, (1, 2, 0, 3)).reshape(n_even, CONV_TAPS, -1)
    cw_full = [jnp.pad(cw_all[e], ((0, HALO - CONV_TAPS), (0, 0))) for e in range(n_even)]

    def rows_nat(blk):
        return blk.reshape(blk.shape[0] * blk.shape[1], blk.shape[2])

    saved = []
    hn = _norm_fwd("norm_first", h0, g_pre=norm_mix_pre[0][None])[0]
    h = h0
    for l in range(depth):
        s = {"h_in": h, "hn1": hn}
        if l % 2 == 0:
            e = l // 2
            proj = _mm_nn_blk(f"fwd_in_{l}", hn, wg[("w_in_ab", e)])[0]
            ab = _gate_fwd(f"gate_fwd_{l}", proj, w_spatial[e], b_spatial[e].T, cw_full[e])
            mix = _mm_nn(f"fwd_out_{l}", ab, rows_nat(wg[("w_out_ab", e)]))
            s.update(proj=proj, ab=ab)
        else:
            o_ = l // 2
            qkv = _mm_nn_blk(f"fwd_qkv_{l}", hn, wg[("w_qkv", o_)])[0]
            att, lse = _attn_fwd(f"attn_fwd_{l}", qkv, cosf, sins)
            mix = _mm_nn(f"fwd_o_{l}", att, rows_nat(wg[("w_o", o_)]))
            s.update(qkv=qkv, att=att, lse=lse)
        h1, hn2 = _norm_fwd(f"norm_mid_{l}", h, mix, norm_mix_post[l][None], norm_mlp_pre[l][None])
        up, act = _mm_nn_blk(f"fwd_up_{l}", hn2, wg[("w_up", l)], relu2=True)
        f = _mm_nn(f"fwd_down_{l}", act, rows_nat(wg[("w_down", l)]))
        s.update(mix=mix, h1=h1, hn2=hn2, up=up, act=act, f=f)
        if l + 1 < depth:
            h, hn = _norm_fwd(f"norm_end_{l}", h1, f, norm_mlp_post[l][None], norm_mix_pre[l + 1][None])
        else:
            h = _norm_fwd(f"norm_end_{l}", h1, f, norm_mlp_post[l][None])[0]
        saved.append(s)

    d_h, loss_row = _loss_grad("loss", h, target)
    grads = {}
    dg = {nm: [None] * depth for nm in ("norm_mix_pre", "norm_mix_post", "norm_mlp_pre", "norm_mlp_post")}
    d_ws, d_bs, d_cw = [None] * n_even, [None] * n_even, [None] * n_even
    d_hn_next = None
    for l in reversed(range(depth)):
        s = saved[l]
        if l == depth - 1:
            d_f, dg["norm_mlp_post"][l] = _norm_bwd(f"nb_end_{l}", d_h, post=(s["f"], norm_mlp_post[l][None]))
        else:
            d_h, dg["norm_mix_pre"][l + 1], d_f, dg["norm_mlp_post"][l] = _norm_bwd(
                f"nb_end_{l}", d_h, pre=(d_hn_next, saved[l + 1]["h_in"], norm_mix_pre[l + 1][None]),
                post=(s["f"], norm_mlp_post[l][None]))
        wd = rows_nat(wg[("w_down", l)])
        d_up = _mm_nt_rows(f"bwd_down_{l}", d_f, wd, up=s["up"])
        g_down = _mm_tn(f"gw_down_{l}", s["act"], d_f)
        grads[("w_down", l)] = g_down.reshape(NDEV, g_down.shape[0] // NDEV, g_down.shape[1])
        grads[("w_up", l)] = _mm_tn(f"gw_up_{l}", s["hn2"], d_up, nb=w_up.shape[2])
        d_hn2 = _mm_nt_blk(f"bwd_up_{l}", d_up, wg[("w_up", l)])
        d_h, dg["norm_mlp_pre"][l], d_mix, dg["norm_mix_post"][l] = _norm_bwd(
            f"nb_mid_{l}", d_h, pre=(d_hn2, s["h1"], norm_mlp_pre[l][None]),
            post=(s["mix"], norm_mix_post[l][None]))
        if l % 2 == 0:
            e = l // 2
            wo = rows_nat(wg[("w_out_ab", e)])
            d_ab = _mm_nt_rows(f"bwd_out_{l}", d_mix, wo)
            g_out = _mm_tn(f"gw_out_{l}", s["ab"], d_mix)
            grads[("w_out_ab", e)] = g_out.reshape(NDEV, g_out.shape[0] // NDEV, g_out.shape[1])
            d_proj, d_ws[e], d_bs[e], d_cw[e] = _gate_bwd(
                f"gate_bwd_{l}", s["proj"], d_ab, w_spatial[e], b_spatial[e].T, cw_full[e])
            grads[("w_in_ab", e)] = _mm_tn(f"gw_in_{l}", s["hn1"], d_proj, nb=w_in_ab.shape[2])
            d_hn_next = _mm_nt_blk(f"bwd_in_{l}", d_proj, wg[("w_in_ab", e)])
        else:
            o_ = l // 2
            wo = rows_nat(wg[("w_o", o_)])
            d_att = _mm_nt_rows(f"bwd_o_{l}", d_mix, wo)
            g_o = _mm_tn(f"gw_o_{l}", s["att"], d_mix)
            grads[("w_o", o_)] = g_o.reshape(NDEV, g_o.shape[0] // NDEV, g_o.shape[1])
            parts = _attn_bwd(f"attn_bwd_{l}", s["qkv"], cosf, sins, d_att, s["att"], s["lse"])
            d_qkv = _attn_merge(f"attn_merge_{l}", *parts)
            grads[("w_qkv", o_)] = _mm_tn(f"gw_qkv_{l}", s["hn1"], d_qkv, nb=w_qkv.shape[2])
            d_hn_next = _mm_nt_blk(f"bwd_qkv_{l}", d_qkv, wg[("w_qkv", o_)])
    grad_x, dg["norm_mix_pre"][0] = _norm_bwd("nb_first", d_h, pre=(d_hn_next, h0, norm_mix_pre[0][None]))

    glist = [grads[k] for k in keys]
    flat = [g.reshape(NDEV, -1, g.shape[-1]) for g in glist]
    got = _rs_swap("rs_swap", flat)
    pairs = [_pair_sum(f"pair_{nm}_{l}", g, r, core) for (nm, l), g, r in zip(keys, flat, got)]
    parts = _rs_scatter("rs_scatter", pairs)
    moments = {"w_in_ab": (m_w_in_ab, v_w_in_ab), "w_out_ab": (m_w_out_ab, v_w_out_ab), "w_qkv": (m_w_qkv, v_w_qkv),
               "w_o": (m_w_o, v_w_o), "w_up": (m_w_up, v_w_up), "w_down": (m_w_down, v_w_down)}
    by_name = {nm: [] for nm in big}
    for (nm, l), p in zip(keys, parts):
        by_name[nm].append(p)
    out_big = {nm: _adamw_layers(f"adamw_{nm}", by_name[nm], big[nm], moments[nm][0], moments[nm][1]) for nm in big}

    small_g = ([jnp.concatenate(dg[nm], axis=0) for nm in dg]
               + [jnp.stack(d_ws), jnp.stack(d_bs), jnp.stack([c[:CONV_TAPS] for c in d_cw]), loss_row])
    packed = _pack(small_g)
    summed = _sum_parts("sum_small", _all_gather("ag_small", [packed])[0])
    g_nmp, g_nmo, g_nlp, g_nlo, g_ws, g_bs, g_cw_all, loss_sum = _unpack(summed, small_g)
    loss = loss_sum[0, 0]
    cwb = conv_w.shape[2]
    g_cw = lax.dynamic_slice_in_dim(g_cw_all, my_block * cwb, cwb, axis=2)
    small_w = [norm_mix_pre, norm_mix_post, norm_mlp_pre, norm_mlp_post, w_spatial, b_spatial, conv_w]
    small_m = [m_norm_mix_pre, m_norm_mix_post, m_norm_mlp_pre, m_norm_mlp_post, m_w_spatial, m_b_spatial, m_conv_w]
    small_v = [v_norm_mix_pre, v_norm_mix_post, v_norm_mlp_pre, v_norm_mlp_post, v_w_spatial, v_b_spatial, v_conv_w]
    small_grad = [g_nmp, g_nmo, g_nlp, g_nlo, g_ws, g_bs, g_cw]
    upd = _adamw("adamw_small", _pack(small_grad)[None], _pack(small_w), _pack(small_m), _pack(small_v))
    sg, sd, sm, sv = [_unpack(u, small_w) for u in upd]

    def outs(i_small, i_big):
        return (i_small[0], i_small[1], i_small[2], i_small[3], i_big["w_in_ab"], i_small[4], i_small[5], i_small[6],
                i_big["w_out_ab"], i_big["w_qkv"], i_big["w_o"], i_big["w_up"], i_big["w_down"])

    pick = lambda i: {nm: out_big[nm][i] for nm in big}
    return (loss, grad_x.reshape(x.shape), *outs(sg, pick(0)), *outs(sd, pick(1)), *outs(sm, pick(2)),
            *outs(sv, pick(3)))
```

```python
import functools

import jax
import jax.numpy as jnp
from jax import lax
from jax.experimental import pallas as pl
from jax.experimental.pallas import tpu as pltpu

F32 = jnp.float32
BF16 = jnp.bfloat16
MESH = pl.DeviceIdType.MESH
ANY = pl.BlockSpec(memory_space=pl.ANY)
HBM = pl.BlockSpec(memory_space=pltpu.HBM)
SEM = pl.BlockSpec(memory_space=pltpu.SEMAPHORE)
EFFECT = pltpu.SideEffectType.DATAFLOW_SIDE_EFFECTING

NDEV = 8
NCHIP = 4
RMS_EPS = 1e-6
LN_EPS = 1e-5
CHUNK = 128
HEAD = 128
ATT_TILE = 2048
ATT_UNROLL = 4
DILATIONS = (1, 4, 16)
ROPE_THETA = 10000.0
CONV_TAPS = 3
HALO = 8
GELU_C = 0.7978845608028654
GELU_A = 0.044715
ADAM_LR, ADAM_B1, ADAM_B2, ADAM_EPS, ADAM_WD, ADAM_STEP = 0.001, 0.9, 0.999, 1e-08, 0.01, 10
VMEM_MM = 52 << 20
VMEM_EW = 40 << 20


def _cp(sem=None, vmem=VMEM_EW):
    if sem is None:
        return pltpu.CompilerParams(vmem_limit_bytes=vmem)
    return pltpu.CompilerParams(dimension_semantics=sem, vmem_limit_bytes=vmem)


def _tile(n, want):
    return want if n % want == 0 else n


def _all_gather(name, shards):
    n = len(shards)

    def body(*refs):
        ins, outs = refs[:n], refs[n:2 * n]
        send_sems, recv_sems, local_sems = refs[2 * n:]
        x, y, c = lax.axis_index("x"), lax.axis_index("y"), lax.axis_index("c")
        me, sibling = (x, y, c), (x, y, 1 - c)
        chips = [(1 - x, y), (x, 1 - y), (1 - x, 1 - y)]

        def slot(p):
            return 4 * p[0] + 2 * p[1] + p[2]

        def copy(i, k, block, to, src=None):
            dst = outs[i].at[slot(block)]
            return pltpu.make_async_remote_copy(
                src_ref=dst if src is None else src, dst_ref=dst,
                send_sem=send_sems.at[i, k], recv_sem=recv_sems.at[i, k],
                device_id=to, device_id_type=MESH)

        mine = [pltpu.make_async_copy(ins[i], outs[i].at[slot(me)], local_sems.at[i]) for i in range(n)]
        for cp in mine:
            cp.start()
        first = []
        for i in range(n):
            first.append(copy(i, 0, me, sibling, src=ins[i]))
            for j, chip in enumerate(chips):
                first.append(copy(i, 1 + j, me, (*chip, c), src=ins[i]))
        for cp in first:
            cp.start()
        passed = []
        for j, chip in enumerate(chips):
            for i in range(n):
                copy(i, 1 + j, (*chip, c), me).wait_recv()
                fwd = copy(i, 4 + j, (*chip, c), sibling)
                fwd.start()
                passed.append(fwd)
        for i in range(n):
            copy(i, 0, sibling, me).wait_recv()
            for j, chip in enumerate(chips):
                copy(i, 4 + j, (*chip, 1 - c), me).wait_recv()
        for cp in first + passed:
            cp.wait_send()
        for cp in mine:
            cp.wait()

    return pl.pallas_call(
        body, name=name,
        out_shape=[jax.ShapeDtypeStruct((NDEV,) + s.shape, s.dtype) for s in shards],
        in_specs=[ANY] * n, out_specs=[ANY] * n,
        scratch_shapes=[pltpu.SemaphoreType.DMA((n, 7)), pltpu.SemaphoreType.DMA((n, 7)),
                        pltpu.SemaphoreType.DMA((n,))],
    )(*shards)


def _peer(x, y, c, r):
    return (1 - x if r & 4 else x, 1 - y if r & 2 else y, 1 - c if r & 1 else c)


def _slot(p):
    return 4 * p[0] + 2 * p[1] + p[2]


def _cast_fill(name, w, layer, block):
    _, rows, cols = w.shape
    tr = _tile(rows, 256)

    def body(blk_ref, w_ref, o_ref):
        o_ref[...] = w_ref[...].astype(BF16)

    return pl.pallas_call(
        body, name=name,
        grid_spec=pltpu.PrefetchScalarGridSpec(
            num_scalar_prefetch=1, grid=(rows // tr,),
            in_specs=[pl.BlockSpec((None, tr, cols), lambda i, blk: (layer, i, 0))],
            out_specs=pl.BlockSpec((None, tr, cols), lambda i, blk: (blk[0], i, 0))),
        out_shape=jax.ShapeDtypeStruct((NDEV, rows, cols), BF16),
        compiler_params=_cp(("parallel",)))(block, w)


def _ag_start(name, lands):
    n = len(lands)

    def body(*refs):
        ins, sends, recvs = refs[:n], refs[n:2 * n], refs[2 * n:3 * n]
        x, y, c = lax.axis_index("x"), lax.axis_index("y"), lax.axis_index("c")
        mine = _slot((x, y, c))
        for i in range(n):
            for r in range(1, NDEV):
                pltpu.make_async_remote_copy(
                    src_ref=ins[i].at[mine], dst_ref=ins[i].at[mine], send_sem=sends[i], recv_sem=recvs[i],
                    device_id=_peer(x, y, c, r), device_id_type=MESH).start()

    outs = pl.pallas_call(
        body, name=name,
        out_shape=[pltpu.SemaphoreType.DMA(())] * (2 * n) + [pltpu.HBM(a.shape, a.dtype) for a in lands],
        in_specs=[HBM] * n, out_specs=[SEM] * (2 * n) + [HBM] * n,
        input_output_aliases={i: 2 * n + i for i in range(n)},
        compiler_params=pltpu.CompilerParams(has_side_effects=EFFECT),
    )(*[pltpu.with_memory_space_constraint(a, pltpu.HBM) for a in lands])
    return outs[2 * n:], outs[:n], outs[n:2 * n]


def _wait_all(name, src, land, send, recv, after):
    def body(src_ref, land_ref, send_ref, recv_ref, after_ref, src_out, land_out):
        x, y, c = lax.axis_index("x"), lax.axis_index("y"), lax.axis_index("c")
        seven = land_ref.at[pl.ds(0, NDEV - 1)]
        copy = pltpu.make_async_remote_copy(src_ref=seven, dst_ref=seven, send_sem=send_ref, recv_sem=recv_ref,
                                            device_id=_peer(x, y, c, 1), device_id_type=MESH)
        copy.wait_send()
        copy.wait_recv()

    return pl.pallas_call(
        body, name=name,
        out_shape=[pltpu.HBM(src.shape, src.dtype), pltpu.HBM(land.shape, land.dtype)],
        in_specs=[HBM, HBM, SEM, SEM, ANY], out_specs=[HBM, HBM],
        input_output_aliases={0: 0, 1: 1},
        compiler_params=pltpu.CompilerParams(has_side_effects=EFFECT),
    )(src, land, send, recv, after)


def _ag_wait(name, land, send, recv, after):
    def body(land_ref, send_ref, recv_ref, after_ref, land_out):
        x, y, c = lax.axis_index("x"), lax.axis_index("y"), lax.axis_index("c")
        seven = land_ref.at[pl.ds(0, NDEV - 1)]
        copy = pltpu.make_async_remote_copy(src_ref=seven, dst_ref=seven, send_sem=send_ref, recv_sem=recv_ref,
                                            device_id=_peer(x, y, c, 1), device_id_type=MESH)
        copy.wait_send()
        copy.wait_recv()

    return pl.pallas_call(
        body, name=name, out_shape=pltpu.HBM(land.shape, land.dtype),
        in_specs=[HBM, SEM, SEM, ANY], out_specs=HBM, input_output_aliases={0: 0},
        compiler_params=pltpu.CompilerParams(has_side_effects=EFFECT),
    )(land, send, recv, after)


def _rs_start(name, grad):
    land = lax.empty((NDEV - 1,) + grad.shape[1:], grad.dtype)

    def body(g_ref, land_ref, send, recv, g_out, land_out):
        x, y, c = lax.axis_index("x"), lax.axis_index("y"), lax.axis_index("c")
        for r in range(1, NDEV):
            peer = _peer(x, y, c, r)
            pltpu.make_async_remote_copy(
                src_ref=g_ref.at[_slot(peer)], dst_ref=land_ref.at[r - 1], send_sem=send, recv_sem=recv,
                device_id=peer, device_id_type=MESH).start()

    send, recv, g_thru, land_thru = pl.pallas_call(
        body, name=name,
        out_shape=[pltpu.SemaphoreType.DMA(()), pltpu.SemaphoreType.DMA(()),
                   pltpu.HBM(grad.shape, grad.dtype), pltpu.HBM(land.shape, land.dtype)],
        in_specs=[HBM, HBM], out_specs=[SEM, SEM, HBM, HBM], input_output_aliases={0: 2, 1: 3},
        compiler_params=pltpu.CompilerParams(has_side_effects=EFFECT),
    )(pltpu.with_memory_space_constraint(grad, pltpu.HBM), pltpu.with_memory_space_constraint(land, pltpu.HBM))
    return g_thru, land_thru, send, recv


def _adam_math(w, g, m, v):
    m = ADAM_B1 * m + (1.0 - ADAM_B1) * g
    v = ADAM_B2 * v + (1.0 - ADAM_B2) * (g * g)
    m_hat = m / (1.0 - ADAM_B1 ** ADAM_STEP)
    v_hat = v / (1.0 - ADAM_B2 ** ADAM_STEP)
    delta = -ADAM_LR * (m_hat / (jnp.sqrt(v_hat) + ADAM_EPS) + ADAM_WD * w)
    return delta, m, v


def _adamw(name, parts, w, m, v):
    nparts, rows, cols = parts.shape
    tr = _tile(rows, 256)

    def body(p_ref, w_ref, m_ref, v_ref, g_out, d_out, m_out, v_out):
        g = p_ref[0].astype(F32)
        for k in range(1, nparts):
            g = g + p_ref[k].astype(F32)
        delta, mn, vn = _adam_math(w_ref[...], g, m_ref[...], v_ref[...])
        g_out[...] = g
        d_out[...] = delta
        m_out[...] = mn
        v_out[...] = vn

    row = pl.BlockSpec((tr, cols), lambda i: (i, 0))
    return pl.pallas_call(
        body, name=name, grid=(rows // tr,),
        in_specs=[pl.BlockSpec((nparts, tr, cols), lambda i: (0, i, 0)), row, row, row],
        out_specs=[row] * 4,
        out_shape=[jax.ShapeDtypeStruct((rows, cols), F32)] * 4,
        compiler_params=_cp(("parallel",)),
    )(parts, w, m, v)


def _adamw_layers(name, grads, lands, block, w, m, v):
    layers, rows, cols = w.shape
    nland = lands[0].shape[0]
    tr = rows
    while tr % 2 == 0 and tr > 8 and nland * tr * cols * 2 > (1 << 20):
        tr //= 2

    def body(blk_ref, *refs):
        own_refs, land_refs = refs[:layers], refs[layers:2 * layers]
        w_ref, m_ref, v_ref, g_out, d_out, m_out, v_out = refs[2 * layers:]
        layer = pl.program_id(0)
        for k in range(layers):
            @pl.when(layer == k)
            def _(k=k):
                g = own_refs[k][...].astype(F32)
                for s in range(nland):
                    g = g + land_refs[k][s].astype(F32)
                delta, mn, vn = _adam_math(w_ref[...], g, m_ref[...], v_ref[...])
                g_out[...] = g
                d_out[...] = delta
                m_out[...] = mn
                v_out[...] = vn

    def own_spec(k):
        return pl.BlockSpec((None, tr, cols), lambda l, i, blk: (blk[0], jnp.where(l == k, i, 0), 0))

    def land_spec(k):
        return pl.BlockSpec((nland, tr, cols), lambda l, i, blk: (0, jnp.where(l == k, i, 0), 0))

    row = pl.BlockSpec((None, tr, cols), lambda l, i, blk: (l, i, 0))
    return pl.pallas_call(
        body, name=name,
        grid_spec=pltpu.PrefetchScalarGridSpec(
            num_scalar_prefetch=1, grid=(layers, rows // tr),
            in_specs=[own_spec(k) for k in range(layers)] + [land_spec(k) for k in range(layers)] + [row, row, row],
            out_specs=[row] * 4),
        out_shape=[jax.ShapeDtypeStruct((layers, rows, cols), F32)] * 4,
        compiler_params=_cp(("arbitrary", "arbitrary")),
    )(block, *grads, *lands, w, m, v)


def _norm_fwd(name, h, z=None, g_post=None, g_pre=None):
    rows, d = h.shape
    tm = _tile(rows, 256)
    has_post, has_pre = z is not None, g_pre is not None

    def body(*refs):
        it = iter(refs)
        hv = next(it)[...]
        if has_post:
            zv, gp = next(it)[...], next(it)[...]
        if has_pre:
            gq = next(it)[...]
        if has_post:
            r = lax.rsqrt(jnp.mean(zv * zv, axis=-1, keepdims=True) + RMS_EPS)
            hv = hv + (zv * r) * gp
            next(it)[...] = hv
        if has_pre:
            r = lax.rsqrt(jnp.mean(hv * hv, axis=-1, keepdims=True) + RMS_EPS)
            next(it)[...] = ((hv * r) * gq).astype(BF16)

    row = pl.BlockSpec((tm, d), lambda i: (i, 0))
    vec = pl.BlockSpec((1, d), lambda i: (0, 0))
    ins, in_specs, out_shape, out_specs = [h], [row], [], []
    if has_post:
        ins += [z, g_post]
        in_specs += [row, vec]
        out_shape.append(jax.ShapeDtypeStruct((rows, d), F32))
        out_specs.append(row)
    if has_pre:
        ins.append(g_pre)
        in_specs.append(vec)
        out_shape.append(jax.ShapeDtypeStruct((rows, d), BF16))
        out_specs.append(row)
    return pl.pallas_call(body, name=name, grid=(rows // tm,), in_specs=in_specs, out_specs=out_specs,
                          out_shape=out_shape, compiler_params=_cp(("parallel",)))(*ins)


def _rms_bwd_rows(x, g, dy):
    r = lax.rsqrt(jnp.mean(x * x, axis=-1, keepdims=True) + RMS_EPS)
    xn = x * r
    dg = jnp.sum(dy * xn, axis=0, keepdims=True)
    dxn = dy * g
    dx = r * (dxn - xn * jnp.mean(dxn * xn, axis=-1, keepdims=True))
    return dx, dg


def _norm_bwd(name, d_out, pre=None, post=None):
    rows, d = d_out.shape
    tm = _tile(rows, 256)
    has_pre, has_post = pre is not None, post is not None

    def body(*refs):
        it = iter(refs)
        dres = next(it)[...]
        if has_pre:
            dy, xp, gq = next(it)[...], next(it)[...], next(it)[...]
        if has_post:
            zv, gp = next(it)[...], next(it)[...]
        first = pl.program_id(0) == 0
        if has_pre:
            dx, dg = _rms_bwd_rows(xp, gq, dy)
            dres = dres + dx
            next(it)[...] = dres
            dg_ref = next(it)

            @pl.when(first)
            def _():
                dg_ref[...] = jnp.zeros_like(dg_ref)
            dg_ref[...] += dg
        if has_post:
            dz, dg2 = _rms_bwd_rows(zv, gp, dres)
            next(it)[...] = dz.astype(BF16)
            dg2_ref = next(it)

            @pl.when(first)
            def _():
                dg2_ref[...] = jnp.zeros_like(dg2_ref)
            dg2_ref[...] += dg2

    row = pl.BlockSpec((tm, d), lambda i: (i, 0))
    vec = pl.BlockSpec((1, d), lambda i: (0, 0))
    ins, in_specs, out_shape, out_specs = [d_out], [row], [], []
    if has_pre:
        ins += list(pre)
        in_specs += [row, row, vec]
        out_shape += [jax.ShapeDtypeStruct((rows, d), F32), jax.ShapeDtypeStruct((1, d), F32)]
        out_specs += [row, vec]
    if has_post:
        ins += list(post)
        in_specs += [row, vec]
        out_shape += [jax.ShapeDtypeStruct((rows, d), BF16), jax.ShapeDtypeStruct((1, d), F32)]
        out_specs += [row, vec]
    return pl.pallas_call(body, name=name, grid=(rows // tm,), in_specs=in_specs, out_specs=out_specs,
                          out_shape=out_shape, compiler_params=_cp(("arbitrary",)))(*ins)


def _loss_grad(name, y, target):
    rows, d = y.shape
    tm = _tile(rows, 256)

    def body(y_ref, t_ref, dy_ref, loss_ref):
        err = y_ref[...] - t_ref[...]
        dy_ref[...] = err * (1.0 / d)

        @pl.when(pl.program_id(0) == 0)
        def _():
            loss_ref[...] = jnp.zeros_like(loss_ref)
        loss_ref[...] += jnp.full(loss_ref.shape, (0.5 / d) * jnp.sum(err * err), F32)

    row = pl.BlockSpec((tm, d), lambda i: (i, 0))
    return pl.pallas_call(
        body, name=name, grid=(rows // tm,), in_specs=[row, row],
        out_specs=[row, pl.BlockSpec((1, 128), lambda i: (0, 0))],
        out_shape=[jax.ShapeDtypeStruct((rows, d), F32), jax.ShapeDtypeStruct((1, 128), F32)],
        compiler_params=_cp(("arbitrary",)))(y, target)


NT_DIMS = (((1,), (1,)), ((), ()))
TN_DIMS = (((0,), (0,)), ((), ()))


def _mm_nn_blk(name, a, wblk, relu2=False):
    m, k = a.shape
    nb = wblk.shape[2]
    tm = _tile(m, 512)

    def body(a_ref, w_ref, *outs):
        r = jnp.dot(a_ref[...], w_ref[...], preferred_element_type=F32)
        outs[0][...] = r
        if relu2:
            rr = jnp.maximum(r, 0.0)
            outs[1][...] = (rr * rr).astype(BF16)

    out_shape = [jax.ShapeDtypeStruct((m, NDEV * nb), F32)]
    if relu2:
        out_shape.append(jax.ShapeDtypeStruct((m, NDEV * nb), BF16))
    return pl.pallas_call(
        body, name=name, grid=(NDEV, m // tm),
        in_specs=[pl.BlockSpec((tm, k), lambda d, i: (i, 0)), pl.BlockSpec((None, k, nb), lambda d, i: (d, 0, 0))],
        out_specs=[pl.BlockSpec((tm, nb), lambda d, i: (i, d))] * len(out_shape),
        out_shape=out_shape, compiler_params=_cp(("parallel", "parallel"), VMEM_MM))(a, wblk)


def _mm_nn(name, a, w):
    m, kb = a.shape
    n = w.shape[1]
    tm, tk = _tile(m, 512), _tile(kb, 2048)

    def body(a_ref, w_ref, o_ref):
        r = jnp.dot(a_ref[...], w_ref[...], preferred_element_type=F32)
        step = pl.program_id(1)

        @pl.when(step == 0)
        def _():
            o_ref[...] = r

        @pl.when(step > 0)
        def _():
            o_ref[...] += r

    return pl.pallas_call(
        body, name=name, grid=(m // tm, kb // tk),
        in_specs=[pl.BlockSpec((tm, tk), lambda i, s: (i, s)), pl.BlockSpec((tk, n), lambda i, s: (s, 0))],
        out_specs=pl.BlockSpec((tm, n), lambda i, s: (i, 0)),
        out_shape=jax.ShapeDtypeStruct((m, n), F32),
        compiler_params=_cp(("parallel", "arbitrary"), VMEM_MM))(a, w)


def _mm_nt_rows(name, dy, w, up=None):
    m, n = dy.shape
    kw = w.shape[0]
    tm, tkw = _tile(m, 512), _tile(kw, 1024)

    def body(dy_ref, w_ref, *rest):
        r = lax.dot_general(dy_ref[...], w_ref[...], NT_DIMS, preferred_element_type=F32)
        if up is None:
            rest[0][...] = r
        else:
            rest[1][...] = (r * (2.0 * jnp.maximum(rest[0][...], 0.0))).astype(BF16)

    ins = [dy, w]
    in_specs = [pl.BlockSpec((tm, n), lambda j, i: (i, 0)), pl.BlockSpec((tkw, n), lambda j, i: (j, 0))]
    if up is not None:
        ins.append(up)
        in_specs.append(pl.BlockSpec((tm, tkw), lambda j, i: (i, j)))
    return pl.pallas_call(
        body, name=name, grid=(kw // tkw, m // tm), in_specs=in_specs,
        out_specs=pl.BlockSpec((tm, tkw), lambda j, i: (i, j)),
        out_shape=jax.ShapeDtypeStruct((m, kw), F32 if up is None else BF16),
        compiler_params=_cp(("parallel", "parallel"), VMEM_MM))(*ins)


def _mm_nt_blk(name, dy, wblk, after=None):
    m = dy.shape[0]
    _, kw, nb = wblk.shape
    tm, per = _tile(m, 512), 2

    def body(dy_ref, w_ref, *rest):
        o_ref = rest[-1]
        r = lax.dot_general(dy_ref[:, :nb], w_ref[0], NT_DIMS, preferred_element_type=F32)
        for t in range(1, per):
            r = r + lax.dot_general(dy_ref[:, t * nb:(t + 1) * nb], w_ref[t], NT_DIMS, preferred_element_type=F32)
        step = pl.program_id(1)

        @pl.when(step == 0)
        def _():
            o_ref[...] = r

        @pl.when(step > 0)
        def _():
            o_ref[...] += r

    extra = [] if after is None else [after]
    return pl.pallas_call(
        body, name=name, grid=(m // tm, NDEV // per),
        in_specs=[pl.BlockSpec((tm, per * nb), lambda i, s: (i, s)),
                  pl.BlockSpec((per, kw, nb), lambda i, s: (s, 0, 0))] + [ANY] * len(extra),
        out_specs=pl.BlockSpec((tm, kw), lambda i, s: (i, 0)),
        out_shape=jax.ShapeDtypeStruct((m, kw), F32),
        compiler_params=_cp(("parallel", "arbitrary"), VMEM_MM))(dy, wblk, *extra)


def _mm_tn(name, x, dy, nb=None):
    t, mx = x.shape
    n = dy.shape[1]
    tmx = _tile(mx, 512)
    tn = nb if nb is not None else _tile(n, 1024)

    def body(x_ref, dy_ref, o_ref):
        o_ref[...] = lax.dot_general(x_ref[...], dy_ref[...], TN_DIMS, preferred_element_type=F32).astype(BF16)

    if nb is None:
        out_shape = jax.ShapeDtypeStruct((mx, n), BF16)
        out_spec = pl.BlockSpec((tmx, tn), lambda j, i: (i, j))
    else:
        out_shape = jax.ShapeDtypeStruct((NDEV, mx, nb), BF16)
        out_spec = pl.BlockSpec((None, tmx, nb), lambda j, i: (j, i, 0))
    return pl.pallas_call(
        body, name=name, grid=(n // tn, mx // tmx),
        in_specs=[pl.BlockSpec((t, tmx), lambda j, i: (0, i)), pl.BlockSpec((t, tn), lambda j, i: (0, j))],
        out_specs=out_spec, out_shape=out_shape,
        compiler_params=_cp(("parallel", "parallel"), VMEM_MM))(x, dy)


def _gelu(x):
    return 0.5 * x * (1.0 + jnp.tanh(GELU_C * (x + GELU_A * (x * x * x))))


def _gelu_grad(x):
    t = jnp.tanh(GELU_C * (x + GELU_A * (x * x * x)))
    return 0.5 * (1.0 + t) + 0.5 * x * (1.0 - t * t) * (GELU_C * (1.0 + 3.0 * GELU_A * (x * x)))


def _layernorm(a):
    mu = jnp.mean(a, axis=-1, keepdims=True)
    ac = a - mu
    rstd = lax.rsqrt(jnp.mean(ac * ac, axis=-1, keepdims=True) + LN_EPS)
    return ac * rstd, rstd


def _shift_rows(z, halo, k):
    zr = pltpu.roll(z, k, 0)
    hr = pltpu.roll(halo, k, 0)
    row = lax.broadcasted_iota(jnp.int32, hr.shape, 0)
    top = jnp.where(row < k, hr, zr[:HALO])
    return jnp.concatenate([top, zr[HALO:]], axis=0)


def _shift_rows_up(z, halo, k):
    rows = z.shape[0]
    zr = pltpu.roll(z, rows - k, 0)
    hr = pltpu.roll(halo, HALO - k, 0)
    row = lax.broadcasted_iota(jnp.int32, hr.shape, 0)
    bot = jnp.where(row >= HALO - k, hr, zr[rows - HALO:])
    return jnp.concatenate([zr[:rows - HALO], bot], axis=0)


def _causal_mask():
    t = lax.broadcasted_iota(jnp.int32, (CHUNK, CHUNK), 0)
    s = lax.broadcasted_iota(jnp.int32, (CHUNK, CHUNK), 1)
    return s <= t


def _gate_specs(tm, rows, width):
    per = tm // HALO
    last = rows // HALO - 1
    cur = pl.BlockSpec((tm, width), lambda i: (i, 0))
    prev = pl.BlockSpec((HALO, width), lambda i: (jnp.maximum(i * per - 1, 0), 0))
    nxt = pl.BlockSpec((HALO, width), lambda i: (jnp.minimum((i + 1) * per, last), 0))
    return cur, prev, nxt


def _gate_fwd(name, proj, w_s, b_st, cw):
    rows, width = proj.shape
    w = width // 5
    groups = w // CHUNK
    tm = _tile(rows, 256)
    cur, prev, _ = _gate_specs(tm, rows, width)

    def body(p_ref, h_ref, ws_ref, b_ref, cw_ref, o_ref):
        mask = _causal_mask()
        au = _gelu(p_ref[:, 0:w])
        vn, _ = _layernorm(_gelu(p_ref[:, w:2 * w]))
        vn = vn.astype(BF16)
        for g in range(groups):
            wc = jnp.where(mask, ws_ref[g], 0.0).astype(BF16)
            cols = slice(g * CHUNK, (g + 1) * CHUNK)
            for ch in range(tm // CHUNK):
                rws = slice(ch * CHUNK, (ch + 1) * CHUNK)
                mixed = jnp.dot(wc, vn[rws, cols], preferred_element_type=F32) + b_ref[:, g:g + 1]
                o_ref[rws, cols] = (au[rws, cols] * mixed).astype(BF16)
        z = p_ref[:, 3 * w:4 * w] * p_ref[:, 4 * w:5 * w]
        zh = h_ref[:, 3 * w:4 * w] * h_ref[:, 4 * w:5 * w]
        zh = jnp.where(pl.program_id(0) == 0, 0.0, zh)
        y = cw_ref[0:1, :] * _shift_rows(z, zh, 2) + cw_ref[1:2, :] * _shift_rows(z, zh, 1) + cw_ref[2:3, :] * z
        o_ref[:, w:2 * w] = (p_ref[:, 2 * w:3 * w] * y).astype(BF16)

    full = lambda a: pl.BlockSpec(a.shape, lambda i: (0,) * a.ndim)
    return pl.pallas_call(
        body, name=name, grid=(rows // tm,),
        in_specs=[cur, prev, full(w_s), full(b_st), full(cw)],
        out_specs=pl.BlockSpec((tm, 2 * w), lambda i: (i, 0)),
        out_shape=jax.ShapeDtypeStruct((rows, 2 * w), BF16),
        compiler_params=_cp(("parallel",)))(proj, proj, w_s, b_st, cw)


def _gate_bwd(name, proj, d_ab, w_s, b_st, cw):
    rows, width = proj.shape
    w = width // 5
    groups = w // CHUNK
    tm = _tile(rows, 256)
    cur, prev, nxt = _gate_specs(tm, rows, width)
    dcur, _, dnxt = _gate_specs(tm, rows, 2 * w)

    def body(p_ref, ph_ref, pn_ref, d_ref, dn_ref, ws_ref, b_ref, cw_ref, o_ref, dws_ref, dbs_ref, dcw_ref):
        i = pl.program_id(0)

        @pl.when(i == 0)
        def _():
            dws_ref[...] = jnp.zeros_like(dws_ref)
            dbs_ref[...] = jnp.zeros_like(dbs_ref)
            dcw_ref[...] = jnp.zeros_like(dcw_ref)

        mask = _causal_mask()
        u, v = p_ref[:, 0:w], p_ref[:, w:2 * w]
        au, av = _gelu(u), _gelu(v)
        vn, rstd = _layernorm(av)
        vnb = vn.astype(BF16)
        d_a = d_ref[:, 0:w]
        d_mixed = (d_a * au).astype(BF16)
        ones = jnp.ones((HALO, CHUNK), BF16)
        d_vn_cols = []
        d_au_cols = []
        for g in range(groups):
            wc = jnp.where(mask, ws_ref[g], 0.0).astype(BF16)
            cols = slice(g * CHUNK, (g + 1) * CHUNK)
            dw = jnp.zeros((CHUNK, CHUNK), F32)
            db = jnp.zeros((HALO, CHUNK), F32)
            d_vn_rows, d_au_rows = [], []
            for ch in range(tm // CHUNK):
                rws = slice(ch * CHUNK, (ch + 1) * CHUNK)
                mixed = jnp.dot(wc, vnb[rws, cols], preferred_element_type=F32) + b_ref[:, g:g + 1]
                d_au_rows.append(d_a[rws, cols] * mixed)
                dm = d_mixed[rws, cols]
                dw = dw + lax.dot_general(dm, vnb[rws, cols], NT_DIMS, preferred_element_type=F32)
                db = db + lax.dot_general(ones, dm, NT_DIMS, preferred_element_type=F32)
                d_vn_rows.append(lax.dot_general(wc, dm, TN_DIMS, preferred_element_type=F32))
            dws_ref[g] += jnp.where(mask, dw, 0.0)
            dbs_ref[g:g + 1, :] += db[0:1, :]
            d_vn_cols.append(jnp.concatenate(d_vn_rows, axis=0))
            d_au_cols.append(jnp.concatenate(d_au_rows, axis=0))
        d_vn = jnp.concatenate(d_vn_cols, axis=1)
        d_au = jnp.concatenate(d_au_cols, axis=1)
        d_av = rstd * (d_vn - jnp.mean(d_vn, axis=-1, keepdims=True)
                       - vn * jnp.mean(d_vn * vn, axis=-1, keepdims=True))
        o_ref[:, 0:w] = (d_au * _gelu_grad(u)).astype(BF16)
        o_ref[:, w:2 * w] = (d_av * _gelu_grad(v)).astype(BF16)

        gb, gc, bx = p_ref[:, 2 * w:3 * w], p_ref[:, 3 * w:4 * w], p_ref[:, 4 * w:5 * w]
        z = gc * bx
        zh = jnp.where(i == 0, 0.0, ph_ref[:, 3 * w:4 * w] * ph_ref[:, 4 * w:5 * w])
        z1, z2 = _shift_rows(z, zh, 1), _shift_rows(z, zh, 2)
        d_b = d_ref[:, w:2 * w]
        y = cw_ref[0:1, :] * z2 + cw_ref[1:2, :] * z1 + cw_ref[2:3, :] * z
        dy = d_b * gb
        dyn = jnp.where(i == pl.num_programs(0) - 1, 0.0, dn_ref[:, w:2 * w] * pn_ref[:, 2 * w:3 * w])
        dz = (cw_ref[2:3, :] * dy + cw_ref[1:2, :] * _shift_rows_up(dy, dyn, 1)
              + cw_ref[0:1, :] * _shift_rows_up(dy, dyn, 2))
        dcw_ref[0:1, :] += jnp.sum(dy * z2, axis=0, keepdims=True)
        dcw_ref[1:2, :] += jnp.sum(dy * z1, axis=0, keepdims=True)
        dcw_ref[2:3, :] += jnp.sum(dy * z, axis=0, keepdims=True)
        o_ref[:, 2 * w:3 * w] = (d_b * y).astype(BF16)
        o_ref[:, 3 * w:4 * w] = (dz * bx).astype(BF16)
        o_ref[:, 4 * w:5 * w] = (dz * gc).astype(BF16)

    full = lambda a: pl.BlockSpec(a.shape, lambda i: (0,) * a.ndim)
    acc = lambda shape: pl.BlockSpec(shape, lambda i: (0,) * len(shape))
    return pl.pallas_call(
        body, name=name, grid=(rows // tm,),
        in_specs=[cur, prev, nxt, dcur, dnxt, full(w_s), full(b_st), full(cw)],
        out_specs=[pl.BlockSpec((tm, width), lambda i: (i, 0)), acc((groups, CHUNK, CHUNK)),
                   acc((groups, CHUNK)), acc((HALO, w))],
        out_shape=[jax.ShapeDtypeStruct((rows, width), BF16), jax.ShapeDtypeStruct((groups, CHUNK, CHUNK), F32),
                   jax.ShapeDtypeStruct((groups, CHUNK), F32), jax.ShapeDtypeStruct((HALO, w), F32)],
        compiler_params=_cp(("arbitrary",), VMEM_MM))(proj, proj, proj, d_ab, d_ab, w_s, b_st, cw)


def _rope(t, cosf, sins):
    return t * cosf + pltpu.roll(t, HEAD // 2, 1) * sins


def _rope_bwd(dt, cosf, sins):
    return dt * cosf + pltpu.roll(dt * sins, HEAD // 2, 1)


def _attn_units(visit):
    for b, d in enumerate(DILATIONS):
        blocks = ATT_TILE // (CHUNK * d)
        for r in range(d):
            if blocks <= ATT_UNROLL:
                for j in range(blocks):
                    visit(b, d, r, j)
            else:
                def step(jj, carry, b=b, d=d, r=r):
                    for u in range(ATT_UNROLL):
                        visit(b, d, r, jj * ATT_UNROLL + u)
                    return carry
                lax.fori_loop(0, blocks // ATT_UNROLL, step, 0)


def _unit_rows(ref, d, r, j, nblk, offset=0):
    base = offset + j * (CHUNK * d)
    if not isinstance(base, int):
        base = pl.multiple_of(base, CHUNK)
    return ref.at[pl.ds(base, nblk * CHUNK * d)], pl.ds(r, nblk * CHUNK, stride=d)


def _band_bias(bias):
    qi = lax.broadcasted_iota(jnp.int32, (CHUNK, 2 * CHUNK), 0)
    ki = lax.broadcasted_iota(jnp.int32, (CHUNK, 2 * CHUNK), 1)
    band = (ki >= qi) & (ki <= qi + CHUNK)
    bias[0] = jnp.where(band, 0.0, -jnp.inf)
    bias[1] = jnp.where(band & (ki >= CHUNK), 0.0, -jnp.inf)


def _unit_bias(bias, n, j):
    if isinstance(j, int) and j != 0:
        return bias[0]
    return bias[jnp.where(jnp.logical_and(n == 0, j == 0), 1, 0)]


def _attn_in_specs(heads):
    blk = (ATT_TILE, HEAD)
    prev = lambda n: jnp.maximum(n - 1, 0)
    return [
        pl.BlockSpec(blk, lambda h, n: (n, h)),
        pl.BlockSpec(blk, lambda h, n: (n, heads + h)),
        pl.BlockSpec(blk, lambda h, n: (prev(n), heads + h)),
        pl.BlockSpec(blk, lambda h, n: (n, 2 * heads + h)),
        pl.BlockSpec(blk, lambda h, n: (prev(n), 2 * heads + h)),
        pl.BlockSpec(blk, lambda h, n: (n, 0)),
        pl.BlockSpec(blk, lambda h, n: (n, 0)),
        pl.BlockSpec(blk, lambda h, n: (prev(n), 0)),
        pl.BlockSpec(blk, lambda h, n: (prev(n), 0)),
    ]


def _attn_load(q_ref, kc_ref, kp_ref, vc_ref, vp_ref, cc_ref, sc_ref, cp_ref, sp_ref, qr, kcat, vcat):
    qr[...] = _rope(q_ref[...], cc_ref[...], sc_ref[...])
    kcat[pl.ds(0, ATT_TILE), :] = _rope(kp_ref[...], cp_ref[...], sp_ref[...])
    kcat[pl.ds(ATT_TILE, ATT_TILE), :] = _rope(kc_ref[...], cc_ref[...], sc_ref[...])
    vcat[pl.ds(0, ATT_TILE), :] = vp_ref[...]
    vcat[pl.ds(ATT_TILE, ATT_TILE), :] = vc_ref[...]


def _attn_fwd(name, qkv, cosf, sins):
    t = qkv.shape[0]
    heads = qkv.shape[1] // (3 * HEAD)
    scale = HEAD ** -0.5
    nbr = len(DILATIONS)

    def body(q_ref, kc_ref, kp_ref, vc_ref, vp_ref, cc_ref, sc_ref, cp_ref, sp_ref, o_ref, lse_ref,
             qr, kcat, vcat, obr, lbr, bias):
        n = pl.program_id(1)
        _attn_load(q_ref, kc_ref, kp_ref, vc_ref, vp_ref, cc_ref, sc_ref, cp_ref, sp_ref, qr, kcat, vcat)
        _band_bias(bias)

        def visit(b, d, r, j):
            qv, qs = _unit_rows(qr, d, r, j, 1)
            kv, ks = _unit_rows(kcat, d, r, j, 2, ATT_TILE - CHUNK * d)
            vv, _ = _unit_rows(vcat, d, r, j, 2, ATT_TILE - CHUNK * d)
            s = lax.dot_general(qv[qs, :].astype(BF16), kv[ks, :].astype(BF16), NT_DIMS,
                                preferred_element_type=F32) * scale + _unit_bias(bias, n, j)
            mx = jnp.max(s, axis=-1, keepdims=True)
            p = jnp.exp(s - mx)
            den = jnp.sum(p, axis=-1, keepdims=True)
            o = jnp.dot((p * (1.0 / den)).astype(BF16), vv[ks, :].astype(BF16), preferred_element_type=F32)
            ov, _ = _unit_rows(obr.at[b], d, r, j, 1)
            lv, _ = _unit_rows(lbr.at[b], d, r, j, 1)
            ov[qs, :] = o
            lv[qs, :] = jnp.broadcast_to(mx + jnp.log(den), (CHUNK, HEAD))

        _attn_units(visit)
        ls = [lbr[b] for b in range(nbr)]
        top = functools.reduce(jnp.maximum, ls)
        ws = [jnp.exp(l - top) for l in ls]
        tot = functools.reduce(jnp.add, ws)
        o = (ws[0] / tot) * obr[0]
        for b in range(1, nbr):
            o = o + (ws[b] / tot) * obr[b]
        o_ref[...] = o.astype(BF16)
        lse_ref[...] = top + jnp.log(tot)

    blk = (ATT_TILE, HEAD)
    tile = pl.BlockSpec(blk, lambda h, n: (n, h))
    return pl.pallas_call(
        body, name=name, grid=(heads, t // ATT_TILE), in_specs=_attn_in_specs(heads),
        out_specs=[tile, tile],
        out_shape=[jax.ShapeDtypeStruct((t, heads * HEAD), BF16), jax.ShapeDtypeStruct((t, heads * HEAD), F32)],
        scratch_shapes=[pltpu.VMEM(blk, F32), pltpu.VMEM((2 * ATT_TILE, HEAD), F32), pltpu.VMEM((2 * ATT_TILE, HEAD), F32),
                        pltpu.VMEM((nbr,) + blk, F32), pltpu.VMEM((nbr,) + blk, F32),
                        pltpu.VMEM((2, CHUNK, 2 * CHUNK), F32)],
        compiler_params=_cp(("parallel", "parallel"), VMEM_MM),
    )(qkv, qkv, qkv, qkv, qkv, cosf, sins, cosf, sins)


def _attn_bwd(name, qkv, cosf, sins, d_o, o, lse):
    t = qkv.shape[0]
    heads = qkv.shape[1] // (3 * HEAD)
    scale = HEAD ** -0.5

    def body(q_ref, kc_ref, kp_ref, vc_ref, vp_ref, cc_ref, sc_ref, cp_ref, sp_ref, do_ref, o_ref, lse_ref,
             dq_ref, dko_ref, dkp_ref, dvo_ref, dvp_ref, qr, kcat, vcat, dq_acc, dk_acc, dv_acc, delta, bias):
        n = pl.program_id(1)
        _attn_load(q_ref, kc_ref, kp_ref, vc_ref, vp_ref, cc_ref, sc_ref, cp_ref, sp_ref, qr, kcat, vcat)
        _band_bias(bias)
        dq_acc[...] = jnp.zeros_like(dq_acc)
        dk_acc[...] = jnp.zeros_like(dk_acc)
        dv_acc[...] = jnp.zeros_like(dv_acc)
        delta[...] = jnp.broadcast_to(
            jnp.sum(do_ref[...] * o_ref[...].astype(F32), axis=-1, keepdims=True), delta.shape)

        def visit(b, d, r, j):
            qv, qs = _unit_rows(qr, d, r, j, 1)
            kv, ks = _unit_rows(kcat, d, r, j, 2, ATT_TILE - CHUNK * d)
            vv, _ = _unit_rows(vcat, d, r, j, 2, ATT_TILE - CHUNK * d)
            dov, _ = _unit_rows(do_ref, d, r, j, 1)
            lv, _ = _unit_rows(lse_ref, d, r, j, 1)
            dlv, _ = _unit_rows(delta, d, r, j, 1)
            q, k = qv[qs, :].astype(BF16), kv[ks, :].astype(BF16)
            do = dov[qs, :].astype(BF16)
            s = lax.dot_general(q, k, NT_DIMS, preferred_element_type=F32) * scale + _unit_bias(bias, n, j)
            p = jnp.exp(s - lv[qs, :][:, 0:1])
            dp = lax.dot_general(do, vv[ks, :].astype(BF16), NT_DIMS, preferred_element_type=F32)
            ds = (p * (dp - dlv[qs, :][:, 0:1]) * scale).astype(BF16)
            dqv, _ = _unit_rows(dq_acc, d, r, j, 1)
            dkv, _ = _unit_rows(dk_acc, d, r, j, 2, ATT_TILE - CHUNK * d)
            dvv, _ = _unit_rows(dv_acc, d, r, j, 2, ATT_TILE - CHUNK * d)
            dqv[qs, :] += jnp.dot(ds, k, preferred_element_type=F32)
            dkv[ks, :] += lax.dot_general(ds, q, TN_DIMS, preferred_element_type=F32)
            dvv[ks, :] += lax.dot_general(p.astype(BF16), do, TN_DIMS, preferred_element_type=F32)

        _attn_units(visit)
        dq_ref[...] = _rope_bwd(dq_acc[...], cc_ref[...], sc_ref[...])
        dkp_ref[...] = _rope_bwd(dk_acc[pl.ds(0, ATT_TILE), :], cp_ref[...], sp_ref[...])
        dko_ref[...] = _rope_bwd(dk_acc[pl.ds(ATT_TILE, ATT_TILE), :], cc_ref[...], sc_ref[...])
        dvp_ref[...] = dv_acc[pl.ds(0, ATT_TILE), :]
        dvo_ref[...] = dv_acc[pl.ds(ATT_TILE, ATT_TILE), :]

    blk = (ATT_TILE, HEAD)
    tile = pl.BlockSpec(blk, lambda h, n: (n, h))
    big = pltpu.VMEM((2 * ATT_TILE, HEAD), F32)
    return pl.pallas_call(
        body, name=name, grid=(heads, t // ATT_TILE), in_specs=_attn_in_specs(heads) + [tile, tile, tile],
        out_specs=[tile] * 5,
        out_shape=[jax.ShapeDtypeStruct((t, heads * HEAD), F32)] * 5,
        scratch_shapes=[pltpu.VMEM(blk, F32), big, big, pltpu.VMEM(blk, F32), big, big, pltpu.VMEM(blk, F32),
                        pltpu.VMEM((2, CHUNK, 2 * CHUNK), F32)],
        compiler_params=_cp(("parallel", "parallel"), 60 << 20),
    )(qkv, qkv, qkv, qkv, qkv, cosf, sins, cosf, sins, d_o, o, lse)


def _attn_merge(name, dq, dk_own, dk_prev, dv_own, dv_prev):
    t, hd = dq.shape
    nt = t // ATT_TILE
    tw = _tile(hd, 512)

    def body(dq_ref, dko_ref, dkn_ref, dvo_ref, dvn_ref, o_ref):
        last = pl.program_id(0) == nt - 1
        part = pl.program_id(1)

        @pl.when(part == 0)
        def _():
            o_ref[...] = dq_ref[...].astype(BF16)

        @pl.when(part == 1)
        def _():
            o_ref[...] = (dko_ref[...] + jnp.where(last, 0.0, dkn_ref[...])).astype(BF16)

        @pl.when(part == 2)
        def _():
            o_ref[...] = (dvo_ref[...] + jnp.where(last, 0.0, dvn_ref[...])).astype(BF16)

    def own(part):
        return pl.BlockSpec((ATT_TILE, tw), lambda n, p, c: (jnp.where(p == part, n, 0), jnp.where(p == part, c, 0)))

    def nxt(part):
        return pl.BlockSpec((ATT_TILE, tw), lambda n, p, c: (jnp.where(p == part, jnp.minimum(n + 1, nt - 1), 0),
                                                            jnp.where(p == part, c, 0)))

    per = hd // tw
    return pl.pallas_call(
        body, name=name, grid=(nt, 3, per), in_specs=[own(0), own(1), nxt(1), own(2), nxt(2)],
        out_specs=pl.BlockSpec((ATT_TILE, tw), lambda n, p, c: (n, p * per + c)),
        out_shape=jax.ShapeDtypeStruct((t, 3 * hd), BF16),
        compiler_params=_cp(("parallel", "parallel", "parallel"), VMEM_MM))(dq, dk_own, dk_prev, dv_own, dv_prev)


def _sum_parts(name, parts):
    nparts, rows, cols = parts.shape
    tr = _tile(rows, 256)

    def body(p_ref, o_ref):
        s = p_ref[0]
        for k in range(1, nparts):
            s = s + p_ref[k]
        o_ref[...] = s

    return pl.pallas_call(
        body, name=name, grid=(rows // tr,),
        in_specs=[pl.BlockSpec((nparts, tr, cols), lambda i: (0, i, 0))],
        out_specs=pl.BlockSpec((tr, cols), lambda i: (i, 0)),
        out_shape=jax.ShapeDtypeStruct((rows, cols), F32),
        compiler_params=_cp(("parallel",)))(parts)


def _rows128(a, pad_to=8):
    flat = a.reshape(-1)
    rows = -(-flat.shape[0] // 128)
    rows = -(-rows // pad_to) * pad_to
    flat = jnp.pad(flat, (0, rows * 128 - flat.shape[0]))
    return flat.reshape(rows, 128)


def _pack(arrays):
    return jnp.concatenate([_rows128(a) for a in arrays], axis=0)


def _unpack(packed, like):
    out, at = [], 0
    for a in like:
        size = 1
        for s in a.shape:
            size *= s
        rows = -(-(-(-size // 128)) // 8) * 8
        out.append(packed[at:at + rows].reshape(-1)[:size].reshape(a.shape))
        at += rows
    return out


def kernel(x, norm_mix_pre, norm_mix_post, norm_mlp_pre, norm_mlp_post, w_in_ab, w_spatial, b_spatial, conv_w, w_out_ab, w_qkv, w_o, w_up, w_down, loss_target, m_norm_mix_pre, m_norm_mix_post, m_norm_mlp_pre, m_norm_mlp_post, m_w_in_ab, m_w_spatial, m_b_spatial, m_conv_w, m_w_out_ab, m_w_qkv, m_w_o, m_w_up, m_w_down, v_norm_mix_pre, v_norm_mix_post, v_norm_mlp_pre, v_norm_mlp_post, v_w_in_ab, v_w_spatial, v_b_spatial, v_conv_w, v_w_out_ab, v_w_qkv, v_w_o, v_w_up, v_w_down):
    depth = norm_mix_pre.shape[0]
    seq, dm = x.shape[1], x.shape[2]
    h0 = x.reshape(seq, dm)
    target = loss_target.reshape(seq, dm)
    ax, ay, ac = lax.axis_index("x"), lax.axis_index("y"), lax.axis_index("c")
    my_block = 4 * ax + 2 * ay + ac
    block = jnp.reshape(my_block, (1,)).astype(jnp.int32)

    half = HEAD // 2
    inv_freq = ROPE_THETA ** (-jnp.arange(half, dtype=F32) * 2.0 / HEAD)
    ang = jnp.arange(seq, dtype=jnp.int32).astype(F32)[:, None] * inv_freq[None, :]
    cosf = jnp.concatenate([jnp.cos(ang), jnp.cos(ang)], axis=-1)
    sins = jnp.concatenate([-jnp.sin(ang), jnp.sin(ang)], axis=-1)

    big = {"w_in_ab": w_in_ab, "w_out_ab": w_out_ab, "w_qkv": w_qkv, "w_o": w_o, "w_up": w_up, "w_down": w_down}
    use_order = []
    for l in range(depth):
        use_order += [("w_in_ab", l // 2), ("w_out_ab", l // 2)] if l % 2 == 0 else [("w_qkv", l // 2), ("w_o", l // 2)]
        use_order += [("w_up", l), ("w_down", l)]
    lands = [_cast_fill(f"cast_{nm}_{l}", big[nm], l, block) for nm, l in use_order]
    lands, sends, recvs = _ag_start("ag_start", lands)
    pending = dict(zip(use_order, zip(lands, sends, recvs)))
    wg = {}

    def weight(key, after):
        if key not in wg:
            wg[key] = _ag_wait(f"ag_wait_{key[0]}_{key[1]}", *pending.pop(key), after)
        return wg[key]

    n_even = w_in_ab.shape[0]
    cw_rows = jnp.pad(conv_w.reshape(n_even * CONV_TAPS, conv_w.shape[2]), ((0, HALO - (n_even * CONV_TAPS) % HALO), (0, 0)))
    cw_all = _all_gather("ag_conv", [cw_rows])[0][:, :n_even * CONV_TAPS].reshape(NDEV, n_even, CONV_TAPS, -1)
    cw_all = jnp.transpose(cw_all, (1, 2, 0, 3)).reshape(n_even, CONV_TAPS, -1)
    cw_full = [jnp.pad(cw_all[e], ((0, HALO - CONV_TAPS), (0, 0))) for e in range(n_even)]

    def rows_nat(blk):
        return blk.reshape(blk.shape[0] * blk.shape[1], blk.shape[2])

    saved = []
    hn = _norm_fwd("norm_first", h0, g_pre=norm_mix_pre[0][None])[0]
    h = h0
    for l in range(depth):
        s = {"h_in": h, "hn1": hn}
        if l % 2 == 0:
            e = l // 2
            proj = _mm_nn_blk(f"fwd_in_{l}", hn, weight(("w_in_ab", e), hn))[0]
            ab = _gate_fwd(f"gate_fwd_{l}", proj, w_spatial[e], b_spatial[e].T, cw_full[e])
            mix = _mm_nn(f"fwd_out_{l}", ab, rows_nat(weight(("w_out_ab", e), ab)))
            s.update(proj=proj, ab=ab)
        else:
            o_ = l // 2
            qkv = _mm_nn_blk(f"fwd_qkv_{l}", hn, weight(("w_qkv", o_), hn))[0]
            att, lse = _attn_fwd(f"attn_fwd_{l}", qkv, cosf, sins)
            mix = _mm_nn(f"fwd_o_{l}", att, rows_nat(weight(("w_o", o_), att)))
            s.update(qkv=qkv, att=att, lse=lse)
        h1, hn2 = _norm_fwd(f"norm_mid_{l}", h, mix, norm_mix_post[l][None], norm_mlp_pre[l][None])
        up, act = _mm_nn_blk(f"fwd_up_{l}", hn2, weight(("w_up", l), hn2), relu2=True)
        f = _mm_nn(f"fwd_down_{l}", act, rows_nat(weight(("w_down", l), act)))
        s.update(mix=mix, h1=h1, hn2=hn2, up=up, act=act, f=f)
        if l + 1 < depth:
            h, hn = _norm_fwd(f"norm_end_{l}", h1, f, norm_mlp_post[l][None], norm_mix_pre[l + 1][None])
        else:
            h = _norm_fwd(f"norm_end_{l}", h1, f, norm_mlp_post[l][None])[0]
        saved.append(s)

    d_h, loss_row = _loss_grad("loss", h, target)
    rs = {}

    def scatter(key, g):
        rs[key] = _rs_start(f"rs_start_{key[0]}_{key[1]}", g.reshape(NDEV, -1, g.shape[-1]))

    dg ={nm: [None] * depth for nm in ("norm_mix_pre", "norm_mix_post", "norm_mlp_pre", "norm_mlp_post")}
    d_ws, d_bs, d_cw = [None] * n_even, [None] * n_even, [None] * n_even
    d_hn_next = None
    for l in reversed(range(depth)):
        s = saved[l]
        if l == depth - 1:
            d_f, dg["norm_mlp_post"][l] = _norm_bwd(f"nb_end_{l}", d_h, post=(s["f"], norm_mlp_post[l][None]))
        else:
            d_h, dg["norm_mix_pre"][l + 1], d_f, dg["norm_mlp_post"][l] = _norm_bwd(
                f"nb_end_{l}", d_h, pre=(d_hn_next, saved[l + 1]["h_in"], norm_mix_pre[l + 1][None]),
                post=(s["f"], norm_mlp_post[l][None]))
        wd = rows_nat(wg[("w_down", l)])
        d_up = _mm_nt_rows(f"bwd_down_{l}", d_f, wd, up=s["up"])
        scatter(("w_down", l), _mm_tn(f"gw_down_{l}", s["act"], d_f))
        scatter(("w_up", l), _mm_tn(f"gw_up_{l}", s["hn2"], d_up, nb=w_up.shape[2]))
        d_hn2 = _mm_nt_blk(f"bwd_up_{l}", d_up, wg[("w_up", l)], after=rs[("w_up", l)][0])
        d_h, dg["norm_mlp_pre"][l], d_mix, dg["norm_mix_post"][l] = _norm_bwd(
            f"nb_mid_{l}", d_h, pre=(d_hn2, s["h1"], norm_mlp_pre[l][None]),
            post=(s["mix"], norm_mix_post[l][None]))
        if l % 2 == 0:
            e = l // 2
            wo = rows_nat(wg[("w_out_ab", e)])
            d_ab = _mm_nt_rows(f"bwd_out_{l}", d_mix, wo)
            scatter(("w_out_ab", e), _mm_tn(f"gw_out_{l}", s["ab"], d_mix))
            d_proj, d_ws[e], d_bs[e], d_cw[e] = _gate_bwd(
                f"gate_bwd_{l}", s["proj"], d_ab, w_spatial[e], b_spatial[e].T, cw_full[e])
            scatter(("w_in_ab", e), _mm_tn(f"gw_in_{l}", s["hn1"], d_proj, nb=w_in_ab.shape[2]))
            d_hn_next = _mm_nt_blk(f"bwd_in_{l}", d_proj, wg[("w_in_ab", e)], after=rs[("w_in_ab", e)][0])
        else:
            o_ = l // 2
            wo = rows_nat(wg[("w_o", o_)])
            d_att = _mm_nt_rows(f"bwd_o_{l}", d_mix, wo)
            scatter(("w_o", o_), _mm_tn(f"gw_o_{l}", s["att"], d_mix))
            parts = _attn_bwd(f"attn_bwd_{l}", s["qkv"], cosf, sins, d_att, s["att"], s["lse"])
            d_qkv = _attn_merge(f"attn_merge_{l}", *parts)
            scatter(("w_qkv", o_), _mm_tn(f"gw_qkv_{l}", s["hn1"], d_qkv, nb=w_qkv.shape[2]))
            d_hn_next = _mm_nt_blk(f"bwd_qkv_{l}", d_qkv, wg[("w_qkv", o_)], after=rs[("w_qkv", o_)][0])
    grad_x, dg["norm_mix_pre"][0] = _norm_bwd("nb_first", d_h, pre=(d_hn_next, h0, norm_mix_pre[0][None]))

    moments = {"w_in_ab": (m_w_in_ab, v_w_in_ab), "w_out_ab": (m_w_out_ab, v_w_out_ab), "w_qkv": (m_w_qkv, v_w_qkv),
               "w_o": (m_w_o, v_w_o), "w_up": (m_w_up, v_w_up), "w_down": (m_w_down, v_w_down)}
    out_big = {}
    for nm in ("w_down", "w_up", "w_o", "w_qkv", "w_out_ab", "w_in_ab"):
        own, landed = [], []
        for l in range(big[nm].shape[0]):
            g, land = _wait_all(f"rs_wait_{nm}_{l}", *rs[(nm, l)], grad_x)
            own.append(g)
            landed.append(land)
        out_big[nm] = _adamw_layers(f"adamw_{nm}", own, landed, block, big[nm], moments[nm][0], moments[nm][1])

    small_g = ([jnp.concatenate(dg[nm], axis=0) for nm in dg]
               + [jnp.stack(d_ws), jnp.stack(d_bs), jnp.stack([c[:CONV_TAPS] for c in d_cw]), loss_row])
    packed = _pack(small_g)
    summed = _sum_parts("sum_small", _all_gather("ag_small", [packed])[0])
    g_nmp, g_nmo, g_nlp, g_nlo, g_ws, g_bs, g_cw_all, loss_sum = _unpack(summed, small_g)
    loss = loss_sum[0, 0]
    cwb = conv_w.shape[2]
    g_cw = lax.dynamic_slice_in_dim(g_cw_all, my_block * cwb, cwb, axis=2)
    small_w = [norm_mix_pre, norm_mix_post, norm_mlp_pre, norm_mlp_post, w_spatial, b_spatial, conv_w]
    small_m = [m_norm_mix_pre, m_norm_mix_post, m_norm_mlp_pre, m_norm_mlp_post, m_w_spatial, m_b_spatial, m_conv_w]
    small_v = [v_norm_mix_pre, v_norm_mix_post, v_norm_mlp_pre, v_norm_mlp_post, v_w_spatial, v_b_spatial, v_conv_w]
    small_grad = [g_nmp, g_nmo, g_nlp, g_nlo, g_ws, g_bs, g_cw]
    upd = _adamw("adamw_small", _pack(small_grad)[None], _pack(small_w), _pack(small_m), _pack(small_v))
    sg, sd, sm, sv = [_unpack(u, small_w) for u in upd]

    def outs(i_small, i_big):
        return (i_small[0], i_small[1], i_small[2], i_small[3], i_big["w_in_ab"], i_small[4], i_small[5], i_small[6],
                i_big["w_out_ab"], i_big["w_qkv"], i_big["w_o"], i_big["w_up"], i_big["w_down"])

    pick = lambda i: {nm: out_big[nm][i] for nm in big}
    return (loss, grad_x.reshape(x.shape), *outs(sg, pick(0)), *outs(sd, pick(1)), *outs(sm, pick(2)),
            *outs(sv, pick(3)))
```

```python
import functools

import jax
import jax.numpy as jnp
from jax import lax
from jax.experimental import pallas as pl
from jax.experimental.pallas import tpu as pltpu

F32 = jnp.float32
BF16 = jnp.bfloat16
MESH = pl.DeviceIdType.MESH
ANY = pl.BlockSpec(memory_space=pl.ANY)
HBM = pl.BlockSpec(memory_space=pltpu.HBM)
SEM = pl.BlockSpec(memory_space=pltpu.SEMAPHORE)
EFFECT = pltpu.SideEffectType.DATAFLOW_SIDE_EFFECTING

NDEV = 8
NCHIP = 4
RMS_EPS = 1e-6
LN_EPS = 1e-5
CHUNK = 128
HEAD = 128
ATT_TILE = 2048
ATT_UNROLL = 4
DILATIONS = (1, 4, 16)
ROPE_THETA = 10000.0
CONV_TAPS = 3
HALO = 8
GELU_C = 0.7978845608028654
GELU_A = 0.044715
ADAM_LR, ADAM_B1, ADAM_B2, ADAM_EPS, ADAM_WD, ADAM_STEP = 0.001, 0.9, 0.999, 1e-08, 0.01, 10
VMEM_MM = 52 << 20
VMEM_EW = 40 << 20


def _cp(sem=None, vmem=VMEM_EW):
    if sem is None:
        return pltpu.CompilerParams(vmem_limit_bytes=vmem)
    return pltpu.CompilerParams(dimension_semantics=sem, vmem_limit_bytes=vmem)


def _tile(n, want):
    return want if n % want == 0 else n


def _all_gather(name, shards):
    n = len(shards)

    def body(*refs):
        ins, outs = refs[:n], refs[n:2 * n]
        send_sems, recv_sems, local_sems = refs[2 * n:]
        x, y, c = lax.axis_index("x"), lax.axis_index("y"), lax.axis_index("c")
        me, sibling = (x, y, c), (x, y, 1 - c)
        chips = [(1 - x, y), (x, 1 - y), (1 - x, 1 - y)]

        def slot(p):
            return 4 * p[0] + 2 * p[1] + p[2]

        def copy(i, k, block, to, src=None):
            dst = outs[i].at[slot(block)]
            return pltpu.make_async_remote_copy(
                src_ref=dst if src is None else src, dst_ref=dst,
                send_sem=send_sems.at[i, k], recv_sem=recv_sems.at[i, k],
                device_id=to, device_id_type=MESH)

        mine = [pltpu.make_async_copy(ins[i], outs[i].at[slot(me)], local_sems.at[i]) for i in range(n)]
        for cp in mine:
            cp.start()
        first = []
        for i in range(n):
            first.append(copy(i, 0, me, sibling, src=ins[i]))
            for j, chip in enumerate(chips):
                first.append(copy(i, 1 + j, me, (*chip, c), src=ins[i]))
        for cp in first:
            cp.start()
        passed = []
        for j, chip in enumerate(chips):
            for i in range(n):
                copy(i, 1 + j, (*chip, c), me).wait_recv()
                fwd = copy(i, 4 + j, (*chip, c), sibling)
                fwd.start()
                passed.append(fwd)
        for i in range(n):
            copy(i, 0, sibling, me).wait_recv()
            for j, chip in enumerate(chips):
                copy(i, 4 + j, (*chip, 1 - c), me).wait_recv()
        for cp in first + passed:
            cp.wait_send()
        for cp in mine:
            cp.wait()

    return pl.pallas_call(
        body, name=name,
        out_shape=[jax.ShapeDtypeStruct((NDEV,) + s.shape, s.dtype) for s in shards],
        in_specs=[ANY] * n, out_specs=[ANY] * n,
        scratch_shapes=[pltpu.SemaphoreType.DMA((n, 7)), pltpu.SemaphoreType.DMA((n, 7)),
                        pltpu.SemaphoreType.DMA((n,))],
    )(*shards)


def _peer(x, y, c, r):
    return (1 - x if r & 4 else x, 1 - y if r & 2 else y, 1 - c if r & 1 else c)


def _slot(p):
    return 4 * p[0] + 2 * p[1] + p[2]


def _cast_fill(name, w, layer, block):
    _, rows, cols = w.shape
    tr = _tile(rows, 256)

    def body(blk_ref, w_ref, o_ref):
        o_ref[...] = w_ref[...].astype(BF16)

    return pl.pallas_call(
        body, name=name,
        grid_spec=pltpu.PrefetchScalarGridSpec(
            num_scalar_prefetch=1, grid=(rows // tr,),
            in_specs=[pl.BlockSpec((None, tr, cols), lambda i, blk: (layer, i, 0))],
            out_specs=pl.BlockSpec((None, tr, cols), lambda i, blk: (blk[0], i, 0))),
        out_shape=jax.ShapeDtypeStruct((NDEV, rows, cols), BF16),
        compiler_params=_cp(("parallel",)))(block, w)


OTHER_CHIPS = (2, 4, 6)


def _ag_start(name, lands, after):
    n = len(lands)

    def body(*refs):
        ins, sems = refs[:n], refs[n + 1:n + 1 + 4 * n]
        x, y, c = lax.axis_index("x"), lax.axis_index("y"), lax.axis_index("c")
        mine = _slot((x, y, c))
        for i in range(n):
            send_a, recv_a, _, recv_b = sems[4 * i:4 * i + 4]
            block = ins[i].at[mine]
            pltpu.make_async_remote_copy(src_ref=block, dst_ref=block, send_sem=send_a, recv_sem=recv_b,
                                         device_id=_peer(x, y, c, 1), device_id_type=MESH).start()
            for r in OTHER_CHIPS:
                pltpu.make_async_remote_copy(src_ref=block, dst_ref=block, send_sem=send_a, recv_sem=recv_a,
                                             device_id=_peer(x, y, c, r), device_id_type=MESH).start()

    outs = pl.pallas_call(
        body, name=name,
        out_shape=[pltpu.SemaphoreType.DMA(())] * (4 * n) + [pltpu.HBM(a.shape, a.dtype) for a in lands],
        in_specs=[HBM] * n + [ANY], out_specs=[SEM] * (4 * n) + [HBM] * n,
        input_output_aliases={i: 4 * n + i for i in range(n)},
        compiler_params=pltpu.CompilerParams(has_side_effects=EFFECT),
    )(*[pltpu.with_memory_space_constraint(a, pltpu.HBM) for a in lands], after)
    return outs[4 * n:], [tuple(outs[4 * i:4 * i + 4]) for i in range(n)]


def _ag_mid(name, land, sems, after):
    _, recv_a, send_b, recv_b = sems

    def body(land_ref, recv_a_ref, send_b_ref, recv_b_ref, after_ref, land_out):
        x, y, c = lax.axis_index("x"), lax.axis_index("y"), lax.axis_index("c")
        sibling = _peer(x, y, c, 1)
        three = land_ref.at[pl.ds(0, len(OTHER_CHIPS))]
        pltpu.make_async_remote_copy(src_ref=three, dst_ref=three, send_sem=send_b_ref, recv_sem=recv_a_ref,
                                     device_id=sibling, device_id_type=MESH).wait_recv()
        for r in OTHER_CHIPS:
            block = land_ref.at[_slot(_peer(x, y, c, r))]
            pltpu.make_async_remote_copy(src_ref=block, dst_ref=block, send_sem=send_b_ref, recv_sem=recv_b_ref,
                                         device_id=sibling, device_id_type=MESH).start()

    return pl.pallas_call(
        body, name=name, out_shape=pltpu.HBM(land.shape, land.dtype),
        in_specs=[HBM, SEM, SEM, SEM, ANY], out_specs=HBM, input_output_aliases={0: 0},
        compiler_params=pltpu.CompilerParams(has_side_effects=EFFECT),
    )(land, recv_a, send_b, recv_b, after)


def _ag_wait(name, land, sems, after):
    send_a, _, send_b, recv_b = sems

    def body(land_ref, send_a_ref, send_b_ref, recv_b_ref, after_ref, land_out):
        x, y, c = lax.axis_index("x"), lax.axis_index("y"), lax.axis_index("c")
        sibling = _peer(x, y, c, 1)
        four = land_ref.at[pl.ds(0, 1 + len(OTHER_CHIPS))]
        three = land_ref.at[pl.ds(0, len(OTHER_CHIPS))]
        first = pltpu.make_async_remote_copy(src_ref=four, dst_ref=four, send_sem=send_a_ref, recv_sem=recv_b_ref,
                                             device_id=sibling, device_id_type=MESH)
        passed = pltpu.make_async_remote_copy(src_ref=three, dst_ref=three, send_sem=send_b_ref, recv_sem=recv_b_ref,
                                              device_id=sibling, device_id_type=MESH)
        first.wait_send()
        passed.wait_send()
        first.wait_recv()

    return pl.pallas_call(
        body, name=name, out_shape=pltpu.HBM(land.shape, land.dtype),
        in_specs=[HBM, SEM, SEM, SEM, ANY], out_specs=HBM, input_output_aliases={0: 0},
        compiler_params=pltpu.CompilerParams(has_side_effects=EFFECT),
    )(land, send_a, send_b, recv_b, after)


def _wait_all(name, src, land, send, recv, after):
    def body(src_ref, land_ref, send_ref, recv_ref, after_ref, src_out, land_out):
        x, y, c = lax.axis_index("x"), lax.axis_index("y"), lax.axis_index("c")
        seven = land_ref.at[pl.ds(0, NDEV - 1)]
        copy = pltpu.make_async_remote_copy(src_ref=seven, dst_ref=seven, send_sem=send_ref, recv_sem=recv_ref,
                                            device_id=_peer(x, y, c, 1), device_id_type=MESH)
        copy.wait_send()
        copy.wait_recv()

    return pl.pallas_call(
        body, name=name,
        out_shape=[pltpu.HBM(src.shape, src.dtype), pltpu.HBM(land.shape, land.dtype)],
        in_specs=[HBM, HBM, SEM, SEM, ANY], out_specs=[HBM, HBM],
        input_output_aliases={0: 0, 1: 1},
        compiler_params=pltpu.CompilerParams(has_side_effects=EFFECT),
    )(src, land, send, recv, after)


def _rs_start(name, grad):
    land = lax.empty((NDEV - 1,) + grad.shape[1:], grad.dtype)

    def body(g_ref, land_ref, send, recv, g_out, land_out):
        x, y, c = lax.axis_index("x"), lax.axis_index("y"), lax.axis_index("c")
        for r in range(1, NDEV):
            peer = _peer(x, y, c, r)
            pltpu.make_async_remote_copy(
                src_ref=g_ref.at[_slot(peer)], dst_ref=land_ref.at[r - 1], send_sem=send, recv_sem=recv,
                device_id=peer, device_id_type=MESH).start()

    send, recv, g_thru, land_thru = pl.pallas_call(
        body, name=name,
        out_shape=[pltpu.SemaphoreType.DMA(()), pltpu.SemaphoreType.DMA(()),
                   pltpu.HBM(grad.shape, grad.dtype), pltpu.HBM(land.shape, land.dtype)],
        in_specs=[HBM, HBM], out_specs=[SEM, SEM, HBM, HBM], input_output_aliases={0: 2, 1: 3},
        compiler_params=pltpu.CompilerParams(has_side_effects=EFFECT),
    )(pltpu.with_memory_space_constraint(grad, pltpu.HBM), pltpu.with_memory_space_constraint(land, pltpu.HBM))
    return g_thru, land_thru, send, recv


def _adam_math(w, g, m, v):
    m = ADAM_B1 * m + (1.0 - ADAM_B1) * g
    v = ADAM_B2 * v + (1.0 - ADAM_B2) * (g * g)
    m_hat = m / (1.0 - ADAM_B1 ** ADAM_STEP)
    v_hat = v / (1.0 - ADAM_B2 ** ADAM_STEP)
    delta = -ADAM_LR * (m_hat / (jnp.sqrt(v_hat) + ADAM_EPS) + ADAM_WD * w)
    return delta, m, v


def _adamw(name, parts, w, m, v):
    nparts, rows, cols = parts.shape
    tr = _tile(rows, 256)

    def body(p_ref, w_ref, m_ref, v_ref, g_out, d_out, m_out, v_out):
        g = p_ref[0].astype(F32)
        for k in range(1, nparts):
            g = g + p_ref[k].astype(F32)
        delta, mn, vn = _adam_math(w_ref[...], g, m_ref[...], v_ref[...])
        g_out[...] = g
        d_out[...] = delta
        m_out[...] = mn
        v_out[...] = vn

    row = pl.BlockSpec((tr, cols), lambda i: (i, 0))
    return pl.pallas_call(
        body, name=name, grid=(rows // tr,),
        in_specs=[pl.BlockSpec((nparts, tr, cols), lambda i: (0, i, 0)), row, row, row],
        out_specs=[row] * 4,
        out_shape=[jax.ShapeDtypeStruct((rows, cols), F32)] * 4,
        compiler_params=_cp(("parallel",)),
    )(parts, w, m, v)


def _adamw_layers(name, grads, lands, block, w, m, v):
    layers, rows, cols = w.shape
    nland = lands[0].shape[0]
    tr = rows
    while tr % 2 == 0 and tr > 8 and nland * tr * cols * 2 > (1 << 20):
        tr //= 2

    def body(blk_ref, *refs):
        own_refs, land_refs = refs[:layers], refs[layers:2 * layers]
        w_ref, m_ref, v_ref, g_out, d_out, m_out, v_out = refs[2 * layers:]
        layer = pl.program_id(0)
        for k in range(layers):
            @pl.when(layer == k)
            def _(k=k):
                g = own_refs[k][...].astype(F32)
                for s in range(nland):
                    g = g + land_refs[k][s].astype(F32)
                delta, mn, vn = _adam_math(w_ref[...], g, m_ref[...], v_ref[...])
                g_out[...] = g
                d_out[...] = delta
                m_out[...] = mn
                v_out[...] = vn

    def own_spec(k):
        return pl.BlockSpec((None, tr, cols), lambda l, i, blk: (blk[0], jnp.where(l == k, i, 0), 0))

    def land_spec(k):
        return pl.BlockSpec((nland, tr, cols), lambda l, i, blk: (0, jnp.where(l == k, i, 0), 0))

    row = pl.BlockSpec((None, tr, cols), lambda l, i, blk: (l, i, 0))
    return pl.pallas_call(
        body, name=name,
        grid_spec=pltpu.PrefetchScalarGridSpec(
            num_scalar_prefetch=1, grid=(layers, rows // tr),
            in_specs=[own_spec(k) for k in range(layers)] + [land_spec(k) for k in range(layers)] + [row, row, row],
            out_specs=[row] * 4),
        out_shape=[jax.ShapeDtypeStruct((layers, rows, cols), F32)] * 4,
        compiler_params=_cp(("arbitrary", "arbitrary")),
    )(block, *grads, *lands, w, m, v)


def _norm_fwd(name, h, z=None, g_post=None, g_pre=None):
    rows, d = h.shape
    tm = _tile(rows, 256)
    has_post, has_pre = z is not None, g_pre is not None

    def body(*refs):
        it = iter(refs)
        hv = next(it)[...]
        if has_post:
            zv, gp = next(it)[...], next(it)[...]
        if has_pre:
            gq = next(it)[...]
        if has_post:
            r = lax.rsqrt(jnp.mean(zv * zv, axis=-1, keepdims=True) + RMS_EPS)
            hv = hv + (zv * r) * gp
            next(it)[...] = hv
        if has_pre:
            r = lax.rsqrt(jnp.mean(hv * hv, axis=-1, keepdims=True) + RMS_EPS)
            next(it)[...] = ((hv * r) * gq).astype(BF16)

    row = pl.BlockSpec((tm, d), lambda i: (i, 0))
    vec = pl.BlockSpec((1, d), lambda i: (0, 0))
    ins, in_specs, out_shape, out_specs = [h], [row], [], []
    if has_post:
        ins += [z, g_post]
        in_specs += [row, vec]
        out_shape.append(jax.ShapeDtypeStruct((rows, d), F32))
        out_specs.append(row)
    if has_pre:
        ins.append(g_pre)
        in_specs.append(vec)
        out_shape.append(jax.ShapeDtypeStruct((rows, d), BF16))
        out_specs.append(row)
    return pl.pallas_call(body, name=name, grid=(rows // tm,), in_specs=in_specs, out_specs=out_specs,
                          out_shape=out_shape, compiler_params=_cp(("parallel",)))(*ins)


def _rms_bwd_rows(x, g, dy):
    r = lax.rsqrt(jnp.mean(x * x, axis=-1, keepdims=True) + RMS_EPS)
    xn = x * r
    dg = jnp.sum(dy * xn, axis=0, keepdims=True)
    dxn = dy * g
    dx = r * (dxn - xn * jnp.mean(dxn * xn, axis=-1, keepdims=True))
    return dx, dg


def _norm_bwd(name, d_out, pre=None, post=None):
    rows, d = d_out.shape
    tm = _tile(rows, 256)
    has_pre, has_post = pre is not None, post is not None

    def body(*refs):
        it = iter(refs)
        dres = next(it)[...]
        if has_pre:
            dy, xp, gq = next(it)[...], next(it)[...], next(it)[...]
        if has_post:
            zv, gp = next(it)[...], next(it)[...]
        first = pl.program_id(0) == 0
        if has_pre:
            dx, dg = _rms_bwd_rows(xp, gq, dy)
            dres = dres + dx
            next(it)[...] = dres
            dg_ref = next(it)

            @pl.when(first)
            def _():
                dg_ref[...] = jnp.zeros_like(dg_ref)
            dg_ref[...] += dg
        if has_post:
            dz, dg2 = _rms_bwd_rows(zv, gp, dres)
            next(it)[...] = dz.astype(BF16)
            dg2_ref = next(it)

            @pl.when(first)
            def _():
                dg2_ref[...] = jnp.zeros_like(dg2_ref)
            dg2_ref[...] += dg2

    row = pl.BlockSpec((tm, d), lambda i: (i, 0))
    vec = pl.BlockSpec((1, d), lambda i: (0, 0))
    ins, in_specs, out_shape, out_specs = [d_out], [row], [], []
    if has_pre:
        ins += list(pre)
        in_specs += [row, row, vec]
        out_shape += [jax.ShapeDtypeStruct((rows, d), F32), jax.ShapeDtypeStruct((1, d), F32)]
        out_specs += [row, vec]
    if has_post:
        ins += list(post)
        in_specs += [row, vec]
        out_shape += [jax.ShapeDtypeStruct((rows, d), BF16), jax.ShapeDtypeStruct((1, d), F32)]
        out_specs += [row, vec]
    return pl.pallas_call(body, name=name, grid=(rows // tm,), in_specs=in_specs, out_specs=out_specs,
                          out_shape=out_shape, compiler_params=_cp(("arbitrary",)))(*ins)


def _loss_grad(name, y, target):
    rows, d = y.shape
    tm = _tile(rows, 256)

    def body(y_ref, t_ref, dy_ref, loss_ref):
        err = y_ref[...] - t_ref[...]
        dy_ref[...] = err * (1.0 / d)

        @pl.when(pl.program_id(0) == 0)
        def _():
            loss_ref[...] = jnp.zeros_like(loss_ref)
        loss_ref[...] += jnp.full(loss_ref.shape, (0.5 / d) * jnp.sum(err * err), F32)

    row = pl.BlockSpec((tm, d), lambda i: (i, 0))
    return pl.pallas_call(
        body, name=name, grid=(rows // tm,), in_specs=[row, row],
        out_specs=[row, pl.BlockSpec((1, 128), lambda i: (0, 0))],
        out_shape=[jax.ShapeDtypeStruct((rows, d), F32), jax.ShapeDtypeStruct((1, 128), F32)],
        compiler_params=_cp(("arbitrary",)))(y, target)


NT_DIMS = (((1,), (1,)), ((), ()))
TN_DIMS = (((0,), (0,)), ((), ()))


def _mm_nn_blk(name, a, wblk, relu2=False):
    m, k = a.shape
    nb = wblk.shape[2]
    tm = _tile(m, 512)

    def body(a_ref, w_ref, *outs):
        r = jnp.dot(a_ref[...], w_ref[...], preferred_element_type=F32)
        outs[0][...] = r
        if relu2:
            rr = jnp.maximum(r, 0.0)
            outs[1][...] = (rr * rr).astype(BF16)

    out_shape = [jax.ShapeDtypeStruct((m, NDEV * nb), F32)]
    if relu2:
        out_shape.append(jax.ShapeDtypeStruct((m, NDEV * nb), BF16))
    return pl.pallas_call(
        body, name=name, grid=(NDEV, m // tm),
        in_specs=[pl.BlockSpec((tm, k), lambda d, i: (i, 0)), pl.BlockSpec((None, k, nb), lambda d, i: (d, 0, 0))],
        out_specs=[pl.BlockSpec((tm, nb), lambda d, i: (i, d))] * len(out_shape),
        out_shape=out_shape, compiler_params=_cp(("parallel", "parallel"), VMEM_MM))(a, wblk)


def _mm_nn(name, a, w):
    m, kb = a.shape
    n = w.shape[1]
    tm, tk = _tile(m, 512), _tile(kb, 2048)

    def body(a_ref, w_ref, o_ref):
        r = jnp.dot(a_ref[...], w_ref[...], preferred_element_type=F32)
        step = pl.program_id(1)

        @pl.when(step == 0)
        def _():
            o_ref[...] = r

        @pl.when(step > 0)
        def _():
            o_ref[...] += r

    return pl.pallas_call(
        body, name=name, grid=(m // tm, kb // tk),
        in_specs=[pl.BlockSpec((tm, tk), lambda i, s: (i, s)), pl.BlockSpec((tk, n), lambda i, s: (s, 0))],
        out_specs=pl.BlockSpec((tm, n), lambda i, s: (i, 0)),
        out_shape=jax.ShapeDtypeStruct((m, n), F32),
        compiler_params=_cp(("parallel", "arbitrary"), VMEM_MM))(a, w)


def _mm_nt_rows(name, dy, w, up=None):
    m, n = dy.shape
    kw = w.shape[0]
    tm, tkw = _tile(m, 512), _tile(kw, 1024)

    def body(dy_ref, w_ref, *rest):
        r = lax.dot_general(dy_ref[...], w_ref[...], NT_DIMS, preferred_element_type=F32)
        if up is None:
            rest[0][...] = r
        else:
            rest[1][...] = (r * (2.0 * jnp.maximum(rest[0][...], 0.0))).astype(BF16)

    ins = [dy, w]
    in_specs = [pl.BlockSpec((tm, n), lambda j, i: (i, 0)), pl.BlockSpec((tkw, n), lambda j, i: (j, 0))]
    if up is not None:
        ins.append(up)
        in_specs.append(pl.BlockSpec((tm, tkw), lambda j, i: (i, j)))
    return pl.pallas_call(
        body, name=name, grid=(kw // tkw, m // tm), in_specs=in_specs,
        out_specs=pl.BlockSpec((tm, tkw), lambda j, i: (i, j)),
        out_shape=jax.ShapeDtypeStruct((m, kw), F32 if up is None else BF16),
        compiler_params=_cp(("parallel", "parallel"), VMEM_MM))(*ins)


def _mm_nt_blk(name, dy, wblk, after=None):
    m = dy.shape[0]
    _, kw, nb = wblk.shape
    tm, per = _tile(m, 512), 2

    def body(dy_ref, w_ref, *rest):
        o_ref = rest[-1]
        r = lax.dot_general(dy_ref[:, :nb], w_ref[0], NT_DIMS, preferred_element_type=F32)
        for t in range(1, per):
            r = r + lax.dot_general(dy_ref[:, t * nb:(t + 1) * nb], w_ref[t], NT_DIMS, preferred_element_type=F32)
        step = pl.program_id(1)

        @pl.when(step == 0)
        def _():
            o_ref[...] = r

        @pl.when(step > 0)
        def _():
            o_ref[...] += r

    extra = list(after or ())
    return pl.pallas_call(
        body, name=name, grid=(m // tm, NDEV // per),
        in_specs=[pl.BlockSpec((tm, per * nb), lambda i, s: (i, s)),
                  pl.BlockSpec((per, kw, nb), lambda i, s: (s, 0, 0))] + [ANY] * len(extra),
        out_specs=pl.BlockSpec((tm, kw), lambda i, s: (i, 0)),
        out_shape=jax.ShapeDtypeStruct((m, kw), F32),
        compiler_params=_cp(("parallel", "arbitrary"), VMEM_MM))(dy, wblk, *extra)


def _mm_tn(name, x, dy, nb=None):
    t, mx = x.shape
    n = dy.shape[1]
    tmx = _tile(mx, 512)
    tn = nb if nb is not None else _tile(n, 1024)

    def body(x_ref, dy_ref, o_ref):
        o_ref[...] = lax.dot_general(x_ref[...], dy_ref[...], TN_DIMS, preferred_element_type=F32).astype(BF16)

    if nb is None:
        out_shape = jax.ShapeDtypeStruct((mx, n), BF16)
        out_spec = pl.BlockSpec((tmx, tn), lambda j, i: (i, j))
    else:
        out_shape = jax.ShapeDtypeStruct((NDEV, mx, nb), BF16)
        out_spec = pl.BlockSpec((None, tmx, nb), lambda j, i: (j, i, 0))
    return pl.pallas_call(
        body, name=name, grid=(n // tn, mx // tmx),
        in_specs=[pl.BlockSpec((t, tmx), lambda j, i: (0, i)), pl.BlockSpec((t, tn), lambda j, i: (0, j))],
        out_specs=out_spec, out_shape=out_shape,
        compiler_params=_cp(("parallel", "parallel"), VMEM_MM))(x, dy)


def _gelu(x):
    return 0.5 * x * (1.0 + jnp.tanh(GELU_C * (x + GELU_A * (x * x * x))))


def _gelu_grad(x):
    t = jnp.tanh(GELU_C * (x + GELU_A * (x * x * x)))
    return 0.5 * (1.0 + t) + 0.5 * x * (1.0 - t * t) * (GELU_C * (1.0 + 3.0 * GELU_A * (x * x)))


def _layernorm(a):
    mu = jnp.mean(a, axis=-1, keepdims=True)
    ac = a - mu
    rstd = lax.rsqrt(jnp.mean(ac * ac, axis=-1, keepdims=True) + LN_EPS)
    return ac * rstd, rstd


def _shift_rows(z, halo, k):
    zr = pltpu.roll(z, k, 0)
    hr = pltpu.roll(halo, k, 0)
    row = lax.broadcasted_iota(jnp.int32, hr.shape, 0)
    top = jnp.where(row < k, hr, zr[:HALO])
    return jnp.concatenate([top, zr[HALO:]], axis=0)


def _shift_rows_up(z, halo, k):
    rows = z.shape[0]
    zr = pltpu.roll(z, rows - k, 0)
    hr = pltpu.roll(halo, HALO - k, 0)
    row = lax.broadcasted_iota(jnp.int32, hr.shape, 0)
    bot = jnp.where(row >= HALO - k, hr, zr[rows - HALO:])
    return jnp.concatenate([zr[:rows - HALO], bot], axis=0)


def _causal_mask():
    t = lax.broadcasted_iota(jnp.int32, (CHUNK, CHUNK), 0)
    s = lax.broadcasted_iota(jnp.int32, (CHUNK, CHUNK), 1)
    return s <= t


def _gate_specs(tm, rows, width):
    per = tm // HALO
    last = rows // HALO - 1
    cur = pl.BlockSpec((tm, width), lambda i: (i, 0))
    prev = pl.BlockSpec((HALO, width), lambda i: (jnp.maximum(i * per - 1, 0), 0))
    nxt = pl.BlockSpec((HALO, width), lambda i: (jnp.minimum((i + 1) * per, last), 0))
    return cur, prev, nxt


def _gate_fwd(name, proj, w_s, b_st, cw):
    rows, width = proj.shape
    w = width // 5
    groups = w // CHUNK
    tm = _tile(rows, 256)
    cur, prev, _ = _gate_specs(tm, rows, width)

    def body(p_ref, h_ref, ws_ref, b_ref, cw_ref, o_ref):
        mask = _causal_mask()
        au = _gelu(p_ref[:, 0:w])
        vn, _ = _layernorm(_gelu(p_ref[:, w:2 * w]))
        vn = vn.astype(BF16)
        for g in range(groups):
            wc = jnp.where(mask, ws_ref[g], 0.0).astype(BF16)
            cols = slice(g * CHUNK, (g + 1) * CHUNK)
            for ch in range(tm // CHUNK):
                rws = slice(ch * CHUNK, (ch + 1) * CHUNK)
                mixed = jnp.dot(wc, vn[rws, cols], preferred_element_type=F32) + b_ref[:, g:g + 1]
                o_ref[rws, cols] = (au[rws, cols] * mixed).astype(BF16)
        z = p_ref[:, 3 * w:4 * w] * p_ref[:, 4 * w:5 * w]
        zh = h_ref[:, 3 * w:4 * w] * h_ref[:, 4 * w:5 * w]
        zh = jnp.where(pl.program_id(0) == 0, 0.0, zh)
        y = cw_ref[0:1, :] * _shift_rows(z, zh, 2) + cw_ref[1:2, :] * _shift_rows(z, zh, 1) + cw_ref[2:3, :] * z
        o_ref[:, w:2 * w] = (p_ref[:, 2 * w:3 * w] * y).astype(BF16)

    full = lambda a: pl.BlockSpec(a.shape, lambda i: (0,) * a.ndim)
    return pl.pallas_call(
        body, name=name, grid=(rows // tm,),
        in_specs=[cur, prev, full(w_s), full(b_st), full(cw)],
        out_specs=pl.BlockSpec((tm, 2 * w), lambda i: (i, 0)),
        out_shape=jax.ShapeDtypeStruct((rows, 2 * w), BF16),
        compiler_params=_cp(("parallel",)))(proj, proj, w_s, b_st, cw)


def _gate_bwd(name, proj, d_ab, w_s, b_st, cw):
    rows, width = proj.shape
    w = width // 5
    groups = w // CHUNK
    tm = _tile(rows, 256)
    cur, prev, nxt = _gate_specs(tm, rows, width)
    dcur, _, dnxt = _gate_specs(tm, rows, 2 * w)

    def body(p_ref, ph_ref, pn_ref, d_ref, dn_ref, ws_ref, b_ref, cw_ref, o_ref, dws_ref, dbs_ref, dcw_ref):
        i = pl.program_id(0)

        @pl.when(i == 0)
        def _():
            dws_ref[...] = jnp.zeros_like(dws_ref)
            dbs_ref[...] = jnp.zeros_like(dbs_ref)
            dcw_ref[...] = jnp.zeros_like(dcw_ref)

        mask = _causal_mask()
        u, v = p_ref[:, 0:w], p_ref[:, w:2 * w]
        au, av = _gelu(u), _gelu(v)
        vn, rstd = _layernorm(av)
        vnb = vn.astype(BF16)
        d_a = d_ref[:, 0:w]
        d_mixed = (d_a * au).astype(BF16)
        ones = jnp.ones((HALO, CHUNK), BF16)
        d_vn_cols = []
        d_au_cols = []
        for g in range(groups):
            wc = jnp.where(mask, ws_ref[g], 0.0).astype(BF16)
            cols = slice(g * CHUNK, (g + 1) * CHUNK)
            dw = jnp.zeros((CHUNK, CHUNK), F32)
            db = jnp.zeros((HALO, CHUNK), F32)
            d_vn_rows, d_au_rows = [], []
            for ch in range(tm // CHUNK):
                rws = slice(ch * CHUNK, (ch + 1) * CHUNK)
                mixed = jnp.dot(wc, vnb[rws, cols], preferred_element_type=F32) + b_ref[:, g:g + 1]
                d_au_rows.append(d_a[rws, cols] * mixed)
                dm = d_mixed[rws, cols]
                dw = dw + lax.dot_general(dm, vnb[rws, cols], NT_DIMS, preferred_element_type=F32)
                db = db + lax.dot_general(ones, dm, NT_DIMS, preferred_element_type=F32)
                d_vn_rows.append(lax.dot_general(wc, dm, TN_DIMS, preferred_element_type=F32))
            dws_ref[g] += jnp.where(mask, dw, 0.0)
            dbs_ref[g:g + 1, :] += db[0:1, :]
            d_vn_cols.append(jnp.concatenate(d_vn_rows, axis=0))
            d_au_cols.append(jnp.concatenate(d_au_rows, axis=0))
        d_vn = jnp.concatenate(d_vn_cols, axis=1)
        d_au = jnp.concatenate(d_au_cols, axis=1)
        d_av = rstd * (d_vn - jnp.mean(d_vn, axis=-1, keepdims=True)
                       - vn * jnp.mean(d_vn * vn, axis=-1, keepdims=True))
        o_ref[:, 0:w] = (d_au * _gelu_grad(u)).astype(BF16)
        o_ref[:, w:2 * w] = (d_av * _gelu_grad(v)).astype(BF16)

        gb, gc, bx = p_ref[:, 2 * w:3 * w], p_ref[:, 3 * w:4 * w], p_ref[:, 4 * w:5 * w]
        z = gc * bx
        zh = jnp.where(i == 0, 0.0, ph_ref[:, 3 * w:4 * w] * ph_ref[:, 4 * w:5 * w])
        z1, z2 = _shift_rows(z, zh, 1), _shift_rows(z, zh, 2)
        d_b = d_ref[:, w:2 * w]
        y = cw_ref[0:1, :] * z2 + cw_ref[1:2, :] * z1 + cw_ref[2:3, :] * z
        dy = d_b * gb
        dyn = jnp.where(i == pl.num_programs(0) - 1, 0.0, dn_ref[:, w:2 * w] * pn_ref[:, 2 * w:3 * w])
        dz = (cw_ref[2:3, :] * dy + cw_ref[1:2, :] * _shift_rows_up(dy, dyn, 1)
              + cw_ref[0:1, :] * _shift_rows_up(dy, dyn, 2))
        dcw_ref[0:1, :] += jnp.sum(dy * z2, axis=0, keepdims=True)
        dcw_ref[1:2, :] += jnp.sum(dy * z1, axis=0, keepdims=True)
        dcw_ref[2:3, :] += jnp.sum(dy * z, axis=0, keepdims=True)
        o_ref[:, 2 * w:3 * w] = (d_b * y).astype(BF16)
        o_ref[:, 3 * w:4 * w] = (dz * bx).astype(BF16)
        o_ref[:, 4 * w:5 * w] = (dz * gc).astype(BF16)

    full = lambda a: pl.BlockSpec(a.shape, lambda i: (0,) * a.ndim)
    acc = lambda shape: pl.BlockSpec(shape, lambda i: (0,) * len(shape))
    return pl.pallas_call(
        body, name=name, grid=(rows // tm,),
        in_specs=[cur, prev, nxt, dcur, dnxt, full(w_s), full(b_st), full(cw)],
        out_specs=[pl.BlockSpec((tm, width), lambda i: (i, 0)), acc((groups, CHUNK, CHUNK)),
                   acc((groups, CHUNK)), acc((HALO, w))],
        out_shape=[jax.ShapeDtypeStruct((rows, width), BF16), jax.ShapeDtypeStruct((groups, CHUNK, CHUNK), F32),
                   jax.ShapeDtypeStruct((groups, CHUNK), F32), jax.ShapeDtypeStruct((HALO, w), F32)],
        compiler_params=_cp(("arbitrary",), VMEM_MM))(proj, proj, proj, d_ab, d_ab, w_s, b_st, cw)


def _rope(t, cosf, sins):
    return t * cosf + pltpu.roll(t, HEAD // 2, 1) * sins


def _rope_bwd(dt, cosf, sins):
    return dt * cosf + pltpu.roll(dt * sins, HEAD // 2, 1)


def _attn_units(visit):
    for b, d in enumerate(DILATIONS):
        blocks = ATT_TILE // (CHUNK * d)
        for r in range(d):
            if blocks <= ATT_UNROLL:
                for j in range(blocks):
                    visit(b, d, r, j)
            else:
                def step(jj, carry, b=b, d=d, r=r):
                    for u in range(ATT_UNROLL):
                        visit(b, d, r, jj * ATT_UNROLL + u)
                    return carry
                lax.fori_loop(0, blocks // ATT_UNROLL, step, 0)


def _unit_rows(ref, d, r, j, nblk, offset=0):
    base = offset + j * (CHUNK * d)
    if not isinstance(base, int):
        base = pl.multiple_of(base, CHUNK)
    return ref.at[pl.ds(base, nblk * CHUNK * d)], pl.ds(r, nblk * CHUNK, stride=d)


def _band_bias(bias):
    qi = lax.broadcasted_iota(jnp.int32, (CHUNK, 2 * CHUNK), 0)
    ki = lax.broadcasted_iota(jnp.int32, (CHUNK, 2 * CHUNK), 1)
    band = (ki >= qi) & (ki <= qi + CHUNK)
    bias[0] = jnp.where(band, 0.0, -jnp.inf)
    bias[1] = jnp.where(band & (ki >= CHUNK), 0.0, -jnp.inf)


def _unit_bias(bias, n, j):
    if isinstance(j, int) and j != 0:
        return bias[0]
    return bias[jnp.where(jnp.logical_and(n == 0, j == 0), 1, 0)]


def _attn_in_specs(heads):
    blk = (ATT_TILE, HEAD)
    prev = lambda n: jnp.maximum(n - 1, 0)
    return [
        pl.BlockSpec(blk, lambda h, n: (n, h)),
        pl.BlockSpec(blk, lambda h, n: (n, heads + h)),
        pl.BlockSpec(blk, lambda h, n: (prev(n), heads + h)),
        pl.BlockSpec(blk, lambda h, n: (n, 2 * heads + h)),
        pl.BlockSpec(blk, lambda h, n: (prev(n), 2 * heads + h)),
        pl.BlockSpec(blk, lambda h, n: (n, 0)),
        pl.BlockSpec(blk, lambda h, n: (n, 0)),
        pl.BlockSpec(blk, lambda h, n: (prev(n), 0)),
        pl.BlockSpec(blk, lambda h, n: (prev(n), 0)),
    ]


def _attn_load(q_ref, kc_ref, kp_ref, vc_ref, vp_ref, cc_ref, sc_ref, cp_ref, sp_ref, qr, kcat, vcat):
    qr[...] = _rope(q_ref[...], cc_ref[...], sc_ref[...])
    kcat[pl.ds(0, ATT_TILE), :] = _rope(kp_ref[...], cp_ref[...], sp_ref[...])
    kcat[pl.ds(ATT_TILE, ATT_TILE), :] = _rope(kc_ref[...], cc_ref[...], sc_ref[...])
    vcat[pl.ds(0, ATT_TILE), :] = vp_ref[...]
    vcat[pl.ds(ATT_TILE, ATT_TILE), :] = vc_ref[...]


def _attn_fwd(name, qkv, cosf, sins):
    t = qkv.shape[0]
    heads = qkv.shape[1] // (3 * HEAD)
    scale = HEAD ** -0.5
    nbr = len(DILATIONS)

    def body(q_ref, kc_ref, kp_ref, vc_ref, vp_ref, cc_ref, sc_ref, cp_ref, sp_ref, o_ref, lse_ref,
             qr, kcat, vcat, obr, lbr, bias):
        n = pl.program_id(1)
        _attn_load(q_ref, kc_ref, kp_ref, vc_ref, vp_ref, cc_ref, sc_ref, cp_ref, sp_ref, qr, kcat, vcat)
        _band_bias(bias)

        def visit(b, d, r, j):
            qv, qs = _unit_rows(qr, d, r, j, 1)
            kv, ks = _unit_rows(kcat, d, r, j, 2, ATT_TILE - CHUNK * d)
            vv, _ = _unit_rows(vcat, d, r, j, 2, ATT_TILE - CHUNK * d)
            s = lax.dot_general(qv[qs, :].astype(BF16), kv[ks, :].astype(BF16), NT_DIMS,
                                preferred_element_type=F32) * scale + _unit_bias(bias, n, j)
            mx = jnp.max(s, axis=-1, keepdims=True)
            p = jnp.exp(s - mx)
            den = jnp.sum(p, axis=-1, keepdims=True)
            o = jnp.dot((p * (1.0 / den)).astype(BF16), vv[ks, :].astype(BF16), preferred_element_type=F32)
            ov, _ = _unit_rows(obr.at[b], d, r, j, 1)
            lv, _ = _unit_rows(lbr.at[b], d, r, j, 1)
            ov[qs, :] = o
            lv[qs, :] = jnp.broadcast_to(mx + jnp.log(den), (CHUNK, HEAD))

        _attn_units(visit)
        ls = [lbr[b] for b in range(nbr)]
        top = functools.reduce(jnp.maximum, ls)
        ws = [jnp.exp(l - top) for l in ls]
        tot = functools.reduce(jnp.add, ws)
        o = (ws[0] / tot) * obr[0]
        for b in range(1, nbr):
            o = o + (ws[b] / tot) * obr[b]
        o_ref[...] = o.astype(BF16)
        lse_ref[...] = top + jnp.log(tot)

    blk = (ATT_TILE, HEAD)
    tile = pl.BlockSpec(blk, lambda h, n: (n, h))
    return pl.pallas_call(
        body, name=name, grid=(heads, t // ATT_TILE), in_specs=_attn_in_specs(heads),
        out_specs=[tile, tile],
        out_shape=[jax.ShapeDtypeStruct((t, heads * HEAD), BF16), jax.ShapeDtypeStruct((t, heads * HEAD), F32)],
        scratch_shapes=[pltpu.VMEM(blk, F32), pltpu.VMEM((2 * ATT_TILE, HEAD), F32), pltpu.VMEM((2 * ATT_TILE, HEAD), F32),
                        pltpu.VMEM((nbr,) + blk, F32), pltpu.VMEM((nbr,) + blk, F32),
                        pltpu.VMEM((2, CHUNK, 2 * CHUNK), F32)],
        compiler_params=_cp(("parallel", "parallel"), VMEM_MM),
    )(qkv, qkv, qkv, qkv, qkv, cosf, sins, cosf, sins)


def _attn_bwd(name, qkv, cosf, sins, d_o, o, lse):
    t = qkv.shape[0]
    heads = qkv.shape[1] // (3 * HEAD)
    scale = HEAD ** -0.5

    def body(q_ref, kc_ref, kp_ref, vc_ref, vp_ref, cc_ref, sc_ref, cp_ref, sp_ref, do_ref, o_ref, lse_ref,
             dq_ref, dko_ref, dkp_ref, dvo_ref, dvp_ref, qr, kcat, vcat, dq_acc, dk_acc, dv_acc, delta, bias):
        n = pl.program_id(1)
        _attn_load(q_ref, kc_ref, kp_ref, vc_ref, vp_ref, cc_ref, sc_ref, cp_ref, sp_ref, qr, kcat, vcat)
        _band_bias(bias)
        dq_acc[...] = jnp.zeros_like(dq_acc)
        dk_acc[...] = jnp.zeros_like(dk_acc)
        dv_acc[...] = jnp.zeros_like(dv_acc)
        delta[...] = jnp.broadcast_to(
            jnp.sum(do_ref[...] * o_ref[...].astype(F32), axis=-1, keepdims=True), delta.shape)

        def visit(b, d, r, j):
            qv, qs = _unit_rows(qr, d, r, j, 1)
            kv, ks = _unit_rows(kcat, d, r, j, 2, ATT_TILE - CHUNK * d)
            vv, _ = _unit_rows(vcat, d, r, j, 2, ATT_TILE - CHUNK * d)
            dov, _ = _unit_rows(do_ref, d, r, j, 1)
            lv, _ = _unit_rows(lse_ref, d, r, j, 1)
            dlv, _ = _unit_rows(delta, d, r, j, 1)
            q, k = qv[qs, :].astype(BF16), kv[ks, :].astype(BF16)
            do = dov[qs, :].astype(BF16)
            s = lax.dot_general(q, k, NT_DIMS, preferred_element_type=F32) * scale + _unit_bias(bias, n, j)
            p = jnp.exp(s - lv[qs, :][:, 0:1])
            dp = lax.dot_general(do, vv[ks, :].astype(BF16), NT_DIMS, preferred_element_type=F32)
            ds = (p * (dp - dlv[qs, :][:, 0:1]) * scale).astype(BF16)
            dqv, _ = _unit_rows(dq_acc, d, r, j, 1)
            dkv, _ = _unit_rows(dk_acc, d, r, j, 2, ATT_TILE - CHUNK * d)
            dvv, _ = _unit_rows(dv_acc, d, r, j, 2, ATT_TILE - CHUNK * d)
            dqv[qs, :] += jnp.dot(ds, k, preferred_element_type=F32)
            dkv[ks, :] += lax.dot_general(ds, q, TN_DIMS, preferred_element_type=F32)
            dvv[ks, :] += lax.dot_general(p.astype(BF16), do, TN_DIMS, preferred_element_type=F32)

        _attn_units(visit)
        dq_ref[...] = _rope_bwd(dq_acc[...], cc_ref[...], sc_ref[...])
        dkp_ref[...] = _rope_bwd(dk_acc[pl.ds(0, ATT_TILE), :], cp_ref[...], sp_ref[...])
        dko_ref[...] = _rope_bwd(dk_acc[pl.ds(ATT_TILE, ATT_TILE), :], cc_ref[...], sc_ref[...])
        dvp_ref[...] = dv_acc[pl.ds(0, ATT_TILE), :]
        dvo_ref[...] = dv_acc[pl.ds(ATT_TILE, ATT_TILE), :]

    blk = (ATT_TILE, HEAD)
    tile = pl.BlockSpec(blk, lambda h, n: (n, h))
    big = pltpu.VMEM((2 * ATT_TILE, HEAD), F32)
    return pl.pallas_call(
        body, name=name, grid=(heads, t // ATT_TILE), in_specs=_attn_in_specs(heads) + [tile, tile, tile],
        out_specs=[tile] * 5,
        out_shape=[jax.ShapeDtypeStruct((t, heads * HEAD), F32)] * 5,
        scratch_shapes=[pltpu.VMEM(blk, F32), big, big, pltpu.VMEM(blk, F32), big, big, pltpu.VMEM(blk, F32),
                        pltpu.VMEM((2, CHUNK, 2 * CHUNK), F32)],
        compiler_params=_cp(("parallel", "parallel"), 60 << 20),
    )(qkv, qkv, qkv, qkv, qkv, cosf, sins, cosf, sins, d_o, o, lse)


def _attn_merge(name, dq, dk_own, dk_prev, dv_own, dv_prev):
    t, hd = dq.shape
    nt = t // ATT_TILE
    tw = _tile(hd, 512)

    def body(dq_ref, dko_ref, dkn_ref, dvo_ref, dvn_ref, o_ref):
        last = pl.program_id(0) == nt - 1
        part = pl.program_id(1)

        @pl.when(part == 0)
        def _():
            o_ref[...] = dq_ref[...].astype(BF16)

        @pl.when(part == 1)
        def _():
            o_ref[...] = (dko_ref[...] + jnp.where(last, 0.0, dkn_ref[...])).astype(BF16)

        @pl.when(part == 2)
        def _():
            o_ref[...] = (dvo_ref[...] + jnp.where(last, 0.0, dvn_ref[...])).astype(BF16)

    def own(part):
        return pl.BlockSpec((ATT_TILE, tw), lambda n, p, c: (jnp.where(p == part, n, 0), jnp.where(p == part, c, 0)))

    def nxt(part):
        return pl.BlockSpec((ATT_TILE, tw), lambda n, p, c: (jnp.where(p == part, jnp.minimum(n + 1, nt - 1), 0),
                                                            jnp.where(p == part, c, 0)))

    per = hd // tw
    return pl.pallas_call(
        body, name=name, grid=(nt, 3, per), in_specs=[own(0), own(1), nxt(1), own(2), nxt(2)],
        out_specs=pl.BlockSpec((ATT_TILE, tw), lambda n, p, c: (n, p * per + c)),
        out_shape=jax.ShapeDtypeStruct((t, 3 * hd), BF16),
        compiler_params=_cp(("parallel", "parallel", "parallel"), VMEM_MM))(dq, dk_own, dk_prev, dv_own, dv_prev)


def _sum_parts(name, parts):
    nparts, rows, cols = parts.shape
    tr = _tile(rows, 256)

    def body(p_ref, o_ref):
        s = p_ref[0]
        for k in range(1, nparts):
            s = s + p_ref[k]
        o_ref[...] = s

    return pl.pallas_call(
        body, name=name, grid=(rows // tr,),
        in_specs=[pl.BlockSpec((nparts, tr, cols), lambda i: (0, i, 0))],
        out_specs=pl.BlockSpec((tr, cols), lambda i: (i, 0)),
        out_shape=jax.ShapeDtypeStruct((rows, cols), F32),
        compiler_params=_cp(("parallel",)))(parts)


def _rows128(a, pad_to=8):
    flat = a.reshape(-1)
    rows = -(-flat.shape[0] // 128)
    rows = -(-rows // pad_to) * pad_to
    flat = jnp.pad(flat, (0, rows * 128 - flat.shape[0]))
    return flat.reshape(rows, 128)


def _pack(arrays):
    return jnp.concatenate([_rows128(a) for a in arrays], axis=0)


def _unpack(packed, like):
    out, at = [], 0
    for a in like:
        size = 1
        for s in a.shape:
            size *= s
        rows = -(-(-(-size // 128)) // 8) * 8
        out.append(packed[at:at + rows].reshape(-1)[:size].reshape(a.shape))
        at += rows
    return out


def kernel(x, norm_mix_pre, norm_mix_post, norm_mlp_pre, norm_mlp_post, w_in_ab, w_spatial, b_spatial, conv_w, w_out_ab, w_qkv, w_o, w_up, w_down, loss_target, m_norm_mix_pre, m_norm_mix_post, m_norm_mlp_pre, m_norm_mlp_post, m_w_in_ab, m_w_spatial, m_b_spatial, m_conv_w, m_w_out_ab, m_w_qkv, m_w_o, m_w_up, m_w_down, v_norm_mix_pre, v_norm_mix_post, v_norm_mlp_pre, v_norm_mlp_post, v_w_in_ab, v_w_spatial, v_b_spatial, v_conv_w, v_w_out_ab, v_w_qkv, v_w_o, v_w_up, v_w_down):
    depth = norm_mix_pre.shape[0]
    seq, dm = x.shape[1], x.shape[2]
    h0 = x.reshape(seq, dm)
    target = loss_target.reshape(seq, dm)
    ax, ay, ac = lax.axis_index("x"), lax.axis_index("y"), lax.axis_index("c")
    my_block = 4 * ax + 2 * ay + ac
    block = jnp.reshape(my_block, (1,)).astype(jnp.int32)

    half = HEAD // 2
    inv_freq = ROPE_THETA ** (-jnp.arange(half, dtype=F32) * 2.0 / HEAD)
    ang = jnp.arange(seq, dtype=jnp.int32).astype(F32)[:, None] * inv_freq[None, :]
    cosf = jnp.concatenate([jnp.cos(ang), jnp.cos(ang)], axis=-1)
    sins = jnp.concatenate([-jnp.sin(ang), jnp.sin(ang)], axis=-1)

    big = {"w_in_ab": w_in_ab, "w_out_ab": w_out_ab, "w_qkv": w_qkv, "w_o": w_o, "w_up": w_up, "w_down": w_down}
    use_order = []
    for l in range(depth):
        use_order += [("w_in_ab", l // 2), ("w_out_ab", l // 2)] if l % 2 == 0 else [("w_qkv", l // 2), ("w_o", l // 2)]
        use_order += [("w_up", l), ("w_down", l)]
    n_even = w_in_ab.shape[0]
    cw_rows = jnp.pad(conv_w.reshape(n_even * CONV_TAPS, conv_w.shape[2]), ((0, HALO - (n_even * CONV_TAPS) % HALO), (0, 0)))
    cw_gathered = _all_gather("ag_conv", [cw_rows])[0]
    lands = [_cast_fill(f"cast_{nm}_{l}", big[nm], l, block) for nm, l in use_order]
    lands, ag_sems = _ag_start("ag_start", lands, cw_gathered)
    lands = dict(zip(use_order, lands))
    ag_sems = dict(zip(use_order, ag_sems))
    passed_on, wg = [], {}

    def weight(key, after):
        if key not in wg:
            upto = min(use_order.index(key) + 1, len(use_order) - 1)
            for k in use_order[len(passed_on):upto + 1]:
                lands[k] = _ag_mid(f"ag_mid_{k[0]}_{k[1]}", lands[k], ag_sems[k], after)
                passed_on.append(k)
            wg[key] = _ag_wait(f"ag_wait_{key[0]}_{key[1]}", lands[key], ag_sems[key], after)
        return wg[key]

    cw_all = cw_gathered[:, :n_even * CONV_TAPS].reshape(NDEV, n_even, CONV_TAPS, -1)
    cw_all = jnp.transpose(cw_all, (1, 2, 0, 3)).reshape(n_even, CONV_TAPS, -1)
    cw_full = [jnp.pad(cw_all[e], ((0, HALO - CONV_TAPS), (0, 0))) for e in range(n_even)]

    def rows_nat(blk):
        return blk.reshape(blk.shape[0] * blk.shape[1], blk.shape[2])

    saved = []
    hn = _norm_fwd("norm_first", h0, g_pre=norm_mix_pre[0][None])[0]
    h = h0
    for l in range(depth):
        s = {"h_in": h, "hn1": hn}
        if l % 2 == 0:
            e = l // 2
            proj = _mm_nn_blk(f"fwd_in_{l}", hn, weight(("w_in_ab", e), hn))[0]
            ab = _gate_fwd(f"gate_fwd_{l}", proj, w_spatial[e], b_spatial[e].T, cw_full[e])
            mix = _mm_nn(f"fwd_out_{l}", ab, rows_nat(weight(("w_out_ab", e), ab)))
            s.update(proj=proj, ab=ab)
        else:
            o_ = l // 2
            qkv = _mm_nn_blk(f"fwd_qkv_{l}", hn, weight(("w_qkv", o_), hn))[0]
            att, lse = _attn_fwd(f"attn_fwd_{l}", qkv, cosf, sins)
            mix = _mm_nn(f"fwd_o_{l}", att, rows_nat(weight(("w_o", o_), att)))
            s.update(qkv=qkv, att=att, lse=lse)
        h1, hn2 = _norm_fwd(f"norm_mid_{l}", h, mix, norm_mix_post[l][None], norm_mlp_pre[l][None])
        up, act = _mm_nn_blk(f"fwd_up_{l}", hn2, weight(("w_up", l), hn2), relu2=True)
        f = _mm_nn(f"fwd_down_{l}", act, rows_nat(weight(("w_down", l), act)))
        s.update(mix=mix, h1=h1, hn2=hn2, up=up, act=act, f=f)
        if l + 1 < depth:
            h, hn = _norm_fwd(f"norm_end_{l}", h1, f, norm_mlp_post[l][None], norm_mix_pre[l + 1][None])
        else:
            h = _norm_fwd(f"norm_end_{l}", h1, f, norm_mlp_post[l][None])[0]
        saved.append(s)

    d_h, loss_row = _loss_grad("loss", h, target)
    rs = {}

    def scatter(key, g):
        rs[key] = _rs_start(f"rs_start_{key[0]}_{key[1]}", g.reshape(NDEV, -1, g.shape[-1]))

    dg ={nm: [None] * depth for nm in ("norm_mix_pre", "norm_mix_post", "norm_mlp_pre", "norm_mlp_post")}
    d_ws, d_bs, d_cw = [None] * n_even, [None] * n_even, [None] * n_even
    d_hn_next = None
    for l in reversed(range(depth)):
        s = saved[l]
        if l == depth - 1:
            d_f, dg["norm_mlp_post"][l] = _norm_bwd(f"nb_end_{l}", d_h, post=(s["f"], norm_mlp_post[l][None]))
        else:
            d_h, dg["norm_mix_pre"][l + 1], d_f, dg["norm_mlp_post"][l] = _norm_bwd(
                f"nb_end_{l}", d_h, pre=(d_hn_next, saved[l + 1]["h_in"], norm_mix_pre[l + 1][None]),
                post=(s["f"], norm_mlp_post[l][None]))
        wd = rows_nat(wg[("w_down", l)])
        d_up = _mm_nt_rows(f"bwd_down_{l}", d_f, wd, up=s["up"])
        scatter(("w_down", l), _mm_tn(f"gw_down_{l}", s["act"], d_f))
        scatter(("w_up", l), _mm_tn(f"gw_up_{l}", s["hn2"], d_up, nb=w_up.shape[2]))
        d_hn2 = _mm_nt_blk(f"bwd_up_{l}", d_up, wg[("w_up", l)], after=[rs[("w_down", l)][0], rs[("w_up", l)][0]])
        d_h, dg["norm_mlp_pre"][l], d_mix, dg["norm_mix_post"][l] = _norm_bwd(
            f"nb_mid_{l}", d_h, pre=(d_hn2, s["h1"], norm_mlp_pre[l][None]),
            post=(s["mix"], norm_mix_post[l][None]))
        if l % 2 == 0:
            e = l // 2
            wo = rows_nat(wg[("w_out_ab", e)])
            d_ab = _mm_nt_rows(f"bwd_out_{l}", d_mix, wo)
            scatter(("w_out_ab", e), _mm_tn(f"gw_out_{l}", s["ab"], d_mix))
            d_proj, d_ws[e], d_bs[e], d_cw[e] = _gate_bwd(
                f"gate_bwd_{l}", s["proj"], d_ab, w_spatial[e], b_spatial[e].T, cw_full[e])
            scatter(("w_in_ab", e), _mm_tn(f"gw_in_{l}", s["hn1"], d_proj, nb=w_in_ab.shape[2]))
            d_hn_next = _mm_nt_blk(f"bwd_in_{l}", d_proj, wg[("w_in_ab", e)],
                                   after=[rs[("w_out_ab", e)][0], rs[("w_in_ab", e)][0]])
        else:
            o_ = l // 2
            wo = rows_nat(wg[("w_o", o_)])
            d_att = _mm_nt_rows(f"bwd_o_{l}", d_mix, wo)
            scatter(("w_o", o_), _mm_tn(f"gw_o_{l}", s["att"], d_mix))
            parts = _attn_bwd(f"attn_bwd_{l}", s["qkv"], cosf, sins, d_att, s["att"], s["lse"])
            d_qkv = _attn_merge(f"attn_merge_{l}", *parts)
            scatter(("w_qkv", o_), _mm_tn(f"gw_qkv_{l}", s["hn1"], d_qkv, nb=w_qkv.shape[2]))
            d_hn_next = _mm_nt_blk(f"bwd_qkv_{l}", d_qkv, wg[("w_qkv", o_)],
                                   after=[rs[("w_o", o_)][0], rs[("w_qkv", o_)][0]])
    grad_x, dg["norm_mix_pre"][0] = _norm_bwd("nb_first", d_h, pre=(d_hn_next, h0, norm_mix_pre[0][None]))

    moments = {"w_in_ab": (m_w_in_ab, v_w_in_ab), "w_out_ab": (m_w_out_ab, v_w_out_ab), "w_qkv": (m_w_qkv, v_w_qkv),
               "w_o": (m_w_o, v_w_o), "w_up": (m_w_up, v_w_up), "w_down": (m_w_down, v_w_down)}
    out_big = {}
    for nm in ("w_down", "w_up", "w_o", "w_qkv", "w_out_ab", "w_in_ab"):
        own, landed = [], []
        for l in range(big[nm].shape[0]):
            g, land = _wait_all(f"rs_wait_{nm}_{l}", *rs[(nm, l)], grad_x)
            own.append(g)
            landed.append(land)
        out_big[nm] = _adamw_layers(f"adamw_{nm}", own, landed, block, big[nm], moments[nm][0], moments[nm][1])

    small_g = ([jnp.concatenate(dg[nm], axis=0) for nm in dg]
               + [jnp.stack(d_ws), jnp.stack(d_bs), jnp.stack([c[:CONV_TAPS] for c in d_cw]), loss_row])
    packed = _pack(small_g)
    summed = _sum_parts("sum_small", _all_gather("ag_small", [packed])[0])
    g_nmp, g_nmo, g_nlp, g_nlo, g_ws, g_bs, g_cw_all, loss_sum = _unpack(summed, small_g)
    loss = loss_sum[0, 0]
    cwb = conv_w.shape[2]
    g_cw = lax.dynamic_slice_in_dim(g_cw_all, my_block * cwb, cwb, axis=2)
    small_w = [norm_mix_pre, norm_mix_post, norm_mlp_pre, norm_mlp_post, w_spatial, b_spatial, conv_w]
    small_m = [m_norm_mix_pre, m_norm_mix_post, m_norm_mlp_pre, m_norm_mlp_post, m_w_spatial, m_b_spatial, m_conv_w]
    small_v = [v_norm_mix_pre, v_norm_mix_post, v_norm_mlp_pre, v_norm_mlp_post, v_w_spatial, v_b_spatial, v_conv_w]
    small_grad = [g_nmp, g_nmo, g_nlp, g_nlo, g_ws, g_bs, g_cw]
    upd = _adamw("adamw_small", _pack(small_grad)[None], _pack(small_w), _pack(small_m), _pack(small_v))
    sg, sd, sm, sv = [_unpack(u, small_w) for u in upd]

    def outs(i_small, i_big):
        return (i_small[0], i_small[1], i_small[2], i_small[3], i_big["w_in_ab"], i_small[4], i_small[5], i_small[6],
                i_big["w_out_ab"], i_big["w_qkv"], i_big["w_o"], i_big["w_up"], i_big["w_down"])

    pick = lambda i: {nm: out_big[nm][i] for nm in big}
    return (loss, grad_x.reshape(x.shape), *outs(sg, pick(0)), *outs(sd, pick(1)), *outs(sm, pick(2)),
            *outs(sv, pick(3)))
```

```python
import functools

import jax
import jax.numpy as jnp
from jax import lax
from jax.experimental import pallas as pl
from jax.experimental.pallas import tpu as pltpu

F32 = jnp.float32
BF16 = jnp.bfloat16
MESH = pl.DeviceIdType.MESH
ANY = pl.BlockSpec(memory_space=pl.ANY)
HBM = pl.BlockSpec(memory_space=pltpu.HBM)
SEM = pl.BlockSpec(memory_space=pltpu.SEMAPHORE)
EFFECT = pltpu.SideEffectType.DATAFLOW_SIDE_EFFECTING

NDEV = 8
NCHIP = 4
RMS_EPS = 1e-6
LN_EPS = 1e-5
CHUNK = 128
HEAD = 128
ATT_TILE = 2048
ATT_UNROLL = 4
DILATIONS = (1, 4, 16)
ROPE_THETA = 10000.0
CONV_TAPS = 3
HALO = 8
GELU_C = 0.7978845608028654
GELU_A = 0.044715
ADAM_LR, ADAM_B1, ADAM_B2, ADAM_EPS, ADAM_WD, ADAM_STEP = 0.001, 0.9, 0.999, 1e-08, 0.01, 10
VMEM_MM = 52 << 20
VMEM_EW = 40 << 20


def _cp(sem=None, vmem=VMEM_EW):
    if sem is None:
        return pltpu.CompilerParams(vmem_limit_bytes=vmem)
    return pltpu.CompilerParams(dimension_semantics=sem, vmem_limit_bytes=vmem)


def _tile(n, want):
    return want if n % want == 0 else n


def _all_gather(name, shards):
    n = len(shards)

    def body(*refs):
        ins, outs = refs[:n], refs[n:2 * n]
        send_sems, recv_sems, local_sems = refs[2 * n:]
        x, y, c = lax.axis_index("x"), lax.axis_index("y"), lax.axis_index("c")
        me, sibling = (x, y, c), (x, y, 1 - c)
        chips = [(1 - x, y), (x, 1 - y), (1 - x, 1 - y)]

        def slot(p):
            return 4 * p[0] + 2 * p[1] + p[2]

        def copy(i, k, block, to, src=None):
            dst = outs[i].at[slot(block)]
            return pltpu.make_async_remote_copy(
                src_ref=dst if src is None else src, dst_ref=dst,
                send_sem=send_sems.at[i, k], recv_sem=recv_sems.at[i, k],
                device_id=to, device_id_type=MESH)

        mine = [pltpu.make_async_copy(ins[i], outs[i].at[slot(me)], local_sems.at[i]) for i in range(n)]
        for cp in mine:
            cp.start()
        first = []
        for i in range(n):
            first.append(copy(i, 0, me, sibling, src=ins[i]))
            for j, chip in enumerate(chips):
                first.append(copy(i, 1 + j, me, (*chip, c), src=ins[i]))
        for cp in first:
            cp.start()
        passed = []
        for j, chip in enumerate(chips):
            for i in range(n):
                copy(i, 1 + j, (*chip, c), me).wait_recv()
                fwd = copy(i, 4 + j, (*chip, c), sibling)
                fwd.start()
                passed.append(fwd)
        for i in range(n):
            copy(i, 0, sibling, me).wait_recv()
            for j, chip in enumerate(chips):
                copy(i, 4 + j, (*chip, 1 - c), me).wait_recv()
        for cp in first + passed:
            cp.wait_send()
        for cp in mine:
            cp.wait()

    return pl.pallas_call(
        body, name=name,
        out_shape=[jax.ShapeDtypeStruct((NDEV,) + s.shape, s.dtype) for s in shards],
        in_specs=[ANY] * n, out_specs=[ANY] * n,
        scratch_shapes=[pltpu.SemaphoreType.DMA((n, 7)), pltpu.SemaphoreType.DMA((n, 7)),
                        pltpu.SemaphoreType.DMA((n,))],
    )(*shards)


def _peer(x, y, c, r):
    return (1 - x if r & 4 else x, 1 - y if r & 2 else y, 1 - c if r & 1 else c)


def _slot(p):
    return 4 * p[0] + 2 * p[1] + p[2]


def _cast_fill(name, w, layer, block):
    _, rows, cols = w.shape
    tr = _tile(rows, 256)

    def body(blk_ref, w_ref, o_ref):
        o_ref[...] = w_ref[...].astype(BF16)

    return pl.pallas_call(
        body, name=name,
        grid_spec=pltpu.PrefetchScalarGridSpec(
            num_scalar_prefetch=1, grid=(rows // tr,),
            in_specs=[pl.BlockSpec((None, tr, cols), lambda i, blk: (layer, i, 0))],
            out_specs=pl.BlockSpec((None, tr, cols), lambda i, blk: (blk[0], i, 0))),
        out_shape=jax.ShapeDtypeStruct((NDEV, rows, cols), BF16),
        compiler_params=_cp(("parallel",)))(block, w)


OTHER_CHIPS = (2, 4, 6)


def _ag_start(name, lands, after):
    n = len(lands)

    def body(*refs):
        ins, sems = refs[:n], refs[n + 1:n + 1 + 4 * n]
        x, y, c = lax.axis_index("x"), lax.axis_index("y"), lax.axis_index("c")
        mine = _slot((x, y, c))
        for i in range(n):
            send_a, recv_a, _, recv_b = sems[4 * i:4 * i + 4]
            block = ins[i].at[mine]
            pltpu.make_async_remote_copy(src_ref=block, dst_ref=block, send_sem=send_a, recv_sem=recv_b,
                                         device_id=_peer(x, y, c, 1), device_id_type=MESH).start()
            for r in OTHER_CHIPS:
                pltpu.make_async_remote_copy(src_ref=block, dst_ref=block, send_sem=send_a, recv_sem=recv_a,
                                             device_id=_peer(x, y, c, r), device_id_type=MESH).start()

    outs = pl.pallas_call(
        body, name=name,
        out_shape=[pltpu.SemaphoreType.DMA(())] * (4 * n) + [pltpu.HBM(a.shape, a.dtype) for a in lands],
        in_specs=[HBM] * n + [ANY], out_specs=[SEM] * (4 * n) + [HBM] * n,
        input_output_aliases={i: 4 * n + i for i in range(n)},
        compiler_params=pltpu.CompilerParams(has_side_effects=EFFECT),
    )(*[pltpu.with_memory_space_constraint(a, pltpu.HBM) for a in lands], after)
    return outs[4 * n:], [tuple(outs[4 * i:4 * i + 4]) for i in range(n)]


def _ag_mid(name, land, sems, after):
    _, recv_a, send_b, recv_b = sems

    def body(land_ref, recv_a_ref, send_b_ref, recv_b_ref, after_ref, land_out):
        x, y, c = lax.axis_index("x"), lax.axis_index("y"), lax.axis_index("c")
        sibling = _peer(x, y, c, 1)
        three = land_ref.at[pl.ds(0, len(OTHER_CHIPS))]
        pltpu.make_async_remote_copy(src_ref=three, dst_ref=three, send_sem=send_b_ref, recv_sem=recv_a_ref,
                                     device_id=sibling, device_id_type=MESH).wait_recv()
        for r in OTHER_CHIPS:
            block = land_ref.at[_slot(_peer(x, y, c, r))]
            pltpu.make_async_remote_copy(src_ref=block, dst_ref=block, send_sem=send_b_ref, recv_sem=recv_b_ref,
                                         device_id=sibling, device_id_type=MESH).start()

    return pl.pallas_call(
        body, name=name, out_shape=pltpu.HBM(land.shape, land.dtype),
        in_specs=[HBM, SEM, SEM, SEM, ANY], out_specs=HBM, input_output_aliases={0: 0},
        compiler_params=pltpu.CompilerParams(has_side_effects=EFFECT),
    )(land, recv_a, send_b, recv_b, after)


def _ag_wait(name, land, sems, after):
    send_a, _, send_b, recv_b = sems

    def body(land_ref, send_a_ref, send_b_ref, recv_b_ref, after_ref, land_out):
        x, y, c = lax.axis_index("x"), lax.axis_index("y"), lax.axis_index("c")
        sibling = _peer(x, y, c, 1)
        four = land_ref.at[pl.ds(0, 1 + len(OTHER_CHIPS))]
        three = land_ref.at[pl.ds(0, len(OTHER_CHIPS))]
        first = pltpu.make_async_remote_copy(src_ref=four, dst_ref=four, send_sem=send_a_ref, recv_sem=recv_b_ref,
                                             device_id=sibling, device_id_type=MESH)
        passed = pltpu.make_async_remote_copy(src_ref=three, dst_ref=three, send_sem=send_b_ref, recv_sem=recv_b_ref,
                                              device_id=sibling, device_id_type=MESH)
        first.wait_send()
        passed.wait_send()
        first.wait_recv()

    return pl.pallas_call(
        body, name=name, out_shape=pltpu.HBM(land.shape, land.dtype),
        in_specs=[HBM, SEM, SEM, SEM, ANY], out_specs=HBM, input_output_aliases={0: 0},
        compiler_params=pltpu.CompilerParams(has_side_effects=EFFECT),
    )(land, send_a, send_b, recv_b, after)


def _wait_all(name, src, land, send, recv, after):
    def body(src_ref, land_ref, send_ref, recv_ref, after_ref, src_out, land_out):
        x, y, c = lax.axis_index("x"), lax.axis_index("y"), lax.axis_index("c")
        seven = land_ref.at[pl.ds(0, NDEV - 1)]
        copy = pltpu.make_async_remote_copy(src_ref=seven, dst_ref=seven, send_sem=send_ref, recv_sem=recv_ref,
                                            device_id=_peer(x, y, c, 1), device_id_type=MESH)
        copy.wait_send()
        copy.wait_recv()

    return pl.pallas_call(
        body, name=name,
        out_shape=[pltpu.HBM(src.shape, src.dtype), pltpu.HBM(land.shape, land.dtype)],
        in_specs=[HBM, HBM, SEM, SEM, ANY], out_specs=[HBM, HBM],
        input_output_aliases={0: 0, 1: 1},
        compiler_params=pltpu.CompilerParams(has_side_effects=EFFECT),
    )(src, land, send, recv, after)


def _rs_start(name, grad):
    land = lax.empty((NDEV - 1,) + grad.shape[1:], grad.dtype)

    def body(g_ref, land_ref, send, recv, g_out, land_out):
        x, y, c = lax.axis_index("x"), lax.axis_index("y"), lax.axis_index("c")
        for r in range(1, NDEV):
            peer = _peer(x, y, c, r)
            pltpu.make_async_remote_copy(
                src_ref=g_ref.at[_slot(peer)], dst_ref=land_ref.at[r - 1], send_sem=send, recv_sem=recv,
                device_id=peer, device_id_type=MESH).start()

    send, recv, g_thru, land_thru = pl.pallas_call(
        body, name=name,
        out_shape=[pltpu.SemaphoreType.DMA(()), pltpu.SemaphoreType.DMA(()),
                   pltpu.HBM(grad.shape, grad.dtype), pltpu.HBM(land.shape, land.dtype)],
        in_specs=[HBM, HBM], out_specs=[SEM, SEM, HBM, HBM], input_output_aliases={0: 2, 1: 3},
        compiler_params=pltpu.CompilerParams(has_side_effects=EFFECT),
    )(pltpu.with_memory_space_constraint(grad, pltpu.HBM), pltpu.with_memory_space_constraint(land, pltpu.HBM))
    return g_thru, land_thru, send, recv


def _adam_math(w, g, m, v):
    m = ADAM_B1 * m + (1.0 - ADAM_B1) * g
    v = ADAM_B2 * v + (1.0 - ADAM_B2) * (g * g)
    m_hat = m / (1.0 - ADAM_B1 ** ADAM_STEP)
    v_hat = v / (1.0 - ADAM_B2 ** ADAM_STEP)
    delta = -ADAM_LR * (m_hat / (jnp.sqrt(v_hat) + ADAM_EPS) + ADAM_WD * w)
    return delta, m, v


def _adamw(name, parts, w, m, v):
    nparts, rows, cols = parts.shape
    tr = _tile(rows, 256)

    def body(p_ref, w_ref, m_ref, v_ref, g_out, d_out, m_out, v_out):
        g = p_ref[0].astype(F32)
        for k in range(1, nparts):
            g = g + p_ref[k].astype(F32)
        delta, mn, vn = _adam_math(w_ref[...], g, m_ref[...], v_ref[...])
        g_out[...] = g
        d_out[...] = delta
        m_out[...] = mn
        v_out[...] = vn

    row = pl.BlockSpec((tr, cols), lambda i: (i, 0))
    return pl.pallas_call(
        body, name=name, grid=(rows // tr,),
        in_specs=[pl.BlockSpec((nparts, tr, cols), lambda i: (0, i, 0)), row, row, row],
        out_specs=[row] * 4,
        out_shape=[jax.ShapeDtypeStruct((rows, cols), F32)] * 4,
        compiler_params=_cp(("parallel",)),
    )(parts, w, m, v)


def _adamw_layers(name, grads, lands, block, w, m, v):
    layers, rows, cols = w.shape
    nland = lands[0].shape[0]
    tr = rows
    while tr % 2 == 0 and tr > 8 and nland * tr * cols * 2 > (1 << 20):
        tr //= 2

    def body(blk_ref, *refs):
        own_refs, land_refs = refs[:layers], refs[layers:2 * layers]
        w_ref, m_ref, v_ref, g_out, d_out, m_out, v_out = refs[2 * layers:]
        layer = pl.program_id(0)
        for k in range(layers):
            @pl.when(layer == k)
            def _(k=k):
                g = own_refs[k][...].astype(F32)
                for s in range(nland):
                    g = g + land_refs[k][s].astype(F32)
                delta, mn, vn = _adam_math(w_ref[...], g, m_ref[...], v_ref[...])
                g_out[...] = g
                d_out[...] = delta
                m_out[...] = mn
                v_out[...] = vn

    def own_spec(k):
        return pl.BlockSpec((None, tr, cols), lambda l, i, blk: (blk[0], jnp.where(l == k, i, 0), 0))

    def land_spec(k):
        return pl.BlockSpec((nland, tr, cols), lambda l, i, blk: (0, jnp.where(l == k, i, 0), 0))

    row = pl.BlockSpec((None, tr, cols), lambda l, i, blk: (l, i, 0))
    return pl.pallas_call(
        body, name=name,
        grid_spec=pltpu.PrefetchScalarGridSpec(
            num_scalar_prefetch=1, grid=(layers, rows // tr),
            in_specs=[own_spec(k) for k in range(layers)] + [land_spec(k) for k in range(layers)] + [row, row, row],
            out_specs=[row] * 4),
        out_shape=[jax.ShapeDtypeStruct((layers, rows, cols), F32)] * 4,
        compiler_params=_cp(("arbitrary", "arbitrary")),
    )(block, *grads, *lands, w, m, v)


def _norm_fwd(name, h, z=None, g_post=None, g_pre=None):
    rows, d = h.shape
    tm = _tile(rows, 256)
    has_post, has_pre = z is not None, g_pre is not None

    def body(*refs):
        it = iter(refs)
        hv = next(it)[...]
        if has_post:
            zv, gp = next(it)[...], next(it)[...]
        if has_pre:
            gq = next(it)[...]
        if has_post:
            r = lax.rsqrt(jnp.mean(zv * zv, axis=-1, keepdims=True) + RMS_EPS)
            hv = hv + (zv * r) * gp
            next(it)[...] = hv
        if has_pre:
            r = lax.rsqrt(jnp.mean(hv * hv, axis=-1, keepdims=True) + RMS_EPS)
            next(it)[...] = ((hv * r) * gq).astype(BF16)

    row = pl.BlockSpec((tm, d), lambda i: (i, 0))
    vec = pl.BlockSpec((1, d), lambda i: (0, 0))
    ins, in_specs, out_shape, out_specs = [h], [row], [], []
    if has_post:
        ins += [z, g_post]
        in_specs += [row, vec]
        out_shape.append(jax.ShapeDtypeStruct((rows, d), F32))
        out_specs.append(row)
    if has_pre:
        ins.append(g_pre)
        in_specs.append(vec)
        out_shape.append(jax.ShapeDtypeStruct((rows, d), BF16))
        out_specs.append(row)
    return pl.pallas_call(body, name=name, grid=(rows // tm,), in_specs=in_specs, out_specs=out_specs,
                          out_shape=out_shape, compiler_params=_cp(("parallel",)))(*ins)


def _rms_bwd_rows(x, g, dy):
    r = lax.rsqrt(jnp.mean(x * x, axis=-1, keepdims=True) + RMS_EPS)
    xn = x * r
    dg = jnp.sum(dy * xn, axis=0, keepdims=True)
    dxn = dy * g
    dx = r * (dxn - xn * jnp.mean(dxn * xn, axis=-1, keepdims=True))
    return dx, dg


def _norm_bwd(name, d_out, pre=None, post=None):
    rows, d = d_out.shape
    tm = _tile(rows, 256)
    has_pre, has_post = pre is not None, post is not None

    def body(*refs):
        it = iter(refs)
        dres = next(it)[...]
        if has_pre:
            dy, xp, gq = next(it)[...], next(it)[...], next(it)[...]
        if has_post:
            zv, gp = next(it)[...], next(it)[...]
        first = pl.program_id(0) == 0
        if has_pre:
            dx, dg = _rms_bwd_rows(xp, gq, dy)
            dres = dres + dx
            next(it)[...] = dres
            dg_ref = next(it)

            @pl.when(first)
            def _():
                dg_ref[...] = jnp.zeros_like(dg_ref)
            dg_ref[...] += dg
        if has_post:
            dz, dg2 = _rms_bwd_rows(zv, gp, dres)
            next(it)[...] = dz.astype(BF16)
            dg2_ref = next(it)

            @pl.when(first)
            def _():
                dg2_ref[...] = jnp.zeros_like(dg2_ref)
            dg2_ref[...] += dg2

    row = pl.BlockSpec((tm, d), lambda i: (i, 0))
    vec = pl.BlockSpec((1, d), lambda i: (0, 0))
    ins, in_specs, out_shape, out_specs = [d_out], [row], [], []
    if has_pre:
        ins += list(pre)
        in_specs += [row, row, vec]
        out_shape += [jax.ShapeDtypeStruct((rows, d), F32), jax.ShapeDtypeStruct((1, d), F32)]
        out_specs += [row, vec]
    if has_post:
        ins += list(post)
        in_specs += [row, vec]
        out_shape += [jax.ShapeDtypeStruct((rows, d), BF16), jax.ShapeDtypeStruct((1, d), F32)]
        out_specs += [row, vec]
    return pl.pallas_call(body, name=name, grid=(rows // tm,), in_specs=in_specs, out_specs=out_specs,
                          out_shape=out_shape, compiler_params=_cp(("arbitrary",)))(*ins)


def _loss_grad(name, y, target):
    rows, d = y.shape
    tm = _tile(rows, 256)

    def body(y_ref, t_ref, dy_ref, loss_ref):
        err = y_ref[...] - t_ref[...]
        dy_ref[...] = err * (1.0 / d)

        @pl.when(pl.program_id(0) == 0)
        def _():
            loss_ref[...] = jnp.zeros_like(loss_ref)
        loss_ref[...] += jnp.full(loss_ref.shape, (0.5 / d) * jnp.sum(err * err), F32)

    row = pl.BlockSpec((tm, d), lambda i: (i, 0))
    return pl.pallas_call(
        body, name=name, grid=(rows // tm,), in_specs=[row, row],
        out_specs=[row, pl.BlockSpec((1, 128), lambda i: (0, 0))],
        out_shape=[jax.ShapeDtypeStruct((rows, d), F32), jax.ShapeDtypeStruct((1, 128), F32)],
        compiler_params=_cp(("arbitrary",)))(y, target)


NT_DIMS = (((1,), (1,)), ((), ()))
TN_DIMS = (((0,), (0,)), ((), ()))


def _mm_nn_blk(name, a, wblk, relu2=False):
    m, k = a.shape
    nb = wblk.shape[2]
    tm = _tile(m, 1024)

    def body(a_ref, w_ref, o_ref):
        r = jnp.dot(a_ref[...], w_ref[...], preferred_element_type=F32)
        if relu2:
            rr = jnp.maximum(r, 0.0)
            o_ref[...] = (rr * rr).astype(BF16)
        else:
            o_ref[...] = r

    return pl.pallas_call(
        body, name=name, grid=(NDEV, m // tm),
        in_specs=[pl.BlockSpec((tm, k), lambda d, i: (i, 0)), pl.BlockSpec((None, k, nb), lambda d, i: (d, 0, 0))],
        out_specs=pl.BlockSpec((tm, nb), lambda d, i: (i, d)),
        out_shape=jax.ShapeDtypeStruct((m, NDEV * nb), BF16 if relu2 else F32),
        compiler_params=_cp(("parallel", "parallel"), VMEM_MM))(a, wblk)


def _mm_nn(name, a, w):
    m, kb = a.shape
    n = w.shape[1]
    tm, tk = _tile(m, 512), _tile(kb, 2048)

    def body(a_ref, w_ref, o_ref):
        r = jnp.dot(a_ref[...], w_ref[...], preferred_element_type=F32)
        step = pl.program_id(1)

        @pl.when(step == 0)
        def _():
            o_ref[...] = r

        @pl.when(step > 0)
        def _():
            o_ref[...] += r

    return pl.pallas_call(
        body, name=name, grid=(m // tm, kb // tk),
        in_specs=[pl.BlockSpec((tm, tk), lambda i, s: (i, s)), pl.BlockSpec((tk, n), lambda i, s: (s, 0))],
        out_specs=pl.BlockSpec((tm, n), lambda i, s: (i, 0)),
        out_shape=jax.ShapeDtypeStruct((m, n), F32),
        compiler_params=_cp(("parallel", "arbitrary"), VMEM_MM))(a, w)


def _mm_nt_rows(name, dy, w, act=None):
    m, n = dy.shape
    kw = w.shape[0]
    tm, tkw = _tile(m, 1024), _tile(kw, 1024)

    def body(dy_ref, w_ref, *rest):
        r = lax.dot_general(dy_ref[...], w_ref[...], NT_DIMS, preferred_element_type=F32)
        if act is None:
            rest[0][...] = r
        else:
            rest[1][...] = (r * (2.0 * jnp.sqrt(rest[0][...].astype(F32)))).astype(BF16)

    ins = [dy, w]
    in_specs = [pl.BlockSpec((tm, n), lambda j, i: (i, 0)), pl.BlockSpec((tkw, n), lambda j, i: (j, 0))]
    if act is not None:
        ins.append(act)
        in_specs.append(pl.BlockSpec((tm, tkw), lambda j, i: (i, j)))
    return pl.pallas_call(
        body, name=name, grid=(kw // tkw, m // tm), in_specs=in_specs,
        out_specs=pl.BlockSpec((tm, tkw), lambda j, i: (i, j)),
        out_shape=jax.ShapeDtypeStruct((m, kw), F32 if act is None else BF16),
        compiler_params=_cp(("parallel", "parallel"), VMEM_MM))(*ins)


def _mm_nt_blk(name, dy, wblk, after=None):
    m = dy.shape[0]
    _, kw, nb = wblk.shape
    tm, per = _tile(m, 512), 2

    def body(dy_ref, w_ref, *rest):
        o_ref = rest[-1]
        r = lax.dot_general(dy_ref[:, :nb], w_ref[0], NT_DIMS, preferred_element_type=F32)
        for t in range(1, per):
            r = r + lax.dot_general(dy_ref[:, t * nb:(t + 1) * nb], w_ref[t], NT_DIMS, preferred_element_type=F32)
        step = pl.program_id(1)

        @pl.when(step == 0)
        def _():
            o_ref[...] = r

        @pl.when(step > 0)
        def _():
            o_ref[...] += r

    extra = list(after or ())
    return pl.pallas_call(
        body, name=name, grid=(m // tm, NDEV // per),
        in_specs=[pl.BlockSpec((tm, per * nb), lambda i, s: (i, s)),
                  pl.BlockSpec((per, kw, nb), lambda i, s: (s, 0, 0))] + [ANY] * len(extra),
        out_specs=pl.BlockSpec((tm, kw), lambda i, s: (i, 0)),
        out_shape=jax.ShapeDtypeStruct((m, kw), F32),
        compiler_params=_cp(("parallel", "arbitrary"), VMEM_MM))(dy, wblk, *extra)


def _mm_tn(name, x, dy, nb=None):
    t, mx = x.shape
    n = dy.shape[1]
    tmx = _tile(mx, 512)
    tn = nb if nb is not None else _tile(n, 1024)

    def body(x_ref, dy_ref, o_ref):
        o_ref[...] = lax.dot_general(x_ref[...], dy_ref[...], TN_DIMS, preferred_element_type=F32).astype(BF16)

    if nb is None:
        out_shape = jax.ShapeDtypeStruct((mx, n), BF16)
        out_spec = pl.BlockSpec((tmx, tn), lambda j, i: (i, j))
    else:
        out_shape = jax.ShapeDtypeStruct((NDEV, mx, nb), BF16)
        out_spec = pl.BlockSpec((None, tmx, nb), lambda j, i: (j, i, 0))
    return pl.pallas_call(
        body, name=name, grid=(n // tn, mx // tmx),
        in_specs=[pl.BlockSpec((t, tmx), lambda j, i: (0, i)), pl.BlockSpec((t, tn), lambda j, i: (0, j))],
        out_specs=out_spec, out_shape=out_shape,
        compiler_params=_cp(("parallel", "parallel"), VMEM_MM))(x, dy)


def _gelu(x):
    return 0.5 * x * (1.0 + jnp.tanh(GELU_C * (x + GELU_A * (x * x * x))))


def _gelu_grad(x):
    t = jnp.tanh(GELU_C * (x + GELU_A * (x * x * x)))
    return 0.5 * (1.0 + t) + 0.5 * x * (1.0 - t * t) * (GELU_C * (1.0 + 3.0 * GELU_A * (x * x)))


def _layernorm(a):
    mu = jnp.mean(a, axis=-1, keepdims=True)
    ac = a - mu
    rstd = lax.rsqrt(jnp.mean(ac * ac, axis=-1, keepdims=True) + LN_EPS)
    return ac * rstd, rstd


def _shift_rows(z, halo, k):
    zr = pltpu.roll(z, k, 0)
    hr = pltpu.roll(halo, k, 0)
    row = lax.broadcasted_iota(jnp.int32, hr.shape, 0)
    top = jnp.where(row < k, hr, zr[:HALO])
    return jnp.concatenate([top, zr[HALO:]], axis=0)


def _shift_rows_up(z, halo, k):
    rows = z.shape[0]
    zr = pltpu.roll(z, rows - k, 0)
    hr = pltpu.roll(halo, HALO - k, 0)
    row = lax.broadcasted_iota(jnp.int32, hr.shape, 0)
    bot = jnp.where(row >= HALO - k, hr, zr[rows - HALO:])
    return jnp.concatenate([zr[:rows - HALO], bot], axis=0)


def _causal_mask():
    t = lax.broadcasted_iota(jnp.int32, (CHUNK, CHUNK), 0)
    s = lax.broadcasted_iota(jnp.int32, (CHUNK, CHUNK), 1)
    return s <= t


def _gate_specs(tm, rows, width):
    per = tm // HALO
    last = rows // HALO - 1
    cur = pl.BlockSpec((tm, width), lambda i: (i, 0))
    prev = pl.BlockSpec((HALO, width), lambda i: (jnp.maximum(i * per - 1, 0), 0))
    nxt = pl.BlockSpec((HALO, width), lambda i: (jnp.minimum((i + 1) * per, last), 0))
    return cur, prev, nxt


def _gate_fwd(name, proj, w_s, b_st, cw):
    rows, width = proj.shape
    w = width // 5
    groups = w // CHUNK
    tm = _tile(rows, 256)
    cur, prev, _ = _gate_specs(tm, rows, width)

    def body(p_ref, h_ref, ws_ref, b_ref, cw_ref, o_ref):
        mask = _causal_mask()
        au = _gelu(p_ref[:, 0:w])
        vn, _ = _layernorm(_gelu(p_ref[:, w:2 * w]))
        vn = vn.astype(BF16)
        for g in range(groups):
            wc = jnp.where(mask, ws_ref[g], 0.0).astype(BF16)
            cols = slice(g * CHUNK, (g + 1) * CHUNK)
            for ch in range(tm // CHUNK):
                rws = slice(ch * CHUNK, (ch + 1) * CHUNK)
                mixed = jnp.dot(wc, vn[rws, cols], preferred_element_type=F32) + b_ref[:, g:g + 1]
                o_ref[rws, cols] = (au[rws, cols] * mixed).astype(BF16)
        z = p_ref[:, 3 * w:4 * w] * p_ref[:, 4 * w:5 * w]
        zh = h_ref[:, 3 * w:4 * w] * h_ref[:, 4 * w:5 * w]
        zh = jnp.where(pl.program_id(0) == 0, 0.0, zh)
        y = cw_ref[0:1, :] * _shift_rows(z, zh, 2) + cw_ref[1:2, :] * _shift_rows(z, zh, 1) + cw_ref[2:3, :] * z
        o_ref[:, w:2 * w] = (p_ref[:, 2 * w:3 * w] * y).astype(BF16)

    full = lambda a: pl.BlockSpec(a.shape, lambda i: (0,) * a.ndim)
    return pl.pallas_call(
        body, name=name, grid=(rows // tm,),
        in_specs=[cur, prev, full(w_s), full(b_st), full(cw)],
        out_specs=pl.BlockSpec((tm, 2 * w), lambda i: (i, 0)),
        out_shape=jax.ShapeDtypeStruct((rows, 2 * w), BF16),
        compiler_params=_cp(("parallel",)))(proj, proj, w_s, b_st, cw)


def _gate_bwd(name, proj, d_ab, w_s, b_st, cw):
    rows, width = proj.shape
    w = width // 5
    groups = w // CHUNK
    tm = _tile(rows, 256)
    cur, prev, nxt = _gate_specs(tm, rows, width)
    dcur, _, dnxt = _gate_specs(tm, rows, 2 * w)

    def body(p_ref, ph_ref, pn_ref, d_ref, dn_ref, ws_ref, b_ref, cw_ref, o_ref, dws_ref, dbs_ref, dcw_ref):
        i = pl.program_id(0)

        @pl.when(i == 0)
        def _():
            dws_ref[...] = jnp.zeros_like(dws_ref)
            dbs_ref[...] = jnp.zeros_like(dbs_ref)
            dcw_ref[...] = jnp.zeros_like(dcw_ref)

        mask = _causal_mask()
        u, v = p_ref[:, 0:w], p_ref[:, w:2 * w]
        au, av = _gelu(u), _gelu(v)
        vn, rstd = _layernorm(av)
        vnb = vn.astype(BF16)
        d_a = d_ref[:, 0:w]
        d_mixed = (d_a * au).astype(BF16)
        ones = jnp.ones((HALO, CHUNK), BF16)
        d_vn_cols = []
        d_au_cols = []
        for g in range(groups):
            wc = jnp.where(mask, ws_ref[g], 0.0).astype(BF16)
            cols = slice(g * CHUNK, (g + 1) * CHUNK)
            dw = jnp.zeros((CHUNK, CHUNK), F32)
            db = jnp.zeros((HALO, CHUNK), F32)
            d_vn_rows, d_au_rows = [], []
            for ch in range(tm // CHUNK):
                rws = slice(ch * CHUNK, (ch + 1) * CHUNK)
                mixed = jnp.dot(wc, vnb[rws, cols], preferred_element_type=F32) + b_ref[:, g:g + 1]
                d_au_rows.append(d_a[rws, cols] * mixed)
                dm = d_mixed[rws, cols]
                dw = dw + lax.dot_general(dm, vnb[rws, cols], NT_DIMS, preferred_element_type=F32)
                db = db + lax.dot_general(ones, dm, NT_DIMS, preferred_element_type=F32)
                d_vn_rows.append(lax.dot_general(wc, dm, TN_DIMS, preferred_element_type=F32))
            dws_ref[g] += jnp.where(mask, dw, 0.0)
            dbs_ref[g:g + 1, :] += db[0:1, :]
            d_vn_cols.append(jnp.concatenate(d_vn_rows, axis=0))
            d_au_cols.append(jnp.concatenate(d_au_rows, axis=0))
        d_vn = jnp.concatenate(d_vn_cols, axis=1)
        d_au = jnp.concatenate(d_au_cols, axis=1)
        d_av = rstd * (d_vn - jnp.mean(d_vn, axis=-1, keepdims=True)
                       - vn * jnp.mean(d_vn * vn, axis=-1, keepdims=True))
        o_ref[:, 0:w] = (d_au * _gelu_grad(u)).astype(BF16)
        o_ref[:, w:2 * w] = (d_av * _gelu_grad(v)).astype(BF16)

        gb, gc, bx = p_ref[:, 2 * w:3 * w], p_ref[:, 3 * w:4 * w], p_ref[:, 4 * w:5 * w]
        z = gc * bx
        zh = jnp.where(i == 0, 0.0, ph_ref[:, 3 * w:4 * w] * ph_ref[:, 4 * w:5 * w])
        z1, z2 = _shift_rows(z, zh, 1), _shift_rows(z, zh, 2)
        d_b = d_ref[:, w:2 * w]
        y = cw_ref[0:1, :] * z2 + cw_ref[1:2, :] * z1 + cw_ref[2:3, :] * z
        dy = d_b * gb
        dyn = jnp.where(i == pl.num_programs(0) - 1, 0.0, dn_ref[:, w:2 * w] * pn_ref[:, 2 * w:3 * w])
        dz = (cw_ref[2:3, :] * dy + cw_ref[1:2, :] * _shift_rows_up(dy, dyn, 1)
              + cw_ref[0:1, :] * _shift_rows_up(dy, dyn, 2))
        dcw_ref[0:1, :] += jnp.sum(dy * z2, axis=0, keepdims=True)
        dcw_ref[1:2, :] += jnp.sum(dy * z1, axis=0, keepdims=True)
        dcw_ref[2:3, :] += jnp.sum(dy * z, axis=0, keepdims=True)
        o_ref[:, 2 * w:3 * w] = (d_b * y).astype(BF16)
        o_ref[:, 3 * w:4 * w] = (dz * bx).astype(BF16)
        o_ref[:, 4 * w:5 * w] = (dz * gc).astype(BF16)

    full = lambda a: pl.BlockSpec(a.shape, lambda i: (0,) * a.ndim)
    acc = lambda shape: pl.BlockSpec(shape, lambda i: (0,) * len(shape))
    return pl.pallas_call(
        body, name=name, grid=(rows // tm,),
        in_specs=[cur, prev, nxt, dcur, dnxt, full(w_s), full(b_st), full(cw)],
        out_specs=[pl.BlockSpec((tm, width), lambda i: (i, 0)), acc((groups, CHUNK, CHUNK)),
                   acc((groups, CHUNK)), acc((HALO, w))],
        out_shape=[jax.ShapeDtypeStruct((rows, width), BF16), jax.ShapeDtypeStruct((groups, CHUNK, CHUNK), F32),
                   jax.ShapeDtypeStruct((groups, CHUNK), F32), jax.ShapeDtypeStruct((HALO, w), F32)],
        compiler_params=_cp(("arbitrary",), VMEM_MM))(proj, proj, proj, d_ab, d_ab, w_s, b_st, cw)


def _rope(t, cosf, sins):
    return t * cosf + pltpu.roll(t, HEAD // 2, 1) * sins


def _rope_bwd(dt, cosf, sins):
    return dt * cosf + pltpu.roll(dt * sins, HEAD // 2, 1)


def _attn_units(visit):
    for b, d in enumerate(DILATIONS):
        blocks = ATT_TILE // (CHUNK * d)
        for r in range(d):
            if blocks <= ATT_UNROLL:
                for j in range(blocks):
                    visit(b, d, r, j)
            else:
                def step(jj, carry, b=b, d=d, r=r):
                    for u in range(ATT_UNROLL):
                        visit(b, d, r, jj * ATT_UNROLL + u)
                    return carry
                lax.fori_loop(0, blocks // ATT_UNROLL, step, 0)


def _unit_rows(ref, d, r, j, nblk, offset=0):
    base = offset + j * (CHUNK * d)
    if not isinstance(base, int):
        base = pl.multiple_of(base, CHUNK)
    return ref.at[pl.ds(base, nblk * CHUNK * d)], pl.ds(r, nblk * CHUNK, stride=d)


def _band_bias(bias):
    qi = lax.broadcasted_iota(jnp.int32, (CHUNK, 2 * CHUNK), 0)
    ki = lax.broadcasted_iota(jnp.int32, (CHUNK, 2 * CHUNK), 1)
    band = (ki >= qi) & (ki <= qi + CHUNK)
    bias[0] = jnp.where(band, 0.0, -jnp.inf)
    bias[1] = jnp.where(band & (ki >= CHUNK), 0.0, -jnp.inf)


def _unit_bias(bias, n, j):
    if isinstance(j, int) and j != 0:
        return bias[0]
    return bias[jnp.where(jnp.logical_and(n == 0, j == 0), 1, 0)]


def _attn_in_specs(heads):
    blk = (ATT_TILE, HEAD)
    prev = lambda n: jnp.maximum(n - 1, 0)
    return [
        pl.BlockSpec(blk, lambda h, n: (n, h)),
        pl.BlockSpec(blk, lambda h, n: (n, heads + h)),
        pl.BlockSpec(blk, lambda h, n: (prev(n), heads + h)),
        pl.BlockSpec(blk, lambda h, n: (n, 2 * heads + h)),
        pl.BlockSpec(blk, lambda h, n: (prev(n), 2 * heads + h)),
        pl.BlockSpec(blk, lambda h, n: (n, 0)),
        pl.BlockSpec(blk, lambda h, n: (n, 0)),
        pl.BlockSpec(blk, lambda h, n: (prev(n), 0)),
        pl.BlockSpec(blk, lambda h, n: (prev(n), 0)),
    ]


def _attn_load(q_ref, kc_ref, kp_ref, vc_ref, vp_ref, cc_ref, sc_ref, cp_ref, sp_ref, qr, kcat, vcat):
    qr[...] = _rope(q_ref[...], cc_ref[...], sc_ref[...])
    kcat[pl.ds(0, ATT_TILE), :] = _rope(kp_ref[...], cp_ref[...], sp_ref[...])
    kcat[pl.ds(ATT_TILE, ATT_TILE), :] = _rope(kc_ref[...], cc_ref[...], sc_ref[...])
    vcat[pl.ds(0, ATT_TILE), :] = vp_ref[...]
    vcat[pl.ds(ATT_TILE, ATT_TILE), :] = vc_ref[...]


def _attn_fwd(name, qkv, cosf, sins):
    t = qkv.shape[0]
    heads = qkv.shape[1] // (3 * HEAD)
    scale = HEAD ** -0.5
    nbr = len(DILATIONS)

    def body(q_ref, kc_ref, kp_ref, vc_ref, vp_ref, cc_ref, sc_ref, cp_ref, sp_ref, o_ref, lse_ref,
             qr, kcat, vcat, obr, lbr, bias):
        n = pl.program_id(1)
        _attn_load(q_ref, kc_ref, kp_ref, vc_ref, vp_ref, cc_ref, sc_ref, cp_ref, sp_ref, qr, kcat, vcat)
        _band_bias(bias)

        def visit(b, d, r, j):
            qv, qs = _unit_rows(qr, d, r, j, 1)
            kv, ks = _unit_rows(kcat, d, r, j, 2, ATT_TILE - CHUNK * d)
            vv, _ = _unit_rows(vcat, d, r, j, 2, ATT_TILE - CHUNK * d)
            s = lax.dot_general(qv[qs, :].astype(BF16), kv[ks, :].astype(BF16), NT_DIMS,
                                preferred_element_type=F32) * scale + _unit_bias(bias, n, j)
            mx = jnp.max(s, axis=-1, keepdims=True)
            p = jnp.exp(s - mx)
            den = jnp.sum(p, axis=-1, keepdims=True)
            o = jnp.dot((p * (1.0 / den)).astype(BF16), vv[ks, :].astype(BF16), preferred_element_type=F32)
            ov, _ = _unit_rows(obr.at[b], d, r, j, 1)
            lv, _ = _unit_rows(lbr.at[b], d, r, j, 1)
            ov[qs, :] = o
            lv[qs, :] = jnp.broadcast_to(mx + jnp.log(den), (CHUNK, HEAD))

        _attn_units(visit)
        ls = [lbr[b] for b in range(nbr)]
        top = functools.reduce(jnp.maximum, ls)
        ws = [jnp.exp(l - top) for l in ls]
        tot = functools.reduce(jnp.add, ws)
        o = (ws[0] / tot) * obr[0]
        for b in range(1, nbr):
            o = o + (ws[b] / tot) * obr[b]
        o_ref[...] = o.astype(BF16)
        lse_ref[...] = top + jnp.log(tot)

    blk = (ATT_TILE, HEAD)
    tile = pl.BlockSpec(blk, lambda h, n: (n, h))
    return pl.pallas_call(
        body, name=name, grid=(heads, t // ATT_TILE), in_specs=_attn_in_specs(heads),
        out_specs=[tile, tile],
        out_shape=[jax.ShapeDtypeStruct((t, heads * HEAD), BF16), jax.ShapeDtypeStruct((t, heads * HEAD), F32)],
        scratch_shapes=[pltpu.VMEM(blk, F32), pltpu.VMEM((2 * ATT_TILE, HEAD), F32), pltpu.VMEM((2 * ATT_TILE, HEAD), F32),
                        pltpu.VMEM((nbr,) + blk, F32), pltpu.VMEM((nbr,) + blk, F32),
                        pltpu.VMEM((2, CHUNK, 2 * CHUNK), F32)],
        compiler_params=_cp(("parallel", "parallel"), VMEM_MM),
    )(qkv, qkv, qkv, qkv, qkv, cosf, sins, cosf, sins)


def _attn_bwd(name, qkv, cosf, sins, d_o, o, lse):
    t = qkv.shape[0]
    heads = qkv.shape[1] // (3 * HEAD)
    scale = HEAD ** -0.5

    def body(q_ref, kc_ref, kp_ref, vc_ref, vp_ref, cc_ref, sc_ref, cp_ref, sp_ref, do_ref, o_ref, lse_ref,
             dq_ref, dko_ref, dkp_ref, dvo_ref, dvp_ref, qr, kcat, vcat, dq_acc, dk_acc, dv_acc, delta, bias):
        n = pl.program_id(1)
        _attn_load(q_ref, kc_ref, kp_ref, vc_ref, vp_ref, cc_ref, sc_ref, cp_ref, sp_ref, qr, kcat, vcat)
        _band_bias(bias)
        dq_acc[...] = jnp.zeros_like(dq_acc)
        dk_acc[...] = jnp.zeros_like(dk_acc)
        dv_acc[...] = jnp.zeros_like(dv_acc)
        delta[...] = jnp.broadcast_to(
            jnp.sum(do_ref[...] * o_ref[...].astype(F32), axis=-1, keepdims=True), delta.shape)

        def visit(b, d, r, j):
            qv, qs = _unit_rows(qr, d, r, j, 1)
            kv, ks = _unit_rows(kcat, d, r, j, 2, ATT_TILE - CHUNK * d)
            vv, _ = _unit_rows(vcat, d, r, j, 2, ATT_TILE - CHUNK * d)
            dov, _ = _unit_rows(do_ref, d, r, j, 1)
            lv, _ = _unit_rows(lse_ref, d, r, j, 1)
            dlv, _ = _unit_rows(delta, d, r, j, 1)
            q, k = qv[qs, :].astype(BF16), kv[ks, :].astype(BF16)
            do = dov[qs, :].astype(BF16)
            s = lax.dot_general(q, k, NT_DIMS, preferred_element_type=F32) * scale + _unit_bias(bias, n, j)
            p = jnp.exp(s - lv[qs, :][:, 0:1])
            dp = lax.dot_general(do, vv[ks, :].astype(BF16), NT_DIMS, preferred_element_type=F32)
            ds = (p * (dp - dlv[qs, :][:, 0:1]) * scale).astype(BF16)
            dqv, _ = _unit_rows(dq_acc, d, r, j, 1)
            dkv, _ = _unit_rows(dk_acc, d, r, j, 2, ATT_TILE - CHUNK * d)
            dvv, _ = _unit_rows(dv_acc, d, r, j, 2, ATT_TILE - CHUNK * d)
            dqv[qs, :] += jnp.dot(ds, k, preferred_element_type=F32)
            dkv[ks, :] += lax.dot_general(ds, q, TN_DIMS, preferred_element_type=F32)
            dvv[ks, :] += lax.dot_general(p.astype(BF16), do, TN_DIMS, preferred_element_type=F32)

        _attn_units(visit)
        dq_ref[...] = _rope_bwd(dq_acc[...], cc_ref[...], sc_ref[...])
        dkp_ref[...] = _rope_bwd(dk_acc[pl.ds(0, ATT_TILE), :], cp_ref[...], sp_ref[...])
        dko_ref[...] = _rope_bwd(dk_acc[pl.ds(ATT_TILE, ATT_TILE), :], cc_ref[...], sc_ref[...])
        dvp_ref[...] = dv_acc[pl.ds(0, ATT_TILE), :]
        dvo_ref[...] = dv_acc[pl.ds(ATT_TILE, ATT_TILE), :]

    blk = (ATT_TILE, HEAD)
    tile = pl.BlockSpec(blk, lambda h, n: (n, h))
    big = pltpu.VMEM((2 * ATT_TILE, HEAD), F32)
    return pl.pallas_call(
        body, name=name, grid=(heads, t // ATT_TILE), in_specs=_attn_in_specs(heads) + [tile, tile, tile],
        out_specs=[tile] * 5,
        out_shape=[jax.ShapeDtypeStruct((t, heads * HEAD), F32)] * 5,
        scratch_shapes=[pltpu.VMEM(blk, F32), big, big, pltpu.VMEM(blk, F32), big, big, pltpu.VMEM(blk, F32),
                        pltpu.VMEM((2, CHUNK, 2 * CHUNK), F32)],
        compiler_params=_cp(("parallel", "parallel"), 60 << 20),
    )(qkv, qkv, qkv, qkv, qkv, cosf, sins, cosf, sins, d_o, o, lse)


def _attn_merge(name, dq, dk_own, dk_prev, dv_own, dv_prev):
    t, hd = dq.shape
    nt = t // ATT_TILE
    tw = _tile(hd, 512)

    def body(dq_ref, dko_ref, dkn_ref, dvo_ref, dvn_ref, o_ref):
        last = pl.program_id(0) == nt - 1
        part = pl.program_id(1)

        @pl.when(part == 0)
        def _():
            o_ref[...] = dq_ref[...].astype(BF16)

        @pl.when(part == 1)
        def _():
            o_ref[...] = (dko_ref[...] + jnp.where(last, 0.0, dkn_ref[...])).astype(BF16)

        @pl.when(part == 2)
        def _():
            o_ref[...] = (dvo_ref[...] + jnp.where(last, 0.0, dvn_ref[...])).astype(BF16)

    def own(part):
        return pl.BlockSpec((ATT_TILE, tw), lambda n, p, c: (jnp.where(p == part, n, 0), jnp.where(p == part, c, 0)))

    def nxt(part):
        return pl.BlockSpec((ATT_TILE, tw), lambda n, p, c: (jnp.where(p == part, jnp.minimum(n + 1, nt - 1), 0),
                                                            jnp.where(p == part, c, 0)))

    per = hd // tw
    return pl.pallas_call(
        body, name=name, grid=(nt, 3, per), in_specs=[own(0), own(1), nxt(1), own(2), nxt(2)],
        out_specs=pl.BlockSpec((ATT_TILE, tw), lambda n, p, c: (n, p * per + c)),
        out_shape=jax.ShapeDtypeStruct((t, 3 * hd), BF16),
        compiler_params=_cp(("parallel", "parallel", "parallel"), VMEM_MM))(dq, dk_own, dk_prev, dv_own, dv_prev)


def _sum_parts(name, parts):
    nparts, rows, cols = parts.shape
    tr = _tile(rows, 256)

    def body(p_ref, o_ref):
        s = p_ref[0]
        for k in range(1, nparts):
            s = s + p_ref[k]
        o_ref[...] = s

    return pl.pallas_call(
        body, name=name, grid=(rows // tr,),
        in_specs=[pl.BlockSpec((nparts, tr, cols), lambda i: (0, i, 0))],
        out_specs=pl.BlockSpec((tr, cols), lambda i: (i, 0)),
        out_shape=jax.ShapeDtypeStruct((rows, cols), F32),
        compiler_params=_cp(("parallel",)))(parts)


def _rows128(a, pad_to=8):
    flat = a.reshape(-1)
    rows = -(-flat.shape[0] // 128)
    rows = -(-rows // pad_to) * pad_to
    flat = jnp.pad(flat, (0, rows * 128 - flat.shape[0]))
    return flat.reshape(rows, 128)


def _pack(arrays):
    return jnp.concatenate([_rows128(a) for a in arrays], axis=0)


def _unpack(packed, like):
    out, at = [], 0
    for a in like:
        size = 1
        for s in a.shape:
            size *= s
        rows = -(-(-(-size // 128)) // 8) * 8
        out.append(packed[at:at + rows].reshape(-1)[:size].reshape(a.shape))
        at += rows
    return out


def kernel(x, norm_mix_pre, norm_mix_post, norm_mlp_pre, norm_mlp_post, w_in_ab, w_spatial, b_spatial, conv_w, w_out_ab, w_qkv, w_o, w_up, w_down, loss_target, m_norm_mix_pre, m_norm_mix_post, m_norm_mlp_pre, m_norm_mlp_post, m_w_in_ab, m_w_spatial, m_b_spatial, m_conv_w, m_w_out_ab, m_w_qkv, m_w_o, m_w_up, m_w_down, v_norm_mix_pre, v_norm_mix_post, v_norm_mlp_pre, v_norm_mlp_post, v_w_in_ab, v_w_spatial, v_b_spatial, v_conv_w, v_w_out_ab, v_w_qkv, v_w_o, v_w_up, v_w_down):
    depth = norm_mix_pre.shape[0]
    seq, dm = x.shape[1], x.shape[2]
    h0 = x.reshape(seq, dm)
    target = loss_target.reshape(seq, dm)
    ax, ay, ac = lax.axis_index("x"), lax.axis_index("y"), lax.axis_index("c")
    my_block = 4 * ax + 2 * ay + ac
    block = jnp.reshape(my_block, (1,)).astype(jnp.int32)

    half = HEAD // 2
    inv_freq = ROPE_THETA ** (-jnp.arange(half, dtype=F32) * 2.0 / HEAD)
    ang = jnp.arange(seq, dtype=jnp.int32).astype(F32)[:, None] * inv_freq[None, :]
    cosf = jnp.concatenate([jnp.cos(ang), jnp.cos(ang)], axis=-1)
    sins = jnp.concatenate([-jnp.sin(ang), jnp.sin(ang)], axis=-1)

    big = {"w_in_ab": w_in_ab, "w_out_ab": w_out_ab, "w_qkv": w_qkv, "w_o": w_o, "w_up": w_up, "w_down": w_down}
    use_order = []
    for l in range(depth):
        use_order += [("w_in_ab", l // 2), ("w_out_ab", l // 2)] if l % 2 == 0 else [("w_qkv", l // 2), ("w_o", l // 2)]
        use_order += [("w_up", l), ("w_down", l)]
    n_even = w_in_ab.shape[0]
    cw_rows = jnp.pad(conv_w.reshape(n_even * CONV_TAPS, conv_w.shape[2]), ((0, HALO - (n_even * CONV_TAPS) % HALO), (0, 0)))
    cw_gathered = _all_gather("ag_conv", [cw_rows])[0]
    first = [k for k in use_order if k in (("w_in_ab", 0), ("w_out_ab", 0), ("w_up", 0), ("w_down", 0))]
    rest = [k for k in use_order if k not in first]
    lands_a, sems_a = _ag_start("ag_start_first", [_cast_fill(f"cast_{nm}_{l}", big[nm], l, block) for nm, l in first],
                                cw_gathered)
    lands_b, sems_b = _ag_start("ag_start_rest", [_cast_fill(f"cast_{nm}_{l}", big[nm], l, block) for nm, l in rest],
                                lands_a[0])
    lands = dict(zip(first + rest, list(lands_a) + list(lands_b)))
    ag_sems = dict(zip(first + rest, list(sems_a) + list(sems_b)))
    passed_on, wg = [], {}

    def weight(key, after):
        if key not in wg:
            upto = min(use_order.index(key) + 2, len(use_order) - 1)
            for k in use_order[len(passed_on):upto + 1]:
                lands[k] = _ag_mid(f"ag_mid_{k[0]}_{k[1]}", lands[k], ag_sems[k], after)
                passed_on.append(k)
            wg[key] = _ag_wait(f"ag_wait_{key[0]}_{key[1]}", lands[key], ag_sems[key], after)
        return wg[key]

    cw_all = cw_gathered[:, :n_even * CONV_TAPS].reshape(NDEV, n_even, CONV_TAPS, -1)
    cw_all = jnp.transpose(cw_all, (1, 2, 0, 3)).reshape(n_even, CONV_TAPS, -1)
    cw_full = [jnp.pad(cw_all[e], ((0, HALO - CONV_TAPS), (0, 0))) for e in range(n_even)]

    def rows_nat(blk):
        return blk.reshape(blk.shape[0] * blk.shape[1], blk.shape[2])

    saved = []
    hn = _norm_fwd("norm_first", h0, g_pre=norm_mix_pre[0][None])[0]
    h = h0
    for l in range(depth):
        s = {"h_in": h, "hn1": hn}
        if l % 2 == 0:
            e = l // 2
            proj = _mm_nn_blk(f"fwd_in_{l}", hn, weight(("w_in_ab", e), hn))
            ab = _gate_fwd(f"gate_fwd_{l}", proj, w_spatial[e], b_spatial[e].T, cw_full[e])
            mix = _mm_nn(f"fwd_out_{l}", ab, rows_nat(weight(("w_out_ab", e), ab)))
            s.update(proj=proj, ab=ab)
        else:
            o_ = l // 2
            qkv = _mm_nn_blk(f"fwd_qkv_{l}", hn, weight(("w_qkv", o_), hn))
            att, lse = _attn_fwd(f"attn_fwd_{l}", qkv, cosf, sins)
            mix = _mm_nn(f"fwd_o_{l}", att, rows_nat(weight(("w_o", o_), att)))
            s.update(qkv=qkv, att=att, lse=lse)
        h1, hn2 = _norm_fwd(f"norm_mid_{l}", h, mix, norm_mix_post[l][None], norm_mlp_pre[l][None])
        act = _mm_nn_blk(f"fwd_up_{l}", hn2, weight(("w_up", l), hn2), relu2=True)
        f = _mm_nn(f"fwd_down_{l}", act, rows_nat(weight(("w_down", l), act)))
        s.update(mix=mix, h1=h1, hn2=hn2, act=act, f=f)
        if l + 1 < depth:
            h, hn = _norm_fwd(f"norm_end_{l}", h1, f, norm_mlp_post[l][None], norm_mix_pre[l + 1][None])
        else:
            h = _norm_fwd(f"norm_end_{l}", h1, f, norm_mlp_post[l][None])[0]
        saved.append(s)

    d_h, loss_row = _loss_grad("loss", h, target)
    rs = {}

    def scatter(key, g):
        rs[key] = _rs_start(f"rs_start_{key[0]}_{key[1]}", g.reshape(NDEV, -1, g.shape[-1]))

    dg ={nm: [None] * depth for nm in ("norm_mix_pre", "norm_mix_post", "norm_mlp_pre", "norm_mlp_post")}
    d_ws, d_bs, d_cw = [None] * n_even, [None] * n_even, [None] * n_even
    d_hn_next = None
    for l in reversed(range(depth)):
        s = saved[l]
        if l == depth - 1:
            d_f, dg["norm_mlp_post"][l] = _norm_bwd(f"nb_end_{l}", d_h, post=(s["f"], norm_mlp_post[l][None]))
        else:
            d_h, dg["norm_mix_pre"][l + 1], d_f, dg["norm_mlp_post"][l] = _norm_bwd(
                f"nb_end_{l}", d_h, pre=(d_hn_next, saved[l + 1]["h_in"], norm_mix_pre[l + 1][None]),
                post=(s["f"], norm_mlp_post[l][None]))
        wd = rows_nat(wg[("w_down", l)])
        d_up = _mm_nt_rows(f"bwd_down_{l}", d_f, wd, act=s["act"])
        scatter(("w_down", l), _mm_tn(f"gw_down_{l}", s["act"], d_f))
        scatter(("w_up", l), _mm_tn(f"gw_up_{l}", s["hn2"], d_up, nb=w_up.shape[2]))
        d_hn2 = _mm_nt_blk(f"bwd_up_{l}", d_up, wg[("w_up", l)], after=[rs[("w_down", l)][0], rs[("w_up", l)][0]])
        d_h, dg["norm_mlp_pre"][l], d_mix, dg["norm_mix_post"][l] = _norm_bwd(
            f"nb_mid_{l}", d_h, pre=(d_hn2, s["h1"], norm_mlp_pre[l][None]),
            post=(s["mix"], norm_mix_post[l][None]))
        if l % 2 == 0:
            e = l // 2
            wo = rows_nat(wg[("w_out_ab", e)])
            d_ab = _mm_nt_rows(f"bwd_out_{l}", d_mix, wo)
            scatter(("w_out_ab", e), _mm_tn(f"gw_out_{l}", s["ab"], d_mix))
            d_proj, d_ws[e], d_bs[e], d_cw[e] = _gate_bwd(
                f"gate_bwd_{l}", s["proj"], d_ab, w_spatial[e], b_spatial[e].T, cw_full[e])
            scatter(("w_in_ab", e), _mm_tn(f"gw_in_{l}", s["hn1"], d_proj, nb=w_in_ab.shape[2]))
            d_hn_next = _mm_nt_blk(f"bwd_in_{l}", d_proj, wg[("w_in_ab", e)],
                                   after=[rs[("w_out_ab", e)][0], rs[("w_in_ab", e)][0]])
        else:
            o_ = l // 2
            wo = rows_nat(wg[("w_o", o_)])
            d_att = _mm_nt_rows(f"bwd_o_{l}", d_mix, wo)
            scatter(("w_o", o_), _mm_tn(f"gw_o_{l}", s["att"], d_mix))
            parts = _attn_bwd(f"attn_bwd_{l}", s["qkv"], cosf, sins, d_att, s["att"], s["lse"])
            d_qkv = _attn_merge(f"attn_merge_{l}", *parts)
            scatter(("w_qkv", o_), _mm_tn(f"gw_qkv_{l}", s["hn1"], d_qkv, nb=w_qkv.shape[2]))
            d_hn_next = _mm_nt_blk(f"bwd_qkv_{l}", d_qkv, wg[("w_qkv", o_)],
                                   after=[rs[("w_o", o_)][0], rs[("w_qkv", o_)][0]])
    grad_x, dg["norm_mix_pre"][0] = _norm_bwd("nb_first", d_h, pre=(d_hn_next, h0, norm_mix_pre[0][None]))

    moments = {"w_in_ab": (m_w_in_ab, v_w_in_ab), "w_out_ab": (m_w_out_ab, v_w_out_ab), "w_qkv": (m_w_qkv, v_w_qkv),
               "w_o": (m_w_o, v_w_o), "w_up": (m_w_up, v_w_up), "w_down": (m_w_down, v_w_down)}
    out_big = {}
    for nm in ("w_down", "w_up", "w_o", "w_qkv", "w_out_ab", "w_in_ab"):
        own, landed = [], []
        for l in range(big[nm].shape[0]):
            g, land = _wait_all(f"rs_wait_{nm}_{l}", *rs[(nm, l)], grad_x)
            own.append(g)
            landed.append(land)
        out_big[nm] = _adamw_layers(f"adamw_{nm}", own, landed, block, big[nm], moments[nm][0], moments[nm][1])

    small_g = ([jnp.concatenate(dg[nm], axis=0) for nm in dg]
               + [jnp.stack(d_ws), jnp.stack(d_bs), jnp.stack([c[:CONV_TAPS] for c in d_cw]), loss_row])
    packed = _pack(small_g)
    summed = _sum_parts("sum_small", _all_gather("ag_small", [packed])[0])
    g_nmp, g_nmo, g_nlp, g_nlo, g_ws, g_bs, g_cw_all, loss_sum = _unpack(summed, small_g)
    loss = loss_sum[0, 0]
    cwb = conv_w.shape[2]
    g_cw = lax.dynamic_slice_in_dim(g_cw_all, my_block * cwb, cwb, axis=2)
    small_w = [norm_mix_pre, norm_mix_post, norm_mlp_pre, norm_mlp_post, w_spatial, b_spatial, conv_w]
    small_m = [m_norm_mix_pre, m_norm_mix_post, m_norm_mlp_pre, m_norm_mlp_post, m_w_spatial, m_b_spatial, m_conv_w]
    small_v = [v_norm_mix_pre, v_norm_mix_post, v_norm_mlp_pre, v_norm_mlp_post, v_w_spatial, v_b_spatial, v_conv_w]
    small_grad = [g_nmp, g_nmo, g_nlp, g_nlo, g_ws, g_bs, g_cw]
    upd = _adamw("adamw_small", _pack(small_grad)[None], _pack(small_w), _pack(small_m), _pack(small_v))
    sg, sd, sm, sv = [_unpack(u, small_w) for u in upd]

    def outs(i_small, i_big):
        return (i_small[0], i_small[1], i_small[2], i_small[3], i_big["w_in_ab"], i_small[4], i_small[5], i_small[6],
                i_big["w_out_ab"], i_big["w_qkv"], i_big["w_o"], i_big["w_up"], i_big["w_down"])

    pick = lambda i: {nm: out_big[nm][i] for nm in big}
    return (loss, grad_x.reshape(x.shape), *outs(sg, pick(0)), *outs(sd, pick(1)), *outs(sm, pick(2)),
            *outs(sv, pick(3)))
```

```python
import functools

import jax
import jax.numpy as jnp
from jax import lax
from jax.experimental import pallas as pl
from jax.experimental.pallas import tpu as pltpu

F32 = jnp.float32
BF16 = jnp.bfloat16
MESH = pl.DeviceIdType.MESH
ANY = pl.BlockSpec(memory_space=pl.ANY)
HBM = pl.BlockSpec(memory_space=pltpu.HBM)
SEM = pl.BlockSpec(memory_space=pltpu.SEMAPHORE)
EFFECT = pltpu.SideEffectType.DATAFLOW_SIDE_EFFECTING

NDEV = 8
NCHIP = 4
RMS_EPS = 1e-6
LN_EPS = 1e-5
CHUNK = 128
HEAD = 128
ATT_TILE = 2048
ATT_UNROLL = 4
DILATIONS = (1, 4, 16)
ROPE_THETA = 10000.0
CONV_TAPS = 3
HALO = 8
HALO_BF16 = 16
GELU_C = 0.7978845608028654
GELU_A = 0.044715
ADAM_LR, ADAM_B1, ADAM_B2, ADAM_EPS, ADAM_WD, ADAM_STEP = 0.001, 0.9, 0.999, 1e-08, 0.01, 10
VMEM_MM = 52 << 20
VMEM_EW = 40 << 20


def _cp(sem=None, vmem=VMEM_EW):
    if sem is None:
        return pltpu.CompilerParams(vmem_limit_bytes=vmem)
    return pltpu.CompilerParams(dimension_semantics=sem, vmem_limit_bytes=vmem)


def _tile(n, want):
    return want if n % want == 0 else n


def _all_gather(name, shards, after=()):
    n = len(shards)
    after = list(after)

    def body(*refs):
        ins, outs = refs[:n], refs[n + len(after):2 * n + len(after)]
        send_sems, recv_sems, local_sems = refs[2 * n + len(after):]
        x, y, c = lax.axis_index("x"), lax.axis_index("y"), lax.axis_index("c")
        me, sibling = (x, y, c), (x, y, 1 - c)
        chips = [(1 - x, y), (x, 1 - y), (1 - x, 1 - y)]

        def slot(p):
            return 4 * p[0] + 2 * p[1] + p[2]

        def copy(i, k, block, to, src=None):
            dst = outs[i].at[slot(block)]
            return pltpu.make_async_remote_copy(
                src_ref=dst if src is None else src, dst_ref=dst,
                send_sem=send_sems.at[i, k], recv_sem=recv_sems.at[i, k],
                device_id=to, device_id_type=MESH)

        mine = [pltpu.make_async_copy(ins[i], outs[i].at[slot(me)], local_sems.at[i]) for i in range(n)]
        for cp in mine:
            cp.start()
        first = []
        for i in range(n):
            first.append(copy(i, 0, me, sibling, src=ins[i]))
            for j, chip in enumerate(chips):
                first.append(copy(i, 1 + j, me, (*chip, c), src=ins[i]))
        for cp in first:
            cp.start()
        passed = []
        for j, chip in enumerate(chips):
            for i in range(n):
                copy(i, 1 + j, (*chip, c), me).wait_recv()
                fwd = copy(i, 4 + j, (*chip, c), sibling)
                fwd.start()
                passed.append(fwd)
        for i in range(n):
            copy(i, 0, sibling, me).wait_recv()
            for j, chip in enumerate(chips):
                copy(i, 4 + j, (*chip, 1 - c), me).wait_recv()
        for cp in first + passed:
            cp.wait_send()
        for cp in mine:
            cp.wait()

    return pl.pallas_call(
        body, name=name,
        out_shape=[jax.ShapeDtypeStruct((NDEV,) + s.shape, s.dtype) for s in shards],
        in_specs=[ANY] * (n + len(after)), out_specs=[ANY] * n,
        scratch_shapes=[pltpu.SemaphoreType.DMA((n, 7)), pltpu.SemaphoreType.DMA((n, 7)),
                        pltpu.SemaphoreType.DMA((n,))],
    )(*shards, *after)


def _peer(x, y, c, r):
    return (1 - x if r & 4 else x, 1 - y if r & 2 else y, 1 - c if r & 1 else c)


def _slot(p):
    return 4 * p[0] + 2 * p[1] + p[2]


def _cast_fill(name, w, layer, block):
    _, rows, cols = w.shape
    tr = _tile(rows, 256)

    def body(blk_ref, w_ref, o_ref):
        o_ref[...] = w_ref[...].astype(BF16)

    return pl.pallas_call(
        body, name=name,
        grid_spec=pltpu.PrefetchScalarGridSpec(
            num_scalar_prefetch=1, grid=(rows // tr,),
            in_specs=[pl.BlockSpec((None, tr, cols), lambda i, blk: (layer, i, 0))],
            out_specs=pl.BlockSpec((None, tr, cols), lambda i, blk: (blk[0], i, 0))),
        out_shape=jax.ShapeDtypeStruct((NDEV, rows, cols), BF16),
        compiler_params=_cp(("parallel",)))(block, w)


OTHER_CHIPS = (2, 4, 6)


def _ag_start(name, lands, after):
    n = len(lands)

    def body(*refs):
        ins, sems = refs[:n], refs[n + 1:n + 1 + 4 * n]
        x, y, c = lax.axis_index("x"), lax.axis_index("y"), lax.axis_index("c")
        mine = _slot((x, y, c))
        for i in range(n):
            send_a, recv_a, _, recv_b = sems[4 * i:4 * i + 4]
            block = ins[i].at[mine]
            pltpu.make_async_remote_copy(src_ref=block, dst_ref=block, send_sem=send_a, recv_sem=recv_b,
                                         device_id=_peer(x, y, c, 1), device_id_type=MESH).start()
            for r in OTHER_CHIPS:
                pltpu.make_async_remote_copy(src_ref=block, dst_ref=block, send_sem=send_a, recv_sem=recv_a,
                                             device_id=_peer(x, y, c, r), device_id_type=MESH).start()

    outs = pl.pallas_call(
        body, name=name,
        out_shape=[pltpu.SemaphoreType.DMA(())] * (4 * n) + [pltpu.HBM(a.shape, a.dtype) for a in lands],
        in_specs=[HBM] * n + [ANY], out_specs=[SEM] * (4 * n) + [HBM] * n,
        input_output_aliases={i: 4 * n + i for i in range(n)},
        compiler_params=pltpu.CompilerParams(has_side_effects=EFFECT),
    )(*[pltpu.with_memory_space_constraint(a, pltpu.HBM) for a in lands], after)
    return outs[4 * n:], [tuple(outs[4 * i:4 * i + 4]) for i in range(n)]


def _ag_mid(name, land, sems, after):
    _, recv_a, send_b, recv_b = sems

    def body(land_ref, recv_a_ref, send_b_ref, recv_b_ref, after_ref, land_out):
        x, y, c = lax.axis_index("x"), lax.axis_index("y"), lax.axis_index("c")
        sibling = _peer(x, y, c, 1)
        three = land_ref.at[pl.ds(0, len(OTHER_CHIPS))]
        pltpu.make_async_remote_copy(src_ref=three, dst_ref=three, send_sem=send_b_ref, recv_sem=recv_a_ref,
                                     device_id=sibling, device_id_type=MESH).wait_recv()
        for r in OTHER_CHIPS:
            block = land_ref.at[_slot(_peer(x, y, c, r))]
            pltpu.make_async_remote_copy(src_ref=block, dst_ref=block, send_sem=send_b_ref, recv_sem=recv_b_ref,
                                         device_id=sibling, device_id_type=MESH).start()

    return pl.pallas_call(
        body, name=name, out_shape=pltpu.HBM(land.shape, land.dtype),
        in_specs=[HBM, SEM, SEM, SEM, ANY], out_specs=HBM, input_output_aliases={0: 0},
        compiler_params=pltpu.CompilerParams(has_side_effects=EFFECT),
    )(land, recv_a, send_b, recv_b, after)


def _ag_wait(name, land, sems, after):
    send_a, _, send_b, recv_b = sems

    def body(land_ref, send_a_ref, send_b_ref, recv_b_ref, after_ref, land_out):
        x, y, c = lax.axis_index("x"), lax.axis_index("y"), lax.axis_index("c")
        sibling = _peer(x, y, c, 1)
        four = land_ref.at[pl.ds(0, 1 + len(OTHER_CHIPS))]
        three = land_ref.at[pl.ds(0, len(OTHER_CHIPS))]
        first = pltpu.make_async_remote_copy(src_ref=four, dst_ref=four, send_sem=send_a_ref, recv_sem=recv_b_ref,
                                             device_id=sibling, device_id_type=MESH)
        passed = pltpu.make_async_remote_copy(src_ref=three, dst_ref=three, send_sem=send_b_ref, recv_sem=recv_b_ref,
                                              device_id=sibling, device_id_type=MESH)
        first.wait_send()
        passed.wait_send()
        first.wait_recv()

    return pl.pallas_call(
        body, name=name, out_shape=pltpu.HBM(land.shape, land.dtype),
        in_specs=[HBM, SEM, SEM, SEM, ANY], out_specs=HBM, input_output_aliases={0: 0},
        compiler_params=pltpu.CompilerParams(has_side_effects=EFFECT),
    )(land, send_a, send_b, recv_b, after)


def _wait_all(name, src, land, send, recv, after):
    def body(src_ref, land_ref, send_ref, recv_ref, after_ref, src_out, land_out):
        x, y, c = lax.axis_index("x"), lax.axis_index("y"), lax.axis_index("c")
        seven = land_ref.at[pl.ds(0, NDEV - 1)]
        copy = pltpu.make_async_remote_copy(src_ref=seven, dst_ref=seven, send_sem=send_ref, recv_sem=recv_ref,
                                            device_id=_peer(x, y, c, 1), device_id_type=MESH)
        copy.wait_send()
        copy.wait_recv()

    return pl.pallas_call(
        body, name=name,
        out_shape=[pltpu.HBM(src.shape, src.dtype), pltpu.HBM(land.shape, land.dtype)],
        in_specs=[HBM, HBM, SEM, SEM, ANY], out_specs=[HBM, HBM],
        input_output_aliases={0: 0, 1: 1},
        compiler_params=pltpu.CompilerParams(has_side_effects=EFFECT),
    )(src, land, send, recv, after)


def _rs_start(name, grad):
    land = lax.empty((NDEV - 1,) + grad.shape[1:], grad.dtype)

    def body(g_ref, land_ref, send, recv, g_out, land_out):
        x, y, c = lax.axis_index("x"), lax.axis_index("y"), lax.axis_index("c")
        for r in range(1, NDEV):
            peer = _peer(x, y, c, r)
            pltpu.make_async_remote_copy(
                src_ref=g_ref.at[_slot(peer)], dst_ref=land_ref.at[r - 1], send_sem=send, recv_sem=recv,
                device_id=peer, device_id_type=MESH).start()

    send, recv, g_thru, land_thru = pl.pallas_call(
        body, name=name,
        out_shape=[pltpu.SemaphoreType.DMA(()), pltpu.SemaphoreType.DMA(()),
                   pltpu.HBM(grad.shape, grad.dtype), pltpu.HBM(land.shape, land.dtype)],
        in_specs=[HBM, HBM], out_specs=[SEM, SEM, HBM, HBM], input_output_aliases={0: 2, 1: 3},
        compiler_params=pltpu.CompilerParams(has_side_effects=EFFECT),
    )(pltpu.with_memory_space_constraint(grad, pltpu.HBM), pltpu.with_memory_space_constraint(land, pltpu.HBM))
    return g_thru, land_thru, send, recv


def _adam_math(w, g, m, v):
    m = ADAM_B1 * m + (1.0 - ADAM_B1) * g
    v = ADAM_B2 * v + (1.0 - ADAM_B2) * (g * g)
    m_hat = m / (1.0 - ADAM_B1 ** ADAM_STEP)
    v_hat = v / (1.0 - ADAM_B2 ** ADAM_STEP)
    delta = -ADAM_LR * (m_hat / (jnp.sqrt(v_hat) + ADAM_EPS) + ADAM_WD * w)
    return delta, m, v


def _adamw(name, parts, w, m, v):
    nparts, rows, cols = parts.shape
    tr = _tile(rows, 256)

    def body(p_ref, w_ref, m_ref, v_ref, g_out, d_out, m_out, v_out):
        g = p_ref[0].astype(F32)
        for k in range(1, nparts):
            g = g + p_ref[k].astype(F32)
        delta, mn, vn = _adam_math(w_ref[...], g, m_ref[...], v_ref[...])
        g_out[...] = g
        d_out[...] = delta
        m_out[...] = mn
        v_out[...] = vn

    row = pl.BlockSpec((tr, cols), lambda i: (i, 0))
    return pl.pallas_call(
        body, name=name, grid=(rows // tr,),
        in_specs=[pl.BlockSpec((nparts, tr, cols), lambda i: (0, i, 0)), row, row, row],
        out_specs=[row] * 4,
        out_shape=[jax.ShapeDtypeStruct((rows, cols), F32)] * 4,
        compiler_params=_cp(("parallel",)),
    )(parts, w, m, v)


def _adamw_layers(name, grads, lands, block, w, m, v):
    layers, rows, cols = w.shape
    nland = lands[0].shape[0]
    tr = rows
    while tr % 2 == 0 and tr > 8 and nland * tr * cols * 2 > (1 << 20):
        tr //= 2

    def body(blk_ref, *refs):
        own_refs, land_refs = refs[:layers], refs[layers:2 * layers]
        w_ref, m_ref, v_ref, g_out, d_out, m_out, v_out = refs[2 * layers:]
        layer = pl.program_id(0)
        for k in range(layers):
            @pl.when(layer == k)
            def _(k=k):
                g = own_refs[k][...].astype(F32)
                for s in range(nland):
                    g = g + land_refs[k][s].astype(F32)
                delta, mn, vn = _adam_math(w_ref[...], g, m_ref[...], v_ref[...])
                g_out[...] = g
                d_out[...] = delta
                m_out[...] = mn
                v_out[...] = vn

    def own_spec(k):
        return pl.BlockSpec((None, tr, cols), lambda l, i, blk: (blk[0], jnp.where(l == k, i, 0), 0))

    def land_spec(k):
        return pl.BlockSpec((nland, tr, cols), lambda l, i, blk: (0, jnp.where(l == k, i, 0), 0))

    row = pl.BlockSpec((None, tr, cols), lambda l, i, blk: (l, i, 0))
    return pl.pallas_call(
        body, name=name,
        grid_spec=pltpu.PrefetchScalarGridSpec(
            num_scalar_prefetch=1, grid=(layers, rows // tr),
            in_specs=[own_spec(k) for k in range(layers)] + [land_spec(k) for k in range(layers)] + [row, row, row],
            out_specs=[row] * 4),
        out_shape=[jax.ShapeDtypeStruct((layers, rows, cols), F32)] * 4,
        compiler_params=_cp(("arbitrary", "arbitrary")),
    )(block, *grads, *lands, w, m, v)


def _norm_fwd(name, h, z=None, g_post=None, g_pre=None):
    rows, d = h.shape
    tm = _tile(rows, 256)
    has_post, has_pre = z is not None, g_pre is not None

    def body(*refs):
        it = iter(refs)
        hv = next(it)[...]
        if has_post:
            zv, gp = next(it)[...].astype(F32), next(it)[...]
        if has_pre:
            gq = next(it)[...]
        if has_post:
            r = lax.rsqrt(jnp.mean(zv * zv, axis=-1, keepdims=True) + RMS_EPS)
            hv = hv + (zv * r) * gp
            next(it)[...] = hv
        if has_pre:
            r = lax.rsqrt(jnp.mean(hv * hv, axis=-1, keepdims=True) + RMS_EPS)
            next(it)[...] = ((hv * r) * gq).astype(BF16)

    row = pl.BlockSpec((tm, d), lambda i: (i, 0))
    vec = pl.BlockSpec((1, d), lambda i: (0, 0))
    ins, in_specs, out_shape, out_specs = [h], [row], [], []
    if has_post:
        ins += [z, g_post]
        in_specs += [row, vec]
        out_shape.append(jax.ShapeDtypeStruct((rows, d), F32))
        out_specs.append(row)
    if has_pre:
        ins.append(g_pre)
        in_specs.append(vec)
        out_shape.append(jax.ShapeDtypeStruct((rows, d), BF16))
        out_specs.append(row)
    return pl.pallas_call(body, name=name, grid=(rows // tm,), in_specs=in_specs, out_specs=out_specs,
                          out_shape=out_shape, compiler_params=_cp(("parallel",)))(*ins)


def _rms_bwd_rows(x, g, dy):
    r = lax.rsqrt(jnp.mean(x * x, axis=-1, keepdims=True) + RMS_EPS)
    xn = x * r
    dg = jnp.sum(dy * xn, axis=0, keepdims=True)
    dxn = dy * g
    dx = r * (dxn - xn * jnp.mean(dxn * xn, axis=-1, keepdims=True))
    return dx, dg


def _norm_bwd(name, d_out, pre=None, post=None):
    rows, d = d_out.shape
    tm = _tile(rows, 256)
    has_pre, has_post = pre is not None, post is not None

    def body(*refs):
        it = iter(refs)
        dres = next(it)[...]
        if has_pre:
            dy, xp, gq = next(it)[...].astype(F32), next(it)[...], next(it)[...]
        if has_post:
            zv, gp = next(it)[...].astype(F32), next(it)[...]
        first = pl.program_id(0) == 0
        if has_pre:
            dx, dg = _rms_bwd_rows(xp, gq, dy)
            dres = dres + dx
            next(it)[...] = dres
            dg_ref = next(it)

            @pl.when(first)
            def _():
                dg_ref[...] = jnp.zeros_like(dg_ref)
            dg_ref[...] += dg
        if has_post:
            dz, dg2 = _rms_bwd_rows(zv, gp, dres)
            next(it)[...] = dz.astype(BF16)
            dg2_ref = next(it)

            @pl.when(first)
            def _():
                dg2_ref[...] = jnp.zeros_like(dg2_ref)
            dg2_ref[...] += dg2

    row = pl.BlockSpec((tm, d), lambda i: (i, 0))
    vec = pl.BlockSpec((1, d), lambda i: (0, 0))
    ins, in_specs, out_shape, out_specs = [d_out], [row], [], []
    if has_pre:
        ins += list(pre)
        in_specs += [row, row, vec]
        out_shape += [jax.ShapeDtypeStruct((rows, d), F32), jax.ShapeDtypeStruct((1, d), F32)]
        out_specs += [row, vec]
    if has_post:
        ins += list(post)
        in_specs += [row, vec]
        out_shape += [jax.ShapeDtypeStruct((rows, d), BF16), jax.ShapeDtypeStruct((1, d), F32)]
        out_specs += [row, vec]
    return pl.pallas_call(body, name=name, grid=(rows // tm,), in_specs=in_specs, out_specs=out_specs,
                          out_shape=out_shape, compiler_params=_cp(("arbitrary",)))(*ins)


def _loss_grad(name, y, target):
    rows, d = y.shape
    tm = _tile(rows, 256)

    def body(y_ref, t_ref, dy_ref, loss_ref):
        err = y_ref[...] - t_ref[...]
        dy_ref[...] = err * (1.0 / d)

        @pl.when(pl.program_id(0) == 0)
        def _():
            loss_ref[...] = jnp.zeros_like(loss_ref)
        loss_ref[...] += jnp.full(loss_ref.shape, (0.5 / d) * jnp.sum(err * err), F32)

    row = pl.BlockSpec((tm, d), lambda i: (i, 0))
    return pl.pallas_call(
        body, name=name, grid=(rows // tm,), in_specs=[row, row],
        out_specs=[row, pl.BlockSpec((1, 128), lambda i: (0, 0))],
        out_shape=[jax.ShapeDtypeStruct((rows, d), F32), jax.ShapeDtypeStruct((1, 128), F32)],
        compiler_params=_cp(("arbitrary",)))(y, target)


NT_DIMS = (((1,), (1,)), ((), ()))
TN_DIMS = (((0,), (0,)), ((), ()))


def _mm_nn_blk(name, a, wblk, relu2=False):
    m, k = a.shape
    nb = wblk.shape[2]
    tm = _tile(m, 1024)

    def body(a_ref, w_ref, o_ref):
        r = jnp.dot(a_ref[...], w_ref[...], preferred_element_type=F32)
        if relu2:
            rr = jnp.maximum(r, 0.0)
            r = rr * rr
        o_ref[...] = r.astype(BF16)

    return pl.pallas_call(
        body, name=name, grid=(NDEV, m // tm),
        in_specs=[pl.BlockSpec((tm, k), lambda d, i: (i, 0)), pl.BlockSpec((None, k, nb), lambda d, i: (d, 0, 0))],
        out_specs=pl.BlockSpec((tm, nb), lambda d, i: (i, d)),
        out_shape=jax.ShapeDtypeStruct((m, NDEV * nb), BF16),
        compiler_params=_cp(("parallel", "parallel"), VMEM_MM))(a, wblk)


def _accumulate(acc, o_ref, r, step, last):
    if acc is None:
        o_ref[...] = r.astype(o_ref.dtype)
        return

    @pl.when(step == 0)
    def _():
        acc[...] = r

    @pl.when(jnp.logical_and(step > 0, step < last))
    def _():
        acc[...] += r

    @pl.when(jnp.logical_and(step > 0, step == last))
    def _():
        o_ref[...] = (acc[...] + r).astype(o_ref.dtype)


def _mm_nn(name, a, w):
    m, kb = a.shape
    n = w.shape[1]
    tm, tk = _tile(m, 512), _tile(kb, 2048)
    steps = kb // tk

    def body(a_ref, w_ref, o_ref, *scratch):
        r = jnp.dot(a_ref[...], w_ref[...], preferred_element_type=F32)
        _accumulate(scratch[0] if scratch else None, o_ref, r, pl.program_id(1), steps - 1)

    return pl.pallas_call(
        body, name=name, grid=(m // tm, steps),
        in_specs=[pl.BlockSpec((tm, tk), lambda i, s: (i, s)), pl.BlockSpec((tk, n), lambda i, s: (s, 0))],
        out_specs=pl.BlockSpec((tm, n), lambda i, s: (i, 0)),
        out_shape=jax.ShapeDtypeStruct((m, n), BF16),
        scratch_shapes=[pltpu.VMEM((tm, n), F32)] if steps > 1 else [],
        compiler_params=_cp(("parallel", "arbitrary"), VMEM_MM))(a, w)


def _mm_nt_rows(name, dy, w, act=None, out_dtype=BF16):
    m, n = dy.shape
    kw = w.shape[0]
    tm, tkw = _tile(m, 1024), _tile(kw, 1024)

    def body(dy_ref, w_ref, *rest):
        r = lax.dot_general(dy_ref[...], w_ref[...], NT_DIMS, preferred_element_type=F32)
        if act is None:
            rest[0][...] = r.astype(out_dtype)
        else:
            rest[1][...] = (r * (2.0 * jnp.sqrt(rest[0][...].astype(F32)))).astype(BF16)

    ins = [dy, w]
    in_specs = [pl.BlockSpec((tm, n), lambda j, i: (i, 0)), pl.BlockSpec((tkw, n), lambda j, i: (j, 0))]
    if act is not None:
        ins.append(act)
        in_specs.append(pl.BlockSpec((tm, tkw), lambda j, i: (i, j)))
    return pl.pallas_call(
        body, name=name, grid=(kw // tkw, m // tm), in_specs=in_specs,
        out_specs=pl.BlockSpec((tm, tkw), lambda j, i: (i, j)),
        out_shape=jax.ShapeDtypeStruct((m, kw), out_dtype if act is None else BF16),
        compiler_params=_cp(("parallel", "parallel"), VMEM_MM))(*ins)


def _mm_nt_blk(name, dy, wblk, after=None):
    m = dy.shape[0]
    _, kw, nb = wblk.shape
    tm, per = _tile(m, 512), 2

    extra = list(after or ())

    def body(dy_ref, w_ref, *rest):
        o_ref, acc = rest[len(extra):]
        r = lax.dot_general(dy_ref[:, :nb], w_ref[0], NT_DIMS, preferred_element_type=F32)
        for t in range(1, per):
            r = r + lax.dot_general(dy_ref[:, t * nb:(t + 1) * nb], w_ref[t], NT_DIMS, preferred_element_type=F32)
        _accumulate(acc, o_ref, r, pl.program_id(1), NDEV // per - 1)

    return pl.pallas_call(
        body, name=name, grid=(m // tm, NDEV // per),
        in_specs=[pl.BlockSpec((tm, per * nb), lambda i, s: (i, s)),
                  pl.BlockSpec((per, kw, nb), lambda i, s: (s, 0, 0))] + [ANY] * len(extra),
        out_specs=pl.BlockSpec((tm, kw), lambda i, s: (i, 0)),
        out_shape=jax.ShapeDtypeStruct((m, kw), BF16),
        scratch_shapes=[pltpu.VMEM((tm, kw), F32)],
        compiler_params=_cp(("parallel", "arbitrary"), VMEM_MM))(dy, wblk, *extra)


def _mm_tn(name, x, dy, nb=None):
    t, mx = x.shape
    n = dy.shape[1]
    tmx = _tile(mx, 512)
    tn = nb if nb is not None else _tile(n, 1024)

    def body(x_ref, dy_ref, o_ref):
        o_ref[...] = lax.dot_general(x_ref[...], dy_ref[...], TN_DIMS, preferred_element_type=F32).astype(BF16)

    if nb is None:
        out_shape = jax.ShapeDtypeStruct((mx, n), BF16)
        out_spec = pl.BlockSpec((tmx, tn), lambda j, i: (i, j))
    else:
        out_shape = jax.ShapeDtypeStruct((NDEV, mx, nb), BF16)
        out_spec = pl.BlockSpec((None, tmx, nb), lambda j, i: (j, i, 0))
    return pl.pallas_call(
        body, name=name, grid=(n // tn, mx // tmx),
        in_specs=[pl.BlockSpec((t, tmx), lambda j, i: (0, i)), pl.BlockSpec((t, tn), lambda j, i: (0, j))],
        out_specs=out_spec, out_shape=out_shape,
        compiler_params=_cp(("parallel", "parallel"), VMEM_MM))(x, dy)


def _gelu(x):
    return 0.5 * x * (1.0 + jnp.tanh(GELU_C * (x + GELU_A * (x * x * x))))


def _gelu_grad(x):
    t = jnp.tanh(GELU_C * (x + GELU_A * (x * x * x)))
    return 0.5 * (1.0 + t) + 0.5 * x * (1.0 - t * t) * (GELU_C * (1.0 + 3.0 * GELU_A * (x * x)))


def _layernorm(a):
    mu = jnp.mean(a, axis=-1, keepdims=True)
    ac = a - mu
    rstd = lax.rsqrt(jnp.mean(ac * ac, axis=-1, keepdims=True) + LN_EPS)
    return ac * rstd, rstd


def _shift_rows(z, halo, k):
    zr = pltpu.roll(z, k, 0)
    hr = pltpu.roll(halo, k, 0)
    row = lax.broadcasted_iota(jnp.int32, hr.shape, 0)
    top = jnp.where(row < k, hr, zr[:HALO])
    return jnp.concatenate([top, zr[HALO:]], axis=0)


def _shift_rows_up(z, halo, k):
    rows = z.shape[0]
    zr = pltpu.roll(z, rows - k, 0)
    hr = pltpu.roll(halo, HALO - k, 0)
    row = lax.broadcasted_iota(jnp.int32, hr.shape, 0)
    bot = jnp.where(row >= HALO - k, hr, zr[rows - HALO:])
    return jnp.concatenate([zr[:rows - HALO], bot], axis=0)


def _causal_mask():
    t = lax.broadcasted_iota(jnp.int32, (CHUNK, CHUNK), 0)
    s = lax.broadcasted_iota(jnp.int32, (CHUNK, CHUNK), 1)
    return s <= t


def _gate_specs(tm, rows, width):
    per = tm // HALO_BF16
    last = rows // HALO_BF16 - 1
    cur = pl.BlockSpec((tm, width), lambda i: (i, 0))
    prev = pl.BlockSpec((HALO_BF16, width), lambda i: (jnp.maximum(i * per - 1, 0), 0))
    nxt = pl.BlockSpec((HALO_BF16, width), lambda i: (jnp.minimum((i + 1) * per, last), 0))
    return cur, prev, nxt


def _cols(ref, lo, hi):
    return ref[:, lo:hi].astype(F32)


def _halo_before(ref, lo, hi):
    return ref[:, lo:hi].astype(F32)[HALO_BF16 - HALO:]


def _halo_after(ref, lo, hi):
    return ref[:, lo:hi].astype(F32)[:HALO]


def _gate_fwd(name, proj, w_s, b_st, cw):
    rows, width = proj.shape
    w = width // 5
    groups = w // CHUNK
    tm = _tile(rows, 256)
    cur, prev, _ = _gate_specs(tm, rows, width)

    def body(p_ref, h_ref, ws_ref, b_ref, cw_ref, o_ref):
        mask = _causal_mask()
        au = _gelu(_cols(p_ref, 0, w))
        vn, _ = _layernorm(_gelu(_cols(p_ref, w, 2 * w)))
        vn = vn.astype(BF16)
        for g in range(groups):
            wc = jnp.where(mask, ws_ref[g], 0.0).astype(BF16)
            cols = slice(g * CHUNK, (g + 1) * CHUNK)
            for ch in range(tm // CHUNK):
                rws = slice(ch * CHUNK, (ch + 1) * CHUNK)
                mixed = jnp.dot(wc, vn[rws, cols], preferred_element_type=F32) + b_ref[:, g:g + 1]
                o_ref[rws, cols] = (au[rws, cols] * mixed).astype(BF16)
        z = _cols(p_ref, 3 * w, 4 * w) * _cols(p_ref, 4 * w, 5 * w)
        zh = _halo_before(h_ref, 3 * w, 4 * w) * _halo_before(h_ref, 4 * w, 5 * w)
        zh = jnp.where(pl.program_id(0) == 0, 0.0, zh)
        y = cw_ref[0:1, :] * _shift_rows(z, zh, 2) + cw_ref[1:2, :] * _shift_rows(z, zh, 1) + cw_ref[2:3, :] * z
        o_ref[:, w:2 * w] = (_cols(p_ref, 2 * w, 3 * w) * y).astype(BF16)

    full = lambda a: pl.BlockSpec(a.shape, lambda i: (0,) * a.ndim)
    return pl.pallas_call(
        body, name=name, grid=(rows // tm,),
        in_specs=[cur, prev, full(w_s), full(b_st), full(cw)],
        out_specs=pl.BlockSpec((tm, 2 * w), lambda i: (i, 0)),
        out_shape=jax.ShapeDtypeStruct((rows, 2 * w), BF16),
        compiler_params=_cp(("parallel",)))(proj, proj, w_s, b_st, cw)


def _gate_bwd(name, proj, d_ab, w_s, b_st, cw):
    rows, width = proj.shape
    w = width // 5
    groups = w // CHUNK
    tm = _tile(rows, 256)
    cur, prev, nxt = _gate_specs(tm, rows, width)
    dcur, _, dnxt = _gate_specs(tm, rows, 2 * w)

    def body(p_ref, ph_ref, pn_ref, d_ref, dn_ref, ws_ref, b_ref, cw_ref, o_ref, dws_ref, dbs_ref, dcw_ref):
        i = pl.program_id(0)

        @pl.when(i == 0)
        def _():
            dws_ref[...] = jnp.zeros_like(dws_ref)
            dbs_ref[...] = jnp.zeros_like(dbs_ref)
            dcw_ref[...] = jnp.zeros_like(dcw_ref)

        mask = _causal_mask()
        u, v = _cols(p_ref, 0, w), _cols(p_ref, w, 2 * w)
        au, av = _gelu(u), _gelu(v)
        vn, rstd = _layernorm(av)
        vnb = vn.astype(BF16)
        d_a = _cols(d_ref, 0, w)
        d_mixed = (d_a * au).astype(BF16)
        ones = jnp.ones((HALO, CHUNK), BF16)
        d_vn_cols = []
        d_au_cols = []
        for g in range(groups):
            wc = jnp.where(mask, ws_ref[g], 0.0).astype(BF16)
            cols = slice(g * CHUNK, (g + 1) * CHUNK)
            dw = jnp.zeros((CHUNK, CHUNK), F32)
            db = jnp.zeros((HALO, CHUNK), F32)
            d_vn_rows, d_au_rows = [], []
            for ch in range(tm // CHUNK):
                rws = slice(ch * CHUNK, (ch + 1) * CHUNK)
                mixed = jnp.dot(wc, vnb[rws, cols], preferred_element_type=F32) + b_ref[:, g:g + 1]
                d_au_rows.append(d_a[rws, cols] * mixed)
                dm = d_mixed[rws, cols]
                dw = dw + lax.dot_general(dm, vnb[rws, cols], NT_DIMS, preferred_element_type=F32)
                db = db + lax.dot_general(ones, dm, NT_DIMS, preferred_element_type=F32)
                d_vn_rows.append(lax.dot_general(wc, dm, TN_DIMS, preferred_element_type=F32))
            dws_ref[g] += jnp.where(mask, dw, 0.0)
            dbs_ref[g:g + 1, :] += db[0:1, :]
            d_vn_cols.append(jnp.concatenate(d_vn_rows, axis=0))
            d_au_cols.append(jnp.concatenate(d_au_rows, axis=0))
        d_vn = jnp.concatenate(d_vn_cols, axis=1)
        d_au = jnp.concatenate(d_au_cols, axis=1)
        d_av = rstd * (d_vn - jnp.mean(d_vn, axis=-1, keepdims=True)
                       - vn * jnp.mean(d_vn * vn, axis=-1, keepdims=True))
        o_ref[:, 0:w] = (d_au * _gelu_grad(u)).astype(BF16)
        o_ref[:, w:2 * w] = (d_av * _gelu_grad(v)).astype(BF16)

        gb, gc, bx = _cols(p_ref, 2 * w, 3 * w), _cols(p_ref, 3 * w, 4 * w), _cols(p_ref, 4 * w, 5 * w)
        z = gc * bx
        zh = jnp.where(i == 0, 0.0, _halo_before(ph_ref, 3 * w, 4 * w) * _halo_before(ph_ref, 4 * w, 5 * w))
        z1, z2 = _shift_rows(z, zh, 1), _shift_rows(z, zh, 2)
        d_b = _cols(d_ref, w, 2 * w)
        y = cw_ref[0:1, :] * z2 + cw_ref[1:2, :] * z1 + cw_ref[2:3, :] * z
        dy = d_b * gb
        dyn = jnp.where(i == pl.num_programs(0) - 1, 0.0,
                        _halo_after(dn_ref, w, 2 * w) * _halo_after(pn_ref, 2 * w, 3 * w))
        dz = (cw_ref[2:3, :] * dy + cw_ref[1:2, :] * _shift_rows_up(dy, dyn, 1)
              + cw_ref[0:1, :] * _shift_rows_up(dy, dyn, 2))
        dcw_ref[0:1, :] += jnp.sum(dy * z2, axis=0, keepdims=True)
        dcw_ref[1:2, :] += jnp.sum(dy * z1, axis=0, keepdims=True)
        dcw_ref[2:3, :] += jnp.sum(dy * z, axis=0, keepdims=True)
        o_ref[:, 2 * w:3 * w] = (d_b * y).astype(BF16)
        o_ref[:, 3 * w:4 * w] = (dz * bx).astype(BF16)
        o_ref[:, 4 * w:5 * w] = (dz * gc).astype(BF16)

    full = lambda a: pl.BlockSpec(a.shape, lambda i: (0,) * a.ndim)
    acc = lambda shape: pl.BlockSpec(shape, lambda i: (0,) * len(shape))
    return pl.pallas_call(
        body, name=name, grid=(rows // tm,),
        in_specs=[cur, prev, nxt, dcur, dnxt, full(w_s), full(b_st), full(cw)],
        out_specs=[pl.BlockSpec((tm, width), lambda i: (i, 0)), acc((groups, CHUNK, CHUNK)),
                   acc((groups, CHUNK)), acc((HALO, w))],
        out_shape=[jax.ShapeDtypeStruct((rows, width), BF16), jax.ShapeDtypeStruct((groups, CHUNK, CHUNK), F32),
                   jax.ShapeDtypeStruct((groups, CHUNK), F32), jax.ShapeDtypeStruct((HALO, w), F32)],
        compiler_params=_cp(("arbitrary",), VMEM_MM))(proj, proj, proj, d_ab, d_ab, w_s, b_st, cw)


def _rope(t, cosf, sins):
    return t * cosf + pltpu.roll(t, HEAD // 2, 1) * sins


def _rope_bwd(dt, cosf, sins):
    return dt * cosf + pltpu.roll(dt * sins, HEAD // 2, 1)


def _attn_units(visit):
    for b, d in enumerate(DILATIONS):
        blocks = ATT_TILE // (CHUNK * d)
        for r in range(d):
            if blocks <= ATT_UNROLL:
                for j in range(blocks):
                    visit(b, d, r, j)
            else:
                def step(jj, carry, b=b, d=d, r=r):
                    for u in range(ATT_UNROLL):
                        visit(b, d, r, jj * ATT_UNROLL + u)
                    return carry
                lax.fori_loop(0, blocks // ATT_UNROLL, step, 0)


def _unit_rows(ref, d, r, j, nblk, offset=0):
    base = offset + j * (CHUNK * d)
    if not isinstance(base, int):
        base = pl.multiple_of(base, CHUNK)
    return ref.at[pl.ds(base, nblk * CHUNK * d)], pl.ds(r, nblk * CHUNK, stride=d)


def _band_bias(bias):
    qi = lax.broadcasted_iota(jnp.int32, (CHUNK, 2 * CHUNK), 0)
    ki = lax.broadcasted_iota(jnp.int32, (CHUNK, 2 * CHUNK), 1)
    band = (ki >= qi) & (ki <= qi + CHUNK)
    bias[0] = jnp.where(band, 0.0, -jnp.inf)
    bias[1] = jnp.where(band & (ki >= CHUNK), 0.0, -jnp.inf)


def _unit_bias(bias, n, j):
    if isinstance(j, int) and j != 0:
        return bias[0]
    return bias[jnp.where(jnp.logical_and(n == 0, j == 0), 1, 0)]


def _attn_in_specs(heads):
    blk = (ATT_TILE, HEAD)
    prev = lambda n: jnp.maximum(n - 1, 0)
    return [
        pl.BlockSpec(blk, lambda h, n: (n, h)),
        pl.BlockSpec(blk, lambda h, n: (n, heads + h)),
        pl.BlockSpec(blk, lambda h, n: (prev(n), heads + h)),
        pl.BlockSpec(blk, lambda h, n: (n, 2 * heads + h)),
        pl.BlockSpec(blk, lambda h, n: (prev(n), 2 * heads + h)),
        pl.BlockSpec(blk, lambda h, n: (n, 0)),
        pl.BlockSpec(blk, lambda h, n: (n, 0)),
        pl.BlockSpec(blk, lambda h, n: (prev(n), 0)),
        pl.BlockSpec(blk, lambda h, n: (prev(n), 0)),
    ]


def _attn_load(q_ref, kc_ref, kp_ref, vc_ref, vp_ref, cc_ref, sc_ref, cp_ref, sp_ref, qr, kcat, vcat):
    qr[...] = _rope(q_ref[...].astype(F32), cc_ref[...], sc_ref[...])
    kcat[pl.ds(0, ATT_TILE), :] = _rope(kp_ref[...].astype(F32), cp_ref[...], sp_ref[...])
    kcat[pl.ds(ATT_TILE, ATT_TILE), :] = _rope(kc_ref[...].astype(F32), cc_ref[...], sc_ref[...])
    vcat[pl.ds(0, ATT_TILE), :] = vp_ref[...].astype(F32)
    vcat[pl.ds(ATT_TILE, ATT_TILE), :] = vc_ref[...].astype(F32)


def _attn_fwd(name, qkv, cosf, sins):
    t = qkv.shape[0]
    heads = qkv.shape[1] // (3 * HEAD)
    scale = HEAD ** -0.5
    nbr = len(DILATIONS)

    def body(q_ref, kc_ref, kp_ref, vc_ref, vp_ref, cc_ref, sc_ref, cp_ref, sp_ref, o_ref, lse_ref,
             qr, kcat, vcat, obr, lbr, bias):
        n = pl.program_id(1)
        _attn_load(q_ref, kc_ref, kp_ref, vc_ref, vp_ref, cc_ref, sc_ref, cp_ref, sp_ref, qr, kcat, vcat)
        _band_bias(bias)

        def visit(b, d, r, j):
            qv, qs = _unit_rows(qr, d, r, j, 1)
            kv, ks = _unit_rows(kcat, d, r, j, 2, ATT_TILE - CHUNK * d)
            vv, _ = _unit_rows(vcat, d, r, j, 2, ATT_TILE - CHUNK * d)
            s = lax.dot_general(qv[qs, :].astype(BF16), kv[ks, :].astype(BF16), NT_DIMS,
                                preferred_element_type=F32) * scale + _unit_bias(bias, n, j)
            mx = jnp.max(s, axis=-1, keepdims=True)
            p = jnp.exp(s - mx)
            den = jnp.sum(p, axis=-1, keepdims=True)
            o = jnp.dot((p * (1.0 / den)).astype(BF16), vv[ks, :].astype(BF16), preferred_element_type=F32)
            ov, _ = _unit_rows(obr.at[b], d, r, j, 1)
            lv, _ = _unit_rows(lbr.at[b], d, r, j, 1)
            ov[qs, :] = o
            lv[qs, :] = jnp.broadcast_to(mx + jnp.log(den), (CHUNK, HEAD))

        _attn_units(visit)
        ls = [lbr[b] for b in range(nbr)]
        top = functools.reduce(jnp.maximum, ls)
        ws = [jnp.exp(l - top) for l in ls]
        tot = functools.reduce(jnp.add, ws)
        o = (ws[0] / tot) * obr[0]
        for b in range(1, nbr):
            o = o + (ws[b] / tot) * obr[b]
        o_ref[...] = o.astype(BF16)
        lse_ref[...] = top + jnp.log(tot)

    blk = (ATT_TILE, HEAD)
    tile = pl.BlockSpec(blk, lambda h, n: (n, h))
    return pl.pallas_call(
        body, name=name, grid=(heads, t // ATT_TILE), in_specs=_attn_in_specs(heads),
        out_specs=[tile, tile],
        out_shape=[jax.ShapeDtypeStruct((t, heads * HEAD), BF16), jax.ShapeDtypeStruct((t, heads * HEAD), F32)],
        scratch_shapes=[pltpu.VMEM(blk, F32), pltpu.VMEM((2 * ATT_TILE, HEAD), F32), pltpu.VMEM((2 * ATT_TILE, HEAD), F32),
                        pltpu.VMEM((nbr,) + blk, F32), pltpu.VMEM((nbr,) + blk, F32),
                        pltpu.VMEM((2, CHUNK, 2 * CHUNK), F32)],
        compiler_params=_cp(("parallel", "parallel"), VMEM_MM),
    )(qkv, qkv, qkv, qkv, qkv, cosf, sins, cosf, sins)


def _attn_bwd(name, qkv, cosf, sins, d_o, o, lse):
    t = qkv.shape[0]
    heads = qkv.shape[1] // (3 * HEAD)
    scale = HEAD ** -0.5

    def body(q_ref, kc_ref, kp_ref, vc_ref, vp_ref, cc_ref, sc_ref, cp_ref, sp_ref, do_ref, o_ref, lse_ref,
             dq_ref, dko_ref, dkp_ref, dvo_ref, dvp_ref, qr, kcat, vcat, dq_acc, dk_acc, dv_acc, delta, bias):
        n = pl.program_id(1)
        _attn_load(q_ref, kc_ref, kp_ref, vc_ref, vp_ref, cc_ref, sc_ref, cp_ref, sp_ref, qr, kcat, vcat)
        _band_bias(bias)
        dq_acc[...] = jnp.zeros_like(dq_acc)
        dk_acc[...] = jnp.zeros_like(dk_acc)
        dv_acc[...] = jnp.zeros_like(dv_acc)
        delta[...] = jnp.broadcast_to(
            jnp.sum(do_ref[...] * o_ref[...].astype(F32), axis=-1, keepdims=True), delta.shape)

        def visit(b, d, r, j):
            qv, qs = _unit_rows(qr, d, r, j, 1)
            kv, ks = _unit_rows(kcat, d, r, j, 2, ATT_TILE - CHUNK * d)
            vv, _ = _unit_rows(vcat, d, r, j, 2, ATT_TILE - CHUNK * d)
            dov, _ = _unit_rows(do_ref, d, r, j, 1)
            lv, _ = _unit_rows(lse_ref, d, r, j, 1)
            dlv, _ = _unit_rows(delta, d, r, j, 1)
            q, k = qv[qs, :].astype(BF16), kv[ks, :].astype(BF16)
            do = dov[qs, :].astype(BF16)
            s = lax.dot_general(q, k, NT_DIMS, preferred_element_type=F32) * scale + _unit_bias(bias, n, j)
            p = jnp.exp(s - lv[qs, :][:, 0:1])
            dp = lax.dot_general(do, vv[ks, :].astype(BF16), NT_DIMS, preferred_element_type=F32)
            ds = (p * (dp - dlv[qs, :][:, 0:1]) * scale).astype(BF16)
            dqv, _ = _unit_rows(dq_acc, d, r, j, 1)
            dkv, _ = _unit_rows(dk_acc, d, r, j, 2, ATT_TILE - CHUNK * d)
            dvv, _ = _unit_rows(dv_acc, d, r, j, 2, ATT_TILE - CHUNK * d)
            dqv[qs, :] += jnp.dot(ds, k, preferred_element_type=F32)
            dkv[ks, :] += lax.dot_general(ds, q, TN_DIMS, preferred_element_type=F32)
            dvv[ks, :] += lax.dot_general(p.astype(BF16), do, TN_DIMS, preferred_element_type=F32)

        _attn_units(visit)
        dq_ref[...] = _rope_bwd(dq_acc[...], cc_ref[...], sc_ref[...])
        dkp_ref[...] = _rope_bwd(dk_acc[pl.ds(0, ATT_TILE), :], cp_ref[...], sp_ref[...])
        dko_ref[...] = _rope_bwd(dk_acc[pl.ds(ATT_TILE, ATT_TILE), :], cc_ref[...], sc_ref[...])
        dvp_ref[...] = dv_acc[pl.ds(0, ATT_TILE), :]
        dvo_ref[...] = dv_acc[pl.ds(ATT_TILE, ATT_TILE), :]

    blk = (ATT_TILE, HEAD)
    tile = pl.BlockSpec(blk, lambda h, n: (n, h))
    big = pltpu.VMEM((2 * ATT_TILE, HEAD), F32)
    return pl.pallas_call(
        body, name=name, grid=(heads, t // ATT_TILE), in_specs=_attn_in_specs(heads) + [tile, tile, tile],
        out_specs=[tile] * 5,
        out_shape=[jax.ShapeDtypeStruct((t, heads * HEAD), F32)] * 5,
        scratch_shapes=[pltpu.VMEM(blk, F32), big, big, pltpu.VMEM(blk, F32), big, big, pltpu.VMEM(blk, F32),
                        pltpu.VMEM((2, CHUNK, 2 * CHUNK), F32)],
        compiler_params=_cp(("parallel", "parallel"), 60 << 20),
    )(qkv, qkv, qkv, qkv, qkv, cosf, sins, cosf, sins, d_o, o, lse)


def _attn_merge(name, dq, dk_own, dk_prev, dv_own, dv_prev):
    t, hd = dq.shape
    nt = t // ATT_TILE
    tw = _tile(hd, 512)

    def body(dq_ref, dko_ref, dkn_ref, dvo_ref, dvn_ref, o_ref):
        last = pl.program_id(0) == nt - 1
        part = pl.program_id(1)

        @pl.when(part == 0)
        def _():
            o_ref[...] = dq_ref[...].astype(BF16)

        @pl.when(part == 1)
        def _():
            o_ref[...] = (dko_ref[...] + jnp.where(last, 0.0, dkn_ref[...])).astype(BF16)

        @pl.when(part == 2)
        def _():
            o_ref[...] = (dvo_ref[...] + jnp.where(last, 0.0, dvn_ref[...])).astype(BF16)

    def own(part):
        return pl.BlockSpec((ATT_TILE, tw), lambda n, p, c: (jnp.where(p == part, n, 0), jnp.where(p == part, c, 0)))

    def nxt(part):
        return pl.BlockSpec((ATT_TILE, tw), lambda n, p, c: (jnp.where(p == part, jnp.minimum(n + 1, nt - 1), 0),
                                                            jnp.where(p == part, c, 0)))

    per = hd // tw
    return pl.pallas_call(
        body, name=name, grid=(nt, 3, per), in_specs=[own(0), own(1), nxt(1), own(2), nxt(2)],
        out_specs=pl.BlockSpec((ATT_TILE, tw), lambda n, p, c: (n, p * per + c)),
        out_shape=jax.ShapeDtypeStruct((t, 3 * hd), BF16),
        compiler_params=_cp(("parallel", "parallel", "parallel"), VMEM_MM))(dq, dk_own, dk_prev, dv_own, dv_prev)


def _sum_parts(name, parts):
    nparts, rows, cols = parts.shape
    tr = _tile(rows, 256)

    def body(p_ref, o_ref):
        s = p_ref[0]
        for k in range(1, nparts):
            s = s + p_ref[k]
        o_ref[...] = s

    return pl.pallas_call(
        body, name=name, grid=(rows // tr,),
        in_specs=[pl.BlockSpec((nparts, tr, cols), lambda i: (0, i, 0))],
        out_specs=pl.BlockSpec((tr, cols), lambda i: (i, 0)),
        out_shape=jax.ShapeDtypeStruct((rows, cols), F32),
        compiler_params=_cp(("parallel",)))(parts)


def _rows128(a, pad_to=8):
    flat = a.reshape(-1)
    rows = -(-flat.shape[0] // 128)
    rows = -(-rows // pad_to) * pad_to
    flat = jnp.pad(flat, (0, rows * 128 - flat.shape[0]))
    return flat.reshape(rows, 128)


def _pack(arrays):
    return jnp.concatenate([_rows128(a) for a in arrays], axis=0)


def _unpack(packed, like):
    out, at = [], 0
    for a in like:
        size = 1
        for s in a.shape:
            size *= s
        rows = -(-(-(-size // 128)) // 8) * 8
        out.append(packed[at:at + rows].reshape(-1)[:size].reshape(a.shape))
        at += rows
    return out


def kernel(x, norm_mix_pre, norm_mix_post, norm_mlp_pre, norm_mlp_post, w_in_ab, w_spatial, b_spatial, conv_w, w_out_ab, w_qkv, w_o, w_up, w_down, loss_target, m_norm_mix_pre, m_norm_mix_post, m_norm_mlp_pre, m_norm_mlp_post, m_w_in_ab, m_w_spatial, m_b_spatial, m_conv_w, m_w_out_ab, m_w_qkv, m_w_o, m_w_up, m_w_down, v_norm_mix_pre, v_norm_mix_post, v_norm_mlp_pre, v_norm_mlp_post, v_w_in_ab, v_w_spatial, v_b_spatial, v_conv_w, v_w_out_ab, v_w_qkv, v_w_o, v_w_up, v_w_down):
    depth = norm_mix_pre.shape[0]
    seq, dm = x.shape[1], x.shape[2]
    h0 = x.reshape(seq, dm)
    target = loss_target.reshape(seq, dm)
    ax, ay, ac = lax.axis_index("x"), lax.axis_index("y"), lax.axis_index("c")
    my_block = 4 * ax + 2 * ay + ac
    block = jnp.reshape(my_block, (1,)).astype(jnp.int32)

    half = HEAD // 2
    inv_freq = ROPE_THETA ** (-jnp.arange(half, dtype=F32) * 2.0 / HEAD)
    ang = jnp.arange(seq, dtype=jnp.int32).astype(F32)[:, None] * inv_freq[None, :]
    cosf = jnp.concatenate([jnp.cos(ang), jnp.cos(ang)], axis=-1)
    sins = jnp.concatenate([-jnp.sin(ang), jnp.sin(ang)], axis=-1)

    big = {"w_in_ab": w_in_ab, "w_out_ab": w_out_ab, "w_qkv": w_qkv, "w_o": w_o, "w_up": w_up, "w_down": w_down}
    use_order = []
    for l in range(depth):
        use_order += [("w_in_ab", l // 2), ("w_out_ab", l // 2)] if l % 2 == 0 else [("w_qkv", l // 2), ("w_o", l // 2)]
        use_order += [("w_up", l), ("w_down", l)]
    n_even = w_in_ab.shape[0]
    cw_rows = jnp.pad(conv_w.reshape(n_even * CONV_TAPS, conv_w.shape[2]), ((0, HALO - (n_even * CONV_TAPS) % HALO), (0, 0)))
    cw_gathered = _all_gather("ag_conv", [cw_rows])[0]
    first = [k for k in use_order if k in (("w_in_ab", 0), ("w_out_ab", 0), ("w_up", 0), ("w_down", 0))]
    rest = [k for k in use_order if k not in first]
    lands_a, sems_a = _ag_start("ag_start_first", [_cast_fill(f"cast_{nm}_{l}", big[nm], l, block) for nm, l in first],
                                cw_gathered)
    lands_b, sems_b = _ag_start("ag_start_rest", [_cast_fill(f"cast_{nm}_{l}", big[nm], l, block) for nm, l in rest],
                                lands_a[0])
    lands = dict(zip(first + rest, list(lands_a) + list(lands_b)))
    ag_sems = dict(zip(first + rest, list(sems_a) + list(sems_b)))
    passed_on, wg = [], {}

    def weight(key, after):
        if key not in wg:
            upto = min(use_order.index(key) + 2, len(use_order) - 1)
            for k in use_order[len(passed_on):upto + 1]:
                lands[k] = _ag_mid(f"ag_mid_{k[0]}_{k[1]}", lands[k], ag_sems[k], after)
                passed_on.append(k)
            wg[key] = _ag_wait(f"ag_wait_{key[0]}_{key[1]}", lands[key], ag_sems[key], after)
        return wg[key]

    cw_all = cw_gathered[:, :n_even * CONV_TAPS].reshape(NDEV, n_even, CONV_TAPS, -1)
    cw_all = jnp.transpose(cw_all, (1, 2, 0, 3)).reshape(n_even, CONV_TAPS, -1)
    cw_full = [jnp.pad(cw_all[e], ((0, HALO - CONV_TAPS), (0, 0))) for e in range(n_even)]

    def rows_nat(blk):
        return blk.reshape(blk.shape[0] * blk.shape[1], blk.shape[2])

    saved = []
    hn = _norm_fwd("norm_first", h0, g_pre=norm_mix_pre[0][None])[0]
    h = h0
    for l in range(depth):
        s = {"h_in": h, "hn1": hn}
        if l % 2 == 0:
            e = l // 2
            proj = _mm_nn_blk(f"fwd_in_{l}", hn, weight(("w_in_ab", e), hn))
            ab = _gate_fwd(f"gate_fwd_{l}", proj, w_spatial[e], b_spatial[e].T, cw_full[e])
            mix = _mm_nn(f"fwd_out_{l}", ab, rows_nat(weight(("w_out_ab", e), ab)))
            s.update(proj=proj, ab=ab)
        else:
            o_ = l // 2
            qkv = _mm_nn_blk(f"fwd_qkv_{l}", hn, weight(("w_qkv", o_), hn))
            att, lse = _attn_fwd(f"attn_fwd_{l}", qkv, cosf, sins)
            mix = _mm_nn(f"fwd_o_{l}", att, rows_nat(weight(("w_o", o_), att)))
            s.update(qkv=qkv, att=att, lse=lse)
        h1, hn2 = _norm_fwd(f"norm_mid_{l}", h, mix, norm_mix_post[l][None], norm_mlp_pre[l][None])
        act = _mm_nn_blk(f"fwd_up_{l}", hn2, weight(("w_up", l), hn2), relu2=True)
        f = _mm_nn(f"fwd_down_{l}", act, rows_nat(weight(("w_down", l), act)))
        s.update(mix=mix, h1=h1, hn2=hn2, act=act, f=f)
        if l + 1 < depth:
            h, hn = _norm_fwd(f"norm_end_{l}", h1, f, norm_mlp_post[l][None], norm_mix_pre[l + 1][None])
        else:
            h = _norm_fwd(f"norm_end_{l}", h1, f, norm_mlp_post[l][None])[0]
        saved.append(s)

    d_h, loss_row = _loss_grad("loss", h, target)
    rs = {}

    def scatter(key, g):
        rs[key] = _rs_start(f"rs_start_{key[0]}_{key[1]}", g.reshape(NDEV, -1, g.shape[-1]))

    dg ={nm: [None] * depth for nm in ("norm_mix_pre", "norm_mix_post", "norm_mlp_pre", "norm_mlp_post")}
    d_ws, d_bs, d_cw = [None] * n_even, [None] * n_even, [None] * n_even
    d_hn_next = None
    for l in reversed(range(depth)):
        s = saved[l]
        if l == depth - 1:
            d_f, dg["norm_mlp_post"][l] = _norm_bwd(f"nb_end_{l}", d_h, post=(s["f"], norm_mlp_post[l][None]))
        else:
            d_h, dg["norm_mix_pre"][l + 1], d_f, dg["norm_mlp_post"][l] = _norm_bwd(
                f"nb_end_{l}", d_h, pre=(d_hn_next, saved[l + 1]["h_in"], norm_mix_pre[l + 1][None]),
                post=(s["f"], norm_mlp_post[l][None]))
        wd = rows_nat(wg[("w_down", l)])
        d_up = _mm_nt_rows(f"bwd_down_{l}", d_f, wd, act=s["act"])
        scatter(("w_down", l), _mm_tn(f"gw_down_{l}", s["act"], d_f))
        scatter(("w_up", l), _mm_tn(f"gw_up_{l}", s["hn2"], d_up, nb=w_up.shape[2]))
        d_hn2 = _mm_nt_blk(f"bwd_up_{l}", d_up, wg[("w_up", l)], after=[rs[("w_down", l)][0], rs[("w_up", l)][0]])
        d_h, dg["norm_mlp_pre"][l], d_mix, dg["norm_mix_post"][l] = _norm_bwd(
            f"nb_mid_{l}", d_h, pre=(d_hn2, s["h1"], norm_mlp_pre[l][None]),
            post=(s["mix"], norm_mix_post[l][None]))
        if l % 2 == 0:
            e = l // 2
            wo = rows_nat(wg[("w_out_ab", e)])
            d_ab = _mm_nt_rows(f"bwd_out_{l}", d_mix, wo)
            scatter(("w_out_ab", e), _mm_tn(f"gw_out_{l}", s["ab"], d_mix))
            d_proj, d_ws[e], d_bs[e], d_cw[e] = _gate_bwd(
                f"gate_bwd_{l}", s["proj"], d_ab, w_spatial[e], b_spatial[e].T, cw_full[e])
            scatter(("w_in_ab", e), _mm_tn(f"gw_in_{l}", s["hn1"], d_proj, nb=w_in_ab.shape[2]))
            d_hn_next = _mm_nt_blk(f"bwd_in_{l}", d_proj, wg[("w_in_ab", e)],
                                   after=[rs[("w_out_ab", e)][0], rs[("w_in_ab", e)][0]])
        else:
            o_ = l // 2
            wo = rows_nat(wg[("w_o", o_)])
            d_att = _mm_nt_rows(f"bwd_o_{l}", d_mix, wo, out_dtype=F32)
            scatter(("w_o", o_), _mm_tn(f"gw_o_{l}", s["att"], d_mix))
            parts = _attn_bwd(f"attn_bwd_{l}", s["qkv"], cosf, sins, d_att, s["att"], s["lse"])
            d_qkv = _attn_merge(f"attn_merge_{l}", *parts)
            scatter(("w_qkv", o_), _mm_tn(f"gw_qkv_{l}", s["hn1"], d_qkv, nb=w_qkv.shape[2]))
            d_hn_next = _mm_nt_blk(f"bwd_qkv_{l}", d_qkv, wg[("w_qkv", o_)],
                                   after=[rs[("w_o", o_)][0], rs[("w_qkv", o_)][0]])
    grad_x, dg["norm_mix_pre"][0] = _norm_bwd("nb_first", d_h, pre=(d_hn_next, h0, norm_mix_pre[0][None]))

    moments = {"w_in_ab": (m_w_in_ab, v_w_in_ab), "w_out_ab": (m_w_out_ab, v_w_out_ab), "w_qkv": (m_w_qkv, v_w_qkv),
               "w_o": (m_w_o, v_w_o), "w_up": (m_w_up, v_w_up), "w_down": (m_w_down, v_w_down)}
    out_big = {}
    behind = grad_x
    for nm in ("w_o", "w_qkv", "w_down", "w_up", "w_out_ab", "w_in_ab"):
        own, landed = [], []
        for l in range(big[nm].shape[0]):
            g, land = _wait_all(f"rs_wait_{nm}_{l}", *rs[(nm, l)], behind)
            own.append(g)
            landed.append(land)
        out_big[nm] = _adamw_layers(f"adamw_{nm}", own, landed, block, big[nm], moments[nm][0], moments[nm][1])
        behind = out_big[nm][0]

    small_g = ([jnp.concatenate(dg[nm], axis=0) for nm in dg]
               + [jnp.stack(d_ws), jnp.stack(d_bs), jnp.stack([c[:CONV_TAPS] for c in d_cw]), loss_row])
    packed = _pack(small_g)
    summed = _sum_parts("sum_small", _all_gather("ag_small", [packed], after=[behind])[0])
    g_nmp, g_nmo, g_nlp, g_nlo, g_ws, g_bs, g_cw_all, loss_sum = _unpack(summed, small_g)
    loss = loss_sum[0, 0]
    cwb = conv_w.shape[2]
    g_cw = lax.dynamic_slice_in_dim(g_cw_all, my_block * cwb, cwb, axis=2)
    small_w = [norm_mix_pre, norm_mix_post, norm_mlp_pre, norm_mlp_post, w_spatial, b_spatial, conv_w]
    small_m = [m_norm_mix_pre, m_norm_mix_post, m_norm_mlp_pre, m_norm_mlp_post, m_w_spatial, m_b_spatial, m_conv_w]
    small_v = [v_norm_mix_pre, v_norm_mix_post, v_norm_mlp_pre, v_norm_mlp_post, v_w_spatial, v_b_spatial, v_conv_w]
    small_grad = [g_nmp, g_nmo, g_nlp, g_nlo, g_ws, g_bs, g_cw]
    upd = _adamw("adamw_small", _pack(small_grad)[None], _pack(small_w), _pack(small_m), _pack(small_v))
    sg, sd, sm, sv = [_unpack(u, small_w) for u in upd]

    def outs(i_small, i_big):
        return (i_small[0], i_small[1], i_small[2], i_small[3], i_big["w_in_ab"], i_small[4], i_small[5], i_small[6],
                i_big["w_out_ab"], i_big["w_qkv"], i_big["w_o"], i_big["w_up"], i_big["w_down"])

    pick = lambda i: {nm: out_big[nm][i] for nm in big}
    return (loss, grad_x.reshape(x.shape), *outs(sg, pick(0)), *outs(sd, pick(1)), *outs(sm, pick(2)),
            *outs(sv, pick(3)))
```

```python
import functools

import jax
import jax.numpy as jnp
from jax import lax
from jax.experimental import pallas as pl
from jax.experimental.pallas import tpu as pltpu

F32 = jnp.float32
BF16 = jnp.bfloat16
MESH = pl.DeviceIdType.MESH
ANY = pl.BlockSpec(memory_space=pl.ANY)
HBM = pl.BlockSpec(memory_space=pltpu.HBM)
SEM = pl.BlockSpec(memory_space=pltpu.SEMAPHORE)
EFFECT = pltpu.SideEffectType.DATAFLOW_SIDE_EFFECTING

NDEV = 8
NCHIP = 4
RMS_EPS = 1e-6
LN_EPS = 1e-5
CHUNK = 128
HEAD = 128
ATT_TILE = 2048
ATT_UNROLL = 4
DILATIONS = (1, 4, 16)
SEGS = 16
ROPE_THETA = 10000.0
CONV_TAPS = 3
HALO = 8
HALO_BF16 = 16
GELU_C = 0.7978845608028654
GELU_A = 0.044715
ADAM_LR, ADAM_B1, ADAM_B2, ADAM_EPS, ADAM_WD, ADAM_STEP = 0.001, 0.9, 0.999, 1e-08, 0.01, 10
VMEM_MM = 52 << 20
VMEM_EW = 40 << 20


def _cp(sem=None, vmem=VMEM_EW):
    if sem is None:
        return pltpu.CompilerParams(vmem_limit_bytes=vmem)
    return pltpu.CompilerParams(dimension_semantics=sem, vmem_limit_bytes=vmem)


def _tile(n, want):
    return want if n % want == 0 else n


def _all_gather(name, shards, after=()):
    n = len(shards)
    after = list(after)

    def body(*refs):
        ins, outs = refs[:n], refs[n + len(after):2 * n + len(after)]
        send_sems, recv_sems, local_sems = refs[2 * n + len(after):]
        x, y, c = lax.axis_index("x"), lax.axis_index("y"), lax.axis_index("c")
        me, sibling = (x, y, c), (x, y, 1 - c)
        chips = [(1 - x, y), (x, 1 - y), (1 - x, 1 - y)]

        def slot(p):
            return 4 * p[0] + 2 * p[1] + p[2]

        def copy(i, k, block, to, src=None):
            dst = outs[i].at[slot(block)]
            return pltpu.make_async_remote_copy(
                src_ref=dst if src is None else src, dst_ref=dst,
                send_sem=send_sems.at[i, k], recv_sem=recv_sems.at[i, k],
                device_id=to, device_id_type=MESH)

        mine = [pltpu.make_async_copy(ins[i], outs[i].at[slot(me)], local_sems.at[i]) for i in range(n)]
        for cp in mine:
            cp.start()
        first = []
        for i in range(n):
            first.append(copy(i, 0, me, sibling, src=ins[i]))
            for j, chip in enumerate(chips):
                first.append(copy(i, 1 + j, me, (*chip, c), src=ins[i]))
        for cp in first:
            cp.start()
        passed = []
        for j, chip in enumerate(chips):
            for i in range(n):
                copy(i, 1 + j, (*chip, c), me).wait_recv()
                fwd = copy(i, 4 + j, (*chip, c), sibling)
                fwd.start()
                passed.append(fwd)
        for i in range(n):
            copy(i, 0, sibling, me).wait_recv()
            for j, chip in enumerate(chips):
                copy(i, 4 + j, (*chip, 1 - c), me).wait_recv()
        for cp in first + passed:
            cp.wait_send()
        for cp in mine:
            cp.wait()

    return pl.pallas_call(
        body, name=name,
        out_shape=[jax.ShapeDtypeStruct((NDEV,) + s.shape, s.dtype) for s in shards],
        in_specs=[ANY] * (n + len(after)), out_specs=[ANY] * n,
        scratch_shapes=[pltpu.SemaphoreType.DMA((n, 7)), pltpu.SemaphoreType.DMA((n, 7)),
                        pltpu.SemaphoreType.DMA((n,))],
    )(*shards, *after)


def _peer(x, y, c, r):
    return (1 - x if r & 4 else x, 1 - y if r & 2 else y, 1 - c if r & 1 else c)


def _slot(p):
    return 4 * p[0] + 2 * p[1] + p[2]


def _cast_fill(name, w, layer, block):
    _, rows, cols = w.shape
    tr = _tile(rows, 256)

    def body(blk_ref, w_ref, o_ref):
        o_ref[...] = w_ref[...].astype(BF16)

    return pl.pallas_call(
        body, name=name,
        grid_spec=pltpu.PrefetchScalarGridSpec(
            num_scalar_prefetch=1, grid=(rows // tr,),
            in_specs=[pl.BlockSpec((None, tr, cols), lambda i, blk: (layer, i, 0))],
            out_specs=pl.BlockSpec((None, tr, cols), lambda i, blk: (blk[0], i, 0))),
        out_shape=jax.ShapeDtypeStruct((NDEV, rows, cols), BF16),
        compiler_params=_cp(("parallel",)))(block, w)


OTHER_CHIPS = (2, 4, 6)


def _ag_start(name, lands, after):
    n = len(lands)

    def body(*refs):
        ins, sems = refs[:n], refs[n + 1:n + 1 + 4 * n]
        x, y, c = lax.axis_index("x"), lax.axis_index("y"), lax.axis_index("c")
        mine = _slot((x, y, c))
        for i in range(n):
            send_a, recv_a, _, recv_b = sems[4 * i:4 * i + 4]
            block = ins[i].at[mine]
            pltpu.make_async_remote_copy(src_ref=block, dst_ref=block, send_sem=send_a, recv_sem=recv_b,
                                         device_id=_peer(x, y, c, 1), device_id_type=MESH).start()
            for r in OTHER_CHIPS:
                pltpu.make_async_remote_copy(src_ref=block, dst_ref=block, send_sem=send_a, recv_sem=recv_a,
                                             device_id=_peer(x, y, c, r), device_id_type=MESH).start()

    outs = pl.pallas_call(
        body, name=name,
        out_shape=[pltpu.SemaphoreType.DMA(())] * (4 * n) + [pltpu.HBM(a.shape, a.dtype) for a in lands],
        in_specs=[HBM] * n + [ANY], out_specs=[SEM] * (4 * n) + [HBM] * n,
        input_output_aliases={i: 4 * n + i for i in range(n)},
        compiler_params=pltpu.CompilerParams(has_side_effects=EFFECT),
    )(*[pltpu.with_memory_space_constraint(a, pltpu.HBM) for a in lands], after)
    return outs[4 * n:], [tuple(outs[4 * i:4 * i + 4]) for i in range(n)]


def _ag_mid(name, land, sems, after):
    _, recv_a, send_b, recv_b = sems

    def body(land_ref, recv_a_ref, send_b_ref, recv_b_ref, after_ref, land_out):
        x, y, c = lax.axis_index("x"), lax.axis_index("y"), lax.axis_index("c")
        sibling = _peer(x, y, c, 1)
        three = land_ref.at[pl.ds(0, len(OTHER_CHIPS))]
        pltpu.make_async_remote_copy(src_ref=three, dst_ref=three, send_sem=send_b_ref, recv_sem=recv_a_ref,
                                     device_id=sibling, device_id_type=MESH).wait_recv()
        for r in OTHER_CHIPS:
            block = land_ref.at[_slot(_peer(x, y, c, r))]
            pltpu.make_async_remote_copy(src_ref=block, dst_ref=block, send_sem=send_b_ref, recv_sem=recv_b_ref,
                                         device_id=sibling, device_id_type=MESH).start()

    return pl.pallas_call(
        body, name=name, out_shape=pltpu.HBM(land.shape, land.dtype),
        in_specs=[HBM, SEM, SEM, SEM, ANY], out_specs=HBM, input_output_aliases={0: 0},
        compiler_params=pltpu.CompilerParams(has_side_effects=EFFECT),
    )(land, recv_a, send_b, recv_b, after)


def _ag_wait(name, land, sems, after):
    send_a, _, send_b, recv_b = sems

    def body(land_ref, send_a_ref, send_b_ref, recv_b_ref, after_ref, land_out):
        x, y, c = lax.axis_index("x"), lax.axis_index("y"), lax.axis_index("c")
        sibling = _peer(x, y, c, 1)
        four = land_ref.at[pl.ds(0, 1 + len(OTHER_CHIPS))]
        three = land_ref.at[pl.ds(0, len(OTHER_CHIPS))]
        first = pltpu.make_async_remote_copy(src_ref=four, dst_ref=four, send_sem=send_a_ref, recv_sem=recv_b_ref,
                                             device_id=sibling, device_id_type=MESH)
        passed = pltpu.make_async_remote_copy(src_ref=three, dst_ref=three, send_sem=send_b_ref, recv_sem=recv_b_ref,
                                              device_id=sibling, device_id_type=MESH)
        first.wait_send()
        passed.wait_send()
        first.wait_recv()

    return pl.pallas_call(
        body, name=name, out_shape=pltpu.HBM(land.shape, land.dtype),
        in_specs=[HBM, SEM, SEM, SEM, ANY], out_specs=HBM, input_output_aliases={0: 0},
        compiler_params=pltpu.CompilerParams(has_side_effects=EFFECT),
    )(land, send_a, send_b, recv_b, after)


def _wait_all(name, src, land, send, recv, after):
    def body(src_ref, land_ref, send_ref, recv_ref, after_ref, src_out, land_out):
        x, y, c = lax.axis_index("x"), lax.axis_index("y"), lax.axis_index("c")
        seven = land_ref.at[pl.ds(0, NDEV - 1)]
        copy = pltpu.make_async_remote_copy(src_ref=seven, dst_ref=seven, send_sem=send_ref, recv_sem=recv_ref,
                                            device_id=_peer(x, y, c, 1), device_id_type=MESH)
        copy.wait_send()
        copy.wait_recv()

    return pl.pallas_call(
        body, name=name,
        out_shape=[pltpu.HBM(src.shape, src.dtype), pltpu.HBM(land.shape, land.dtype)],
        in_specs=[HBM, HBM, SEM, SEM, ANY], out_specs=[HBM, HBM],
        input_output_aliases={0: 0, 1: 1},
        compiler_params=pltpu.CompilerParams(has_side_effects=EFFECT),
    )(src, land, send, recv, after)


def _rs_start(name, grad):
    land = lax.empty((NDEV - 1,) + grad.shape[1:], grad.dtype)

    def body(g_ref, land_ref, send, recv, g_out, land_out):
        x, y, c = lax.axis_index("x"), lax.axis_index("y"), lax.axis_index("c")
        for r in range(1, NDEV):
            peer = _peer(x, y, c, r)
            pltpu.make_async_remote_copy(
                src_ref=g_ref.at[_slot(peer)], dst_ref=land_ref.at[r - 1], send_sem=send, recv_sem=recv,
                device_id=peer, device_id_type=MESH).start()

    send, recv, g_thru, land_thru = pl.pallas_call(
        body, name=name,
        out_shape=[pltpu.SemaphoreType.DMA(()), pltpu.SemaphoreType.DMA(()),
                   pltpu.HBM(grad.shape, grad.dtype), pltpu.HBM(land.shape, land.dtype)],
        in_specs=[HBM, HBM], out_specs=[SEM, SEM, HBM, HBM], input_output_aliases={0: 2, 1: 3},
        compiler_params=pltpu.CompilerParams(has_side_effects=EFFECT),
    )(pltpu.with_memory_space_constraint(grad, pltpu.HBM), pltpu.with_memory_space_constraint(land, pltpu.HBM))
    return g_thru, land_thru, send, recv


def _adam_math(w, g, m, v):
    m = ADAM_B1 * m + (1.0 - ADAM_B1) * g
    v = ADAM_B2 * v + (1.0 - ADAM_B2) * (g * g)
    m_hat = m / (1.0 - ADAM_B1 ** ADAM_STEP)
    v_hat = v / (1.0 - ADAM_B2 ** ADAM_STEP)
    delta = -ADAM_LR * (m_hat / (jnp.sqrt(v_hat) + ADAM_EPS) + ADAM_WD * w)
    return delta, m, v


def _adamw(name, parts, w, m, v):
    nparts, rows, cols = parts.shape
    tr = _tile(rows, 256)

    def body(p_ref, w_ref, m_ref, v_ref, g_out, d_out, m_out, v_out):
        g = p_ref[0].astype(F32)
        for k in range(1, nparts):
            g = g + p_ref[k].astype(F32)
        delta, mn, vn = _adam_math(w_ref[...], g, m_ref[...], v_ref[...])
        g_out[...] = g
        d_out[...] = delta
        m_out[...] = mn
        v_out[...] = vn

    row = pl.BlockSpec((tr, cols), lambda i: (i, 0))
    return pl.pallas_call(
        body, name=name, grid=(rows // tr,),
        in_specs=[pl.BlockSpec((nparts, tr, cols), lambda i: (0, i, 0)), row, row, row],
        out_specs=[row] * 4,
        out_shape=[jax.ShapeDtypeStruct((rows, cols), F32)] * 4,
        compiler_params=_cp(("parallel",)),
    )(parts, w, m, v)


def _adamw_layers(name, grads, lands, block, w, m, v):
    layers, rows, cols = w.shape
    nland = lands[0].shape[0]
    tr = rows
    while tr % 2 == 0 and tr > 8 and nland * tr * cols * 2 > (1 << 20):
        tr //= 2

    def body(blk_ref, *refs):
        own_refs, land_refs = refs[:layers], refs[layers:2 * layers]
        w_ref, m_ref, v_ref, g_out, d_out, m_out, v_out = refs[2 * layers:]
        layer = pl.program_id(0)
        for k in range(layers):
            @pl.when(layer == k)
            def _(k=k):
                g = own_refs[k][...].astype(F32)
                for s in range(nland):
                    g = g + land_refs[k][s].astype(F32)
                delta, mn, vn = _adam_math(w_ref[...], g, m_ref[...], v_ref[...])
                g_out[...] = g
                d_out[...] = delta
                m_out[...] = mn
                v_out[...] = vn

    def own_spec(k):
        return pl.BlockSpec((None, tr, cols), lambda l, i, blk: (blk[0], jnp.where(l == k, i, 0), 0))

    def land_spec(k):
        return pl.BlockSpec((nland, tr, cols), lambda l, i, blk: (0, jnp.where(l == k, i, 0), 0))

    row = pl.BlockSpec((None, tr, cols), lambda l, i, blk: (l, i, 0))
    return pl.pallas_call(
        body, name=name,
        grid_spec=pltpu.PrefetchScalarGridSpec(
            num_scalar_prefetch=1, grid=(layers, rows // tr),
            in_specs=[own_spec(k) for k in range(layers)] + [land_spec(k) for k in range(layers)] + [row, row, row],
            out_specs=[row] * 4),
        out_shape=[jax.ShapeDtypeStruct((layers, rows, cols), F32)] * 4,
        compiler_params=_cp(("arbitrary", "arbitrary")),
    )(block, *grads, *lands, w, m, v)


LANES = 128


def _seg_scratch(rows, d):
    return pltpu.VMEM((d // LANES, rows, LANES), F32)


def _to_segments(vals, scratch, out_ref):
    per = scratch.shape[1] // SEGS
    for c in range(scratch.shape[0]):
        cols = slice(c * LANES, (c + 1) * LANES)
        scratch[c] = vals[:, cols]
        for s in range(SEGS):
            out_ref[s, :, cols] = scratch.at[c][pl.ds(s, per, stride=SEGS), :].astype(out_ref.dtype)


def _from_segments(in_ref, scratch):
    per = scratch.shape[1] // SEGS
    for c in range(scratch.shape[0]):
        for s in range(SEGS):
            scratch.at[c][pl.ds(s, per, stride=SEGS), :] = in_ref[s, :, c * LANES:(c + 1) * LANES].astype(F32)
    return jnp.concatenate([scratch[c] for c in range(scratch.shape[0])], axis=1)


def _seg_view(a):
    return a.reshape(SEGS, a.shape[0] // SEGS, a.shape[1])


def _norm_fwd(name, h, z=None, g_post=None, g_pre=None, seg_z=False, seg_y=False):
    rows, d = h.shape
    tm = _tile(rows, 256)
    has_post, has_pre = z is not None, g_pre is not None
    nscratch = int(seg_z) + int(seg_y)

    def body(*refs):
        scratch = list(refs[len(refs) - nscratch:])
        it = iter(refs)
        hv = next(it)[...]
        if has_post:
            z_ref = next(it)
            zv = _from_segments(z_ref, scratch.pop(0)) if seg_z else z_ref[...].astype(F32)
            gp = next(it)[...]
        if has_pre:
            gq = next(it)[...]
        if has_post:
            r = lax.rsqrt(jnp.mean(zv * zv, axis=-1, keepdims=True) + RMS_EPS)
            hv = hv + (zv * r) * gp
            next(it)[...] = hv
        if has_pre:
            r = lax.rsqrt(jnp.mean(hv * hv, axis=-1, keepdims=True) + RMS_EPS)
            y = (hv * r) * gq
            if seg_y:
                _to_segments(y, scratch.pop(0), next(it))
            else:
                next(it)[...] = y.astype(BF16)

    row = pl.BlockSpec((tm, d), lambda i: (i, 0))
    seg = pl.BlockSpec((SEGS, tm // SEGS, d), lambda i: (0, i, 0))
    vec = pl.BlockSpec((1, d), lambda i: (0, 0))
    ins, in_specs, out_shape, out_specs = [h], [row], [], []
    if has_post:
        ins += [_seg_view(z) if seg_z else z, g_post]
        in_specs += [seg if seg_z else row, vec]
        out_shape.append(jax.ShapeDtypeStruct((rows, d), F32))
        out_specs.append(row)
    if has_pre:
        ins.append(g_pre)
        in_specs.append(vec)
        out_shape.append(jax.ShapeDtypeStruct((SEGS, rows // SEGS, d) if seg_y else (rows, d), BF16))
        out_specs.append(seg if seg_y else row)
    outs = pl.pallas_call(body, name=name, grid=(rows // tm,), in_specs=in_specs, out_specs=out_specs,
                          out_shape=out_shape, scratch_shapes=[_seg_scratch(tm, d)] * nscratch,
                          compiler_params=_cp(("parallel",)))(*ins)
    if seg_y:
        outs = list(outs[:-1]) + [outs[-1].reshape(rows, d)]
    return outs


def _rms_bwd_rows(x, g, dy):
    r = lax.rsqrt(jnp.mean(x * x, axis=-1, keepdims=True) + RMS_EPS)
    xn = x * r
    dg = jnp.sum(dy * xn, axis=0, keepdims=True)
    dxn = dy * g
    dx = r * (dxn - xn * jnp.mean(dxn * xn, axis=-1, keepdims=True))
    return dx, dg


def _norm_bwd(name, d_out, pre=None, post=None, seg_dy=False, seg_z=False):
    rows, d = d_out.shape
    tm = _tile(rows, 256)
    has_pre, has_post = pre is not None, post is not None
    nscratch = int(seg_dy) + 2 * int(seg_z)

    def body(*refs):
        scratch = list(refs[len(refs) - nscratch:])
        it = iter(refs)
        dres = next(it)[...]
        if has_pre:
            dy_ref = next(it)
            dy = _from_segments(dy_ref, scratch.pop(0)) if seg_dy else dy_ref[...].astype(F32)
            xp, gq = next(it)[...], next(it)[...]
        if has_post:
            z_ref = next(it)
            zv = _from_segments(z_ref, scratch.pop(0)) if seg_z else z_ref[...].astype(F32)
            gp = next(it)[...]
        first = pl.program_id(0) == 0
        if has_pre:
            dx, dg = _rms_bwd_rows(xp, gq, dy)
            dres = dres + dx
            next(it)[...] = dres
            dg_ref = next(it)

            @pl.when(first)
            def _():
                dg_ref[...] = jnp.zeros_like(dg_ref)
            dg_ref[...] += dg
        if has_post:
            dz, dg2 = _rms_bwd_rows(zv, gp, dres)
            if seg_z:
                _to_segments(dz, scratch.pop(0), next(it))
            else:
                next(it)[...] = dz.astype(BF16)
            dg2_ref = next(it)

            @pl.when(first)
            def _():
                dg2_ref[...] = jnp.zeros_like(dg2_ref)
            dg2_ref[...] += dg2

    row = pl.BlockSpec((tm, d), lambda i: (i, 0))
    seg = pl.BlockSpec((SEGS, tm // SEGS, d), lambda i: (0, i, 0))
    vec = pl.BlockSpec((1, d), lambda i: (0, 0))
    ins, in_specs, out_shape, out_specs = [d_out], [row], [], []
    if has_pre:
        d_y, x_pre, g_pre = pre
        ins += [_seg_view(d_y) if seg_dy else d_y, x_pre, g_pre]
        in_specs += [seg if seg_dy else row, row, vec]
        out_shape += [jax.ShapeDtypeStruct((rows, d), F32), jax.ShapeDtypeStruct((1, d), F32)]
        out_specs += [row, vec]
    if has_post:
        z, g_post = post
        ins += [_seg_view(z) if seg_z else z, g_post]
        in_specs += [seg if seg_z else row, vec]
        out_shape += [jax.ShapeDtypeStruct((SEGS, rows // SEGS, d) if seg_z else (rows, d), BF16),
                      jax.ShapeDtypeStruct((1, d), F32)]
        out_specs += [seg if seg_z else row, vec]
    outs = pl.pallas_call(body, name=name, grid=(rows // tm,), in_specs=in_specs, out_specs=out_specs,
                          out_shape=out_shape, scratch_shapes=[_seg_scratch(tm, d)] * nscratch,
                          compiler_params=_cp(("arbitrary",)))(*ins)
    if seg_z:
        outs = list(outs)
        outs[-2] = outs[-2].reshape(rows, d)
    return outs


def _loss_grad(name, y, target):
    rows, d = y.shape
    tm = _tile(rows, 256)

    def body(y_ref, t_ref, dy_ref, loss_ref):
        err = y_ref[...] - t_ref[...]
        dy_ref[...] = err * (1.0 / d)

        @pl.when(pl.program_id(0) == 0)
        def _():
            loss_ref[...] = jnp.zeros_like(loss_ref)
        loss_ref[...] += jnp.full(loss_ref.shape, (0.5 / d) * jnp.sum(err * err), F32)

    row = pl.BlockSpec((tm, d), lambda i: (i, 0))
    return pl.pallas_call(
        body, name=name, grid=(rows // tm,), in_specs=[row, row],
        out_specs=[row, pl.BlockSpec((1, 128), lambda i: (0, 0))],
        out_shape=[jax.ShapeDtypeStruct((rows, d), F32), jax.ShapeDtypeStruct((1, 128), F32)],
        compiler_params=_cp(("arbitrary",)))(y, target)


NT_DIMS = (((1,), (1,)), ((), ()))
TN_DIMS = (((0,), (0,)), ((), ()))


def _mm_nn_blk(name, a, wblk, relu2=False):
    m, k = a.shape
    nb = wblk.shape[2]
    tm = _tile(m, 1024)

    def body(a_ref, w_ref, o_ref):
        r = jnp.dot(a_ref[...], w_ref[...], preferred_element_type=F32)
        if relu2:
            rr = jnp.maximum(r, 0.0)
            r = rr * rr
        o_ref[...] = r.astype(BF16)

    return pl.pallas_call(
        body, name=name, grid=(NDEV, m // tm),
        in_specs=[pl.BlockSpec((tm, k), lambda d, i: (i, 0)), pl.BlockSpec((None, k, nb), lambda d, i: (d, 0, 0))],
        out_specs=pl.BlockSpec((tm, nb), lambda d, i: (i, d)),
        out_shape=jax.ShapeDtypeStruct((m, NDEV * nb), BF16),
        compiler_params=_cp(("parallel", "parallel"), VMEM_MM))(a, wblk)


def _accumulate(acc, o_ref, r, step, last):
    if acc is None:
        o_ref[...] = r.astype(o_ref.dtype)
        return

    @pl.when(step == 0)
    def _():
        acc[...] = r

    @pl.when(jnp.logical_and(step > 0, step < last))
    def _():
        acc[...] += r

    @pl.when(jnp.logical_and(step > 0, step == last))
    def _():
        o_ref[...] = (acc[...] + r).astype(o_ref.dtype)


def _mm_nn(name, a, w):
    m, kb = a.shape
    n = w.shape[1]
    tm, tk = _tile(m, 512), _tile(kb, 2048)
    steps = kb // tk

    def body(a_ref, w_ref, o_ref, *scratch):
        r = jnp.dot(a_ref[...], w_ref[...], preferred_element_type=F32)
        _accumulate(scratch[0] if scratch else None, o_ref, r, pl.program_id(1), steps - 1)

    return pl.pallas_call(
        body, name=name, grid=(m // tm, steps),
        in_specs=[pl.BlockSpec((tm, tk), lambda i, s: (i, s)), pl.BlockSpec((tk, n), lambda i, s: (s, 0))],
        out_specs=pl.BlockSpec((tm, n), lambda i, s: (i, 0)),
        out_shape=jax.ShapeDtypeStruct((m, n), BF16),
        scratch_shapes=[pltpu.VMEM((tm, n), F32)] if steps > 1 else [],
        compiler_params=_cp(("parallel", "arbitrary"), VMEM_MM))(a, w)


def _mm_nt_rows(name, dy, w, act=None, out_dtype=BF16):
    m, n = dy.shape
    kw = w.shape[0]
    tm, tkw = _tile(m, 1024), _tile(kw, 1024)

    def body(dy_ref, w_ref, *rest):
        r = lax.dot_general(dy_ref[...], w_ref[...], NT_DIMS, preferred_element_type=F32)
        if act is None:
            rest[0][...] = r.astype(out_dtype)
        else:
            rest[1][...] = (r * (2.0 * jnp.sqrt(rest[0][...].astype(F32)))).astype(BF16)

    ins = [dy, w]
    in_specs = [pl.BlockSpec((tm, n), lambda j, i: (i, 0)), pl.BlockSpec((tkw, n), lambda j, i: (j, 0))]
    if act is not None:
        ins.append(act)
        in_specs.append(pl.BlockSpec((tm, tkw), lambda j, i: (i, j)))
    return pl.pallas_call(
        body, name=name, grid=(kw // tkw, m // tm), in_specs=in_specs,
        out_specs=pl.BlockSpec((tm, tkw), lambda j, i: (i, j)),
        out_shape=jax.ShapeDtypeStruct((m, kw), out_dtype if act is None else BF16),
        compiler_params=_cp(("parallel", "parallel"), VMEM_MM))(*ins)


def _mm_nt_blk(name, dy, wblk, after=None):
    m = dy.shape[0]
    _, kw, nb = wblk.shape
    tm, per = _tile(m, 512), 2

    extra = list(after or ())

    def body(dy_ref, w_ref, *rest):
        o_ref, acc = rest[len(extra):]
        r = lax.dot_general(dy_ref[:, :nb], w_ref[0], NT_DIMS, preferred_element_type=F32)
        for t in range(1, per):
            r = r + lax.dot_general(dy_ref[:, t * nb:(t + 1) * nb], w_ref[t], NT_DIMS, preferred_element_type=F32)
        _accumulate(acc, o_ref, r, pl.program_id(1), NDEV // per - 1)

    return pl.pallas_call(
        body, name=name, grid=(m // tm, NDEV // per),
        in_specs=[pl.BlockSpec((tm, per * nb), lambda i, s: (i, s)),
                  pl.BlockSpec((per, kw, nb), lambda i, s: (s, 0, 0))] + [ANY] * len(extra),
        out_specs=pl.BlockSpec((tm, kw), lambda i, s: (i, 0)),
        out_shape=jax.ShapeDtypeStruct((m, kw), BF16),
        scratch_shapes=[pltpu.VMEM((tm, kw), F32)],
        compiler_params=_cp(("parallel", "arbitrary"), VMEM_MM))(dy, wblk, *extra)


def _mm_tn(name, x, dy, nb=None):
    t, mx = x.shape
    n = dy.shape[1]
    tmx = _tile(mx, 512)
    tn = nb if nb is not None else _tile(n, 1024)

    def body(x_ref, dy_ref, o_ref):
        o_ref[...] = lax.dot_general(x_ref[...], dy_ref[...], TN_DIMS, preferred_element_type=F32).astype(BF16)

    if nb is None:
        out_shape = jax.ShapeDtypeStruct((mx, n), BF16)
        out_spec = pl.BlockSpec((tmx, tn), lambda j, i: (i, j))
    else:
        out_shape = jax.ShapeDtypeStruct((NDEV, mx, nb), BF16)
        out_spec = pl.BlockSpec((None, tmx, nb), lambda j, i: (j, i, 0))
    return pl.pallas_call(
        body, name=name, grid=(n // tn, mx // tmx),
        in_specs=[pl.BlockSpec((t, tmx), lambda j, i: (0, i)), pl.BlockSpec((t, tn), lambda j, i: (0, j))],
        out_specs=out_spec, out_shape=out_shape,
        compiler_params=_cp(("parallel", "parallel"), VMEM_MM))(x, dy)


def _gelu(x):
    return 0.5 * x * (1.0 + jnp.tanh(GELU_C * (x + GELU_A * (x * x * x))))


def _gelu_grad(x):
    t = jnp.tanh(GELU_C * (x + GELU_A * (x * x * x)))
    return 0.5 * (1.0 + t) + 0.5 * x * (1.0 - t * t) * (GELU_C * (1.0 + 3.0 * GELU_A * (x * x)))


def _layernorm(a):
    mu = jnp.mean(a, axis=-1, keepdims=True)
    ac = a - mu
    rstd = lax.rsqrt(jnp.mean(ac * ac, axis=-1, keepdims=True) + LN_EPS)
    return ac * rstd, rstd


def _shift_rows(z, halo, k):
    zr = pltpu.roll(z, k, 0)
    hr = pltpu.roll(halo, k, 0)
    row = lax.broadcasted_iota(jnp.int32, hr.shape, 0)
    top = jnp.where(row < k, hr, zr[:HALO])
    return jnp.concatenate([top, zr[HALO:]], axis=0)


def _shift_rows_up(z, halo, k):
    rows = z.shape[0]
    zr = pltpu.roll(z, rows - k, 0)
    hr = pltpu.roll(halo, HALO - k, 0)
    row = lax.broadcasted_iota(jnp.int32, hr.shape, 0)
    bot = jnp.where(row >= HALO - k, hr, zr[rows - HALO:])
    return jnp.concatenate([zr[:rows - HALO], bot], axis=0)


def _causal_mask():
    t = lax.broadcasted_iota(jnp.int32, (CHUNK, CHUNK), 0)
    s = lax.broadcasted_iota(jnp.int32, (CHUNK, CHUNK), 1)
    return s <= t


def _gate_specs(tm, rows, width):
    per = tm // HALO_BF16
    last = rows // HALO_BF16 - 1
    cur = pl.BlockSpec((tm, width), lambda i: (i, 0))
    prev = pl.BlockSpec((HALO_BF16, width), lambda i: (jnp.maximum(i * per - 1, 0), 0))
    nxt = pl.BlockSpec((HALO_BF16, width), lambda i: (jnp.minimum((i + 1) * per, last), 0))
    return cur, prev, nxt


def _cols(ref, lo, hi):
    return ref[:, lo:hi].astype(F32)


def _halo_before(ref, lo, hi):
    return ref[:, lo:hi].astype(F32)[HALO_BF16 - HALO:]


def _halo_after(ref, lo, hi):
    return ref[:, lo:hi].astype(F32)[:HALO]


def _gate_fwd(name, proj, w_s, b_st, cw):
    rows, width = proj.shape
    w = width // 5
    groups = w // CHUNK
    tm = _tile(rows, 256)
    cur, prev, _ = _gate_specs(tm, rows, width)

    def body(p_ref, h_ref, ws_ref, b_ref, cw_ref, o_ref):
        mask = _causal_mask()
        au = _gelu(_cols(p_ref, 0, w))
        vn, _ = _layernorm(_gelu(_cols(p_ref, w, 2 * w)))
        vn = vn.astype(BF16)
        for g in range(groups):
            wc = jnp.where(mask, ws_ref[g], 0.0).astype(BF16)
            cols = slice(g * CHUNK, (g + 1) * CHUNK)
            for ch in range(tm // CHUNK):
                rws = slice(ch * CHUNK, (ch + 1) * CHUNK)
                mixed = jnp.dot(wc, vn[rws, cols], preferred_element_type=F32) + b_ref[:, g:g + 1]
                o_ref[rws, cols] = (au[rws, cols] * mixed).astype(BF16)
        z = _cols(p_ref, 3 * w, 4 * w) * _cols(p_ref, 4 * w, 5 * w)
        zh = _halo_before(h_ref, 3 * w, 4 * w) * _halo_before(h_ref, 4 * w, 5 * w)
        zh = jnp.where(pl.program_id(0) == 0, 0.0, zh)
        y = cw_ref[0:1, :] * _shift_rows(z, zh, 2) + cw_ref[1:2, :] * _shift_rows(z, zh, 1) + cw_ref[2:3, :] * z
        o_ref[:, w:2 * w] = (_cols(p_ref, 2 * w, 3 * w) * y).astype(BF16)

    full = lambda a: pl.BlockSpec(a.shape, lambda i: (0,) * a.ndim)
    return pl.pallas_call(
        body, name=name, grid=(rows // tm,),
        in_specs=[cur, prev, full(w_s), full(b_st), full(cw)],
        out_specs=pl.BlockSpec((tm, 2 * w), lambda i: (i, 0)),
        out_shape=jax.ShapeDtypeStruct((rows, 2 * w), BF16),
        compiler_params=_cp(("parallel",)))(proj, proj, w_s, b_st, cw)


def _gate_bwd(name, proj, d_ab, w_s, b_st, cw):
    rows, width = proj.shape
    w = width // 5
    groups = w // CHUNK
    tm = _tile(rows, 256)
    cur, prev, nxt = _gate_specs(tm, rows, width)
    dcur, _, dnxt = _gate_specs(tm, rows, 2 * w)

    def body(p_ref, ph_ref, pn_ref, d_ref, dn_ref, ws_ref, b_ref, cw_ref, o_ref, dws_ref, dbs_ref, dcw_ref):
        i = pl.program_id(0)

        @pl.when(i == 0)
        def _():
            dws_ref[...] = jnp.zeros_like(dws_ref)
            dbs_ref[...] = jnp.zeros_like(dbs_ref)
            dcw_ref[...] = jnp.zeros_like(dcw_ref)

        mask = _causal_mask()
        u, v = _cols(p_ref, 0, w), _cols(p_ref, w, 2 * w)
        au, av = _gelu(u), _gelu(v)
        vn, rstd = _layernorm(av)
        vnb = vn.astype(BF16)
        d_a = _cols(d_ref, 0, w)
        d_mixed = (d_a * au).astype(BF16)
        ones = jnp.ones((HALO, CHUNK), BF16)
        d_vn_cols = []
        d_au_cols = []
        for g in range(groups):
            wc = jnp.where(mask, ws_ref[g], 0.0).astype(BF16)
            cols = slice(g * CHUNK, (g + 1) * CHUNK)
            dw = jnp.zeros((CHUNK, CHUNK), F32)
            db = jnp.zeros((HALO, CHUNK), F32)
            d_vn_rows, d_au_rows = [], []
            for ch in range(tm // CHUNK):
                rws = slice(ch * CHUNK, (ch + 1) * CHUNK)
                mixed = jnp.dot(wc, vnb[rws, cols], preferred_element_type=F32) + b_ref[:, g:g + 1]
                d_au_rows.append(d_a[rws, cols] * mixed)
                dm = d_mixed[rws, cols]
                dw = dw + lax.dot_general(dm, vnb[rws, cols], NT_DIMS, preferred_element_type=F32)
                db = db + lax.dot_general(ones, dm, NT_DIMS, preferred_element_type=F32)
                d_vn_rows.append(lax.dot_general(wc, dm, TN_DIMS, preferred_element_type=F32))
            dws_ref[g] += jnp.where(mask, dw, 0.0)
            dbs_ref[g:g + 1, :] += db[0:1, :]
            d_vn_cols.append(jnp.concatenate(d_vn_rows, axis=0))
            d_au_cols.append(jnp.concatenate(d_au_rows, axis=0))
        d_vn = jnp.concatenate(d_vn_cols, axis=1)
        d_au = jnp.concatenate(d_au_cols, axis=1)
        d_av = rstd * (d_vn - jnp.mean(d_vn, axis=-1, keepdims=True)
                       - vn * jnp.mean(d_vn * vn, axis=-1, keepdims=True))
        o_ref[:, 0:w] = (d_au * _gelu_grad(u)).astype(BF16)
        o_ref[:, w:2 * w] = (d_av * _gelu_grad(v)).astype(BF16)

        gb, gc, bx = _cols(p_ref, 2 * w, 3 * w), _cols(p_ref, 3 * w, 4 * w), _cols(p_ref, 4 * w, 5 * w)
        z = gc * bx
        zh = jnp.where(i == 0, 0.0, _halo_before(ph_ref, 3 * w, 4 * w) * _halo_before(ph_ref, 4 * w, 5 * w))
        z1, z2 = _shift_rows(z, zh, 1), _shift_rows(z, zh, 2)
        d_b = _cols(d_ref, w, 2 * w)
        y = cw_ref[0:1, :] * z2 + cw_ref[1:2, :] * z1 + cw_ref[2:3, :] * z
        dy = d_b * gb
        dyn = jnp.where(i == pl.num_programs(0) - 1, 0.0,
                        _halo_after(dn_ref, w, 2 * w) * _halo_after(pn_ref, 2 * w, 3 * w))
        dz = (cw_ref[2:3, :] * dy + cw_ref[1:2, :] * _shift_rows_up(dy, dyn, 1)
              + cw_ref[0:1, :] * _shift_rows_up(dy, dyn, 2))
        dcw_ref[0:1, :] += jnp.sum(dy * z2, axis=0, keepdims=True)
        dcw_ref[1:2, :] += jnp.sum(dy * z1, axis=0, keepdims=True)
        dcw_ref[2:3, :] += jnp.sum(dy * z, axis=0, keepdims=True)
        o_ref[:, 2 * w:3 * w] = (d_b * y).astype(BF16)
        o_ref[:, 3 * w:4 * w] = (dz * bx).astype(BF16)
        o_ref[:, 4 * w:5 * w] = (dz * gc).astype(BF16)

    full = lambda a: pl.BlockSpec(a.shape, lambda i: (0,) * a.ndim)
    acc = lambda shape: pl.BlockSpec(shape, lambda i: (0,) * len(shape))
    return pl.pallas_call(
        body, name=name, grid=(rows // tm,),
        in_specs=[cur, prev, nxt, dcur, dnxt, full(w_s), full(b_st), full(cw)],
        out_specs=[pl.BlockSpec((tm, width), lambda i: (i, 0)), acc((groups, CHUNK, CHUNK)),
                   acc((groups, CHUNK)), acc((HALO, w))],
        out_shape=[jax.ShapeDtypeStruct((rows, width), BF16), jax.ShapeDtypeStruct((groups, CHUNK, CHUNK), F32),
                   jax.ShapeDtypeStruct((groups, CHUNK), F32), jax.ShapeDtypeStruct((HALO, w), F32)],
        compiler_params=_cp(("arbitrary",), VMEM_MM))(proj, proj, proj, d_ab, d_ab, w_s, b_st, cw)


def _flat(x):
    return x.reshape(-1, x.shape[-1])


def _rope(t, cosf, sins):
    t2 = _flat(t)
    return (t2 * _flat(cosf) + pltpu.roll(t2, HEAD // 2, 1) * _flat(sins)).reshape(t.shape)


def _rope_bwd(dt, cosf, sins):
    d2 = _flat(dt)
    return (d2 * _flat(cosf) + pltpu.roll(d2 * _flat(sins), HEAD // 2, 1)).reshape(dt.shape)


ATT_UNITS = ATT_TILE // CHUNK


def _attn_units(phases):
    for b, d in enumerate(DILATIONS):
        blocks = ATT_TILE // (CHUNK * d)
        for visit in phases:
            for r in range(d):
                if blocks <= ATT_UNROLL:
                    for j in range(blocks):
                        visit(b, d, r, j, r * blocks + j)
                else:
                    def step(jj, carry, b=b, d=d, r=r, visit=visit, blocks=blocks):
                        for u in range(ATT_UNROLL):
                            j = jj * ATT_UNROLL + u
                            visit(b, d, r, j, r * blocks + j)
                        return carry
                    lax.fori_loop(0, blocks // ATT_UNROLL, step, 0)


class _Unit:
    def __init__(self, d, r, j):
        self.segs = [r + d * k for k in range(SEGS // d)]
        self.w = CHUNK * d // SEGS
        q0 = j * self.w
        self.q0 = q0 if isinstance(q0, int) else pl.multiple_of(q0, HALO)
        k0 = CHUNK + (j - 1) * self.w
        self.k0 = k0 if isinstance(k0, int) else pl.multiple_of(k0, HALO)

    def queries(self, ref):
        return _chunks(ref, self.segs, self.q0, self.w)

    def keys(self, ref):
        return _chunks(ref, self.segs, self.k0, 2 * self.w)

    def put_queries(self, ref, val, add=False):
        _put_chunks(ref, self.segs, self.q0, self.w, val, add)

    def put_keys(self, ref, val, add=False):
        _put_chunks(ref, self.segs, self.k0, 2 * self.w, val, add)


def _chunks(ref, segs, start, size):
    parts = [ref[s, pl.ds(start, size), :] for s in segs]
    return parts[0] if len(parts) == 1 else jnp.concatenate(parts, axis=0)


def _put_chunks(ref, segs, start, size, val, add):
    for k, s in enumerate(segs):
        piece = val[k * size:(k + 1) * size]
        if add:
            ref[s, pl.ds(start, size), :] += piece
        else:
            ref[s, pl.ds(start, size), :] = piece


def _band_bias(bias):
    qi = lax.broadcasted_iota(jnp.int32, (CHUNK, 2 * CHUNK), 0)
    ki = lax.broadcasted_iota(jnp.int32, (CHUNK, 2 * CHUNK), 1)
    for b, d in enumerate(DILATIONS):
        nseg, w = SEGS // d, CHUNK * d // SEGS
        shift = w.bit_length() - 1
        pos_q = nseg * (qi & (w - 1)) + (qi >> shift)
        pos_k = nseg * ((ki & (2 * w - 1)) - w) + (ki >> (shift + 1))
        band = (pos_q >= pos_k) & (pos_q - pos_k <= CHUNK)
        bias[2 * b] = jnp.where(band, 0.0, -jnp.inf)
        bias[2 * b + 1] = jnp.where(band & (pos_k >= 0), 0.0, -jnp.inf)


def _unit_bias(bias, b, n, j):
    if isinstance(j, int) and j != 0:
        return bias[2 * b]
    return bias[2 * b + jnp.where(jnp.logical_and(n == 0, j == 0), 1, 0)]


def _attn_in_specs(heads):
    blk = (SEGS, CHUNK, HEAD)
    prev = lambda n: jnp.maximum(n - 1, 0)
    return [
        pl.BlockSpec(blk, lambda h, n: (0, n, h)),
        pl.BlockSpec(blk, lambda h, n: (0, n, heads + h)),
        pl.BlockSpec(blk, lambda h, n: (0, prev(n), heads + h)),
        pl.BlockSpec(blk, lambda h, n: (0, n, 2 * heads + h)),
        pl.BlockSpec(blk, lambda h, n: (0, prev(n), 2 * heads + h)),
        pl.BlockSpec(blk, lambda h, n: (0, n, 0)),
        pl.BlockSpec(blk, lambda h, n: (0, n, 0)),
        pl.BlockSpec(blk, lambda h, n: (0, prev(n), 0)),
        pl.BlockSpec(blk, lambda h, n: (0, prev(n), 0)),
    ]


def _attn_load(q_ref, kc_ref, kp_ref, vc_ref, vp_ref, cc_ref, sc_ref, cp_ref, sp_ref, qr, kcat, vcat):
    qr[...] = _rope(q_ref[...].astype(F32), cc_ref[...], sc_ref[...])
    kcat[:, pl.ds(0, CHUNK), :] = _rope(kp_ref[...].astype(F32), cp_ref[...], sp_ref[...])
    kcat[:, pl.ds(CHUNK, CHUNK), :] = _rope(kc_ref[...].astype(F32), cc_ref[...], sc_ref[...])
    vcat[:, pl.ds(0, CHUNK), :] = vp_ref[...].astype(F32)
    vcat[:, pl.ds(CHUNK, CHUNK), :] = vc_ref[...].astype(F32)


def _attn_fwd(name, qkv, cosf, sins):
    t = qkv.shape[0]
    heads = qkv.shape[1] // (3 * HEAD)
    scale = HEAD ** -0.5
    nbr = len(DILATIONS)

    def body(q_ref, kc_ref, kp_ref, vc_ref, vp_ref, cc_ref, sc_ref, cp_ref, sp_ref, o_ref, lse_ref,
             qr, kcat, vcat, obr, lbr, bias, pn):
        n = pl.program_id(1)
        _attn_load(q_ref, kc_ref, kp_ref, vc_ref, vp_ref, cc_ref, sc_ref, cp_ref, sp_ref, qr, kcat, vcat)
        _band_bias(bias)

        def probs(b, d, r, j, u):
            unit = _Unit(d, r, j)
            s = lax.dot_general(unit.queries(qr).astype(BF16), unit.keys(kcat).astype(BF16), NT_DIMS,
                                preferred_element_type=F32) * scale + _unit_bias(bias, b, n, j)
            mx = jnp.max(s, axis=-1, keepdims=True)
            p = jnp.exp(s - mx)
            den = jnp.sum(p, axis=-1, keepdims=True)
            pn[u] = (p * (1.0 / den)).astype(BF16)
            unit.put_queries(lbr.at[b], jnp.broadcast_to(mx + jnp.log(den), (CHUNK, HEAD)))

        def values(b, d, r, j, u):
            unit = _Unit(d, r, j)
            unit.put_queries(obr.at[b], jnp.dot(pn[u], unit.keys(vcat).astype(BF16), preferred_element_type=F32))

        _attn_units([probs, values])
        ls = [lbr[b] for b in range(nbr)]
        top = functools.reduce(jnp.maximum, ls)
        ws = [jnp.exp(l - top) for l in ls]
        tot = functools.reduce(jnp.add, ws)
        o = (ws[0] / tot) * obr[0]
        for b in range(1, nbr):
            o = o + (ws[b] / tot) * obr[b]
        o_ref[...] = o.astype(BF16)
        lse_ref[...] = top + jnp.log(tot)

    blk = (SEGS, CHUNK, HEAD)
    keys = pltpu.VMEM((SEGS, 2 * CHUNK, HEAD), F32)
    tile = pl.BlockSpec(blk, lambda h, n: (0, n, h))
    seg = t // SEGS
    qkv3, cos3, sin3 = _seg_view(qkv), _seg_view(cosf), _seg_view(sins)
    o, lse = pl.pallas_call(
        body, name=name, grid=(heads, t // ATT_TILE), in_specs=_attn_in_specs(heads),
        out_specs=[tile, tile],
        out_shape=[jax.ShapeDtypeStruct((SEGS, seg, heads * HEAD), BF16),
                   jax.ShapeDtypeStruct((SEGS, seg, heads * HEAD), F32)],
        scratch_shapes=[pltpu.VMEM(blk, F32), keys, keys,
                        pltpu.VMEM((nbr,) + blk, F32), pltpu.VMEM((nbr,) + blk, F32),
                        pltpu.VMEM((2 * nbr, CHUNK, 2 * CHUNK), F32), pltpu.VMEM((ATT_UNITS, CHUNK, 2 * CHUNK), BF16)],
        compiler_params=_cp(("parallel", "parallel"), VMEM_MM),
    )(qkv3, qkv3, qkv3, qkv3, qkv3, cos3, sin3, cos3, sin3)
    return o.reshape(t, heads * HEAD), lse.reshape(t, heads * HEAD)


def _attn_bwd(name, qkv, cosf, sins, d_o, o, lse):
    t = qkv.shape[0]
    heads = qkv.shape[1] // (3 * HEAD)
    scale = HEAD ** -0.5

    def body(q_ref, kc_ref, kp_ref, vc_ref, vp_ref, cc_ref, sc_ref, cp_ref, sp_ref, do_ref, o_ref, lse_ref,
             dq_ref, dko_ref, dkp_ref, dvo_ref, dvp_ref, qr, kcat, vcat, dq_acc, dk_acc, dv_acc, delta, bias, ps, dss):
        n = pl.program_id(1)
        _attn_load(q_ref, kc_ref, kp_ref, vc_ref, vp_ref, cc_ref, sc_ref, cp_ref, sp_ref, qr, kcat, vcat)
        _band_bias(bias)
        dq_acc[...] = jnp.zeros_like(dq_acc)
        dk_acc[...] = jnp.zeros_like(dk_acc)
        dv_acc[...] = jnp.zeros_like(dv_acc)
        delta[...] = jnp.broadcast_to(
            jnp.sum(do_ref[...] * o_ref[...].astype(F32), axis=-1, keepdims=True), delta.shape)

        def probs(b, d, r, j, u):
            unit = _Unit(d, r, j)
            s = lax.dot_general(unit.queries(qr).astype(BF16), unit.keys(kcat).astype(BF16), NT_DIMS,
                                preferred_element_type=F32) * scale + _unit_bias(bias, b, n, j)
            ps[u] = jnp.exp(s - unit.queries(lse_ref)[:, 0:1]).astype(BF16)

        def score_grads(b, d, r, j, u):
            unit = _Unit(d, r, j)
            dp = lax.dot_general(unit.queries(do_ref).astype(BF16), unit.keys(vcat).astype(BF16), NT_DIMS,
                                 preferred_element_type=F32)
            dss[u] = (ps[u].astype(F32) * (dp - unit.queries(delta)[:, 0:1]) * scale).astype(BF16)

        def input_grads(b, d, r, j, u):
            unit = _Unit(d, r, j)
            ds = dss[u]
            unit.put_queries(dq_acc, jnp.dot(ds, unit.keys(kcat).astype(BF16), preferred_element_type=F32), add=True)
            unit.put_keys(dk_acc, lax.dot_general(ds, unit.queries(qr).astype(BF16), TN_DIMS,
                                                  preferred_element_type=F32), add=True)
            unit.put_keys(dv_acc, lax.dot_general(ps[u], unit.queries(do_ref).astype(BF16), TN_DIMS,
                                                  preferred_element_type=F32), add=True)

        _attn_units([probs, score_grads, input_grads])
        dq_ref[...] = _rope_bwd(dq_acc[...], cc_ref[...], sc_ref[...])
        dkp_ref[...] = _rope_bwd(dk_acc[:, pl.ds(0, CHUNK), :], cp_ref[...], sp_ref[...])
        dko_ref[...] = _rope_bwd(dk_acc[:, pl.ds(CHUNK, CHUNK), :], cc_ref[...], sc_ref[...])
        dvp_ref[...] = dv_acc[:, pl.ds(0, CHUNK), :]
        dvo_ref[...] = dv_acc[:, pl.ds(CHUNK, CHUNK), :]

    blk = (SEGS, CHUNK, HEAD)
    tile = pl.BlockSpec(blk, lambda h, n: (0, n, h))
    big = pltpu.VMEM((SEGS, 2 * CHUNK, HEAD), F32)
    nbr = len(DILATIONS)
    seg = t // SEGS
    qkv3, cos3, sin3 = _seg_view(qkv), _seg_view(cosf), _seg_view(sins)
    outs = pl.pallas_call(
        body, name=name, grid=(heads, t // ATT_TILE), in_specs=_attn_in_specs(heads) + [tile, tile, tile],
        out_specs=[tile] * 5,
        out_shape=[jax.ShapeDtypeStruct((SEGS, seg, heads * HEAD), F32)] * 5,
        scratch_shapes=[pltpu.VMEM(blk, F32), big, big, pltpu.VMEM(blk, F32), big, big, pltpu.VMEM(blk, F32),
                        pltpu.VMEM((2 * nbr, CHUNK, 2 * CHUNK), F32), pltpu.VMEM((ATT_UNITS, CHUNK, 2 * CHUNK), BF16),
                        pltpu.VMEM((ATT_UNITS, CHUNK, 2 * CHUNK), BF16)],
        compiler_params=_cp(("parallel", "parallel"), 60 << 20),
    )(qkv3, qkv3, qkv3, qkv3, qkv3, cos3, sin3, cos3, sin3, _seg_view(d_o), _seg_view(o), _seg_view(lse))
    return outs


def _attn_merge(name, dq, dk_own, dk_prev, dv_own, dv_prev):
    _, seg, hd = dq.shape
    nt = seg // CHUNK
    tw = _tile(hd, 512)

    def body(dq_ref, dko_ref, dkn_ref, dvo_ref, dvn_ref, o_ref):
        last = pl.program_id(0) == nt - 1
        part = pl.program_id(1)

        @pl.when(part == 0)
        def _():
            o_ref[...] = dq_ref[...].astype(BF16)

        @pl.when(part == 1)
        def _():
            o_ref[...] = (dko_ref[...] + jnp.where(last, 0.0, dkn_ref[...])).astype(BF16)

        @pl.when(part == 2)
        def _():
            o_ref[...] = (dvo_ref[...] + jnp.where(last, 0.0, dvn_ref[...])).astype(BF16)

    blk = (SEGS, CHUNK, tw)

    def own(part):
        return pl.BlockSpec(blk, lambda n, p, c: (0, jnp.where(p == part, n, 0), jnp.where(p == part, c, 0)))

    def nxt(part):
        return pl.BlockSpec(blk, lambda n, p, c: (0, jnp.where(p == part, jnp.minimum(n + 1, nt - 1), 0),
                                                  jnp.where(p == part, c, 0)))

    per = hd // tw
    return pl.pallas_call(
        body, name=name, grid=(nt, 3, per), in_specs=[own(0), own(1), nxt(1), own(2), nxt(2)],
        out_specs=pl.BlockSpec(blk, lambda n, p, c: (0, n, p * per + c)),
        out_shape=jax.ShapeDtypeStruct((SEGS, seg, 3 * hd), BF16),
        compiler_params=_cp(("parallel", "parallel", "parallel"), VMEM_MM),
    )(dq, dk_own, dk_prev, dv_own, dv_prev).reshape(SEGS * seg, 3 * hd)


def _sum_parts(name, parts):
    nparts, rows, cols = parts.shape
    tr = _tile(rows, 256)

    def body(p_ref, o_ref):
        s = p_ref[0]
        for k in range(1, nparts):
            s = s + p_ref[k]
        o_ref[...] = s

    return pl.pallas_call(
        body, name=name, grid=(rows // tr,),
        in_specs=[pl.BlockSpec((nparts, tr, cols), lambda i: (0, i, 0))],
        out_specs=pl.BlockSpec((tr, cols), lambda i: (i, 0)),
        out_shape=jax.ShapeDtypeStruct((rows, cols), F32),
        compiler_params=_cp(("parallel",)))(parts)


def _rows128(a, pad_to=8):
    flat = a.reshape(-1)
    rows = -(-flat.shape[0] // 128)
    rows = -(-rows // pad_to) * pad_to
    flat = jnp.pad(flat, (0, rows * 128 - flat.shape[0]))
    return flat.reshape(rows, 128)


def _pack(arrays):
    return jnp.concatenate([_rows128(a) for a in arrays], axis=0)


def _unpack(packed, like):
    out, at = [], 0
    for a in like:
        size = 1
        for s in a.shape:
            size *= s
        rows = -(-(-(-size // 128)) // 8) * 8
        out.append(packed[at:at + rows].reshape(-1)[:size].reshape(a.shape))
        at += rows
    return out


def kernel(x, norm_mix_pre, norm_mix_post, norm_mlp_pre, norm_mlp_post, w_in_ab, w_spatial, b_spatial, conv_w, w_out_ab, w_qkv, w_o, w_up, w_down, loss_target, m_norm_mix_pre, m_norm_mix_post, m_norm_mlp_pre, m_norm_mlp_post, m_w_in_ab, m_w_spatial, m_b_spatial, m_conv_w, m_w_out_ab, m_w_qkv, m_w_o, m_w_up, m_w_down, v_norm_mix_pre, v_norm_mix_post, v_norm_mlp_pre, v_norm_mlp_post, v_w_in_ab, v_w_spatial, v_b_spatial, v_conv_w, v_w_out_ab, v_w_qkv, v_w_o, v_w_up, v_w_down):
    depth = norm_mix_pre.shape[0]
    seq, dm = x.shape[1], x.shape[2]
    h0 = x.reshape(seq, dm)
    target = loss_target.reshape(seq, dm)
    ax, ay, ac = lax.axis_index("x"), lax.axis_index("y"), lax.axis_index("c")
    my_block = 4 * ax + 2 * ay + ac
    block = jnp.reshape(my_block, (1,)).astype(jnp.int32)

    half = HEAD // 2
    inv_freq = ROPE_THETA ** (-jnp.arange(half, dtype=F32) * 2.0 / HEAD)
    ang = jnp.arange(seq, dtype=jnp.int32).astype(F32)[:, None] * inv_freq[None, :]
    ang = ang.reshape(seq // SEGS, SEGS, half).transpose(1, 0, 2).reshape(seq, half)
    cosf = jnp.concatenate([jnp.cos(ang), jnp.cos(ang)], axis=-1)
    sins = jnp.concatenate([-jnp.sin(ang), jnp.sin(ang)], axis=-1)

    big = {"w_in_ab": w_in_ab, "w_out_ab": w_out_ab, "w_qkv": w_qkv, "w_o": w_o, "w_up": w_up, "w_down": w_down}
    use_order = []
    for l in range(depth):
        use_order += [("w_in_ab", l // 2), ("w_out_ab", l // 2)] if l % 2 == 0 else [("w_qkv", l // 2), ("w_o", l // 2)]
        use_order += [("w_up", l), ("w_down", l)]
    n_even = w_in_ab.shape[0]
    cw_rows = jnp.pad(conv_w.reshape(n_even * CONV_TAPS, conv_w.shape[2]), ((0, HALO - (n_even * CONV_TAPS) % HALO), (0, 0)))
    cw_gathered = _all_gather("ag_conv", [cw_rows])[0]
    first = [k for k in use_order if k in (("w_in_ab", 0), ("w_out_ab", 0), ("w_up", 0), ("w_down", 0))]
    rest = [k for k in use_order if k not in first]
    lands_a, sems_a = _ag_start("ag_start_first", [_cast_fill(f"cast_{nm}_{l}", big[nm], l, block) for nm, l in first],
                                cw_gathered)
    lands_b, sems_b = _ag_start("ag_start_rest", [_cast_fill(f"cast_{nm}_{l}", big[nm], l, block) for nm, l in rest],
                                lands_a[0])
    lands = dict(zip(first + rest, list(lands_a) + list(lands_b)))
    ag_sems = dict(zip(first + rest, list(sems_a) + list(sems_b)))
    passed_on, wg = [], {}

    def weight(key, after):
        if key not in wg:
            upto = min(use_order.index(key) + 2, len(use_order) - 1)
            for k in use_order[len(passed_on):upto + 1]:
                lands[k] = _ag_mid(f"ag_mid_{k[0]}_{k[1]}", lands[k], ag_sems[k], after)
                passed_on.append(k)
            wg[key] = _ag_wait(f"ag_wait_{key[0]}_{key[1]}", lands[key], ag_sems[key], after)
        return wg[key]

    cw_all = cw_gathered[:, :n_even * CONV_TAPS].reshape(NDEV, n_even, CONV_TAPS, -1)
    cw_all = jnp.transpose(cw_all, (1, 2, 0, 3)).reshape(n_even, CONV_TAPS, -1)
    cw_full = [jnp.pad(cw_all[e], ((0, HALO - CONV_TAPS), (0, 0))) for e in range(n_even)]

    def rows_nat(blk):
        return blk.reshape(blk.shape[0] * blk.shape[1], blk.shape[2])

    saved = []
    hn = _norm_fwd("norm_first", h0, g_pre=norm_mix_pre[0][None])[0]
    h = h0
    for l in range(depth):
        s = {"h_in": h, "hn1": hn}
        if l % 2 == 0:
            e = l // 2
            proj = _mm_nn_blk(f"fwd_in_{l}", hn, weight(("w_in_ab", e), hn))
            ab = _gate_fwd(f"gate_fwd_{l}", proj, w_spatial[e], b_spatial[e].T, cw_full[e])
            mix = _mm_nn(f"fwd_out_{l}", ab, rows_nat(weight(("w_out_ab", e), ab)))
            s.update(proj=proj, ab=ab)
        else:
            o_ = l // 2
            qkv = _mm_nn_blk(f"fwd_qkv_{l}", hn, weight(("w_qkv", o_), hn))
            att, lse = _attn_fwd(f"attn_fwd_{l}", qkv, cosf, sins)
            mix = _mm_nn(f"fwd_o_{l}", att, rows_nat(weight(("w_o", o_), att)))
            s.update(qkv=qkv, att=att, lse=lse)
        h1, hn2 = _norm_fwd(f"norm_mid_{l}", h, mix, norm_mix_post[l][None], norm_mlp_pre[l][None], seg_z=l % 2 == 1)
        act = _mm_nn_blk(f"fwd_up_{l}", hn2, weight(("w_up", l), hn2), relu2=True)
        f = _mm_nn(f"fwd_down_{l}", act, rows_nat(weight(("w_down", l), act)))
        s.update(mix=mix, h1=h1, hn2=hn2, act=act, f=f)
        if l + 1 < depth:
            h, hn = _norm_fwd(f"norm_end_{l}", h1, f, norm_mlp_post[l][None], norm_mix_pre[l + 1][None],
                              seg_y=(l + 1) % 2 == 1)
        else:
            h = _norm_fwd(f"norm_end_{l}", h1, f, norm_mlp_post[l][None])[0]
        saved.append(s)

    d_h, loss_row = _loss_grad("loss", h, target)
    rs = {}

    def scatter(key, g):
        rs[key] = _rs_start(f"rs_start_{key[0]}_{key[1]}", g.reshape(NDEV, -1, g.shape[-1]))

    dg ={nm: [None] * depth for nm in ("norm_mix_pre", "norm_mix_post", "norm_mlp_pre", "norm_mlp_post")}
    d_ws, d_bs, d_cw = [None] * n_even, [None] * n_even, [None] * n_even
    d_hn_next = None
    for l in reversed(range(depth)):
        s = saved[l]
        if l == depth - 1:
            d_f, dg["norm_mlp_post"][l] = _norm_bwd(f"nb_end_{l}", d_h, post=(s["f"], norm_mlp_post[l][None]))
        else:
            d_h, dg["norm_mix_pre"][l + 1], d_f, dg["norm_mlp_post"][l] = _norm_bwd(
                f"nb_end_{l}", d_h, pre=(d_hn_next, saved[l + 1]["h_in"], norm_mix_pre[l + 1][None]),
                post=(s["f"], norm_mlp_post[l][None]), seg_dy=(l + 1) % 2 == 1)
        wd = rows_nat(wg[("w_down", l)])
        d_up = _mm_nt_rows(f"bwd_down_{l}", d_f, wd, act=s["act"])
        scatter(("w_down", l), _mm_tn(f"gw_down_{l}", s["act"], d_f))
        scatter(("w_up", l), _mm_tn(f"gw_up_{l}", s["hn2"], d_up, nb=w_up.shape[2]))
        d_hn2 = _mm_nt_blk(f"bwd_up_{l}", d_up, wg[("w_up", l)], after=[rs[("w_down", l)][0], rs[("w_up", l)][0]])
        d_h, dg["norm_mlp_pre"][l], d_mix, dg["norm_mix_post"][l] = _norm_bwd(
            f"nb_mid_{l}", d_h, pre=(d_hn2, s["h1"], norm_mlp_pre[l][None]),
            post=(s["mix"], norm_mix_post[l][None]), seg_z=l % 2 == 1)
        if l % 2 == 0:
            e = l // 2
            wo = rows_nat(wg[("w_out_ab", e)])
            d_ab = _mm_nt_rows(f"bwd_out_{l}", d_mix, wo)
            scatter(("w_out_ab", e), _mm_tn(f"gw_out_{l}", s["ab"], d_mix))
            d_proj, d_ws[e], d_bs[e], d_cw[e] = _gate_bwd(
                f"gate_bwd_{l}", s["proj"], d_ab, w_spatial[e], b_spatial[e].T, cw_full[e])
            scatter(("w_in_ab", e), _mm_tn(f"gw_in_{l}", s["hn1"], d_proj, nb=w_in_ab.shape[2]))
            d_hn_next = _mm_nt_blk(f"bwd_in_{l}", d_proj, wg[("w_in_ab", e)],
                                   after=[rs[("w_out_ab", e)][0], rs[("w_in_ab", e)][0]])
        else:
            o_ = l // 2
            wo = rows_nat(wg[("w_o", o_)])
            d_att = _mm_nt_rows(f"bwd_o_{l}", d_mix, wo, out_dtype=F32)
            scatter(("w_o", o_), _mm_tn(f"gw_o_{l}", s["att"], d_mix))
            parts = _attn_bwd(f"attn_bwd_{l}", s["qkv"], cosf, sins, d_att, s["att"], s["lse"])
            d_qkv = _attn_merge(f"attn_merge_{l}", *parts)
            scatter(("w_qkv", o_), _mm_tn(f"gw_qkv_{l}", s["hn1"], d_qkv, nb=w_qkv.shape[2]))
            d_hn_next = _mm_nt_blk(f"bwd_qkv_{l}", d_qkv, wg[("w_qkv", o_)],
                                   after=[rs[("w_o", o_)][0], rs[("w_qkv", o_)][0]])
    grad_x, dg["norm_mix_pre"][0] = _norm_bwd("nb_first", d_h, pre=(d_hn_next, h0, norm_mix_pre[0][None]))

    moments = {"w_in_ab": (m_w_in_ab, v_w_in_ab), "w_out_ab": (m_w_out_ab, v_w_out_ab), "w_qkv": (m_w_qkv, v_w_qkv),
               "w_o": (m_w_o, v_w_o), "w_up": (m_w_up, v_w_up), "w_down": (m_w_down, v_w_down)}
    out_big = {}
    behind = grad_x
    for nm in ("w_o", "w_qkv", "w_down", "w_up", "w_out_ab", "w_in_ab"):
        own, landed = [], []
        for l in range(big[nm].shape[0]):
            g, land = _wait_all(f"rs_wait_{nm}_{l}", *rs[(nm, l)], behind)
            own.append(g)
            landed.append(land)
        out_big[nm] = _adamw_layers(f"adamw_{nm}", own, landed, block, big[nm], moments[nm][0], moments[nm][1])
        behind = out_big[nm][0]

    small_g = ([jnp.concatenate(dg[nm], axis=0) for nm in dg]
               + [jnp.stack(d_ws), jnp.stack(d_bs), jnp.stack([c[:CONV_TAPS] for c in d_cw]), loss_row])
    packed = _pack(small_g)
    summed = _sum_parts("sum_small", _all_gather("ag_small", [packed], after=[behind])[0])
    g_nmp, g_nmo, g_nlp, g_nlo, g_ws, g_bs, g_cw_all, loss_sum = _unpack(summed, small_g)
    loss = loss_sum[0, 0]
    cwb = conv_w.shape[2]
    g_cw = lax.dynamic_slice_in_dim(g_cw_all, my_block * cwb, cwb, axis=2)
    small_w = [norm_mix_pre, norm_mix_post, norm_mlp_pre, norm_mlp_post, w_spatial, b_spatial, conv_w]
    small_m = [m_norm_mix_pre, m_norm_mix_post, m_norm_mlp_pre, m_norm_mlp_post, m_w_spatial, m_b_spatial, m_conv_w]
    small_v = [v_norm_mix_pre, v_norm_mix_post, v_norm_mlp_pre, v_norm_mlp_post, v_w_spatial, v_b_spatial, v_conv_w]
    small_grad = [g_nmp, g_nmo, g_nlp, g_nlo, g_ws, g_bs, g_cw]
    upd = _adamw("adamw_small", _pack(small_grad)[None], _pack(small_w), _pack(small_m), _pack(small_v))
    sg, sd, sm, sv = [_unpack(u, small_w) for u in upd]

    def outs(i_small, i_big):
        return (i_small[0], i_small[1], i_small[2], i_small[3], i_big["w_in_ab"], i_small[4], i_small[5], i_small[6],
                i_big["w_out_ab"], i_big["w_qkv"], i_big["w_o"], i_big["w_up"], i_big["w_down"])

    pick = lambda i: {nm: out_big[nm][i] for nm in big}
    return (loss, grad_x.reshape(x.shape), *outs(sg, pick(0)), *outs(sd, pick(1)), *outs(sm, pick(2)),
            *outs(sv, pick(3)))
```

```python
import functools

import jax
import jax.numpy as jnp
from jax import lax
from jax.experimental import pallas as pl
from jax.experimental.pallas import tpu as pltpu

F32 = jnp.float32
BF16 = jnp.bfloat16
MESH = pl.DeviceIdType.MESH
ANY = pl.BlockSpec(memory_space=pl.ANY)
HBM = pl.BlockSpec(memory_space=pltpu.HBM)
SEM = pl.BlockSpec(memory_space=pltpu.SEMAPHORE)
EFFECT = pltpu.SideEffectType.DATAFLOW_SIDE_EFFECTING

NDEV = 8
NCHIP = 4
RMS_EPS = 1e-6
LN_EPS = 1e-5
CHUNK = 128
HEAD = 128
ATT_TILE = 2048
ATT_UNROLL = 4
DILATIONS = (1, 4, 16)
SEGS = 16
ROPE_THETA = 10000.0
CONV_TAPS = 3
HALO = 8
HALO_BF16 = 16
GELU_C = 0.7978845608028654
GELU_A = 0.044715
ADAM_LR, ADAM_B1, ADAM_B2, ADAM_EPS, ADAM_WD, ADAM_STEP = 0.001, 0.9, 0.999, 1e-08, 0.01, 10
VMEM_MM = 52 << 20
VMEM_EW = 40 << 20


def _cp(sem=None, vmem=VMEM_EW):
    if sem is None:
        return pltpu.CompilerParams(vmem_limit_bytes=vmem)
    return pltpu.CompilerParams(dimension_semantics=sem, vmem_limit_bytes=vmem)


def _tile(n, want):
    return want if n % want == 0 else n


def _all_gather(name, shards, after=()):
    n = len(shards)
    after = list(after)

    def body(*refs):
        ins, outs = refs[:n], refs[n + len(after):2 * n + len(after)]
        send_sems, recv_sems, local_sems = refs[2 * n + len(after):]
        x, y, c = lax.axis_index("x"), lax.axis_index("y"), lax.axis_index("c")
        me, sibling = (x, y, c), (x, y, 1 - c)
        chips = [(1 - x, y), (x, 1 - y), (1 - x, 1 - y)]

        def slot(p):
            return 4 * p[0] + 2 * p[1] + p[2]

        def copy(i, k, block, to, src=None):
            dst = outs[i].at[slot(block)]
            return pltpu.make_async_remote_copy(
                src_ref=dst if src is None else src, dst_ref=dst,
                send_sem=send_sems.at[i, k], recv_sem=recv_sems.at[i, k],
                device_id=to, device_id_type=MESH)

        mine = [pltpu.make_async_copy(ins[i], outs[i].at[slot(me)], local_sems.at[i]) for i in range(n)]
        for cp in mine:
            cp.start()
        first = []
        for i in range(n):
            first.append(copy(i, 0, me, sibling, src=ins[i]))
            for j, chip in enumerate(chips):
                first.append(copy(i, 1 + j, me, (*chip, c), src=ins[i]))
        for cp in first:
            cp.start()
        passed = []
        for j, chip in enumerate(chips):
            for i in range(n):
                copy(i, 1 + j, (*chip, c), me).wait_recv()
                fwd = copy(i, 4 + j, (*chip, c), sibling)
                fwd.start()
                passed.append(fwd)
        for i in range(n):
            copy(i, 0, sibling, me).wait_recv()
            for j, chip in enumerate(chips):
                copy(i, 4 + j, (*chip, 1 - c), me).wait_recv()
        for cp in first + passed:
            cp.wait_send()
        for cp in mine:
            cp.wait()

    return pl.pallas_call(
        body, name=name,
        out_shape=[jax.ShapeDtypeStruct((NDEV,) + s.shape, s.dtype) for s in shards],
        in_specs=[ANY] * (n + len(after)), out_specs=[ANY] * n,
        scratch_shapes=[pltpu.SemaphoreType.DMA((n, 7)), pltpu.SemaphoreType.DMA((n, 7)),
                        pltpu.SemaphoreType.DMA((n,))],
    )(*shards, *after)


def _peer(x, y, c, r):
    return (1 - x if r & 4 else x, 1 - y if r & 2 else y, 1 - c if r & 1 else c)


def _slot(p):
    return 4 * p[0] + 2 * p[1] + p[2]


def _cast_fill(name, w, layer, block):
    _, rows, cols = w.shape
    tr = _tile(rows, 256)

    def body(blk_ref, w_ref, o_ref):
        o_ref[...] = w_ref[...].astype(BF16)

    return pl.pallas_call(
        body, name=name,
        grid_spec=pltpu.PrefetchScalarGridSpec(
            num_scalar_prefetch=1, grid=(rows // tr,),
            in_specs=[pl.BlockSpec((None, tr, cols), lambda i, blk: (layer, i, 0))],
            out_specs=pl.BlockSpec((None, tr, cols), lambda i, blk: (blk[0], i, 0))),
        out_shape=jax.ShapeDtypeStruct((NDEV, rows, cols), BF16),
        compiler_params=_cp(("parallel",)))(block, w)


OTHER_CHIPS = (2, 4, 6)
AG_SEMS = 6


def _ag_start(name, lands, after):
    n = len(lands)

    def body(*refs):
        ins, sems = refs[:n], refs[n + 1:n + 1 + AG_SEMS * n]
        x, y, c = lax.axis_index("x"), lax.axis_index("y"), lax.axis_index("c")
        mine = _slot((x, y, c))
        for i in range(n):
            send_a, *recv_a, _, recv_b = sems[AG_SEMS * i:AG_SEMS * (i + 1)]
            block = ins[i].at[mine]
            pltpu.make_async_remote_copy(src_ref=block, dst_ref=block, send_sem=send_a, recv_sem=recv_b,
                                         device_id=_peer(x, y, c, 1), device_id_type=MESH).start()
            for k, r in enumerate(OTHER_CHIPS):
                pltpu.make_async_remote_copy(src_ref=block, dst_ref=block, send_sem=send_a, recv_sem=recv_a[k],
                                             device_id=_peer(x, y, c, r), device_id_type=MESH).start()

    outs = pl.pallas_call(
        body, name=name,
        out_shape=[pltpu.SemaphoreType.DMA(())] * (AG_SEMS * n) + [pltpu.HBM(a.shape, a.dtype) for a in lands],
        in_specs=[HBM] * n + [ANY], out_specs=[SEM] * (AG_SEMS * n) + [HBM] * n,
        input_output_aliases={i: AG_SEMS * n + i for i in range(n)},
        compiler_params=pltpu.CompilerParams(has_side_effects=EFFECT),
    )(*[pltpu.with_memory_space_constraint(a, pltpu.HBM) for a in lands], after)
    return outs[AG_SEMS * n:], [tuple(outs[AG_SEMS * i:AG_SEMS * (i + 1)]) for i in range(n)]


def _ag_mid(name, land, sems, after):
    _, *recv_a, send_b, recv_b = sems

    def body(land_ref, ra0, ra1, ra2, send_b_ref, recv_b_ref, after_ref, land_out):
        x, y, c = lax.axis_index("x"), lax.axis_index("y"), lax.axis_index("c")
        sibling = _peer(x, y, c, 1)
        for arrival, r in zip((ra0, ra1, ra2), OTHER_CHIPS):
            block = land_ref.at[_slot(_peer(x, y, c, r))]
            pltpu.make_async_remote_copy(src_ref=block, dst_ref=block, send_sem=send_b_ref, recv_sem=arrival,
                                         device_id=sibling, device_id_type=MESH).wait_recv()
            pltpu.make_async_remote_copy(src_ref=block, dst_ref=block, send_sem=send_b_ref, recv_sem=recv_b_ref,
                                         device_id=sibling, device_id_type=MESH).start()

    return pl.pallas_call(
        body, name=name, out_shape=pltpu.HBM(land.shape, land.dtype),
        in_specs=[HBM] + [SEM] * 5 + [ANY], out_specs=HBM, input_output_aliases={0: 0},
        compiler_params=pltpu.CompilerParams(has_side_effects=EFFECT),
    )(land, *recv_a, send_b, recv_b, after)


def _ag_wait(name, land, sems, after):
    send_a, _, _, _, send_b, recv_b = sems

    def body(land_ref, send_a_ref, send_b_ref, recv_b_ref, after_ref, land_out):
        x, y, c = lax.axis_index("x"), lax.axis_index("y"), lax.axis_index("c")
        sibling = _peer(x, y, c, 1)
        four = land_ref.at[pl.ds(0, 1 + len(OTHER_CHIPS))]
        three = land_ref.at[pl.ds(0, len(OTHER_CHIPS))]
        first = pltpu.make_async_remote_copy(src_ref=four, dst_ref=four, send_sem=send_a_ref, recv_sem=recv_b_ref,
                                             device_id=sibling, device_id_type=MESH)
        passed = pltpu.make_async_remote_copy(src_ref=three, dst_ref=three, send_sem=send_b_ref, recv_sem=recv_b_ref,
                                              device_id=sibling, device_id_type=MESH)
        first.wait_send()
        passed.wait_send()
        first.wait_recv()

    return pl.pallas_call(
        body, name=name, out_shape=pltpu.HBM(land.shape, land.dtype),
        in_specs=[HBM, SEM, SEM, SEM, ANY], out_specs=HBM, input_output_aliases={0: 0},
        compiler_params=pltpu.CompilerParams(has_side_effects=EFFECT),
    )(land, send_a, send_b, recv_b, after)


def _fill_slot(name, rows, block):
    r, c = rows.shape

    def body(blk_ref, i_ref, o_ref):
        o_ref[...] = i_ref[...]

    return pl.pallas_call(
        body, name=name,
        grid_spec=pltpu.PrefetchScalarGridSpec(
            num_scalar_prefetch=1, grid=(1,),
            in_specs=[pl.BlockSpec((r, c), lambda i, blk: (0, 0))],
            out_specs=pl.BlockSpec((None, r, c), lambda i, blk: (blk[0], 0, 0))),
        out_shape=jax.ShapeDtypeStruct((NDEV, r, c), rows.dtype),
        compiler_params=_cp(("arbitrary",)))(block, rows)


def _ag_direct_start(name, land):
    def body(land_ref, send, recv, land_out):
        x, y, c = lax.axis_index("x"), lax.axis_index("y"), lax.axis_index("c")
        block = land_ref.at[_slot((x, y, c))]
        for r in range(1, NDEV):
            pltpu.make_async_remote_copy(src_ref=block, dst_ref=block, send_sem=send, recv_sem=recv,
                                         device_id=_peer(x, y, c, r), device_id_type=MESH).start()

    send, recv, land_thru = pl.pallas_call(
        body, name=name,
        out_shape=[pltpu.SemaphoreType.DMA(()), pltpu.SemaphoreType.DMA(()), pltpu.HBM(land.shape, land.dtype)],
        in_specs=[HBM], out_specs=[SEM, SEM, HBM], input_output_aliases={0: 2},
        compiler_params=pltpu.CompilerParams(has_side_effects=EFFECT),
    )(pltpu.with_memory_space_constraint(land, pltpu.HBM))
    return land_thru, send, recv


def _ag_direct_wait(name, land, send, recv, after):
    def body(land_ref, send_ref, recv_ref, after_ref, land_out):
        x, y, c = lax.axis_index("x"), lax.axis_index("y"), lax.axis_index("c")
        seven = land_ref.at[pl.ds(0, NDEV - 1)]
        copy = pltpu.make_async_remote_copy(src_ref=seven, dst_ref=seven, send_sem=send_ref, recv_sem=recv_ref,
                                            device_id=_peer(x, y, c, 1), device_id_type=MESH)
        copy.wait_send()
        copy.wait_recv()

    return pl.pallas_call(
        body, name=name, out_shape=pltpu.HBM(land.shape, land.dtype),
        in_specs=[HBM, SEM, SEM, ANY], out_specs=HBM, input_output_aliases={0: 0},
        compiler_params=pltpu.CompilerParams(has_side_effects=EFFECT),
    )(land, send, recv, after)


def _wait_all(name, src, land, send, recv, after):
    def body(src_ref, land_ref, send_ref, recv_ref, after_ref, src_out, land_out):
        x, y, c = lax.axis_index("x"), lax.axis_index("y"), lax.axis_index("c")
        seven = land_ref.at[pl.ds(0, NDEV - 1)]
        copy = pltpu.make_async_remote_copy(src_ref=seven, dst_ref=seven, send_sem=send_ref, recv_sem=recv_ref,
                                            device_id=_peer(x, y, c, 1), device_id_type=MESH)
        copy.wait_send()
        copy.wait_recv()

    return pl.pallas_call(
        body, name=name,
        out_shape=[pltpu.HBM(src.shape, src.dtype), pltpu.HBM(land.shape, land.dtype)],
        in_specs=[HBM, HBM, SEM, SEM, ANY], out_specs=[HBM, HBM],
        input_output_aliases={0: 0, 1: 1},
        compiler_params=pltpu.CompilerParams(has_side_effects=EFFECT),
    )(src, land, send, recv, after)


def _rs_start(name, grad):
    land = lax.empty((NDEV - 1,) + grad.shape[1:], grad.dtype)

    def body(g_ref, land_ref, send, recv, g_out, land_out):
        x, y, c = lax.axis_index("x"), lax.axis_index("y"), lax.axis_index("c")
        for r in range(1, NDEV):
            peer = _peer(x, y, c, r)
            pltpu.make_async_remote_copy(
                src_ref=g_ref.at[_slot(peer)], dst_ref=land_ref.at[r - 1], send_sem=send, recv_sem=recv,
                device_id=peer, device_id_type=MESH).start()

    send, recv, g_thru, land_thru = pl.pallas_call(
        body, name=name,
        out_shape=[pltpu.SemaphoreType.DMA(()), pltpu.SemaphoreType.DMA(()),
                   pltpu.HBM(grad.shape, grad.dtype), pltpu.HBM(land.shape, land.dtype)],
        in_specs=[HBM, HBM], out_specs=[SEM, SEM, HBM, HBM], input_output_aliases={0: 2, 1: 3},
        compiler_params=pltpu.CompilerParams(has_side_effects=EFFECT),
    )(pltpu.with_memory_space_constraint(grad, pltpu.HBM), pltpu.with_memory_space_constraint(land, pltpu.HBM))
    return g_thru, land_thru, send, recv


def _adam_math(w, g, m, v):
    m = ADAM_B1 * m + (1.0 - ADAM_B1) * g
    v = ADAM_B2 * v + (1.0 - ADAM_B2) * (g * g)
    m_hat = m / (1.0 - ADAM_B1 ** ADAM_STEP)
    v_hat = v / (1.0 - ADAM_B2 ** ADAM_STEP)
    delta = -ADAM_LR * (m_hat / (jnp.sqrt(v_hat) + ADAM_EPS) + ADAM_WD * w)
    return delta, m, v


def _adamw(name, parts, w, m, v):
    nparts, rows, cols = parts.shape
    tr = _tile(rows, 256)

    def body(p_ref, w_ref, m_ref, v_ref, g_out, d_out, m_out, v_out):
        g = p_ref[0].astype(F32)
        for k in range(1, nparts):
            g = g + p_ref[k].astype(F32)
        delta, mn, vn = _adam_math(w_ref[...], g, m_ref[...], v_ref[...])
        g_out[...] = g
        d_out[...] = delta
        m_out[...] = mn
        v_out[...] = vn

    row = pl.BlockSpec((tr, cols), lambda i: (i, 0))
    return pl.pallas_call(
        body, name=name, grid=(rows // tr,),
        in_specs=[pl.BlockSpec((nparts, tr, cols), lambda i: (0, i, 0)), row, row, row],
        out_specs=[row] * 4,
        out_shape=[jax.ShapeDtypeStruct((rows, cols), F32)] * 4,
        compiler_params=_cp(("parallel",)),
    )(parts, w, m, v)


def _adamw_layers(name, grads, lands, block, w, m, v):
    layers, rows, cols = w.shape
    nland = lands[0].shape[0]
    tr = rows
    while tr % 2 == 0 and tr > 8 and nland * tr * cols * 2 > (1 << 20):
        tr //= 2

    def body(blk_ref, *refs):
        own_refs, land_refs = refs[:layers], refs[layers:2 * layers]
        w_ref, m_ref, v_ref, g_out, d_out, m_out, v_out = refs[2 * layers:]
        layer = pl.program_id(0)
        for k in range(layers):
            @pl.when(layer == k)
            def _(k=k):
                g = own_refs[k][...].astype(F32)
                for s in range(nland):
                    g = g + land_refs[k][s].astype(F32)
                delta, mn, vn = _adam_math(w_ref[...], g, m_ref[...], v_ref[...])
                g_out[...] = g
                d_out[...] = delta
                m_out[...] = mn
                v_out[...] = vn

    def own_spec(k):
        return pl.BlockSpec((None, tr, cols), lambda l, i, blk: (blk[0], jnp.where(l == k, i, 0), 0))

    def land_spec(k):
        return pl.BlockSpec((nland, tr, cols), lambda l, i, blk: (0, jnp.where(l == k, i, 0), 0))

    row = pl.BlockSpec((None, tr, cols), lambda l, i, blk: (l, i, 0))
    return pl.pallas_call(
        body, name=name,
        grid_spec=pltpu.PrefetchScalarGridSpec(
            num_scalar_prefetch=1, grid=(layers, rows // tr),
            in_specs=[own_spec(k) for k in range(layers)] + [land_spec(k) for k in range(layers)] + [row, row, row],
            out_specs=[row] * 4),
        out_shape=[jax.ShapeDtypeStruct((layers, rows, cols), F32)] * 4,
        compiler_params=_cp(("arbitrary", "arbitrary")),
    )(block, *grads, *lands, w, m, v)


LANES = 128


def _seg_scratch(rows, d):
    return pltpu.VMEM((d // LANES, rows, LANES), F32)


def _to_segments(vals, scratch, out_ref):
    per = scratch.shape[1] // SEGS
    for c in range(scratch.shape[0]):
        cols = slice(c * LANES, (c + 1) * LANES)
        scratch[c] = vals[:, cols]
        for s in range(SEGS):
            out_ref[s, :, cols] = scratch.at[c][pl.ds(s, per, stride=SEGS), :].astype(out_ref.dtype)


def _from_segments(in_ref, scratch):
    per = scratch.shape[1] // SEGS
    for c in range(scratch.shape[0]):
        for s in range(SEGS):
            scratch.at[c][pl.ds(s, per, stride=SEGS), :] = in_ref[s, :, c * LANES:(c + 1) * LANES].astype(F32)
    return jnp.concatenate([scratch[c] for c in range(scratch.shape[0])], axis=1)


def _seg_view(a):
    return a.reshape(SEGS, a.shape[0] // SEGS, a.shape[1])


def _norm_fwd(name, h, z=None, g_post=None, g_pre=None, seg_z=False, seg_y=False):
    rows, d = h.shape
    tm = _tile(rows, 256)
    has_post, has_pre = z is not None, g_pre is not None
    nscratch = int(seg_z) + int(seg_y)

    def body(*refs):
        scratch = list(refs[len(refs) - nscratch:])
        it = iter(refs)
        hv = next(it)[...]
        if has_post:
            z_ref = next(it)
            zv = _from_segments(z_ref, scratch.pop(0)) if seg_z else z_ref[...].astype(F32)
            gp = next(it)[...]
        if has_pre:
            gq = next(it)[...]
        if has_post:
            r = lax.rsqrt(jnp.mean(zv * zv, axis=-1, keepdims=True) + RMS_EPS)
            hv = hv + (zv * r) * gp
            next(it)[...] = hv
        if has_pre:
            r = lax.rsqrt(jnp.mean(hv * hv, axis=-1, keepdims=True) + RMS_EPS)
            y = (hv * r) * gq
            if seg_y:
                _to_segments(y, scratch.pop(0), next(it))
            else:
                next(it)[...] = y.astype(BF16)

    row = pl.BlockSpec((tm, d), lambda i: (i, 0))
    seg = pl.BlockSpec((SEGS, tm // SEGS, d), lambda i: (0, i, 0))
    vec = pl.BlockSpec((1, d), lambda i: (0, 0))
    ins, in_specs, out_shape, out_specs = [h], [row], [], []
    if has_post:
        ins += [_seg_view(z) if seg_z else z, g_post]
        in_specs += [seg if seg_z else row, vec]
        out_shape.append(jax.ShapeDtypeStruct((rows, d), F32))
        out_specs.append(row)
    if has_pre:
        ins.append(g_pre)
        in_specs.append(vec)
        out_shape.append(jax.ShapeDtypeStruct((SEGS, rows // SEGS, d) if seg_y else (rows, d), BF16))
        out_specs.append(seg if seg_y else row)
    outs = pl.pallas_call(body, name=name, grid=(rows // tm,), in_specs=in_specs, out_specs=out_specs,
                          out_shape=out_shape, scratch_shapes=[_seg_scratch(tm, d)] * nscratch,
                          compiler_params=_cp(("parallel",)))(*ins)
    if seg_y:
        outs = list(outs[:-1]) + [outs[-1].reshape(rows, d)]
    return outs


def _rms_bwd_rows(x, g, dy):
    r = lax.rsqrt(jnp.mean(x * x, axis=-1, keepdims=True) + RMS_EPS)
    xn = x * r
    dg = jnp.sum(dy * xn, axis=0, keepdims=True)
    dxn = dy * g
    dx = r * (dxn - xn * jnp.mean(dxn * xn, axis=-1, keepdims=True))
    return dx, dg


def _norm_bwd(name, d_out, pre=None, post=None, seg_dy=False, seg_z=False):
    rows, d = d_out.shape
    tm = _tile(rows, 256)
    has_pre, has_post = pre is not None, post is not None
    nscratch = int(seg_dy) + 2 * int(seg_z)

    def body(*refs):
        scratch = list(refs[len(refs) - nscratch:])
        it = iter(refs)
        dres = next(it)[...]
        if has_pre:
            dy_ref = next(it)
            dy = _from_segments(dy_ref, scratch.pop(0)) if seg_dy else dy_ref[...].astype(F32)
            xp, gq = next(it)[...], next(it)[...]
        if has_post:
            z_ref = next(it)
            zv = _from_segments(z_ref, scratch.pop(0)) if seg_z else z_ref[...].astype(F32)
            gp = next(it)[...]
        first = pl.program_id(0) == 0
        if has_pre:
            dx, dg = _rms_bwd_rows(xp, gq, dy)
            dres = dres + dx
            next(it)[...] = dres
            dg_ref = next(it)

            @pl.when(first)
            def _():
                dg_ref[...] = jnp.zeros_like(dg_ref)
            dg_ref[...] += dg
        if has_post:
            dz, dg2 = _rms_bwd_rows(zv, gp, dres)
            if seg_z:
                _to_segments(dz, scratch.pop(0), next(it))
            else:
                next(it)[...] = dz.astype(BF16)
            dg2_ref = next(it)

            @pl.when(first)
            def _():
                dg2_ref[...] = jnp.zeros_like(dg2_ref)
            dg2_ref[...] += dg2

    row = pl.BlockSpec((tm, d), lambda i: (i, 0))
    seg = pl.BlockSpec((SEGS, tm // SEGS, d), lambda i: (0, i, 0))
    vec = pl.BlockSpec((1, d), lambda i: (0, 0))
    ins, in_specs, out_shape, out_specs = [d_out], [row], [], []
    if has_pre:
        d_y, x_pre, g_pre = pre
        ins += [_seg_view(d_y) if seg_dy else d_y, x_pre, g_pre]
        in_specs += [seg if seg_dy else row, row, vec]
        out_shape += [jax.ShapeDtypeStruct((rows, d), F32), jax.ShapeDtypeStruct((1, d), F32)]
        out_specs += [row, vec]
    if has_post:
        z, g_post = post
        ins += [_seg_view(z) if seg_z else z, g_post]
        in_specs += [seg if seg_z else row, vec]
        out_shape += [jax.ShapeDtypeStruct((SEGS, rows // SEGS, d) if seg_z else (rows, d), BF16),
                      jax.ShapeDtypeStruct((1, d), F32)]
        out_specs += [seg if seg_z else row, vec]
    outs = pl.pallas_call(body, name=name, grid=(rows // tm,), in_specs=in_specs, out_specs=out_specs,
                          out_shape=out_shape, scratch_shapes=[_seg_scratch(tm, d)] * nscratch,
                          compiler_params=_cp(("arbitrary",)))(*ins)
    if seg_z:
        outs = list(outs)
        outs[-2] = outs[-2].reshape(rows, d)
    return outs


def _loss_grad(name, y, target):
    rows, d = y.shape
    tm = _tile(rows, 256)

    def body(y_ref, t_ref, dy_ref, loss_ref):
        err = y_ref[...] - t_ref[...]
        dy_ref[...] = err * (1.0 / d)

        @pl.when(pl.program_id(0) == 0)
        def _():
            loss_ref[...] = jnp.zeros_like(loss_ref)
        loss_ref[...] += jnp.full(loss_ref.shape, (0.5 / d) * jnp.sum(err * err), F32)

    row = pl.BlockSpec((tm, d), lambda i: (i, 0))
    return pl.pallas_call(
        body, name=name, grid=(rows // tm,), in_specs=[row, row],
        out_specs=[row, pl.BlockSpec((1, 128), lambda i: (0, 0))],
        out_shape=[jax.ShapeDtypeStruct((rows, d), F32), jax.ShapeDtypeStruct((1, 128), F32)],
        compiler_params=_cp(("arbitrary",)))(y, target)


NT_DIMS = (((1,), (1,)), ((), ()))
TN_DIMS = (((0,), (0,)), ((), ()))


def _mm_nn_blk(name, a, wblk, relu2=False):
    m, k = a.shape
    nb = wblk.shape[2]
    tm = _tile(m, 1024)

    def body(a_ref, w_ref, o_ref):
        r = jnp.dot(a_ref[...], w_ref[...], preferred_element_type=F32)
        if relu2:
            rr = jnp.maximum(r, 0.0)
            r = rr * rr
        o_ref[...] = r.astype(BF16)

    return pl.pallas_call(
        body, name=name, grid=(NDEV, m // tm),
        in_specs=[pl.BlockSpec((tm, k), lambda d, i: (i, 0)), pl.BlockSpec((None, k, nb), lambda d, i: (d, 0, 0))],
        out_specs=pl.BlockSpec((tm, nb), lambda d, i: (i, d)),
        out_shape=jax.ShapeDtypeStruct((m, NDEV * nb), BF16),
        compiler_params=_cp(("parallel", "parallel"), VMEM_MM))(a, wblk)


def _accumulate(acc, o_ref, r, step, last):
    if acc is None:
        o_ref[...] = r.astype(o_ref.dtype)
        return

    @pl.when(step == 0)
    def _():
        acc[...] = r

    @pl.when(jnp.logical_and(step > 0, step < last))
    def _():
        acc[...] += r

    @pl.when(jnp.logical_and(step > 0, step == last))
    def _():
        o_ref[...] = (acc[...] + r).astype(o_ref.dtype)


def _mm_nn(name, a, w):
    m, kb = a.shape
    n = w.shape[1]
    tm, tk = _tile(m, 512), _tile(kb, 2048)
    steps = kb // tk

    def body(a_ref, w_ref, o_ref, *scratch):
        r = jnp.dot(a_ref[...], w_ref[...], preferred_element_type=F32)
        _accumulate(scratch[0] if scratch else None, o_ref, r, pl.program_id(1), steps - 1)

    return pl.pallas_call(
        body, name=name, grid=(m // tm, steps),
        in_specs=[pl.BlockSpec((tm, tk), lambda i, s: (i, s)), pl.BlockSpec((tk, n), lambda i, s: (s, 0))],
        out_specs=pl.BlockSpec((tm, n), lambda i, s: (i, 0)),
        out_shape=jax.ShapeDtypeStruct((m, n), BF16),
        scratch_shapes=[pltpu.VMEM((tm, n), F32)] if steps > 1 else [],
        compiler_params=_cp(("parallel", "arbitrary"), VMEM_MM))(a, w)


def _mm_nt_rows(name, dy, w, act=None, out_dtype=BF16):
    m, n = dy.shape
    kw = w.shape[0]
    tm, tkw = _tile(m, 1024), _tile(kw, 1024)

    def body(dy_ref, w_ref, *rest):
        r = lax.dot_general(dy_ref[...], w_ref[...], NT_DIMS, preferred_element_type=F32)
        if act is None:
            rest[0][...] = r.astype(out_dtype)
        else:
            rest[1][...] = (r * (2.0 * jnp.sqrt(rest[0][...].astype(F32)))).astype(BF16)

    ins = [dy, w]
    in_specs = [pl.BlockSpec((tm, n), lambda j, i: (i, 0)), pl.BlockSpec((tkw, n), lambda j, i: (j, 0))]
    if act is not None:
        ins.append(act)
        in_specs.append(pl.BlockSpec((tm, tkw), lambda j, i: (i, j)))
    return pl.pallas_call(
        body, name=name, grid=(kw // tkw, m // tm), in_specs=in_specs,
        out_specs=pl.BlockSpec((tm, tkw), lambda j, i: (i, j)),
        out_shape=jax.ShapeDtypeStruct((m, kw), out_dtype if act is None else BF16),
        compiler_params=_cp(("parallel", "parallel"), VMEM_MM))(*ins)


def _mm_nt_blk(name, dy, wblk, after=None):
    m = dy.shape[0]
    _, kw, nb = wblk.shape
    tm, per = _tile(m, 512), 2

    extra = list(after or ())

    def body(dy_ref, w_ref, *rest):
        o_ref, acc = rest[len(extra):]
        r = lax.dot_general(dy_ref[:, :nb], w_ref[0], NT_DIMS, preferred_element_type=F32)
        for t in range(1, per):
            r = r + lax.dot_general(dy_ref[:, t * nb:(t + 1) * nb], w_ref[t], NT_DIMS, preferred_element_type=F32)
        _accumulate(acc, o_ref, r, pl.program_id(1), NDEV // per - 1)

    return pl.pallas_call(
        body, name=name, grid=(m // tm, NDEV // per),
        in_specs=[pl.BlockSpec((tm, per * nb), lambda i, s: (i, s)),
                  pl.BlockSpec((per, kw, nb), lambda i, s: (s, 0, 0))] + [ANY] * len(extra),
        out_specs=pl.BlockSpec((tm, kw), lambda i, s: (i, 0)),
        out_shape=jax.ShapeDtypeStruct((m, kw), BF16),
        scratch_shapes=[pltpu.VMEM((tm, kw), F32)],
        compiler_params=_cp(("parallel", "arbitrary"), VMEM_MM))(dy, wblk, *extra)


def _mm_tn(name, x, dy, nb=None):
    t, mx = x.shape
    n = dy.shape[1]
    tmx = _tile(mx, 512)
    tn = nb if nb is not None else _tile(n, 1024)

    def body(x_ref, dy_ref, o_ref):
        o_ref[...] = lax.dot_general(x_ref[...], dy_ref[...], TN_DIMS, preferred_element_type=F32).astype(BF16)

    if nb is None:
        out_shape = jax.ShapeDtypeStruct((mx, n), BF16)
        out_spec = pl.BlockSpec((tmx, tn), lambda j, i: (i, j))
    else:
        out_shape = jax.ShapeDtypeStruct((NDEV, mx, nb), BF16)
        out_spec = pl.BlockSpec((None, tmx, nb), lambda j, i: (j, i, 0))
    return pl.pallas_call(
        body, name=name, grid=(n // tn, mx // tmx),
        in_specs=[pl.BlockSpec((t, tmx), lambda j, i: (0, i)), pl.BlockSpec((t, tn), lambda j, i: (0, j))],
        out_specs=out_spec, out_shape=out_shape,
        compiler_params=_cp(("parallel", "parallel"), VMEM_MM))(x, dy)


def _gelu(x):
    return 0.5 * x * (1.0 + jnp.tanh(GELU_C * (x + GELU_A * (x * x * x))))


def _gelu_grad(x):
    t = jnp.tanh(GELU_C * (x + GELU_A * (x * x * x)))
    return 0.5 * (1.0 + t) + 0.5 * x * (1.0 - t * t) * (GELU_C * (1.0 + 3.0 * GELU_A * (x * x)))


def _layernorm(a):
    mu = jnp.mean(a, axis=-1, keepdims=True)
    ac = a - mu
    rstd = lax.rsqrt(jnp.mean(ac * ac, axis=-1, keepdims=True) + LN_EPS)
    return ac * rstd, rstd


def _shift_rows(z, halo, k):
    zr = pltpu.roll(z, k, 0)
    hr = pltpu.roll(halo, k, 0)
    row = lax.broadcasted_iota(jnp.int32, hr.shape, 0)
    top = jnp.where(row < k, hr, zr[:HALO])
    return jnp.concatenate([top, zr[HALO:]], axis=0)


def _shift_rows_up(z, halo, k):
    rows = z.shape[0]
    zr = pltpu.roll(z, rows - k, 0)
    hr = pltpu.roll(halo, HALO - k, 0)
    row = lax.broadcasted_iota(jnp.int32, hr.shape, 0)
    bot = jnp.where(row >= HALO - k, hr, zr[rows - HALO:])
    return jnp.concatenate([zr[:rows - HALO], bot], axis=0)


def _causal_mask():
    t = lax.broadcasted_iota(jnp.int32, (CHUNK, CHUNK), 0)
    s = lax.broadcasted_iota(jnp.int32, (CHUNK, CHUNK), 1)
    return s <= t


def _gate_specs(tm, rows, width):
    per = tm // HALO_BF16
    last = rows // HALO_BF16 - 1
    cur = pl.BlockSpec((tm, width), lambda i: (i, 0))
    prev = pl.BlockSpec((HALO_BF16, width), lambda i: (jnp.maximum(i * per - 1, 0), 0))
    nxt = pl.BlockSpec((HALO_BF16, width), lambda i: (jnp.minimum((i + 1) * per, last), 0))
    return cur, prev, nxt


def _cols(ref, lo, hi):
    return ref[:, lo:hi].astype(F32)


def _halo_before(ref, lo, hi):
    return ref[:, lo:hi].astype(F32)[HALO_BF16 - HALO:]


def _halo_after(ref, lo, hi):
    return ref[:, lo:hi].astype(F32)[:HALO]


def _gate_fwd(name, proj, w_s, b_st, cw):
    rows, width = proj.shape
    w = width // 5
    groups = w // CHUNK
    tm = _tile(rows, 256)
    cur, prev, _ = _gate_specs(tm, rows, width)

    def body(p_ref, h_ref, ws_ref, b_ref, cw_ref, o_ref):
        mask = _causal_mask()
        au = _gelu(_cols(p_ref, 0, w))
        vn, _ = _layernorm(_gelu(_cols(p_ref, w, 2 * w)))
        vn = vn.astype(BF16)
        for g in range(groups):
            wc = jnp.where(mask, ws_ref[g], 0.0).astype(BF16)
            cols = slice(g * CHUNK, (g + 1) * CHUNK)
            for ch in range(tm // CHUNK):
                rws = slice(ch * CHUNK, (ch + 1) * CHUNK)
                mixed = jnp.dot(wc, vn[rws, cols], preferred_element_type=F32) + b_ref[:, g:g + 1]
                o_ref[rws, cols] = (au[rws, cols] * mixed).astype(BF16)
        z = _cols(p_ref, 3 * w, 4 * w) * _cols(p_ref, 4 * w, 5 * w)
        zh = _halo_before(h_ref, 3 * w, 4 * w) * _halo_before(h_ref, 4 * w, 5 * w)
        zh = jnp.where(pl.program_id(0) == 0, 0.0, zh)
        y = cw_ref[0:1, :] * _shift_rows(z, zh, 2) + cw_ref[1:2, :] * _shift_rows(z, zh, 1) + cw_ref[2:3, :] * z
        o_ref[:, w:2 * w] = (_cols(p_ref, 2 * w, 3 * w) * y).astype(BF16)

    full = lambda a: pl.BlockSpec(a.shape, lambda i: (0,) * a.ndim)
    return pl.pallas_call(
        body, name=name, grid=(rows // tm,),
        in_specs=[cur, prev, full(w_s), full(b_st), full(cw)],
        out_specs=pl.BlockSpec((tm, 2 * w), lambda i: (i, 0)),
        out_shape=jax.ShapeDtypeStruct((rows, 2 * w), BF16),
        compiler_params=_cp(("parallel",)))(proj, proj, w_s, b_st, cw)


def _gate_bwd(name, proj, d_ab, w_s, b_st, cw):
    rows, width = proj.shape
    w = width // 5
    groups = w // CHUNK
    tm = _tile(rows, 256)
    cur, prev, nxt = _gate_specs(tm, rows, width)
    dcur, _, dnxt = _gate_specs(tm, rows, 2 * w)

    def body(p_ref, ph_ref, pn_ref, d_ref, dn_ref, ws_ref, b_ref, cw_ref, o_ref, dws_ref, dbs_ref, dcw_ref):
        i = pl.program_id(0)

        @pl.when(i == 0)
        def _():
            dws_ref[...] = jnp.zeros_like(dws_ref)
            dbs_ref[...] = jnp.zeros_like(dbs_ref)
            dcw_ref[...] = jnp.zeros_like(dcw_ref)

        mask = _causal_mask()
        u, v = _cols(p_ref, 0, w), _cols(p_ref, w, 2 * w)
        au, av = _gelu(u), _gelu(v)
        vn, rstd = _layernorm(av)
        vnb = vn.astype(BF16)
        d_a = _cols(d_ref, 0, w)
        d_mixed = (d_a * au).astype(BF16)
        ones = jnp.ones((HALO, CHUNK), BF16)
        d_vn_cols = []
        d_au_cols = []
        for g in range(groups):
            wc = jnp.where(mask, ws_ref[g], 0.0).astype(BF16)
            cols = slice(g * CHUNK, (g + 1) * CHUNK)
            dw = jnp.zeros((CHUNK, CHUNK), F32)
            db = jnp.zeros((HALO, CHUNK), F32)
            d_vn_rows, d_au_rows = [], []
            for ch in range(tm // CHUNK):
                rws = slice(ch * CHUNK, (ch + 1) * CHUNK)
                mixed = jnp.dot(wc, vnb[rws, cols], preferred_element_type=F32) + b_ref[:, g:g + 1]
                d_au_rows.append(d_a[rws, cols] * mixed)
                dm = d_mixed[rws, cols]
                dw = dw + lax.dot_general(dm, vnb[rws, cols], NT_DIMS, preferred_element_type=F32)
                db = db + lax.dot_general(ones, dm, NT_DIMS, preferred_element_type=F32)
                d_vn_rows.append(lax.dot_general(wc, dm, TN_DIMS, preferred_element_type=F32))
            dws_ref[g] += jnp.where(mask, dw, 0.0)
            dbs_ref[g:g + 1, :] += db[0:1, :]
            d_vn_cols.append(jnp.concatenate(d_vn_rows, axis=0))
            d_au_cols.append(jnp.concatenate(d_au_rows, axis=0))
        d_vn = jnp.concatenate(d_vn_cols, axis=1)
        d_au = jnp.concatenate(d_au_cols, axis=1)
        d_av = rstd * (d_vn - jnp.mean(d_vn, axis=-1, keepdims=True)
                       - vn * jnp.mean(d_vn * vn, axis=-1, keepdims=True))
        o_ref[:, 0:w] = (d_au * _gelu_grad(u)).astype(BF16)
        o_ref[:, w:2 * w] = (d_av * _gelu_grad(v)).astype(BF16)

        gb, gc, bx = _cols(p_ref, 2 * w, 3 * w), _cols(p_ref, 3 * w, 4 * w), _cols(p_ref, 4 * w, 5 * w)
        z = gc * bx
        zh = jnp.where(i == 0, 0.0, _halo_before(ph_ref, 3 * w, 4 * w) * _halo_before(ph_ref, 4 * w, 5 * w))
        z1, z2 = _shift_rows(z, zh, 1), _shift_rows(z, zh, 2)
        d_b = _cols(d_ref, w, 2 * w)
        y = cw_ref[0:1, :] * z2 + cw_ref[1:2, :] * z1 + cw_ref[2:3, :] * z
        dy = d_b * gb
        dyn = jnp.where(i == pl.num_programs(0) - 1, 0.0,
                        _halo_after(dn_ref, w, 2 * w) * _halo_after(pn_ref, 2 * w, 3 * w))
        dz = (cw_ref[2:3, :] * dy + cw_ref[1:2, :] * _shift_rows_up(dy, dyn, 1)
              + cw_ref[0:1, :] * _shift_rows_up(dy, dyn, 2))
        dcw_ref[0:1, :] += jnp.sum(dy * z2, axis=0, keepdims=True)
        dcw_ref[1:2, :] += jnp.sum(dy * z1, axis=0, keepdims=True)
        dcw_ref[2:3, :] += jnp.sum(dy * z, axis=0, keepdims=True)
        o_ref[:, 2 * w:3 * w] = (d_b * y).astype(BF16)
        o_ref[:, 3 * w:4 * w] = (dz * bx).astype(BF16)
        o_ref[:, 4 * w:5 * w] = (dz * gc).astype(BF16)

    full = lambda a: pl.BlockSpec(a.shape, lambda i: (0,) * a.ndim)
    acc = lambda shape: pl.BlockSpec(shape, lambda i: (0,) * len(shape))
    return pl.pallas_call(
        body, name=name, grid=(rows // tm,),
        in_specs=[cur, prev, nxt, dcur, dnxt, full(w_s), full(b_st), full(cw)],
        out_specs=[pl.BlockSpec((tm, width), lambda i: (i, 0)), acc((groups, CHUNK, CHUNK)),
                   acc((groups, CHUNK)), acc((HALO, w))],
        out_shape=[jax.ShapeDtypeStruct((rows, width), BF16), jax.ShapeDtypeStruct((groups, CHUNK, CHUNK), F32),
                   jax.ShapeDtypeStruct((groups, CHUNK), F32), jax.ShapeDtypeStruct((HALO, w), F32)],
        compiler_params=_cp(("arbitrary",), VMEM_MM))(proj, proj, proj, d_ab, d_ab, w_s, b_st, cw)


def _flat(x):
    return x.reshape(-1, x.shape[-1])


def _rope(t, cosf, sins):
    t2 = _flat(t)
    return (t2 * _flat(cosf) + pltpu.roll(t2, HEAD // 2, 1) * _flat(sins)).reshape(t.shape)


def _rope_bwd(dt, cosf, sins):
    d2 = _flat(dt)
    return (d2 * _flat(cosf) + pltpu.roll(d2 * _flat(sins), HEAD // 2, 1)).reshape(dt.shape)


ATT_UNITS = ATT_TILE // CHUNK


def _attn_units(phases):
    for b, d in enumerate(DILATIONS):
        blocks = ATT_TILE // (CHUNK * d)
        for visit in phases:
            for r in range(d):
                if blocks <= ATT_UNROLL:
                    for j in range(blocks):
                        visit(b, d, r, j, r * blocks + j)
                else:
                    def step(jj, carry, b=b, d=d, r=r, visit=visit, blocks=blocks):
                        for u in range(ATT_UNROLL):
                            j = jj * ATT_UNROLL + u
                            visit(b, d, r, j, r * blocks + j)
                        return carry
                    lax.fori_loop(0, blocks // ATT_UNROLL, step, 0)


class _Unit:
    def __init__(self, d, r, j):
        self.segs = [r + d * k for k in range(SEGS // d)]
        self.w = CHUNK * d // SEGS
        q0 = j * self.w
        self.q0 = q0 if isinstance(q0, int) else pl.multiple_of(q0, HALO)
        k0 = CHUNK + (j - 1) * self.w
        self.k0 = k0 if isinstance(k0, int) else pl.multiple_of(k0, HALO)

    def queries(self, ref):
        return _chunks(ref, self.segs, self.q0, self.w)

    def keys(self, ref):
        return _chunks(ref, self.segs, self.k0, 2 * self.w)

    def put_queries(self, ref, val, add=False):
        _put_chunks(ref, self.segs, self.q0, self.w, val, add)

    def put_keys(self, ref, val, add=False):
        _put_chunks(ref, self.segs, self.k0, 2 * self.w, val, add)


def _chunks(ref, segs, start, size):
    parts = [ref[s, pl.ds(start, size), :] for s in segs]
    return parts[0] if len(parts) == 1 else jnp.concatenate(parts, axis=0)


def _put_chunks(ref, segs, start, size, val, add):
    for k, s in enumerate(segs):
        piece = val[k * size:(k + 1) * size]
        if add:
            ref[s, pl.ds(start, size), :] += piece
        else:
            ref[s, pl.ds(start, size), :] = piece


def _band_bias():
    qi = lax.broadcasted_iota(jnp.int32, (CHUNK, 2 * CHUNK), 0)
    ki = lax.broadcasted_iota(jnp.int32, (CHUNK, 2 * CHUNK), 1)
    tables = []
    for d in DILATIONS:
        nseg, w = SEGS // d, CHUNK * d // SEGS
        pos_q = nseg * (qi % w) + qi // w
        pos_k = nseg * (ki % (2 * w) - w) + ki // (2 * w)
        band = (pos_q >= pos_k) & (pos_q - pos_k <= CHUNK)
        tables += [jnp.where(band, 0.0, -jnp.inf), jnp.where(band & (pos_k >= 0), 0.0, -jnp.inf)]
    return jnp.stack(tables).astype(F32)


def _bias_spec():
    return pl.BlockSpec((2 * len(DILATIONS), CHUNK, 2 * CHUNK), lambda h, n: (0, 0, 0))


def _unit_bias(bias, b, n, j):
    if isinstance(j, int) and j != 0:
        return bias[2 * b]
    return bias[2 * b + jnp.where(jnp.logical_and(n == 0, j == 0), 1, 0)]


def _attn_in_specs(heads):
    blk = (SEGS, CHUNK, HEAD)
    prev = lambda n: jnp.maximum(n - 1, 0)
    return [
        pl.BlockSpec(blk, lambda h, n: (0, n, h)),
        pl.BlockSpec(blk, lambda h, n: (0, n, heads + h)),
        pl.BlockSpec(blk, lambda h, n: (0, prev(n), heads + h)),
        pl.BlockSpec(blk, lambda h, n: (0, n, 2 * heads + h)),
        pl.BlockSpec(blk, lambda h, n: (0, prev(n), 2 * heads + h)),
        pl.BlockSpec(blk, lambda h, n: (0, n, 0)),
        pl.BlockSpec(blk, lambda h, n: (0, n, 0)),
        pl.BlockSpec(blk, lambda h, n: (0, prev(n), 0)),
        pl.BlockSpec(blk, lambda h, n: (0, prev(n), 0)),
    ]


def _attn_load(q_ref, kc_ref, kp_ref, vc_ref, vp_ref, cc_ref, sc_ref, cp_ref, sp_ref, qr, kcat, vcat):
    qr[...] = _rope(q_ref[...].astype(F32), cc_ref[...], sc_ref[...])
    kcat[:, pl.ds(0, CHUNK), :] = _rope(kp_ref[...].astype(F32), cp_ref[...], sp_ref[...])
    kcat[:, pl.ds(CHUNK, CHUNK), :] = _rope(kc_ref[...].astype(F32), cc_ref[...], sc_ref[...])
    vcat[:, pl.ds(0, CHUNK), :] = vp_ref[...].astype(F32)
    vcat[:, pl.ds(CHUNK, CHUNK), :] = vc_ref[...].astype(F32)


def _attn_fwd(name, qkv, cosf, sins, bias):
    t = qkv.shape[0]
    heads = qkv.shape[1] // (3 * HEAD)
    scale = HEAD ** -0.5
    nbr = len(DILATIONS)

    def body(q_ref, kc_ref, kp_ref, vc_ref, vp_ref, cc_ref, sc_ref, cp_ref, sp_ref, bias, o_ref, lse_ref,
             qr, kcat, vcat, obr, lbr, pn):
        n = pl.program_id(1)
        _attn_load(q_ref, kc_ref, kp_ref, vc_ref, vp_ref, cc_ref, sc_ref, cp_ref, sp_ref, qr, kcat, vcat)

        def probs(b, d, r, j, u):
            unit = _Unit(d, r, j)
            s = lax.dot_general(unit.queries(qr).astype(BF16), unit.keys(kcat).astype(BF16), NT_DIMS,
                                preferred_element_type=F32) * scale + _unit_bias(bias, b, n, j)
            mx = jnp.max(s, axis=-1, keepdims=True)
            p = jnp.exp(s - mx)
            den = jnp.sum(p, axis=-1, keepdims=True)
            pn[u] = (p * (1.0 / den)).astype(BF16)
            unit.put_queries(lbr.at[b], jnp.broadcast_to(mx + jnp.log(den), (CHUNK, HEAD)))

        def values(b, d, r, j, u):
            unit = _Unit(d, r, j)
            unit.put_queries(obr.at[b], jnp.dot(pn[u], unit.keys(vcat).astype(BF16), preferred_element_type=F32))

        _attn_units([probs, values])
        ls = [lbr[b] for b in range(nbr)]
        top = functools.reduce(jnp.maximum, ls)
        ws = [jnp.exp(l - top) for l in ls]
        tot = functools.reduce(jnp.add, ws)
        inv = 1.0 / tot
        o = (ws[0] * inv) * obr[0]
        for b in range(1, nbr):
            o = o + (ws[b] * inv) * obr[b]
        o_ref[...] = o.astype(BF16)
        lse_ref[...] = top + jnp.log(tot)

    blk = (SEGS, CHUNK, HEAD)
    keys = pltpu.VMEM((SEGS, 2 * CHUNK, HEAD), F32)
    tile = pl.BlockSpec(blk, lambda h, n: (0, n, h))
    seg = t // SEGS
    qkv3, cos3, sin3 = _seg_view(qkv), _seg_view(cosf), _seg_view(sins)
    o, lse = pl.pallas_call(
        body, name=name, grid=(heads, t // ATT_TILE), in_specs=_attn_in_specs(heads) + [_bias_spec()],
        out_specs=[tile, tile],
        out_shape=[jax.ShapeDtypeStruct((SEGS, seg, heads * HEAD), BF16),
                   jax.ShapeDtypeStruct((SEGS, seg, heads * HEAD), F32)],
        scratch_shapes=[pltpu.VMEM(blk, F32), keys, keys,
                        pltpu.VMEM((nbr,) + blk, F32), pltpu.VMEM((nbr,) + blk, F32),
                        pltpu.VMEM((ATT_UNITS, CHUNK, 2 * CHUNK), BF16)],
        compiler_params=_cp(("parallel", "parallel"), VMEM_MM),
    )(qkv3, qkv3, qkv3, qkv3, qkv3, cos3, sin3, cos3, sin3, bias)
    return o.reshape(t, heads * HEAD), lse.reshape(t, heads * HEAD)


def _attn_bwd(name, qkv, cosf, sins, bias, d_o, o, lse):
    t = qkv.shape[0]
    heads = qkv.shape[1] // (3 * HEAD)
    scale = HEAD ** -0.5

    def body(q_ref, kc_ref, kp_ref, vc_ref, vp_ref, cc_ref, sc_ref, cp_ref, sp_ref, do_ref, o_ref, lse_ref, bias,
             dq_ref, dko_ref, dkp_ref, dvo_ref, dvp_ref, qr, kcat, vcat, dq_acc, dk_acc, dv_acc, delta, ps, dss):
        n = pl.program_id(1)
        _attn_load(q_ref, kc_ref, kp_ref, vc_ref, vp_ref, cc_ref, sc_ref, cp_ref, sp_ref, qr, kcat, vcat)
        dq_acc[...] = jnp.zeros_like(dq_acc)
        dk_acc[...] = jnp.zeros_like(dk_acc)
        dv_acc[...] = jnp.zeros_like(dv_acc)
        delta[...] = jnp.broadcast_to(
            jnp.sum(do_ref[...] * o_ref[...].astype(F32), axis=-1, keepdims=True), delta.shape)

        def probs(b, d, r, j, u):
            unit = _Unit(d, r, j)
            s = lax.dot_general(unit.queries(qr).astype(BF16), unit.keys(kcat).astype(BF16), NT_DIMS,
                                preferred_element_type=F32) * scale + _unit_bias(bias, b, n, j)
            ps[u] = jnp.exp(s - unit.queries(lse_ref)[:, 0:1]).astype(BF16)

        def score_grads(b, d, r, j, u):
            unit = _Unit(d, r, j)
            dp = lax.dot_general(unit.queries(do_ref).astype(BF16), unit.keys(vcat).astype(BF16), NT_DIMS,
                                 preferred_element_type=F32)
            dss[u] = (ps[u].astype(F32) * (dp - unit.queries(delta)[:, 0:1]) * scale).astype(BF16)

        def input_grads(b, d, r, j, u):
            unit = _Unit(d, r, j)
            ds = dss[u]
            unit.put_queries(dq_acc, jnp.dot(ds, unit.keys(kcat).astype(BF16), preferred_element_type=F32), add=True)
            unit.put_keys(dk_acc, lax.dot_general(ds, unit.queries(qr).astype(BF16), TN_DIMS,
                                                  preferred_element_type=F32), add=True)
            unit.put_keys(dv_acc, lax.dot_general(ps[u], unit.queries(do_ref).astype(BF16), TN_DIMS,
                                                  preferred_element_type=F32), add=True)

        _attn_units([probs, score_grads, input_grads])
        dq_ref[...] = _rope_bwd(dq_acc[...], cc_ref[...], sc_ref[...]).astype(BF16)
        dkp_ref[...] = _rope_bwd(dk_acc[:, pl.ds(0, CHUNK), :], cp_ref[...], sp_ref[...]).astype(BF16)
        dko_ref[...] = _rope_bwd(dk_acc[:, pl.ds(CHUNK, CHUNK), :], cc_ref[...], sc_ref[...]).astype(BF16)
        dvp_ref[...] = dv_acc[:, pl.ds(0, CHUNK), :].astype(BF16)
        dvo_ref[...] = dv_acc[:, pl.ds(CHUNK, CHUNK), :].astype(BF16)

    blk = (SEGS, CHUNK, HEAD)
    tile = pl.BlockSpec(blk, lambda h, n: (0, n, h))
    big = pltpu.VMEM((SEGS, 2 * CHUNK, HEAD), F32)
    seg = t // SEGS
    qkv3, cos3, sin3 = _seg_view(qkv), _seg_view(cosf), _seg_view(sins)
    return pl.pallas_call(
        body, name=name, grid=(heads, t // ATT_TILE),
        in_specs=_attn_in_specs(heads) + [tile, tile, tile, _bias_spec()],
        out_specs=[tile] * 5,
        out_shape=[jax.ShapeDtypeStruct((SEGS, seg, heads * HEAD), BF16)] * 5,
        scratch_shapes=[pltpu.VMEM(blk, F32), big, big, pltpu.VMEM(blk, F32), big, big, pltpu.VMEM(blk, F32),
                        pltpu.VMEM((ATT_UNITS, CHUNK, 2 * CHUNK), BF16), pltpu.VMEM((ATT_UNITS, CHUNK, 2 * CHUNK), BF16)],
        compiler_params=_cp(("parallel", "parallel"), 60 << 20),
    )(qkv3, qkv3, qkv3, qkv3, qkv3, cos3, sin3, cos3, sin3, _seg_view(d_o), _seg_view(o), _seg_view(lse), bias)


def _attn_merge(name, dq, dk_own, dk_prev, dv_own, dv_prev):
    _, seg, hd = dq.shape
    nt = seg // CHUNK
    tw = _tile(hd, 512)

    def body(dq_ref, dko_ref, dkn_ref, dvo_ref, dvn_ref, o_ref):
        last = pl.program_id(0) == nt - 1
        part = pl.program_id(1)

        @pl.when(part == 0)
        def _():
            o_ref[...] = dq_ref[...]

        @pl.when(part == 1)
        def _():
            o_ref[...] = (dko_ref[...].astype(F32) + jnp.where(last, 0.0, dkn_ref[...].astype(F32))).astype(BF16)

        @pl.when(part == 2)
        def _():
            o_ref[...] = (dvo_ref[...].astype(F32) + jnp.where(last, 0.0, dvn_ref[...].astype(F32))).astype(BF16)

    blk = (SEGS, CHUNK, tw)

    def own(part):
        return pl.BlockSpec(blk, lambda n, p, c: (0, jnp.where(p == part, n, 0), jnp.where(p == part, c, 0)))

    def nxt(part):
        return pl.BlockSpec(blk, lambda n, p, c: (0, jnp.where(p == part, jnp.minimum(n + 1, nt - 1), 0),
                                                  jnp.where(p == part, c, 0)))

    per = hd // tw
    return pl.pallas_call(
        body, name=name, grid=(nt, 3, per), in_specs=[own(0), own(1), nxt(1), own(2), nxt(2)],
        out_specs=pl.BlockSpec(blk, lambda n, p, c: (0, n, p * per + c)),
        out_shape=jax.ShapeDtypeStruct((SEGS, seg, 3 * hd), BF16),
        compiler_params=_cp(("parallel", "parallel", "parallel"), VMEM_MM),
    )(dq, dk_own, dk_prev, dv_own, dv_prev).reshape(SEGS * seg, 3 * hd)


def _sum_parts(name, parts):
    nparts, rows, cols = parts.shape
    tr = _tile(rows, 256)

    def body(p_ref, o_ref):
        s = p_ref[0]
        for k in range(1, nparts):
            s = s + p_ref[k]
        o_ref[...] = s

    return pl.pallas_call(
        body, name=name, grid=(rows // tr,),
        in_specs=[pl.BlockSpec((nparts, tr, cols), lambda i: (0, i, 0))],
        out_specs=pl.BlockSpec((tr, cols), lambda i: (i, 0)),
        out_shape=jax.ShapeDtypeStruct((rows, cols), F32),
        compiler_params=_cp(("parallel",)))(parts)


def _rows128(a, pad_to=8):
    flat = a.reshape(-1)
    rows = -(-flat.shape[0] // 128)
    rows = -(-rows // pad_to) * pad_to
    flat = jnp.pad(flat, (0, rows * 128 - flat.shape[0]))
    return flat.reshape(rows, 128)


def _pack(arrays):
    return jnp.concatenate([_rows128(a) for a in arrays], axis=0)


def _unpack(packed, like):
    out, at = [], 0
    for a in like:
        size = 1
        for s in a.shape:
            size *= s
        rows = -(-(-(-size // 128)) // 8) * 8
        out.append(packed[at:at + rows].reshape(-1)[:size].reshape(a.shape))
        at += rows
    return out


def kernel(x, norm_mix_pre, norm_mix_post, norm_mlp_pre, norm_mlp_post, w_in_ab, w_spatial, b_spatial, conv_w, w_out_ab, w_qkv, w_o, w_up, w_down, loss_target, m_norm_mix_pre, m_norm_mix_post, m_norm_mlp_pre, m_norm_mlp_post, m_w_in_ab, m_w_spatial, m_b_spatial, m_conv_w, m_w_out_ab, m_w_qkv, m_w_o, m_w_up, m_w_down, v_norm_mix_pre, v_norm_mix_post, v_norm_mlp_pre, v_norm_mlp_post, v_w_in_ab, v_w_spatial, v_b_spatial, v_conv_w, v_w_out_ab, v_w_qkv, v_w_o, v_w_up, v_w_down):
    depth = norm_mix_pre.shape[0]
    seq, dm = x.shape[1], x.shape[2]
    h0 = x.reshape(seq, dm)
    target = loss_target.reshape(seq, dm)
    ax, ay, ac = lax.axis_index("x"), lax.axis_index("y"), lax.axis_index("c")
    my_block = 4 * ax + 2 * ay + ac
    block = jnp.reshape(my_block, (1,)).astype(jnp.int32)

    half = HEAD // 2
    inv_freq = ROPE_THETA ** (-jnp.arange(half, dtype=F32) * 2.0 / HEAD)
    ang = jnp.arange(seq, dtype=jnp.int32).astype(F32)[:, None] * inv_freq[None, :]
    ang = ang.reshape(seq // SEGS, SEGS, half).transpose(1, 0, 2).reshape(seq, half)
    cosf = jnp.concatenate([jnp.cos(ang), jnp.cos(ang)], axis=-1)
    sins = jnp.concatenate([-jnp.sin(ang), jnp.sin(ang)], axis=-1)
    band_bias = _band_bias()

    big = {"w_in_ab": w_in_ab, "w_out_ab": w_out_ab, "w_qkv": w_qkv, "w_o": w_o, "w_up": w_up, "w_down": w_down}
    use_order = []
    for l in range(depth):
        use_order += [("w_in_ab", l // 2), ("w_out_ab", l // 2)] if l % 2 == 0 else [("w_qkv", l // 2), ("w_o", l // 2)]
        use_order += [("w_up", l), ("w_down", l)]
    n_even = w_in_ab.shape[0]
    cw_rows = jnp.pad(conv_w.reshape(n_even * CONV_TAPS, conv_w.shape[2]), ((0, HALO - (n_even * CONV_TAPS) % HALO), (0, 0)))
    cw_gathered = _all_gather("ag_conv", [cw_rows])[0]
    first = [k for k in use_order if k in (("w_in_ab", 0), ("w_out_ab", 0), ("w_up", 0), ("w_down", 0))]
    rest = [k for k in use_order if k not in first]
    lands_a, sems_a = _ag_start("ag_start_first", [_cast_fill(f"cast_{nm}_{l}", big[nm], l, block) for nm, l in first],
                                cw_gathered)
    lands_b, sems_b = _ag_start("ag_start_rest", [_cast_fill(f"cast_{nm}_{l}", big[nm], l, block) for nm, l in rest],
                                lands_a[0])
    lands = dict(zip(first + rest, list(lands_a) + list(lands_b)))
    ag_sems = dict(zip(first + rest, list(sems_a) + list(sems_b)))
    passed_on, wg = [], {}

    def weight(key, after):
        if key not in wg:
            upto = min(use_order.index(key) + 2, len(use_order) - 1)
            for k in use_order[len(passed_on):upto + 1]:
                lands[k] = _ag_mid(f"ag_mid_{k[0]}_{k[1]}", lands[k], ag_sems[k], after)
                passed_on.append(k)
            wg[key] = _ag_wait(f"ag_wait_{key[0]}_{key[1]}", lands[key], ag_sems[key], after)
        return wg[key]

    cw_all = cw_gathered[:, :n_even * CONV_TAPS].reshape(NDEV, n_even, CONV_TAPS, -1)
    cw_all = jnp.transpose(cw_all, (1, 2, 0, 3)).reshape(n_even, CONV_TAPS, -1)
    cw_full = [jnp.pad(cw_all[e], ((0, HALO - CONV_TAPS), (0, 0))) for e in range(n_even)]

    def rows_nat(blk):
        return blk.reshape(blk.shape[0] * blk.shape[1], blk.shape[2])

    saved = []
    hn = _norm_fwd("norm_first", h0, g_pre=norm_mix_pre[0][None])[0]
    h = h0
    for l in range(depth):
        s = {"h_in": h, "hn1": hn}
        if l % 2 == 0:
            e = l // 2
            proj = _mm_nn_blk(f"fwd_in_{l}", hn, weight(("w_in_ab", e), hn))
            ab = _gate_fwd(f"gate_fwd_{l}", proj, w_spatial[e], b_spatial[e].T, cw_full[e])
            mix = _mm_nn(f"fwd_out_{l}", ab, rows_nat(weight(("w_out_ab", e), ab)))
            s.update(proj=proj, ab=ab)
        else:
            o_ = l // 2
            qkv = _mm_nn_blk(f"fwd_qkv_{l}", hn, weight(("w_qkv", o_), hn))
            att, lse = _attn_fwd(f"attn_fwd_{l}", qkv, cosf, sins, band_bias)
            mix = _mm_nn(f"fwd_o_{l}", att, rows_nat(weight(("w_o", o_), att)))
            s.update(qkv=qkv, att=att, lse=lse)
        h1, hn2 = _norm_fwd(f"norm_mid_{l}", h, mix, norm_mix_post[l][None], norm_mlp_pre[l][None], seg_z=l % 2 == 1)
        act = _mm_nn_blk(f"fwd_up_{l}", hn2, weight(("w_up", l), hn2), relu2=True)
        f = _mm_nn(f"fwd_down_{l}", act, rows_nat(weight(("w_down", l), act)))
        s.update(mix=mix, h1=h1, hn2=hn2, act=act, f=f)
        if l + 1 < depth:
            h, hn = _norm_fwd(f"norm_end_{l}", h1, f, norm_mlp_post[l][None], norm_mix_pre[l + 1][None],
                              seg_y=(l + 1) % 2 == 1)
        else:
            h = _norm_fwd(f"norm_end_{l}", h1, f, norm_mlp_post[l][None])[0]
        saved.append(s)

    d_h, loss_row = _loss_grad("loss", h, target)
    rs = {}

    def scatter(key, g):
        rs[key] = _rs_start(f"rs_start_{key[0]}_{key[1]}", g.reshape(NDEV, -1, g.shape[-1]))

    dg ={nm: [None] * depth for nm in ("norm_mix_pre", "norm_mix_post", "norm_mlp_pre", "norm_mlp_post")}
    d_ws, d_bs, d_cw = [None] * n_even, [None] * n_even, [None] * n_even
    d_hn_next = None
    for l in reversed(range(depth)):
        s = saved[l]
        if l == depth - 1:
            d_f, dg["norm_mlp_post"][l] = _norm_bwd(f"nb_end_{l}", d_h, post=(s["f"], norm_mlp_post[l][None]))
        else:
            d_h, dg["norm_mix_pre"][l + 1], d_f, dg["norm_mlp_post"][l] = _norm_bwd(
                f"nb_end_{l}", d_h, pre=(d_hn_next, saved[l + 1]["h_in"], norm_mix_pre[l + 1][None]),
                post=(s["f"], norm_mlp_post[l][None]), seg_dy=(l + 1) % 2 == 1)
        wd = rows_nat(wg[("w_down", l)])
        d_up = _mm_nt_rows(f"bwd_down_{l}", d_f, wd, act=s["act"])
        scatter(("w_down", l), _mm_tn(f"gw_down_{l}", s["act"], d_f))
        scatter(("w_up", l), _mm_tn(f"gw_up_{l}", s["hn2"], d_up, nb=w_up.shape[2]))
        d_hn2 = _mm_nt_blk(f"bwd_up_{l}", d_up, wg[("w_up", l)], after=[rs[("w_down", l)][0], rs[("w_up", l)][0]])
        d_h, dg["norm_mlp_pre"][l], d_mix, dg["norm_mix_post"][l] = _norm_bwd(
            f"nb_mid_{l}", d_h, pre=(d_hn2, s["h1"], norm_mlp_pre[l][None]),
            post=(s["mix"], norm_mix_post[l][None]), seg_z=l % 2 == 1)
        if l % 2 == 0:
            e = l // 2
            wo = rows_nat(wg[("w_out_ab", e)])
            d_ab = _mm_nt_rows(f"bwd_out_{l}", d_mix, wo)
            scatter(("w_out_ab", e), _mm_tn(f"gw_out_{l}", s["ab"], d_mix))
            d_proj, d_ws[e], d_bs[e], d_cw[e] = _gate_bwd(
                f"gate_bwd_{l}", s["proj"], d_ab, w_spatial[e], b_spatial[e].T, cw_full[e])
            scatter(("w_in_ab", e), _mm_tn(f"gw_in_{l}", s["hn1"], d_proj, nb=w_in_ab.shape[2]))
            d_hn_next = _mm_nt_blk(f"bwd_in_{l}", d_proj, wg[("w_in_ab", e)],
                                   after=[rs[("w_out_ab", e)][0], rs[("w_in_ab", e)][0]])
        else:
            o_ = l // 2
            wo = rows_nat(wg[("w_o", o_)])
            d_att = _mm_nt_rows(f"bwd_o_{l}", d_mix, wo, out_dtype=F32)
            scatter(("w_o", o_), _mm_tn(f"gw_o_{l}", s["att"], d_mix))
            parts = _attn_bwd(f"attn_bwd_{l}", s["qkv"], cosf, sins, band_bias, d_att, s["att"], s["lse"])
            d_qkv = _attn_merge(f"attn_merge_{l}", *parts)
            scatter(("w_qkv", o_), _mm_tn(f"gw_qkv_{l}", s["hn1"], d_qkv, nb=w_qkv.shape[2]))
            d_hn_next = _mm_nt_blk(f"bwd_qkv_{l}", d_qkv, wg[("w_qkv", o_)],
                                   after=[rs[("w_o", o_)][0], rs[("w_qkv", o_)][0]])
    grad_x, dg["norm_mix_pre"][0] = _norm_bwd("nb_first", d_h, pre=(d_hn_next, h0, norm_mix_pre[0][None]))

    small_g = ([jnp.concatenate(dg[nm], axis=0) for nm in dg]
               + [jnp.stack(d_ws), jnp.stack(d_bs), jnp.stack([c[:CONV_TAPS] for c in d_cw]), loss_row])
    small_land, small_send, small_recv = _ag_direct_start("ag_small_start", _fill_slot("fill_small", _pack(small_g), block))

    moments = {"w_in_ab": (m_w_in_ab, v_w_in_ab), "w_out_ab": (m_w_out_ab, v_w_out_ab), "w_qkv": (m_w_qkv, v_w_qkv),
               "w_o": (m_w_o, v_w_o), "w_up": (m_w_up, v_w_up), "w_down": (m_w_down, v_w_down)}
    out_big = {}
    behind = small_land
    for nm in ("w_o", "w_qkv", "w_down", "w_up", "w_out_ab", "w_in_ab"):
        own, landed = [], []
        for l in range(big[nm].shape[0]):
            g, land = _wait_all(f"rs_wait_{nm}_{l}", *rs[(nm, l)], behind)
            own.append(g)
            landed.append(land)
        out_big[nm] = _adamw_layers(f"adamw_{nm}", own, landed, block, big[nm], moments[nm][0], moments[nm][1])
        behind = out_big[nm][0]

    summed = _sum_parts("sum_small", _ag_direct_wait("ag_small_wait", small_land, small_send, small_recv, behind))
    g_nmp, g_nmo, g_nlp, g_nlo, g_ws, g_bs, g_cw_all, loss_sum = _unpack(summed, small_g)
    loss = loss_sum[0, 0]
    cwb = conv_w.shape[2]
    g_cw = lax.dynamic_slice_in_dim(g_cw_all, my_block * cwb, cwb, axis=2)
    small_w = [norm_mix_pre, norm_mix_post, norm_mlp_pre, norm_mlp_post, w_spatial, b_spatial, conv_w]
    small_m = [m_norm_mix_pre, m_norm_mix_post, m_norm_mlp_pre, m_norm_mlp_post, m_w_spatial, m_b_spatial, m_conv_w]
    small_v = [v_norm_mix_pre, v_norm_mix_post, v_norm_mlp_pre, v_norm_mlp_post, v_w_spatial, v_b_spatial, v_conv_w]
    small_grad = [g_nmp, g_nmo, g_nlp, g_nlo, g_ws, g_bs, g_cw]
    upd = _adamw("adamw_small", _pack(small_grad)[None], _pack(small_w), _pack(small_m), _pack(small_v))
    sg, sd, sm, sv = [_unpack(u, small_w) for u in upd]

    def outs(i_small, i_big):
        return (i_small[0], i_small[1], i_small[2], i_small[3], i_big["w_in_ab"], i_small[4], i_small[5], i_small[6],
                i_big["w_out_ab"], i_big["w_qkv"], i_big["w_o"], i_big["w_up"], i_big["w_down"])

    pick = lambda i: {nm: out_big[nm][i] for nm in big}
    return (loss, grad_x.reshape(x.shape), *outs(sg, pick(0)), *outs(sd, pick(1)), *outs(sm, pick(2)),
            *outs(sv, pick(3)))
```

```python
import functools

import jax
import jax.numpy as jnp
from jax import lax
from jax.experimental import pallas as pl
from jax.experimental.pallas import tpu as pltpu

F32 = jnp.float32
BF16 = jnp.bfloat16
MESH = pl.DeviceIdType.MESH
ANY = pl.BlockSpec(memory_space=pl.ANY)
HBM = pl.BlockSpec(memory_space=pltpu.HBM)
SEM = pl.BlockSpec(memory_space=pltpu.SEMAPHORE)
EFFECT = pltpu.SideEffectType.DATAFLOW_SIDE_EFFECTING

NDEV = 8
NCHIP = 4
RMS_EPS = 1e-6
LN_EPS = 1e-5
CHUNK = 128
HEAD = 128
ATT_TILE = 2048
ATT_UNROLL = 4
DILATIONS = (1, 4, 16)
SEGS = 16
ROPE_THETA = 10000.0
CONV_TAPS = 3
HALO = 8
HALO_BF16 = 16
GELU_C = 0.7978845608028654
GELU_A = 0.044715
ADAM_LR, ADAM_B1, ADAM_B2, ADAM_EPS, ADAM_WD, ADAM_STEP = 0.001, 0.9, 0.999, 1e-08, 0.01, 10
VMEM_MM = 52 << 20
VMEM_EW = 40 << 20


def _cp(sem=None, vmem=VMEM_EW):
    if sem is None:
        return pltpu.CompilerParams(vmem_limit_bytes=vmem)
    return pltpu.CompilerParams(dimension_semantics=sem, vmem_limit_bytes=vmem)


def _tile(n, want):
    return want if n % want == 0 else n


def _all_gather(name, shards, after=()):
    n = len(shards)
    after = list(after)

    def body(*refs):
        ins, outs = refs[:n], refs[n + len(after):2 * n + len(after)]
        send_sems, recv_sems, local_sems = refs[2 * n + len(after):]
        x, y, c = lax.axis_index("x"), lax.axis_index("y"), lax.axis_index("c")
        me, sibling = (x, y, c), (x, y, 1 - c)
        chips = [(1 - x, y), (x, 1 - y), (1 - x, 1 - y)]

        def slot(p):
            return 4 * p[0] + 2 * p[1] + p[2]

        def copy(i, k, block, to, src=None):
            dst = outs[i].at[slot(block)]
            return pltpu.make_async_remote_copy(
                src_ref=dst if src is None else src, dst_ref=dst,
                send_sem=send_sems.at[i, k], recv_sem=recv_sems.at[i, k],
                device_id=to, device_id_type=MESH)

        mine = [pltpu.make_async_copy(ins[i], outs[i].at[slot(me)], local_sems.at[i]) for i in range(n)]
        for cp in mine:
            cp.start()
        first = []
        for i in range(n):
            first.append(copy(i, 0, me, sibling, src=ins[i]))
            for j, chip in enumerate(chips):
                first.append(copy(i, 1 + j, me, (*chip, c), src=ins[i]))
        for cp in first:
            cp.start()
        passed = []
        for j, chip in enumerate(chips):
            for i in range(n):
                copy(i, 1 + j, (*chip, c), me).wait_recv()
                fwd = copy(i, 4 + j, (*chip, c), sibling)
                fwd.start()
                passed.append(fwd)
        for i in range(n):
            copy(i, 0, sibling, me).wait_recv()
            for j, chip in enumerate(chips):
                copy(i, 4 + j, (*chip, 1 - c), me).wait_recv()
        for cp in first + passed:
            cp.wait_send()
        for cp in mine:
            cp.wait()

    return pl.pallas_call(
        body, name=name,
        out_shape=[jax.ShapeDtypeStruct((NDEV,) + s.shape, s.dtype) for s in shards],
        in_specs=[ANY] * (n + len(after)), out_specs=[ANY] * n,
        scratch_shapes=[pltpu.SemaphoreType.DMA((n, 7)), pltpu.SemaphoreType.DMA((n, 7)),
                        pltpu.SemaphoreType.DMA((n,))],
    )(*shards, *after)


def _peer(x, y, c, r):
    return (1 - x if r & 4 else x, 1 - y if r & 2 else y, 1 - c if r & 1 else c)


def _slot(p):
    return 4 * p[0] + 2 * p[1] + p[2]


def _cast_fill(name, w, layer, block):
    _, rows, cols = w.shape
    tr = _tile(rows, 256)

    def body(blk_ref, w_ref, o_ref):
        o_ref[...] = w_ref[...].astype(BF16)

    return pl.pallas_call(
        body, name=name,
        grid_spec=pltpu.PrefetchScalarGridSpec(
            num_scalar_prefetch=1, grid=(rows // tr,),
            in_specs=[pl.BlockSpec((None, tr, cols), lambda i, blk: (layer, i, 0))],
            out_specs=pl.BlockSpec((None, tr, cols), lambda i, blk: (blk[0], i, 0))),
        out_shape=jax.ShapeDtypeStruct((NDEV, rows, cols), BF16),
        compiler_params=_cp(("parallel",)))(block, w)


OTHER_CHIPS = (2, 4, 6)
AG_SEMS = 6


def _ag_start(name, lands, after):
    n = len(lands)

    def body(*refs):
        ins, sems = refs[:n], refs[n + 1:n + 1 + AG_SEMS * n]
        x, y, c = lax.axis_index("x"), lax.axis_index("y"), lax.axis_index("c")
        mine = _slot((x, y, c))
        for i in range(n):
            send_a, *recv_a, _, recv_b = sems[AG_SEMS * i:AG_SEMS * (i + 1)]
            block = ins[i].at[mine]
            pltpu.make_async_remote_copy(src_ref=block, dst_ref=block, send_sem=send_a, recv_sem=recv_b,
                                         device_id=_peer(x, y, c, 1), device_id_type=MESH).start()
            for k, r in enumerate(OTHER_CHIPS):
                pltpu.make_async_remote_copy(src_ref=block, dst_ref=block, send_sem=send_a, recv_sem=recv_a[k],
                                             device_id=_peer(x, y, c, r), device_id_type=MESH).start()

    outs = pl.pallas_call(
        body, name=name,
        out_shape=[pltpu.SemaphoreType.DMA(())] * (AG_SEMS * n) + [pltpu.HBM(a.shape, a.dtype) for a in lands],
        in_specs=[HBM] * n + [ANY], out_specs=[SEM] * (AG_SEMS * n) + [HBM] * n,
        input_output_aliases={i: AG_SEMS * n + i for i in range(n)},
        compiler_params=pltpu.CompilerParams(has_side_effects=EFFECT),
    )(*[pltpu.with_memory_space_constraint(a, pltpu.HBM) for a in lands], after)
    return outs[AG_SEMS * n:], [tuple(outs[AG_SEMS * i:AG_SEMS * (i + 1)]) for i in range(n)]


def _ag_mid(name, land, sems, after):
    _, *recv_a, send_b, recv_b = sems

    def body(land_ref, ra0, ra1, ra2, send_b_ref, recv_b_ref, after_ref, land_out):
        x, y, c = lax.axis_index("x"), lax.axis_index("y"), lax.axis_index("c")
        sibling = _peer(x, y, c, 1)
        for arrival, r in zip((ra0, ra1, ra2), OTHER_CHIPS):
            block = land_ref.at[_slot(_peer(x, y, c, r))]
            pltpu.make_async_remote_copy(src_ref=block, dst_ref=block, send_sem=send_b_ref, recv_sem=arrival,
                                         device_id=sibling, device_id_type=MESH).wait_recv()
            pltpu.make_async_remote_copy(src_ref=block, dst_ref=block, send_sem=send_b_ref, recv_sem=recv_b_ref,
                                         device_id=sibling, device_id_type=MESH).start()

    return pl.pallas_call(
        body, name=name, out_shape=pltpu.HBM(land.shape, land.dtype),
        in_specs=[HBM] + [SEM] * 5 + [ANY], out_specs=HBM, input_output_aliases={0: 0},
        compiler_params=pltpu.CompilerParams(has_side_effects=EFFECT),
    )(land, *recv_a, send_b, recv_b, after)


def _ag_wait(name, land, sems, after):
    send_a, _, _, _, send_b, recv_b = sems

    def body(land_ref, send_a_ref, send_b_ref, recv_b_ref, after_ref, land_out):
        x, y, c = lax.axis_index("x"), lax.axis_index("y"), lax.axis_index("c")
        sibling = _peer(x, y, c, 1)
        four = land_ref.at[pl.ds(0, 1 + len(OTHER_CHIPS))]
        three = land_ref.at[pl.ds(0, len(OTHER_CHIPS))]
        first = pltpu.make_async_remote_copy(src_ref=four, dst_ref=four, send_sem=send_a_ref, recv_sem=recv_b_ref,
                                             device_id=sibling, device_id_type=MESH)
        passed = pltpu.make_async_remote_copy(src_ref=three, dst_ref=three, send_sem=send_b_ref, recv_sem=recv_b_ref,
                                              device_id=sibling, device_id_type=MESH)
        first.wait_send()
        passed.wait_send()
        first.wait_recv()

    return pl.pallas_call(
        body, name=name, out_shape=pltpu.HBM(land.shape, land.dtype),
        in_specs=[HBM, SEM, SEM, SEM, ANY], out_specs=HBM, input_output_aliases={0: 0},
        compiler_params=pltpu.CompilerParams(has_side_effects=EFFECT),
    )(land, send_a, send_b, recv_b, after)


def _fill_slot(name, rows, block):
    r, c = rows.shape

    def body(blk_ref, i_ref, o_ref):
        o_ref[...] = i_ref[...]

    return pl.pallas_call(
        body, name=name,
        grid_spec=pltpu.PrefetchScalarGridSpec(
            num_scalar_prefetch=1, grid=(1,),
            in_specs=[pl.BlockSpec((r, c), lambda i, blk: (0, 0))],
            out_specs=pl.BlockSpec((None, r, c), lambda i, blk: (blk[0], 0, 0))),
        out_shape=jax.ShapeDtypeStruct((NDEV, r, c), rows.dtype),
        compiler_params=_cp(("arbitrary",)))(block, rows)


def _ag_direct_start(name, land):
    def body(land_ref, send, recv, land_out):
        x, y, c = lax.axis_index("x"), lax.axis_index("y"), lax.axis_index("c")
        block = land_ref.at[_slot((x, y, c))]
        for r in range(1, NDEV):
            pltpu.make_async_remote_copy(src_ref=block, dst_ref=block, send_sem=send, recv_sem=recv,
                                         device_id=_peer(x, y, c, r), device_id_type=MESH).start()

    send, recv, land_thru = pl.pallas_call(
        body, name=name,
        out_shape=[pltpu.SemaphoreType.DMA(()), pltpu.SemaphoreType.DMA(()), pltpu.HBM(land.shape, land.dtype)],
        in_specs=[HBM], out_specs=[SEM, SEM, HBM], input_output_aliases={0: 2},
        compiler_params=pltpu.CompilerParams(has_side_effects=EFFECT),
    )(pltpu.with_memory_space_constraint(land, pltpu.HBM))
    return land_thru, send, recv


def _ag_direct_wait(name, land, send, recv, after):
    def body(land_ref, send_ref, recv_ref, after_ref, land_out):
        x, y, c = lax.axis_index("x"), lax.axis_index("y"), lax.axis_index("c")
        seven = land_ref.at[pl.ds(0, NDEV - 1)]
        copy = pltpu.make_async_remote_copy(src_ref=seven, dst_ref=seven, send_sem=send_ref, recv_sem=recv_ref,
                                            device_id=_peer(x, y, c, 1), device_id_type=MESH)
        copy.wait_send()
        copy.wait_recv()

    return pl.pallas_call(
        body, name=name, out_shape=pltpu.HBM(land.shape, land.dtype),
        in_specs=[HBM, SEM, SEM, ANY], out_specs=HBM, input_output_aliases={0: 0},
        compiler_params=pltpu.CompilerParams(has_side_effects=EFFECT),
    )(land, send, recv, after)


def _wait_all(name, src, land, send, recv, after):
    def body(src_ref, land_ref, send_ref, recv_ref, after_ref, src_out, land_out):
        x, y, c = lax.axis_index("x"), lax.axis_index("y"), lax.axis_index("c")
        seven = land_ref.at[pl.ds(0, NDEV - 1)]
        copy = pltpu.make_async_remote_copy(src_ref=seven, dst_ref=seven, send_sem=send_ref, recv_sem=recv_ref,
                                            device_id=_peer(x, y, c, 1), device_id_type=MESH)
        copy.wait_send()
        copy.wait_recv()

    return pl.pallas_call(
        body, name=name,
        out_shape=[pltpu.HBM(src.shape, src.dtype), pltpu.HBM(land.shape, land.dtype)],
        in_specs=[HBM, HBM, SEM, SEM, ANY], out_specs=[HBM, HBM],
        input_output_aliases={0: 0, 1: 1},
        compiler_params=pltpu.CompilerParams(has_side_effects=EFFECT),
    )(src, land, send, recv, after)


def _rs_start(name, grad):
    land = lax.empty((NDEV - 1,) + grad.shape[1:], grad.dtype)

    def body(g_ref, land_ref, send, recv, g_out, land_out):
        x, y, c = lax.axis_index("x"), lax.axis_index("y"), lax.axis_index("c")
        for r in range(1, NDEV):
            peer = _peer(x, y, c, r)
            pltpu.make_async_remote_copy(
                src_ref=g_ref.at[_slot(peer)], dst_ref=land_ref.at[r - 1], send_sem=send, recv_sem=recv,
                device_id=peer, device_id_type=MESH).start()

    send, recv, g_thru, land_thru = pl.pallas_call(
        body, name=name,
        out_shape=[pltpu.SemaphoreType.DMA(()), pltpu.SemaphoreType.DMA(()),
                   pltpu.HBM(grad.shape, grad.dtype), pltpu.HBM(land.shape, land.dtype)],
        in_specs=[HBM, HBM], out_specs=[SEM, SEM, HBM, HBM], input_output_aliases={0: 2, 1: 3},
        compiler_params=pltpu.CompilerParams(has_side_effects=EFFECT),
    )(pltpu.with_memory_space_constraint(grad, pltpu.HBM), pltpu.with_memory_space_constraint(land, pltpu.HBM))
    return g_thru, land_thru, send, recv


def _adam_math(w, g, m, v):
    m = ADAM_B1 * m + (1.0 - ADAM_B1) * g
    v = ADAM_B2 * v + (1.0 - ADAM_B2) * (g * g)
    m_hat = m / (1.0 - ADAM_B1 ** ADAM_STEP)
    v_hat = v / (1.0 - ADAM_B2 ** ADAM_STEP)
    delta = -ADAM_LR * (m_hat / (jnp.sqrt(v_hat) + ADAM_EPS) + ADAM_WD * w)
    return delta, m, v


def _adamw(name, parts, w, m, v):
    nparts, rows, cols = parts.shape
    tr = _tile(rows, 256)

    def body(p_ref, w_ref, m_ref, v_ref, g_out, d_out, m_out, v_out):
        g = p_ref[0].astype(F32)
        for k in range(1, nparts):
            g = g + p_ref[k].astype(F32)
        delta, mn, vn = _adam_math(w_ref[...], g, m_ref[...], v_ref[...])
        g_out[...] = g
        d_out[...] = delta
        m_out[...] = mn
        v_out[...] = vn

    row = pl.BlockSpec((tr, cols), lambda i: (i, 0))
    return pl.pallas_call(
        body, name=name, grid=(rows // tr,),
        in_specs=[pl.BlockSpec((nparts, tr, cols), lambda i: (0, i, 0)), row, row, row],
        out_specs=[row] * 4,
        out_shape=[jax.ShapeDtypeStruct((rows, cols), F32)] * 4,
        compiler_params=_cp(("parallel",)),
    )(parts, w, m, v)


def _adamw_layers(name, grads, lands, block, w, m, v):
    layers, rows, cols = w.shape
    nland = lands[0].shape[0]
    tr = rows
    while tr % 2 == 0 and tr > 8 and nland * tr * cols * 2 > (1 << 20):
        tr //= 2

    def body(blk_ref, *refs):
        own_refs, land_refs = refs[:layers], refs[layers:2 * layers]
        w_ref, m_ref, v_ref, g_out, d_out, m_out, v_out = refs[2 * layers:]
        layer = pl.program_id(0)
        for k in range(layers):
            @pl.when(layer == k)
            def _(k=k):
                g = own_refs[k][...].astype(F32)
                for s in range(nland):
                    g = g + land_refs[k][s].astype(F32)
                delta, mn, vn = _adam_math(w_ref[...], g, m_ref[...], v_ref[...])
                g_out[...] = g
                d_out[...] = delta
                m_out[...] = mn
                v_out[...] = vn

    def own_spec(k):
        return pl.BlockSpec((None, tr, cols), lambda l, i, blk: (blk[0], jnp.where(l == k, i, 0), 0))

    def land_spec(k):
        return pl.BlockSpec((nland, tr, cols), lambda l, i, blk: (0, jnp.where(l == k, i, 0), 0))

    row = pl.BlockSpec((None, tr, cols), lambda l, i, blk: (l, i, 0))
    return pl.pallas_call(
        body, name=name,
        grid_spec=pltpu.PrefetchScalarGridSpec(
            num_scalar_prefetch=1, grid=(layers, rows // tr),
            in_specs=[own_spec(k) for k in range(layers)] + [land_spec(k) for k in range(layers)] + [row, row, row],
            out_specs=[row] * 4),
        out_shape=[jax.ShapeDtypeStruct((layers, rows, cols), F32)] * 4,
        compiler_params=_cp(("arbitrary", "arbitrary")),
    )(block, *grads, *lands, w, m, v)


LANES = 128


def _seg_scratch(rows, d):
    return pltpu.VMEM((d // LANES, rows, LANES), F32)


def _to_segments(vals, scratch, out_ref):
    per = scratch.shape[1] // SEGS
    for c in range(scratch.shape[0]):
        cols = slice(c * LANES, (c + 1) * LANES)
        scratch[c] = vals[:, cols]
        for s in range(SEGS):
            out_ref[s, :, cols] = scratch.at[c][pl.ds(s, per, stride=SEGS), :].astype(out_ref.dtype)


def _from_segments(in_ref, scratch):
    per = scratch.shape[1] // SEGS
    for c in range(scratch.shape[0]):
        for s in range(SEGS):
            scratch.at[c][pl.ds(s, per, stride=SEGS), :] = in_ref[s, :, c * LANES:(c + 1) * LANES].astype(F32)
    return jnp.concatenate([scratch[c] for c in range(scratch.shape[0])], axis=1)


def _seg_view(a):
    return a.reshape(SEGS, a.shape[0] // SEGS, a.shape[1])


def _norm_fwd(name, h, z=None, g_post=None, g_pre=None, seg_z=False, seg_y=False):
    rows, d = h.shape
    tm = _tile(rows, 256)
    has_post, has_pre = z is not None, g_pre is not None
    nscratch = int(seg_z) + int(seg_y)

    def body(*refs):
        scratch = list(refs[len(refs) - nscratch:])
        it = iter(refs)
        hv = next(it)[...]
        if has_post:
            z_ref = next(it)
            zv = _from_segments(z_ref, scratch.pop(0)) if seg_z else z_ref[...].astype(F32)
            gp = next(it)[...]
        if has_pre:
            gq = next(it)[...]
        if has_post:
            r = lax.rsqrt(jnp.mean(zv * zv, axis=-1, keepdims=True) + RMS_EPS)
            hv = hv + (zv * r) * gp
            next(it)[...] = hv
        if has_pre:
            r = lax.rsqrt(jnp.mean(hv * hv, axis=-1, keepdims=True) + RMS_EPS)
            y = (hv * r) * gq
            if seg_y:
                _to_segments(y, scratch.pop(0), next(it))
            else:
                next(it)[...] = y.astype(BF16)

    row = pl.BlockSpec((tm, d), lambda i: (i, 0))
    seg = pl.BlockSpec((SEGS, tm // SEGS, d), lambda i: (0, i, 0))
    vec = pl.BlockSpec((1, d), lambda i: (0, 0))
    ins, in_specs, out_shape, out_specs = [h], [row], [], []
    if has_post:
        ins += [_seg_view(z) if seg_z else z, g_post]
        in_specs += [seg if seg_z else row, vec]
        out_shape.append(jax.ShapeDtypeStruct((rows, d), F32))
        out_specs.append(row)
    if has_pre:
        ins.append(g_pre)
        in_specs.append(vec)
        out_shape.append(jax.ShapeDtypeStruct((SEGS, rows // SEGS, d) if seg_y else (rows, d), BF16))
        out_specs.append(seg if seg_y else row)
    outs = pl.pallas_call(body, name=name, grid=(rows // tm,), in_specs=in_specs, out_specs=out_specs,
                          out_shape=out_shape, scratch_shapes=[_seg_scratch(tm, d)] * nscratch,
                          compiler_params=_cp(("parallel",)))(*ins)
    if seg_y:
        outs = list(outs[:-1]) + [outs[-1].reshape(rows, d)]
    return outs


def _rms_bwd_rows(x, g, dy):
    r = lax.rsqrt(jnp.mean(x * x, axis=-1, keepdims=True) + RMS_EPS)
    xn = x * r
    dg = jnp.sum(dy * xn, axis=0, keepdims=True)
    dxn = dy * g
    dx = r * (dxn - xn * jnp.mean(dxn * xn, axis=-1, keepdims=True))
    return dx, dg


def _norm_bwd(name, d_out, pre=None, post=None, seg_dy=False, seg_z=False):
    rows, d = d_out.shape
    tm = _tile(rows, 256)
    has_pre, has_post = pre is not None, post is not None
    nscratch = int(seg_dy) + 2 * int(seg_z)

    def body(*refs):
        scratch = list(refs[len(refs) - nscratch:])
        it = iter(refs)
        dres = next(it)[...]
        if has_pre:
            dy_ref = next(it)
            dy = _from_segments(dy_ref, scratch.pop(0)) if seg_dy else dy_ref[...].astype(F32)
            xp, gq = next(it)[...], next(it)[...]
        if has_post:
            z_ref = next(it)
            zv = _from_segments(z_ref, scratch.pop(0)) if seg_z else z_ref[...].astype(F32)
            gp = next(it)[...]
        first = pl.program_id(0) == 0
        if has_pre:
            dx, dg = _rms_bwd_rows(xp, gq, dy)
            dres = dres + dx
            next(it)[...] = dres
            dg_ref = next(it)

            @pl.when(first)
            def _():
                dg_ref[...] = jnp.zeros_like(dg_ref)
            dg_ref[...] += dg
        if has_post:
            dz, dg2 = _rms_bwd_rows(zv, gp, dres)
            if seg_z:
                _to_segments(dz, scratch.pop(0), next(it))
            else:
                next(it)[...] = dz.astype(BF16)
            dg2_ref = next(it)

            @pl.when(first)
            def _():
                dg2_ref[...] = jnp.zeros_like(dg2_ref)
            dg2_ref[...] += dg2

    row = pl.BlockSpec((tm, d), lambda i: (i, 0))
    seg = pl.BlockSpec((SEGS, tm // SEGS, d), lambda i: (0, i, 0))
    vec = pl.BlockSpec((1, d), lambda i: (0, 0))
    ins, in_specs, out_shape, out_specs = [d_out], [row], [], []
    if has_pre:
        d_y, x_pre, g_pre = pre
        ins += [_seg_view(d_y) if seg_dy else d_y, x_pre, g_pre]
        in_specs += [seg if seg_dy else row, row, vec]
        out_shape += [jax.ShapeDtypeStruct((rows, d), F32), jax.ShapeDtypeStruct((1, d), F32)]
        out_specs += [row, vec]
    if has_post:
        z, g_post = post
        ins += [_seg_view(z) if seg_z else z, g_post]
        in_specs += [seg if seg_z else row, vec]
        out_shape += [jax.ShapeDtypeStruct((SEGS, rows // SEGS, d) if seg_z else (rows, d), BF16),
                      jax.ShapeDtypeStruct((1, d), F32)]
        out_specs += [seg if seg_z else row, vec]
    outs = pl.pallas_call(body, name=name, grid=(rows // tm,), in_specs=in_specs, out_specs=out_specs,
                          out_shape=out_shape, scratch_shapes=[_seg_scratch(tm, d)] * nscratch,
                          compiler_params=_cp(("arbitrary",)))(*ins)
    if seg_z:
        outs = list(outs)
        outs[-2] = outs[-2].reshape(rows, d)
    return outs


def _loss_grad(name, y, target):
    rows, d = y.shape
    tm = _tile(rows, 256)

    def body(y_ref, t_ref, dy_ref, loss_ref):
        err = y_ref[...] - t_ref[...]
        dy_ref[...] = err * (1.0 / d)

        @pl.when(pl.program_id(0) == 0)
        def _():
            loss_ref[...] = jnp.zeros_like(loss_ref)
        loss_ref[...] += jnp.full(loss_ref.shape, (0.5 / d) * jnp.sum(err * err), F32)

    row = pl.BlockSpec((tm, d), lambda i: (i, 0))
    return pl.pallas_call(
        body, name=name, grid=(rows // tm,), in_specs=[row, row],
        out_specs=[row, pl.BlockSpec((1, 128), lambda i: (0, 0))],
        out_shape=[jax.ShapeDtypeStruct((rows, d), F32), jax.ShapeDtypeStruct((1, 128), F32)],
        compiler_params=_cp(("arbitrary",)))(y, target)


NT_DIMS = (((1,), (1,)), ((), ()))
TN_DIMS = (((0,), (0,)), ((), ()))


def _mm_nn_blk(name, a, wblk, relu2=False, after=()):
    m, k = a.shape
    nb = wblk.shape[2]
    tm = _tile(m, 1024)
    after = list(after)

    def body(a_ref, w_ref, *rest):
        r = jnp.dot(a_ref[...], w_ref[...], preferred_element_type=F32)
        if relu2:
            rr = jnp.maximum(r, 0.0)
            r = rr * rr
        rest[-1][...] = r.astype(BF16)

    return pl.pallas_call(
        body, name=name, grid=(NDEV, m // tm),
        in_specs=[pl.BlockSpec((tm, k), lambda d, i: (i, 0)), pl.BlockSpec((None, k, nb), lambda d, i: (d, 0, 0))]
        + [ANY] * len(after),
        out_specs=pl.BlockSpec((tm, nb), lambda d, i: (i, d)),
        out_shape=jax.ShapeDtypeStruct((m, NDEV * nb), BF16),
        compiler_params=_cp(("parallel", "parallel"), VMEM_MM))(a, wblk, *after)


def _accumulate(acc, o_ref, r, step, last):
    if acc is None:
        o_ref[...] = r.astype(o_ref.dtype)
        return

    @pl.when(step == 0)
    def _():
        acc[...] = r

    @pl.when(jnp.logical_and(step > 0, step < last))
    def _():
        acc[...] += r

    @pl.when(jnp.logical_and(step > 0, step == last))
    def _():
        o_ref[...] = (acc[...] + r).astype(o_ref.dtype)


def _mm_nn(name, a, w, after=()):
    m, kb = a.shape
    n = w.shape[1]
    tm, tk = _tile(m, 512), _tile(kb, 2048)
    steps = kb // tk
    after = list(after)

    def body(a_ref, w_ref, *rest):
        o_ref, scratch = rest[len(after)], rest[len(after) + 1:]
        r = jnp.dot(a_ref[...], w_ref[...], preferred_element_type=F32)
        _accumulate(scratch[0] if scratch else None, o_ref, r, pl.program_id(1), steps - 1)

    return pl.pallas_call(
        body, name=name, grid=(m // tm, steps),
        in_specs=[pl.BlockSpec((tm, tk), lambda i, s: (i, s)), pl.BlockSpec((tk, n), lambda i, s: (s, 0))]
        + [ANY] * len(after),
        out_specs=pl.BlockSpec((tm, n), lambda i, s: (i, 0)),
        out_shape=jax.ShapeDtypeStruct((m, n), BF16),
        scratch_shapes=[pltpu.VMEM((tm, n), F32)] if steps > 1 else [],
        compiler_params=_cp(("parallel", "arbitrary"), VMEM_MM))(a, w, *after)


def _mm_nt_rows(name, dy, w, act=None, out_dtype=BF16):
    m, n = dy.shape
    kw = w.shape[0]
    tm, tkw = _tile(m, 1024), _tile(kw, 1024)

    def body(dy_ref, w_ref, *rest):
        r = lax.dot_general(dy_ref[...], w_ref[...], NT_DIMS, preferred_element_type=F32)
        if act is None:
            rest[0][...] = r.astype(out_dtype)
        else:
            rest[1][...] = (r * (2.0 * jnp.sqrt(rest[0][...].astype(F32)))).astype(BF16)

    ins = [dy, w]
    in_specs = [pl.BlockSpec((tm, n), lambda j, i: (i, 0)), pl.BlockSpec((tkw, n), lambda j, i: (j, 0))]
    if act is not None:
        ins.append(act)
        in_specs.append(pl.BlockSpec((tm, tkw), lambda j, i: (i, j)))
    return pl.pallas_call(
        body, name=name, grid=(kw // tkw, m // tm), in_specs=in_specs,
        out_specs=pl.BlockSpec((tm, tkw), lambda j, i: (i, j)),
        out_shape=jax.ShapeDtypeStruct((m, kw), out_dtype if act is None else BF16),
        compiler_params=_cp(("parallel", "parallel"), VMEM_MM))(*ins)


def _mm_nt_blk(name, dy, wblk, after=None):
    m = dy.shape[0]
    _, kw, nb = wblk.shape
    tm, per = _tile(m, 512), 2

    extra = list(after or ())

    def body(dy_ref, w_ref, *rest):
        o_ref, acc = rest[len(extra):]
        r = lax.dot_general(dy_ref[:, :nb], w_ref[0], NT_DIMS, preferred_element_type=F32)
        for t in range(1, per):
            r = r + lax.dot_general(dy_ref[:, t * nb:(t + 1) * nb], w_ref[t], NT_DIMS, preferred_element_type=F32)
        _accumulate(acc, o_ref, r, pl.program_id(1), NDEV // per - 1)

    return pl.pallas_call(
        body, name=name, grid=(m // tm, NDEV // per),
        in_specs=[pl.BlockSpec((tm, per * nb), lambda i, s: (i, s)),
                  pl.BlockSpec((per, kw, nb), lambda i, s: (s, 0, 0))] + [ANY] * len(extra),
        out_specs=pl.BlockSpec((tm, kw), lambda i, s: (i, 0)),
        out_shape=jax.ShapeDtypeStruct((m, kw), BF16),
        scratch_shapes=[pltpu.VMEM((tm, kw), F32)],
        compiler_params=_cp(("parallel", "arbitrary"), VMEM_MM))(dy, wblk, *extra)


def _mm_tn(name, x, dy, nb=None):
    t, mx = x.shape
    n = dy.shape[1]
    tmx = _tile(mx, 512)
    tn = nb if nb is not None else _tile(n, 1024)

    def body(x_ref, dy_ref, o_ref):
        o_ref[...] = lax.dot_general(x_ref[...], dy_ref[...], TN_DIMS, preferred_element_type=F32).astype(BF16)

    if nb is None:
        out_shape = jax.ShapeDtypeStruct((mx, n), BF16)
        out_spec = pl.BlockSpec((tmx, tn), lambda j, i: (i, j))
    else:
        out_shape = jax.ShapeDtypeStruct((NDEV, mx, nb), BF16)
        out_spec = pl.BlockSpec((None, tmx, nb), lambda j, i: (j, i, 0))
    return pl.pallas_call(
        body, name=name, grid=(n // tn, mx // tmx),
        in_specs=[pl.BlockSpec((t, tmx), lambda j, i: (0, i)), pl.BlockSpec((t, tn), lambda j, i: (0, j))],
        out_specs=out_spec, out_shape=out_shape,
        compiler_params=_cp(("parallel", "parallel"), VMEM_MM))(x, dy)


def _gelu(x):
    return 0.5 * x * (1.0 + jnp.tanh(GELU_C * (x + GELU_A * (x * x * x))))


def _gelu_grad(x):
    t = jnp.tanh(GELU_C * (x + GELU_A * (x * x * x)))
    return 0.5 * (1.0 + t) + 0.5 * x * (1.0 - t * t) * (GELU_C * (1.0 + 3.0 * GELU_A * (x * x)))


def _layernorm(a):
    mu = jnp.mean(a, axis=-1, keepdims=True)
    ac = a - mu
    rstd = lax.rsqrt(jnp.mean(ac * ac, axis=-1, keepdims=True) + LN_EPS)
    return ac * rstd, rstd


def _shift_rows(z, halo, k):
    zr = pltpu.roll(z, k, 0)
    hr = pltpu.roll(halo, k, 0)
    row = lax.broadcasted_iota(jnp.int32, hr.shape, 0)
    top = jnp.where(row < k, hr, zr[:HALO])
    return jnp.concatenate([top, zr[HALO:]], axis=0)


def _shift_rows_up(z, halo, k):
    rows = z.shape[0]
    zr = pltpu.roll(z, rows - k, 0)
    hr = pltpu.roll(halo, HALO - k, 0)
    row = lax.broadcasted_iota(jnp.int32, hr.shape, 0)
    bot = jnp.where(row >= HALO - k, hr, zr[rows - HALO:])
    return jnp.concatenate([zr[:rows - HALO], bot], axis=0)


def _causal_mask():
    t = lax.broadcasted_iota(jnp.int32, (CHUNK, CHUNK), 0)
    s = lax.broadcasted_iota(jnp.int32, (CHUNK, CHUNK), 1)
    return s <= t


def _gate_specs(tm, rows, width):
    per = tm // HALO_BF16
    last = rows // HALO_BF16 - 1
    cur = pl.BlockSpec((tm, width), lambda i: (i, 0))
    prev = pl.BlockSpec((HALO_BF16, width), lambda i: (jnp.maximum(i * per - 1, 0), 0))
    nxt = pl.BlockSpec((HALO_BF16, width), lambda i: (jnp.minimum((i + 1) * per, last), 0))
    return cur, prev, nxt


def _cols(ref, lo, hi):
    return ref[:, lo:hi].astype(F32)


def _halo_before(ref, lo, hi):
    return ref[:, lo:hi].astype(F32)[HALO_BF16 - HALO:]


def _halo_after(ref, lo, hi):
    return ref[:, lo:hi].astype(F32)[:HALO]


def _gate_fwd(name, proj, w_s, b_st, cw):
    rows, width = proj.shape
    w = width // 5
    groups = w // CHUNK
    tm = _tile(rows, 256)
    cur, prev, _ = _gate_specs(tm, rows, width)

    def body(p_ref, h_ref, ws_ref, b_ref, cw_ref, o_ref):
        mask = _causal_mask()
        au = _gelu(_cols(p_ref, 0, w))
        vn, _ = _layernorm(_gelu(_cols(p_ref, w, 2 * w)))
        vn = vn.astype(BF16)
        for g in range(groups):
            wc = jnp.where(mask, ws_ref[g], 0.0).astype(BF16)
            cols = slice(g * CHUNK, (g + 1) * CHUNK)
            for ch in range(tm // CHUNK):
                rws = slice(ch * CHUNK, (ch + 1) * CHUNK)
                mixed = jnp.dot(wc, vn[rws, cols], preferred_element_type=F32) + b_ref[:, g:g + 1]
                o_ref[rws, cols] = (au[rws, cols] * mixed).astype(BF16)
        z = _cols(p_ref, 3 * w, 4 * w) * _cols(p_ref, 4 * w, 5 * w)
        zh = _halo_before(h_ref, 3 * w, 4 * w) * _halo_before(h_ref, 4 * w, 5 * w)
        zh = jnp.where(pl.program_id(0) == 0, 0.0, zh)
        y = cw_ref[0:1, :] * _shift_rows(z, zh, 2) + cw_ref[1:2, :] * _shift_rows(z, zh, 1) + cw_ref[2:3, :] * z
        o_ref[:, w:2 * w] = (_cols(p_ref, 2 * w, 3 * w) * y).astype(BF16)

    full = lambda a: pl.BlockSpec(a.shape, lambda i: (0,) * a.ndim)
    return pl.pallas_call(
        body, name=name, grid=(rows // tm,),
        in_specs=[cur, prev, full(w_s), full(b_st), full(cw)],
        out_specs=pl.BlockSpec((tm, 2 * w), lambda i: (i, 0)),
        out_shape=jax.ShapeDtypeStruct((rows, 2 * w), BF16),
        compiler_params=_cp(("parallel",)))(proj, proj, w_s, b_st, cw)


def _gate_bwd(name, proj, d_ab, w_s, b_st, cw):
    rows, width = proj.shape
    w = width // 5
    groups = w // CHUNK
    tm = _tile(rows, 256)
    cur, prev, nxt = _gate_specs(tm, rows, width)
    dcur, _, dnxt = _gate_specs(tm, rows, 2 * w)

    def body(p_ref, ph_ref, pn_ref, d_ref, dn_ref, ws_ref, b_ref, cw_ref, o_ref, dws_ref, dbs_ref, dcw_ref):
        i = pl.program_id(0)

        @pl.when(i == 0)
        def _():
            dws_ref[...] = jnp.zeros_like(dws_ref)
            dbs_ref[...] = jnp.zeros_like(dbs_ref)
            dcw_ref[...] = jnp.zeros_like(dcw_ref)

        mask = _causal_mask()
        u, v = _cols(p_ref, 0, w), _cols(p_ref, w, 2 * w)
        au, av = _gelu(u), _gelu(v)
        vn, rstd = _layernorm(av)
        vnb = vn.astype(BF16)
        d_a = _cols(d_ref, 0, w)
        d_mixed = (d_a * au).astype(BF16)
        ones = jnp.ones((HALO, CHUNK), BF16)
        d_vn_cols = []
        d_au_cols = []
        for g in range(groups):
            wc = jnp.where(mask, ws_ref[g], 0.0).astype(BF16)
            cols = slice(g * CHUNK, (g + 1) * CHUNK)
            dw = jnp.zeros((CHUNK, CHUNK), F32)
            db = jnp.zeros((HALO, CHUNK), F32)
            d_vn_rows, d_au_rows = [], []
            for ch in range(tm // CHUNK):
                rws = slice(ch * CHUNK, (ch + 1) * CHUNK)
                mixed = jnp.dot(wc, vnb[rws, cols], preferred_element_type=F32) + b_ref[:, g:g + 1]
                d_au_rows.append(d_a[rws, cols] * mixed)
                dm = d_mixed[rws, cols]
                dw = dw + lax.dot_general(dm, vnb[rws, cols], NT_DIMS, preferred_element_type=F32)
                db = db + lax.dot_general(ones, dm, NT_DIMS, preferred_element_type=F32)
                d_vn_rows.append(lax.dot_general(wc, dm, TN_DIMS, preferred_element_type=F32))
            dws_ref[g] += jnp.where(mask, dw, 0.0)
            dbs_ref[g:g + 1, :] += db[0:1, :]
            d_vn_cols.append(jnp.concatenate(d_vn_rows, axis=0))
            d_au_cols.append(jnp.concatenate(d_au_rows, axis=0))
        d_vn = jnp.concatenate(d_vn_cols, axis=1)
        d_au = jnp.concatenate(d_au_cols, axis=1)
        d_av = rstd * (d_vn - jnp.mean(d_vn, axis=-1, keepdims=True)
                       - vn * jnp.mean(d_vn * vn, axis=-1, keepdims=True))
        o_ref[:, 0:w] = (d_au * _gelu_grad(u)).astype(BF16)
        o_ref[:, w:2 * w] = (d_av * _gelu_grad(v)).astype(BF16)

        gb, gc, bx = _cols(p_ref, 2 * w, 3 * w), _cols(p_ref, 3 * w, 4 * w), _cols(p_ref, 4 * w, 5 * w)
        z = gc * bx
        zh = jnp.where(i == 0, 0.0, _halo_before(ph_ref, 3 * w, 4 * w) * _halo_before(ph_ref, 4 * w, 5 * w))
        z1, z2 = _shift_rows(z, zh, 1), _shift_rows(z, zh, 2)
        d_b = _cols(d_ref, w, 2 * w)
        y = cw_ref[0:1, :] * z2 + cw_ref[1:2, :] * z1 + cw_ref[2:3, :] * z
        dy = d_b * gb
        dyn = jnp.where(i == pl.num_programs(0) - 1, 0.0,
                        _halo_after(dn_ref, w, 2 * w) * _halo_after(pn_ref, 2 * w, 3 * w))
        dz = (cw_ref[2:3, :] * dy + cw_ref[1:2, :] * _shift_rows_up(dy, dyn, 1)
              + cw_ref[0:1, :] * _shift_rows_up(dy, dyn, 2))
        dcw_ref[0:1, :] += jnp.sum(dy * z2, axis=0, keepdims=True)
        dcw_ref[1:2, :] += jnp.sum(dy * z1, axis=0, keepdims=True)
        dcw_ref[2:3, :] += jnp.sum(dy * z, axis=0, keepdims=True)
        o_ref[:, 2 * w:3 * w] = (d_b * y).astype(BF16)
        o_ref[:, 3 * w:4 * w] = (dz * bx).astype(BF16)
        o_ref[:, 4 * w:5 * w] = (dz * gc).astype(BF16)

    full = lambda a: pl.BlockSpec(a.shape, lambda i: (0,) * a.ndim)
    acc = lambda shape: pl.BlockSpec(shape, lambda i: (0,) * len(shape))
    return pl.pallas_call(
        body, name=name, grid=(rows // tm,),
        in_specs=[cur, prev, nxt, dcur, dnxt, full(w_s), full(b_st), full(cw)],
        out_specs=[pl.BlockSpec((tm, width), lambda i: (i, 0)), acc((groups, CHUNK, CHUNK)),
                   acc((groups, CHUNK)), acc((HALO, w))],
        out_shape=[jax.ShapeDtypeStruct((rows, width), BF16), jax.ShapeDtypeStruct((groups, CHUNK, CHUNK), F32),
                   jax.ShapeDtypeStruct((groups, CHUNK), F32), jax.ShapeDtypeStruct((HALO, w), F32)],
        compiler_params=_cp(("arbitrary",), VMEM_MM))(proj, proj, proj, d_ab, d_ab, w_s, b_st, cw)


def _flat(x):
    return x.reshape(-1, x.shape[-1])


def _rope(t, cosf, sins):
    t2 = _flat(t)
    return (t2 * _flat(cosf) + pltpu.roll(t2, HEAD // 2, 1) * _flat(sins)).reshape(t.shape)


def _rope_bwd(dt, cosf, sins):
    d2 = _flat(dt)
    return (d2 * _flat(cosf) + pltpu.roll(d2 * _flat(sins), HEAD // 2, 1)).reshape(dt.shape)


ATT_UNITS = ATT_TILE // CHUNK


def _attn_units(phases):
    for b, d in enumerate(DILATIONS):
        blocks = ATT_TILE // (CHUNK * d)
        for visit in phases:
            for r in range(d):
                if blocks <= ATT_UNROLL:
                    for j in range(blocks):
                        visit(b, d, r, j, r * blocks + j)
                else:
                    def step(jj, carry, b=b, d=d, r=r, visit=visit, blocks=blocks):
                        for u in range(ATT_UNROLL):
                            j = jj * ATT_UNROLL + u
                            visit(b, d, r, j, r * blocks + j)
                        return carry
                    lax.fori_loop(0, blocks // ATT_UNROLL, step, 0)


class _Unit:
    def __init__(self, d, r, j):
        self.segs = [r + d * k for k in range(SEGS // d)]
        self.w = CHUNK * d // SEGS
        q0 = j * self.w
        self.q0 = q0 if isinstance(q0, int) else pl.multiple_of(q0, HALO)
        k0 = CHUNK + (j - 1) * self.w
        self.k0 = k0 if isinstance(k0, int) else pl.multiple_of(k0, HALO)

    def queries(self, ref):
        return _chunks(ref, self.segs, self.q0, self.w)

    def keys(self, ref):
        return _chunks(ref, self.segs, self.k0, 2 * self.w)

    def put_queries(self, ref, val, add=False):
        _put_chunks(ref, self.segs, self.q0, self.w, val, add)

    def put_keys(self, ref, val, add=False):
        _put_chunks(ref, self.segs, self.k0, 2 * self.w, val, add)


def _chunks(ref, segs, start, size):
    parts = [ref[s, pl.ds(start, size), :] for s in segs]
    return parts[0] if len(parts) == 1 else jnp.concatenate(parts, axis=0)


def _put_chunks(ref, segs, start, size, val, add):
    for k, s in enumerate(segs):
        piece = val[k * size:(k + 1) * size]
        if add:
            ref[s, pl.ds(start, size), :] += piece
        else:
            ref[s, pl.ds(start, size), :] = piece


def _band_bias():
    qi = lax.broadcasted_iota(jnp.int32, (CHUNK, 2 * CHUNK), 0)
    ki = lax.broadcasted_iota(jnp.int32, (CHUNK, 2 * CHUNK), 1)
    tables = []
    for d in DILATIONS:
        nseg, w = SEGS // d, CHUNK * d // SEGS
        pos_q = nseg * (qi % w) + qi // w
        pos_k = nseg * (ki % (2 * w) - w) + ki // (2 * w)
        band = (pos_q >= pos_k) & (pos_q - pos_k <= CHUNK)
        tables += [jnp.where(band, 0.0, -jnp.inf), jnp.where(band & (pos_k >= 0), 0.0, -jnp.inf)]
    return jnp.stack(tables).astype(F32)


def _bias_spec():
    return pl.BlockSpec((2 * len(DILATIONS), CHUNK, 2 * CHUNK), lambda h, n: (0, 0, 0))


def _unit_bias(bias, b, n, j):
    if isinstance(j, int) and j != 0:
        return bias[2 * b]
    return bias[2 * b + jnp.where(jnp.logical_and(n == 0, j == 0), 1, 0)]


def _attn_in_specs(heads):
    blk = (SEGS, CHUNK, HEAD)
    prev = lambda n: jnp.maximum(n - 1, 0)
    return [
        pl.BlockSpec(blk, lambda h, n: (0, n, h)),
        pl.BlockSpec(blk, lambda h, n: (0, n, heads + h)),
        pl.BlockSpec(blk, lambda h, n: (0, prev(n), heads + h)),
        pl.BlockSpec(blk, lambda h, n: (0, n, 2 * heads + h)),
        pl.BlockSpec(blk, lambda h, n: (0, prev(n), 2 * heads + h)),
        pl.BlockSpec(blk, lambda h, n: (0, n, 0)),
        pl.BlockSpec(blk, lambda h, n: (0, n, 0)),
        pl.BlockSpec(blk, lambda h, n: (0, prev(n), 0)),
        pl.BlockSpec(blk, lambda h, n: (0, prev(n), 0)),
    ]


def _attn_load(q_ref, kc_ref, kp_ref, vc_ref, vp_ref, cc_ref, sc_ref, cp_ref, sp_ref, qr, kcat, vcat):
    qr[...] = _rope(q_ref[...].astype(F32), cc_ref[...], sc_ref[...])
    kcat[:, pl.ds(0, CHUNK), :] = _rope(kp_ref[...].astype(F32), cp_ref[...], sp_ref[...])
    kcat[:, pl.ds(CHUNK, CHUNK), :] = _rope(kc_ref[...].astype(F32), cc_ref[...], sc_ref[...])
    vcat[:, pl.ds(0, CHUNK), :] = vp_ref[...].astype(F32)
    vcat[:, pl.ds(CHUNK, CHUNK), :] = vc_ref[...].astype(F32)


def _attn_fwd(name, qkv, cosf, sins, bias):
    t = qkv.shape[0]
    heads = qkv.shape[1] // (3 * HEAD)
    scale = HEAD ** -0.5
    nbr = len(DILATIONS)

    def body(q_ref, kc_ref, kp_ref, vc_ref, vp_ref, cc_ref, sc_ref, cp_ref, sp_ref, bias, o_ref, lse_ref,
             qr, kcat, vcat, obr, lbr, pn):
        n = pl.program_id(1)
        _attn_load(q_ref, kc_ref, kp_ref, vc_ref, vp_ref, cc_ref, sc_ref, cp_ref, sp_ref, qr, kcat, vcat)

        def probs(b, d, r, j, u):
            unit = _Unit(d, r, j)
            s = lax.dot_general(unit.queries(qr).astype(BF16), unit.keys(kcat).astype(BF16), NT_DIMS,
                                preferred_element_type=F32) * scale + _unit_bias(bias, b, n, j)
            mx = jnp.max(s, axis=-1, keepdims=True)
            p = jnp.exp(s - mx)
            den = jnp.sum(p, axis=-1, keepdims=True)
            pn[u] = (p * (1.0 / den)).astype(BF16)
            unit.put_queries(lbr.at[b], jnp.broadcast_to(mx + jnp.log(den), (CHUNK, HEAD)))

        def values(b, d, r, j, u):
            unit = _Unit(d, r, j)
            unit.put_queries(obr.at[b], jnp.dot(pn[u], unit.keys(vcat).astype(BF16), preferred_element_type=F32))

        _attn_units([probs, values])
        ls = [lbr[b] for b in range(nbr)]
        top = functools.reduce(jnp.maximum, ls)
        ws = [jnp.exp(l - top) for l in ls]
        tot = functools.reduce(jnp.add, ws)
        inv = 1.0 / tot
        o = (ws[0] * inv) * obr[0]
        for b in range(1, nbr):
            o = o + (ws[b] * inv) * obr[b]
        o_ref[...] = o.astype(BF16)
        lse_ref[...] = top + jnp.log(tot)

    blk = (SEGS, CHUNK, HEAD)
    keys = pltpu.VMEM((SEGS, 2 * CHUNK, HEAD), F32)
    tile = pl.BlockSpec(blk, lambda h, n: (0, n, h))
    seg = t // SEGS
    qkv3, cos3, sin3 = _seg_view(qkv), _seg_view(cosf), _seg_view(sins)
    o, lse = pl.pallas_call(
        body, name=name, grid=(heads, t // ATT_TILE), in_specs=_attn_in_specs(heads) + [_bias_spec()],
        out_specs=[tile, tile],
        out_shape=[jax.ShapeDtypeStruct((SEGS, seg, heads * HEAD), BF16),
                   jax.ShapeDtypeStruct((SEGS, seg, heads * HEAD), F32)],
        scratch_shapes=[pltpu.VMEM(blk, F32), keys, keys,
                        pltpu.VMEM((nbr,) + blk, F32), pltpu.VMEM((nbr,) + blk, F32),
                        pltpu.VMEM((ATT_UNITS, CHUNK, 2 * CHUNK), BF16)],
        compiler_params=_cp(("parallel", "parallel"), VMEM_MM),
    )(qkv3, qkv3, qkv3, qkv3, qkv3, cos3, sin3, cos3, sin3, bias)
    return o.reshape(t, heads * HEAD), lse.reshape(t, heads * HEAD)


def _attn_bwd(name, qkv, cosf, sins, bias, d_o, o, lse):
    t = qkv.shape[0]
    heads = qkv.shape[1] // (3 * HEAD)
    scale = HEAD ** -0.5

    def body(q_ref, kc_ref, kp_ref, vc_ref, vp_ref, cc_ref, sc_ref, cp_ref, sp_ref, do_ref, o_ref, lse_ref, bias,
             dq_ref, dko_ref, dkp_ref, dvo_ref, dvp_ref, qr, kcat, vcat, dq_acc, dk_acc, dv_acc, delta, ps, dss):
        n = pl.program_id(1)
        _attn_load(q_ref, kc_ref, kp_ref, vc_ref, vp_ref, cc_ref, sc_ref, cp_ref, sp_ref, qr, kcat, vcat)
        dq_acc[...] = jnp.zeros_like(dq_acc)
        dk_acc[...] = jnp.zeros_like(dk_acc)
        dv_acc[...] = jnp.zeros_like(dv_acc)
        delta[...] = jnp.broadcast_to(
            jnp.sum(do_ref[...] * o_ref[...].astype(F32), axis=-1, keepdims=True), delta.shape)

        def probs(b, d, r, j, u):
            unit = _Unit(d, r, j)
            s = lax.dot_general(unit.queries(qr).astype(BF16), unit.keys(kcat).astype(BF16), NT_DIMS,
                                preferred_element_type=F32) * scale + _unit_bias(bias, b, n, j)
            ps[u] = jnp.exp(s - unit.queries(lse_ref)[:, 0:1]).astype(BF16)

        def score_grads(b, d, r, j, u):
            unit = _Unit(d, r, j)
            dp = lax.dot_general(unit.queries(do_ref).astype(BF16), unit.keys(vcat).astype(BF16), NT_DIMS,
                                 preferred_element_type=F32)
            dss[u] = (ps[u].astype(F32) * (dp - unit.queries(delta)[:, 0:1]) * scale).astype(BF16)

        def input_grads(b, d, r, j, u):
            unit = _Unit(d, r, j)
            ds = dss[u]
            unit.put_queries(dq_acc, jnp.dot(ds, unit.keys(kcat).astype(BF16), preferred_element_type=F32), add=True)
            unit.put_keys(dk_acc, lax.dot_general(ds, unit.queries(qr).astype(BF16), TN_DIMS,
                                                  preferred_element_type=F32), add=True)
            unit.put_keys(dv_acc, lax.dot_general(ps[u], unit.queries(do_ref).astype(BF16), TN_DIMS,
                                                  preferred_element_type=F32), add=True)

        _attn_units([probs, score_grads, input_grads])
        dq_ref[...] = _rope_bwd(dq_acc[...], cc_ref[...], sc_ref[...]).astype(BF16)
        dkp_ref[...] = _rope_bwd(dk_acc[:, pl.ds(0, CHUNK), :], cp_ref[...], sp_ref[...]).astype(BF16)
        dko_ref[...] = _rope_bwd(dk_acc[:, pl.ds(CHUNK, CHUNK), :], cc_ref[...], sc_ref[...]).astype(BF16)
        dvp_ref[...] = dv_acc[:, pl.ds(0, CHUNK), :].astype(BF16)
        dvo_ref[...] = dv_acc[:, pl.ds(CHUNK, CHUNK), :].astype(BF16)

    blk = (SEGS, CHUNK, HEAD)
    tile = pl.BlockSpec(blk, lambda h, n: (0, n, h))
    big = pltpu.VMEM((SEGS, 2 * CHUNK, HEAD), F32)
    seg = t // SEGS
    qkv3, cos3, sin3 = _seg_view(qkv), _seg_view(cosf), _seg_view(sins)
    return pl.pallas_call(
        body, name=name, grid=(heads, t // ATT_TILE),
        in_specs=_attn_in_specs(heads) + [tile, tile, tile, _bias_spec()],
        out_specs=[tile] * 5,
        out_shape=[jax.ShapeDtypeStruct((SEGS, seg, heads * HEAD), BF16)] * 5,
        scratch_shapes=[pltpu.VMEM(blk, F32), big, big, pltpu.VMEM(blk, F32), big, big, pltpu.VMEM(blk, F32),
                        pltpu.VMEM((ATT_UNITS, CHUNK, 2 * CHUNK), BF16), pltpu.VMEM((ATT_UNITS, CHUNK, 2 * CHUNK), BF16)],
        compiler_params=_cp(("parallel", "parallel"), 60 << 20),
    )(qkv3, qkv3, qkv3, qkv3, qkv3, cos3, sin3, cos3, sin3, _seg_view(d_o), _seg_view(o), _seg_view(lse), bias)


def _attn_merge(name, dq, dk_own, dk_prev, dv_own, dv_prev):
    _, seg, hd = dq.shape
    nt = seg // CHUNK
    tw = _tile(hd, 512)

    def body(dq_ref, dko_ref, dkn_ref, dvo_ref, dvn_ref, o_ref):
        last = pl.program_id(0) == nt - 1
        part = pl.program_id(1)

        @pl.when(part == 0)
        def _():
            o_ref[...] = dq_ref[...]

        @pl.when(part == 1)
        def _():
            o_ref[...] = (dko_ref[...].astype(F32) + jnp.where(last, 0.0, dkn_ref[...].astype(F32))).astype(BF16)

        @pl.when(part == 2)
        def _():
            o_ref[...] = (dvo_ref[...].astype(F32) + jnp.where(last, 0.0, dvn_ref[...].astype(F32))).astype(BF16)

    blk = (SEGS, CHUNK, tw)

    def own(part):
        return pl.BlockSpec(blk, lambda n, p, c: (0, jnp.where(p == part, n, 0), jnp.where(p == part, c, 0)))

    def nxt(part):
        return pl.BlockSpec(blk, lambda n, p, c: (0, jnp.where(p == part, jnp.minimum(n + 1, nt - 1), 0),
                                                  jnp.where(p == part, c, 0)))

    per = hd // tw
    return pl.pallas_call(
        body, name=name, grid=(nt, 3, per), in_specs=[own(0), own(1), nxt(1), own(2), nxt(2)],
        out_specs=pl.BlockSpec(blk, lambda n, p, c: (0, n, p * per + c)),
        out_shape=jax.ShapeDtypeStruct((SEGS, seg, 3 * hd), BF16),
        compiler_params=_cp(("parallel", "parallel", "parallel"), VMEM_MM),
    )(dq, dk_own, dk_prev, dv_own, dv_prev).reshape(SEGS * seg, 3 * hd)


def _sum_parts(name, parts):
    nparts, rows, cols = parts.shape
    tr = _tile(rows, 256)

    def body(p_ref, o_ref):
        s = p_ref[0]
        for k in range(1, nparts):
            s = s + p_ref[k]
        o_ref[...] = s

    return pl.pallas_call(
        body, name=name, grid=(rows // tr,),
        in_specs=[pl.BlockSpec((nparts, tr, cols), lambda i: (0, i, 0))],
        out_specs=pl.BlockSpec((tr, cols), lambda i: (i, 0)),
        out_shape=jax.ShapeDtypeStruct((rows, cols), F32),
        compiler_params=_cp(("parallel",)))(parts)


def _rows128(a, pad_to=8):
    flat = a.reshape(-1)
    rows = -(-flat.shape[0] // 128)
    rows = -(-rows // pad_to) * pad_to
    flat = jnp.pad(flat, (0, rows * 128 - flat.shape[0]))
    return flat.reshape(rows, 128)


def _pack(arrays):
    return jnp.concatenate([_rows128(a) for a in arrays], axis=0)


def _unpack(packed, like):
    out, at = [], 0
    for a in like:
        size = 1
        for s in a.shape:
            size *= s
        rows = -(-(-(-size // 128)) // 8) * 8
        out.append(packed[at:at + rows].reshape(-1)[:size].reshape(a.shape))
        at += rows
    return out


def kernel(x, norm_mix_pre, norm_mix_post, norm_mlp_pre, norm_mlp_post, w_in_ab, w_spatial, b_spatial, conv_w, w_out_ab, w_qkv, w_o, w_up, w_down, loss_target, m_norm_mix_pre, m_norm_mix_post, m_norm_mlp_pre, m_norm_mlp_post, m_w_in_ab, m_w_spatial, m_b_spatial, m_conv_w, m_w_out_ab, m_w_qkv, m_w_o, m_w_up, m_w_down, v_norm_mix_pre, v_norm_mix_post, v_norm_mlp_pre, v_norm_mlp_post, v_w_in_ab, v_w_spatial, v_b_spatial, v_conv_w, v_w_out_ab, v_w_qkv, v_w_o, v_w_up, v_w_down):
    depth = norm_mix_pre.shape[0]
    seq, dm = x.shape[1], x.shape[2]
    h0 = x.reshape(seq, dm)
    target = loss_target.reshape(seq, dm)
    ax, ay, ac = lax.axis_index("x"), lax.axis_index("y"), lax.axis_index("c")
    my_block = 4 * ax + 2 * ay + ac
    block = jnp.reshape(my_block, (1,)).astype(jnp.int32)

    half = HEAD // 2
    inv_freq = ROPE_THETA ** (-jnp.arange(half, dtype=F32) * 2.0 / HEAD)
    ang = jnp.arange(seq, dtype=jnp.int32).astype(F32)[:, None] * inv_freq[None, :]
    ang = ang.reshape(seq // SEGS, SEGS, half).transpose(1, 0, 2).reshape(seq, half)
    cosf = jnp.concatenate([jnp.cos(ang), jnp.cos(ang)], axis=-1)
    sins = jnp.concatenate([-jnp.sin(ang), jnp.sin(ang)], axis=-1)
    band_bias = _band_bias()

    big = {"w_in_ab": w_in_ab, "w_out_ab": w_out_ab, "w_qkv": w_qkv, "w_o": w_o, "w_up": w_up, "w_down": w_down}
    use_order = []
    for l in range(depth):
        use_order += [("w_in_ab", l // 2), ("w_out_ab", l // 2)] if l % 2 == 0 else [("w_qkv", l // 2), ("w_o", l // 2)]
        use_order += [("w_up", l), ("w_down", l)]
    n_even = w_in_ab.shape[0]
    cw_rows = jnp.pad(conv_w.reshape(n_even * CONV_TAPS, conv_w.shape[2]), ((0, HALO - (n_even * CONV_TAPS) % HALO), (0, 0)))
    cw_gathered = _all_gather("ag_conv", [cw_rows])[0]
    first = [k for k in use_order if k in (("w_in_ab", 0), ("w_out_ab", 0), ("w_up", 0), ("w_down", 0))]
    rest = [k for k in use_order if k not in first]
    lands_a, sems_a = _ag_start("ag_start_first", [_cast_fill(f"cast_{nm}_{l}", big[nm], l, block) for nm, l in first],
                                cw_gathered)
    lands_b, sems_b = _ag_start("ag_start_rest", [_cast_fill(f"cast_{nm}_{l}", big[nm], l, block) for nm, l in rest],
                                lands_a[0])
    lands = dict(zip(first + rest, list(lands_a) + list(lands_b)))
    ag_sems = dict(zip(first + rest, list(sems_a) + list(sems_b)))
    passed_on, wg = [], {}

    def weight(key, after):
        pins = []
        if key not in wg:
            upto = min(use_order.index(key) + 2, len(use_order) - 1)
            for k in use_order[len(passed_on):upto + 1]:
                lands[k] = _ag_mid(f"ag_mid_{k[0]}_{k[1]}", lands[k], ag_sems[k], after)
                passed_on.append(k)
                if k != key:
                    pins.append(lands[k])
            wg[key] = _ag_wait(f"ag_wait_{key[0]}_{key[1]}", lands[key], ag_sems[key], after)
        return wg[key], pins

    cw_all = cw_gathered[:, :n_even * CONV_TAPS].reshape(NDEV, n_even, CONV_TAPS, -1)
    cw_all = jnp.transpose(cw_all, (1, 2, 0, 3)).reshape(n_even, CONV_TAPS, -1)
    cw_full = [jnp.pad(cw_all[e], ((0, HALO - CONV_TAPS), (0, 0))) for e in range(n_even)]

    def rows_nat(blk):
        return blk.reshape(blk.shape[0] * blk.shape[1], blk.shape[2])

    saved = []
    hn = _norm_fwd("norm_first", h0, g_pre=norm_mix_pre[0][None])[0]
    h = h0
    for l in range(depth):
        s = {"h_in": h, "hn1": hn}
        if l % 2 == 0:
            e = l // 2
            w_, pins = weight(("w_in_ab", e), hn)
            proj = _mm_nn_blk(f"fwd_in_{l}", hn, w_, after=pins)
            ab = _gate_fwd(f"gate_fwd_{l}", proj, w_spatial[e], b_spatial[e].T, cw_full[e])
            w_, pins = weight(("w_out_ab", e), ab)
            mix = _mm_nn(f"fwd_out_{l}", ab, rows_nat(w_), after=pins)
            s.update(proj=proj, ab=ab)
        else:
            o_ = l // 2
            w_, pins = weight(("w_qkv", o_), hn)
            qkv = _mm_nn_blk(f"fwd_qkv_{l}", hn, w_, after=pins)
            att, lse = _attn_fwd(f"attn_fwd_{l}", qkv, cosf, sins, band_bias)
            w_, pins = weight(("w_o", o_), att)
            mix = _mm_nn(f"fwd_o_{l}", att, rows_nat(w_), after=pins)
            s.update(qkv=qkv, att=att, lse=lse)
        h1, hn2 = _norm_fwd(f"norm_mid_{l}", h, mix, norm_mix_post[l][None], norm_mlp_pre[l][None], seg_z=l % 2 == 1)
        w_, pins = weight(("w_up", l), hn2)
        act = _mm_nn_blk(f"fwd_up_{l}", hn2, w_, relu2=True, after=pins)
        w_, pins = weight(("w_down", l), act)
        f = _mm_nn(f"fwd_down_{l}", act, rows_nat(w_), after=pins)
        s.update(mix=mix, h1=h1, hn2=hn2, act=act, f=f)
        if l + 1 < depth:
            h, hn = _norm_fwd(f"norm_end_{l}", h1, f, norm_mlp_post[l][None], norm_mix_pre[l + 1][None],
                              seg_y=(l + 1) % 2 == 1)
        else:
            h = _norm_fwd(f"norm_end_{l}", h1, f, norm_mlp_post[l][None])[0]
        saved.append(s)

    d_h, loss_row = _loss_grad("loss", h, target)
    rs = {}

    def scatter(key, g):
        rs[key] = _rs_start(f"rs_start_{key[0]}_{key[1]}", g.reshape(NDEV, -1, g.shape[-1]))

    dg ={nm: [None] * depth for nm in ("norm_mix_pre", "norm_mix_post", "norm_mlp_pre", "norm_mlp_post")}
    d_ws, d_bs, d_cw = [None] * n_even, [None] * n_even, [None] * n_even
    d_hn_next = None
    for l in reversed(range(depth)):
        s = saved[l]
        if l == depth - 1:
            d_f, dg["norm_mlp_post"][l] = _norm_bwd(f"nb_end_{l}", d_h, post=(s["f"], norm_mlp_post[l][None]))
        else:
            d_h, dg["norm_mix_pre"][l + 1], d_f, dg["norm_mlp_post"][l] = _norm_bwd(
                f"nb_end_{l}", d_h, pre=(d_hn_next, saved[l + 1]["h_in"], norm_mix_pre[l + 1][None]),
                post=(s["f"], norm_mlp_post[l][None]), seg_dy=(l + 1) % 2 == 1)
        wd = rows_nat(wg[("w_down", l)])
        d_up = _mm_nt_rows(f"bwd_down_{l}", d_f, wd, act=s["act"])
        scatter(("w_down", l), _mm_tn(f"gw_down_{l}", s["act"], d_f))
        scatter(("w_up", l), _mm_tn(f"gw_up_{l}", s["hn2"], d_up, nb=w_up.shape[2]))
        d_hn2 = _mm_nt_blk(f"bwd_up_{l}", d_up, wg[("w_up", l)], after=[rs[("w_down", l)][0], rs[("w_up", l)][0]])
        d_h, dg["norm_mlp_pre"][l], d_mix, dg["norm_mix_post"][l] = _norm_bwd(
            f"nb_mid_{l}", d_h, pre=(d_hn2, s["h1"], norm_mlp_pre[l][None]),
            post=(s["mix"], norm_mix_post[l][None]), seg_z=l % 2 == 1)
        if l % 2 == 0:
            e = l // 2
            wo = rows_nat(wg[("w_out_ab", e)])
            d_ab = _mm_nt_rows(f"bwd_out_{l}", d_mix, wo)
            scatter(("w_out_ab", e), _mm_tn(f"gw_out_{l}", s["ab"], d_mix))
            d_proj, d_ws[e], d_bs[e], d_cw[e] = _gate_bwd(
                f"gate_bwd_{l}", s["proj"], d_ab, w_spatial[e], b_spatial[e].T, cw_full[e])
            scatter(("w_in_ab", e), _mm_tn(f"gw_in_{l}", s["hn1"], d_proj, nb=w_in_ab.shape[2]))
            d_hn_next = _mm_nt_blk(f"bwd_in_{l}", d_proj, wg[("w_in_ab", e)],
                                   after=[rs[("w_out_ab", e)][0], rs[("w_in_ab", e)][0]])
        else:
            o_ = l // 2
            wo = rows_nat(wg[("w_o", o_)])
            d_att = _mm_nt_rows(f"bwd_o_{l}", d_mix, wo, out_dtype=F32)
            scatter(("w_o", o_), _mm_tn(f"gw_o_{l}", s["att"], d_mix))
            parts = _attn_bwd(f"attn_bwd_{l}", s["qkv"], cosf, sins, band_bias, d_att, s["att"], s["lse"])
            d_qkv = _attn_merge(f"attn_merge_{l}", *parts)
            scatter(("w_qkv", o_), _mm_tn(f"gw_qkv_{l}", s["hn1"], d_qkv, nb=w_qkv.shape[2]))
            d_hn_next = _mm_nt_blk(f"bwd_qkv_{l}", d_qkv, wg[("w_qkv", o_)],
                                   after=[rs[("w_o", o_)][0], rs[("w_qkv", o_)][0]])
    grad_x, dg["norm_mix_pre"][0] = _norm_bwd("nb_first", d_h, pre=(d_hn_next, h0, norm_mix_pre[0][None]))

    small_g = ([jnp.concatenate(dg[nm], axis=0) for nm in dg]
               + [jnp.stack(d_ws), jnp.stack(d_bs), jnp.stack([c[:CONV_TAPS] for c in d_cw]), loss_row])
    small_land, small_send, small_recv = _ag_direct_start("ag_small_start", _fill_slot("fill_small", _pack(small_g), block))

    moments = {"w_in_ab": (m_w_in_ab, v_w_in_ab), "w_out_ab": (m_w_out_ab, v_w_out_ab), "w_qkv": (m_w_qkv, v_w_qkv),
               "w_o": (m_w_o, v_w_o), "w_up": (m_w_up, v_w_up), "w_down": (m_w_down, v_w_down)}
    out_big = {}
    behind = small_land
    for nm in ("w_o", "w_qkv", "w_down", "w_up", "w_out_ab", "w_in_ab"):
        own, landed = [], []
        for l in range(big[nm].shape[0]):
            g, land = _wait_all(f"rs_wait_{nm}_{l}", *rs[(nm, l)], behind)
            own.append(g)
            landed.append(land)
        out_big[nm] = _adamw_layers(f"adamw_{nm}", own, landed, block, big[nm], moments[nm][0], moments[nm][1])
        behind = out_big[nm][0]

    summed = _sum_parts("sum_small", _ag_direct_wait("ag_small_wait", small_land, small_send, small_recv, behind))
    g_nmp, g_nmo, g_nlp, g_nlo, g_ws, g_bs, g_cw_all, loss_sum = _unpack(summed, small_g)
    loss = loss_sum[0, 0]
    cwb = conv_w.shape[2]
    g_cw = lax.dynamic_slice_in_dim(g_cw_all, my_block * cwb, cwb, axis=2)
    small_w = [norm_mix_pre, norm_mix_post, norm_mlp_pre, norm_mlp_post, w_spatial, b_spatial, conv_w]
    small_m = [m_norm_mix_pre, m_norm_mix_post, m_norm_mlp_pre, m_norm_mlp_post, m_w_spatial, m_b_spatial, m_conv_w]
    small_v = [v_norm_mix_pre, v_norm_mix_post, v_norm_mlp_pre, v_norm_mlp_post, v_w_spatial, v_b_spatial, v_conv_w]
    small_grad = [g_nmp, g_nmo, g_nlp, g_nlo, g_ws, g_bs, g_cw]
    upd = _adamw("adamw_small", _pack(small_grad)[None], _pack(small_w), _pack(small_m), _pack(small_v))
    sg, sd, sm, sv = [_unpack(u, small_w) for u in upd]

    def outs(i_small, i_big):
        return (i_small[0], i_small[1], i_small[2], i_small[3], i_big["w_in_ab"], i_small[4], i_small[5], i_small[6],
                i_big["w_out_ab"], i_big["w_qkv"], i_big["w_o"], i_big["w_up"], i_big["w_down"])

    pick = lambda i: {nm: out_big[nm][i] for nm in big}
    return (loss, grad_x.reshape(x.shape), *outs(sg, pick(0)), *outs(sd, pick(1)), *outs(sm, pick(2)),
            *outs(sv, pick(3)))
```

```python
import functools

import jax
import jax.numpy as jnp
from jax import lax
from jax.experimental import pallas as pl
from jax.experimental.pallas import tpu as pltpu

F32 = jnp.float32
BF16 = jnp.bfloat16
MESH = pl.DeviceIdType.MESH
ANY = pl.BlockSpec(memory_space=pl.ANY)
HBM = pl.BlockSpec(memory_space=pltpu.HBM)
SEM = pl.BlockSpec(memory_space=pltpu.SEMAPHORE)
EFFECT = pltpu.SideEffectType.DATAFLOW_SIDE_EFFECTING

NDEV = 8
NCHIP = 4
RMS_EPS = 1e-6
LN_EPS = 1e-5
CHUNK = 128
HEAD = 128
ATT_TILE = 2048
ATT_UNROLL = 4
DILATIONS = (1, 4, 16)
SEGS = 16
ROPE_THETA = 10000.0
CONV_TAPS = 3
HALO = 8
HALO_BF16 = 16
GELU_C = 0.7978845608028654
GELU_A = 0.044715
ADAM_LR, ADAM_B1, ADAM_B2, ADAM_EPS, ADAM_WD, ADAM_STEP = 0.001, 0.9, 0.999, 1e-08, 0.01, 10
VMEM_MM = 52 << 20
VMEM_EW = 40 << 20


def _cp(sem=None, vmem=VMEM_EW):
    if sem is None:
        return pltpu.CompilerParams(vmem_limit_bytes=vmem)
    return pltpu.CompilerParams(dimension_semantics=sem, vmem_limit_bytes=vmem)


def _tile(n, want):
    return want if n % want == 0 else n


def _all_gather(name, shards, after=()):
    n = len(shards)
    after = list(after)

    def body(*refs):
        ins, outs = refs[:n], refs[n + len(after):2 * n + len(after)]
        send_sems, recv_sems, local_sems = refs[2 * n + len(after):]
        x, y, c = lax.axis_index("x"), lax.axis_index("y"), lax.axis_index("c")
        me, sibling = (x, y, c), (x, y, 1 - c)
        chips = [(1 - x, y), (x, 1 - y), (1 - x, 1 - y)]

        def slot(p):
            return 4 * p[0] + 2 * p[1] + p[2]

        def copy(i, k, block, to, src=None):
            dst = outs[i].at[slot(block)]
            return pltpu.make_async_remote_copy(
                src_ref=dst if src is None else src, dst_ref=dst,
                send_sem=send_sems.at[i, k], recv_sem=recv_sems.at[i, k],
                device_id=to, device_id_type=MESH)

        mine = [pltpu.make_async_copy(ins[i], outs[i].at[slot(me)], local_sems.at[i]) for i in range(n)]
        for cp in mine:
            cp.start()
        first = []
        for i in range(n):
            first.append(copy(i, 0, me, sibling, src=ins[i]))
            for j, chip in enumerate(chips):
                first.append(copy(i, 1 + j, me, (*chip, c), src=ins[i]))
        for cp in first:
            cp.start()
        passed = []
        for j, chip in enumerate(chips):
            for i in range(n):
                copy(i, 1 + j, (*chip, c), me).wait_recv()
                fwd = copy(i, 4 + j, (*chip, c), sibling)
                fwd.start()
                passed.append(fwd)
        for i in range(n):
            copy(i, 0, sibling, me).wait_recv()
            for j, chip in enumerate(chips):
                copy(i, 4 + j, (*chip, 1 - c), me).wait_recv()
        for cp in first + passed:
            cp.wait_send()
        for cp in mine:
            cp.wait()

    return pl.pallas_call(
        body, name=name,
        out_shape=[jax.ShapeDtypeStruct((NDEV,) + s.shape, s.dtype) for s in shards],
        in_specs=[ANY] * (n + len(after)), out_specs=[ANY] * n,
        scratch_shapes=[pltpu.SemaphoreType.DMA((n, 7)), pltpu.SemaphoreType.DMA((n, 7)),
                        pltpu.SemaphoreType.DMA((n,))],
    )(*shards, *after)


def _peer(x, y, c, r):
    return (1 - x if r & 4 else x, 1 - y if r & 2 else y, 1 - c if r & 1 else c)


def _slot(p):
    return 4 * p[0] + 2 * p[1] + p[2]


def _cast_fill(name, w, layer, block):
    _, rows, cols = w.shape
    tr = _tile(rows, 256)

    def body(blk_ref, w_ref, o_ref):
        o_ref[...] = w_ref[...].astype(BF16)

    return pl.pallas_call(
        body, name=name,
        grid_spec=pltpu.PrefetchScalarGridSpec(
            num_scalar_prefetch=1, grid=(rows // tr,),
            in_specs=[pl.BlockSpec((None, tr, cols), lambda i, blk: (layer, i, 0))],
            out_specs=pl.BlockSpec((None, tr, cols), lambda i, blk: (blk[0], i, 0))),
        out_shape=jax.ShapeDtypeStruct((NDEV, rows, cols), BF16),
        compiler_params=_cp(("parallel",)))(block, w)


OTHER_CHIPS = (2, 4, 6)
AG_SEMS = 6
AG_PIN_FROM = 5


def _ag_start(name, lands, after):
    n = len(lands)

    def body(*refs):
        ins, sems = refs[:n], refs[n + 1:n + 1 + AG_SEMS * n]
        x, y, c = lax.axis_index("x"), lax.axis_index("y"), lax.axis_index("c")
        mine = _slot((x, y, c))
        for i in range(n):
            send_a, *recv_a, _, recv_b = sems[AG_SEMS * i:AG_SEMS * (i + 1)]
            block = ins[i].at[mine]
            pltpu.make_async_remote_copy(src_ref=block, dst_ref=block, send_sem=send_a, recv_sem=recv_b,
                                         device_id=_peer(x, y, c, 1), device_id_type=MESH).start()
            for k, r in enumerate(OTHER_CHIPS):
                pltpu.make_async_remote_copy(src_ref=block, dst_ref=block, send_sem=send_a, recv_sem=recv_a[k],
                                             device_id=_peer(x, y, c, r), device_id_type=MESH).start()

    outs = pl.pallas_call(
        body, name=name,
        out_shape=[pltpu.SemaphoreType.DMA(())] * (AG_SEMS * n) + [pltpu.HBM(a.shape, a.dtype) for a in lands],
        in_specs=[HBM] * n + [ANY], out_specs=[SEM] * (AG_SEMS * n) + [HBM] * n,
        input_output_aliases={i: AG_SEMS * n + i for i in range(n)},
        compiler_params=pltpu.CompilerParams(has_side_effects=EFFECT),
    )(*[pltpu.with_memory_space_constraint(a, pltpu.HBM) for a in lands], after)
    return outs[AG_SEMS * n:], [tuple(outs[AG_SEMS * i:AG_SEMS * (i + 1)]) for i in range(n)]


def _ag_mid(name, land, sems, after):
    _, *recv_a, send_b, recv_b = sems

    def body(land_ref, ra0, ra1, ra2, send_b_ref, recv_b_ref, after_ref, land_out):
        x, y, c = lax.axis_index("x"), lax.axis_index("y"), lax.axis_index("c")
        sibling = _peer(x, y, c, 1)
        for arrival, r in zip((ra0, ra1, ra2), OTHER_CHIPS):
            block = land_ref.at[_slot(_peer(x, y, c, r))]
            pltpu.make_async_remote_copy(src_ref=block, dst_ref=block, send_sem=send_b_ref, recv_sem=arrival,
                                         device_id=sibling, device_id_type=MESH).wait_recv()
            pltpu.make_async_remote_copy(src_ref=block, dst_ref=block, send_sem=send_b_ref, recv_sem=recv_b_ref,
                                         device_id=sibling, device_id_type=MESH).start()

    return pl.pallas_call(
        body, name=name, out_shape=pltpu.HBM(land.shape, land.dtype),
        in_specs=[HBM] + [SEM] * 5 + [ANY], out_specs=HBM, input_output_aliases={0: 0},
        compiler_params=pltpu.CompilerParams(has_side_effects=EFFECT),
    )(land, *recv_a, send_b, recv_b, after)


def _ag_wait(name, land, sems, after):
    send_a, _, _, _, send_b, recv_b = sems

    def body(land_ref, send_a_ref, send_b_ref, recv_b_ref, after_ref, land_out):
        x, y, c = lax.axis_index("x"), lax.axis_index("y"), lax.axis_index("c")
        sibling = _peer(x, y, c, 1)
        four = land_ref.at[pl.ds(0, 1 + len(OTHER_CHIPS))]
        three = land_ref.at[pl.ds(0, len(OTHER_CHIPS))]
        first = pltpu.make_async_remote_copy(src_ref=four, dst_ref=four, send_sem=send_a_ref, recv_sem=recv_b_ref,
                                             device_id=sibling, device_id_type=MESH)
        passed = pltpu.make_async_remote_copy(src_ref=three, dst_ref=three, send_sem=send_b_ref, recv_sem=recv_b_ref,
                                              device_id=sibling, device_id_type=MESH)
        first.wait_send()
        passed.wait_send()
        first.wait_recv()

    return pl.pallas_call(
        body, name=name, out_shape=pltpu.HBM(land.shape, land.dtype),
        in_specs=[HBM, SEM, SEM, SEM, ANY], out_specs=HBM, input_output_aliases={0: 0},
        compiler_params=pltpu.CompilerParams(has_side_effects=EFFECT),
    )(land, send_a, send_b, recv_b, after)


def _fill_slot(name, rows, block):
    r, c = rows.shape

    def body(blk_ref, i_ref, o_ref):
        o_ref[...] = i_ref[...]

    return pl.pallas_call(
        body, name=name,
        grid_spec=pltpu.PrefetchScalarGridSpec(
            num_scalar_prefetch=1, grid=(1,),
            in_specs=[pl.BlockSpec((r, c), lambda i, blk: (0, 0))],
            out_specs=pl.BlockSpec((None, r, c), lambda i, blk: (blk[0], 0, 0))),
        out_shape=jax.ShapeDtypeStruct((NDEV, r, c), rows.dtype),
        compiler_params=_cp(("arbitrary",)))(block, rows)


def _ag_direct_start(name, land):
    def body(land_ref, send, recv, land_out):
        x, y, c = lax.axis_index("x"), lax.axis_index("y"), lax.axis_index("c")
        block = land_ref.at[_slot((x, y, c))]
        for r in range(1, NDEV):
            pltpu.make_async_remote_copy(src_ref=block, dst_ref=block, send_sem=send, recv_sem=recv,
                                         device_id=_peer(x, y, c, r), device_id_type=MESH).start()

    send, recv, land_thru = pl.pallas_call(
        body, name=name,
        out_shape=[pltpu.SemaphoreType.DMA(()), pltpu.SemaphoreType.DMA(()), pltpu.HBM(land.shape, land.dtype)],
        in_specs=[HBM], out_specs=[SEM, SEM, HBM], input_output_aliases={0: 2},
        compiler_params=pltpu.CompilerParams(has_side_effects=EFFECT),
    )(pltpu.with_memory_space_constraint(land, pltpu.HBM))
    return land_thru, send, recv


def _ag_direct_wait(name, land, send, recv, after):
    def body(land_ref, send_ref, recv_ref, after_ref, land_out):
        x, y, c = lax.axis_index("x"), lax.axis_index("y"), lax.axis_index("c")
        seven = land_ref.at[pl.ds(0, NDEV - 1)]
        copy = pltpu.make_async_remote_copy(src_ref=seven, dst_ref=seven, send_sem=send_ref, recv_sem=recv_ref,
                                            device_id=_peer(x, y, c, 1), device_id_type=MESH)
        copy.wait_send()
        copy.wait_recv()

    return pl.pallas_call(
        body, name=name, out_shape=pltpu.HBM(land.shape, land.dtype),
        in_specs=[HBM, SEM, SEM, ANY], out_specs=HBM, input_output_aliases={0: 0},
        compiler_params=pltpu.CompilerParams(has_side_effects=EFFECT),
    )(land, send, recv, after)


def _wait_all(name, src, land, send, recv, after):
    def body(src_ref, land_ref, send_ref, recv_ref, after_ref, src_out, land_out):
        x, y, c = lax.axis_index("x"), lax.axis_index("y"), lax.axis_index("c")
        seven = land_ref.at[pl.ds(0, NDEV - 1)]
        copy = pltpu.make_async_remote_copy(src_ref=seven, dst_ref=seven, send_sem=send_ref, recv_sem=recv_ref,
                                            device_id=_peer(x, y, c, 1), device_id_type=MESH)
        copy.wait_send()
        copy.wait_recv()

    return pl.pallas_call(
        body, name=name,
        out_shape=[pltpu.HBM(src.shape, src.dtype), pltpu.HBM(land.shape, land.dtype)],
        in_specs=[HBM, HBM, SEM, SEM, ANY], out_specs=[HBM, HBM],
        input_output_aliases={0: 0, 1: 1},
        compiler_params=pltpu.CompilerParams(has_side_effects=EFFECT),
    )(src, land, send, recv, after)


def _rs_start(name, grad):
    land = lax.empty((NDEV - 1,) + grad.shape[1:], grad.dtype)

    def body(g_ref, land_ref, send, recv, g_out, land_out):
        x, y, c = lax.axis_index("x"), lax.axis_index("y"), lax.axis_index("c")
        for r in range(1, NDEV):
            peer = _peer(x, y, c, r)
            pltpu.make_async_remote_copy(
                src_ref=g_ref.at[_slot(peer)], dst_ref=land_ref.at[r - 1], send_sem=send, recv_sem=recv,
                device_id=peer, device_id_type=MESH).start()

    send, recv, g_thru, land_thru = pl.pallas_call(
        body, name=name,
        out_shape=[pltpu.SemaphoreType.DMA(()), pltpu.SemaphoreType.DMA(()),
                   pltpu.HBM(grad.shape, grad.dtype), pltpu.HBM(land.shape, land.dtype)],
        in_specs=[HBM, HBM], out_specs=[SEM, SEM, HBM, HBM], input_output_aliases={0: 2, 1: 3},
        compiler_params=pltpu.CompilerParams(has_side_effects=EFFECT),
    )(pltpu.with_memory_space_constraint(grad, pltpu.HBM), pltpu.with_memory_space_constraint(land, pltpu.HBM))
    return g_thru, land_thru, send, recv


def _adam_math(w, g, m, v):
    m = ADAM_B1 * m + (1.0 - ADAM_B1) * g
    v = ADAM_B2 * v + (1.0 - ADAM_B2) * (g * g)
    m_hat = m / (1.0 - ADAM_B1 ** ADAM_STEP)
    v_hat = v / (1.0 - ADAM_B2 ** ADAM_STEP)
    delta = -ADAM_LR * (m_hat / (jnp.sqrt(v_hat) + ADAM_EPS) + ADAM_WD * w)
    return delta, m, v


def _adamw(name, parts, w, m, v):
    nparts, rows, cols = parts.shape
    tr = _tile(rows, 256)

    def body(p_ref, w_ref, m_ref, v_ref, g_out, d_out, m_out, v_out):
        g = p_ref[0].astype(F32)
        for k in range(1, nparts):
            g = g + p_ref[k].astype(F32)
        delta, mn, vn = _adam_math(w_ref[...], g, m_ref[...], v_ref[...])
        g_out[...] = g
        d_out[...] = delta
        m_out[...] = mn
        v_out[...] = vn

    row = pl.BlockSpec((tr, cols), lambda i: (i, 0))
    return pl.pallas_call(
        body, name=name, grid=(rows // tr,),
        in_specs=[pl.BlockSpec((nparts, tr, cols), lambda i: (0, i, 0)), row, row, row],
        out_specs=[row] * 4,
        out_shape=[jax.ShapeDtypeStruct((rows, cols), F32)] * 4,
        compiler_params=_cp(("parallel",)),
    )(parts, w, m, v)


def _adamw_layers(name, grads, lands, block, w, m, v):
    layers, rows, cols = w.shape
    nland = lands[0].shape[0]
    tr = rows
    while tr % 2 == 0 and tr > 8 and nland * tr * cols * 2 > (2 << 20):
        tr //= 2

    def body(blk_ref, *refs):
        own_refs, land_refs = refs[:layers], refs[layers:2 * layers]
        w_ref, m_ref, v_ref, g_out, d_out, m_out, v_out = refs[2 * layers:]
        layer = pl.program_id(0)
        for k in range(layers):
            @pl.when(layer == k)
            def _(k=k):
                g = own_refs[k][...].astype(F32)
                for s in range(nland):
                    g = g + land_refs[k][s].astype(F32)
                delta, mn, vn = _adam_math(w_ref[...], g, m_ref[...], v_ref[...])
                g_out[...] = g
                d_out[...] = delta
                m_out[...] = mn
                v_out[...] = vn

    def own_spec(k):
        return pl.BlockSpec((None, tr, cols), lambda l, i, blk: (blk[0], jnp.where(l == k, i, 0), 0))

    def land_spec(k):
        return pl.BlockSpec((nland, tr, cols), lambda l, i, blk: (0, jnp.where(l == k, i, 0), 0))

    row = pl.BlockSpec((None, tr, cols), lambda l, i, blk: (l, i, 0))
    return pl.pallas_call(
        body, name=name,
        grid_spec=pltpu.PrefetchScalarGridSpec(
            num_scalar_prefetch=1, grid=(layers, rows // tr),
            in_specs=[own_spec(k) for k in range(layers)] + [land_spec(k) for k in range(layers)] + [row, row, row],
            out_specs=[row] * 4),
        out_shape=[jax.ShapeDtypeStruct((layers, rows, cols), F32)] * 4,
        compiler_params=_cp(("arbitrary", "arbitrary")),
    )(block, *grads, *lands, w, m, v)


LANES = 128


def _seg_scratch(rows, d):
    return pltpu.VMEM((d // LANES, rows, LANES), F32)


def _to_segments(vals, scratch, out_ref):
    per = scratch.shape[1] // SEGS
    for c in range(scratch.shape[0]):
        cols = slice(c * LANES, (c + 1) * LANES)
        scratch[c] = vals[:, cols]
        for s in range(SEGS):
            out_ref[s, :, cols] = scratch.at[c][pl.ds(s, per, stride=SEGS), :].astype(out_ref.dtype)


def _from_segments(in_ref, scratch):
    per = scratch.shape[1] // SEGS
    for c in range(scratch.shape[0]):
        for s in range(SEGS):
            scratch.at[c][pl.ds(s, per, stride=SEGS), :] = in_ref[s, :, c * LANES:(c + 1) * LANES].astype(F32)
    return jnp.concatenate([scratch[c] for c in range(scratch.shape[0])], axis=1)


def _seg_view(a):
    return a.reshape(SEGS, a.shape[0] // SEGS, a.shape[1])


def _norm_fwd(name, h, z=None, g_post=None, g_pre=None, seg_z=False, seg_y=False):
    rows, d = h.shape
    tm = _tile(rows, 256)
    has_post, has_pre = z is not None, g_pre is not None
    nscratch = int(seg_z) + int(seg_y)

    def body(*refs):
        scratch = list(refs[len(refs) - nscratch:])
        it = iter(refs)
        hv = next(it)[...]
        if has_post:
            z_ref = next(it)
            zv = _from_segments(z_ref, scratch.pop(0)) if seg_z else z_ref[...].astype(F32)
            gp = next(it)[...]
        if has_pre:
            gq = next(it)[...]
        if has_post:
            r = lax.rsqrt(jnp.mean(zv * zv, axis=-1, keepdims=True) + RMS_EPS)
            hv = hv + (zv * r) * gp
            next(it)[...] = hv
        if has_pre:
            r = lax.rsqrt(jnp.mean(hv * hv, axis=-1, keepdims=True) + RMS_EPS)
            y = (hv * r) * gq
            if seg_y:
                _to_segments(y, scratch.pop(0), next(it))
            else:
                next(it)[...] = y.astype(BF16)

    row = pl.BlockSpec((tm, d), lambda i: (i, 0))
    seg = pl.BlockSpec((SEGS, tm // SEGS, d), lambda i: (0, i, 0))
    vec = pl.BlockSpec((1, d), lambda i: (0, 0))
    ins, in_specs, out_shape, out_specs = [h], [row], [], []
    if has_post:
        ins += [_seg_view(z) if seg_z else z, g_post]
        in_specs += [seg if seg_z else row, vec]
        out_shape.append(jax.ShapeDtypeStruct((rows, d), F32))
        out_specs.append(row)
    if has_pre:
        ins.append(g_pre)
        in_specs.append(vec)
        out_shape.append(jax.ShapeDtypeStruct((SEGS, rows // SEGS, d) if seg_y else (rows, d), BF16))
        out_specs.append(seg if seg_y else row)
    outs = pl.pallas_call(body, name=name, grid=(rows // tm,), in_specs=in_specs, out_specs=out_specs,
                          out_shape=out_shape, scratch_shapes=[_seg_scratch(tm, d)] * nscratch,
                          compiler_params=_cp(("parallel",)))(*ins)
    if seg_y:
        outs = list(outs[:-1]) + [outs[-1].reshape(rows, d)]
    return outs


def _rms_bwd_rows(x, g, dy):
    r = lax.rsqrt(jnp.mean(x * x, axis=-1, keepdims=True) + RMS_EPS)
    xn = x * r
    dg = jnp.sum(dy * xn, axis=0, keepdims=True)
    dxn = dy * g
    dx = r * (dxn - xn * jnp.mean(dxn * xn, axis=-1, keepdims=True))
    return dx, dg


def _norm_bwd(name, d_out, pre=None, post=None, seg_dy=False, seg_z=False):
    rows, d = d_out.shape
    tm = _tile(rows, 256)
    has_pre, has_post = pre is not None, post is not None
    nscratch = int(seg_dy) + 2 * int(seg_z)

    def body(*refs):
        scratch = list(refs[len(refs) - nscratch:])
        it = iter(refs)
        dres = next(it)[...]
        if has_pre:
            dy_ref = next(it)
            dy = _from_segments(dy_ref, scratch.pop(0)) if seg_dy else dy_ref[...].astype(F32)
            xp, gq = next(it)[...], next(it)[...]
        if has_post:
            z_ref = next(it)
            zv = _from_segments(z_ref, scratch.pop(0)) if seg_z else z_ref[...].astype(F32)
            gp = next(it)[...]
        first = pl.program_id(0) == 0
        if has_pre:
            dx, dg = _rms_bwd_rows(xp, gq, dy)
            dres = dres + dx
            next(it)[...] = dres
            dg_ref = next(it)

            @pl.when(first)
            def _():
                dg_ref[...] = jnp.zeros_like(dg_ref)
            dg_ref[...] += dg
        if has_post:
            dz, dg2 = _rms_bwd_rows(zv, gp, dres)
            if seg_z:
                _to_segments(dz, scratch.pop(0), next(it))
            else:
                next(it)[...] = dz.astype(BF16)
            dg2_ref = next(it)

            @pl.when(first)
            def _():
                dg2_ref[...] = jnp.zeros_like(dg2_ref)
            dg2_ref[...] += dg2

    row = pl.BlockSpec((tm, d), lambda i: (i, 0))
    seg = pl.BlockSpec((SEGS, tm // SEGS, d), lambda i: (0, i, 0))
    vec = pl.BlockSpec((1, d), lambda i: (0, 0))
    ins, in_specs, out_shape, out_specs = [d_out], [row], [], []
    if has_pre:
        d_y, x_pre, g_pre = pre
        ins += [_seg_view(d_y) if seg_dy else d_y, x_pre, g_pre]
        in_specs += [seg if seg_dy else row, row, vec]
        out_shape += [jax.ShapeDtypeStruct((rows, d), F32), jax.ShapeDtypeStruct((1, d), F32)]
        out_specs += [row, vec]
    if has_post:
        z, g_post = post
        ins += [_seg_view(z) if seg_z else z, g_post]
        in_specs += [seg if seg_z else row, vec]
        out_shape += [jax.ShapeDtypeStruct((SEGS, rows // SEGS, d) if seg_z else (rows, d), BF16),
                      jax.ShapeDtypeStruct((1, d), F32)]
        out_specs += [seg if seg_z else row, vec]
    outs = pl.pallas_call(body, name=name, grid=(rows // tm,), in_specs=in_specs, out_specs=out_specs,
                          out_shape=out_shape, scratch_shapes=[_seg_scratch(tm, d)] * nscratch,
                          compiler_params=_cp(("arbitrary",)))(*ins)
    if seg_z:
        outs = list(outs)
        outs[-2] = outs[-2].reshape(rows, d)
    return outs


def _loss_grad(name, y, target):
    rows, d = y.shape
    tm = _tile(rows, 256)

    def body(y_ref, t_ref, dy_ref, loss_ref):
        err = y_ref[...] - t_ref[...]
        dy_ref[...] = err * (1.0 / d)

        @pl.when(pl.program_id(0) == 0)
        def _():
            loss_ref[...] = jnp.zeros_like(loss_ref)
        loss_ref[...] += jnp.full(loss_ref.shape, (0.5 / d) * jnp.sum(err * err), F32)

    row = pl.BlockSpec((tm, d), lambda i: (i, 0))
    return pl.pallas_call(
        body, name=name, grid=(rows // tm,), in_specs=[row, row],
        out_specs=[row, pl.BlockSpec((1, 128), lambda i: (0, 0))],
        out_shape=[jax.ShapeDtypeStruct((rows, d), F32), jax.ShapeDtypeStruct((1, 128), F32)],
        compiler_params=_cp(("arbitrary",)))(y, target)


NT_DIMS = (((1,), (1,)), ((), ()))
TN_DIMS = (((0,), (0,)), ((), ()))


def _mm_nn_blk(name, a, wblk, relu2=False, after=()):
    m, k = a.shape
    nb = wblk.shape[2]
    tm = _tile(m, 1024)
    after = list(after)

    def body(a_ref, w_ref, *rest):
        r = jnp.dot(a_ref[...], w_ref[...], preferred_element_type=F32)
        if relu2:
            rr = jnp.maximum(r, 0.0)
            r = rr * rr
        rest[-1][...] = r.astype(BF16)

    return pl.pallas_call(
        body, name=name, grid=(NDEV, m // tm),
        in_specs=[pl.BlockSpec((tm, k), lambda d, i: (i, 0)), pl.BlockSpec((None, k, nb), lambda d, i: (d, 0, 0))]
        + [ANY] * len(after),
        out_specs=pl.BlockSpec((tm, nb), lambda d, i: (i, d)),
        out_shape=jax.ShapeDtypeStruct((m, NDEV * nb), BF16),
        compiler_params=_cp(("parallel", "parallel"), VMEM_MM))(a, wblk, *after)


def _accumulate(acc, o_ref, r, step, last):
    if acc is None:
        o_ref[...] = r.astype(o_ref.dtype)
        return

    @pl.when(step == 0)
    def _():
        acc[...] = r

    @pl.when(jnp.logical_and(step > 0, step < last))
    def _():
        acc[...] += r

    @pl.when(jnp.logical_and(step > 0, step == last))
    def _():
        o_ref[...] = (acc[...] + r).astype(o_ref.dtype)


def _mm_nn(name, a, w, after=()):
    m, kb = a.shape
    n = w.shape[1]
    tm = _tile(m, 512)
    one_step = kb <= 2048
    tk = kb if one_step else _tile(kb, 4096)
    tn = n if one_step else _tile(n, 1024)
    steps = kb // tk
    after = list(after)

    def body(a_ref, w_ref, *rest):
        o_ref, scratch = rest[len(after)], rest[len(after) + 1:]
        r = jnp.dot(a_ref[...], w_ref[...], preferred_element_type=F32)
        _accumulate(scratch[0] if scratch else None, o_ref, r, pl.program_id(2), steps - 1)

    return pl.pallas_call(
        body, name=name, grid=(m // tm, n // tn, steps),
        in_specs=[pl.BlockSpec((tm, tk), lambda i, j, s: (i, s)), pl.BlockSpec((tk, tn), lambda i, j, s: (s, j))]
        + [ANY] * len(after),
        out_specs=pl.BlockSpec((tm, tn), lambda i, j, s: (i, j)),
        out_shape=jax.ShapeDtypeStruct((m, n), BF16),
        scratch_shapes=[pltpu.VMEM((tm, tn), F32)] if steps > 1 else [],
        compiler_params=_cp(("parallel", "parallel", "arbitrary"), VMEM_MM))(a, w, *after)


def _mm_nt_rows(name, dy, w, act=None, out_dtype=BF16):
    m, n = dy.shape
    kw = w.shape[0]
    tm, tkw = _tile(m, 1024), _tile(kw, 1024)

    def body(dy_ref, w_ref, *rest):
        r = lax.dot_general(dy_ref[...], w_ref[...], NT_DIMS, preferred_element_type=F32)
        if act is None:
            rest[0][...] = r.astype(out_dtype)
        else:
            rest[1][...] = (r * (2.0 * jnp.sqrt(rest[0][...].astype(F32)))).astype(BF16)

    ins = [dy, w]
    in_specs = [pl.BlockSpec((tm, n), lambda j, i: (i, 0)), pl.BlockSpec((tkw, n), lambda j, i: (j, 0))]
    if act is not None:
        ins.append(act)
        in_specs.append(pl.BlockSpec((tm, tkw), lambda j, i: (i, j)))
    return pl.pallas_call(
        body, name=name, grid=(kw // tkw, m // tm), in_specs=in_specs,
        out_specs=pl.BlockSpec((tm, tkw), lambda j, i: (i, j)),
        out_shape=jax.ShapeDtypeStruct((m, kw), out_dtype if act is None else BF16),
        compiler_params=_cp(("parallel", "parallel"), VMEM_MM))(*ins)


def _mm_nt_blk(name, dy, wblk, after=None):
    m = dy.shape[0]
    _, kw, nb = wblk.shape
    tm, tkw, per = _tile(m, 512), _tile(kw, 1024), 4

    extra = list(after or ())

    def body(dy_ref, w_ref, *rest):
        o_ref, acc = rest[len(extra):]
        r = lax.dot_general(dy_ref[:, :nb], w_ref[0], NT_DIMS, preferred_element_type=F32)
        for t in range(1, per):
            r = r + lax.dot_general(dy_ref[:, t * nb:(t + 1) * nb], w_ref[t], NT_DIMS, preferred_element_type=F32)
        _accumulate(acc, o_ref, r, pl.program_id(2), NDEV // per - 1)

    return pl.pallas_call(
        body, name=name, grid=(m // tm, kw // tkw, NDEV // per),
        in_specs=[pl.BlockSpec((tm, per * nb), lambda i, j, s: (i, s)),
                  pl.BlockSpec((per, tkw, nb), lambda i, j, s: (s, j, 0))] + [ANY] * len(extra),
        out_specs=pl.BlockSpec((tm, tkw), lambda i, j, s: (i, j)),
        out_shape=jax.ShapeDtypeStruct((m, kw), BF16),
        scratch_shapes=[pltpu.VMEM((tm, tkw), F32)],
        compiler_params=_cp(("parallel", "parallel", "arbitrary"), VMEM_MM))(dy, wblk, *extra)


def _mm_tn(name, x, dy, nb=None):
    t, mx = x.shape
    n = dy.shape[1]
    tmx = _tile(mx, 512)
    tn = nb if nb is not None else _tile(n, 1024)

    def body(x_ref, dy_ref, o_ref):
        o_ref[...] = lax.dot_general(x_ref[...], dy_ref[...], TN_DIMS, preferred_element_type=F32).astype(BF16)

    if nb is None:
        out_shape = jax.ShapeDtypeStruct((mx, n), BF16)
        out_spec = pl.BlockSpec((tmx, tn), lambda j, i: (i, j))
    else:
        out_shape = jax.ShapeDtypeStruct((NDEV, mx, nb), BF16)
        out_spec = pl.BlockSpec((None, tmx, nb), lambda j, i: (j, i, 0))
    return pl.pallas_call(
        body, name=name, grid=(n // tn, mx // tmx),
        in_specs=[pl.BlockSpec((t, tmx), lambda j, i: (0, i)), pl.BlockSpec((t, tn), lambda j, i: (0, j))],
        out_specs=out_spec, out_shape=out_shape,
        compiler_params=_cp(("parallel", "parallel"), VMEM_MM))(x, dy)


def _gelu(x):
    return 0.5 * x * (1.0 + jnp.tanh(GELU_C * (x + GELU_A * (x * x * x))))


def _gelu_grad(x):
    t = jnp.tanh(GELU_C * (x + GELU_A * (x * x * x)))
    return 0.5 * (1.0 + t) + 0.5 * x * (1.0 - t * t) * (GELU_C * (1.0 + 3.0 * GELU_A * (x * x)))


def _layernorm(a):
    mu = jnp.mean(a, axis=-1, keepdims=True)
    ac = a - mu
    rstd = lax.rsqrt(jnp.mean(ac * ac, axis=-1, keepdims=True) + LN_EPS)
    return ac * rstd, rstd


def _shift_rows(z, halo, k):
    zr = pltpu.roll(z, k, 0)
    hr = pltpu.roll(halo, k, 0)
    row = lax.broadcasted_iota(jnp.int32, hr.shape, 0)
    top = jnp.where(row < k, hr, zr[:HALO])
    return jnp.concatenate([top, zr[HALO:]], axis=0)


def _shift_rows_up(z, halo, k):
    rows = z.shape[0]
    zr = pltpu.roll(z, rows - k, 0)
    hr = pltpu.roll(halo, HALO - k, 0)
    row = lax.broadcasted_iota(jnp.int32, hr.shape, 0)
    bot = jnp.where(row >= HALO - k, hr, zr[rows - HALO:])
    return jnp.concatenate([zr[:rows - HALO], bot], axis=0)


def _causal_mask():
    t = lax.broadcasted_iota(jnp.int32, (CHUNK, CHUNK), 0)
    s = lax.broadcasted_iota(jnp.int32, (CHUNK, CHUNK), 1)
    return s <= t


def _gate_specs(tm, rows, width):
    per = tm // HALO_BF16
    last = rows // HALO_BF16 - 1
    cur = pl.BlockSpec((tm, width), lambda i: (i, 0))
    prev = pl.BlockSpec((HALO_BF16, width), lambda i: (jnp.maximum(i * per - 1, 0), 0))
    nxt = pl.BlockSpec((HALO_BF16, width), lambda i: (jnp.minimum((i + 1) * per, last), 0))
    return cur, prev, nxt


def _cols(ref, lo, hi):
    return ref[:, lo:hi].astype(F32)


def _halo_before(ref, lo, hi):
    return ref[:, lo:hi].astype(F32)[HALO_BF16 - HALO:]


def _halo_after(ref, lo, hi):
    return ref[:, lo:hi].astype(F32)[:HALO]


def _gate_fwd(name, proj, w_s, b_st, cw):
    rows, width = proj.shape
    w = width // 5
    groups = w // CHUNK
    tm = _tile(rows, 256)
    cur, prev, _ = _gate_specs(tm, rows, width)

    def body(p_ref, h_ref, ws_ref, b_ref, cw_ref, o_ref):
        mask = _causal_mask()
        au = _gelu(_cols(p_ref, 0, w))
        vn, _ = _layernorm(_gelu(_cols(p_ref, w, 2 * w)))
        vn = vn.astype(BF16)
        for g in range(groups):
            wc = jnp.where(mask, ws_ref[g], 0.0).astype(BF16)
            cols = slice(g * CHUNK, (g + 1) * CHUNK)
            for ch in range(tm // CHUNK):
                rws = slice(ch * CHUNK, (ch + 1) * CHUNK)
                mixed = jnp.dot(wc, vn[rws, cols], preferred_element_type=F32) + b_ref[:, g:g + 1]
                o_ref[rws, cols] = (au[rws, cols] * mixed).astype(BF16)
        z = _cols(p_ref, 3 * w, 4 * w) * _cols(p_ref, 4 * w, 5 * w)
        zh = _halo_before(h_ref, 3 * w, 4 * w) * _halo_before(h_ref, 4 * w, 5 * w)
        zh = jnp.where(pl.program_id(0) == 0, 0.0, zh)
        y = cw_ref[0:1, :] * _shift_rows(z, zh, 2) + cw_ref[1:2, :] * _shift_rows(z, zh, 1) + cw_ref[2:3, :] * z
        o_ref[:, w:2 * w] = (_cols(p_ref, 2 * w, 3 * w) * y).astype(BF16)

    full = lambda a: pl.BlockSpec(a.shape, lambda i: (0,) * a.ndim)
    return pl.pallas_call(
        body, name=name, grid=(rows // tm,),
        in_specs=[cur, prev, full(w_s), full(b_st), full(cw)],
        out_specs=pl.BlockSpec((tm, 2 * w), lambda i: (i, 0)),
        out_shape=jax.ShapeDtypeStruct((rows, 2 * w), BF16),
        compiler_params=_cp(("parallel",)))(proj, proj, w_s, b_st, cw)


def _gate_bwd(name, proj, d_ab, w_s, b_st, cw):
    rows, width = proj.shape
    w = width // 5
    groups = w // CHUNK
    tm = _tile(rows, 256)
    cur, prev, nxt = _gate_specs(tm, rows, width)
    dcur, _, dnxt = _gate_specs(tm, rows, 2 * w)

    def body(p_ref, ph_ref, pn_ref, d_ref, dn_ref, ws_ref, b_ref, cw_ref, o_ref, dws_ref, dbs_ref, dcw_ref):
        i = pl.program_id(0)

        @pl.when(i == 0)
        def _():
            dws_ref[...] = jnp.zeros_like(dws_ref)
            dbs_ref[...] = jnp.zeros_like(dbs_ref)
            dcw_ref[...] = jnp.zeros_like(dcw_ref)

        mask = _causal_mask()
        u, v = _cols(p_ref, 0, w), _cols(p_ref, w, 2 * w)
        au, av = _gelu(u), _gelu(v)
        vn, rstd = _layernorm(av)
        vnb = vn.astype(BF16)
        d_a = _cols(d_ref, 0, w)
        d_mixed = (d_a * au).astype(BF16)
        ones = jnp.ones((HALO, CHUNK), BF16)
        d_vn_cols = []
        d_au_cols = []
        for g in range(groups):
            wc = jnp.where(mask, ws_ref[g], 0.0).astype(BF16)
            cols = slice(g * CHUNK, (g + 1) * CHUNK)
            dw = jnp.zeros((CHUNK, CHUNK), F32)
            db = jnp.zeros((HALO, CHUNK), F32)
            d_vn_rows, d_au_rows = [], []
            for ch in range(tm // CHUNK):
                rws = slice(ch * CHUNK, (ch + 1) * CHUNK)
                mixed = jnp.dot(wc, vnb[rws, cols], preferred_element_type=F32) + b_ref[:, g:g + 1]
                d_au_rows.append(d_a[rws, cols] * mixed)
                dm = d_mixed[rws, cols]
                dw = dw + lax.dot_general(dm, vnb[rws, cols], NT_DIMS, preferred_element_type=F32)
                db = db + lax.dot_general(ones, dm, NT_DIMS, preferred_element_type=F32)
                d_vn_rows.append(lax.dot_general(wc, dm, TN_DIMS, preferred_element_type=F32))
            dws_ref[g] += jnp.where(mask, dw, 0.0)
            dbs_ref[g:g + 1, :] += db[0:1, :]
            d_vn_cols.append(jnp.concatenate(d_vn_rows, axis=0))
            d_au_cols.append(jnp.concatenate(d_au_rows, axis=0))
        d_vn = jnp.concatenate(d_vn_cols, axis=1)
        d_au = jnp.concatenate(d_au_cols, axis=1)
        d_av = rstd * (d_vn - jnp.mean(d_vn, axis=-1, keepdims=True)
                       - vn * jnp.mean(d_vn * vn, axis=-1, keepdims=True))
        o_ref[:, 0:w] = (d_au * _gelu_grad(u)).astype(BF16)
        o_ref[:, w:2 * w] = (d_av * _gelu_grad(v)).astype(BF16)

        gb, gc, bx = _cols(p_ref, 2 * w, 3 * w), _cols(p_ref, 3 * w, 4 * w), _cols(p_ref, 4 * w, 5 * w)
        z = gc * bx
        zh = jnp.where(i == 0, 0.0, _halo_before(ph_ref, 3 * w, 4 * w) * _halo_before(ph_ref, 4 * w, 5 * w))
        z1, z2 = _shift_rows(z, zh, 1), _shift_rows(z, zh, 2)
        d_b = _cols(d_ref, w, 2 * w)
        y = cw_ref[0:1, :] * z2 + cw_ref[1:2, :] * z1 + cw_ref[2:3, :] * z
        dy = d_b * gb
        dyn = jnp.where(i == pl.num_programs(0) - 1, 0.0,
                        _halo_after(dn_ref, w, 2 * w) * _halo_after(pn_ref, 2 * w, 3 * w))
        dz = (cw_ref[2:3, :] * dy + cw_ref[1:2, :] * _shift_rows_up(dy, dyn, 1)
              + cw_ref[0:1, :] * _shift_rows_up(dy, dyn, 2))
        dcw_ref[0:1, :] += jnp.sum(dy * z2, axis=0, keepdims=True)
        dcw_ref[1:2, :] += jnp.sum(dy * z1, axis=0, keepdims=True)
        dcw_ref[2:3, :] += jnp.sum(dy * z, axis=0, keepdims=True)
        o_ref[:, 2 * w:3 * w] = (d_b * y).astype(BF16)
        o_ref[:, 3 * w:4 * w] = (dz * bx).astype(BF16)
        o_ref[:, 4 * w:5 * w] = (dz * gc).astype(BF16)

    full = lambda a: pl.BlockSpec(a.shape, lambda i: (0,) * a.ndim)
    acc = lambda shape: pl.BlockSpec(shape, lambda i: (0,) * len(shape))
    return pl.pallas_call(
        body, name=name, grid=(rows // tm,),
        in_specs=[cur, prev, nxt, dcur, dnxt, full(w_s), full(b_st), full(cw)],
        out_specs=[pl.BlockSpec((tm, width), lambda i: (i, 0)), acc((groups, CHUNK, CHUNK)),
                   acc((groups, CHUNK)), acc((HALO, w))],
        out_shape=[jax.ShapeDtypeStruct((rows, width), BF16), jax.ShapeDtypeStruct((groups, CHUNK, CHUNK), F32),
                   jax.ShapeDtypeStruct((groups, CHUNK), F32), jax.ShapeDtypeStruct((HALO, w), F32)],
        compiler_params=_cp(("arbitrary",), VMEM_MM))(proj, proj, proj, d_ab, d_ab, w_s, b_st, cw)


def _flat(x):
    return x.reshape(-1, x.shape[-1])


def _rope(t, cosf, sins):
    t2 = _flat(t)
    return (t2 * _flat(cosf) + pltpu.roll(t2, HEAD // 2, 1) * _flat(sins)).reshape(t.shape)


def _rope_bwd(dt, cosf, sins):
    d2 = _flat(dt)
    return (d2 * _flat(cosf) + pltpu.roll(d2 * _flat(sins), HEAD // 2, 1)).reshape(dt.shape)


ATT_UNITS = ATT_TILE // CHUNK


def _attn_units(phases):
    for b, d in enumerate(DILATIONS):
        blocks = ATT_TILE // (CHUNK * d)
        for visit in phases:
            for r in range(d):
                if blocks <= ATT_UNROLL:
                    for j in range(blocks):
                        visit(b, d, r, j, r * blocks + j)
                else:
                    def step(jj, carry, b=b, d=d, r=r, visit=visit, blocks=blocks):
                        for u in range(ATT_UNROLL):
                            j = jj * ATT_UNROLL + u
                            visit(b, d, r, j, r * blocks + j)
                        return carry
                    lax.fori_loop(0, blocks // ATT_UNROLL, step, 0)


class _Unit:
    def __init__(self, d, r, j):
        self.segs = [r + d * k for k in range(SEGS // d)]
        self.w = CHUNK * d // SEGS
        q0 = j * self.w
        self.q0 = q0 if isinstance(q0, int) else pl.multiple_of(q0, HALO)
        k0 = CHUNK + (j - 1) * self.w
        self.k0 = k0 if isinstance(k0, int) else pl.multiple_of(k0, HALO)

    def queries(self, ref):
        return _chunks(ref, self.segs, self.q0, self.w)

    def keys(self, ref):
        return _chunks(ref, self.segs, self.k0, 2 * self.w)

    def put_queries(self, ref, val, add=False):
        _put_chunks(ref, self.segs, self.q0, self.w, val, add)

    def put_keys(self, ref, val, add=False):
        _put_chunks(ref, self.segs, self.k0, 2 * self.w, val, add)


def _chunks(ref, segs, start, size):
    parts = [ref[s, pl.ds(start, size), :] for s in segs]
    return parts[0] if len(parts) == 1 else jnp.concatenate(parts, axis=0)


def _put_chunks(ref, segs, start, size, val, add):
    for k, s in enumerate(segs):
        piece = val[k * size:(k + 1) * size]
        if add:
            ref[s, pl.ds(start, size), :] += piece
        else:
            ref[s, pl.ds(start, size), :] = piece


def _band_bias():
    qi = lax.broadcasted_iota(jnp.int32, (CHUNK, 2 * CHUNK), 0)
    ki = lax.broadcasted_iota(jnp.int32, (CHUNK, 2 * CHUNK), 1)
    tables = []
    for d in DILATIONS:
        nseg, w = SEGS // d, CHUNK * d // SEGS
        pos_q = nseg * (qi % w) + qi // w
        pos_k = nseg * (ki % (2 * w) - w) + ki // (2 * w)
        band = (pos_q >= pos_k) & (pos_q - pos_k <= CHUNK)
        tables += [jnp.where(band, 0.0, -jnp.inf), jnp.where(band & (pos_k >= 0), 0.0, -jnp.inf)]
    return jnp.stack(tables).astype(F32)


def _bias_spec():
    return pl.BlockSpec((2 * len(DILATIONS), CHUNK, 2 * CHUNK), lambda h, n: (0, 0, 0))


def _unit_bias(bias, b, n, j):
    if isinstance(j, int) and j != 0:
        return bias[2 * b]
    return bias[2 * b + jnp.where(jnp.logical_and(n == 0, j == 0), 1, 0)]


def _attn_in_specs(heads):
    blk = (SEGS, CHUNK, HEAD)
    prev = lambda n: jnp.maximum(n - 1, 0)
    return [
        pl.BlockSpec(blk, lambda h, n: (0, n, h)),
        pl.BlockSpec(blk, lambda h, n: (0, n, heads + h)),
        pl.BlockSpec(blk, lambda h, n: (0, prev(n), heads + h)),
        pl.BlockSpec(blk, lambda h, n: (0, n, 2 * heads + h)),
        pl.BlockSpec(blk, lambda h, n: (0, prev(n), 2 * heads + h)),
        pl.BlockSpec(blk, lambda h, n: (0, n, 0)),
        pl.BlockSpec(blk, lambda h, n: (0, n, 0)),
        pl.BlockSpec(blk, lambda h, n: (0, prev(n), 0)),
        pl.BlockSpec(blk, lambda h, n: (0, prev(n), 0)),
    ]


def _attn_load(q_ref, kc_ref, kp_ref, vc_ref, vp_ref, cc_ref, sc_ref, cp_ref, sp_ref, qr, kcat, vcat):
    qr[...] = _rope(q_ref[...].astype(F32), cc_ref[...], sc_ref[...])
    kcat[:, pl.ds(0, CHUNK), :] = _rope(kp_ref[...].astype(F32), cp_ref[...], sp_ref[...])
    kcat[:, pl.ds(CHUNK, CHUNK), :] = _rope(kc_ref[...].astype(F32), cc_ref[...], sc_ref[...])
    vcat[:, pl.ds(0, CHUNK), :] = vp_ref[...].astype(F32)
    vcat[:, pl.ds(CHUNK, CHUNK), :] = vc_ref[...].astype(F32)


def _attn_fwd(name, qkv, cosf, sins, bias):
    t = qkv.shape[0]
    heads = qkv.shape[1] // (3 * HEAD)
    scale = HEAD ** -0.5
    nbr = len(DILATIONS)

    def body(q_ref, kc_ref, kp_ref, vc_ref, vp_ref, cc_ref, sc_ref, cp_ref, sp_ref, bias, o_ref, lse_ref,
             qr, kcat, vcat, obr, lbr, pn):
        n = pl.program_id(1)
        _attn_load(q_ref, kc_ref, kp_ref, vc_ref, vp_ref, cc_ref, sc_ref, cp_ref, sp_ref, qr, kcat, vcat)

        def probs(b, d, r, j, u):
            unit = _Unit(d, r, j)
            s = lax.dot_general(unit.queries(qr).astype(BF16), unit.keys(kcat).astype(BF16), NT_DIMS,
                                preferred_element_type=F32) * scale + _unit_bias(bias, b, n, j)
            mx = jnp.max(s, axis=-1, keepdims=True)
            p = jnp.exp(s - mx)
            den = jnp.sum(p, axis=-1, keepdims=True)
            pn[u] = (p * (1.0 / den)).astype(BF16)
            unit.put_queries(lbr.at[b], jnp.broadcast_to(mx + jnp.log(den), (CHUNK, HEAD)))

        def values(b, d, r, j, u):
            unit = _Unit(d, r, j)
            unit.put_queries(obr.at[b], jnp.dot(pn[u], unit.keys(vcat).astype(BF16), preferred_element_type=F32))

        _attn_units([probs, values])
        ls = [lbr[b] for b in range(nbr)]
        top = functools.reduce(jnp.maximum, ls)
        ws = [jnp.exp(l - top) for l in ls]
        tot = functools.reduce(jnp.add, ws)
        inv = 1.0 / tot
        o = (ws[0] * inv) * obr[0]
        for b in range(1, nbr):
            o = o + (ws[b] * inv) * obr[b]
        o_ref[...] = o.astype(BF16)
        lse_ref[...] = top + jnp.log(tot)

    blk = (SEGS, CHUNK, HEAD)
    keys = pltpu.VMEM((SEGS, 2 * CHUNK, HEAD), F32)
    tile = pl.BlockSpec(blk, lambda h, n: (0, n, h))
    seg = t // SEGS
    qkv3, cos3, sin3 = _seg_view(qkv), _seg_view(cosf), _seg_view(sins)
    o, lse = pl.pallas_call(
        body, name=name, grid=(heads, t // ATT_TILE), in_specs=_attn_in_specs(heads) + [_bias_spec()],
        out_specs=[tile, tile],
        out_shape=[jax.ShapeDtypeStruct((SEGS, seg, heads * HEAD), BF16),
                   jax.ShapeDtypeStruct((SEGS, seg, heads * HEAD), F32)],
        scratch_shapes=[pltpu.VMEM(blk, F32), keys, keys,
                        pltpu.VMEM((nbr,) + blk, F32), pltpu.VMEM((nbr,) + blk, F32),
                        pltpu.VMEM((ATT_UNITS, CHUNK, 2 * CHUNK), BF16)],
        compiler_params=_cp(("parallel", "parallel"), VMEM_MM),
    )(qkv3, qkv3, qkv3, qkv3, qkv3, cos3, sin3, cos3, sin3, bias)
    return o.reshape(t, heads * HEAD), lse.reshape(t, heads * HEAD)


def _attn_bwd(name, qkv, cosf, sins, bias, d_o, o, lse):
    t = qkv.shape[0]
    heads = qkv.shape[1] // (3 * HEAD)
    scale = HEAD ** -0.5

    def body(q_ref, kc_ref, kp_ref, vc_ref, vp_ref, cc_ref, sc_ref, cp_ref, sp_ref, do_ref, o_ref, lse_ref, bias,
             dq_ref, dko_ref, dkp_ref, dvo_ref, dvp_ref, qr, kcat, vcat, dq_acc, dk_acc, dv_acc, delta, ps, dss):
        n = pl.program_id(1)
        _attn_load(q_ref, kc_ref, kp_ref, vc_ref, vp_ref, cc_ref, sc_ref, cp_ref, sp_ref, qr, kcat, vcat)
        dq_acc[...] = jnp.zeros_like(dq_acc)
        dk_acc[...] = jnp.zeros_like(dk_acc)
        dv_acc[...] = jnp.zeros_like(dv_acc)
        delta[...] = jnp.broadcast_to(
            jnp.sum(do_ref[...] * o_ref[...].astype(F32), axis=-1, keepdims=True), delta.shape)

        def probs(b, d, r, j, u):
            unit = _Unit(d, r, j)
            s = lax.dot_general(unit.queries(qr).astype(BF16), unit.keys(kcat).astype(BF16), NT_DIMS,
                                preferred_element_type=F32) * scale + _unit_bias(bias, b, n, j)
            ps[u] = jnp.exp(s - unit.queries(lse_ref)[:, 0:1]).astype(BF16)

        def score_grads(b, d, r, j, u):
            unit = _Unit(d, r, j)
            dp = lax.dot_general(unit.queries(do_ref).astype(BF16), unit.keys(vcat).astype(BF16), NT_DIMS,
                                 preferred_element_type=F32)
            dss[u] = (ps[u].astype(F32) * (dp - unit.queries(delta)[:, 0:1]) * scale).astype(BF16)

        def input_grads(b, d, r, j, u):
            unit = _Unit(d, r, j)
            ds = dss[u]
            unit.put_queries(dq_acc, jnp.dot(ds, unit.keys(kcat).astype(BF16), preferred_element_type=F32), add=True)
            unit.put_keys(dk_acc, lax.dot_general(ds, unit.queries(qr).astype(BF16), TN_DIMS,
                                                  preferred_element_type=F32), add=True)
            unit.put_keys(dv_acc, lax.dot_general(ps[u], unit.queries(do_ref).astype(BF16), TN_DIMS,
                                                  preferred_element_type=F32), add=True)

        _attn_units([probs, score_grads, input_grads])
        dq_ref[...] = _rope_bwd(dq_acc[...], cc_ref[...], sc_ref[...]).astype(BF16)
        dkp_ref[...] = _rope_bwd(dk_acc[:, pl.ds(0, CHUNK), :], cp_ref[...], sp_ref[...]).astype(BF16)
        dko_ref[...] = _rope_bwd(dk_acc[:, pl.ds(CHUNK, CHUNK), :], cc_ref[...], sc_ref[...]).astype(BF16)
        dvp_ref[...] = dv_acc[:, pl.ds(0, CHUNK), :].astype(BF16)
        dvo_ref[...] = dv_acc[:, pl.ds(CHUNK, CHUNK), :].astype(BF16)

    blk = (SEGS, CHUNK, HEAD)
    tile = pl.BlockSpec(blk, lambda h, n: (0, n, h))
    big = pltpu.VMEM((SEGS, 2 * CHUNK, HEAD), F32)
    seg = t // SEGS
    qkv3, cos3, sin3 = _seg_view(qkv), _seg_view(cosf), _seg_view(sins)
    return pl.pallas_call(
        body, name=name, grid=(heads, t // ATT_TILE),
        in_specs=_attn_in_specs(heads) + [tile, tile, tile, _bias_spec()],
        out_specs=[tile] * 5,
        out_shape=[jax.ShapeDtypeStruct((SEGS, seg, heads * HEAD), BF16)] * 5,
        scratch_shapes=[pltpu.VMEM(blk, F32), big, big, pltpu.VMEM(blk, F32), big, big, pltpu.VMEM(blk, F32),
                        pltpu.VMEM((ATT_UNITS, CHUNK, 2 * CHUNK), BF16), pltpu.VMEM((ATT_UNITS, CHUNK, 2 * CHUNK), BF16)],
        compiler_params=_cp(("parallel", "parallel"), 60 << 20),
    )(qkv3, qkv3, qkv3, qkv3, qkv3, cos3, sin3, cos3, sin3, _seg_view(d_o), _seg_view(o), _seg_view(lse), bias)


def _attn_merge(name, dq, dk_own, dk_prev, dv_own, dv_prev):
    _, seg, hd = dq.shape
    nt = seg // CHUNK
    tw = _tile(hd, 512)

    def body(dq_ref, dko_ref, dkn_ref, dvo_ref, dvn_ref, o_ref):
        last = pl.program_id(0) == nt - 1
        part = pl.program_id(1)

        @pl.when(part == 0)
        def _():
            o_ref[...] = dq_ref[...]

        @pl.when(part == 1)
        def _():
            o_ref[...] = (dko_ref[...].astype(F32) + jnp.where(last, 0.0, dkn_ref[...].astype(F32))).astype(BF16)

        @pl.when(part == 2)
        def _():
            o_ref[...] = (dvo_ref[...].astype(F32) + jnp.where(last, 0.0, dvn_ref[...].astype(F32))).astype(BF16)

    blk = (SEGS, CHUNK, tw)

    def own(part):
        return pl.BlockSpec(blk, lambda n, p, c: (0, jnp.where(p == part, n, 0), jnp.where(p == part, c, 0)))

    def nxt(part):
        return pl.BlockSpec(blk, lambda n, p, c: (0, jnp.where(p == part, jnp.minimum(n + 1, nt - 1), 0),
                                                  jnp.where(p == part, c, 0)))

    per = hd // tw
    return pl.pallas_call(
        body, name=name, grid=(nt, 3, per), in_specs=[own(0), own(1), nxt(1), own(2), nxt(2)],
        out_specs=pl.BlockSpec(blk, lambda n, p, c: (0, n, p * per + c)),
        out_shape=jax.ShapeDtypeStruct((SEGS, seg, 3 * hd), BF16),
        compiler_params=_cp(("parallel", "parallel", "parallel"), VMEM_MM),
    )(dq, dk_own, dk_prev, dv_own, dv_prev).reshape(SEGS * seg, 3 * hd)


def _sum_parts(name, parts):
    nparts, rows, cols = parts.shape
    tr = _tile(rows, 256)

    def body(p_ref, o_ref):
        s = p_ref[0]
        for k in range(1, nparts):
            s = s + p_ref[k]
        o_ref[...] = s

    return pl.pallas_call(
        body, name=name, grid=(rows // tr,),
        in_specs=[pl.BlockSpec((nparts, tr, cols), lambda i: (0, i, 0))],
        out_specs=pl.BlockSpec((tr, cols), lambda i: (i, 0)),
        out_shape=jax.ShapeDtypeStruct((rows, cols), F32),
        compiler_params=_cp(("parallel",)))(parts)


def _rows128(a, pad_to=8):
    flat = a.reshape(-1)
    rows = -(-flat.shape[0] // 128)
    rows = -(-rows // pad_to) * pad_to
    flat = jnp.pad(flat, (0, rows * 128 - flat.shape[0]))
    return flat.reshape(rows, 128)


def _pack(arrays):
    return jnp.concatenate([_rows128(a) for a in arrays], axis=0)


def _unpack(packed, like):
    out, at = [], 0
    for a in like:
        size = 1
        for s in a.shape:
            size *= s
        rows = -(-(-(-size // 128)) // 8) * 8
        out.append(packed[at:at + rows].reshape(-1)[:size].reshape(a.shape))
        at += rows
    return out


def kernel(x, norm_mix_pre, norm_mix_post, norm_mlp_pre, norm_mlp_post, w_in_ab, w_spatial, b_spatial, conv_w, w_out_ab, w_qkv, w_o, w_up, w_down, loss_target, m_norm_mix_pre, m_norm_mix_post, m_norm_mlp_pre, m_norm_mlp_post, m_w_in_ab, m_w_spatial, m_b_spatial, m_conv_w, m_w_out_ab, m_w_qkv, m_w_o, m_w_up, m_w_down, v_norm_mix_pre, v_norm_mix_post, v_norm_mlp_pre, v_norm_mlp_post, v_w_in_ab, v_w_spatial, v_b_spatial, v_conv_w, v_w_out_ab, v_w_qkv, v_w_o, v_w_up, v_w_down):
    depth = norm_mix_pre.shape[0]
    seq, dm = x.shape[1], x.shape[2]
    h0 = x.reshape(seq, dm)
    target = loss_target.reshape(seq, dm)
    ax, ay, ac = lax.axis_index("x"), lax.axis_index("y"), lax.axis_index("c")
    my_block = 4 * ax + 2 * ay + ac
    block = jnp.reshape(my_block, (1,)).astype(jnp.int32)

    half = HEAD // 2
    inv_freq = ROPE_THETA ** (-jnp.arange(half, dtype=F32) * 2.0 / HEAD)
    ang = jnp.arange(seq, dtype=jnp.int32).astype(F32)[:, None] * inv_freq[None, :]
    ang = ang.reshape(seq // SEGS, SEGS, half).transpose(1, 0, 2).reshape(seq, half)
    cosf = jnp.concatenate([jnp.cos(ang), jnp.cos(ang)], axis=-1)
    sins = jnp.concatenate([-jnp.sin(ang), jnp.sin(ang)], axis=-1)
    band_bias = _band_bias()

    big = {"w_in_ab": w_in_ab, "w_out_ab": w_out_ab, "w_qkv": w_qkv, "w_o": w_o, "w_up": w_up, "w_down": w_down}
    use_order = []
    for l in range(depth):
        use_order += [("w_in_ab", l // 2), ("w_out_ab", l // 2)] if l % 2 == 0 else [("w_qkv", l // 2), ("w_o", l // 2)]
        use_order += [("w_up", l), ("w_down", l)]
    n_even = w_in_ab.shape[0]
    cw_rows = jnp.pad(conv_w.reshape(n_even * CONV_TAPS, conv_w.shape[2]), ((0, HALO - (n_even * CONV_TAPS) % HALO), (0, 0)))
    cw_gathered = _all_gather("ag_conv", [cw_rows])[0]
    first = [k for k in use_order if k in (("w_in_ab", 0), ("w_out_ab", 0), ("w_up", 0), ("w_down", 0))]
    rest = [k for k in use_order if k not in first]
    lands_a, sems_a = _ag_start("ag_start_first", [_cast_fill(f"cast_{nm}_{l}", big[nm], l, block) for nm, l in first],
                                cw_gathered)
    lands_b, sems_b = _ag_start("ag_start_rest", [_cast_fill(f"cast_{nm}_{l}", big[nm], l, block) for nm, l in rest],
                                lands_a[0])
    lands = dict(zip(first + rest, list(lands_a) + list(lands_b)))
    ag_sems = dict(zip(first + rest, list(sems_a) + list(sems_b)))
    passed_on, wg = [], {}

    def weight(key, after):
        pins = []
        if key not in wg:
            upto = min(use_order.index(key) + 2, len(use_order) - 1)
            for k in use_order[len(passed_on):upto + 1]:
                lands[k] = _ag_mid(f"ag_mid_{k[0]}_{k[1]}", lands[k], ag_sems[k], after)
                passed_on.append(k)
                if k != key and use_order.index(k) >= AG_PIN_FROM:
                    pins.append(lands[k])
            wg[key] = _ag_wait(f"ag_wait_{key[0]}_{key[1]}", lands[key], ag_sems[key], after)
        return wg[key], pins

    cw_all = cw_gathered[:, :n_even * CONV_TAPS].reshape(NDEV, n_even, CONV_TAPS, -1)
    cw_all = jnp.transpose(cw_all, (1, 2, 0, 3)).reshape(n_even, CONV_TAPS, -1)
    cw_full = [jnp.pad(cw_all[e], ((0, HALO - CONV_TAPS), (0, 0))) for e in range(n_even)]

    def rows_nat(blk):
        return blk.reshape(blk.shape[0] * blk.shape[1], blk.shape[2])

    saved = []
    hn = _norm_fwd("norm_first", h0, g_pre=norm_mix_pre[0][None])[0]
    h = h0
    for l in range(depth):
        s = {"h_in": h, "hn1": hn}
        if l % 2 == 0:
            e = l // 2
            w_, pins = weight(("w_in_ab", e), hn)
            proj = _mm_nn_blk(f"fwd_in_{l}", hn, w_, after=pins)
            ab = _gate_fwd(f"gate_fwd_{l}", proj, w_spatial[e], b_spatial[e].T, cw_full[e])
            w_, pins = weight(("w_out_ab", e), ab)
            mix = _mm_nn(f"fwd_out_{l}", ab, rows_nat(w_), after=pins)
            s.update(proj=proj, ab=ab)
        else:
            o_ = l // 2
            w_, pins = weight(("w_qkv", o_), hn)
            qkv = _mm_nn_blk(f"fwd_qkv_{l}", hn, w_, after=pins)
            att, lse = _attn_fwd(f"attn_fwd_{l}", qkv, cosf, sins, band_bias)
            w_, pins = weight(("w_o", o_), att)
            mix = _mm_nn(f"fwd_o_{l}", att, rows_nat(w_), after=pins)
            s.update(qkv=qkv, att=att, lse=lse)
        h1, hn2 = _norm_fwd(f"norm_mid_{l}", h, mix, norm_mix_post[l][None], norm_mlp_pre[l][None], seg_z=l % 2 == 1)
        w_, pins = weight(("w_up", l), hn2)
        act = _mm_nn_blk(f"fwd_up_{l}", hn2, w_, relu2=True, after=pins)
        w_, pins = weight(("w_down", l), act)
        f = _mm_nn(f"fwd_down_{l}", act, rows_nat(w_), after=pins)
        s.update(mix=mix, h1=h1, hn2=hn2, act=act, f=f)
        if l + 1 < depth:
            h, hn = _norm_fwd(f"norm_end_{l}", h1, f, norm_mlp_post[l][None], norm_mix_pre[l + 1][None],
                              seg_y=(l + 1) % 2 == 1)
        else:
            h = _norm_fwd(f"norm_end_{l}", h1, f, norm_mlp_post[l][None])[0]
        saved.append(s)

    d_h, loss_row = _loss_grad("loss", h, target)
    rs = {}

    def scatter(key, g):
        rs[key] = _rs_start(f"rs_start_{key[0]}_{key[1]}", g.reshape(NDEV, -1, g.shape[-1]))

    dg ={nm: [None] * depth for nm in ("norm_mix_pre", "norm_mix_post", "norm_mlp_pre", "norm_mlp_post")}
    d_ws, d_bs, d_cw = [None] * n_even, [None] * n_even, [None] * n_even
    d_hn_next = None
    for l in reversed(range(depth)):
        s = saved[l]
        if l == depth - 1:
            d_f, dg["norm_mlp_post"][l] = _norm_bwd(f"nb_end_{l}", d_h, post=(s["f"], norm_mlp_post[l][None]))
        else:
            d_h, dg["norm_mix_pre"][l + 1], d_f, dg["norm_mlp_post"][l] = _norm_bwd(
                f"nb_end_{l}", d_h, pre=(d_hn_next, saved[l + 1]["h_in"], norm_mix_pre[l + 1][None]),
                post=(s["f"], norm_mlp_post[l][None]), seg_dy=(l + 1) % 2 == 1)
        wd = rows_nat(wg[("w_down", l)])
        d_up = _mm_nt_rows(f"bwd_down_{l}", d_f, wd, act=s["act"])
        scatter(("w_down", l), _mm_tn(f"gw_down_{l}", s["act"], d_f))
        scatter(("w_up", l), _mm_tn(f"gw_up_{l}", s["hn2"], d_up, nb=w_up.shape[2]))
        d_hn2 = _mm_nt_blk(f"bwd_up_{l}", d_up, wg[("w_up", l)], after=[rs[("w_down", l)][0], rs[("w_up", l)][0]])
        d_h, dg["norm_mlp_pre"][l], d_mix, dg["norm_mix_post"][l] = _norm_bwd(
            f"nb_mid_{l}", d_h, pre=(d_hn2, s["h1"], norm_mlp_pre[l][None]),
            post=(s["mix"], norm_mix_post[l][None]), seg_z=l % 2 == 1)
        if l % 2 == 0:
            e = l // 2
            wo = rows_nat(wg[("w_out_ab", e)])
            d_ab = _mm_nt_rows(f"bwd_out_{l}", d_mix, wo)
            scatter(("w_out_ab", e), _mm_tn(f"gw_out_{l}", s["ab"], d_mix))
            d_proj, d_ws[e], d_bs[e], d_cw[e] = _gate_bwd(
                f"gate_bwd_{l}", s["proj"], d_ab, w_spatial[e], b_spatial[e].T, cw_full[e])
            scatter(("w_in_ab", e), _mm_tn(f"gw_in_{l}", s["hn1"], d_proj, nb=w_in_ab.shape[2]))
            d_hn_next = _mm_nt_blk(f"bwd_in_{l}", d_proj, wg[("w_in_ab", e)],
                                   after=[rs[("w_out_ab", e)][0], rs[("w_in_ab", e)][0]])
        else:
            o_ = l // 2
            wo = rows_nat(wg[("w_o", o_)])
            d_att = _mm_nt_rows(f"bwd_o_{l}", d_mix, wo, out_dtype=F32)
            scatter(("w_o", o_), _mm_tn(f"gw_o_{l}", s["att"], d_mix))
            parts = _attn_bwd(f"attn_bwd_{l}", s["qkv"], cosf, sins, band_bias, d_att, s["att"], s["lse"])
            d_qkv = _attn_merge(f"attn_merge_{l}", *parts)
            scatter(("w_qkv", o_), _mm_tn(f"gw_qkv_{l}", s["hn1"], d_qkv, nb=w_qkv.shape[2]))
            d_hn_next = _mm_nt_blk(f"bwd_qkv_{l}", d_qkv, wg[("w_qkv", o_)],
                                   after=[rs[("w_o", o_)][0], rs[("w_qkv", o_)][0]])
    grad_x, dg["norm_mix_pre"][0] = _norm_bwd("nb_first", d_h, pre=(d_hn_next, h0, norm_mix_pre[0][None]))

    small_g = ([jnp.concatenate(dg[nm], axis=0) for nm in dg]
               + [jnp.stack(d_ws), jnp.stack(d_bs), jnp.stack([c[:CONV_TAPS] for c in d_cw]), loss_row])
    small_land, small_send, small_recv = _ag_direct_start("ag_small_start", _fill_slot("fill_small", _pack(small_g), block))

    moments = {"w_in_ab": (m_w_in_ab, v_w_in_ab), "w_out_ab": (m_w_out_ab, v_w_out_ab), "w_qkv": (m_w_qkv, v_w_qkv),
               "w_o": (m_w_o, v_w_o), "w_up": (m_w_up, v_w_up), "w_down": (m_w_down, v_w_down)}
    out_big = {}
    behind = small_land
    for nm in ("w_o", "w_qkv", "w_down", "w_up", "w_out_ab", "w_in_ab"):
        own, landed = [], []
        for l in range(big[nm].shape[0]):
            g, land = _wait_all(f"rs_wait_{nm}_{l}", *rs[(nm, l)], behind)
            own.append(g)
            landed.append(land)
        out_big[nm] = _adamw_layers(f"adamw_{nm}", own, landed, block, big[nm], moments[nm][0], moments[nm][1])
        behind = out_big[nm][0]

    summed = _sum_parts("sum_small", _ag_direct_wait("ag_small_wait", small_land, small_send, small_recv, behind))
    g_nmp, g_nmo, g_nlp, g_nlo, g_ws, g_bs, g_cw_all, loss_sum = _unpack(summed, small_g)
    loss = loss_sum[0, 0]
    cwb = conv_w.shape[2]
    g_cw = lax.dynamic_slice_in_dim(g_cw_all, my_block * cwb, cwb, axis=2)
    small_w = [norm_mix_pre, norm_mix_post, norm_mlp_pre, norm_mlp_post, w_spatial, b_spatial, conv_w]
    small_m = [m_norm_mix_pre, m_norm_mix_post, m_norm_mlp_pre, m_norm_mlp_post, m_w_spatial, m_b_spatial, m_conv_w]
    small_v = [v_norm_mix_pre, v_norm_mix_post, v_norm_mlp_pre, v_norm_mlp_post, v_w_spatial, v_b_spatial, v_conv_w]
    small_grad = [g_nmp, g_nmo, g_nlp, g_nlo, g_ws, g_bs, g_cw]
    upd = _adamw("adamw_small", _pack(small_grad)[None], _pack(small_w), _pack(small_m), _pack(small_v))
    sg, sd, sm, sv = [_unpack(u, small_w) for u in upd]

    def outs(i_small, i_big):
        return (i_small[0], i_small[1], i_small[2], i_small[3], i_big["w_in_ab"], i_small[4], i_small[5], i_small[6],
                i_big["w_out_ab"], i_big["w_qkv"], i_big["w_o"], i_big["w_up"], i_big["w_down"])

    pick = lambda i: {nm: out_big[nm][i] for nm in big}
    return (loss, grad_x.reshape(x.shape), *outs(sg, pick(0)), *outs(sd, pick(1)), *outs(sm, pick(2)),
            *outs(sv, pick(3)))
```

```python
import functools

import jax
import jax.numpy as jnp
from jax import lax
from jax.experimental import pallas as pl
from jax.experimental.pallas import tpu as pltpu

F32 = jnp.float32
BF16 = jnp.bfloat16
MESH = pl.DeviceIdType.MESH
ANY = pl.BlockSpec(memory_space=pl.ANY)
HBM = pl.BlockSpec(memory_space=pltpu.HBM)
SEM = pl.BlockSpec(memory_space=pltpu.SEMAPHORE)
EFFECT = pltpu.SideEffectType.DATAFLOW_SIDE_EFFECTING

NDEV = 8
NCHIP = 4
RMS_EPS = 1e-6
LN_EPS = 1e-5
CHUNK = 128
HEAD = 128
ATT_TILE = 2048
ATT_UNROLL = 4
DILATIONS = (1, 4, 16)
SEGS = 16
ROPE_THETA = 10000.0
CONV_TAPS = 3
HALO = 8
HALO_BF16 = 16
GELU_C = 0.7978845608028654
GELU_A = 0.044715
ADAM_LR, ADAM_B1, ADAM_B2, ADAM_EPS, ADAM_WD, ADAM_STEP = 0.001, 0.9, 0.999, 1e-08, 0.01, 10
VMEM_MM = 52 << 20
VMEM_EW = 40 << 20


def _cp(sem=None, vmem=VMEM_EW):
    if sem is None:
        return pltpu.CompilerParams(vmem_limit_bytes=vmem)
    return pltpu.CompilerParams(dimension_semantics=sem, vmem_limit_bytes=vmem)


def _tile(n, want):
    return want if n % want == 0 else n


def _all_gather(name, shards, after=()):
    n = len(shards)
    after = list(after)

    def body(*refs):
        ins, outs = refs[:n], refs[n + len(after):2 * n + len(after)]
        send_sems, recv_sems, local_sems = refs[2 * n + len(after):]
        x, y, c = lax.axis_index("x"), lax.axis_index("y"), lax.axis_index("c")
        me, sibling = (x, y, c), (x, y, 1 - c)
        chips = [(1 - x, y), (x, 1 - y), (1 - x, 1 - y)]

        def slot(p):
            return 4 * p[0] + 2 * p[1] + p[2]

        def copy(i, k, block, to, src=None):
            dst = outs[i].at[slot(block)]
            return pltpu.make_async_remote_copy(
                src_ref=dst if src is None else src, dst_ref=dst,
                send_sem=send_sems.at[i, k], recv_sem=recv_sems.at[i, k],
                device_id=to, device_id_type=MESH)

        mine = [pltpu.make_async_copy(ins[i], outs[i].at[slot(me)], local_sems.at[i]) for i in range(n)]
        for cp in mine:
            cp.start()
        first = []
        for i in range(n):
            first.append(copy(i, 0, me, sibling, src=ins[i]))
            for j, chip in enumerate(chips):
                first.append(copy(i, 1 + j, me, (*chip, c), src=ins[i]))
        for cp in first:
            cp.start()
        passed = []
        for j, chip in enumerate(chips):
            for i in range(n):
                copy(i, 1 + j, (*chip, c), me).wait_recv()
                fwd = copy(i, 4 + j, (*chip, c), sibling)
                fwd.start()
                passed.append(fwd)
        for i in range(n):
            copy(i, 0, sibling, me).wait_recv()
            for j, chip in enumerate(chips):
                copy(i, 4 + j, (*chip, 1 - c), me).wait_recv()
        for cp in first + passed:
            cp.wait_send()
        for cp in mine:
            cp.wait()

    return pl.pallas_call(
        body, name=name,
        out_shape=[jax.ShapeDtypeStruct((NDEV,) + s.shape, s.dtype) for s in shards],
        in_specs=[ANY] * (n + len(after)), out_specs=[ANY] * n,
        scratch_shapes=[pltpu.SemaphoreType.DMA((n, 7)), pltpu.SemaphoreType.DMA((n, 7)),
                        pltpu.SemaphoreType.DMA((n,))],
    )(*shards, *after)


def _peer(x, y, c, r):
    return (1 - x if r & 4 else x, 1 - y if r & 2 else y, 1 - c if r & 1 else c)


def _slot(p):
    return 4 * p[0] + 2 * p[1] + p[2]


def _cast_fill(name, w, layer, block):
    _, rows, cols = w.shape
    tr = _tile(rows, 256)

    def body(blk_ref, w_ref, o_ref):
        o_ref[...] = w_ref[...].astype(BF16)

    return pl.pallas_call(
        body, name=name,
        grid_spec=pltpu.PrefetchScalarGridSpec(
            num_scalar_prefetch=1, grid=(rows // tr,),
            in_specs=[pl.BlockSpec((None, tr, cols), lambda i, blk: (layer, i, 0))],
            out_specs=pl.BlockSpec((None, tr, cols), lambda i, blk: (blk[0], i, 0))),
        out_shape=jax.ShapeDtypeStruct((NDEV, rows, cols), BF16),
        compiler_params=_cp(("parallel",)))(block, w)


OTHER_CHIPS = (2, 4, 6)
AG_SEMS = 6
AG_PIN_FROM = 5


def _ag_start(name, lands, after):
    n = len(lands)

    def body(*refs):
        ins, sems = refs[:n], refs[n + 1:n + 1 + AG_SEMS * n]
        x, y, c = lax.axis_index("x"), lax.axis_index("y"), lax.axis_index("c")
        mine = _slot((x, y, c))
        for i in range(n):
            send_a, *recv_a, _, recv_b = sems[AG_SEMS * i:AG_SEMS * (i + 1)]
            block = ins[i].at[mine]
            pltpu.make_async_remote_copy(src_ref=block, dst_ref=block, send_sem=send_a, recv_sem=recv_b,
                                         device_id=_peer(x, y, c, 1), device_id_type=MESH).start()
            for k, r in enumerate(OTHER_CHIPS):
                pltpu.make_async_remote_copy(src_ref=block, dst_ref=block, send_sem=send_a, recv_sem=recv_a[k],
                                             device_id=_peer(x, y, c, r), device_id_type=MESH).start()

    outs = pl.pallas_call(
        body, name=name,
        out_shape=[pltpu.SemaphoreType.DMA(())] * (AG_SEMS * n) + [pltpu.HBM(a.shape, a.dtype) for a in lands],
        in_specs=[HBM] * n + [ANY], out_specs=[SEM] * (AG_SEMS * n) + [HBM] * n,
        input_output_aliases={i: AG_SEMS * n + i for i in range(n)},
        compiler_params=pltpu.CompilerParams(has_side_effects=EFFECT),
    )(*[pltpu.with_memory_space_constraint(a, pltpu.HBM) for a in lands], after)
    return outs[AG_SEMS * n:], [tuple(outs[AG_SEMS * i:AG_SEMS * (i + 1)]) for i in range(n)]


def _ag_mid(name, land, sems, after):
    _, *recv_a, send_b, recv_b = sems

    def body(land_ref, ra0, ra1, ra2, send_b_ref, recv_b_ref, after_ref, land_out):
        x, y, c = lax.axis_index("x"), lax.axis_index("y"), lax.axis_index("c")
        sibling = _peer(x, y, c, 1)
        for arrival, r in zip((ra0, ra1, ra2), OTHER_CHIPS):
            block = land_ref.at[_slot(_peer(x, y, c, r))]
            pltpu.make_async_remote_copy(src_ref=block, dst_ref=block, send_sem=send_b_ref, recv_sem=arrival,
                                         device_id=sibling, device_id_type=MESH).wait_recv()
            pltpu.make_async_remote_copy(src_ref=block, dst_ref=block, send_sem=send_b_ref, recv_sem=recv_b_ref,
                                         device_id=sibling, device_id_type=MESH).start()

    return pl.pallas_call(
        body, name=name, out_shape=pltpu.HBM(land.shape, land.dtype),
        in_specs=[HBM] + [SEM] * 5 + [ANY], out_specs=HBM, input_output_aliases={0: 0},
        compiler_params=pltpu.CompilerParams(has_side_effects=EFFECT),
    )(land, *recv_a, send_b, recv_b, after)


def _ag_wait(name, land, sems, after):
    send_a, _, _, _, send_b, recv_b = sems

    def body(land_ref, send_a_ref, send_b_ref, recv_b_ref, after_ref, land_out):
        x, y, c = lax.axis_index("x"), lax.axis_index("y"), lax.axis_index("c")
        sibling = _peer(x, y, c, 1)
        four = land_ref.at[pl.ds(0, 1 + len(OTHER_CHIPS))]
        three = land_ref.at[pl.ds(0, len(OTHER_CHIPS))]
        first = pltpu.make_async_remote_copy(src_ref=four, dst_ref=four, send_sem=send_a_ref, recv_sem=recv_b_ref,
                                             device_id=sibling, device_id_type=MESH)
        passed = pltpu.make_async_remote_copy(src_ref=three, dst_ref=three, send_sem=send_b_ref, recv_sem=recv_b_ref,
                                              device_id=sibling, device_id_type=MESH)
        first.wait_send()
        passed.wait_send()
        first.wait_recv()

    return pl.pallas_call(
        body, name=name, out_shape=pltpu.HBM(land.shape, land.dtype),
        in_specs=[HBM, SEM, SEM, SEM, ANY], out_specs=HBM, input_output_aliases={0: 0},
        compiler_params=pltpu.CompilerParams(has_side_effects=EFFECT),
    )(land, send_a, send_b, recv_b, after)


def _fill_slot(name, rows, block):
    r, c = rows.shape

    def body(blk_ref, i_ref, o_ref):
        o_ref[...] = i_ref[...]

    return pl.pallas_call(
        body, name=name,
        grid_spec=pltpu.PrefetchScalarGridSpec(
            num_scalar_prefetch=1, grid=(1,),
            in_specs=[pl.BlockSpec((r, c), lambda i, blk: (0, 0))],
            out_specs=pl.BlockSpec((None, r, c), lambda i, blk: (blk[0], 0, 0))),
        out_shape=jax.ShapeDtypeStruct((NDEV, r, c), rows.dtype),
        compiler_params=_cp(("arbitrary",)))(block, rows)


def _ag_direct_start(name, land):
    def body(land_ref, send, recv, land_out):
        x, y, c = lax.axis_index("x"), lax.axis_index("y"), lax.axis_index("c")
        block = land_ref.at[_slot((x, y, c))]
        for r in range(1, NDEV):
            pltpu.make_async_remote_copy(src_ref=block, dst_ref=block, send_sem=send, recv_sem=recv,
                                         device_id=_peer(x, y, c, r), device_id_type=MESH).start()

    send, recv, land_thru = pl.pallas_call(
        body, name=name,
        out_shape=[pltpu.SemaphoreType.DMA(()), pltpu.SemaphoreType.DMA(()), pltpu.HBM(land.shape, land.dtype)],
        in_specs=[HBM], out_specs=[SEM, SEM, HBM], input_output_aliases={0: 2},
        compiler_params=pltpu.CompilerParams(has_side_effects=EFFECT),
    )(pltpu.with_memory_space_constraint(land, pltpu.HBM))
    return land_thru, send, recv


def _ag_direct_wait(name, land, send, recv, after):
    def body(land_ref, send_ref, recv_ref, after_ref, land_out):
        x, y, c = lax.axis_index("x"), lax.axis_index("y"), lax.axis_index("c")
        seven = land_ref.at[pl.ds(0, NDEV - 1)]
        copy = pltpu.make_async_remote_copy(src_ref=seven, dst_ref=seven, send_sem=send_ref, recv_sem=recv_ref,
                                            device_id=_peer(x, y, c, 1), device_id_type=MESH)
        copy.wait_send()
        copy.wait_recv()

    return pl.pallas_call(
        body, name=name, out_shape=pltpu.HBM(land.shape, land.dtype),
        in_specs=[HBM, SEM, SEM, ANY], out_specs=HBM, input_output_aliases={0: 0},
        compiler_params=pltpu.CompilerParams(has_side_effects=EFFECT),
    )(land, send, recv, after)


def _wait_all(name, src, land, send, recv, after):
    def body(src_ref, land_ref, send_ref, recv_ref, after_ref, src_out, land_out):
        x, y, c = lax.axis_index("x"), lax.axis_index("y"), lax.axis_index("c")
        seven = land_ref.at[pl.ds(0, NDEV - 1)]
        copy = pltpu.make_async_remote_copy(src_ref=seven, dst_ref=seven, send_sem=send_ref, recv_sem=recv_ref,
                                            device_id=_peer(x, y, c, 1), device_id_type=MESH)
        copy.wait_send()
        copy.wait_recv()

    return pl.pallas_call(
        body, name=name,
        out_shape=[pltpu.HBM(src.shape, src.dtype), pltpu.HBM(land.shape, land.dtype)],
        in_specs=[HBM, HBM, SEM, SEM, ANY], out_specs=[HBM, HBM],
        input_output_aliases={0: 0, 1: 1},
        compiler_params=pltpu.CompilerParams(has_side_effects=EFFECT),
    )(src, land, send, recv, after)


def _rs_start(name, grad):
    land = lax.empty((NDEV - 1,) + grad.shape[1:], grad.dtype)

    def body(g_ref, land_ref, send, recv, g_out, land_out):
        x, y, c = lax.axis_index("x"), lax.axis_index("y"), lax.axis_index("c")
        for r in range(1, NDEV):
            peer = _peer(x, y, c, r)
            pltpu.make_async_remote_copy(
                src_ref=g_ref.at[_slot(peer)], dst_ref=land_ref.at[r - 1], send_sem=send, recv_sem=recv,
                device_id=peer, device_id_type=MESH).start()

    send, recv, g_thru, land_thru = pl.pallas_call(
        body, name=name,
        out_shape=[pltpu.SemaphoreType.DMA(()), pltpu.SemaphoreType.DMA(()),
                   pltpu.HBM(grad.shape, grad.dtype), pltpu.HBM(land.shape, land.dtype)],
        in_specs=[HBM, HBM], out_specs=[SEM, SEM, HBM, HBM], input_output_aliases={0: 2, 1: 3},
        compiler_params=pltpu.CompilerParams(has_side_effects=EFFECT),
    )(pltpu.with_memory_space_constraint(grad, pltpu.HBM), pltpu.with_memory_space_constraint(land, pltpu.HBM))
    return g_thru, land_thru, send, recv


def _adam_math(w, g, m, v):
    m = ADAM_B1 * m + (1.0 - ADAM_B1) * g
    v = ADAM_B2 * v + (1.0 - ADAM_B2) * (g * g)
    m_hat = m / (1.0 - ADAM_B1 ** ADAM_STEP)
    v_hat = v / (1.0 - ADAM_B2 ** ADAM_STEP)
    delta = -ADAM_LR * (m_hat / (jnp.sqrt(v_hat) + ADAM_EPS) + ADAM_WD * w)
    return delta, m, v


def _adamw(name, parts, w, m, v):
    nparts, rows, cols = parts.shape
    tr = _tile(rows, 256)

    def body(p_ref, w_ref, m_ref, v_ref, g_out, d_out, m_out, v_out):
        g = p_ref[0].astype(F32)
        for k in range(1, nparts):
            g = g + p_ref[k].astype(F32)
        delta, mn, vn = _adam_math(w_ref[...], g, m_ref[...], v_ref[...])
        g_out[...] = g
        d_out[...] = delta
        m_out[...] = mn
        v_out[...] = vn

    row = pl.BlockSpec((tr, cols), lambda i: (i, 0))
    return pl.pallas_call(
        body, name=name, grid=(rows // tr,),
        in_specs=[pl.BlockSpec((nparts, tr, cols), lambda i: (0, i, 0)), row, row, row],
        out_specs=[row] * 4,
        out_shape=[jax.ShapeDtypeStruct((rows, cols), F32)] * 4,
        compiler_params=_cp(("parallel",)),
    )(parts, w, m, v)


def _adamw_layers(name, grads, lands, block, w, m, v):
    layers, rows, cols = w.shape
    nland = lands[0].shape[0]
    tr = rows
    while tr % 2 == 0 and tr > 8 and nland * tr * cols * 2 > (2 << 20):
        tr //= 2

    def body(blk_ref, *refs):
        own_refs, land_refs = refs[:layers], refs[layers:2 * layers]
        w_ref, m_ref, v_ref, g_out, d_out, m_out, v_out = refs[2 * layers:]
        layer = pl.program_id(0)
        for k in range(layers):
            @pl.when(layer == k)
            def _(k=k):
                g = own_refs[k][...].astype(F32)
                for s in range(nland):
                    g = g + land_refs[k][s].astype(F32)
                delta, mn, vn = _adam_math(w_ref[...], g, m_ref[...], v_ref[...])
                g_out[...] = g
                d_out[...] = delta
                m_out[...] = mn
                v_out[...] = vn

    def own_spec(k):
        return pl.BlockSpec((None, tr, cols), lambda l, i, blk: (blk[0], jnp.where(l == k, i, 0), 0))

    def land_spec(k):
        return pl.BlockSpec((nland, tr, cols), lambda l, i, blk: (0, jnp.where(l == k, i, 0), 0))

    row = pl.BlockSpec((None, tr, cols), lambda l, i, blk: (l, i, 0))
    return pl.pallas_call(
        body, name=name,
        grid_spec=pltpu.PrefetchScalarGridSpec(
            num_scalar_prefetch=1, grid=(layers, rows // tr),
            in_specs=[own_spec(k) for k in range(layers)] + [land_spec(k) for k in range(layers)] + [row, row, row],
            out_specs=[row] * 4),
        out_shape=[jax.ShapeDtypeStruct((layers, rows, cols), F32)] * 4,
        compiler_params=_cp(("arbitrary", "arbitrary")),
    )(block, *grads, *lands, w, m, v)


LANES = 128


def _seg_scratch(rows, d):
    return pltpu.VMEM((d // LANES, rows, LANES), F32)


def _to_segments(vals, scratch, out_ref):
    per = scratch.shape[1] // SEGS
    for c in range(scratch.shape[0]):
        cols = slice(c * LANES, (c + 1) * LANES)
        scratch[c] = vals[:, cols]
        for s in range(SEGS):
            out_ref[s, :, cols] = scratch.at[c][pl.ds(s, per, stride=SEGS), :].astype(out_ref.dtype)


def _from_segments(in_ref, scratch):
    per = scratch.shape[1] // SEGS
    for c in range(scratch.shape[0]):
        for s in range(SEGS):
            scratch.at[c][pl.ds(s, per, stride=SEGS), :] = in_ref[s, :, c * LANES:(c + 1) * LANES].astype(F32)
    return jnp.concatenate([scratch[c] for c in range(scratch.shape[0])], axis=1)


def _seg_view(a):
    return a.reshape(SEGS, a.shape[0] // SEGS, a.shape[1])


def _norm_fwd(name, h, z=None, g_post=None, g_pre=None, seg_z=False, seg_y=False):
    rows, d = h.shape
    tm = _tile(rows, 256)
    has_post, has_pre = z is not None, g_pre is not None
    nscratch = int(seg_z) + int(seg_y)

    def body(*refs):
        scratch = list(refs[len(refs) - nscratch:])
        it = iter(refs)
        hv = next(it)[...]
        if has_post:
            z_ref = next(it)
            zv = _from_segments(z_ref, scratch.pop(0)) if seg_z else z_ref[...].astype(F32)
            gp = next(it)[...]
        if has_pre:
            gq = next(it)[...]
        if has_post:
            r = lax.rsqrt(jnp.mean(zv * zv, axis=-1, keepdims=True) + RMS_EPS)
            hv = hv + (zv * r) * gp
            next(it)[...] = hv
        if has_pre:
            r = lax.rsqrt(jnp.mean(hv * hv, axis=-1, keepdims=True) + RMS_EPS)
            y = (hv * r) * gq
            if seg_y:
                _to_segments(y, scratch.pop(0), next(it))
            else:
                next(it)[...] = y.astype(BF16)

    row = pl.BlockSpec((tm, d), lambda i: (i, 0))
    seg = pl.BlockSpec((SEGS, tm // SEGS, d), lambda i: (0, i, 0))
    vec = pl.BlockSpec((1, d), lambda i: (0, 0))
    ins, in_specs, out_shape, out_specs = [h], [row], [], []
    if has_post:
        ins += [_seg_view(z) if seg_z else z, g_post]
        in_specs += [seg if seg_z else row, vec]
        out_shape.append(jax.ShapeDtypeStruct((rows, d), F32))
        out_specs.append(row)
    if has_pre:
        ins.append(g_pre)
        in_specs.append(vec)
        out_shape.append(jax.ShapeDtypeStruct((SEGS, rows // SEGS, d) if seg_y else (rows, d), BF16))
        out_specs.append(seg if seg_y else row)
    outs = pl.pallas_call(body, name=name, grid=(rows // tm,), in_specs=in_specs, out_specs=out_specs,
                          out_shape=out_shape, scratch_shapes=[_seg_scratch(tm, d)] * nscratch,
                          compiler_params=_cp(("parallel",)))(*ins)
    if seg_y:
        outs = list(outs[:-1]) + [outs[-1].reshape(rows, d)]
    return outs


def _rms_bwd_rows(x, g, dy):
    r = lax.rsqrt(jnp.mean(x * x, axis=-1, keepdims=True) + RMS_EPS)
    xn = x * r
    dg = jnp.sum(dy * xn, axis=0, keepdims=True)
    dxn = dy * g
    dx = r * (dxn - xn * jnp.mean(dxn * xn, axis=-1, keepdims=True))
    return dx, dg


def _norm_bwd(name, d_out, pre=None, post=None, seg_dy=False, seg_z=False):
    rows, d = d_out.shape
    tm = _tile(rows, 256)
    has_pre, has_post = pre is not None, post is not None
    nscratch = int(seg_dy) + 2 * int(seg_z)

    def body(*refs):
        scratch = list(refs[len(refs) - nscratch:])
        it = iter(refs)
        dres = next(it)[...]
        if has_pre:
            dy_ref = next(it)
            dy = _from_segments(dy_ref, scratch.pop(0)) if seg_dy else dy_ref[...].astype(F32)
            xp, gq = next(it)[...], next(it)[...]
        if has_post:
            z_ref = next(it)
            zv = _from_segments(z_ref, scratch.pop(0)) if seg_z else z_ref[...].astype(F32)
            gp = next(it)[...]
        first = pl.program_id(0) == 0
        if has_pre:
            dx, dg = _rms_bwd_rows(xp, gq, dy)
            dres = dres + dx
            next(it)[...] = dres
            dg_ref = next(it)

            @pl.when(first)
            def _():
                dg_ref[...] = jnp.zeros_like(dg_ref)
            dg_ref[...] += dg
        if has_post:
            dz, dg2 = _rms_bwd_rows(zv, gp, dres)
            if seg_z:
                _to_segments(dz, scratch.pop(0), next(it))
            else:
                next(it)[...] = dz.astype(BF16)
            dg2_ref = next(it)

            @pl.when(first)
            def _():
                dg2_ref[...] = jnp.zeros_like(dg2_ref)
            dg2_ref[...] += dg2

    row = pl.BlockSpec((tm, d), lambda i: (i, 0))
    seg = pl.BlockSpec((SEGS, tm // SEGS, d), lambda i: (0, i, 0))
    vec = pl.BlockSpec((1, d), lambda i: (0, 0))
    ins, in_specs, out_shape, out_specs = [d_out], [row], [], []
    if has_pre:
        d_y, x_pre, g_pre = pre
        ins += [_seg_view(d_y) if seg_dy else d_y, x_pre, g_pre]
        in_specs += [seg if seg_dy else row, row, vec]
        out_shape += [jax.ShapeDtypeStruct((rows, d), F32), jax.ShapeDtypeStruct((1, d), F32)]
        out_specs += [row, vec]
    if has_post:
        z, g_post = post
        ins += [_seg_view(z) if seg_z else z, g_post]
        in_specs += [seg if seg_z else row, vec]
        out_shape += [jax.ShapeDtypeStruct((SEGS, rows // SEGS, d) if seg_z else (rows, d), BF16),
                      jax.ShapeDtypeStruct((1, d), F32)]
        out_specs += [seg if seg_z else row, vec]
    outs = pl.pallas_call(body, name=name, grid=(rows // tm,), in_specs=in_specs, out_specs=out_specs,
                          out_shape=out_shape, scratch_shapes=[_seg_scratch(tm, d)] * nscratch,
                          compiler_params=_cp(("arbitrary",)))(*ins)
    if seg_z:
        outs = list(outs)
        outs[-2] = outs[-2].reshape(rows, d)
    return outs


def _loss_grad(name, y, target):
    rows, d = y.shape
    tm = _tile(rows, 256)

    def body(y_ref, t_ref, dy_ref, loss_ref):
        err = y_ref[...] - t_ref[...]
        dy_ref[...] = err * (1.0 / d)

        @pl.when(pl.program_id(0) == 0)
        def _():
            loss_ref[...] = jnp.zeros_like(loss_ref)
        loss_ref[...] += jnp.full(loss_ref.shape, (0.5 / d) * jnp.sum(err * err), F32)

    row = pl.BlockSpec((tm, d), lambda i: (i, 0))
    return pl.pallas_call(
        body, name=name, grid=(rows // tm,), in_specs=[row, row],
        out_specs=[row, pl.BlockSpec((1, 128), lambda i: (0, 0))],
        out_shape=[jax.ShapeDtypeStruct((rows, d), F32), jax.ShapeDtypeStruct((1, 128), F32)],
        compiler_params=_cp(("arbitrary",)))(y, target)


NT_DIMS = (((1,), (1,)), ((), ()))
TN_DIMS = (((0,), (0,)), ((), ()))


def _mm_nn_blk(name, a, wblk, relu2=False, after=()):
    m, k = a.shape
    nb = wblk.shape[2]
    tm = _tile(m, 1024)
    after = list(after)

    def body(a_ref, w_ref, *rest):
        r = jnp.dot(a_ref[...], w_ref[...], preferred_element_type=F32)
        if relu2:
            rr = jnp.maximum(r, 0.0)
            r = rr * rr
        rest[-1][...] = r.astype(BF16)

    return pl.pallas_call(
        body, name=name, grid=(NDEV, m // tm),
        in_specs=[pl.BlockSpec((tm, k), lambda d, i: (i, 0)), pl.BlockSpec((None, k, nb), lambda d, i: (d, 0, 0))]
        + [ANY] * len(after),
        out_specs=pl.BlockSpec((tm, nb), lambda d, i: (i, d)),
        out_shape=jax.ShapeDtypeStruct((m, NDEV * nb), BF16),
        compiler_params=_cp(("parallel", "parallel"), VMEM_MM))(a, wblk, *after)


def _accumulate(acc, o_ref, r, step, last):
    if acc is None:
        o_ref[...] = r.astype(o_ref.dtype)
        return

    @pl.when(step == 0)
    def _():
        acc[...] = r

    @pl.when(jnp.logical_and(step > 0, step < last))
    def _():
        acc[...] += r

    @pl.when(jnp.logical_and(step > 0, step == last))
    def _():
        o_ref[...] = (acc[...] + r).astype(o_ref.dtype)


def _mm_nn(name, a, w, after=()):
    m, kb = a.shape
    n = w.shape[1]
    one_step = kb <= 2048
    tm = _tile(m, 512 if one_step else 1024)
    tk = kb if one_step else _tile(kb, 4096)
    tn = n if one_step else _tile(n, 1024)
    steps = kb // tk
    after = list(after)

    def body(a_ref, w_ref, *rest):
        o_ref, scratch = rest[len(after)], rest[len(after) + 1:]
        r = jnp.dot(a_ref[...], w_ref[...], preferred_element_type=F32)
        _accumulate(scratch[0] if scratch else None, o_ref, r, pl.program_id(2), steps - 1)

    return pl.pallas_call(
        body, name=name, grid=(m // tm, n // tn, steps),
        in_specs=[pl.BlockSpec((tm, tk), lambda i, j, s: (i, s)), pl.BlockSpec((tk, tn), lambda i, j, s: (s, j))]
        + [ANY] * len(after),
        out_specs=pl.BlockSpec((tm, tn), lambda i, j, s: (i, j)),
        out_shape=jax.ShapeDtypeStruct((m, n), BF16),
        scratch_shapes=[pltpu.VMEM((tm, tn), F32)] if steps > 1 else [],
        compiler_params=_cp(("parallel", "parallel", "arbitrary"), VMEM_MM))(a, w, *after)


def _mm_nt_rows(name, dy, w, act=None, out_dtype=BF16):
    m, n = dy.shape
    kw = w.shape[0]
    tm, tkw = _tile(m, 1024), _tile(kw, 1024)

    def body(dy_ref, w_ref, *rest):
        r = lax.dot_general(dy_ref[...], w_ref[...], NT_DIMS, preferred_element_type=F32)
        if act is None:
            rest[0][...] = r.astype(out_dtype)
        else:
            rest[1][...] = (r * (2.0 * jnp.sqrt(rest[0][...].astype(F32)))).astype(BF16)

    ins = [dy, w]
    in_specs = [pl.BlockSpec((tm, n), lambda j, i: (i, 0)), pl.BlockSpec((tkw, n), lambda j, i: (j, 0))]
    if act is not None:
        ins.append(act)
        in_specs.append(pl.BlockSpec((tm, tkw), lambda j, i: (i, j)))
    return pl.pallas_call(
        body, name=name, grid=(kw // tkw, m // tm), in_specs=in_specs,
        out_specs=pl.BlockSpec((tm, tkw), lambda j, i: (i, j)),
        out_shape=jax.ShapeDtypeStruct((m, kw), out_dtype if act is None else BF16),
        compiler_params=_cp(("parallel", "parallel"), VMEM_MM))(*ins)


def _mm_nt_blk(name, dy, wblk, after=None):
    m = dy.shape[0]
    _, kw, nb = wblk.shape
    tm, tkw, per = _tile(m, 1024), _tile(kw, 1024), 4

    extra = list(after or ())

    def body(dy_ref, w_ref, *rest):
        o_ref, acc = rest[len(extra):]
        r = lax.dot_general(dy_ref[:, :nb], w_ref[0], NT_DIMS, preferred_element_type=F32)
        for t in range(1, per):
            r = r + lax.dot_general(dy_ref[:, t * nb:(t + 1) * nb], w_ref[t], NT_DIMS, preferred_element_type=F32)
        _accumulate(acc, o_ref, r, pl.program_id(2), NDEV // per - 1)

    return pl.pallas_call(
        body, name=name, grid=(m // tm, kw // tkw, NDEV // per),
        in_specs=[pl.BlockSpec((tm, per * nb), lambda i, j, s: (i, s)),
                  pl.BlockSpec((per, tkw, nb), lambda i, j, s: (s, j, 0))] + [ANY] * len(extra),
        out_specs=pl.BlockSpec((tm, tkw), lambda i, j, s: (i, j)),
        out_shape=jax.ShapeDtypeStruct((m, kw), BF16),
        scratch_shapes=[pltpu.VMEM((tm, tkw), F32)],
        compiler_params=_cp(("parallel", "parallel", "arbitrary"), VMEM_MM))(dy, wblk, *extra)


def _mm_tn(name, x, dy, nb=None):
    t, mx = x.shape
    n = dy.shape[1]
    tmx = _tile(mx, 512)
    tn = nb if nb is not None else _tile(n, 1024)

    def body(x_ref, dy_ref, o_ref):
        o_ref[...] = lax.dot_general(x_ref[...], dy_ref[...], TN_DIMS, preferred_element_type=F32).astype(BF16)

    if nb is None:
        out_shape = jax.ShapeDtypeStruct((mx, n), BF16)
        out_spec = pl.BlockSpec((tmx, tn), lambda j, i: (i, j))
    else:
        out_shape = jax.ShapeDtypeStruct((NDEV, mx, nb), BF16)
        out_spec = pl.BlockSpec((None, tmx, nb), lambda j, i: (j, i, 0))
    return pl.pallas_call(
        body, name=name, grid=(n // tn, mx // tmx),
        in_specs=[pl.BlockSpec((t, tmx), lambda j, i: (0, i)), pl.BlockSpec((t, tn), lambda j, i: (0, j))],
        out_specs=out_spec, out_shape=out_shape,
        compiler_params=_cp(("parallel", "parallel"), VMEM_MM))(x, dy)


def _gelu(x):
    return 0.5 * x * (1.0 + jnp.tanh(GELU_C * (x + GELU_A * (x * x * x))))


def _gelu_grad(x):
    t = jnp.tanh(GELU_C * (x + GELU_A * (x * x * x)))
    return 0.5 * (1.0 + t) + 0.5 * x * (1.0 - t * t) * (GELU_C * (1.0 + 3.0 * GELU_A * (x * x)))


def _layernorm(a):
    mu = jnp.mean(a, axis=-1, keepdims=True)
    ac = a - mu
    rstd = lax.rsqrt(jnp.mean(ac * ac, axis=-1, keepdims=True) + LN_EPS)
    return ac * rstd, rstd


def _shift_rows(z, halo, k):
    zr = pltpu.roll(z, k, 0)
    hr = pltpu.roll(halo, k, 0)
    row = lax.broadcasted_iota(jnp.int32, hr.shape, 0)
    top = jnp.where(row < k, hr, zr[:HALO])
    return jnp.concatenate([top, zr[HALO:]], axis=0)


def _shift_rows_up(z, halo, k):
    rows = z.shape[0]
    zr = pltpu.roll(z, rows - k, 0)
    hr = pltpu.roll(halo, HALO - k, 0)
    row = lax.broadcasted_iota(jnp.int32, hr.shape, 0)
    bot = jnp.where(row >= HALO - k, hr, zr[rows - HALO:])
    return jnp.concatenate([zr[:rows - HALO], bot], axis=0)


def _causal_mask():
    t = lax.broadcasted_iota(jnp.int32, (CHUNK, CHUNK), 0)
    s = lax.broadcasted_iota(jnp.int32, (CHUNK, CHUNK), 1)
    return s <= t


def _gate_specs(tm, rows, width):
    per = tm // HALO_BF16
    last = rows // HALO_BF16 - 1
    cur = pl.BlockSpec((tm, width), lambda i: (i, 0))
    prev = pl.BlockSpec((HALO_BF16, width), lambda i: (jnp.maximum(i * per - 1, 0), 0))
    nxt = pl.BlockSpec((HALO_BF16, width), lambda i: (jnp.minimum((i + 1) * per, last), 0))
    return cur, prev, nxt


def _cols(ref, lo, hi):
    return ref[:, lo:hi].astype(F32)


def _halo_before(ref, lo, hi):
    return ref[:, lo:hi].astype(F32)[HALO_BF16 - HALO:]


def _halo_after(ref, lo, hi):
    return ref[:, lo:hi].astype(F32)[:HALO]


def _gate_fwd(name, proj, w_s, b_st, cw):
    rows, width = proj.shape
    w = width // 5
    groups = w // CHUNK
    tm = _tile(rows, 256)
    cur, prev, _ = _gate_specs(tm, rows, width)

    def body(p_ref, h_ref, ws_ref, b_ref, cw_ref, o_ref):
        mask = _causal_mask()
        au = _gelu(_cols(p_ref, 0, w))
        vn, _ = _layernorm(_gelu(_cols(p_ref, w, 2 * w)))
        vn = vn.astype(BF16)
        for g in range(groups):
            wc = jnp.where(mask, ws_ref[g], 0.0).astype(BF16)
            cols = slice(g * CHUNK, (g + 1) * CHUNK)
            for ch in range(tm // CHUNK):
                rws = slice(ch * CHUNK, (ch + 1) * CHUNK)
                mixed = jnp.dot(wc, vn[rws, cols], preferred_element_type=F32) + b_ref[:, g:g + 1]
                o_ref[rws, cols] = (au[rws, cols] * mixed).astype(BF16)
        z = _cols(p_ref, 3 * w, 4 * w) * _cols(p_ref, 4 * w, 5 * w)
        zh = _halo_before(h_ref, 3 * w, 4 * w) * _halo_before(h_ref, 4 * w, 5 * w)
        zh = jnp.where(pl.program_id(0) == 0, 0.0, zh)
        y = cw_ref[0:1, :] * _shift_rows(z, zh, 2) + cw_ref[1:2, :] * _shift_rows(z, zh, 1) + cw_ref[2:3, :] * z
        o_ref[:, w:2 * w] = (_cols(p_ref, 2 * w, 3 * w) * y).astype(BF16)

    full = lambda a: pl.BlockSpec(a.shape, lambda i: (0,) * a.ndim)
    return pl.pallas_call(
        body, name=name, grid=(rows // tm,),
        in_specs=[cur, prev, full(w_s), full(b_st), full(cw)],
        out_specs=pl.BlockSpec((tm, 2 * w), lambda i: (i, 0)),
        out_shape=jax.ShapeDtypeStruct((rows, 2 * w), BF16),
        compiler_params=_cp(("parallel",)))(proj, proj, w_s, b_st, cw)


def _gate_bwd(name, proj, d_ab, w_s, b_st, cw):
    rows, width = proj.shape
    w = width // 5
    groups = w // CHUNK
    tm = _tile(rows, 256)
    cur, prev, nxt = _gate_specs(tm, rows, width)
    dcur, _, dnxt = _gate_specs(tm, rows, 2 * w)

    def body(p_ref, ph_ref, pn_ref, d_ref, dn_ref, ws_ref, b_ref, cw_ref, o_ref, dws_ref, dbs_ref, dcw_ref):
        i = pl.program_id(0)

        @pl.when(i == 0)
        def _():
            dws_ref[...] = jnp.zeros_like(dws_ref)
            dbs_ref[...] = jnp.zeros_like(dbs_ref)
            dcw_ref[...] = jnp.zeros_like(dcw_ref)

        mask = _causal_mask()
        u, v = _cols(p_ref, 0, w), _cols(p_ref, w, 2 * w)
        au, av = _gelu(u), _gelu(v)
        vn, rstd = _layernorm(av)
        vnb = vn.astype(BF16)
        d_a = _cols(d_ref, 0, w)
        d_mixed = (d_a * au).astype(BF16)
        ones = jnp.ones((HALO, CHUNK), BF16)
        d_vn_cols = []
        d_au_cols = []
        for g in range(groups):
            wc = jnp.where(mask, ws_ref[g], 0.0).astype(BF16)
            cols = slice(g * CHUNK, (g + 1) * CHUNK)
            dw = jnp.zeros((CHUNK, CHUNK), F32)
            db = jnp.zeros((HALO, CHUNK), F32)
            d_vn_rows, d_au_rows = [], []
            for ch in range(tm // CHUNK):
                rws = slice(ch * CHUNK, (ch + 1) * CHUNK)
                mixed = jnp.dot(wc, vnb[rws, cols], preferred_element_type=F32) + b_ref[:, g:g + 1]
                d_au_rows.append(d_a[rws, cols] * mixed)
                dm = d_mixed[rws, cols]
                dw = dw + lax.dot_general(dm, vnb[rws, cols], NT_DIMS, preferred_element_type=F32)
                db = db + lax.dot_general(ones, dm, NT_DIMS, preferred_element_type=F32)
                d_vn_rows.append(lax.dot_general(wc, dm, TN_DIMS, preferred_element_type=F32))
            dws_ref[g] += jnp.where(mask, dw, 0.0)
            dbs_ref[g:g + 1, :] += db[0:1, :]
            d_vn_cols.append(jnp.concatenate(d_vn_rows, axis=0))
            d_au_cols.append(jnp.concatenate(d_au_rows, axis=0))
        d_vn = jnp.concatenate(d_vn_cols, axis=1)
        d_au = jnp.concatenate(d_au_cols, axis=1)
        d_av = rstd * (d_vn - jnp.mean(d_vn, axis=-1, keepdims=True)
                       - vn * jnp.mean(d_vn * vn, axis=-1, keepdims=True))
        o_ref[:, 0:w] = (d_au * _gelu_grad(u)).astype(BF16)
        o_ref[:, w:2 * w] = (d_av * _gelu_grad(v)).astype(BF16)

        gb, gc, bx = _cols(p_ref, 2 * w, 3 * w), _cols(p_ref, 3 * w, 4 * w), _cols(p_ref, 4 * w, 5 * w)
        z = gc * bx
        zh = jnp.where(i == 0, 0.0, _halo_before(ph_ref, 3 * w, 4 * w) * _halo_before(ph_ref, 4 * w, 5 * w))
        z1, z2 = _shift_rows(z, zh, 1), _shift_rows(z, zh, 2)
        d_b = _cols(d_ref, w, 2 * w)
        y = cw_ref[0:1, :] * z2 + cw_ref[1:2, :] * z1 + cw_ref[2:3, :] * z
        dy = d_b * gb
        dyn = jnp.where(i == pl.num_programs(0) - 1, 0.0,
                        _halo_after(dn_ref, w, 2 * w) * _halo_after(pn_ref, 2 * w, 3 * w))
        dz = (cw_ref[2:3, :] * dy + cw_ref[1:2, :] * _shift_rows_up(dy, dyn, 1)
              + cw_ref[0:1, :] * _shift_rows_up(dy, dyn, 2))
        dcw_ref[0:1, :] += jnp.sum(dy * z2, axis=0, keepdims=True)
        dcw_ref[1:2, :] += jnp.sum(dy * z1, axis=0, keepdims=True)
        dcw_ref[2:3, :] += jnp.sum(dy * z, axis=0, keepdims=True)
        o_ref[:, 2 * w:3 * w] = (d_b * y).astype(BF16)
        o_ref[:, 3 * w:4 * w] = (dz * bx).astype(BF16)
        o_ref[:, 4 * w:5 * w] = (dz * gc).astype(BF16)

    full = lambda a: pl.BlockSpec(a.shape, lambda i: (0,) * a.ndim)
    acc = lambda shape: pl.BlockSpec(shape, lambda i: (0,) * len(shape))
    return pl.pallas_call(
        body, name=name, grid=(rows // tm,),
        in_specs=[cur, prev, nxt, dcur, dnxt, full(w_s), full(b_st), full(cw)],
        out_specs=[pl.BlockSpec((tm, width), lambda i: (i, 0)), acc((groups, CHUNK, CHUNK)),
                   acc((groups, CHUNK)), acc((HALO, w))],
        out_shape=[jax.ShapeDtypeStruct((rows, width), BF16), jax.ShapeDtypeStruct((groups, CHUNK, CHUNK), F32),
                   jax.ShapeDtypeStruct((groups, CHUNK), F32), jax.ShapeDtypeStruct((HALO, w), F32)],
        compiler_params=_cp(("arbitrary",), VMEM_MM))(proj, proj, proj, d_ab, d_ab, w_s, b_st, cw)


def _flat(x):
    return x.reshape(-1, x.shape[-1])


def _rope(t, cosf, sins):
    t2 = _flat(t)
    return (t2 * _flat(cosf) + pltpu.roll(t2, HEAD // 2, 1) * _flat(sins)).reshape(t.shape)


def _rope_bwd(dt, cosf, sins):
    d2 = _flat(dt)
    return (d2 * _flat(cosf) + pltpu.roll(d2 * _flat(sins), HEAD // 2, 1)).reshape(dt.shape)


ATT_UNITS = ATT_TILE // CHUNK


def _attn_units(phases):
    for b, d in enumerate(DILATIONS):
        blocks = ATT_TILE // (CHUNK * d)
        for visit in phases:
            for r in range(d):
                if blocks <= ATT_UNROLL:
                    for j in range(blocks):
                        visit(b, d, r, j, r * blocks + j)
                else:
                    def step(jj, carry, b=b, d=d, r=r, visit=visit, blocks=blocks):
                        for u in range(ATT_UNROLL):
                            j = jj * ATT_UNROLL + u
                            visit(b, d, r, j, r * blocks + j)
                        return carry
                    lax.fori_loop(0, blocks // ATT_UNROLL, step, 0)


class _Unit:
    def __init__(self, d, r, j):
        self.segs = [r + d * k for k in range(SEGS // d)]
        self.w = CHUNK * d // SEGS
        q0 = j * self.w
        self.q0 = q0 if isinstance(q0, int) else pl.multiple_of(q0, HALO)
        k0 = CHUNK + (j - 1) * self.w
        self.k0 = k0 if isinstance(k0, int) else pl.multiple_of(k0, HALO)

    def queries(self, ref):
        return _chunks(ref, self.segs, self.q0, self.w)

    def keys(self, ref):
        return _chunks(ref, self.segs, self.k0, 2 * self.w)

    def put_queries(self, ref, val, add=False):
        _put_chunks(ref, self.segs, self.q0, self.w, val, add)

    def put_keys(self, ref, val, add=False):
        _put_chunks(ref, self.segs, self.k0, 2 * self.w, val, add)


def _chunks(ref, segs, start, size):
    parts = [ref[s, pl.ds(start, size), :] for s in segs]
    return parts[0] if len(parts) == 1 else jnp.concatenate(parts, axis=0)


def _put_chunks(ref, segs, start, size, val, add):
    for k, s in enumerate(segs):
        piece = val[k * size:(k + 1) * size]
        if add:
            ref[s, pl.ds(start, size), :] += piece
        else:
            ref[s, pl.ds(start, size), :] = piece


def _band_bias():
    qi = lax.broadcasted_iota(jnp.int32, (CHUNK, 2 * CHUNK), 0)
    ki = lax.broadcasted_iota(jnp.int32, (CHUNK, 2 * CHUNK), 1)
    tables = []
    for d in DILATIONS:
        nseg, w = SEGS // d, CHUNK * d // SEGS
        pos_q = nseg * (qi % w) + qi // w
        pos_k = nseg * (ki % (2 * w) - w) + ki // (2 * w)
        band = (pos_q >= pos_k) & (pos_q - pos_k <= CHUNK)
        tables += [jnp.where(band, 0.0, -jnp.inf), jnp.where(band & (pos_k >= 0), 0.0, -jnp.inf)]
    return jnp.stack(tables).astype(F32)


def _bias_spec():
    return pl.BlockSpec((2 * len(DILATIONS), CHUNK, 2 * CHUNK), lambda h, n: (0, 0, 0))


def _unit_bias(bias, b, n, j):
    if isinstance(j, int) and j != 0:
        return bias[2 * b]
    return bias[2 * b + jnp.where(jnp.logical_and(n == 0, j == 0), 1, 0)]


def _attn_in_specs(heads):
    blk = (SEGS, CHUNK, HEAD)
    prev = lambda n: jnp.maximum(n - 1, 0)
    return [
        pl.BlockSpec(blk, lambda h, n: (0, n, h)),
        pl.BlockSpec(blk, lambda h, n: (0, n, heads + h)),
        pl.BlockSpec(blk, lambda h, n: (0, prev(n), heads + h)),
        pl.BlockSpec(blk, lambda h, n: (0, n, 2 * heads + h)),
        pl.BlockSpec(blk, lambda h, n: (0, prev(n), 2 * heads + h)),
        pl.BlockSpec(blk, lambda h, n: (0, n, 0)),
        pl.BlockSpec(blk, lambda h, n: (0, n, 0)),
        pl.BlockSpec(blk, lambda h, n: (0, prev(n), 0)),
        pl.BlockSpec(blk, lambda h, n: (0, prev(n), 0)),
    ]


def _attn_load(q_ref, kc_ref, kp_ref, vc_ref, vp_ref, cc_ref, sc_ref, cp_ref, sp_ref, qr, kcat, vcat):
    qr[...] = _rope(q_ref[...].astype(F32), cc_ref[...], sc_ref[...])
    kcat[:, pl.ds(0, CHUNK), :] = _rope(kp_ref[...].astype(F32), cp_ref[...], sp_ref[...])
    kcat[:, pl.ds(CHUNK, CHUNK), :] = _rope(kc_ref[...].astype(F32), cc_ref[...], sc_ref[...])
    vcat[:, pl.ds(0, CHUNK), :] = vp_ref[...].astype(F32)
    vcat[:, pl.ds(CHUNK, CHUNK), :] = vc_ref[...].astype(F32)


def _attn_fwd(name, qkv, cosf, sins, bias):
    t = qkv.shape[0]
    heads = qkv.shape[1] // (3 * HEAD)
    scale = HEAD ** -0.5
    nbr = len(DILATIONS)

    def body(q_ref, kc_ref, kp_ref, vc_ref, vp_ref, cc_ref, sc_ref, cp_ref, sp_ref, bias, o_ref, lse_ref,
             qr, kcat, vcat, obr, lbr, pn):
        n = pl.program_id(1)
        _attn_load(q_ref, kc_ref, kp_ref, vc_ref, vp_ref, cc_ref, sc_ref, cp_ref, sp_ref, qr, kcat, vcat)

        def probs(b, d, r, j, u):
            unit = _Unit(d, r, j)
            s = lax.dot_general(unit.queries(qr).astype(BF16), unit.keys(kcat).astype(BF16), NT_DIMS,
                                preferred_element_type=F32) * scale + _unit_bias(bias, b, n, j)
            mx = jnp.max(s, axis=-1, keepdims=True)
            p = jnp.exp(s - mx)
            den = jnp.sum(p, axis=-1, keepdims=True)
            pn[u] = (p * (1.0 / den)).astype(BF16)
            unit.put_queries(lbr.at[b], jnp.broadcast_to(mx + jnp.log(den), (CHUNK, HEAD)))

        def values(b, d, r, j, u):
            unit = _Unit(d, r, j)
            unit.put_queries(obr.at[b], jnp.dot(pn[u], unit.keys(vcat).astype(BF16), preferred_element_type=F32))

        _attn_units([probs, values])
        ls = [lbr[b] for b in range(nbr)]
        top = functools.reduce(jnp.maximum, ls)
        ws = [jnp.exp(l - top) for l in ls]
        tot = functools.reduce(jnp.add, ws)
        inv = 1.0 / tot
        o = (ws[0] * inv) * obr[0]
        for b in range(1, nbr):
            o = o + (ws[b] * inv) * obr[b]
        o_ref[...] = o.astype(BF16)
        lse_ref[...] = top + jnp.log(tot)

    blk = (SEGS, CHUNK, HEAD)
    keys = pltpu.VMEM((SEGS, 2 * CHUNK, HEAD), F32)
    tile = pl.BlockSpec(blk, lambda h, n: (0, n, h))
    seg = t // SEGS
    qkv3, cos3, sin3 = _seg_view(qkv), _seg_view(cosf), _seg_view(sins)
    o, lse = pl.pallas_call(
        body, name=name, grid=(heads, t // ATT_TILE), in_specs=_attn_in_specs(heads) + [_bias_spec()],
        out_specs=[tile, tile],
        out_shape=[jax.ShapeDtypeStruct((SEGS, seg, heads * HEAD), BF16),
                   jax.ShapeDtypeStruct((SEGS, seg, heads * HEAD), F32)],
        scratch_shapes=[pltpu.VMEM(blk, F32), keys, keys,
                        pltpu.VMEM((nbr,) + blk, F32), pltpu.VMEM((nbr,) + blk, F32),
                        pltpu.VMEM((ATT_UNITS, CHUNK, 2 * CHUNK), BF16)],
        compiler_params=_cp(("parallel", "parallel"), VMEM_MM),
    )(qkv3, qkv3, qkv3, qkv3, qkv3, cos3, sin3, cos3, sin3, bias)
    return o.reshape(t, heads * HEAD), lse.reshape(t, heads * HEAD)


def _attn_bwd(name, qkv, cosf, sins, bias, d_o, o, lse):
    t = qkv.shape[0]
    heads = qkv.shape[1] // (3 * HEAD)
    scale = HEAD ** -0.5

    def body(q_ref, kc_ref, kp_ref, vc_ref, vp_ref, cc_ref, sc_ref, cp_ref, sp_ref, do_ref, o_ref, lse_ref, bias,
             dq_ref, dko_ref, dkp_ref, dvo_ref, dvp_ref, qr, kcat, vcat, dq_acc, dk_acc, dv_acc, delta, ps, dss):
        n = pl.program_id(1)
        _attn_load(q_ref, kc_ref, kp_ref, vc_ref, vp_ref, cc_ref, sc_ref, cp_ref, sp_ref, qr, kcat, vcat)
        dq_acc[...] = jnp.zeros_like(dq_acc)
        dk_acc[...] = jnp.zeros_like(dk_acc)
        dv_acc[...] = jnp.zeros_like(dv_acc)
        delta[...] = jnp.broadcast_to(
            jnp.sum(do_ref[...] * o_ref[...].astype(F32), axis=-1, keepdims=True), delta.shape)

        def probs(b, d, r, j, u):
            unit = _Unit(d, r, j)
            s = lax.dot_general(unit.queries(qr).astype(BF16), unit.keys(kcat).astype(BF16), NT_DIMS,
                                preferred_element_type=F32) * scale + _unit_bias(bias, b, n, j)
            ps[u] = jnp.exp(s - unit.queries(lse_ref)[:, 0:1]).astype(BF16)

        def score_grads(b, d, r, j, u):
            unit = _Unit(d, r, j)
            dp = lax.dot_general(unit.queries(do_ref).astype(BF16), unit.keys(vcat).astype(BF16), NT_DIMS,
                                 preferred_element_type=F32)
            dss[u] = (ps[u].astype(F32) * (dp - unit.queries(delta)[:, 0:1]) * scale).astype(BF16)

        def input_grads(b, d, r, j, u):
            unit = _Unit(d, r, j)
            ds = dss[u]
            unit.put_queries(dq_acc, jnp.dot(ds, unit.keys(kcat).astype(BF16), preferred_element_type=F32), add=True)
            unit.put_keys(dk_acc, lax.dot_general(ds, unit.queries(qr).astype(BF16), TN_DIMS,
                                                  preferred_element_type=F32), add=True)
            unit.put_keys(dv_acc, lax.dot_general(ps[u], unit.queries(do_ref).astype(BF16), TN_DIMS,
                                                  preferred_element_type=F32), add=True)

        _attn_units([probs, score_grads, input_grads])
        dq_ref[...] = _rope_bwd(dq_acc[...], cc_ref[...], sc_ref[...]).astype(BF16)
        dkp_ref[...] = _rope_bwd(dk_acc[:, pl.ds(0, CHUNK), :], cp_ref[...], sp_ref[...]).astype(BF16)
        dko_ref[...] = _rope_bwd(dk_acc[:, pl.ds(CHUNK, CHUNK), :], cc_ref[...], sc_ref[...]).astype(BF16)
        dvp_ref[...] = dv_acc[:, pl.ds(0, CHUNK), :].astype(BF16)
        dvo_ref[...] = dv_acc[:, pl.ds(CHUNK, CHUNK), :].astype(BF16)

    blk = (SEGS, CHUNK, HEAD)
    tile = pl.BlockSpec(blk, lambda h, n: (0, n, h))
    big = pltpu.VMEM((SEGS, 2 * CHUNK, HEAD), F32)
    seg = t // SEGS
    qkv3, cos3, sin3 = _seg_view(qkv), _seg_view(cosf), _seg_view(sins)
    return pl.pallas_call(
        body, name=name, grid=(heads, t // ATT_TILE),
        in_specs=_attn_in_specs(heads) + [tile, tile, tile, _bias_spec()],
        out_specs=[tile] * 5,
        out_shape=[jax.ShapeDtypeStruct((SEGS, seg, heads * HEAD), BF16)] * 5,
        scratch_shapes=[pltpu.VMEM(blk, F32), big, big, pltpu.VMEM(blk, F32), big, big, pltpu.VMEM(blk, F32),
                        pltpu.VMEM((ATT_UNITS, CHUNK, 2 * CHUNK), BF16), pltpu.VMEM((ATT_UNITS, CHUNK, 2 * CHUNK), BF16)],
        compiler_params=_cp(("parallel", "parallel"), 60 << 20),
    )(qkv3, qkv3, qkv3, qkv3, qkv3, cos3, sin3, cos3, sin3, _seg_view(d_o), _seg_view(o), _seg_view(lse), bias)


def _attn_merge(name, dq, dk_own, dk_prev, dv_own, dv_prev):
    _, seg, hd = dq.shape
    nt = seg // CHUNK
    tw = _tile(hd, 512)

    def body(dq_ref, dko_ref, dkn_ref, dvo_ref, dvn_ref, o_ref):
        last = pl.program_id(0) == nt - 1
        part = pl.program_id(1)

        @pl.when(part == 0)
        def _():
            o_ref[...] = dq_ref[...]

        @pl.when(part == 1)
        def _():
            o_ref[...] = (dko_ref[...].astype(F32) + jnp.where(last, 0.0, dkn_ref[...].astype(F32))).astype(BF16)

        @pl.when(part == 2)
        def _():
            o_ref[...] = (dvo_ref[...].astype(F32) + jnp.where(last, 0.0, dvn_ref[...].astype(F32))).astype(BF16)

    blk = (SEGS, CHUNK, tw)

    def own(part):
        return pl.BlockSpec(blk, lambda n, p, c: (0, jnp.where(p == part, n, 0), jnp.where(p == part, c, 0)))

    def nxt(part):
        return pl.BlockSpec(blk, lambda n, p, c: (0, jnp.where(p == part, jnp.minimum(n + 1, nt - 1), 0),
                                                  jnp.where(p == part, c, 0)))

    per = hd // tw
    return pl.pallas_call(
        body, name=name, grid=(nt, 3, per), in_specs=[own(0), own(1), nxt(1), own(2), nxt(2)],
        out_specs=pl.BlockSpec(blk, lambda n, p, c: (0, n, p * per + c)),
        out_shape=jax.ShapeDtypeStruct((SEGS, seg, 3 * hd), BF16),
        compiler_params=_cp(("parallel", "parallel", "parallel"), VMEM_MM),
    )(dq, dk_own, dk_prev, dv_own, dv_prev).reshape(SEGS * seg, 3 * hd)


def _sum_parts(name, parts):
    nparts, rows, cols = parts.shape
    tr = _tile(rows, 256)

    def body(p_ref, o_ref):
        s = p_ref[0]
        for k in range(1, nparts):
            s = s + p_ref[k]
        o_ref[...] = s

    return pl.pallas_call(
        body, name=name, grid=(rows // tr,),
        in_specs=[pl.BlockSpec((nparts, tr, cols), lambda i: (0, i, 0))],
        out_specs=pl.BlockSpec((tr, cols), lambda i: (i, 0)),
        out_shape=jax.ShapeDtypeStruct((rows, cols), F32),
        compiler_params=_cp(("parallel",)))(parts)


def _rows128(a, pad_to=8):
    flat = a.reshape(-1)
    rows = -(-flat.shape[0] // 128)
    rows = -(-rows // pad_to) * pad_to
    flat = jnp.pad(flat, (0, rows * 128 - flat.shape[0]))
    return flat.reshape(rows, 128)


def _pack(arrays):
    return jnp.concatenate([_rows128(a) for a in arrays], axis=0)


def _unpack(packed, like):
    out, at = [], 0
    for a in like:
        size = 1
        for s in a.shape:
            size *= s
        rows = -(-(-(-size // 128)) // 8) * 8
        out.append(packed[at:at + rows].reshape(-1)[:size].reshape(a.shape))
        at += rows
    return out


def kernel(x, norm_mix_pre, norm_mix_post, norm_mlp_pre, norm_mlp_post, w_in_ab, w_spatial, b_spatial, conv_w, w_out_ab, w_qkv, w_o, w_up, w_down, loss_target, m_norm_mix_pre, m_norm_mix_post, m_norm_mlp_pre, m_norm_mlp_post, m_w_in_ab, m_w_spatial, m_b_spatial, m_conv_w, m_w_out_ab, m_w_qkv, m_w_o, m_w_up, m_w_down, v_norm_mix_pre, v_norm_mix_post, v_norm_mlp_pre, v_norm_mlp_post, v_w_in_ab, v_w_spatial, v_b_spatial, v_conv_w, v_w_out_ab, v_w_qkv, v_w_o, v_w_up, v_w_down):
    depth = norm_mix_pre.shape[0]
    seq, dm = x.shape[1], x.shape[2]
    h0 = x.reshape(seq, dm)
    target = loss_target.reshape(seq, dm)
    ax, ay, ac = lax.axis_index("x"), lax.axis_index("y"), lax.axis_index("c")
    my_block = 4 * ax + 2 * ay + ac
    block = jnp.reshape(my_block, (1,)).astype(jnp.int32)

    half = HEAD // 2
    inv_freq = ROPE_THETA ** (-jnp.arange(half, dtype=F32) * 2.0 / HEAD)
    ang = jnp.arange(seq, dtype=jnp.int32).astype(F32)[:, None] * inv_freq[None, :]
    ang = ang.reshape(seq // SEGS, SEGS, half).transpose(1, 0, 2).reshape(seq, half)
    cosf = jnp.concatenate([jnp.cos(ang), jnp.cos(ang)], axis=-1)
    sins = jnp.concatenate([-jnp.sin(ang), jnp.sin(ang)], axis=-1)
    band_bias = _band_bias()

    big = {"w_in_ab": w_in_ab, "w_out_ab": w_out_ab, "w_qkv": w_qkv, "w_o": w_o, "w_up": w_up, "w_down": w_down}
    use_order = []
    for l in range(depth):
        use_order += [("w_in_ab", l // 2), ("w_out_ab", l // 2)] if l % 2 == 0 else [("w_qkv", l // 2), ("w_o", l // 2)]
        use_order += [("w_up", l), ("w_down", l)]
    n_even = w_in_ab.shape[0]
    cw_rows = jnp.pad(conv_w.reshape(n_even * CONV_TAPS, conv_w.shape[2]), ((0, HALO - (n_even * CONV_TAPS) % HALO), (0, 0)))
    cw_gathered = _all_gather("ag_conv", [cw_rows])[0]
    first = [k for k in use_order if k in (("w_in_ab", 0), ("w_out_ab", 0), ("w_up", 0), ("w_down", 0))]
    rest = [k for k in use_order if k not in first]
    lands_a, sems_a = _ag_start("ag_start_first", [_cast_fill(f"cast_{nm}_{l}", big[nm], l, block) for nm, l in first],
                                cw_gathered)
    lands_b, sems_b = _ag_start("ag_start_rest", [_cast_fill(f"cast_{nm}_{l}", big[nm], l, block) for nm, l in rest],
                                lands_a[0])
    lands = dict(zip(first + rest, list(lands_a) + list(lands_b)))
    ag_sems = dict(zip(first + rest, list(sems_a) + list(sems_b)))
    passed_on, wg = [], {}

    def weight(key, after):
        pins = []
        if key not in wg:
            upto = min(use_order.index(key) + 1, len(use_order) - 1)
            for k in use_order[len(passed_on):upto + 1]:
                lands[k] = _ag_mid(f"ag_mid_{k[0]}_{k[1]}", lands[k], ag_sems[k], after)
                passed_on.append(k)
                if k != key and use_order.index(k) >= AG_PIN_FROM:
                    pins.append(lands[k])
            wg[key] = _ag_wait(f"ag_wait_{key[0]}_{key[1]}", lands[key], ag_sems[key], after)
        return wg[key], pins

    cw_all = cw_gathered[:, :n_even * CONV_TAPS].reshape(NDEV, n_even, CONV_TAPS, -1)
    cw_all = jnp.transpose(cw_all, (1, 2, 0, 3)).reshape(n_even, CONV_TAPS, -1)
    cw_full = [jnp.pad(cw_all[e], ((0, HALO - CONV_TAPS), (0, 0))) for e in range(n_even)]

    def rows_nat(blk):
        return blk.reshape(blk.shape[0] * blk.shape[1], blk.shape[2])

    saved = []
    hn = _norm_fwd("norm_first", h0, g_pre=norm_mix_pre[0][None])[0]
    h = h0
    for l in range(depth):
        s = {"h_in": h, "hn1": hn}
        if l % 2 == 0:
            e = l // 2
            w_, pins = weight(("w_in_ab", e), hn)
            proj = _mm_nn_blk(f"fwd_in_{l}", hn, w_, after=pins)
            ab = _gate_fwd(f"gate_fwd_{l}", proj, w_spatial[e], b_spatial[e].T, cw_full[e])
            w_, pins = weight(("w_out_ab", e), ab)
            mix = _mm_nn(f"fwd_out_{l}", ab, rows_nat(w_), after=pins)
            s.update(proj=proj, ab=ab)
        else:
            o_ = l // 2
            w_, pins = weight(("w_qkv", o_), hn)
            qkv = _mm_nn_blk(f"fwd_qkv_{l}", hn, w_, after=pins)
            att, lse = _attn_fwd(f"attn_fwd_{l}", qkv, cosf, sins, band_bias)
            w_, pins = weight(("w_o", o_), att)
            mix = _mm_nn(f"fwd_o_{l}", att, rows_nat(w_), after=pins)
            s.update(qkv=qkv, att=att, lse=lse)
        h1, hn2 = _norm_fwd(f"norm_mid_{l}", h, mix, norm_mix_post[l][None], norm_mlp_pre[l][None], seg_z=l % 2 == 1)
        w_, pins = weight(("w_up", l), hn2)
        act = _mm_nn_blk(f"fwd_up_{l}", hn2, w_, relu2=True, after=pins)
        w_, pins = weight(("w_down", l), act)
        f = _mm_nn(f"fwd_down_{l}", act, rows_nat(w_), after=pins)
        s.update(mix=mix, h1=h1, hn2=hn2, act=act, f=f)
        if l + 1 < depth:
            h, hn = _norm_fwd(f"norm_end_{l}", h1, f, norm_mlp_post[l][None], norm_mix_pre[l + 1][None],
                              seg_y=(l + 1) % 2 == 1)
        else:
            h = _norm_fwd(f"norm_end_{l}", h1, f, norm_mlp_post[l][None])[0]
        saved.append(s)

    d_h, loss_row = _loss_grad("loss", h, target)
    rs = {}

    def scatter(key, g):
        rs[key] = _rs_start(f"rs_start_{key[0]}_{key[1]}", g.reshape(NDEV, -1, g.shape[-1]))

    dg ={nm: [None] * depth for nm in ("norm_mix_pre", "norm_mix_post", "norm_mlp_pre", "norm_mlp_post")}
    d_ws, d_bs, d_cw = [None] * n_even, [None] * n_even, [None] * n_even
    d_hn_next = None
    for l in reversed(range(depth)):
        s = saved[l]
        if l == depth - 1:
            d_f, dg["norm_mlp_post"][l] = _norm_bwd(f"nb_end_{l}", d_h, post=(s["f"], norm_mlp_post[l][None]))
        else:
            d_h, dg["norm_mix_pre"][l + 1], d_f, dg["norm_mlp_post"][l] = _norm_bwd(
                f"nb_end_{l}", d_h, pre=(d_hn_next, saved[l + 1]["h_in"], norm_mix_pre[l + 1][None]),
                post=(s["f"], norm_mlp_post[l][None]), seg_dy=(l + 1) % 2 == 1)
        wd = rows_nat(wg[("w_down", l)])
        d_up = _mm_nt_rows(f"bwd_down_{l}", d_f, wd, act=s["act"])
        scatter(("w_down", l), _mm_tn(f"gw_down_{l}", s["act"], d_f))
        scatter(("w_up", l), _mm_tn(f"gw_up_{l}", s["hn2"], d_up, nb=w_up.shape[2]))
        d_hn2 = _mm_nt_blk(f"bwd_up_{l}", d_up, wg[("w_up", l)], after=[rs[("w_down", l)][0], rs[("w_up", l)][0]])
        d_h, dg["norm_mlp_pre"][l], d_mix, dg["norm_mix_post"][l] = _norm_bwd(
            f"nb_mid_{l}", d_h, pre=(d_hn2, s["h1"], norm_mlp_pre[l][None]),
            post=(s["mix"], norm_mix_post[l][None]), seg_z=l % 2 == 1)
        if l % 2 == 0:
            e = l // 2
            wo = rows_nat(wg[("w_out_ab", e)])
            d_ab = _mm_nt_rows(f"bwd_out_{l}", d_mix, wo)
            scatter(("w_out_ab", e), _mm_tn(f"gw_out_{l}", s["ab"], d_mix))
            d_proj, d_ws[e], d_bs[e], d_cw[e] = _gate_bwd(
                f"gate_bwd_{l}", s["proj"], d_ab, w_spatial[e], b_spatial[e].T, cw_full[e])
            scatter(("w_in_ab", e), _mm_tn(f"gw_in_{l}", s["hn1"], d_proj, nb=w_in_ab.shape[2]))
            d_hn_next = _mm_nt_blk(f"bwd_in_{l}", d_proj, wg[("w_in_ab", e)],
                                   after=[rs[("w_out_ab", e)][0], rs[("w_in_ab", e)][0]])
        else:
            o_ = l // 2
            wo = rows_nat(wg[("w_o", o_)])
            d_att = _mm_nt_rows(f"bwd_o_{l}", d_mix, wo, out_dtype=F32)
            scatter(("w_o", o_), _mm_tn(f"gw_o_{l}", s["att"], d_mix))
            parts = _attn_bwd(f"attn_bwd_{l}", s["qkv"], cosf, sins, band_bias, d_att, s["att"], s["lse"])
            d_qkv = _attn_merge(f"attn_merge_{l}", *parts)
            scatter(("w_qkv", o_), _mm_tn(f"gw_qkv_{l}", s["hn1"], d_qkv, nb=w_qkv.shape[2]))
            d_hn_next = _mm_nt_blk(f"bwd_qkv_{l}", d_qkv, wg[("w_qkv", o_)],
                                   after=[rs[("w_o", o_)][0], rs[("w_qkv", o_)][0]])
    grad_x, dg["norm_mix_pre"][0] = _norm_bwd("nb_first", d_h, pre=(d_hn_next, h0, norm_mix_pre[0][None]))

    small_g = ([jnp.concatenate(dg[nm], axis=0) for nm in dg]
               + [jnp.stack(d_ws), jnp.stack(d_bs), jnp.stack([c[:CONV_TAPS] for c in d_cw]), loss_row])
    small_land, small_send, small_recv = _ag_direct_start("ag_small_start", _fill_slot("fill_small", _pack(small_g), block))

    moments = {"w_in_ab": (m_w_in_ab, v_w_in_ab), "w_out_ab": (m_w_out_ab, v_w_out_ab), "w_qkv": (m_w_qkv, v_w_qkv),
               "w_o": (m_w_o, v_w_o), "w_up": (m_w_up, v_w_up), "w_down": (m_w_down, v_w_down)}
    out_big = {}
    behind = small_land
    for nm in ("w_o", "w_qkv", "w_down", "w_up", "w_out_ab", "w_in_ab"):
        own, landed = [], []
        for l in range(big[nm].shape[0]):
            g, land = _wait_all(f"rs_wait_{nm}_{l}", *rs[(nm, l)], behind)
            own.append(g)
            landed.append(land)
        out_big[nm] = _adamw_layers(f"adamw_{nm}", own, landed, block, big[nm], moments[nm][0], moments[nm][1])
        behind = out_big[nm][0]

    summed = _sum_parts("sum_small", _ag_direct_wait("ag_small_wait", small_land, small_send, small_recv, behind))
    g_nmp, g_nmo, g_nlp, g_nlo, g_ws, g_bs, g_cw_all, loss_sum = _unpack(summed, small_g)
    loss = loss_sum[0, 0]
    cwb = conv_w.shape[2]
    g_cw = lax.dynamic_slice_in_dim(g_cw_all, my_block * cwb, cwb, axis=2)
    small_w = [norm_mix_pre, norm_mix_post, norm_mlp_pre, norm_mlp_post, w_spatial, b_spatial, conv_w]
    small_m = [m_norm_mix_pre, m_norm_mix_post, m_norm_mlp_pre, m_norm_mlp_post, m_w_spatial, m_b_spatial, m_conv_w]
    small_v = [v_norm_mix_pre, v_norm_mix_post, v_norm_mlp_pre, v_norm_mlp_post, v_w_spatial, v_b_spatial, v_conv_w]
    small_grad = [g_nmp, g_nmo, g_nlp, g_nlo, g_ws, g_bs, g_cw]
    upd = _adamw("adamw_small", _pack(small_grad)[None], _pack(small_w), _pack(small_m), _pack(small_v))
    sg, sd, sm, sv = [_unpack(u, small_w) for u in upd]

    def outs(i_small, i_big):
        return (i_small[0], i_small[1], i_small[2], i_small[3], i_big["w_in_ab"], i_small[4], i_small[5], i_small[6],
                i_big["w_out_ab"], i_big["w_qkv"], i_big["w_o"], i_big["w_up"], i_big["w_down"])

    pick = lambda i: {nm: out_big[nm][i] for nm in big}
    return (loss, grad_x.reshape(x.shape), *outs(sg, pick(0)), *outs(sd, pick(1)), *outs(sm, pick(2)),
            *outs(sv, pick(3)))
```

```python
import functools

import jax
import jax.numpy as jnp
from jax import lax
from jax.experimental import pallas as pl
from jax.experimental.pallas import tpu as pltpu

F32 = jnp.float32
BF16 = jnp.bfloat16
MESH = pl.DeviceIdType.MESH
ANY = pl.BlockSpec(memory_space=pl.ANY)
HBM = pl.BlockSpec(memory_space=pltpu.HBM)
SEM = pl.BlockSpec(memory_space=pltpu.SEMAPHORE)
EFFECT = pltpu.SideEffectType.DATAFLOW_SIDE_EFFECTING

NDEV = 8
NCHIP = 4
RMS_EPS = 1e-6
LN_EPS = 1e-5
CHUNK = 128
HEAD = 128
ATT_TILE = 2048
ATT_UNROLL = 16
DILATIONS = (1, 4, 16)
SEGS = 16
ROPE_THETA = 10000.0
CONV_TAPS = 3
HALO = 8
HALO_BF16 = 16
GELU_C = 0.7978845608028654
GELU_A = 0.044715
ADAM_LR, ADAM_B1, ADAM_B2, ADAM_EPS, ADAM_WD, ADAM_STEP = 0.001, 0.9, 0.999, 1e-08, 0.01, 10
VMEM_MM = 52 << 20
VMEM_EW = 40 << 20


def _cp(sem=None, vmem=VMEM_EW):
    if sem is None:
        return pltpu.CompilerParams(vmem_limit_bytes=vmem)
    return pltpu.CompilerParams(dimension_semantics=sem, vmem_limit_bytes=vmem)


def _tile(n, want):
    return want if n % want == 0 else n


def _all_gather(name, shards, after=()):
    n = len(shards)
    after = list(after)

    def body(*refs):
        ins, outs = refs[:n], refs[n + len(after):2 * n + len(after)]
        send_sems, recv_sems, local_sems = refs[2 * n + len(after):]
        x, y, c = lax.axis_index("x"), lax.axis_index("y"), lax.axis_index("c")
        me, sibling = (x, y, c), (x, y, 1 - c)
        chips = [(1 - x, y), (x, 1 - y), (1 - x, 1 - y)]

        def slot(p):
            return 4 * p[0] + 2 * p[1] + p[2]

        def copy(i, k, block, to, src=None):
            dst = outs[i].at[slot(block)]
            return pltpu.make_async_remote_copy(
                src_ref=dst if src is None else src, dst_ref=dst,
                send_sem=send_sems.at[i, k], recv_sem=recv_sems.at[i, k],
                device_id=to, device_id_type=MESH)

        mine = [pltpu.make_async_copy(ins[i], outs[i].at[slot(me)], local_sems.at[i]) for i in range(n)]
        for cp in mine:
            cp.start()
        first = []
        for i in range(n):
            first.append(copy(i, 0, me, sibling, src=ins[i]))
            for j, chip in enumerate(chips):
                first.append(copy(i, 1 + j, me, (*chip, c), src=ins[i]))
        for cp in first:
            cp.start()
        passed = []
        for j, chip in enumerate(chips):
            for i in range(n):
                copy(i, 1 + j, (*chip, c), me).wait_recv()
                fwd = copy(i, 4 + j, (*chip, c), sibling)
                fwd.start()
                passed.append(fwd)
        for i in range(n):
            copy(i, 0, sibling, me).wait_recv()
            for j, chip in enumerate(chips):
                copy(i, 4 + j, (*chip, 1 - c), me).wait_recv()
        for cp in first + passed:
            cp.wait_send()
        for cp in mine:
            cp.wait()

    return pl.pallas_call(
        body, name=name,
        out_shape=[jax.ShapeDtypeStruct((NDEV,) + s.shape, s.dtype) for s in shards],
        in_specs=[ANY] * (n + len(after)), out_specs=[ANY] * n,
        scratch_shapes=[pltpu.SemaphoreType.DMA((n, 7)), pltpu.SemaphoreType.DMA((n, 7)),
                        pltpu.SemaphoreType.DMA((n,))],
    )(*shards, *after)


def _peer(x, y, c, r):
    return (1 - x if r & 4 else x, 1 - y if r & 2 else y, 1 - c if r & 1 else c)


def _slot(p):
    return 4 * p[0] + 2 * p[1] + p[2]


def _cast_fill(name, w, layer, block):
    _, rows, cols = w.shape
    tr = _tile(rows, 256)

    def body(blk_ref, w_ref, o_ref):
        o_ref[...] = w_ref[...].astype(BF16)

    return pl.pallas_call(
        body, name=name,
        grid_spec=pltpu.PrefetchScalarGridSpec(
            num_scalar_prefetch=1, grid=(rows // tr,),
            in_specs=[pl.BlockSpec((None, tr, cols), lambda i, blk: (layer, i, 0))],
            out_specs=pl.BlockSpec((None, tr, cols), lambda i, blk: (blk[0], i, 0))),
        out_shape=jax.ShapeDtypeStruct((NDEV, rows, cols), BF16),
        compiler_params=_cp(("parallel",)))(block, w)


OTHER_CHIPS = (2, 4, 6)
AG_SEMS = 6
AG_PIN_FROM = 5


def _ag_start(name, lands, after):
    n = len(lands)

    def body(*refs):
        ins, sems = refs[:n], refs[n + 1:n + 1 + AG_SEMS * n]
        x, y, c = lax.axis_index("x"), lax.axis_index("y"), lax.axis_index("c")
        mine = _slot((x, y, c))
        for i in range(n):
            send_a, *recv_a, _, recv_b = sems[AG_SEMS * i:AG_SEMS * (i + 1)]
            block = ins[i].at[mine]
            pltpu.make_async_remote_copy(src_ref=block, dst_ref=block, send_sem=send_a, recv_sem=recv_b,
                                         device_id=_peer(x, y, c, 1), device_id_type=MESH).start()
            for k, r in enumerate(OTHER_CHIPS):
                pltpu.make_async_remote_copy(src_ref=block, dst_ref=block, send_sem=send_a, recv_sem=recv_a[k],
                                             device_id=_peer(x, y, c, r), device_id_type=MESH).start()

    outs = pl.pallas_call(
        body, name=name,
        out_shape=[pltpu.SemaphoreType.DMA(())] * (AG_SEMS * n) + [pltpu.HBM(a.shape, a.dtype) for a in lands],
        in_specs=[HBM] * n + [ANY], out_specs=[SEM] * (AG_SEMS * n) + [HBM] * n,
        input_output_aliases={i: AG_SEMS * n + i for i in range(n)},
        compiler_params=pltpu.CompilerParams(has_side_effects=EFFECT),
    )(*[pltpu.with_memory_space_constraint(a, pltpu.HBM) for a in lands], after)
    return outs[AG_SEMS * n:], [tuple(outs[AG_SEMS * i:AG_SEMS * (i + 1)]) for i in range(n)]


def _ag_mid(name, land, sems, after):
    _, *recv_a, send_b, recv_b = sems

    def body(land_ref, ra0, ra1, ra2, send_b_ref, recv_b_ref, after_ref, land_out):
        x, y, c = lax.axis_index("x"), lax.axis_index("y"), lax.axis_index("c")
        sibling = _peer(x, y, c, 1)
        for arrival, r in zip((ra0, ra1, ra2), OTHER_CHIPS):
            block = land_ref.at[_slot(_peer(x, y, c, r))]
            pltpu.make_async_remote_copy(src_ref=block, dst_ref=block, send_sem=send_b_ref, recv_sem=arrival,
                                         device_id=sibling, device_id_type=MESH).wait_recv()
            pltpu.make_async_remote_copy(src_ref=block, dst_ref=block, send_sem=send_b_ref, recv_sem=recv_b_ref,
                                         device_id=sibling, device_id_type=MESH).start()

    return pl.pallas_call(
        body, name=name, out_shape=pltpu.HBM(land.shape, land.dtype),
        in_specs=[HBM] + [SEM] * 5 + [ANY], out_specs=HBM, input_output_aliases={0: 0},
        compiler_params=pltpu.CompilerParams(has_side_effects=EFFECT),
    )(land, *recv_a, send_b, recv_b, after)


def _ag_wait(name, land, sems, after):
    send_a, _, _, _, send_b, recv_b = sems

    def body(land_ref, send_a_ref, send_b_ref, recv_b_ref, after_ref, land_out):
        x, y, c = lax.axis_index("x"), lax.axis_index("y"), lax.axis_index("c")
        sibling = _peer(x, y, c, 1)
        four = land_ref.at[pl.ds(0, 1 + len(OTHER_CHIPS))]
        three = land_ref.at[pl.ds(0, len(OTHER_CHIPS))]
        first = pltpu.make_async_remote_copy(src_ref=four, dst_ref=four, send_sem=send_a_ref, recv_sem=recv_b_ref,
                                             device_id=sibling, device_id_type=MESH)
        passed = pltpu.make_async_remote_copy(src_ref=three, dst_ref=three, send_sem=send_b_ref, recv_sem=recv_b_ref,
                                              device_id=sibling, device_id_type=MESH)
        first.wait_send()
        passed.wait_send()
        first.wait_recv()

    return pl.pallas_call(
        body, name=name, out_shape=pltpu.HBM(land.shape, land.dtype),
        in_specs=[HBM, SEM, SEM, SEM, ANY], out_specs=HBM, input_output_aliases={0: 0},
        compiler_params=pltpu.CompilerParams(has_side_effects=EFFECT),
    )(land, send_a, send_b, recv_b, after)


def _fill_slot(name, rows, block):
    r, c = rows.shape

    def body(blk_ref, i_ref, o_ref):
        o_ref[...] = i_ref[...]

    return pl.pallas_call(
        body, name=name,
        grid_spec=pltpu.PrefetchScalarGridSpec(
            num_scalar_prefetch=1, grid=(1,),
            in_specs=[pl.BlockSpec((r, c), lambda i, blk: (0, 0))],
            out_specs=pl.BlockSpec((None, r, c), lambda i, blk: (blk[0], 0, 0))),
        out_shape=jax.ShapeDtypeStruct((NDEV, r, c), rows.dtype),
        compiler_params=_cp(("arbitrary",)))(block, rows)


def _ag_direct_start(name, land):
    def body(land_ref, send, recv, land_out):
        x, y, c = lax.axis_index("x"), lax.axis_index("y"), lax.axis_index("c")
        block = land_ref.at[_slot((x, y, c))]
        for r in range(1, NDEV):
            pltpu.make_async_remote_copy(src_ref=block, dst_ref=block, send_sem=send, recv_sem=recv,
                                         device_id=_peer(x, y, c, r), device_id_type=MESH).start()

    send, recv, land_thru = pl.pallas_call(
        body, name=name,
        out_shape=[pltpu.SemaphoreType.DMA(()), pltpu.SemaphoreType.DMA(()), pltpu.HBM(land.shape, land.dtype)],
        in_specs=[HBM], out_specs=[SEM, SEM, HBM], input_output_aliases={0: 2},
        compiler_params=pltpu.CompilerParams(has_side_effects=EFFECT),
    )(pltpu.with_memory_space_constraint(land, pltpu.HBM))
    return land_thru, send, recv


def _ag_direct_wait(name, land, send, recv, after):
    def body(land_ref, send_ref, recv_ref, after_ref, land_out):
        x, y, c = lax.axis_index("x"), lax.axis_index("y"), lax.axis_index("c")
        seven = land_ref.at[pl.ds(0, NDEV - 1)]
        copy = pltpu.make_async_remote_copy(src_ref=seven, dst_ref=seven, send_sem=send_ref, recv_sem=recv_ref,
                                            device_id=_peer(x, y, c, 1), device_id_type=MESH)
        copy.wait_send()
        copy.wait_recv()

    return pl.pallas_call(
        body, name=name, out_shape=pltpu.HBM(land.shape, land.dtype),
        in_specs=[HBM, SEM, SEM, ANY], out_specs=HBM, input_output_aliases={0: 0},
        compiler_params=pltpu.CompilerParams(has_side_effects=EFFECT),
    )(land, send, recv, after)


def _wait_all(name, src, land, send, recv, after):
    def body(src_ref, land_ref, send_ref, recv_ref, after_ref, src_out, land_out):
        x, y, c = lax.axis_index("x"), lax.axis_index("y"), lax.axis_index("c")
        seven = land_ref.at[pl.ds(0, NDEV - 1)]
        copy = pltpu.make_async_remote_copy(src_ref=seven, dst_ref=seven, send_sem=send_ref, recv_sem=recv_ref,
                                            device_id=_peer(x, y, c, 1), device_id_type=MESH)
        copy.wait_send()
        copy.wait_recv()

    return pl.pallas_call(
        body, name=name,
        out_shape=[pltpu.HBM(src.shape, src.dtype), pltpu.HBM(land.shape, land.dtype)],
        in_specs=[HBM, HBM, SEM, SEM, ANY], out_specs=[HBM, HBM],
        input_output_aliases={0: 0, 1: 1},
        compiler_params=pltpu.CompilerParams(has_side_effects=EFFECT),
    )(src, land, send, recv, after)


def _rs_start(name, grad):
    land = lax.empty((NDEV - 1,) + grad.shape[1:], grad.dtype)

    def body(g_ref, land_ref, send, recv, g_out, land_out):
        x, y, c = lax.axis_index("x"), lax.axis_index("y"), lax.axis_index("c")
        for r in range(1, NDEV):
            peer = _peer(x, y, c, r)
            pltpu.make_async_remote_copy(
                src_ref=g_ref.at[_slot(peer)], dst_ref=land_ref.at[r - 1], send_sem=send, recv_sem=recv,
                device_id=peer, device_id_type=MESH).start()

    send, recv, g_thru, land_thru = pl.pallas_call(
        body, name=name,
        out_shape=[pltpu.SemaphoreType.DMA(()), pltpu.SemaphoreType.DMA(()),
                   pltpu.HBM(grad.shape, grad.dtype), pltpu.HBM(land.shape, land.dtype)],
        in_specs=[HBM, HBM], out_specs=[SEM, SEM, HBM, HBM], input_output_aliases={0: 2, 1: 3},
        compiler_params=pltpu.CompilerParams(has_side_effects=EFFECT),
    )(pltpu.with_memory_space_constraint(grad, pltpu.HBM), pltpu.with_memory_space_constraint(land, pltpu.HBM))
    return g_thru, land_thru, send, recv


def _adam_math(w, g, m, v):
    m = ADAM_B1 * m + (1.0 - ADAM_B1) * g
    v = ADAM_B2 * v + (1.0 - ADAM_B2) * (g * g)
    m_hat = m / (1.0 - ADAM_B1 ** ADAM_STEP)
    v_hat = v / (1.0 - ADAM_B2 ** ADAM_STEP)
    delta = -ADAM_LR * (m_hat / (jnp.sqrt(v_hat) + ADAM_EPS) + ADAM_WD * w)
    return delta, m, v


def _adamw(name, parts, w, m, v):
    nparts, rows, cols = parts.shape
    tr = _tile(rows, 256)

    def body(p_ref, w_ref, m_ref, v_ref, g_out, d_out, m_out, v_out):
        g = p_ref[0].astype(F32)
        for k in range(1, nparts):
            g = g + p_ref[k].astype(F32)
        delta, mn, vn = _adam_math(w_ref[...], g, m_ref[...], v_ref[...])
        g_out[...] = g
        d_out[...] = delta
        m_out[...] = mn
        v_out[...] = vn

    row = pl.BlockSpec((tr, cols), lambda i: (i, 0))
    return pl.pallas_call(
        body, name=name, grid=(rows // tr,),
        in_specs=[pl.BlockSpec((nparts, tr, cols), lambda i: (0, i, 0)), row, row, row],
        out_specs=[row] * 4,
        out_shape=[jax.ShapeDtypeStruct((rows, cols), F32)] * 4,
        compiler_params=_cp(("parallel",)),
    )(parts, w, m, v)


def _adamw_layers(name, grads, lands, block, w, m, v):
    layers, rows, cols = w.shape
    nland = lands[0].shape[0]
    tr = rows
    while tr % 2 == 0 and tr > 8 and nland * tr * cols * 2 > (2 << 20):
        tr //= 2

    def body(blk_ref, *refs):
        own_refs, land_refs = refs[:layers], refs[layers:2 * layers]
        w_ref, m_ref, v_ref, g_out, d_out, m_out, v_out = refs[2 * layers:]
        layer = pl.program_id(0)
        for k in range(layers):
            @pl.when(layer == k)
            def _(k=k):
                g = own_refs[k][...].astype(F32)
                for s in range(nland):
                    g = g + land_refs[k][s].astype(F32)
                delta, mn, vn = _adam_math(w_ref[...], g, m_ref[...], v_ref[...])
                g_out[...] = g
                d_out[...] = delta
                m_out[...] = mn
                v_out[...] = vn

    def own_spec(k):
        return pl.BlockSpec((None, tr, cols), lambda l, i, blk: (blk[0], jnp.where(l == k, i, 0), 0))

    def land_spec(k):
        return pl.BlockSpec((nland, tr, cols), lambda l, i, blk: (0, jnp.where(l == k, i, 0), 0))

    row = pl.BlockSpec((None, tr, cols), lambda l, i, blk: (l, i, 0))
    return pl.pallas_call(
        body, name=name,
        grid_spec=pltpu.PrefetchScalarGridSpec(
            num_scalar_prefetch=1, grid=(layers, rows // tr),
            in_specs=[own_spec(k) for k in range(layers)] + [land_spec(k) for k in range(layers)] + [row, row, row],
            out_specs=[row] * 4),
        out_shape=[jax.ShapeDtypeStruct((layers, rows, cols), F32)] * 4,
        compiler_params=_cp(("arbitrary", "arbitrary")),
    )(block, *grads, *lands, w, m, v)


LANES = 128


def _seg_scratch(rows, d):
    return pltpu.VMEM((d // LANES, rows, LANES), F32)


def _to_segments(vals, scratch, out_ref):
    per = scratch.shape[1] // SEGS
    for c in range(scratch.shape[0]):
        cols = slice(c * LANES, (c + 1) * LANES)
        scratch[c] = vals[:, cols]
        for s in range(SEGS):
            out_ref[s, :, cols] = scratch.at[c][pl.ds(s, per, stride=SEGS), :].astype(out_ref.dtype)


def _from_segments(in_ref, scratch):
    per = scratch.shape[1] // SEGS
    for c in range(scratch.shape[0]):
        for s in range(SEGS):
            scratch.at[c][pl.ds(s, per, stride=SEGS), :] = in_ref[s, :, c * LANES:(c + 1) * LANES].astype(F32)
    return jnp.concatenate([scratch[c] for c in range(scratch.shape[0])], axis=1)


def _seg_view(a):
    return a.reshape(SEGS, a.shape[0] // SEGS, a.shape[1])


def _norm_fwd(name, h, z=None, g_post=None, g_pre=None, seg_z=False, seg_y=False):
    rows, d = h.shape
    tm = _tile(rows, 256)
    has_post, has_pre = z is not None, g_pre is not None
    nscratch = int(seg_z) + int(seg_y)

    def body(*refs):
        scratch = list(refs[len(refs) - nscratch:])
        it = iter(refs)
        hv = next(it)[...]
        if has_post:
            z_ref = next(it)
            zv = _from_segments(z_ref, scratch.pop(0)) if seg_z else z_ref[...].astype(F32)
            gp = next(it)[...]
        if has_pre:
            gq = next(it)[...]
        if has_post:
            r = lax.rsqrt(jnp.mean(zv * zv, axis=-1, keepdims=True) + RMS_EPS)
            hv = hv + (zv * r) * gp
            next(it)[...] = hv
        if has_pre:
            r = lax.rsqrt(jnp.mean(hv * hv, axis=-1, keepdims=True) + RMS_EPS)
            y = (hv * r) * gq
            if seg_y:
                _to_segments(y, scratch.pop(0), next(it))
            else:
                next(it)[...] = y.astype(BF16)

    row = pl.BlockSpec((tm, d), lambda i: (i, 0))
    seg = pl.BlockSpec((SEGS, tm // SEGS, d), lambda i: (0, i, 0))
    vec = pl.BlockSpec((1, d), lambda i: (0, 0))
    ins, in_specs, out_shape, out_specs = [h], [row], [], []
    if has_post:
        ins += [_seg_view(z) if seg_z else z, g_post]
        in_specs += [seg if seg_z else row, vec]
        out_shape.append(jax.ShapeDtypeStruct((rows, d), F32))
        out_specs.append(row)
    if has_pre:
        ins.append(g_pre)
        in_specs.append(vec)
        out_shape.append(jax.ShapeDtypeStruct((SEGS, rows // SEGS, d) if seg_y else (rows, d), BF16))
        out_specs.append(seg if seg_y else row)
    outs = pl.pallas_call(body, name=name, grid=(rows // tm,), in_specs=in_specs, out_specs=out_specs,
                          out_shape=out_shape, scratch_shapes=[_seg_scratch(tm, d)] * nscratch,
                          compiler_params=_cp(("parallel",)))(*ins)
    if seg_y:
        outs = list(outs[:-1]) + [outs[-1].reshape(rows, d)]
    return outs


def _rms_bwd_rows(x, g, dy):
    r = lax.rsqrt(jnp.mean(x * x, axis=-1, keepdims=True) + RMS_EPS)
    xn = x * r
    dg = jnp.sum(dy * xn, axis=0, keepdims=True)
    dxn = dy * g
    dx = r * (dxn - xn * jnp.mean(dxn * xn, axis=-1, keepdims=True))
    return dx, dg


def _norm_bwd(name, d_out, pre=None, post=None, seg_dy=False, seg_z=False):
    rows, d = d_out.shape
    tm = _tile(rows, 256)
    has_pre, has_post = pre is not None, post is not None
    nscratch = int(seg_dy) + 2 * int(seg_z)

    def body(*refs):
        scratch = list(refs[len(refs) - nscratch:])
        it = iter(refs)
        dres = next(it)[...]
        if has_pre:
            dy_ref = next(it)
            dy = _from_segments(dy_ref, scratch.pop(0)) if seg_dy else dy_ref[...].astype(F32)
            xp, gq = next(it)[...], next(it)[...]
        if has_post:
            z_ref = next(it)
            zv = _from_segments(z_ref, scratch.pop(0)) if seg_z else z_ref[...].astype(F32)
            gp = next(it)[...]
        first = pl.program_id(0) == 0
        if has_pre:
            dx, dg = _rms_bwd_rows(xp, gq, dy)
            dres = dres + dx
            next(it)[...] = dres
            dg_ref = next(it)

            @pl.when(first)
            def _():
                dg_ref[...] = jnp.zeros_like(dg_ref)
            dg_ref[...] += dg
        if has_post:
            dz, dg2 = _rms_bwd_rows(zv, gp, dres)
            if seg_z:
                _to_segments(dz, scratch.pop(0), next(it))
            else:
                next(it)[...] = dz.astype(BF16)
            dg2_ref = next(it)

            @pl.when(first)
            def _():
                dg2_ref[...] = jnp.zeros_like(dg2_ref)
            dg2_ref[...] += dg2

    row = pl.BlockSpec((tm, d), lambda i: (i, 0))
    seg = pl.BlockSpec((SEGS, tm // SEGS, d), lambda i: (0, i, 0))
    vec = pl.BlockSpec((1, d), lambda i: (0, 0))
    ins, in_specs, out_shape, out_specs = [d_out], [row], [], []
    if has_pre:
        d_y, x_pre, g_pre = pre
        ins += [_seg_view(d_y) if seg_dy else d_y, x_pre, g_pre]
        in_specs += [seg if seg_dy else row, row, vec]
        out_shape += [jax.ShapeDtypeStruct((rows, d), F32), jax.ShapeDtypeStruct((1, d), F32)]
        out_specs += [row, vec]
    if has_post:
        z, g_post = post
        ins += [_seg_view(z) if seg_z else z, g_post]
        in_specs += [seg if seg_z else row, vec]
        out_shape += [jax.ShapeDtypeStruct((SEGS, rows // SEGS, d) if seg_z else (rows, d), BF16),
                      jax.ShapeDtypeStruct((1, d), F32)]
        out_specs += [seg if seg_z else row, vec]
    outs = pl.pallas_call(body, name=name, grid=(rows // tm,), in_specs=in_specs, out_specs=out_specs,
                          out_shape=out_shape, scratch_shapes=[_seg_scratch(tm, d)] * nscratch,
                          compiler_params=_cp(("arbitrary",)))(*ins)
    if seg_z:
        outs = list(outs)
        outs[-2] = outs[-2].reshape(rows, d)
    return outs


def _loss_grad(name, y, target):
    rows, d = y.shape
    tm = _tile(rows, 256)

    def body(y_ref, t_ref, dy_ref, loss_ref):
        err = y_ref[...] - t_ref[...]
        dy_ref[...] = err * (1.0 / d)

        @pl.when(pl.program_id(0) == 0)
        def _():
            loss_ref[...] = jnp.zeros_like(loss_ref)
        loss_ref[...] += jnp.full(loss_ref.shape, (0.5 / d) * jnp.sum(err * err), F32)

    row = pl.BlockSpec((tm, d), lambda i: (i, 0))
    return pl.pallas_call(
        body, name=name, grid=(rows // tm,), in_specs=[row, row],
        out_specs=[row, pl.BlockSpec((1, 128), lambda i: (0, 0))],
        out_shape=[jax.ShapeDtypeStruct((rows, d), F32), jax.ShapeDtypeStruct((1, 128), F32)],
        compiler_params=_cp(("arbitrary",)))(y, target)


NT_DIMS = (((1,), (1,)), ((), ()))
TN_DIMS = (((0,), (0,)), ((), ()))


def _mm_nn_blk(name, a, wblk, relu2=False, after=()):
    m, k = a.shape
    nb = wblk.shape[2]
    tm = _tile(m, 1024)
    after = list(after)

    def body(a_ref, w_ref, *rest):
        r = jnp.dot(a_ref[...], w_ref[...], preferred_element_type=F32)
        if relu2:
            rr = jnp.maximum(r, 0.0)
            r = rr * rr
        rest[-1][...] = r.astype(BF16)

    return pl.pallas_call(
        body, name=name, grid=(NDEV, m // tm),
        in_specs=[pl.BlockSpec((tm, k), lambda d, i: (i, 0)), pl.BlockSpec((None, k, nb), lambda d, i: (d, 0, 0))]
        + [ANY] * len(after),
        out_specs=pl.BlockSpec((tm, nb), lambda d, i: (i, d)),
        out_shape=jax.ShapeDtypeStruct((m, NDEV * nb), BF16),
        compiler_params=_cp(("parallel", "parallel"), VMEM_MM))(a, wblk, *after)


def _accumulate(acc, o_ref, r, step, last):
    if acc is None:
        o_ref[...] = r.astype(o_ref.dtype)
        return

    @pl.when(step == 0)
    def _():
        acc[...] = r

    @pl.when(jnp.logical_and(step > 0, step < last))
    def _():
        acc[...] += r

    @pl.when(jnp.logical_and(step > 0, step == last))
    def _():
        o_ref[...] = (acc[...] + r).astype(o_ref.dtype)


def _mm_nn(name, a, w, after=()):
    m, kb = a.shape
    n = w.shape[1]
    one_step = kb <= 2048
    tm = _tile(m, 512 if one_step else 1024)
    tk = kb if one_step else _tile(kb, 4096)
    tn = n if one_step else _tile(n, 1024)
    steps = kb // tk
    after = list(after)

    def body(a_ref, w_ref, *rest):
        o_ref, scratch = rest[len(after)], rest[len(after) + 1:]
        r = jnp.dot(a_ref[...], w_ref[...], preferred_element_type=F32)
        _accumulate(scratch[0] if scratch else None, o_ref, r, pl.program_id(2), steps - 1)

    return pl.pallas_call(
        body, name=name, grid=(m // tm, n // tn, steps),
        in_specs=[pl.BlockSpec((tm, tk), lambda i, j, s: (i, s)), pl.BlockSpec((tk, tn), lambda i, j, s: (s, j))]
        + [ANY] * len(after),
        out_specs=pl.BlockSpec((tm, tn), lambda i, j, s: (i, j)),
        out_shape=jax.ShapeDtypeStruct((m, n), BF16),
        scratch_shapes=[pltpu.VMEM((tm, tn), F32)] if steps > 1 else [],
        compiler_params=_cp(("parallel", "parallel", "arbitrary"), VMEM_MM))(a, w, *after)


def _mm_nt_rows(name, dy, w, act=None, out_dtype=BF16):
    m, n = dy.shape
    kw = w.shape[0]
    tm, tkw = _tile(m, 1024), _tile(kw, 1024)

    def body(dy_ref, w_ref, *rest):
        r = lax.dot_general(dy_ref[...], w_ref[...], NT_DIMS, preferred_element_type=F32)
        if act is None:
            rest[0][...] = r.astype(out_dtype)
        else:
            rest[1][...] = (r * (2.0 * jnp.sqrt(rest[0][...].astype(F32)))).astype(BF16)

    ins = [dy, w]
    in_specs = [pl.BlockSpec((tm, n), lambda j, i: (i, 0)), pl.BlockSpec((tkw, n), lambda j, i: (j, 0))]
    if act is not None:
        ins.append(act)
        in_specs.append(pl.BlockSpec((tm, tkw), lambda j, i: (i, j)))
    return pl.pallas_call(
        body, name=name, grid=(kw // tkw, m // tm), in_specs=in_specs,
        out_specs=pl.BlockSpec((tm, tkw), lambda j, i: (i, j)),
        out_shape=jax.ShapeDtypeStruct((m, kw), out_dtype if act is None else BF16),
        compiler_params=_cp(("parallel", "parallel"), VMEM_MM))(*ins)


def _mm_nt_blk(name, dy, wblk, after=None):
    m = dy.shape[0]
    _, kw, nb = wblk.shape
    tm, tkw, per = _tile(m, 1024), _tile(kw, 1024), 4

    extra = list(after or ())

    def body(dy_ref, w_ref, *rest):
        o_ref, acc = rest[len(extra):]
        r = lax.dot_general(dy_ref[:, :nb], w_ref[0], NT_DIMS, preferred_element_type=F32)
        for t in range(1, per):
            r = r + lax.dot_general(dy_ref[:, t * nb:(t + 1) * nb], w_ref[t], NT_DIMS, preferred_element_type=F32)
        _accumulate(acc, o_ref, r, pl.program_id(2), NDEV // per - 1)

    return pl.pallas_call(
        body, name=name, grid=(m // tm, kw // tkw, NDEV // per),
        in_specs=[pl.BlockSpec((tm, per * nb), lambda i, j, s: (i, s)),
                  pl.BlockSpec((per, tkw, nb), lambda i, j, s: (s, j, 0))] + [ANY] * len(extra),
        out_specs=pl.BlockSpec((tm, tkw), lambda i, j, s: (i, j)),
        out_shape=jax.ShapeDtypeStruct((m, kw), BF16),
        scratch_shapes=[pltpu.VMEM((tm, tkw), F32)],
        compiler_params=_cp(("parallel", "parallel", "arbitrary"), VMEM_MM))(dy, wblk, *extra)


def _mm_tn(name, x, dy, nb=None):
    t, mx = x.shape
    n = dy.shape[1]
    tmx = _tile(mx, 512)
    tn = nb if nb is not None else _tile(n, 1024)

    def body(x_ref, dy_ref, o_ref):
        o_ref[...] = lax.dot_general(x_ref[...], dy_ref[...], TN_DIMS, preferred_element_type=F32).astype(BF16)

    if nb is None:
        out_shape = jax.ShapeDtypeStruct((mx, n), BF16)
        out_spec = pl.BlockSpec((tmx, tn), lambda j, i: (i, j))
    else:
        out_shape = jax.ShapeDtypeStruct((NDEV, mx, nb), BF16)
        out_spec = pl.BlockSpec((None, tmx, nb), lambda j, i: (j, i, 0))
    return pl.pallas_call(
        body, name=name, grid=(n // tn, mx // tmx),
        in_specs=[pl.BlockSpec((t, tmx), lambda j, i: (0, i)), pl.BlockSpec((t, tn), lambda j, i: (0, j))],
        out_specs=out_spec, out_shape=out_shape,
        compiler_params=_cp(("parallel", "parallel"), VMEM_MM))(x, dy)


def _gelu(x):
    return 0.5 * x * (1.0 + jnp.tanh(GELU_C * (x + GELU_A * (x * x * x))))


def _gelu_grad(x):
    t = jnp.tanh(GELU_C * (x + GELU_A * (x * x * x)))
    return 0.5 * (1.0 + t) + 0.5 * x * (1.0 - t * t) * (GELU_C * (1.0 + 3.0 * GELU_A * (x * x)))


def _layernorm(a):
    mu = jnp.mean(a, axis=-1, keepdims=True)
    ac = a - mu
    rstd = lax.rsqrt(jnp.mean(ac * ac, axis=-1, keepdims=True) + LN_EPS)
    return ac * rstd, rstd


def _shift_rows(z, halo, k):
    zr = pltpu.roll(z, k, 0)
    hr = pltpu.roll(halo, k, 0)
    row = lax.broadcasted_iota(jnp.int32, hr.shape, 0)
    top = jnp.where(row < k, hr, zr[:HALO])
    return jnp.concatenate([top, zr[HALO:]], axis=0)


def _shift_rows_up(z, halo, k):
    rows = z.shape[0]
    zr = pltpu.roll(z, rows - k, 0)
    hr = pltpu.roll(halo, HALO - k, 0)
    row = lax.broadcasted_iota(jnp.int32, hr.shape, 0)
    bot = jnp.where(row >= HALO - k, hr, zr[rows - HALO:])
    return jnp.concatenate([zr[:rows - HALO], bot], axis=0)


def _causal_mask():
    t = lax.broadcasted_iota(jnp.int32, (CHUNK, CHUNK), 0)
    s = lax.broadcasted_iota(jnp.int32, (CHUNK, CHUNK), 1)
    return s <= t


def _gate_specs(tm, rows, width):
    per = tm // HALO_BF16
    last = rows // HALO_BF16 - 1
    cur = pl.BlockSpec((tm, width), lambda i: (i, 0))
    prev = pl.BlockSpec((HALO_BF16, width), lambda i: (jnp.maximum(i * per - 1, 0), 0))
    nxt = pl.BlockSpec((HALO_BF16, width), lambda i: (jnp.minimum((i + 1) * per, last), 0))
    return cur, prev, nxt


def _cols(ref, lo, hi):
    return ref[:, lo:hi].astype(F32)


def _halo_before(ref, lo, hi):
    return ref[:, lo:hi].astype(F32)[HALO_BF16 - HALO:]


def _halo_after(ref, lo, hi):
    return ref[:, lo:hi].astype(F32)[:HALO]


def _gate_fwd(name, proj, w_s, b_st, cw):
    rows, width = proj.shape
    w = width // 5
    groups = w // CHUNK
    tm = _tile(rows, 256)
    cur, prev, _ = _gate_specs(tm, rows, width)

    def body(p_ref, h_ref, ws_ref, b_ref, cw_ref, o_ref):
        mask = _causal_mask()
        au = _gelu(_cols(p_ref, 0, w))
        vn, _ = _layernorm(_gelu(_cols(p_ref, w, 2 * w)))
        vn = vn.astype(BF16)
        for g in range(groups):
            wc = jnp.where(mask, ws_ref[g], 0.0).astype(BF16)
            cols = slice(g * CHUNK, (g + 1) * CHUNK)
            for ch in range(tm // CHUNK):
                rws = slice(ch * CHUNK, (ch + 1) * CHUNK)
                mixed = jnp.dot(wc, vn[rws, cols], preferred_element_type=F32) + b_ref[:, g:g + 1]
                o_ref[rws, cols] = (au[rws, cols] * mixed).astype(BF16)
        z = _cols(p_ref, 3 * w, 4 * w) * _cols(p_ref, 4 * w, 5 * w)
        zh = _halo_before(h_ref, 3 * w, 4 * w) * _halo_before(h_ref, 4 * w, 5 * w)
        zh = jnp.where(pl.program_id(0) == 0, 0.0, zh)
        y = cw_ref[0:1, :] * _shift_rows(z, zh, 2) + cw_ref[1:2, :] * _shift_rows(z, zh, 1) + cw_ref[2:3, :] * z
        o_ref[:, w:2 * w] = (_cols(p_ref, 2 * w, 3 * w) * y).astype(BF16)

    full = lambda a: pl.BlockSpec(a.shape, lambda i: (0,) * a.ndim)
    return pl.pallas_call(
        body, name=name, grid=(rows // tm,),
        in_specs=[cur, prev, full(w_s), full(b_st), full(cw)],
        out_specs=pl.BlockSpec((tm, 2 * w), lambda i: (i, 0)),
        out_shape=jax.ShapeDtypeStruct((rows, 2 * w), BF16),
        compiler_params=_cp(("parallel",)))(proj, proj, w_s, b_st, cw)


def _gate_bwd(name, proj, d_ab, w_s, b_st, cw):
    rows, width = proj.shape
    w = width // 5
    groups = w // CHUNK
    tm = _tile(rows, 256)
    cur, prev, nxt = _gate_specs(tm, rows, width)
    dcur, _, dnxt = _gate_specs(tm, rows, 2 * w)

    def body(p_ref, ph_ref, pn_ref, d_ref, dn_ref, ws_ref, b_ref, cw_ref, o_ref, dws_ref, dbs_ref, dcw_ref):
        i = pl.program_id(0)

        @pl.when(i == 0)
        def _():
            dws_ref[...] = jnp.zeros_like(dws_ref)
            dbs_ref[...] = jnp.zeros_like(dbs_ref)
            dcw_ref[...] = jnp.zeros_like(dcw_ref)

        mask = _causal_mask()
        u, v = _cols(p_ref, 0, w), _cols(p_ref, w, 2 * w)
        au, av = _gelu(u), _gelu(v)
        vn, rstd = _layernorm(av)
        vnb = vn.astype(BF16)
        d_a = _cols(d_ref, 0, w)
        d_mixed = (d_a * au).astype(BF16)
        ones = jnp.ones((HALO, CHUNK), BF16)
        d_vn_cols = []
        d_au_cols = []
        for g in range(groups):
            wc = jnp.where(mask, ws_ref[g], 0.0).astype(BF16)
            cols = slice(g * CHUNK, (g + 1) * CHUNK)
            dw = jnp.zeros((CHUNK, CHUNK), F32)
            db = jnp.zeros((HALO, CHUNK), F32)
            d_vn_rows, d_au_rows = [], []
            for ch in range(tm // CHUNK):
                rws = slice(ch * CHUNK, (ch + 1) * CHUNK)
                mixed = jnp.dot(wc, vnb[rws, cols], preferred_element_type=F32) + b_ref[:, g:g + 1]
                d_au_rows.append(d_a[rws, cols] * mixed)
                dm = d_mixed[rws, cols]
                dw = dw + lax.dot_general(dm, vnb[rws, cols], NT_DIMS, preferred_element_type=F32)
                db = db + lax.dot_general(ones, dm, NT_DIMS, preferred_element_type=F32)
                d_vn_rows.append(lax.dot_general(wc, dm, TN_DIMS, preferred_element_type=F32))
            dws_ref[g] += jnp.where(mask, dw, 0.0)
            dbs_ref[g:g + 1, :] += db[0:1, :]
            d_vn_cols.append(jnp.concatenate(d_vn_rows, axis=0))
            d_au_cols.append(jnp.concatenate(d_au_rows, axis=0))
        d_vn = jnp.concatenate(d_vn_cols, axis=1)
        d_au = jnp.concatenate(d_au_cols, axis=1)
        d_av = rstd * (d_vn - jnp.mean(d_vn, axis=-1, keepdims=True)
                       - vn * jnp.mean(d_vn * vn, axis=-1, keepdims=True))
        o_ref[:, 0:w] = (d_au * _gelu_grad(u)).astype(BF16)
        o_ref[:, w:2 * w] = (d_av * _gelu_grad(v)).astype(BF16)

        gb, gc, bx = _cols(p_ref, 2 * w, 3 * w), _cols(p_ref, 3 * w, 4 * w), _cols(p_ref, 4 * w, 5 * w)
        z = gc * bx
        zh = jnp.where(i == 0, 0.0, _halo_before(ph_ref, 3 * w, 4 * w) * _halo_before(ph_ref, 4 * w, 5 * w))
        z1, z2 = _shift_rows(z, zh, 1), _shift_rows(z, zh, 2)
        d_b = _cols(d_ref, w, 2 * w)
        y = cw_ref[0:1, :] * z2 + cw_ref[1:2, :] * z1 + cw_ref[2:3, :] * z
        dy = d_b * gb
        dyn = jnp.where(i == pl.num_programs(0) - 1, 0.0,
                        _halo_after(dn_ref, w, 2 * w) * _halo_after(pn_ref, 2 * w, 3 * w))
        dz = (cw_ref[2:3, :] * dy + cw_ref[1:2, :] * _shift_rows_up(dy, dyn, 1)
              + cw_ref[0:1, :] * _shift_rows_up(dy, dyn, 2))
        dcw_ref[0:1, :] += jnp.sum(dy * z2, axis=0, keepdims=True)
        dcw_ref[1:2, :] += jnp.sum(dy * z1, axis=0, keepdims=True)
        dcw_ref[2:3, :] += jnp.sum(dy * z, axis=0, keepdims=True)
        o_ref[:, 2 * w:3 * w] = (d_b * y).astype(BF16)
        o_ref[:, 3 * w:4 * w] = (dz * bx).astype(BF16)
        o_ref[:, 4 * w:5 * w] = (dz * gc).astype(BF16)

    full = lambda a: pl.BlockSpec(a.shape, lambda i: (0,) * a.ndim)
    acc = lambda shape: pl.BlockSpec(shape, lambda i: (0,) * len(shape))
    return pl.pallas_call(
        body, name=name, grid=(rows // tm,),
        in_specs=[cur, prev, nxt, dcur, dnxt, full(w_s), full(b_st), full(cw)],
        out_specs=[pl.BlockSpec((tm, width), lambda i: (i, 0)), acc((groups, CHUNK, CHUNK)),
                   acc((groups, CHUNK)), acc((HALO, w))],
        out_shape=[jax.ShapeDtypeStruct((rows, width), BF16), jax.ShapeDtypeStruct((groups, CHUNK, CHUNK), F32),
                   jax.ShapeDtypeStruct((groups, CHUNK), F32), jax.ShapeDtypeStruct((HALO, w), F32)],
        compiler_params=_cp(("arbitrary",), VMEM_MM))(proj, proj, proj, d_ab, d_ab, w_s, b_st, cw)


def _flat(x):
    return x.reshape(-1, x.shape[-1])


def _rope(t, cosf, sins):
    t2 = _flat(t)
    return (t2 * _flat(cosf) + pltpu.roll(t2, HEAD // 2, 1) * _flat(sins)).reshape(t.shape)


def _rope_bwd(dt, cosf, sins):
    d2 = _flat(dt)
    return (d2 * _flat(cosf) + pltpu.roll(d2 * _flat(sins), HEAD // 2, 1)).reshape(dt.shape)


ATT_UNITS = ATT_TILE // CHUNK


def _attn_units(phases):
    for b, d in enumerate(DILATIONS):
        blocks = ATT_TILE // (CHUNK * d)
        for visit in phases:
            for r in range(d):
                if blocks <= ATT_UNROLL:
                    for j in range(blocks):
                        visit(b, d, r, j, r * blocks + j)
                else:
                    def step(jj, carry, b=b, d=d, r=r, visit=visit, blocks=blocks):
                        for u in range(ATT_UNROLL):
                            j = jj * ATT_UNROLL + u
                            visit(b, d, r, j, r * blocks + j)
                        return carry
                    lax.fori_loop(0, blocks // ATT_UNROLL, step, 0)


class _Unit:
    def __init__(self, d, r, j):
        self.segs = [r + d * k for k in range(SEGS // d)]
        self.w = CHUNK * d // SEGS
        q0 = j * self.w
        self.q0 = q0 if isinstance(q0, int) else pl.multiple_of(q0, HALO)
        k0 = CHUNK + (j - 1) * self.w
        self.k0 = k0 if isinstance(k0, int) else pl.multiple_of(k0, HALO)

    def queries(self, ref):
        return _chunks(ref, self.segs, self.q0, self.w)

    def keys(self, ref):
        return _chunks(ref, self.segs, self.k0, 2 * self.w)

    def put_queries(self, ref, val, add=False):
        _put_chunks(ref, self.segs, self.q0, self.w, val, add)

    def put_keys(self, ref, val, add=False):
        _put_chunks(ref, self.segs, self.k0, 2 * self.w, val, add)


def _chunks(ref, segs, start, size):
    parts = [ref[s, pl.ds(start, size), :] for s in segs]
    return parts[0] if len(parts) == 1 else jnp.concatenate(parts, axis=0)


def _put_chunks(ref, segs, start, size, val, add):
    for k, s in enumerate(segs):
        piece = val[k * size:(k + 1) * size]
        if add:
            ref[s, pl.ds(start, size), :] += piece
        else:
            ref[s, pl.ds(start, size), :] = piece


def _band_bias():
    qi = lax.broadcasted_iota(jnp.int32, (CHUNK, 2 * CHUNK), 0)
    ki = lax.broadcasted_iota(jnp.int32, (CHUNK, 2 * CHUNK), 1)
    tables = []
    for d in DILATIONS:
        nseg, w = SEGS // d, CHUNK * d // SEGS
        pos_q = nseg * (qi % w) + qi // w
        pos_k = nseg * (ki % (2 * w) - w) + ki // (2 * w)
        band = (pos_q >= pos_k) & (pos_q - pos_k <= CHUNK)
        tables += [jnp.where(band, 0.0, -jnp.inf), jnp.where(band & (pos_k >= 0), 0.0, -jnp.inf)]
    return jnp.stack(tables).astype(F32)


def _bias_spec():
    return pl.BlockSpec((2 * len(DILATIONS), CHUNK, 2 * CHUNK), lambda h, n: (0, 0, 0))


def _unit_bias(bias, b, n, j):
    if isinstance(j, int) and j != 0:
        return bias[2 * b]
    return bias[2 * b + jnp.where(jnp.logical_and(n == 0, j == 0), 1, 0)]


def _attn_in_specs(heads):
    blk = (SEGS, CHUNK, HEAD)
    prev = lambda n: jnp.maximum(n - 1, 0)
    return [
        pl.BlockSpec(blk, lambda h, n: (0, n, h)),
        pl.BlockSpec(blk, lambda h, n: (0, n, heads + h)),
        pl.BlockSpec(blk, lambda h, n: (0, prev(n), heads + h)),
        pl.BlockSpec(blk, lambda h, n: (0, n, 2 * heads + h)),
        pl.BlockSpec(blk, lambda h, n: (0, prev(n), 2 * heads + h)),
        pl.BlockSpec(blk, lambda h, n: (0, n, 0)),
        pl.BlockSpec(blk, lambda h, n: (0, n, 0)),
        pl.BlockSpec(blk, lambda h, n: (0, prev(n), 0)),
        pl.BlockSpec(blk, lambda h, n: (0, prev(n), 0)),
    ]


def _attn_load(q_ref, kc_ref, kp_ref, vc_ref, vp_ref, cc_ref, sc_ref, cp_ref, sp_ref, qr, kcat, vcat):
    qr[...] = _rope(q_ref[...].astype(F32), cc_ref[...], sc_ref[...])
    kcat[:, pl.ds(0, CHUNK), :] = _rope(kp_ref[...].astype(F32), cp_ref[...], sp_ref[...])
    kcat[:, pl.ds(CHUNK, CHUNK), :] = _rope(kc_ref[...].astype(F32), cc_ref[...], sc_ref[...])
    vcat[:, pl.ds(0, CHUNK), :] = vp_ref[...].astype(F32)
    vcat[:, pl.ds(CHUNK, CHUNK), :] = vc_ref[...].astype(F32)


def _attn_fwd(name, qkv, cosf, sins, bias):
    t = qkv.shape[0]
    heads = qkv.shape[1] // (3 * HEAD)
    scale = HEAD ** -0.5
    nbr = len(DILATIONS)

    def body(q_ref, kc_ref, kp_ref, vc_ref, vp_ref, cc_ref, sc_ref, cp_ref, sp_ref, bias, o_ref, lse_ref,
             qr, kcat, vcat, obr, lbr, pn):
        n = pl.program_id(1)
        _attn_load(q_ref, kc_ref, kp_ref, vc_ref, vp_ref, cc_ref, sc_ref, cp_ref, sp_ref, qr, kcat, vcat)

        def probs(b, d, r, j, u):
            unit = _Unit(d, r, j)
            s = lax.dot_general(unit.queries(qr).astype(BF16), unit.keys(kcat).astype(BF16), NT_DIMS,
                                preferred_element_type=F32) * scale + _unit_bias(bias, b, n, j)
            mx = jnp.max(s, axis=-1, keepdims=True)
            p = jnp.exp(s - mx)
            den = jnp.sum(p, axis=-1, keepdims=True)
            pn[u] = (p * (1.0 / den)).astype(BF16)
            unit.put_queries(lbr.at[b], jnp.broadcast_to(mx + jnp.log(den), (CHUNK, HEAD)))

        def values(b, d, r, j, u):
            unit = _Unit(d, r, j)
            unit.put_queries(obr.at[b], jnp.dot(pn[u], unit.keys(vcat).astype(BF16), preferred_element_type=F32))

        _attn_units([probs, values])
        ls = [lbr[b] for b in range(nbr)]
        top = functools.reduce(jnp.maximum, ls)
        ws = [jnp.exp(l - top) for l in ls]
        tot = functools.reduce(jnp.add, ws)
        inv = 1.0 / tot
        o = (ws[0] * inv) * obr[0]
        for b in range(1, nbr):
            o = o + (ws[b] * inv) * obr[b]
        o_ref[...] = o.astype(BF16)
        lse_ref[...] = top + jnp.log(tot)

    blk = (SEGS, CHUNK, HEAD)
    keys = pltpu.VMEM((SEGS, 2 * CHUNK, HEAD), F32)
    tile = pl.BlockSpec(blk, lambda h, n: (0, n, h))
    seg = t // SEGS
    qkv3, cos3, sin3 = _seg_view(qkv), _seg_view(cosf), _seg_view(sins)
    o, lse = pl.pallas_call(
        body, name=name, grid=(heads, t // ATT_TILE), in_specs=_attn_in_specs(heads) + [_bias_spec()],
        out_specs=[tile, tile],
        out_shape=[jax.ShapeDtypeStruct((SEGS, seg, heads * HEAD), BF16),
                   jax.ShapeDtypeStruct((SEGS, seg, heads * HEAD), F32)],
        scratch_shapes=[pltpu.VMEM(blk, F32), keys, keys,
                        pltpu.VMEM((nbr,) + blk, F32), pltpu.VMEM((nbr,) + blk, F32),
                        pltpu.VMEM((ATT_UNITS, CHUNK, 2 * CHUNK), BF16)],
        compiler_params=_cp(("parallel", "parallel"), VMEM_MM),
    )(qkv3, qkv3, qkv3, qkv3, qkv3, cos3, sin3, cos3, sin3, bias)
    return o.reshape(t, heads * HEAD), lse.reshape(t, heads * HEAD)


def _attn_bwd(name, qkv, cosf, sins, bias, d_o, o, lse):
    t = qkv.shape[0]
    heads = qkv.shape[1] // (3 * HEAD)
    scale = HEAD ** -0.5

    def body(q_ref, kc_ref, kp_ref, vc_ref, vp_ref, cc_ref, sc_ref, cp_ref, sp_ref, do_ref, o_ref, lse_ref, bias,
             dq_ref, dko_ref, dkp_ref, dvo_ref, dvp_ref, qr, kcat, vcat, dq_acc, dk_acc, dv_acc, delta, ps, dss):
        n = pl.program_id(1)
        _attn_load(q_ref, kc_ref, kp_ref, vc_ref, vp_ref, cc_ref, sc_ref, cp_ref, sp_ref, qr, kcat, vcat)
        dq_acc[...] = jnp.zeros_like(dq_acc)
        dk_acc[...] = jnp.zeros_like(dk_acc)
        dv_acc[...] = jnp.zeros_like(dv_acc)
        delta[...] = jnp.broadcast_to(
            jnp.sum(do_ref[...] * o_ref[...].astype(F32), axis=-1, keepdims=True), delta.shape)

        def probs(b, d, r, j, u):
            unit = _Unit(d, r, j)
            s = lax.dot_general(unit.queries(qr).astype(BF16), unit.keys(kcat).astype(BF16), NT_DIMS,
                                preferred_element_type=F32) * scale + _unit_bias(bias, b, n, j)
            ps[u] = jnp.exp(s - unit.queries(lse_ref)[:, 0:1]).astype(BF16)

        def score_grads(b, d, r, j, u):
            unit = _Unit(d, r, j)
            dp = lax.dot_general(unit.queries(do_ref).astype(BF16), unit.keys(vcat).astype(BF16), NT_DIMS,
                                 preferred_element_type=F32)
            dss[u] = (ps[u].astype(F32) * (dp - unit.queries(delta)[:, 0:1]) * scale).astype(BF16)

        def input_grads(b, d, r, j, u):
            unit = _Unit(d, r, j)
            ds = dss[u]
            unit.put_queries(dq_acc, jnp.dot(ds, unit.keys(kcat).astype(BF16), preferred_element_type=F32), add=True)
            unit.put_keys(dk_acc, lax.dot_general(ds, unit.queries(qr).astype(BF16), TN_DIMS,
                                                  preferred_element_type=F32), add=True)
            unit.put_keys(dv_acc, lax.dot_general(ps[u], unit.queries(do_ref).astype(BF16), TN_DIMS,
                                                  preferred_element_type=F32), add=True)

        _attn_units([probs, score_grads, input_grads])
        dq_ref[...] = _rope_bwd(dq_acc[...], cc_ref[...], sc_ref[...]).astype(BF16)
        dkp_ref[...] = _rope_bwd(dk_acc[:, pl.ds(0, CHUNK), :], cp_ref[...], sp_ref[...]).astype(BF16)
        dko_ref[...] = _rope_bwd(dk_acc[:, pl.ds(CHUNK, CHUNK), :], cc_ref[...], sc_ref[...]).astype(BF16)
        dvp_ref[...] = dv_acc[:, pl.ds(0, CHUNK), :].astype(BF16)
        dvo_ref[...] = dv_acc[:, pl.ds(CHUNK, CHUNK), :].astype(BF16)

    blk = (SEGS, CHUNK, HEAD)
    tile = pl.BlockSpec(blk, lambda h, n: (0, n, h))
    big = pltpu.VMEM((SEGS, 2 * CHUNK, HEAD), F32)
    seg = t // SEGS
    qkv3, cos3, sin3 = _seg_view(qkv), _seg_view(cosf), _seg_view(sins)
    return pl.pallas_call(
        body, name=name, grid=(heads, t // ATT_TILE),
        in_specs=_attn_in_specs(heads) + [tile, tile, tile, _bias_spec()],
        out_specs=[tile] * 5,
        out_shape=[jax.ShapeDtypeStruct((SEGS, seg, heads * HEAD), BF16)] * 5,
        scratch_shapes=[pltpu.VMEM(blk, F32), big, big, pltpu.VMEM(blk, F32), big, big, pltpu.VMEM(blk, F32),
                        pltpu.VMEM((ATT_UNITS, CHUNK, 2 * CHUNK), BF16), pltpu.VMEM((ATT_UNITS, CHUNK, 2 * CHUNK), BF16)],
        compiler_params=_cp(("parallel", "parallel"), 60 << 20),
    )(qkv3, qkv3, qkv3, qkv3, qkv3, cos3, sin3, cos3, sin3, _seg_view(d_o), _seg_view(o), _seg_view(lse), bias)


def _attn_merge(name, dq, dk_own, dk_prev, dv_own, dv_prev):
    _, seg, hd = dq.shape
    nt = seg // CHUNK
    tw = _tile(hd, 512)

    def body(dq_ref, dko_ref, dkn_ref, dvo_ref, dvn_ref, o_ref):
        last = pl.program_id(0) == nt - 1
        part = pl.program_id(1)

        @pl.when(part == 0)
        def _():
            o_ref[...] = dq_ref[...]

        @pl.when(part == 1)
        def _():
            o_ref[...] = (dko_ref[...].astype(F32) + jnp.where(last, 0.0, dkn_ref[...].astype(F32))).astype(BF16)

        @pl.when(part == 2)
        def _():
            o_ref[...] = (dvo_ref[...].astype(F32) + jnp.where(last, 0.0, dvn_ref[...].astype(F32))).astype(BF16)

    blk = (SEGS, CHUNK, tw)

    def own(part):
        return pl.BlockSpec(blk, lambda n, p, c: (0, jnp.where(p == part, n, 0), jnp.where(p == part, c, 0)))

    def nxt(part):
        return pl.BlockSpec(blk, lambda n, p, c: (0, jnp.where(p == part, jnp.minimum(n + 1, nt - 1), 0),
                                                  jnp.where(p == part, c, 0)))

    per = hd // tw
    return pl.pallas_call(
        body, name=name, grid=(nt, 3, per), in_specs=[own(0), own(1), nxt(1), own(2), nxt(2)],
        out_specs=pl.BlockSpec(blk, lambda n, p, c: (0, n, p * per + c)),
        out_shape=jax.ShapeDtypeStruct((SEGS, seg, 3 * hd), BF16),
        compiler_params=_cp(("parallel", "parallel", "parallel"), VMEM_MM),
    )(dq, dk_own, dk_prev, dv_own, dv_prev).reshape(SEGS * seg, 3 * hd)


def _sum_parts(name, parts):
    nparts, rows, cols = parts.shape
    tr = _tile(rows, 256)

    def body(p_ref, o_ref):
        s = p_ref[0]
        for k in range(1, nparts):
            s = s + p_ref[k]
        o_ref[...] = s

    return pl.pallas_call(
        body, name=name, grid=(rows // tr,),
        in_specs=[pl.BlockSpec((nparts, tr, cols), lambda i: (0, i, 0))],
        out_specs=pl.BlockSpec((tr, cols), lambda i: (i, 0)),
        out_shape=jax.ShapeDtypeStruct((rows, cols), F32),
        compiler_params=_cp(("parallel",)))(parts)


def _rows128(a, pad_to=8):
    flat = a.reshape(-1)
    rows = -(-flat.shape[0] // 128)
    rows = -(-rows // pad_to) * pad_to
    flat = jnp.pad(flat, (0, rows * 128 - flat.shape[0]))
    return flat.reshape(rows, 128)


def _pack(arrays):
    return jnp.concatenate([_rows128(a) for a in arrays], axis=0)


def _unpack(packed, like):
    out, at = [], 0
    for a in like:
        size = 1
        for s in a.shape:
            size *= s
        rows = -(-(-(-size // 128)) // 8) * 8
        out.append(packed[at:at + rows].reshape(-1)[:size].reshape(a.shape))
        at += rows
    return out


def kernel(x, norm_mix_pre, norm_mix_post, norm_mlp_pre, norm_mlp_post, w_in_ab, w_spatial, b_spatial, conv_w, w_out_ab, w_qkv, w_o, w_up, w_down, loss_target, m_norm_mix_pre, m_norm_mix_post, m_norm_mlp_pre, m_norm_mlp_post, m_w_in_ab, m_w_spatial, m_b_spatial, m_conv_w, m_w_out_ab, m_w_qkv, m_w_o, m_w_up, m_w_down, v_norm_mix_pre, v_norm_mix_post, v_norm_mlp_pre, v_norm_mlp_post, v_w_in_ab, v_w_spatial, v_b_spatial, v_conv_w, v_w_out_ab, v_w_qkv, v_w_o, v_w_up, v_w_down):
    depth = norm_mix_pre.shape[0]
    seq, dm = x.shape[1], x.shape[2]
    h0 = x.reshape(seq, dm)
    target = loss_target.reshape(seq, dm)
    ax, ay, ac = lax.axis_index("x"), lax.axis_index("y"), lax.axis_index("c")
    my_block = 4 * ax + 2 * ay + ac
    block = jnp.reshape(my_block, (1,)).astype(jnp.int32)

    half = HEAD // 2
    inv_freq = ROPE_THETA ** (-jnp.arange(half, dtype=F32) * 2.0 / HEAD)
    ang = jnp.arange(seq, dtype=jnp.int32).astype(F32)[:, None] * inv_freq[None, :]
    ang = ang.reshape(seq // SEGS, SEGS, half).transpose(1, 0, 2).reshape(seq, half)
    cosf = jnp.concatenate([jnp.cos(ang), jnp.cos(ang)], axis=-1)
    sins = jnp.concatenate([-jnp.sin(ang), jnp.sin(ang)], axis=-1)
    band_bias = _band_bias()

    big = {"w_in_ab": w_in_ab, "w_out_ab": w_out_ab, "w_qkv": w_qkv, "w_o": w_o, "w_up": w_up, "w_down": w_down}
    use_order = []
    for l in range(depth):
        use_order += [("w_in_ab", l // 2), ("w_out_ab", l // 2)] if l % 2 == 0 else [("w_qkv", l // 2), ("w_o", l // 2)]
        use_order += [("w_up", l), ("w_down", l)]
    n_even = w_in_ab.shape[0]
    cw_rows = jnp.pad(conv_w.reshape(n_even * CONV_TAPS, conv_w.shape[2]), ((0, HALO - (n_even * CONV_TAPS) % HALO), (0, 0)))
    cw_gathered = _all_gather("ag_conv", [cw_rows])[0]
    first = [k for k in use_order if k in (("w_in_ab", 0), ("w_out_ab", 0), ("w_up", 0), ("w_down", 0))]
    rest = [k for k in use_order if k not in first]
    lands_a, sems_a = _ag_start("ag_start_first", [_cast_fill(f"cast_{nm}_{l}", big[nm], l, block) for nm, l in first],
                                cw_gathered)
    lands_b, sems_b = _ag_start("ag_start_rest", [_cast_fill(f"cast_{nm}_{l}", big[nm], l, block) for nm, l in rest],
                                lands_a[0])
    lands = dict(zip(first + rest, list(lands_a) + list(lands_b)))
    ag_sems = dict(zip(first + rest, list(sems_a) + list(sems_b)))
    passed_on, wg = [], {}

    def weight(key, after):
        pins = []
        if key not in wg:
            upto = min(use_order.index(key) + 1, len(use_order) - 1)
            for k in use_order[len(passed_on):upto + 1]:
                lands[k] = _ag_mid(f"ag_mid_{k[0]}_{k[1]}", lands[k], ag_sems[k], after)
                passed_on.append(k)
                if k != key and use_order.index(k) >= AG_PIN_FROM:
                    pins.append(lands[k])
            wg[key] = _ag_wait(f"ag_wait_{key[0]}_{key[1]}", lands[key], ag_sems[key], after)
        return wg[key], pins

    cw_all = cw_gathered[:, :n_even * CONV_TAPS].reshape(NDEV, n_even, CONV_TAPS, -1)
    cw_all = jnp.transpose(cw_all, (1, 2, 0, 3)).reshape(n_even, CONV_TAPS, -1)
    cw_full = [jnp.pad(cw_all[e], ((0, HALO - CONV_TAPS), (0, 0))) for e in range(n_even)]

    def rows_nat(blk):
        return blk.reshape(blk.shape[0] * blk.shape[1], blk.shape[2])

    saved = []
    hn = _norm_fwd("norm_first", h0, g_pre=norm_mix_pre[0][None])[0]
    h = h0
    for l in range(depth):
        s = {"h_in": h, "hn1": hn}
        if l % 2 == 0:
            e = l // 2
            w_, pins = weight(("w_in_ab", e), hn)
            proj = _mm_nn_blk(f"fwd_in_{l}", hn, w_, after=pins)
            ab = _gate_fwd(f"gate_fwd_{l}", proj, w_spatial[e], b_spatial[e].T, cw_full[e])
            w_, pins = weight(("w_out_ab", e), ab)
            mix = _mm_nn(f"fwd_out_{l}", ab, rows_nat(w_), after=pins)
            s.update(proj=proj, ab=ab)
        else:
            o_ = l // 2
            w_, pins = weight(("w_qkv", o_), hn)
            qkv = _mm_nn_blk(f"fwd_qkv_{l}", hn, w_, after=pins)
            att, lse = _attn_fwd(f"attn_fwd_{l}", qkv, cosf, sins, band_bias)
            w_, pins = weight(("w_o", o_), att)
            mix = _mm_nn(f"fwd_o_{l}", att, rows_nat(w_), after=pins)
            s.update(qkv=qkv, att=att, lse=lse)
        h1, hn2 = _norm_fwd(f"norm_mid_{l}", h, mix, norm_mix_post[l][None], norm_mlp_pre[l][None], seg_z=l % 2 == 1)
        w_, pins = weight(("w_up", l), hn2)
        act = _mm_nn_blk(f"fwd_up_{l}", hn2, w_, relu2=True, after=pins)
        w_, pins = weight(("w_down", l), act)
        f = _mm_nn(f"fwd_down_{l}", act, rows_nat(w_), after=pins)
        s.update(mix=mix, h1=h1, hn2=hn2, act=act, f=f)
        if l + 1 < depth:
            h, hn = _norm_fwd(f"norm_end_{l}", h1, f, norm_mlp_post[l][None], norm_mix_pre[l + 1][None],
                              seg_y=(l + 1) % 2 == 1)
        else:
            h = _norm_fwd(f"norm_end_{l}", h1, f, norm_mlp_post[l][None])[0]
        saved.append(s)

    d_h, loss_row = _loss_grad("loss", h, target)
    rs = {}

    def scatter(key, g):
        rs[key] = _rs_start(f"rs_start_{key[0]}_{key[1]}", g.reshape(NDEV, -1, g.shape[-1]))

    dg ={nm: [None] * depth for nm in ("norm_mix_pre", "norm_mix_post", "norm_mlp_pre", "norm_mlp_post")}
    d_ws, d_bs, d_cw = [None] * n_even, [None] * n_even, [None] * n_even
    d_hn_next = None
    for l in reversed(range(depth)):
        s = saved[l]
        if l == depth - 1:
            d_f, dg["norm_mlp_post"][l] = _norm_bwd(f"nb_end_{l}", d_h, post=(s["f"], norm_mlp_post[l][None]))
        else:
            d_h, dg["norm_mix_pre"][l + 1], d_f, dg["norm_mlp_post"][l] = _norm_bwd(
                f"nb_end_{l}", d_h, pre=(d_hn_next, saved[l + 1]["h_in"], norm_mix_pre[l + 1][None]),
                post=(s["f"], norm_mlp_post[l][None]), seg_dy=(l + 1) % 2 == 1)
        wd = rows_nat(wg[("w_down", l)])
        d_up = _mm_nt_rows(f"bwd_down_{l}", d_f, wd, act=s["act"])
        scatter(("w_down", l), _mm_tn(f"gw_down_{l}", s["act"], d_f))
        scatter(("w_up", l), _mm_tn(f"gw_up_{l}", s["hn2"], d_up, nb=w_up.shape[2]))
        d_hn2 = _mm_nt_blk(f"bwd_up_{l}", d_up, wg[("w_up", l)], after=[rs[("w_down", l)][0], rs[("w_up", l)][0]])
        d_h, dg["norm_mlp_pre"][l], d_mix, dg["norm_mix_post"][l] = _norm_bwd(
            f"nb_mid_{l}", d_h, pre=(d_hn2, s["h1"], norm_mlp_pre[l][None]),
            post=(s["mix"], norm_mix_post[l][None]), seg_z=l % 2 == 1)
        if l % 2 == 0:
            e = l // 2
            wo = rows_nat(wg[("w_out_ab", e)])
            d_ab = _mm_nt_rows(f"bwd_out_{l}", d_mix, wo)
            scatter(("w_out_ab", e), _mm_tn(f"gw_out_{l}", s["ab"], d_mix))
            d_proj, d_ws[e], d_bs[e], d_cw[e] = _gate_bwd(
                f"gate_bwd_{l}", s["proj"], d_ab, w_spatial[e], b_spatial[e].T, cw_full[e])
            scatter(("w_in_ab", e), _mm_tn(f"gw_in_{l}", s["hn1"], d_proj, nb=w_in_ab.shape[2]))
            d_hn_next = _mm_nt_blk(f"bwd_in_{l}", d_proj, wg[("w_in_ab", e)],
                                   after=[rs[("w_out_ab", e)][0], rs[("w_in_ab", e)][0]])
        else:
            o_ = l // 2
            wo = rows_nat(wg[("w_o", o_)])
            d_att = _mm_nt_rows(f"bwd_o_{l}", d_mix, wo, out_dtype=F32)
            scatter(("w_o", o_), _mm_tn(f"gw_o_{l}", s["att"], d_mix))
            parts = _attn_bwd(f"attn_bwd_{l}", s["qkv"], cosf, sins, band_bias, d_att, s["att"], s["lse"])
            d_qkv = _attn_merge(f"attn_merge_{l}", *parts)
            scatter(("w_qkv", o_), _mm_tn(f"gw_qkv_{l}", s["hn1"], d_qkv, nb=w_qkv.shape[2]))
            d_hn_next = _mm_nt_blk(f"bwd_qkv_{l}", d_qkv, wg[("w_qkv", o_)],
                                   after=[rs[("w_o", o_)][0], rs[("w_qkv", o_)][0]])
    grad_x, dg["norm_mix_pre"][0] = _norm_bwd("nb_first", d_h, pre=(d_hn_next, h0, norm_mix_pre[0][None]))

    small_g = ([jnp.concatenate(dg[nm], axis=0) for nm in dg]
               + [jnp.stack(d_ws), jnp.stack(d_bs), jnp.stack([c[:CONV_TAPS] for c in d_cw]), loss_row])
    small_land, small_send, small_recv = _ag_direct_start("ag_small_start", _fill_slot("fill_small", _pack(small_g), block))

    moments = {"w_in_ab": (m_w_in_ab, v_w_in_ab), "w_out_ab": (m_w_out_ab, v_w_out_ab), "w_qkv": (m_w_qkv, v_w_qkv),
               "w_o": (m_w_o, v_w_o), "w_up": (m_w_up, v_w_up), "w_down": (m_w_down, v_w_down)}
    out_big = {}
    behind = small_land
    for nm in ("w_o", "w_qkv", "w_down", "w_up", "w_out_ab", "w_in_ab"):
        own, landed = [], []
        for l in range(big[nm].shape[0]):
            g, land = _wait_all(f"rs_wait_{nm}_{l}", *rs[(nm, l)], behind)
            own.append(g)
            landed.append(land)
        out_big[nm] = _adamw_layers(f"adamw_{nm}", own, landed, block, big[nm], moments[nm][0], moments[nm][1])
        behind = out_big[nm][0]

    summed = _sum_parts("sum_small", _ag_direct_wait("ag_small_wait", small_land, small_send, small_recv, behind))
    g_nmp, g_nmo, g_nlp, g_nlo, g_ws, g_bs, g_cw_all, loss_sum = _unpack(summed, small_g)
    loss = loss_sum[0, 0]
    cwb = conv_w.shape[2]
    g_cw = lax.dynamic_slice_in_dim(g_cw_all, my_block * cwb, cwb, axis=2)
    small_w = [norm_mix_pre, norm_mix_post, norm_mlp_pre, norm_mlp_post, w_spatial, b_spatial, conv_w]
    small_m = [m_norm_mix_pre, m_norm_mix_post, m_norm_mlp_pre, m_norm_mlp_post, m_w_spatial, m_b_spatial, m_conv_w]
    small_v = [v_norm_mix_pre, v_norm_mix_post, v_norm_mlp_pre, v_norm_mlp_post, v_w_spatial, v_b_spatial, v_conv_w]
    small_grad = [g_nmp, g_nmo, g_nlp, g_nlo, g_ws, g_bs, g_cw]
    upd = _adamw("adamw_small", _pack(small_grad)[None], _pack(small_w), _pack(small_m), _pack(small_v))
    sg, sd, sm, sv = [_unpack(u, small_w) for u in upd]

    def outs(i_small, i_big):
        return (i_small[0], i_small[1], i_small[2], i_small[3], i_big["w_in_ab"], i_small[4], i_small[5], i_small[6],
                i_big["w_out_ab"], i_big["w_qkv"], i_big["w_o"], i_big["w_up"], i_big["w_down"])

    pick = lambda i: {nm: out_big[nm][i] for nm in big}
    return (loss, grad_x.reshape(x.shape), *outs(sg, pick(0)), *outs(sd, pick(1)), *outs(sm, pick(2)),
            *outs(sv, pick(3)))
```

```python
import functools

import jax
import jax.numpy as jnp
from jax import lax
from jax.experimental import pallas as pl
from jax.experimental.pallas import tpu as pltpu

F32 = jnp.float32
BF16 = jnp.bfloat16
MESH = pl.DeviceIdType.MESH
ANY = pl.BlockSpec(memory_space=pl.ANY)
HBM = pl.BlockSpec(memory_space=pltpu.HBM)
SEM = pl.BlockSpec(memory_space=pltpu.SEMAPHORE)
EFFECT = pltpu.SideEffectType.DATAFLOW_SIDE_EFFECTING

NDEV = 8
NCHIP = 4
RMS_EPS = 1e-6
LN_EPS = 1e-5
CHUNK = 128
HEAD = 128
ATT_TILE = 2048
ATT_UNROLL = 16
DILATIONS = (1, 4, 16)
SEGS = 16
ROPE_THETA = 10000.0
CONV_TAPS = 3
HALO = 8
HALO_BF16 = 16
GELU_C = 0.7978845608028654
GELU_A = 0.044715
ADAM_LR, ADAM_B1, ADAM_B2, ADAM_EPS, ADAM_WD, ADAM_STEP = 0.001, 0.9, 0.999, 1e-08, 0.01, 10
VMEM_MM = 52 << 20
VMEM_EW = 40 << 20


def _cp(sem=None, vmem=VMEM_EW):
    if sem is None:
        return pltpu.CompilerParams(vmem_limit_bytes=vmem)
    return pltpu.CompilerParams(dimension_semantics=sem, vmem_limit_bytes=vmem)


def _tile(n, want):
    return want if n % want == 0 else n


def _all_gather(name, shards, after=()):
    n = len(shards)
    after = list(after)

    def body(*refs):
        ins, outs = refs[:n], refs[n + len(after):2 * n + len(after)]
        send_sems, recv_sems, local_sems = refs[2 * n + len(after):]
        x, y, c = lax.axis_index("x"), lax.axis_index("y"), lax.axis_index("c")
        me, sibling = (x, y, c), (x, y, 1 - c)
        chips = [(1 - x, y), (x, 1 - y), (1 - x, 1 - y)]

        def slot(p):
            return 4 * p[0] + 2 * p[1] + p[2]

        def copy(i, k, block, to, src=None):
            dst = outs[i].at[slot(block)]
            return pltpu.make_async_remote_copy(
                src_ref=dst if src is None else src, dst_ref=dst,
                send_sem=send_sems.at[i, k], recv_sem=recv_sems.at[i, k],
                device_id=to, device_id_type=MESH)

        mine = [pltpu.make_async_copy(ins[i], outs[i].at[slot(me)], local_sems.at[i]) for i in range(n)]
        for cp in mine:
            cp.start()
        first = []
        for i in range(n):
            first.append(copy(i, 0, me, sibling, src=ins[i]))
            for j, chip in enumerate(chips):
                first.append(copy(i, 1 + j, me, (*chip, c), src=ins[i]))
        for cp in first:
            cp.start()
        passed = []
        for j, chip in enumerate(chips):
            for i in range(n):
                copy(i, 1 + j, (*chip, c), me).wait_recv()
                fwd = copy(i, 4 + j, (*chip, c), sibling)
                fwd.start()
                passed.append(fwd)
        for i in range(n):
            copy(i, 0, sibling, me).wait_recv()
            for j, chip in enumerate(chips):
                copy(i, 4 + j, (*chip, 1 - c), me).wait_recv()
        for cp in first + passed:
            cp.wait_send()
        for cp in mine:
            cp.wait()

    return pl.pallas_call(
        body, name=name,
        out_shape=[jax.ShapeDtypeStruct((NDEV,) + s.shape, s.dtype) for s in shards],
        in_specs=[ANY] * (n + len(after)), out_specs=[ANY] * n,
        scratch_shapes=[pltpu.SemaphoreType.DMA((n, 7)), pltpu.SemaphoreType.DMA((n, 7)),
                        pltpu.SemaphoreType.DMA((n,))],
    )(*shards, *after)


def _peer(x, y, c, r):
    return (1 - x if r & 4 else x, 1 - y if r & 2 else y, 1 - c if r & 1 else c)


def _slot(p):
    return 4 * p[0] + 2 * p[1] + p[2]


def _cast_fill(name, w, layer, block):
    _, rows, cols = w.shape
    tr = _tile(rows, 256)

    def body(blk_ref, w_ref, o_ref):
        o_ref[...] = w_ref[...].astype(BF16)

    return pl.pallas_call(
        body, name=name,
        grid_spec=pltpu.PrefetchScalarGridSpec(
            num_scalar_prefetch=1, grid=(rows // tr,),
            in_specs=[pl.BlockSpec((None, tr, cols), lambda i, blk: (layer, i, 0))],
            out_specs=pl.BlockSpec((None, tr, cols), lambda i, blk: (blk[0], i, 0))),
        out_shape=jax.ShapeDtypeStruct((NDEV, rows, cols), BF16),
        compiler_params=_cp(("parallel",)))(block, w)


OTHER_CHIPS = (2, 4, 6)
AG_SEMS = 6
AG_PIN_FROM = 5


def _ag_start(name, lands, after):
    n = len(lands)

    def body(*refs):
        ins, sems = refs[:n], refs[n + 1:n + 1 + AG_SEMS * n]
        x, y, c = lax.axis_index("x"), lax.axis_index("y"), lax.axis_index("c")
        mine = _slot((x, y, c))
        for i in range(n):
            send_a, *recv_a, _, recv_b = sems[AG_SEMS * i:AG_SEMS * (i + 1)]
            block = ins[i].at[mine]
            pltpu.make_async_remote_copy(src_ref=block, dst_ref=block, send_sem=send_a, recv_sem=recv_b,
                                         device_id=_peer(x, y, c, 1), device_id_type=MESH).start()
            for k, r in enumerate(OTHER_CHIPS):
                pltpu.make_async_remote_copy(src_ref=block, dst_ref=block, send_sem=send_a, recv_sem=recv_a[k],
                                             device_id=_peer(x, y, c, r), device_id_type=MESH).start()

    outs = pl.pallas_call(
        body, name=name,
        out_shape=[pltpu.SemaphoreType.DMA(())] * (AG_SEMS * n) + [pltpu.HBM(a.shape, a.dtype) for a in lands],
        in_specs=[HBM] * n + [ANY], out_specs=[SEM] * (AG_SEMS * n) + [HBM] * n,
        input_output_aliases={i: AG_SEMS * n + i for i in range(n)},
        compiler_params=pltpu.CompilerParams(has_side_effects=EFFECT),
    )(*[pltpu.with_memory_space_constraint(a, pltpu.HBM) for a in lands], after)
    return outs[AG_SEMS * n:], [tuple(outs[AG_SEMS * i:AG_SEMS * (i + 1)]) for i in range(n)]


def _ag_mid(name, land, sems, after):
    _, *recv_a, send_b, recv_b = sems

    def body(land_ref, ra0, ra1, ra2, send_b_ref, recv_b_ref, after_ref, land_out):
        x, y, c = lax.axis_index("x"), lax.axis_index("y"), lax.axis_index("c")
        sibling = _peer(x, y, c, 1)
        for arrival, r in zip((ra0, ra1, ra2), OTHER_CHIPS):
            block = land_ref.at[_slot(_peer(x, y, c, r))]
            pltpu.make_async_remote_copy(src_ref=block, dst_ref=block, send_sem=send_b_ref, recv_sem=arrival,
                                         device_id=sibling, device_id_type=MESH).wait_recv()
            pltpu.make_async_remote_copy(src_ref=block, dst_ref=block, send_sem=send_b_ref, recv_sem=recv_b_ref,
                                         device_id=sibling, device_id_type=MESH).start()

    return pl.pallas_call(
        body, name=name, out_shape=pltpu.HBM(land.shape, land.dtype),
        in_specs=[HBM] + [SEM] * 5 + [ANY], out_specs=HBM, input_output_aliases={0: 0},
        compiler_params=pltpu.CompilerParams(has_side_effects=EFFECT),
    )(land, *recv_a, send_b, recv_b, after)


def _ag_wait(name, land, sems, after):
    send_a, _, _, _, send_b, recv_b = sems

    def body(land_ref, send_a_ref, send_b_ref, recv_b_ref, after_ref, land_out):
        x, y, c = lax.axis_index("x"), lax.axis_index("y"), lax.axis_index("c")
        sibling = _peer(x, y, c, 1)
        four = land_ref.at[pl.ds(0, 1 + len(OTHER_CHIPS))]
        three = land_ref.at[pl.ds(0, len(OTHER_CHIPS))]
        first = pltpu.make_async_remote_copy(src_ref=four, dst_ref=four, send_sem=send_a_ref, recv_sem=recv_b_ref,
                                             device_id=sibling, device_id_type=MESH)
        passed = pltpu.make_async_remote_copy(src_ref=three, dst_ref=three, send_sem=send_b_ref, recv_sem=recv_b_ref,
                                              device_id=sibling, device_id_type=MESH)
        first.wait_send()
        passed.wait_send()
        first.wait_recv()

    return pl.pallas_call(
        body, name=name, out_shape=pltpu.HBM(land.shape, land.dtype),
        in_specs=[HBM, SEM, SEM, SEM, ANY], out_specs=HBM, input_output_aliases={0: 0},
        compiler_params=pltpu.CompilerParams(has_side_effects=EFFECT),
    )(land, send_a, send_b, recv_b, after)


def _fill_slot(name, rows, block):
    r, c = rows.shape

    def body(blk_ref, i_ref, o_ref):
        o_ref[...] = i_ref[...]

    return pl.pallas_call(
        body, name=name,
        grid_spec=pltpu.PrefetchScalarGridSpec(
            num_scalar_prefetch=1, grid=(1,),
            in_specs=[pl.BlockSpec((r, c), lambda i, blk: (0, 0))],
            out_specs=pl.BlockSpec((None, r, c), lambda i, blk: (blk[0], 0, 0))),
        out_shape=jax.ShapeDtypeStruct((NDEV, r, c), rows.dtype),
        compiler_params=_cp(("arbitrary",)))(block, rows)


def _ag_direct_start(name, land):
    def body(land_ref, send, recv, land_out):
        x, y, c = lax.axis_index("x"), lax.axis_index("y"), lax.axis_index("c")
        block = land_ref.at[_slot((x, y, c))]
        for r in range(1, NDEV):
            pltpu.make_async_remote_copy(src_ref=block, dst_ref=block, send_sem=send, recv_sem=recv,
                                         device_id=_peer(x, y, c, r), device_id_type=MESH).start()

    send, recv, land_thru = pl.pallas_call(
        body, name=name,
        out_shape=[pltpu.SemaphoreType.DMA(()), pltpu.SemaphoreType.DMA(()), pltpu.HBM(land.shape, land.dtype)],
        in_specs=[HBM], out_specs=[SEM, SEM, HBM], input_output_aliases={0: 2},
        compiler_params=pltpu.CompilerParams(has_side_effects=EFFECT),
    )(pltpu.with_memory_space_constraint(land, pltpu.HBM))
    return land_thru, send, recv


def _ag_direct_wait(name, land, send, recv, after):
    def body(land_ref, send_ref, recv_ref, after_ref, land_out):
        x, y, c = lax.axis_index("x"), lax.axis_index("y"), lax.axis_index("c")
        seven = land_ref.at[pl.ds(0, NDEV - 1)]
        copy = pltpu.make_async_remote_copy(src_ref=seven, dst_ref=seven, send_sem=send_ref, recv_sem=recv_ref,
                                            device_id=_peer(x, y, c, 1), device_id_type=MESH)
        copy.wait_send()
        copy.wait_recv()

    return pl.pallas_call(
        body, name=name, out_shape=pltpu.HBM(land.shape, land.dtype),
        in_specs=[HBM, SEM, SEM, ANY], out_specs=HBM, input_output_aliases={0: 0},
        compiler_params=pltpu.CompilerParams(has_side_effects=EFFECT),
    )(land, send, recv, after)


def _wait_all(name, src, land, send, recv, after):
    def body(src_ref, land_ref, send_ref, recv_ref, after_ref, src_out, land_out):
        x, y, c = lax.axis_index("x"), lax.axis_index("y"), lax.axis_index("c")
        seven = land_ref.at[pl.ds(0, NDEV - 1)]
        copy = pltpu.make_async_remote_copy(src_ref=seven, dst_ref=seven, send_sem=send_ref, recv_sem=recv_ref,
                                            device_id=_peer(x, y, c, 1), device_id_type=MESH)
        copy.wait_send()
        copy.wait_recv()

    return pl.pallas_call(
        body, name=name,
        out_shape=[pltpu.HBM(src.shape, src.dtype), pltpu.HBM(land.shape, land.dtype)],
        in_specs=[HBM, HBM, SEM, SEM, ANY], out_specs=[HBM, HBM],
        input_output_aliases={0: 0, 1: 1},
        compiler_params=pltpu.CompilerParams(has_side_effects=EFFECT),
    )(src, land, send, recv, after)


def _rs_start(name, grad):
    land = lax.empty((NDEV - 1,) + grad.shape[1:], grad.dtype)

    def body(g_ref, land_ref, send, recv, g_out, land_out):
        x, y, c = lax.axis_index("x"), lax.axis_index("y"), lax.axis_index("c")
        for r in range(1, NDEV):
            peer = _peer(x, y, c, r)
            pltpu.make_async_remote_copy(
                src_ref=g_ref.at[_slot(peer)], dst_ref=land_ref.at[r - 1], send_sem=send, recv_sem=recv,
                device_id=peer, device_id_type=MESH).start()

    send, recv, g_thru, land_thru = pl.pallas_call(
        body, name=name,
        out_shape=[pltpu.SemaphoreType.DMA(()), pltpu.SemaphoreType.DMA(()),
                   pltpu.HBM(grad.shape, grad.dtype), pltpu.HBM(land.shape, land.dtype)],
        in_specs=[HBM, HBM], out_specs=[SEM, SEM, HBM, HBM], input_output_aliases={0: 2, 1: 3},
        compiler_params=pltpu.CompilerParams(has_side_effects=EFFECT),
    )(pltpu.with_memory_space_constraint(grad, pltpu.HBM), pltpu.with_memory_space_constraint(land, pltpu.HBM))
    return g_thru, land_thru, send, recv


def _adam_math(w, g, m, v):
    m = ADAM_B1 * m + (1.0 - ADAM_B1) * g
    v = ADAM_B2 * v + (1.0 - ADAM_B2) * (g * g)
    m_hat = m / (1.0 - ADAM_B1 ** ADAM_STEP)
    v_hat = v / (1.0 - ADAM_B2 ** ADAM_STEP)
    delta = -ADAM_LR * (m_hat / (jnp.sqrt(v_hat) + ADAM_EPS) + ADAM_WD * w)
    return delta, m, v


def _adamw(name, parts, w, m, v):
    nparts, rows, cols = parts.shape
    tr = _tile(rows, 256)

    def body(p_ref, w_ref, m_ref, v_ref, g_out, d_out, m_out, v_out):
        g = p_ref[0].astype(F32)
        for k in range(1, nparts):
            g = g + p_ref[k].astype(F32)
        delta, mn, vn = _adam_math(w_ref[...], g, m_ref[...], v_ref[...])
        g_out[...] = g
        d_out[...] = delta
        m_out[...] = mn
        v_out[...] = vn

    row = pl.BlockSpec((tr, cols), lambda i: (i, 0))
    return pl.pallas_call(
        body, name=name, grid=(rows // tr,),
        in_specs=[pl.BlockSpec((nparts, tr, cols), lambda i: (0, i, 0)), row, row, row],
        out_specs=[row] * 4,
        out_shape=[jax.ShapeDtypeStruct((rows, cols), F32)] * 4,
        compiler_params=_cp(("parallel",)),
    )(parts, w, m, v)


def _adamw_layers(name, grads, lands, block, w, m, v):
    layers, rows, cols = w.shape
    nland = lands[0].shape[0]
    tr = rows
    while tr % 2 == 0 and tr > 8 and nland * tr * cols * 2 > (2 << 20):
        tr //= 2

    def body(blk_ref, *refs):
        own_refs, land_refs = refs[:layers], refs[layers:2 * layers]
        w_ref, m_ref, v_ref, g_out, d_out, m_out, v_out = refs[2 * layers:]
        layer = pl.program_id(0)
        for k in range(layers):
            @pl.when(layer == k)
            def _(k=k):
                g = own_refs[k][...].astype(F32)
                for s in range(nland):
                    g = g + land_refs[k][s].astype(F32)
                delta, mn, vn = _adam_math(w_ref[...], g, m_ref[...], v_ref[...])
                g_out[...] = g
                d_out[...] = delta
                m_out[...] = mn
                v_out[...] = vn

    def own_spec(k):
        return pl.BlockSpec((None, tr, cols), lambda l, i, blk: (blk[0], jnp.where(l == k, i, 0), 0))

    def land_spec(k):
        return pl.BlockSpec((nland, tr, cols), lambda l, i, blk: (0, jnp.where(l == k, i, 0), 0))

    row = pl.BlockSpec((None, tr, cols), lambda l, i, blk: (l, i, 0))
    return pl.pallas_call(
        body, name=name,
        grid_spec=pltpu.PrefetchScalarGridSpec(
            num_scalar_prefetch=1, grid=(layers, rows // tr),
            in_specs=[own_spec(k) for k in range(layers)] + [land_spec(k) for k in range(layers)] + [row, row, row],
            out_specs=[row] * 4),
        out_shape=[jax.ShapeDtypeStruct((layers, rows, cols), F32)] * 4,
        compiler_params=_cp(("arbitrary", "arbitrary")),
    )(block, *grads, *lands, w, m, v)


LANES = 128


def _seg_scratch(rows, d):
    return pltpu.VMEM((d // LANES, rows, LANES), F32)


def _to_segments(vals, scratch, out_ref):
    per = scratch.shape[1] // SEGS
    for c in range(scratch.shape[0]):
        cols = slice(c * LANES, (c + 1) * LANES)
        scratch[c] = vals[:, cols]
        for s in range(SEGS):
            out_ref[s, :, cols] = scratch.at[c][pl.ds(s, per, stride=SEGS), :].astype(out_ref.dtype)


def _from_segments(in_ref, scratch):
    per = scratch.shape[1] // SEGS
    for c in range(scratch.shape[0]):
        for s in range(SEGS):
            scratch.at[c][pl.ds(s, per, stride=SEGS), :] = in_ref[s, :, c * LANES:(c + 1) * LANES].astype(F32)
    return jnp.concatenate([scratch[c] for c in range(scratch.shape[0])], axis=1)


def _seg_view(a):
    return a.reshape(SEGS, a.shape[0] // SEGS, a.shape[1])


def _norm_fwd(name, h, z=None, g_post=None, g_pre=None, seg_z=False, seg_y=False):
    rows, d = h.shape
    tm = _tile(rows, 256)
    has_post, has_pre = z is not None, g_pre is not None
    nscratch = int(seg_z) + int(seg_y)

    def body(*refs):
        scratch = list(refs[len(refs) - nscratch:])
        it = iter(refs)
        hv = next(it)[...]
        if has_post:
            z_ref = next(it)
            zv = _from_segments(z_ref, scratch.pop(0)) if seg_z else z_ref[...].astype(F32)
            gp = next(it)[...]
        if has_pre:
            gq = next(it)[...]
        if has_post:
            r = lax.rsqrt(jnp.mean(zv * zv, axis=-1, keepdims=True) + RMS_EPS)
            hv = hv + (zv * r) * gp
            next(it)[...] = hv
        if has_pre:
            r = lax.rsqrt(jnp.mean(hv * hv, axis=-1, keepdims=True) + RMS_EPS)
            y = (hv * r) * gq
            if seg_y:
                _to_segments(y, scratch.pop(0), next(it))
            else:
                next(it)[...] = y.astype(BF16)

    row = pl.BlockSpec((tm, d), lambda i: (i, 0))
    seg = pl.BlockSpec((SEGS, tm // SEGS, d), lambda i: (0, i, 0))
    vec = pl.BlockSpec((1, d), lambda i: (0, 0))
    ins, in_specs, out_shape, out_specs = [h], [row], [], []
    if has_post:
        ins += [_seg_view(z) if seg_z else z, g_post]
        in_specs += [seg if seg_z else row, vec]
        out_shape.append(jax.ShapeDtypeStruct((rows, d), F32))
        out_specs.append(row)
    if has_pre:
        ins.append(g_pre)
        in_specs.append(vec)
        out_shape.append(jax.ShapeDtypeStruct((SEGS, rows // SEGS, d) if seg_y else (rows, d), BF16))
        out_specs.append(seg if seg_y else row)
    outs = pl.pallas_call(body, name=name, grid=(rows // tm,), in_specs=in_specs, out_specs=out_specs,
                          out_shape=out_shape, scratch_shapes=[_seg_scratch(tm, d)] * nscratch,
                          compiler_params=_cp(("parallel",)))(*ins)
    if seg_y:
        outs = list(outs[:-1]) + [outs[-1].reshape(rows, d)]
    return outs


def _rms_bwd_rows(x, g, dy):
    r = lax.rsqrt(jnp.mean(x * x, axis=-1, keepdims=True) + RMS_EPS)
    xn = x * r
    dg = jnp.sum(dy * xn, axis=0, keepdims=True)
    dxn = dy * g
    dx = r * (dxn - xn * jnp.mean(dxn * xn, axis=-1, keepdims=True))
    return dx, dg


def _norm_bwd(name, d_out, pre=None, post=None, seg_dy=False, seg_z=False):
    rows, d = d_out.shape
    tm = _tile(rows, 256)
    has_pre, has_post = pre is not None, post is not None
    nscratch = int(seg_dy) + 2 * int(seg_z)

    def body(*refs):
        scratch = list(refs[len(refs) - nscratch:])
        it = iter(refs)
        dres = next(it)[...]
        if has_pre:
            dy_ref = next(it)
            dy = _from_segments(dy_ref, scratch.pop(0)) if seg_dy else dy_ref[...].astype(F32)
            xp, gq = next(it)[...], next(it)[...]
        if has_post:
            z_ref = next(it)
            zv = _from_segments(z_ref, scratch.pop(0)) if seg_z else z_ref[...].astype(F32)
            gp = next(it)[...]
        first = pl.program_id(0) == 0
        if has_pre:
            dx, dg = _rms_bwd_rows(xp, gq, dy)
            dres = dres + dx
            next(it)[...] = dres
            dg_ref = next(it)

            @pl.when(first)
            def _():
                dg_ref[...] = jnp.zeros_like(dg_ref)
            dg_ref[...] += dg
        if has_post:
            dz, dg2 = _rms_bwd_rows(zv, gp, dres)
            if seg_z:
                _to_segments(dz, scratch.pop(0), next(it))
            else:
                next(it)[...] = dz.astype(BF16)
            dg2_ref = next(it)

            @pl.when(first)
            def _():
                dg2_ref[...] = jnp.zeros_like(dg2_ref)
            dg2_ref[...] += dg2

    row = pl.BlockSpec((tm, d), lambda i: (i, 0))
    seg = pl.BlockSpec((SEGS, tm // SEGS, d), lambda i: (0, i, 0))
    vec = pl.BlockSpec((1, d), lambda i: (0, 0))
    ins, in_specs, out_shape, out_specs = [d_out], [row], [], []
    if has_pre:
        d_y, x_pre, g_pre = pre
        ins += [_seg_view(d_y) if seg_dy else d_y, x_pre, g_pre]
        in_specs += [seg if seg_dy else row, row, vec]
        out_shape += [jax.ShapeDtypeStruct((rows, d), F32), jax.ShapeDtypeStruct((1, d), F32)]
        out_specs += [row, vec]
    if has_post:
        z, g_post = post
        ins += [_seg_view(z) if seg_z else z, g_post]
        in_specs += [seg if seg_z else row, vec]
        out_shape += [jax.ShapeDtypeStruct((SEGS, rows // SEGS, d) if seg_z else (rows, d), BF16),
                      jax.ShapeDtypeStruct((1, d), F32)]
        out_specs += [seg if seg_z else row, vec]
    outs = pl.pallas_call(body, name=name, grid=(rows // tm,), in_specs=in_specs, out_specs=out_specs,
                          out_shape=out_shape, scratch_shapes=[_seg_scratch(tm, d)] * nscratch,
                          compiler_params=_cp(("arbitrary",)))(*ins)
    if seg_z:
        outs = list(outs)
        outs[-2] = outs[-2].reshape(rows, d)
    return outs


def _loss_grad(name, y, target):
    rows, d = y.shape
    tm = _tile(rows, 256)

    def body(y_ref, t_ref, dy_ref, loss_ref):
        err = y_ref[...] - t_ref[...]
        dy_ref[...] = err * (1.0 / d)

        @pl.when(pl.program_id(0) == 0)
        def _():
            loss_ref[...] = jnp.zeros_like(loss_ref)
        loss_ref[...] += jnp.full(loss_ref.shape, (0.5 / d) * jnp.sum(err * err), F32)

    row = pl.BlockSpec((tm, d), lambda i: (i, 0))
    return pl.pallas_call(
        body, name=name, grid=(rows // tm,), in_specs=[row, row],
        out_specs=[row, pl.BlockSpec((1, 128), lambda i: (0, 0))],
        out_shape=[jax.ShapeDtypeStruct((rows, d), F32), jax.ShapeDtypeStruct((1, 128), F32)],
        compiler_params=_cp(("arbitrary",)))(y, target)


NT_DIMS = (((1,), (1,)), ((), ()))
TN_DIMS = (((0,), (0,)), ((), ()))


def _mm_nn_blk(name, a, wblk, relu2=False, after=()):
    m, k = a.shape
    nb = wblk.shape[2]
    tm = _tile(m, 1024)
    after = list(after)

    def body(a_ref, w_ref, *rest):
        r = jnp.dot(a_ref[...], w_ref[...], preferred_element_type=F32)
        if relu2:
            rr = jnp.maximum(r, 0.0)
            r = rr * rr
        rest[-1][...] = r.astype(BF16)

    return pl.pallas_call(
        body, name=name, grid=(NDEV, m // tm),
        in_specs=[pl.BlockSpec((tm, k), lambda d, i: (i, 0)), pl.BlockSpec((None, k, nb), lambda d, i: (d, 0, 0))]
        + [ANY] * len(after),
        out_specs=pl.BlockSpec((tm, nb), lambda d, i: (i, d)),
        out_shape=jax.ShapeDtypeStruct((m, NDEV * nb), BF16),
        compiler_params=_cp(("parallel", "parallel"), VMEM_MM))(a, wblk, *after)


def _accumulate(acc, o_ref, r, step, last):
    if acc is None:
        o_ref[...] = r.astype(o_ref.dtype)
        return

    @pl.when(step == 0)
    def _():
        acc[...] = r

    @pl.when(jnp.logical_and(step > 0, step < last))
    def _():
        acc[...] += r

    @pl.when(jnp.logical_and(step > 0, step == last))
    def _():
        o_ref[...] = (acc[...] + r).astype(o_ref.dtype)


def _mm_nn(name, a, w, after=()):
    m, kb = a.shape
    n = w.shape[1]
    one_step = kb <= 2048
    tm = _tile(m, 512 if one_step else 1024)
    tk = kb if one_step else _tile(kb, 4096)
    tn = n if one_step else _tile(n, 1024)
    steps = kb // tk
    after = list(after)

    def body(a_ref, w_ref, *rest):
        o_ref, scratch = rest[len(after)], rest[len(after) + 1:]
        r = jnp.dot(a_ref[...], w_ref[...], preferred_element_type=F32)
        _accumulate(scratch[0] if scratch else None, o_ref, r, pl.program_id(2), steps - 1)

    return pl.pallas_call(
        body, name=name, grid=(m // tm, n // tn, steps),
        in_specs=[pl.BlockSpec((tm, tk), lambda i, j, s: (i, s)), pl.BlockSpec((tk, tn), lambda i, j, s: (s, j))]
        + [ANY] * len(after),
        out_specs=pl.BlockSpec((tm, tn), lambda i, j, s: (i, j)),
        out_shape=jax.ShapeDtypeStruct((m, n), BF16),
        scratch_shapes=[pltpu.VMEM((tm, tn), F32)] if steps > 1 else [],
        compiler_params=_cp(("parallel", "parallel", "arbitrary"), VMEM_MM))(a, w, *after)


def _mm_nt_rows(name, dy, w, act=None, out_dtype=BF16):
    m, n = dy.shape
    kw = w.shape[0]
    tm, tkw = _tile(m, 1024), _tile(kw, 1024)

    def body(dy_ref, w_ref, *rest):
        r = lax.dot_general(dy_ref[...], w_ref[...], NT_DIMS, preferred_element_type=F32)
        if act is None:
            rest[0][...] = r.astype(out_dtype)
        else:
            rest[1][...] = (r * (2.0 * jnp.sqrt(rest[0][...].astype(F32)))).astype(BF16)

    ins = [dy, w]
    in_specs = [pl.BlockSpec((tm, n), lambda j, i: (i, 0)), pl.BlockSpec((tkw, n), lambda j, i: (j, 0))]
    if act is not None:
        ins.append(act)
        in_specs.append(pl.BlockSpec((tm, tkw), lambda j, i: (i, j)))
    return pl.pallas_call(
        body, name=name, grid=(kw // tkw, m // tm), in_specs=in_specs,
        out_specs=pl.BlockSpec((tm, tkw), lambda j, i: (i, j)),
        out_shape=jax.ShapeDtypeStruct((m, kw), out_dtype if act is None else BF16),
        compiler_params=_cp(("parallel", "parallel"), VMEM_MM))(*ins)


def _mm_nt_blk(name, dy, wblk, after=None):
    m = dy.shape[0]
    _, kw, nb = wblk.shape
    tm, tkw, per = _tile(m, 1024), _tile(kw, 1024), 4

    extra = list(after or ())

    def body(dy_ref, w_ref, *rest):
        o_ref, acc = rest[len(extra):]
        r = lax.dot_general(dy_ref[:, :nb], w_ref[0], NT_DIMS, preferred_element_type=F32)
        for t in range(1, per):
            r = r + lax.dot_general(dy_ref[:, t * nb:(t + 1) * nb], w_ref[t], NT_DIMS, preferred_element_type=F32)
        _accumulate(acc, o_ref, r, pl.program_id(2), NDEV // per - 1)

    return pl.pallas_call(
        body, name=name, grid=(m // tm, kw // tkw, NDEV // per),
        in_specs=[pl.BlockSpec((tm, per * nb), lambda i, j, s: (i, s)),
                  pl.BlockSpec((per, tkw, nb), lambda i, j, s: (s, j, 0))] + [ANY] * len(extra),
        out_specs=pl.BlockSpec((tm, tkw), lambda i, j, s: (i, j)),
        out_shape=jax.ShapeDtypeStruct((m, kw), BF16),
        scratch_shapes=[pltpu.VMEM((tm, tkw), F32)],
        compiler_params=_cp(("parallel", "parallel", "arbitrary"), VMEM_MM))(dy, wblk, *extra)


def _mm_tn(name, x, dy, nb=None):
    t, mx = x.shape
    n = dy.shape[1]
    tmx = _tile(mx, 512)
    tn = nb if nb is not None else _tile(n, 1024)

    def body(x_ref, dy_ref, o_ref):
        o_ref[...] = lax.dot_general(x_ref[...], dy_ref[...], TN_DIMS, preferred_element_type=F32).astype(BF16)

    if nb is None:
        out_shape = jax.ShapeDtypeStruct((mx, n), BF16)
        out_spec = pl.BlockSpec((tmx, tn), lambda j, i: (i, j))
    else:
        out_shape = jax.ShapeDtypeStruct((NDEV, mx, nb), BF16)
        out_spec = pl.BlockSpec((None, tmx, nb), lambda j, i: (j, i, 0))
    return pl.pallas_call(
        body, name=name, grid=(n // tn, mx // tmx),
        in_specs=[pl.BlockSpec((t, tmx), lambda j, i: (0, i)), pl.BlockSpec((t, tn), lambda j, i: (0, j))],
        out_specs=out_spec, out_shape=out_shape,
        compiler_params=_cp(("parallel", "parallel"), VMEM_MM))(x, dy)


def _gelu(x):
    return 0.5 * x * (1.0 + jnp.tanh(GELU_C * (x + GELU_A * (x * x * x))))


def _gelu_grad(x):
    t = jnp.tanh(GELU_C * (x + GELU_A * (x * x * x)))
    return 0.5 * (1.0 + t) + 0.5 * x * (1.0 - t * t) * (GELU_C * (1.0 + 3.0 * GELU_A * (x * x)))


def _layernorm(a):
    mu = jnp.mean(a, axis=-1, keepdims=True)
    ac = a - mu
    rstd = lax.rsqrt(jnp.mean(ac * ac, axis=-1, keepdims=True) + LN_EPS)
    return ac * rstd, rstd


def _shift_rows(z, halo, k):
    zr = pltpu.roll(z, k, 0)
    hr = pltpu.roll(halo, k, 0)
    row = lax.broadcasted_iota(jnp.int32, hr.shape, 0)
    top = jnp.where(row < k, hr, zr[:HALO])
    return jnp.concatenate([top, zr[HALO:]], axis=0)


def _shift_rows_up(z, halo, k):
    rows = z.shape[0]
    zr = pltpu.roll(z, rows - k, 0)
    hr = pltpu.roll(halo, HALO - k, 0)
    row = lax.broadcasted_iota(jnp.int32, hr.shape, 0)
    bot = jnp.where(row >= HALO - k, hr, zr[rows - HALO:])
    return jnp.concatenate([zr[:rows - HALO], bot], axis=0)


def _causal_mask():
    t = lax.broadcasted_iota(jnp.int32, (CHUNK, CHUNK), 0)
    s = lax.broadcasted_iota(jnp.int32, (CHUNK, CHUNK), 1)
    return s <= t


def _gate_specs(tm, rows, width):
    per = tm // HALO_BF16
    last = rows // HALO_BF16 - 1
    cur = pl.BlockSpec((tm, width), lambda i: (i, 0))
    prev = pl.BlockSpec((HALO_BF16, width), lambda i: (jnp.maximum(i * per - 1, 0), 0))
    nxt = pl.BlockSpec((HALO_BF16, width), lambda i: (jnp.minimum((i + 1) * per, last), 0))
    return cur, prev, nxt


def _cols(ref, lo, hi):
    return ref[:, lo:hi].astype(F32)


def _halo_before(ref, lo, hi):
    return ref[:, lo:hi].astype(F32)[HALO_BF16 - HALO:]


def _halo_after(ref, lo, hi):
    return ref[:, lo:hi].astype(F32)[:HALO]


def _gate_fwd(name, proj, w_s, b_st, cw):
    rows, width = proj.shape
    w = width // 5
    groups = w // CHUNK
    tm = _tile(rows, 256)
    cur, prev, _ = _gate_specs(tm, rows, width)

    def body(p_ref, h_ref, ws_ref, b_ref, cw_ref, o_ref):
        mask = _causal_mask()
        au = _gelu(_cols(p_ref, 0, w))
        vn, _ = _layernorm(_gelu(_cols(p_ref, w, 2 * w)))
        vn = vn.astype(BF16)
        for g in range(groups):
            wc = jnp.where(mask, ws_ref[g], 0.0).astype(BF16)
            cols = slice(g * CHUNK, (g + 1) * CHUNK)
            for ch in range(tm // CHUNK):
                rws = slice(ch * CHUNK, (ch + 1) * CHUNK)
                mixed = jnp.dot(wc, vn[rws, cols], preferred_element_type=F32) + b_ref[:, g:g + 1]
                o_ref[rws, cols] = (au[rws, cols] * mixed).astype(BF16)
        z = _cols(p_ref, 3 * w, 4 * w) * _cols(p_ref, 4 * w, 5 * w)
        zh = _halo_before(h_ref, 3 * w, 4 * w) * _halo_before(h_ref, 4 * w, 5 * w)
        zh = jnp.where(pl.program_id(0) == 0, 0.0, zh)
        y = cw_ref[0:1, :] * _shift_rows(z, zh, 2) + cw_ref[1:2, :] * _shift_rows(z, zh, 1) + cw_ref[2:3, :] * z
        o_ref[:, w:2 * w] = (_cols(p_ref, 2 * w, 3 * w) * y).astype(BF16)

    full = lambda a: pl.BlockSpec(a.shape, lambda i: (0,) * a.ndim)
    return pl.pallas_call(
        body, name=name, grid=(rows // tm,),
        in_specs=[cur, prev, full(w_s), full(b_st), full(cw)],
        out_specs=pl.BlockSpec((tm, 2 * w), lambda i: (i, 0)),
        out_shape=jax.ShapeDtypeStruct((rows, 2 * w), BF16),
        compiler_params=_cp(("parallel",)))(proj, proj, w_s, b_st, cw)


def _gate_bwd(name, proj, d_ab, w_s, b_st, cw):
    rows, width = proj.shape
    w = width // 5
    groups = w // CHUNK
    tm = _tile(rows, 256)
    cur, prev, nxt = _gate_specs(tm, rows, width)
    dcur, _, dnxt = _gate_specs(tm, rows, 2 * w)

    def body(p_ref, ph_ref, pn_ref, d_ref, dn_ref, ws_ref, b_ref, cw_ref, o_ref, dws_ref, dbs_ref, dcw_ref):
        i = pl.program_id(0)

        @pl.when(i == 0)
        def _():
            dws_ref[...] = jnp.zeros_like(dws_ref)
            dbs_ref[...] = jnp.zeros_like(dbs_ref)
            dcw_ref[...] = jnp.zeros_like(dcw_ref)

        mask = _causal_mask()
        u, v = _cols(p_ref, 0, w), _cols(p_ref, w, 2 * w)
        au, av = _gelu(u), _gelu(v)
        vn, rstd = _layernorm(av)
        vnb = vn.astype(BF16)
        d_a = _cols(d_ref, 0, w)
        d_mixed = (d_a * au).astype(BF16)
        ones = jnp.ones((HALO, CHUNK), BF16)
        d_vn_cols = []
        d_au_cols = []
        for g in range(groups):
            wc = jnp.where(mask, ws_ref[g], 0.0).astype(BF16)
            cols = slice(g * CHUNK, (g + 1) * CHUNK)
            dw = jnp.zeros((CHUNK, CHUNK), F32)
            db = jnp.zeros((HALO, CHUNK), F32)
            d_vn_rows, d_au_rows = [], []
            for ch in range(tm // CHUNK):
                rws = slice(ch * CHUNK, (ch + 1) * CHUNK)
                mixed = jnp.dot(wc, vnb[rws, cols], preferred_element_type=F32) + b_ref[:, g:g + 1]
                d_au_rows.append(d_a[rws, cols] * mixed)
                dm = d_mixed[rws, cols]
                dw = dw + lax.dot_general(dm, vnb[rws, cols], NT_DIMS, preferred_element_type=F32)
                db = db + lax.dot_general(ones, dm, NT_DIMS, preferred_element_type=F32)
                d_vn_rows.append(lax.dot_general(wc, dm, TN_DIMS, preferred_element_type=F32))
            dws_ref[g] += jnp.where(mask, dw, 0.0)
            dbs_ref[g:g + 1, :] += db[0:1, :]
            d_vn_cols.append(jnp.concatenate(d_vn_rows, axis=0))
            d_au_cols.append(jnp.concatenate(d_au_rows, axis=0))
        d_vn = jnp.concatenate(d_vn_cols, axis=1)
        d_au = jnp.concatenate(d_au_cols, axis=1)
        d_av = rstd * (d_vn - jnp.mean(d_vn, axis=-1, keepdims=True)
                       - vn * jnp.mean(d_vn * vn, axis=-1, keepdims=True))
        o_ref[:, 0:w] = (d_au * _gelu_grad(u)).astype(BF16)
        o_ref[:, w:2 * w] = (d_av * _gelu_grad(v)).astype(BF16)

        gb, gc, bx = _cols(p_ref, 2 * w, 3 * w), _cols(p_ref, 3 * w, 4 * w), _cols(p_ref, 4 * w, 5 * w)
        z = gc * bx
        zh = jnp.where(i == 0, 0.0, _halo_before(ph_ref, 3 * w, 4 * w) * _halo_before(ph_ref, 4 * w, 5 * w))
        z1, z2 = _shift_rows(z, zh, 1), _shift_rows(z, zh, 2)
        d_b = _cols(d_ref, w, 2 * w)
        y = cw_ref[0:1, :] * z2 + cw_ref[1:2, :] * z1 + cw_ref[2:3, :] * z
        dy = d_b * gb
        dyn = jnp.where(i == pl.num_programs(0) - 1, 0.0,
                        _halo_after(dn_ref, w, 2 * w) * _halo_after(pn_ref, 2 * w, 3 * w))
        dz = (cw_ref[2:3, :] * dy + cw_ref[1:2, :] * _shift_rows_up(dy, dyn, 1)
              + cw_ref[0:1, :] * _shift_rows_up(dy, dyn, 2))
        dcw_ref[0:1, :] += jnp.sum(dy * z2, axis=0, keepdims=True)
        dcw_ref[1:2, :] += jnp.sum(dy * z1, axis=0, keepdims=True)
        dcw_ref[2:3, :] += jnp.sum(dy * z, axis=0, keepdims=True)
        o_ref[:, 2 * w:3 * w] = (d_b * y).astype(BF16)
        o_ref[:, 3 * w:4 * w] = (dz * bx).astype(BF16)
        o_ref[:, 4 * w:5 * w] = (dz * gc).astype(BF16)

    full = lambda a: pl.BlockSpec(a.shape, lambda i: (0,) * a.ndim)
    acc = lambda shape: pl.BlockSpec(shape, lambda i: (0,) * len(shape))
    return pl.pallas_call(
        body, name=name, grid=(rows // tm,),
        in_specs=[cur, prev, nxt, dcur, dnxt, full(w_s), full(b_st), full(cw)],
        out_specs=[pl.BlockSpec((tm, width), lambda i: (i, 0)), acc((groups, CHUNK, CHUNK)),
                   acc((groups, CHUNK)), acc((HALO, w))],
        out_shape=[jax.ShapeDtypeStruct((rows, width), BF16), jax.ShapeDtypeStruct((groups, CHUNK, CHUNK), F32),
                   jax.ShapeDtypeStruct((groups, CHUNK), F32), jax.ShapeDtypeStruct((HALO, w), F32)],
        compiler_params=_cp(("arbitrary",), VMEM_MM))(proj, proj, proj, d_ab, d_ab, w_s, b_st, cw)


def _flat(x):
    return x.reshape(-1, x.shape[-1])


def _rope(t, cosf, sins):
    t2 = _flat(t)
    return (t2 * _flat(cosf) + pltpu.roll(t2, HEAD // 2, 1) * _flat(sins)).reshape(t.shape)


def _rope_bwd(dt, cosf, sins):
    d2 = _flat(dt)
    return (d2 * _flat(cosf) + pltpu.roll(d2 * _flat(sins), HEAD // 2, 1)).reshape(dt.shape)


ATT_UNITS = ATT_TILE // CHUNK


def _attn_units(phases):
    for b, d in enumerate(DILATIONS):
        blocks = ATT_TILE // (CHUNK * d)
        for visit in phases:
            for r in range(d):
                if blocks <= ATT_UNROLL:
                    for j in range(blocks):
                        visit(b, d, r, j, r * blocks + j)
                else:
                    def step(jj, carry, b=b, d=d, r=r, visit=visit, blocks=blocks):
                        for u in range(ATT_UNROLL):
                            j = jj * ATT_UNROLL + u
                            visit(b, d, r, j, r * blocks + j)
                        return carry
                    lax.fori_loop(0, blocks // ATT_UNROLL, step, 0)


class _Unit:
    def __init__(self, d, r, j):
        self.segs = [r + d * k for k in range(SEGS // d)]
        self.w = CHUNK * d // SEGS
        q0 = j * self.w
        self.q0 = q0 if isinstance(q0, int) else pl.multiple_of(q0, HALO)
        k0 = CHUNK + (j - 1) * self.w
        self.k0 = k0 if isinstance(k0, int) else pl.multiple_of(k0, HALO)

    def queries(self, ref):
        return _chunks(ref, self.segs, self.q0, self.w)

    def keys(self, ref):
        return _chunks(ref, self.segs, self.k0, 2 * self.w)

    def put_queries(self, ref, val, add=False):
        _put_chunks(ref, self.segs, self.q0, self.w, val, add)

    def put_keys(self, ref, val, add=False):
        _put_chunks(ref, self.segs, self.k0, 2 * self.w, val, add)


def _chunks(ref, segs, start, size):
    parts = [ref[s, pl.ds(start, size), :] for s in segs]
    return parts[0] if len(parts) == 1 else jnp.concatenate(parts, axis=0)


def _put_chunks(ref, segs, start, size, val, add):
    for k, s in enumerate(segs):
        piece = val[k * size:(k + 1) * size]
        if add:
            ref[s, pl.ds(start, size), :] += piece
        else:
            ref[s, pl.ds(start, size), :] = piece


def _band_bias():
    qi = lax.broadcasted_iota(jnp.int32, (CHUNK, 2 * CHUNK), 0)
    ki = lax.broadcasted_iota(jnp.int32, (CHUNK, 2 * CHUNK), 1)
    tables = []
    for d in DILATIONS:
        nseg, w = SEGS // d, CHUNK * d // SEGS
        pos_q = nseg * (qi % w) + qi // w
        pos_k = nseg * (ki % (2 * w) - w) + ki // (2 * w)
        band = (pos_q >= pos_k) & (pos_q - pos_k <= CHUNK)
        tables += [jnp.where(band, 0.0, -jnp.inf), jnp.where(band & (pos_k >= 0), 0.0, -jnp.inf)]
    return jnp.stack(tables).astype(F32)


def _bias_spec():
    return pl.BlockSpec((2 * len(DILATIONS), CHUNK, 2 * CHUNK), lambda h, n: (0, 0, 0))


def _unit_bias(bias, b, n, j):
    if isinstance(j, int) and j != 0:
        return bias[2 * b]
    return bias[2 * b + jnp.where(jnp.logical_and(n == 0, j == 0), 1, 0)]


def _attn_in_specs(heads):
    blk = (SEGS, CHUNK, HEAD)
    prev = lambda n: jnp.maximum(n - 1, 0)
    return [
        pl.BlockSpec(blk, lambda h, n: (0, n, h)),
        pl.BlockSpec(blk, lambda h, n: (0, n, heads + h)),
        pl.BlockSpec(blk, lambda h, n: (0, prev(n), heads + h)),
        pl.BlockSpec(blk, lambda h, n: (0, n, 2 * heads + h)),
        pl.BlockSpec(blk, lambda h, n: (0, prev(n), 2 * heads + h)),
        pl.BlockSpec(blk, lambda h, n: (0, n, 0)),
        pl.BlockSpec(blk, lambda h, n: (0, n, 0)),
        pl.BlockSpec(blk, lambda h, n: (0, prev(n), 0)),
        pl.BlockSpec(blk, lambda h, n: (0, prev(n), 0)),
    ]


def _attn_load(q_ref, kc_ref, kp_ref, vc_ref, vp_ref, cc_ref, sc_ref, cp_ref, sp_ref, qr, kcat, vcat):
    qr[...] = _rope(q_ref[...].astype(F32), cc_ref[...], sc_ref[...]) * (HEAD ** -0.5)
    kcat[:, pl.ds(0, CHUNK), :] = _rope(kp_ref[...].astype(F32), cp_ref[...], sp_ref[...])
    kcat[:, pl.ds(CHUNK, CHUNK), :] = _rope(kc_ref[...].astype(F32), cc_ref[...], sc_ref[...])
    vcat[:, pl.ds(0, CHUNK), :] = vp_ref[...].astype(F32)
    vcat[:, pl.ds(CHUNK, CHUNK), :] = vc_ref[...].astype(F32)


def _attn_fwd(name, qkv, cosf, sins, bias):
    t = qkv.shape[0]
    heads = qkv.shape[1] // (3 * HEAD)
    nbr = len(DILATIONS)

    def body(q_ref, kc_ref, kp_ref, vc_ref, vp_ref, cc_ref, sc_ref, cp_ref, sp_ref, bias, o_ref, lse_ref,
             qr, kcat, vcat, obr, mbr, dbr, pn):
        n = pl.program_id(1)
        _attn_load(q_ref, kc_ref, kp_ref, vc_ref, vp_ref, cc_ref, sc_ref, cp_ref, sp_ref, qr, kcat, vcat)

        def probs(b, d, r, j, u):
            unit = _Unit(d, r, j)
            s = lax.dot_general(unit.queries(qr).astype(BF16), unit.keys(kcat).astype(BF16), NT_DIMS,
                                preferred_element_type=F32) + _unit_bias(bias, b, n, j)
            mx = jnp.max(s, axis=-1, keepdims=True)
            p = jnp.exp(s - mx)
            pn[u] = p.astype(BF16)
            unit.put_queries(mbr.at[b], jnp.broadcast_to(mx, (CHUNK, HEAD)))
            unit.put_queries(dbr.at[b], jnp.broadcast_to(jnp.sum(p, axis=-1, keepdims=True), (CHUNK, HEAD)))

        def values(b, d, r, j, u):
            unit = _Unit(d, r, j)
            unit.put_queries(obr.at[b], jnp.dot(pn[u], unit.keys(vcat).astype(BF16), preferred_element_type=F32))

        _attn_units([probs, values])
        ms = [mbr[b] for b in range(nbr)]
        top = functools.reduce(jnp.maximum, ms)
        ws = [jnp.exp(m - top) for m in ms]
        tot = functools.reduce(jnp.add, [ws[b] * dbr[b] for b in range(nbr)])
        inv = 1.0 / tot
        o = (ws[0] * inv) * obr[0]
        for b in range(1, nbr):
            o = o + (ws[b] * inv) * obr[b]
        o_ref[...] = o.astype(BF16)
        lse_ref[...] = top + jnp.log(tot)

    blk = (SEGS, CHUNK, HEAD)
    keys = pltpu.VMEM((SEGS, 2 * CHUNK, HEAD), F32)
    tile = pl.BlockSpec(blk, lambda h, n: (0, n, h))
    seg = t // SEGS
    qkv3, cos3, sin3 = _seg_view(qkv), _seg_view(cosf), _seg_view(sins)
    o, lse = pl.pallas_call(
        body, name=name, grid=(heads, t // ATT_TILE), in_specs=_attn_in_specs(heads) + [_bias_spec()],
        out_specs=[tile, tile],
        out_shape=[jax.ShapeDtypeStruct((SEGS, seg, heads * HEAD), BF16),
                   jax.ShapeDtypeStruct((SEGS, seg, heads * HEAD), F32)],
        scratch_shapes=[pltpu.VMEM(blk, F32), keys, keys,
                        pltpu.VMEM((nbr,) + blk, F32), pltpu.VMEM((nbr,) + blk, F32), pltpu.VMEM((nbr,) + blk, F32),
                        pltpu.VMEM((ATT_UNITS, CHUNK, 2 * CHUNK), BF16)],
        compiler_params=_cp(("parallel", "parallel"), VMEM_MM),
    )(qkv3, qkv3, qkv3, qkv3, qkv3, cos3, sin3, cos3, sin3, bias)
    return o.reshape(t, heads * HEAD), lse.reshape(t, heads * HEAD)


def _attn_bwd(name, qkv, cosf, sins, bias, d_o, o, lse):
    t = qkv.shape[0]
    heads = qkv.shape[1] // (3 * HEAD)
    scale = HEAD ** -0.5

    def body(q_ref, kc_ref, kp_ref, vc_ref, vp_ref, cc_ref, sc_ref, cp_ref, sp_ref, do_ref, o_ref, lse_ref, bias,
             dq_ref, dko_ref, dkp_ref, dvo_ref, dvp_ref, qr, kcat, vcat, dq_acc, dk_acc, dv_acc, delta, ps, dss):
        n = pl.program_id(1)
        _attn_load(q_ref, kc_ref, kp_ref, vc_ref, vp_ref, cc_ref, sc_ref, cp_ref, sp_ref, qr, kcat, vcat)
        dq_acc[...] = jnp.zeros_like(dq_acc)
        dk_acc[...] = jnp.zeros_like(dk_acc)
        dv_acc[...] = jnp.zeros_like(dv_acc)
        delta[...] = jnp.broadcast_to(
            jnp.sum(do_ref[...] * o_ref[...].astype(F32), axis=-1, keepdims=True), delta.shape)

        def probs(b, d, r, j, u):
            unit = _Unit(d, r, j)
            s = lax.dot_general(unit.queries(qr).astype(BF16), unit.keys(kcat).astype(BF16), NT_DIMS,
                                preferred_element_type=F32) + _unit_bias(bias, b, n, j)
            ps[u] = jnp.exp(s - unit.queries(lse_ref)[:, 0:1]).astype(BF16)

        def score_grads(b, d, r, j, u):
            unit = _Unit(d, r, j)
            dp = lax.dot_general(unit.queries(do_ref).astype(BF16), unit.keys(vcat).astype(BF16), NT_DIMS,
                                 preferred_element_type=F32)
            dss[u] = (ps[u].astype(F32) * (dp - unit.queries(delta)[:, 0:1])).astype(BF16)

        def input_grads(b, d, r, j, u):
            unit = _Unit(d, r, j)
            ds = dss[u]
            unit.put_queries(dq_acc, jnp.dot(ds, unit.keys(kcat).astype(BF16), preferred_element_type=F32), add=True)
            unit.put_keys(dk_acc, lax.dot_general(ds, unit.queries(qr).astype(BF16), TN_DIMS,
                                                  preferred_element_type=F32), add=True)
            unit.put_keys(dv_acc, lax.dot_general(ps[u], unit.queries(do_ref).astype(BF16), TN_DIMS,
                                                  preferred_element_type=F32), add=True)

        _attn_units([probs, score_grads, input_grads])
        dq_ref[...] = _rope_bwd(dq_acc[...] * scale, cc_ref[...], sc_ref[...]).astype(BF16)
        dkp_ref[...] = _rope_bwd(dk_acc[:, pl.ds(0, CHUNK), :], cp_ref[...], sp_ref[...]).astype(BF16)
        dko_ref[...] = _rope_bwd(dk_acc[:, pl.ds(CHUNK, CHUNK), :], cc_ref[...], sc_ref[...]).astype(BF16)
        dvp_ref[...] = dv_acc[:, pl.ds(0, CHUNK), :].astype(BF16)
        dvo_ref[...] = dv_acc[:, pl.ds(CHUNK, CHUNK), :].astype(BF16)

    blk = (SEGS, CHUNK, HEAD)
    tile = pl.BlockSpec(blk, lambda h, n: (0, n, h))
    big = pltpu.VMEM((SEGS, 2 * CHUNK, HEAD), F32)
    seg = t // SEGS
    qkv3, cos3, sin3 = _seg_view(qkv), _seg_view(cosf), _seg_view(sins)
    return pl.pallas_call(
        body, name=name, grid=(heads, t // ATT_TILE),
        in_specs=_attn_in_specs(heads) + [tile, tile, tile, _bias_spec()],
        out_specs=[tile] * 5,
        out_shape=[jax.ShapeDtypeStruct((SEGS, seg, heads * HEAD), BF16)] * 5,
        scratch_shapes=[pltpu.VMEM(blk, F32), big, big, pltpu.VMEM(blk, F32), big, big, pltpu.VMEM(blk, F32),
                        pltpu.VMEM((ATT_UNITS, CHUNK, 2 * CHUNK), BF16), pltpu.VMEM((ATT_UNITS, CHUNK, 2 * CHUNK), BF16)],
        compiler_params=_cp(("parallel", "parallel"), 60 << 20),
    )(qkv3, qkv3, qkv3, qkv3, qkv3, cos3, sin3, cos3, sin3, _seg_view(d_o), _seg_view(o), _seg_view(lse), bias)


def _attn_merge(name, dq, dk_own, dk_prev, dv_own, dv_prev):
    _, seg, hd = dq.shape
    nt = seg // CHUNK
    tw = _tile(hd, 512)

    def body(dq_ref, dko_ref, dkn_ref, dvo_ref, dvn_ref, o_ref):
        last = pl.program_id(0) == nt - 1
        part = pl.program_id(1)

        @pl.when(part == 0)
        def _():
            o_ref[...] = dq_ref[...]

        @pl.when(part == 1)
        def _():
            o_ref[...] = (dko_ref[...].astype(F32) + jnp.where(last, 0.0, dkn_ref[...].astype(F32))).astype(BF16)

        @pl.when(part == 2)
        def _():
            o_ref[...] = (dvo_ref[...].astype(F32) + jnp.where(last, 0.0, dvn_ref[...].astype(F32))).astype(BF16)

    blk = (SEGS, CHUNK, tw)

    def own(part):
        return pl.BlockSpec(blk, lambda n, p, c: (0, jnp.where(p == part, n, 0), jnp.where(p == part, c, 0)))

    def nxt(part):
        return pl.BlockSpec(blk, lambda n, p, c: (0, jnp.where(p == part, jnp.minimum(n + 1, nt - 1), 0),
                                                  jnp.where(p == part, c, 0)))

    per = hd // tw
    return pl.pallas_call(
        body, name=name, grid=(nt, 3, per), in_specs=[own(0), own(1), nxt(1), own(2), nxt(2)],
        out_specs=pl.BlockSpec(blk, lambda n, p, c: (0, n, p * per + c)),
        out_shape=jax.ShapeDtypeStruct((SEGS, seg, 3 * hd), BF16),
        compiler_params=_cp(("parallel", "parallel", "parallel"), VMEM_MM),
    )(dq, dk_own, dk_prev, dv_own, dv_prev).reshape(SEGS * seg, 3 * hd)


def _sum_parts(name, parts):
    nparts, rows, cols = parts.shape
    tr = _tile(rows, 256)

    def body(p_ref, o_ref):
        s = p_ref[0]
        for k in range(1, nparts):
            s = s + p_ref[k]
        o_ref[...] = s

    return pl.pallas_call(
        body, name=name, grid=(rows // tr,),
        in_specs=[pl.BlockSpec((nparts, tr, cols), lambda i: (0, i, 0))],
        out_specs=pl.BlockSpec((tr, cols), lambda i: (i, 0)),
        out_shape=jax.ShapeDtypeStruct((rows, cols), F32),
        compiler_params=_cp(("parallel",)))(parts)


def _rows128(a, pad_to=8):
    flat = a.reshape(-1)
    rows = -(-flat.shape[0] // 128)
    rows = -(-rows // pad_to) * pad_to
    flat = jnp.pad(flat, (0, rows * 128 - flat.shape[0]))
    return flat.reshape(rows, 128)


def _pack(arrays):
    return jnp.concatenate([_rows128(a) for a in arrays], axis=0)


def _unpack(packed, like):
    out, at = [], 0
    for a in like:
        size = 1
        for s in a.shape:
            size *= s
        rows = -(-(-(-size // 128)) // 8) * 8
        out.append(packed[at:at + rows].reshape(-1)[:size].reshape(a.shape))
        at += rows
    return out


def kernel(x, norm_mix_pre, norm_mix_post, norm_mlp_pre, norm_mlp_post, w_in_ab, w_spatial, b_spatial, conv_w, w_out_ab, w_qkv, w_o, w_up, w_down, loss_target, m_norm_mix_pre, m_norm_mix_post, m_norm_mlp_pre, m_norm_mlp_post, m_w_in_ab, m_w_spatial, m_b_spatial, m_conv_w, m_w_out_ab, m_w_qkv, m_w_o, m_w_up, m_w_down, v_norm_mix_pre, v_norm_mix_post, v_norm_mlp_pre, v_norm_mlp_post, v_w_in_ab, v_w_spatial, v_b_spatial, v_conv_w, v_w_out_ab, v_w_qkv, v_w_o, v_w_up, v_w_down):
    depth = norm_mix_pre.shape[0]
    seq, dm = x.shape[1], x.shape[2]
    h0 = x.reshape(seq, dm)
    target = loss_target.reshape(seq, dm)
    ax, ay, ac = lax.axis_index("x"), lax.axis_index("y"), lax.axis_index("c")
    my_block = 4 * ax + 2 * ay + ac
    block = jnp.reshape(my_block, (1,)).astype(jnp.int32)

    half = HEAD // 2
    inv_freq = ROPE_THETA ** (-jnp.arange(half, dtype=F32) * 2.0 / HEAD)
    ang = jnp.arange(seq, dtype=jnp.int32).astype(F32)[:, None] * inv_freq[None, :]
    ang = ang.reshape(seq // SEGS, SEGS, half).transpose(1, 0, 2).reshape(seq, half)
    cosf = jnp.concatenate([jnp.cos(ang), jnp.cos(ang)], axis=-1)
    sins = jnp.concatenate([-jnp.sin(ang), jnp.sin(ang)], axis=-1)
    band_bias = _band_bias()

    big = {"w_in_ab": w_in_ab, "w_out_ab": w_out_ab, "w_qkv": w_qkv, "w_o": w_o, "w_up": w_up, "w_down": w_down}
    use_order = []
    for l in range(depth):
        use_order += [("w_in_ab", l // 2), ("w_out_ab", l // 2)] if l % 2 == 0 else [("w_qkv", l // 2), ("w_o", l // 2)]
        use_order += [("w_up", l), ("w_down", l)]
    n_even = w_in_ab.shape[0]
    cw_rows = jnp.pad(conv_w.reshape(n_even * CONV_TAPS, conv_w.shape[2]), ((0, HALO - (n_even * CONV_TAPS) % HALO), (0, 0)))
    cw_gathered = _all_gather("ag_conv", [cw_rows])[0]
    first = [k for k in use_order if k in (("w_in_ab", 0), ("w_out_ab", 0), ("w_up", 0), ("w_down", 0))]
    rest = [k for k in use_order if k not in first]
    lands_a, sems_a = _ag_start("ag_start_first", [_cast_fill(f"cast_{nm}_{l}", big[nm], l, block) for nm, l in first],
                                cw_gathered)
    lands_b, sems_b = _ag_start("ag_start_rest", [_cast_fill(f"cast_{nm}_{l}", big[nm], l, block) for nm, l in rest],
                                lands_a[0])
    lands = dict(zip(first + rest, list(lands_a) + list(lands_b)))
    ag_sems = dict(zip(first + rest, list(sems_a) + list(sems_b)))
    passed_on, wg = [], {}

    def weight(key, after):
        pins = []
        if key not in wg:
            upto = min(use_order.index(key) + 1, len(use_order) - 1)
            for k in use_order[len(passed_on):upto + 1]:
                lands[k] = _ag_mid(f"ag_mid_{k[0]}_{k[1]}", lands[k], ag_sems[k], after)
                passed_on.append(k)
                if k != key and use_order.index(k) >= AG_PIN_FROM:
                    pins.append(lands[k])
            wg[key] = _ag_wait(f"ag_wait_{key[0]}_{key[1]}", lands[key], ag_sems[key], after)
        return wg[key], pins

    cw_all = cw_gathered[:, :n_even * CONV_TAPS].reshape(NDEV, n_even, CONV_TAPS, -1)
    cw_all = jnp.transpose(cw_all, (1, 2, 0, 3)).reshape(n_even, CONV_TAPS, -1)
    cw_full = [jnp.pad(cw_all[e], ((0, HALO - CONV_TAPS), (0, 0))) for e in range(n_even)]

    def rows_nat(blk):
        return blk.reshape(blk.shape[0] * blk.shape[1], blk.shape[2])

    saved = []
    hn = _norm_fwd("norm_first", h0, g_pre=norm_mix_pre[0][None])[0]
    h = h0
    for l in range(depth):
        s = {"h_in": h, "hn1": hn}
        if l % 2 == 0:
            e = l // 2
            w_, pins = weight(("w_in_ab", e), hn)
            proj = _mm_nn_blk(f"fwd_in_{l}", hn, w_, after=pins)
            ab = _gate_fwd(f"gate_fwd_{l}", proj, w_spatial[e], b_spatial[e].T, cw_full[e])
            w_, pins = weight(("w_out_ab", e), ab)
            mix = _mm_nn(f"fwd_out_{l}", ab, rows_nat(w_), after=pins)
            s.update(proj=proj, ab=ab)
        else:
            o_ = l // 2
            w_, pins = weight(("w_qkv", o_), hn)
            qkv = _mm_nn_blk(f"fwd_qkv_{l}", hn, w_, after=pins)
            att, lse = _attn_fwd(f"attn_fwd_{l}", qkv, cosf, sins, band_bias)
            w_, pins = weight(("w_o", o_), att)
            mix = _mm_nn(f"fwd_o_{l}", att, rows_nat(w_), after=pins)
            s.update(qkv=qkv, att=att, lse=lse)
        h1, hn2 = _norm_fwd(f"norm_mid_{l}", h, mix, norm_mix_post[l][None], norm_mlp_pre[l][None], seg_z=l % 2 == 1)
        w_, pins = weight(("w_up", l), hn2)
        act = _mm_nn_blk(f"fwd_up_{l}", hn2, w_, relu2=True, after=pins)
        w_, pins = weight(("w_down", l), act)
        f = _mm_nn(f"fwd_down_{l}", act, rows_nat(w_), after=pins)
        s.update(mix=mix, h1=h1, hn2=hn2, act=act, f=f)
        if l + 1 < depth:
            h, hn = _norm_fwd(f"norm_end_{l}", h1, f, norm_mlp_post[l][None], norm_mix_pre[l + 1][None],
                              seg_y=(l + 1) % 2 == 1)
        else:
            h = _norm_fwd(f"norm_end_{l}", h1, f, norm_mlp_post[l][None])[0]
        saved.append(s)

    d_h, loss_row = _loss_grad("loss", h, target)
    rs = {}

    def scatter(key, g):
        rs[key] = _rs_start(f"rs_start_{key[0]}_{key[1]}", g.reshape(NDEV, -1, g.shape[-1]))

    dg ={nm: [None] * depth for nm in ("norm_mix_pre", "norm_mix_post", "norm_mlp_pre", "norm_mlp_post")}
    d_ws, d_bs, d_cw = [None] * n_even, [None] * n_even, [None] * n_even
    d_hn_next = None
    for l in reversed(range(depth)):
        s = saved[l]
        if l == depth - 1:
            d_f, dg["norm_mlp_post"][l] = _norm_bwd(f"nb_end_{l}", d_h, post=(s["f"], norm_mlp_post[l][None]))
        else:
            d_h, dg["norm_mix_pre"][l + 1], d_f, dg["norm_mlp_post"][l] = _norm_bwd(
                f"nb_end_{l}", d_h, pre=(d_hn_next, saved[l + 1]["h_in"], norm_mix_pre[l + 1][None]),
                post=(s["f"], norm_mlp_post[l][None]), seg_dy=(l + 1) % 2 == 1)
        wd = rows_nat(wg[("w_down", l)])
        d_up = _mm_nt_rows(f"bwd_down_{l}", d_f, wd, act=s["act"])
        scatter(("w_down", l), _mm_tn(f"gw_down_{l}", s["act"], d_f))
        scatter(("w_up", l), _mm_tn(f"gw_up_{l}", s["hn2"], d_up, nb=w_up.shape[2]))
        d_hn2 = _mm_nt_blk(f"bwd_up_{l}", d_up, wg[("w_up", l)], after=[rs[("w_down", l)][0], rs[("w_up", l)][0]])
        d_h, dg["norm_mlp_pre"][l], d_mix, dg["norm_mix_post"][l] = _norm_bwd(
            f"nb_mid_{l}", d_h, pre=(d_hn2, s["h1"], norm_mlp_pre[l][None]),
            post=(s["mix"], norm_mix_post[l][None]), seg_z=l % 2 == 1)
        if l % 2 == 0:
            e = l // 2
            wo = rows_nat(wg[("w_out_ab", e)])
            d_ab = _mm_nt_rows(f"bwd_out_{l}", d_mix, wo)
            scatter(("w_out_ab", e), _mm_tn(f"gw_out_{l}", s["ab"], d_mix))
            d_proj, d_ws[e], d_bs[e], d_cw[e] = _gate_bwd(
                f"gate_bwd_{l}", s["proj"], d_ab, w_spatial[e], b_spatial[e].T, cw_full[e])
            scatter(("w_in_ab", e), _mm_tn(f"gw_in_{l}", s["hn1"], d_proj, nb=w_in_ab.shape[2]))
            d_hn_next = _mm_nt_blk(f"bwd_in_{l}", d_proj, wg[("w_in_ab", e)],
                                   after=[rs[("w_out_ab", e)][0], rs[("w_in_ab", e)][0]])
        else:
            o_ = l // 2
            wo = rows_nat(wg[("w_o", o_)])
            d_att = _mm_nt_rows(f"bwd_o_{l}", d_mix, wo, out_dtype=F32)
            scatter(("w_o", o_), _mm_tn(f"gw_o_{l}", s["att"], d_mix))
            parts = _attn_bwd(f"attn_bwd_{l}", s["qkv"], cosf, sins, band_bias, d_att, s["att"], s["lse"])
            d_qkv = _attn_merge(f"attn_merge_{l}", *parts)
            scatter(("w_qkv", o_), _mm_tn(f"gw_qkv_{l}", s["hn1"], d_qkv, nb=w_qkv.shape[2]))
            d_hn_next = _mm_nt_blk(f"bwd_qkv_{l}", d_qkv, wg[("w_qkv", o_)],
                                   after=[rs[("w_o", o_)][0], rs[("w_qkv", o_)][0]])
    grad_x, dg["norm_mix_pre"][0] = _norm_bwd("nb_first", d_h, pre=(d_hn_next, h0, norm_mix_pre[0][None]))

    small_g = ([jnp.concatenate(dg[nm], axis=0) for nm in dg]
               + [jnp.stack(d_ws), jnp.stack(d_bs), jnp.stack([c[:CONV_TAPS] for c in d_cw]), loss_row])
    small_land, small_send, small_recv = _ag_direct_start("ag_small_start", _fill_slot("fill_small", _pack(small_g), block))

    moments = {"w_in_ab": (m_w_in_ab, v_w_in_ab), "w_out_ab": (m_w_out_ab, v_w_out_ab), "w_qkv": (m_w_qkv, v_w_qkv),
               "w_o": (m_w_o, v_w_o), "w_up": (m_w_up, v_w_up), "w_down": (m_w_down, v_w_down)}
    out_big = {}
    behind = small_land
    for nm in ("w_o", "w_qkv", "w_down", "w_up", "w_out_ab", "w_in_ab"):
        own, landed = [], []
        for l in range(big[nm].shape[0]):
            g, land = _wait_all(f"rs_wait_{nm}_{l}", *rs[(nm, l)], behind)
            own.append(g)
            landed.append(land)
        out_big[nm] = _adamw_layers(f"adamw_{nm}", own, landed, block, big[nm], moments[nm][0], moments[nm][1])
        behind = out_big[nm][0]

    summed = _sum_parts("sum_small", _ag_direct_wait("ag_small_wait", small_land, small_send, small_recv, behind))
    g_nmp, g_nmo, g_nlp, g_nlo, g_ws, g_bs, g_cw_all, loss_sum = _unpack(summed, small_g)
    loss = loss_sum[0, 0]
    cwb = conv_w.shape[2]
    g_cw = lax.dynamic_slice_in_dim(g_cw_all, my_block * cwb, cwb, axis=2)
    small_w = [norm_mix_pre, norm_mix_post, norm_mlp_pre, norm_mlp_post, w_spatial, b_spatial, conv_w]
    small_m = [m_norm_mix_pre, m_norm_mix_post, m_norm_mlp_pre, m_norm_mlp_post, m_w_spatial, m_b_spatial, m_conv_w]
    small_v = [v_norm_mix_pre, v_norm_mix_post, v_norm_mlp_pre, v_norm_mlp_post, v_w_spatial, v_b_spatial, v_conv_w]
    small_grad = [g_nmp, g_nmo, g_nlp, g_nlo, g_ws, g_bs, g_cw]
    upd = _adamw("adamw_small", _pack(small_grad)[None], _pack(small_w), _pack(small_m), _pack(small_v))
    sg, sd, sm, sv = [_unpack(u, small_w) for u in upd]

    def outs(i_small, i_big):
        return (i_small[0], i_small[1], i_small[2], i_small[3], i_big["w_in_ab"], i_small[4], i_small[5], i_small[6],
                i_big["w_out_ab"], i_big["w_qkv"], i_big["w_o"], i_big["w_up"], i_big["w_down"])

    pick = lambda i: {nm: out_big[nm][i] for nm in big}
    return (loss, grad_x.reshape(x.shape), *outs(sg, pick(0)), *outs(sd, pick(1)), *outs(sm, pick(2)),
            *outs(sv, pick(3)))
```

```python
import functools

import jax
import jax.numpy as jnp
from jax import lax
from jax.experimental import pallas as pl
from jax.experimental.pallas import tpu as pltpu

F32 = jnp.float32
BF16 = jnp.bfloat16
MESH = pl.DeviceIdType.MESH
ANY = pl.BlockSpec(memory_space=pl.ANY)
HBM = pl.BlockSpec(memory_space=pltpu.HBM)
SEM = pl.BlockSpec(memory_space=pltpu.SEMAPHORE)
EFFECT = pltpu.SideEffectType.DATAFLOW_SIDE_EFFECTING

NDEV = 8
NCHIP = 4
RMS_EPS = 1e-6
LN_EPS = 1e-5
CHUNK = 128
HEAD = 128
ATT_TILE = 2048
ATT_UNROLL = 16
DILATIONS = (1, 4, 16)
SEGS = 16
ROPE_THETA = 10000.0
CONV_TAPS = 3
HALO = 8
HALO_BF16 = 16
GELU_C = 0.7978845608028654
GELU_A = 0.044715
ADAM_LR, ADAM_B1, ADAM_B2, ADAM_EPS, ADAM_WD, ADAM_STEP = 0.001, 0.9, 0.999, 1e-08, 0.01, 10
VMEM_MM = 52 << 20
VMEM_EW = 40 << 20


def _cp(sem=None, vmem=VMEM_EW):
    if sem is None:
        return pltpu.CompilerParams(vmem_limit_bytes=vmem)
    return pltpu.CompilerParams(dimension_semantics=sem, vmem_limit_bytes=vmem)


def _tile(n, want):
    return want if n % want == 0 else n


def _all_gather(name, shards, after=()):
    n = len(shards)
    after = list(after)

    def body(*refs):
        ins, outs = refs[:n], refs[n + len(after):2 * n + len(after)]
        send_sems, recv_sems, local_sems = refs[2 * n + len(after):]
        x, y, c = lax.axis_index("x"), lax.axis_index("y"), lax.axis_index("c")
        me, sibling = (x, y, c), (x, y, 1 - c)
        chips = [(1 - x, y), (x, 1 - y), (1 - x, 1 - y)]

        def slot(p):
            return 4 * p[0] + 2 * p[1] + p[2]

        def copy(i, k, block, to, src=None):
            dst = outs[i].at[slot(block)]
            return pltpu.make_async_remote_copy(
                src_ref=dst if src is None else src, dst_ref=dst,
                send_sem=send_sems.at[i, k], recv_sem=recv_sems.at[i, k],
                device_id=to, device_id_type=MESH)

        mine = [pltpu.make_async_copy(ins[i], outs[i].at[slot(me)], local_sems.at[i]) for i in range(n)]
        for cp in mine:
            cp.start()
        first = []
        for i in range(n):
            first.append(copy(i, 0, me, sibling, src=ins[i]))
            for j, chip in enumerate(chips):
                first.append(copy(i, 1 + j, me, (*chip, c), src=ins[i]))
        for cp in first:
            cp.start()
        passed = []
        for j, chip in enumerate(chips):
            for i in range(n):
                copy(i, 1 + j, (*chip, c), me).wait_recv()
                fwd = copy(i, 4 + j, (*chip, c), sibling)
                fwd.start()
                passed.append(fwd)
        for i in range(n):
            copy(i, 0, sibling, me).wait_recv()
            for j, chip in enumerate(chips):
                copy(i, 4 + j, (*chip, 1 - c), me).wait_recv()
        for cp in first + passed:
            cp.wait_send()
        for cp in mine:
            cp.wait()

    return pl.pallas_call(
        body, name=name,
        out_shape=[jax.ShapeDtypeStruct((NDEV,) + s.shape, s.dtype) for s in shards],
        in_specs=[ANY] * (n + len(after)), out_specs=[ANY] * n,
        scratch_shapes=[pltpu.SemaphoreType.DMA((n, 7)), pltpu.SemaphoreType.DMA((n, 7)),
                        pltpu.SemaphoreType.DMA((n,))],
    )(*shards, *after)


def _peer(x, y, c, r):
    return (1 - x if r & 4 else x, 1 - y if r & 2 else y, 1 - c if r & 1 else c)


def _slot(p):
    return 4 * p[0] + 2 * p[1] + p[2]


def _cast_fill(name, w, layer, block):
    _, rows, cols = w.shape
    tr = _tile(rows, 256)

    def body(blk_ref, w_ref, o_ref):
        o_ref[...] = w_ref[...].astype(BF16)

    return pl.pallas_call(
        body, name=name,
        grid_spec=pltpu.PrefetchScalarGridSpec(
            num_scalar_prefetch=1, grid=(rows // tr,),
            in_specs=[pl.BlockSpec((None, tr, cols), lambda i, blk: (layer, i, 0))],
            out_specs=pl.BlockSpec((None, tr, cols), lambda i, blk: (blk[0], i, 0))),
        out_shape=jax.ShapeDtypeStruct((NDEV, rows, cols), BF16),
        compiler_params=_cp(("parallel",)))(block, w)


OTHER_CHIPS = (2, 4, 6)
AG_SEMS = 6
AG_PIN_FROM = 5


def _ag_start(name, lands, after):
    n = len(lands)

    def body(*refs):
        ins, sems = refs[:n], refs[n + 1:n + 1 + AG_SEMS * n]
        x, y, c = lax.axis_index("x"), lax.axis_index("y"), lax.axis_index("c")
        mine = _slot((x, y, c))
        for i in range(n):
            send_a, *recv_a, _, recv_b = sems[AG_SEMS * i:AG_SEMS * (i + 1)]
            block = ins[i].at[mine]
            pltpu.make_async_remote_copy(src_ref=block, dst_ref=block, send_sem=send_a, recv_sem=recv_b,
                                         device_id=_peer(x, y, c, 1), device_id_type=MESH).start()
            for k, r in enumerate(OTHER_CHIPS):
                pltpu.make_async_remote_copy(src_ref=block, dst_ref=block, send_sem=send_a, recv_sem=recv_a[k],
                                             device_id=_peer(x, y, c, r), device_id_type=MESH).start()

    outs = pl.pallas_call(
        body, name=name,
        out_shape=[pltpu.SemaphoreType.DMA(())] * (AG_SEMS * n) + [pltpu.HBM(a.shape, a.dtype) for a in lands],
        in_specs=[HBM] * n + [ANY], out_specs=[SEM] * (AG_SEMS * n) + [HBM] * n,
        input_output_aliases={i: AG_SEMS * n + i for i in range(n)},
        compiler_params=pltpu.CompilerParams(has_side_effects=EFFECT),
    )(*[pltpu.with_memory_space_constraint(a, pltpu.HBM) for a in lands], after)
    return outs[AG_SEMS * n:], [tuple(outs[AG_SEMS * i:AG_SEMS * (i + 1)]) for i in range(n)]


def _ag_mid(name, land, sems, after):
    _, *recv_a, send_b, recv_b = sems

    def body(land_ref, ra0, ra1, ra2, send_b_ref, recv_b_ref, after_ref, land_out):
        x, y, c = lax.axis_index("x"), lax.axis_index("y"), lax.axis_index("c")
        sibling = _peer(x, y, c, 1)
        for arrival, r in zip((ra0, ra1, ra2), OTHER_CHIPS):
            block = land_ref.at[_slot(_peer(x, y, c, r))]
            pltpu.make_async_remote_copy(src_ref=block, dst_ref=block, send_sem=send_b_ref, recv_sem=arrival,
                                         device_id=sibling, device_id_type=MESH).wait_recv()
            pltpu.make_async_remote_copy(src_ref=block, dst_ref=block, send_sem=send_b_ref, recv_sem=recv_b_ref,
                                         device_id=sibling, device_id_type=MESH).start()

    return pl.pallas_call(
        body, name=name, out_shape=pltpu.HBM(land.shape, land.dtype),
        in_specs=[HBM] + [SEM] * 5 + [ANY], out_specs=HBM, input_output_aliases={0: 0},
        compiler_params=pltpu.CompilerParams(has_side_effects=EFFECT),
    )(land, *recv_a, send_b, recv_b, after)


def _ag_wait(name, land, sems, after):
    send_a, _, _, _, send_b, recv_b = sems

    def body(land_ref, send_a_ref, send_b_ref, recv_b_ref, after_ref, land_out):
        x, y, c = lax.axis_index("x"), lax.axis_index("y"), lax.axis_index("c")
        sibling = _peer(x, y, c, 1)
        four = land_ref.at[pl.ds(0, 1 + len(OTHER_CHIPS))]
        three = land_ref.at[pl.ds(0, len(OTHER_CHIPS))]
        first = pltpu.make_async_remote_copy(src_ref=four, dst_ref=four, send_sem=send_a_ref, recv_sem=recv_b_ref,
                                             device_id=sibling, device_id_type=MESH)
        passed = pltpu.make_async_remote_copy(src_ref=three, dst_ref=three, send_sem=send_b_ref, recv_sem=recv_b_ref,
                                              device_id=sibling, device_id_type=MESH)
        first.wait_send()
        passed.wait_send()
        first.wait_recv()

    return pl.pallas_call(
        body, name=name, out_shape=pltpu.HBM(land.shape, land.dtype),
        in_specs=[HBM, SEM, SEM, SEM, ANY], out_specs=HBM, input_output_aliases={0: 0},
        compiler_params=pltpu.CompilerParams(has_side_effects=EFFECT),
    )(land, send_a, send_b, recv_b, after)


def _fill_slot(name, rows, block):
    r, c = rows.shape

    def body(blk_ref, i_ref, o_ref):
        o_ref[...] = i_ref[...]

    return pl.pallas_call(
        body, name=name,
        grid_spec=pltpu.PrefetchScalarGridSpec(
            num_scalar_prefetch=1, grid=(1,),
            in_specs=[pl.BlockSpec((r, c), lambda i, blk: (0, 0))],
            out_specs=pl.BlockSpec((None, r, c), lambda i, blk: (blk[0], 0, 0))),
        out_shape=jax.ShapeDtypeStruct((NDEV, r, c), rows.dtype),
        compiler_params=_cp(("arbitrary",)))(block, rows)


def _ag_direct_start(name, land):
    def body(land_ref, send, recv, land_out):
        x, y, c = lax.axis_index("x"), lax.axis_index("y"), lax.axis_index("c")
        block = land_ref.at[_slot((x, y, c))]
        for r in range(1, NDEV):
            pltpu.make_async_remote_copy(src_ref=block, dst_ref=block, send_sem=send, recv_sem=recv,
                                         device_id=_peer(x, y, c, r), device_id_type=MESH).start()

    send, recv, land_thru = pl.pallas_call(
        body, name=name,
        out_shape=[pltpu.SemaphoreType.DMA(()), pltpu.SemaphoreType.DMA(()), pltpu.HBM(land.shape, land.dtype)],
        in_specs=[HBM], out_specs=[SEM, SEM, HBM], input_output_aliases={0: 2},
        compiler_params=pltpu.CompilerParams(has_side_effects=EFFECT),
    )(pltpu.with_memory_space_constraint(land, pltpu.HBM))
    return land_thru, send, recv


def _ag_direct_wait(name, land, send, recv, after):
    def body(land_ref, send_ref, recv_ref, after_ref, land_out):
        x, y, c = lax.axis_index("x"), lax.axis_index("y"), lax.axis_index("c")
        seven = land_ref.at[pl.ds(0, NDEV - 1)]
        copy = pltpu.make_async_remote_copy(src_ref=seven, dst_ref=seven, send_sem=send_ref, recv_sem=recv_ref,
                                            device_id=_peer(x, y, c, 1), device_id_type=MESH)
        copy.wait_send()
        copy.wait_recv()

    return pl.pallas_call(
        body, name=name, out_shape=pltpu.HBM(land.shape, land.dtype),
        in_specs=[HBM, SEM, SEM, ANY], out_specs=HBM, input_output_aliases={0: 0},
        compiler_params=pltpu.CompilerParams(has_side_effects=EFFECT),
    )(land, send, recv, after)


def _wait_all(name, src, land, send, recv, after):
    def body(src_ref, land_ref, send_ref, recv_ref, after_ref, src_out, land_out):
        x, y, c = lax.axis_index("x"), lax.axis_index("y"), lax.axis_index("c")
        seven = land_ref.at[pl.ds(0, NDEV - 1)]
        copy = pltpu.make_async_remote_copy(src_ref=seven, dst_ref=seven, send_sem=send_ref, recv_sem=recv_ref,
                                            device_id=_peer(x, y, c, 1), device_id_type=MESH)
        copy.wait_send()
        copy.wait_recv()

    return pl.pallas_call(
        body, name=name,
        out_shape=[pltpu.HBM(src.shape, src.dtype), pltpu.HBM(land.shape, land.dtype)],
        in_specs=[HBM, HBM, SEM, SEM, ANY], out_specs=[HBM, HBM],
        input_output_aliases={0: 0, 1: 1},
        compiler_params=pltpu.CompilerParams(has_side_effects=EFFECT),
    )(src, land, send, recv, after)


def _rs_start(name, grad):
    land = lax.empty((NDEV - 1,) + grad.shape[1:], grad.dtype)

    def body(g_ref, land_ref, send, recv, g_out, land_out):
        x, y, c = lax.axis_index("x"), lax.axis_index("y"), lax.axis_index("c")
        for r in range(1, NDEV):
            peer = _peer(x, y, c, r)
            pltpu.make_async_remote_copy(
                src_ref=g_ref.at[_slot(peer)], dst_ref=land_ref.at[r - 1], send_sem=send, recv_sem=recv,
                device_id=peer, device_id_type=MESH).start()

    send, recv, g_thru, land_thru = pl.pallas_call(
        body, name=name,
        out_shape=[pltpu.SemaphoreType.DMA(()), pltpu.SemaphoreType.DMA(()),
                   pltpu.HBM(grad.shape, grad.dtype), pltpu.HBM(land.shape, land.dtype)],
        in_specs=[HBM, HBM], out_specs=[SEM, SEM, HBM, HBM], input_output_aliases={0: 2, 1: 3},
        compiler_params=pltpu.CompilerParams(has_side_effects=EFFECT),
    )(pltpu.with_memory_space_constraint(grad, pltpu.HBM), pltpu.with_memory_space_constraint(land, pltpu.HBM))
    return g_thru, land_thru, send, recv


def _adam_math(w, g, m, v):
    m = ADAM_B1 * m + (1.0 - ADAM_B1) * g
    v = ADAM_B2 * v + (1.0 - ADAM_B2) * (g * g)
    m_hat = m / (1.0 - ADAM_B1 ** ADAM_STEP)
    v_hat = v / (1.0 - ADAM_B2 ** ADAM_STEP)
    delta = -ADAM_LR * (m_hat / (jnp.sqrt(v_hat) + ADAM_EPS) + ADAM_WD * w)
    return delta, m, v


def _adamw(name, parts, w, m, v):
    nparts, rows, cols = parts.shape
    tr = _tile(rows, 256)

    def body(p_ref, w_ref, m_ref, v_ref, g_out, d_out, m_out, v_out):
        g = p_ref[0].astype(F32)
        for k in range(1, nparts):
            g = g + p_ref[k].astype(F32)
        delta, mn, vn = _adam_math(w_ref[...], g, m_ref[...], v_ref[...])
        g_out[...] = g
        d_out[...] = delta
        m_out[...] = mn
        v_out[...] = vn

    row = pl.BlockSpec((tr, cols), lambda i: (i, 0))
    return pl.pallas_call(
        body, name=name, grid=(rows // tr,),
        in_specs=[pl.BlockSpec((nparts, tr, cols), lambda i: (0, i, 0)), row, row, row],
        out_specs=[row] * 4,
        out_shape=[jax.ShapeDtypeStruct((rows, cols), F32)] * 4,
        compiler_params=_cp(("parallel",)),
    )(parts, w, m, v)


def _adamw_layers(name, grads, lands, block, w, m, v):
    layers, rows, cols = w.shape
    nland = lands[0].shape[0]
    tr = rows
    while tr % 2 == 0 and tr > 8 and nland * tr * cols * 2 > (2 << 20):
        tr //= 2

    def body(blk_ref, *refs):
        own_refs, land_refs = refs[:layers], refs[layers:2 * layers]
        w_ref, m_ref, v_ref, g_out, d_out, m_out, v_out = refs[2 * layers:]
        layer = pl.program_id(0)
        for k in range(layers):
            @pl.when(layer == k)
            def _(k=k):
                g = own_refs[k][...].astype(F32)
                for s in range(nland):
                    g = g + land_refs[k][s].astype(F32)
                delta, mn, vn = _adam_math(w_ref[...], g, m_ref[...], v_ref[...])
                g_out[...] = g
                d_out[...] = delta
                m_out[...] = mn
                v_out[...] = vn

    def own_spec(k):
        return pl.BlockSpec((None, tr, cols), lambda l, i, blk: (blk[0], jnp.where(l == k, i, 0), 0))

    def land_spec(k):
        return pl.BlockSpec((nland, tr, cols), lambda l, i, blk: (0, jnp.where(l == k, i, 0), 0))

    row = pl.BlockSpec((None, tr, cols), lambda l, i, blk: (l, i, 0))
    return pl.pallas_call(
        body, name=name,
        grid_spec=pltpu.PrefetchScalarGridSpec(
            num_scalar_prefetch=1, grid=(layers, rows // tr),
            in_specs=[own_spec(k) for k in range(layers)] + [land_spec(k) for k in range(layers)] + [row, row, row],
            out_specs=[row] * 4),
        out_shape=[jax.ShapeDtypeStruct((layers, rows, cols), F32)] * 4,
        compiler_params=_cp(("arbitrary", "arbitrary")),
    )(block, *grads, *lands, w, m, v)


LANES = 128


def _seg_scratch(rows, d):
    return pltpu.VMEM((d // LANES, rows, LANES), F32)


def _to_segments(vals, scratch, out_ref):
    per = scratch.shape[1] // SEGS
    for c in range(scratch.shape[0]):
        cols = slice(c * LANES, (c + 1) * LANES)
        scratch[c] = vals[:, cols]
        for s in range(SEGS):
            out_ref[s, :, cols] = scratch.at[c][pl.ds(s, per, stride=SEGS), :].astype(out_ref.dtype)


def _from_segments(in_ref, scratch):
    per = scratch.shape[1] // SEGS
    for c in range(scratch.shape[0]):
        for s in range(SEGS):
            scratch.at[c][pl.ds(s, per, stride=SEGS), :] = in_ref[s, :, c * LANES:(c + 1) * LANES].astype(F32)
    return jnp.concatenate([scratch[c] for c in range(scratch.shape[0])], axis=1)


def _seg_view(a):
    return a.reshape(SEGS, a.shape[0] // SEGS, a.shape[1])


def _norm_fwd(name, h, z=None, g_post=None, g_pre=None, seg_z=False, seg_y=False):
    rows, d = h.shape
    tm = _tile(rows, 256)
    has_post, has_pre = z is not None, g_pre is not None
    nscratch = int(seg_z) + int(seg_y)

    def body(*refs):
        scratch = list(refs[len(refs) - nscratch:])
        it = iter(refs)
        hv = next(it)[...]
        if has_post:
            z_ref = next(it)
            zv = _from_segments(z_ref, scratch.pop(0)) if seg_z else z_ref[...].astype(F32)
            gp = next(it)[...]
        if has_pre:
            gq = next(it)[...]
        if has_post:
            r = lax.rsqrt(jnp.mean(zv * zv, axis=-1, keepdims=True) + RMS_EPS)
            hv = hv + (zv * r) * gp
            next(it)[...] = hv
        if has_pre:
            r = lax.rsqrt(jnp.mean(hv * hv, axis=-1, keepdims=True) + RMS_EPS)
            y = (hv * r) * gq
            if seg_y:
                _to_segments(y, scratch.pop(0), next(it))
            else:
                next(it)[...] = y.astype(BF16)

    row = pl.BlockSpec((tm, d), lambda i: (i, 0))
    seg = pl.BlockSpec((SEGS, tm // SEGS, d), lambda i: (0, i, 0))
    vec = pl.BlockSpec((1, d), lambda i: (0, 0))
    ins, in_specs, out_shape, out_specs = [h], [row], [], []
    if has_post:
        ins += [_seg_view(z) if seg_z else z, g_post]
        in_specs += [seg if seg_z else row, vec]
        out_shape.append(jax.ShapeDtypeStruct((rows, d), F32))
        out_specs.append(row)
    if has_pre:
        ins.append(g_pre)
        in_specs.append(vec)
        out_shape.append(jax.ShapeDtypeStruct((SEGS, rows // SEGS, d) if seg_y else (rows, d), BF16))
        out_specs.append(seg if seg_y else row)
    outs = pl.pallas_call(body, name=name, grid=(rows // tm,), in_specs=in_specs, out_specs=out_specs,
                          out_shape=out_shape, scratch_shapes=[_seg_scratch(tm, d)] * nscratch,
                          compiler_params=_cp(("parallel",)))(*ins)
    if seg_y:
        outs = list(outs[:-1]) + [outs[-1].reshape(rows, d)]
    return outs


def _rms_bwd_rows(x, g, dy):
    r = lax.rsqrt(jnp.mean(x * x, axis=-1, keepdims=True) + RMS_EPS)
    xn = x * r
    dg = jnp.sum(dy * xn, axis=0, keepdims=True)
    dxn = dy * g
    dx = r * (dxn - xn * jnp.mean(dxn * xn, axis=-1, keepdims=True))
    return dx, dg


def _norm_bwd(name, d_out, pre=None, post=None, seg_dy=False, seg_z=False):
    rows, d = d_out.shape
    tm = _tile(rows, 256)
    has_pre, has_post = pre is not None, post is not None
    nscratch = int(seg_dy) + 2 * int(seg_z)

    def body(*refs):
        scratch = list(refs[len(refs) - nscratch:])
        it = iter(refs)
        dres = next(it)[...]
        if has_pre:
            dy_ref = next(it)
            dy = _from_segments(dy_ref, scratch.pop(0)) if seg_dy else dy_ref[...].astype(F32)
            xp, gq = next(it)[...], next(it)[...]
        if has_post:
            z_ref = next(it)
            zv = _from_segments(z_ref, scratch.pop(0)) if seg_z else z_ref[...].astype(F32)
            gp = next(it)[...]
        first = pl.program_id(0) == 0
        if has_pre:
            dx, dg = _rms_bwd_rows(xp, gq, dy)
            dres = dres + dx
            next(it)[...] = dres
            dg_ref = next(it)

            @pl.when(first)
            def _():
                dg_ref[...] = jnp.zeros_like(dg_ref)
            dg_ref[...] += dg
        if has_post:
            dz, dg2 = _rms_bwd_rows(zv, gp, dres)
            if seg_z:
                _to_segments(dz, scratch.pop(0), next(it))
            else:
                next(it)[...] = dz.astype(BF16)
            dg2_ref = next(it)

            @pl.when(first)
            def _():
                dg2_ref[...] = jnp.zeros_like(dg2_ref)
            dg2_ref[...] += dg2

    row = pl.BlockSpec((tm, d), lambda i: (i, 0))
    seg = pl.BlockSpec((SEGS, tm // SEGS, d), lambda i: (0, i, 0))
    vec = pl.BlockSpec((1, d), lambda i: (0, 0))
    ins, in_specs, out_shape, out_specs = [d_out], [row], [], []
    if has_pre:
        d_y, x_pre, g_pre = pre
        ins += [_seg_view(d_y) if seg_dy else d_y, x_pre, g_pre]
        in_specs += [seg if seg_dy else row, row, vec]
        out_shape += [jax.ShapeDtypeStruct((rows, d), F32), jax.ShapeDtypeStruct((1, d), F32)]
        out_specs += [row, vec]
    if has_post:
        z, g_post = post
        ins += [_seg_view(z) if seg_z else z, g_post]
        in_specs += [seg if seg_z else row, vec]
        out_shape += [jax.ShapeDtypeStruct((SEGS, rows // SEGS, d) if seg_z else (rows, d), BF16),
                      jax.ShapeDtypeStruct((1, d), F32)]
        out_specs += [seg if seg_z else row, vec]
    outs = pl.pallas_call(body, name=name, grid=(rows // tm,), in_specs=in_specs, out_specs=out_specs,
                          out_shape=out_shape, scratch_shapes=[_seg_scratch(tm, d)] * nscratch,
                          compiler_params=_cp(("arbitrary",)))(*ins)
    if seg_z:
        outs = list(outs)
        outs[-2] = outs[-2].reshape(rows, d)
    return outs


def _loss_grad(name, y, target):
    rows, d = y.shape
    tm = _tile(rows, 256)

    def body(y_ref, t_ref, dy_ref, loss_ref):
        err = y_ref[...] - t_ref[...]
        dy_ref[...] = err * (1.0 / d)

        @pl.when(pl.program_id(0) == 0)
        def _():
            loss_ref[...] = jnp.zeros_like(loss_ref)
        loss_ref[...] += jnp.full(loss_ref.shape, (0.5 / d) * jnp.sum(err * err), F32)

    row = pl.BlockSpec((tm, d), lambda i: (i, 0))
    return pl.pallas_call(
        body, name=name, grid=(rows // tm,), in_specs=[row, row],
        out_specs=[row, pl.BlockSpec((1, 128), lambda i: (0, 0))],
        out_shape=[jax.ShapeDtypeStruct((rows, d), F32), jax.ShapeDtypeStruct((1, 128), F32)],
        compiler_params=_cp(("arbitrary",)))(y, target)


NT_DIMS = (((1,), (1,)), ((), ()))
TN_DIMS = (((0,), (0,)), ((), ()))


MXU_WIDTH = 256


def _blocks_per_step(nb):
    return 1 if nb % MXU_WIDTH == 0 else 2


def _mm_nn_blk(name, a, wblk, relu2=False, after=()):
    m, k = a.shape
    nb = wblk.shape[2]
    tm = _tile(m, 1024)
    per = _blocks_per_step(nb)
    after = list(after)

    def body(a_ref, w_ref, *rest):
        w = w_ref[0] if per == 1 else jnp.concatenate([w_ref[t] for t in range(per)], axis=1)
        r = jnp.dot(a_ref[...], w, preferred_element_type=F32)
        if relu2:
            rr = jnp.maximum(r, 0.0)
            r = rr * rr
        rest[-1][...] = r.astype(BF16)

    return pl.pallas_call(
        body, name=name, grid=(NDEV // per, m // tm),
        in_specs=[pl.BlockSpec((tm, k), lambda d, i: (i, 0)), pl.BlockSpec((per, k, nb), lambda d, i: (d, 0, 0))]
        + [ANY] * len(after),
        out_specs=pl.BlockSpec((tm, per * nb), lambda d, i: (i, d)),
        out_shape=jax.ShapeDtypeStruct((m, NDEV * nb), BF16),
        compiler_params=_cp(("parallel", "parallel"), VMEM_MM))(a, wblk, *after)


def _accumulate(acc, o_ref, r, step, last):
    if acc is None:
        o_ref[...] = r.astype(o_ref.dtype)
        return

    @pl.when(step == 0)
    def _():
        acc[...] = r

    @pl.when(jnp.logical_and(step > 0, step < last))
    def _():
        acc[...] += r

    @pl.when(jnp.logical_and(step > 0, step == last))
    def _():
        o_ref[...] = (acc[...] + r).astype(o_ref.dtype)


def _mm_nn(name, a, w, after=()):
    m, kb = a.shape
    n = w.shape[1]
    one_step = kb <= 2048
    tm = _tile(m, 512 if one_step else 1024)
    tk = kb if one_step else _tile(kb, 4096)
    tn = n if one_step else _tile(n, 1024)
    steps = kb // tk
    after = list(after)

    def body(a_ref, w_ref, *rest):
        o_ref, scratch = rest[len(after)], rest[len(after) + 1:]
        r = jnp.dot(a_ref[...], w_ref[...], preferred_element_type=F32)
        _accumulate(scratch[0] if scratch else None, o_ref, r, pl.program_id(2), steps - 1)

    return pl.pallas_call(
        body, name=name, grid=(m // tm, n // tn, steps),
        in_specs=[pl.BlockSpec((tm, tk), lambda i, j, s: (i, s)), pl.BlockSpec((tk, tn), lambda i, j, s: (s, j))]
        + [ANY] * len(after),
        out_specs=pl.BlockSpec((tm, tn), lambda i, j, s: (i, j)),
        out_shape=jax.ShapeDtypeStruct((m, n), BF16),
        scratch_shapes=[pltpu.VMEM((tm, tn), F32)] if steps > 1 else [],
        compiler_params=_cp(("parallel", "parallel", "arbitrary"), VMEM_MM))(a, w, *after)


def _mm_nt_rows(name, dy, w, act=None, out_dtype=BF16):
    m, n = dy.shape
    kw = w.shape[0]
    tm, tkw = _tile(m, 1024), _tile(kw, 1024)

    def body(dy_ref, w_ref, *rest):
        r = lax.dot_general(dy_ref[...], w_ref[...], NT_DIMS, preferred_element_type=F32)
        if act is None:
            rest[0][...] = r.astype(out_dtype)
        else:
            rest[1][...] = (r * (2.0 * jnp.sqrt(rest[0][...].astype(F32)))).astype(BF16)

    ins = [dy, w]
    in_specs = [pl.BlockSpec((tm, n), lambda j, i: (i, 0)), pl.BlockSpec((tkw, n), lambda j, i: (j, 0))]
    if act is not None:
        ins.append(act)
        in_specs.append(pl.BlockSpec((tm, tkw), lambda j, i: (i, j)))
    return pl.pallas_call(
        body, name=name, grid=(kw // tkw, m // tm), in_specs=in_specs,
        out_specs=pl.BlockSpec((tm, tkw), lambda j, i: (i, j)),
        out_shape=jax.ShapeDtypeStruct((m, kw), out_dtype if act is None else BF16),
        compiler_params=_cp(("parallel", "parallel"), VMEM_MM))(*ins)


def _mm_nt_blk(name, dy, wblk, after=None):
    m = dy.shape[0]
    _, kw, nb = wblk.shape
    tm, tkw, per = _tile(m, 1024), _tile(kw, 1024), 4

    extra = list(after or ())

    def body(dy_ref, w_ref, *rest):
        o_ref, acc = rest[len(extra):]
        r = lax.dot_general(dy_ref[:, :nb], w_ref[0], NT_DIMS, preferred_element_type=F32)
        for t in range(1, per):
            r = r + lax.dot_general(dy_ref[:, t * nb:(t + 1) * nb], w_ref[t], NT_DIMS, preferred_element_type=F32)
        _accumulate(acc, o_ref, r, pl.program_id(2), NDEV // per - 1)

    return pl.pallas_call(
        body, name=name, grid=(m // tm, kw // tkw, NDEV // per),
        in_specs=[pl.BlockSpec((tm, per * nb), lambda i, j, s: (i, s)),
                  pl.BlockSpec((per, tkw, nb), lambda i, j, s: (s, j, 0))] + [ANY] * len(extra),
        out_specs=pl.BlockSpec((tm, tkw), lambda i, j, s: (i, j)),
        out_shape=jax.ShapeDtypeStruct((m, kw), BF16),
        scratch_shapes=[pltpu.VMEM((tm, tkw), F32)],
        compiler_params=_cp(("parallel", "parallel", "arbitrary"), VMEM_MM))(dy, wblk, *extra)


def _mm_tn(name, x, dy, nb=None):
    t, mx = x.shape
    n = dy.shape[1]
    tmx = _tile(mx, 512)
    per = 1 if nb is None else _blocks_per_step(nb)
    tn = per * nb if nb is not None else _tile(n, 1024)

    def body(x_ref, dy_ref, o_ref):
        r = lax.dot_general(x_ref[...], dy_ref[...], TN_DIMS, preferred_element_type=F32).astype(BF16)
        if nb is None:
            o_ref[...] = r
        else:
            for b in range(per):
                o_ref[b] = r[:, b * nb:(b + 1) * nb]

    if nb is None:
        out_shape = jax.ShapeDtypeStruct((mx, n), BF16)
        out_spec = pl.BlockSpec((tmx, tn), lambda j, i: (i, j))
    else:
        out_shape = jax.ShapeDtypeStruct((NDEV, mx, nb), BF16)
        out_spec = pl.BlockSpec((per, tmx, nb), lambda j, i: (j, i, 0))
    return pl.pallas_call(
        body, name=name, grid=(n // tn, mx // tmx),
        in_specs=[pl.BlockSpec((t, tmx), lambda j, i: (0, i)), pl.BlockSpec((t, tn), lambda j, i: (0, j))],
        out_specs=out_spec, out_shape=out_shape,
        compiler_params=_cp(("parallel", "parallel"), VMEM_MM))(x, dy)


def _gelu(x):
    return 0.5 * x * (1.0 + jnp.tanh(GELU_C * (x + GELU_A * (x * x * x))))


def _gelu_grad(x):
    t = jnp.tanh(GELU_C * (x + GELU_A * (x * x * x)))
    return 0.5 * (1.0 + t) + 0.5 * x * (1.0 - t * t) * (GELU_C * (1.0 + 3.0 * GELU_A * (x * x)))


def _layernorm(a):
    mu = jnp.mean(a, axis=-1, keepdims=True)
    ac = a - mu
    rstd = lax.rsqrt(jnp.mean(ac * ac, axis=-1, keepdims=True) + LN_EPS)
    return ac * rstd, rstd


def _shift_rows(z, halo, k):
    zr = pltpu.roll(z, k, 0)
    hr = pltpu.roll(halo, k, 0)
    row = lax.broadcasted_iota(jnp.int32, hr.shape, 0)
    top = jnp.where(row < k, hr, zr[:HALO])
    return jnp.concatenate([top, zr[HALO:]], axis=0)


def _shift_rows_up(z, halo, k):
    rows = z.shape[0]
    zr = pltpu.roll(z, rows - k, 0)
    hr = pltpu.roll(halo, HALO - k, 0)
    row = lax.broadcasted_iota(jnp.int32, hr.shape, 0)
    bot = jnp.where(row >= HALO - k, hr, zr[rows - HALO:])
    return jnp.concatenate([zr[:rows - HALO], bot], axis=0)


def _causal_mask():
    t = lax.broadcasted_iota(jnp.int32, (CHUNK, CHUNK), 0)
    s = lax.broadcasted_iota(jnp.int32, (CHUNK, CHUNK), 1)
    return s <= t


def _gate_specs(tm, rows, width):
    per = tm // HALO_BF16
    last = rows // HALO_BF16 - 1
    cur = pl.BlockSpec((tm, width), lambda i: (i, 0))
    prev = pl.BlockSpec((HALO_BF16, width), lambda i: (jnp.maximum(i * per - 1, 0), 0))
    nxt = pl.BlockSpec((HALO_BF16, width), lambda i: (jnp.minimum((i + 1) * per, last), 0))
    return cur, prev, nxt


def _cols(ref, lo, hi):
    return ref[:, lo:hi].astype(F32)


def _halo_before(ref, lo, hi):
    return ref[:, lo:hi].astype(F32)[HALO_BF16 - HALO:]


def _halo_after(ref, lo, hi):
    return ref[:, lo:hi].astype(F32)[:HALO]


def _gate_fwd(name, proj, w_s, b_st, cw):
    rows, width = proj.shape
    w = width // 5
    groups = w // CHUNK
    tm = _tile(rows, 256)
    cur, prev, _ = _gate_specs(tm, rows, width)

    def body(p_ref, h_ref, ws_ref, b_ref, cw_ref, o_ref):
        mask = _causal_mask()
        au = _gelu(_cols(p_ref, 0, w))
        vn, _ = _layernorm(_gelu(_cols(p_ref, w, 2 * w)))
        vn = vn.astype(BF16)
        for g in range(groups):
            wc = jnp.where(mask, ws_ref[g], 0.0).astype(BF16)
            cols = slice(g * CHUNK, (g + 1) * CHUNK)
            for ch in range(tm // CHUNK):
                rws = slice(ch * CHUNK, (ch + 1) * CHUNK)
                mixed = jnp.dot(wc, vn[rws, cols], preferred_element_type=F32) + b_ref[:, g:g + 1]
                o_ref[rws, cols] = (au[rws, cols] * mixed).astype(BF16)
        z = _cols(p_ref, 3 * w, 4 * w) * _cols(p_ref, 4 * w, 5 * w)
        zh = _halo_before(h_ref, 3 * w, 4 * w) * _halo_before(h_ref, 4 * w, 5 * w)
        zh = jnp.where(pl.program_id(0) == 0, 0.0, zh)
        y = cw_ref[0:1, :] * _shift_rows(z, zh, 2) + cw_ref[1:2, :] * _shift_rows(z, zh, 1) + cw_ref[2:3, :] * z
        o_ref[:, w:2 * w] = (_cols(p_ref, 2 * w, 3 * w) * y).astype(BF16)

    full = lambda a: pl.BlockSpec(a.shape, lambda i: (0,) * a.ndim)
    return pl.pallas_call(
        body, name=name, grid=(rows // tm,),
        in_specs=[cur, prev, full(w_s), full(b_st), full(cw)],
        out_specs=pl.BlockSpec((tm, 2 * w), lambda i: (i, 0)),
        out_shape=jax.ShapeDtypeStruct((rows, 2 * w), BF16),
        compiler_params=_cp(("parallel",)))(proj, proj, w_s, b_st, cw)


def _gate_bwd(name, proj, d_ab, w_s, b_st, cw):
    rows, width = proj.shape
    w = width // 5
    groups = w // CHUNK
    tm = _tile(rows, 256)
    cur, prev, nxt = _gate_specs(tm, rows, width)
    dcur, _, dnxt = _gate_specs(tm, rows, 2 * w)

    def body(p_ref, ph_ref, pn_ref, d_ref, dn_ref, ws_ref, b_ref, cw_ref, o_ref, dws_ref, dbs_ref, dcw_ref):
        i = pl.program_id(0)

        @pl.when(i == 0)
        def _():
            dws_ref[...] = jnp.zeros_like(dws_ref)
            dbs_ref[...] = jnp.zeros_like(dbs_ref)
            dcw_ref[...] = jnp.zeros_like(dcw_ref)

        mask = _causal_mask()
        u, v = _cols(p_ref, 0, w), _cols(p_ref, w, 2 * w)
        au, av = _gelu(u), _gelu(v)
        vn, rstd = _layernorm(av)
        vnb = vn.astype(BF16)
        d_a = _cols(d_ref, 0, w)
        d_mixed = (d_a * au).astype(BF16)
        ones = jnp.ones((HALO, CHUNK), BF16)
        d_vn_cols = []
        d_au_cols = []
        for g in range(groups):
            wc = jnp.where(mask, ws_ref[g], 0.0).astype(BF16)
            cols = slice(g * CHUNK, (g + 1) * CHUNK)
            dw = jnp.zeros((CHUNK, CHUNK), F32)
            db = jnp.zeros((HALO, CHUNK), F32)
            d_vn_rows, d_au_rows = [], []
            for ch in range(tm // CHUNK):
                rws = slice(ch * CHUNK, (ch + 1) * CHUNK)
                mixed = jnp.dot(wc, vnb[rws, cols], preferred_element_type=F32) + b_ref[:, g:g + 1]
                d_au_rows.append(d_a[rws, cols] * mixed)
                dm = d_mixed[rws, cols]
                dw = dw + lax.dot_general(dm, vnb[rws, cols], NT_DIMS, preferred_element_type=F32)
                db = db + lax.dot_general(ones, dm, NT_DIMS, preferred_element_type=F32)
                d_vn_rows.append(lax.dot_general(wc, dm, TN_DIMS, preferred_element_type=F32))
            dws_ref[g] += jnp.where(mask, dw, 0.0)
            dbs_ref[g:g + 1, :] += db[0:1, :]
            d_vn_cols.append(jnp.concatenate(d_vn_rows, axis=0))
            d_au_cols.append(jnp.concatenate(d_au_rows, axis=0))
        d_vn = jnp.concatenate(d_vn_cols, axis=1)
        d_au = jnp.concatenate(d_au_cols, axis=1)
        d_av = rstd * (d_vn - jnp.mean(d_vn, axis=-1, keepdims=True)
                       - vn * jnp.mean(d_vn * vn, axis=-1, keepdims=True))
        o_ref[:, 0:w] = (d_au * _gelu_grad(u)).astype(BF16)
        o_ref[:, w:2 * w] = (d_av * _gelu_grad(v)).astype(BF16)

        gb, gc, bx = _cols(p_ref, 2 * w, 3 * w), _cols(p_ref, 3 * w, 4 * w), _cols(p_ref, 4 * w, 5 * w)
        z = gc * bx
        zh = jnp.where(i == 0, 0.0, _halo_before(ph_ref, 3 * w, 4 * w) * _halo_before(ph_ref, 4 * w, 5 * w))
        z1, z2 = _shift_rows(z, zh, 1), _shift_rows(z, zh, 2)
        d_b = _cols(d_ref, w, 2 * w)
        y = cw_ref[0:1, :] * z2 + cw_ref[1:2, :] * z1 + cw_ref[2:3, :] * z
        dy = d_b * gb
        dyn = jnp.where(i == pl.num_programs(0) - 1, 0.0,
                        _halo_after(dn_ref, w, 2 * w) * _halo_after(pn_ref, 2 * w, 3 * w))
        dz = (cw_ref[2:3, :] * dy + cw_ref[1:2, :] * _shift_rows_up(dy, dyn, 1)
              + cw_ref[0:1, :] * _shift_rows_up(dy, dyn, 2))
        dcw_ref[0:1, :] += jnp.sum(dy * z2, axis=0, keepdims=True)
        dcw_ref[1:2, :] += jnp.sum(dy * z1, axis=0, keepdims=True)
        dcw_ref[2:3, :] += jnp.sum(dy * z, axis=0, keepdims=True)
        o_ref[:, 2 * w:3 * w] = (d_b * y).astype(BF16)
        o_ref[:, 3 * w:4 * w] = (dz * bx).astype(BF16)
        o_ref[:, 4 * w:5 * w] = (dz * gc).astype(BF16)

    full = lambda a: pl.BlockSpec(a.shape, lambda i: (0,) * a.ndim)
    acc = lambda shape: pl.BlockSpec(shape, lambda i: (0,) * len(shape))
    return pl.pallas_call(
        body, name=name, grid=(rows // tm,),
        in_specs=[cur, prev, nxt, dcur, dnxt, full(w_s), full(b_st), full(cw)],
        out_specs=[pl.BlockSpec((tm, width), lambda i: (i, 0)), acc((groups, CHUNK, CHUNK)),
                   acc((groups, CHUNK)), acc((HALO, w))],
        out_shape=[jax.ShapeDtypeStruct((rows, width), BF16), jax.ShapeDtypeStruct((groups, CHUNK, CHUNK), F32),
                   jax.ShapeDtypeStruct((groups, CHUNK), F32), jax.ShapeDtypeStruct((HALO, w), F32)],
        compiler_params=_cp(("arbitrary",), VMEM_MM))(proj, proj, proj, d_ab, d_ab, w_s, b_st, cw)


def _flat(x):
    return x.reshape(-1, x.shape[-1])


def _rope(t, cosf, sins):
    t2 = _flat(t)
    return (t2 * _flat(cosf) + pltpu.roll(t2, HEAD // 2, 1) * _flat(sins)).reshape(t.shape)


def _rope_bwd(dt, cosf, sins):
    d2 = _flat(dt)
    return (d2 * _flat(cosf) + pltpu.roll(d2 * _flat(sins), HEAD // 2, 1)).reshape(dt.shape)


ATT_UNITS = ATT_TILE // CHUNK


def _attn_units(phases):
    for b, d in enumerate(DILATIONS):
        blocks = ATT_TILE // (CHUNK * d)
        for visit in phases:
            for r in range(d):
                if blocks <= ATT_UNROLL:
                    for j in range(blocks):
                        visit(b, d, r, j, r * blocks + j)
                else:
                    def step(jj, carry, b=b, d=d, r=r, visit=visit, blocks=blocks):
                        for u in range(ATT_UNROLL):
                            j = jj * ATT_UNROLL + u
                            visit(b, d, r, j, r * blocks + j)
                        return carry
                    lax.fori_loop(0, blocks // ATT_UNROLL, step, 0)


class _Unit:
    def __init__(self, d, r, j):
        self.segs = [r + d * k for k in range(SEGS // d)]
        self.w = CHUNK * d // SEGS
        q0 = j * self.w
        self.q0 = q0 if isinstance(q0, int) else pl.multiple_of(q0, HALO)
        k0 = CHUNK + (j - 1) * self.w
        self.k0 = k0 if isinstance(k0, int) else pl.multiple_of(k0, HALO)

    def queries(self, ref):
        return _chunks(ref, self.segs, self.q0, self.w)

    def keys(self, ref):
        return _chunks(ref, self.segs, self.k0, 2 * self.w)

    def put_queries(self, ref, val, add=False):
        _put_chunks(ref, self.segs, self.q0, self.w, val, add)

    def put_keys(self, ref, val, add=False):
        _put_chunks(ref, self.segs, self.k0, 2 * self.w, val, add)


def _chunks(ref, segs, start, size):
    parts = [ref[s, pl.ds(start, size), :] for s in segs]
    return parts[0] if len(parts) == 1 else jnp.concatenate(parts, axis=0)


def _put_chunks(ref, segs, start, size, val, add):
    for k, s in enumerate(segs):
        piece = val[k * size:(k + 1) * size]
        if add:
            ref[s, pl.ds(start, size), :] += piece
        else:
            ref[s, pl.ds(start, size), :] = piece


def _band_bias():
    qi = lax.broadcasted_iota(jnp.int32, (CHUNK, 2 * CHUNK), 0)
    ki = lax.broadcasted_iota(jnp.int32, (CHUNK, 2 * CHUNK), 1)
    tables = []
    for d in DILATIONS:
        nseg, w = SEGS // d, CHUNK * d // SEGS
        pos_q = nseg * (qi % w) + qi // w
        pos_k = nseg * (ki % (2 * w) - w) + ki // (2 * w)
        band = (pos_q >= pos_k) & (pos_q - pos_k <= CHUNK)
        tables += [jnp.where(band, 0.0, -jnp.inf), jnp.where(band & (pos_k >= 0), 0.0, -jnp.inf)]
    return jnp.stack(tables).astype(F32)


def _bias_spec():
    return pl.BlockSpec((2 * len(DILATIONS), CHUNK, 2 * CHUNK), lambda h, n: (0, 0, 0))


def _unit_bias(bias, b, n, j):
    if isinstance(j, int) and j != 0:
        return bias[2 * b]
    return bias[2 * b + jnp.where(jnp.logical_and(n == 0, j == 0), 1, 0)]


def _attn_in_specs(heads):
    blk = (SEGS, CHUNK, HEAD)
    prev = lambda n: jnp.maximum(n - 1, 0)
    return [
        pl.BlockSpec(blk, lambda h, n: (0, n, h)),
        pl.BlockSpec(blk, lambda h, n: (0, n, heads + h)),
        pl.BlockSpec(blk, lambda h, n: (0, prev(n), heads + h)),
        pl.BlockSpec(blk, lambda h, n: (0, n, 2 * heads + h)),
        pl.BlockSpec(blk, lambda h, n: (0, prev(n), 2 * heads + h)),
        pl.BlockSpec(blk, lambda h, n: (0, n, 0)),
        pl.BlockSpec(blk, lambda h, n: (0, n, 0)),
        pl.BlockSpec(blk, lambda h, n: (0, prev(n), 0)),
        pl.BlockSpec(blk, lambda h, n: (0, prev(n), 0)),
    ]


def _attn_load(q_ref, kc_ref, kp_ref, vc_ref, vp_ref, cc_ref, sc_ref, cp_ref, sp_ref, qr, kcat, vcat):
    qr[...] = _rope(q_ref[...].astype(F32), cc_ref[...], sc_ref[...]) * (HEAD ** -0.5)
    kcat[:, pl.ds(0, CHUNK), :] = _rope(kp_ref[...].astype(F32), cp_ref[...], sp_ref[...])
    kcat[:, pl.ds(CHUNK, CHUNK), :] = _rope(kc_ref[...].astype(F32), cc_ref[...], sc_ref[...])
    vcat[:, pl.ds(0, CHUNK), :] = vp_ref[...].astype(F32)
    vcat[:, pl.ds(CHUNK, CHUNK), :] = vc_ref[...].astype(F32)


def _attn_fwd(name, qkv, cosf, sins, bias):
    t = qkv.shape[0]
    heads = qkv.shape[1] // (3 * HEAD)
    nbr = len(DILATIONS)

    def body(q_ref, kc_ref, kp_ref, vc_ref, vp_ref, cc_ref, sc_ref, cp_ref, sp_ref, bias, o_ref, lse_ref,
             qr, kcat, vcat, obr, mbr, dbr, pn):
        n = pl.program_id(1)
        _attn_load(q_ref, kc_ref, kp_ref, vc_ref, vp_ref, cc_ref, sc_ref, cp_ref, sp_ref, qr, kcat, vcat)

        def probs(b, d, r, j, u):
            unit = _Unit(d, r, j)
            s = lax.dot_general(unit.queries(qr).astype(BF16), unit.keys(kcat).astype(BF16), NT_DIMS,
                                preferred_element_type=F32) + _unit_bias(bias, b, n, j)
            mx = jnp.max(s, axis=-1, keepdims=True)
            p = jnp.exp(s - mx)
            pn[u] = p.astype(BF16)
            unit.put_queries(mbr.at[b], jnp.broadcast_to(mx, (CHUNK, HEAD)))
            unit.put_queries(dbr.at[b], jnp.broadcast_to(jnp.sum(p, axis=-1, keepdims=True), (CHUNK, HEAD)))

        def values(b, d, r, j, u):
            unit = _Unit(d, r, j)
            unit.put_queries(obr.at[b], jnp.dot(pn[u], unit.keys(vcat).astype(BF16), preferred_element_type=F32))

        _attn_units([probs, values])
        ms = [mbr[b] for b in range(nbr)]
        top = functools.reduce(jnp.maximum, ms)
        ws = [jnp.exp(m - top) for m in ms]
        tot = functools.reduce(jnp.add, [ws[b] * dbr[b] for b in range(nbr)])
        inv = 1.0 / tot
        o = (ws[0] * inv) * obr[0]
        for b in range(1, nbr):
            o = o + (ws[b] * inv) * obr[b]
        o_ref[...] = o.astype(BF16)
        lse_ref[...] = top + jnp.log(tot)

    blk = (SEGS, CHUNK, HEAD)
    keys = pltpu.VMEM((SEGS, 2 * CHUNK, HEAD), F32)
    tile = pl.BlockSpec(blk, lambda h, n: (0, n, h))
    seg = t // SEGS
    qkv3, cos3, sin3 = _seg_view(qkv), _seg_view(cosf), _seg_view(sins)
    o, lse = pl.pallas_call(
        body, name=name, grid=(heads, t // ATT_TILE), in_specs=_attn_in_specs(heads) + [_bias_spec()],
        out_specs=[tile, tile],
        out_shape=[jax.ShapeDtypeStruct((SEGS, seg, heads * HEAD), BF16),
                   jax.ShapeDtypeStruct((SEGS, seg, heads * HEAD), F32)],
        scratch_shapes=[pltpu.VMEM(blk, F32), keys, keys,
                        pltpu.VMEM((nbr,) + blk, F32), pltpu.VMEM((nbr,) + blk, F32), pltpu.VMEM((nbr,) + blk, F32),
                        pltpu.VMEM((ATT_UNITS, CHUNK, 2 * CHUNK), BF16)],
        compiler_params=_cp(("parallel", "parallel"), VMEM_MM),
    )(qkv3, qkv3, qkv3, qkv3, qkv3, cos3, sin3, cos3, sin3, bias)
    return o.reshape(t, heads * HEAD), lse.reshape(t, heads * HEAD)


def _attn_bwd(name, qkv, cosf, sins, bias, d_o, o, lse):
    t = qkv.shape[0]
    heads = qkv.shape[1] // (3 * HEAD)
    scale = HEAD ** -0.5

    def body(q_ref, kc_ref, kp_ref, vc_ref, vp_ref, cc_ref, sc_ref, cp_ref, sp_ref, do_ref, o_ref, lse_ref, bias,
             dq_ref, dko_ref, dkp_ref, dvo_ref, dvp_ref, qr, kcat, vcat, dq_acc, dk_acc, dv_acc, delta, ps, dss):
        n = pl.program_id(1)
        _attn_load(q_ref, kc_ref, kp_ref, vc_ref, vp_ref, cc_ref, sc_ref, cp_ref, sp_ref, qr, kcat, vcat)
        dq_acc[...] = jnp.zeros_like(dq_acc)
        dk_acc[...] = jnp.zeros_like(dk_acc)
        dv_acc[...] = jnp.zeros_like(dv_acc)
        delta[...] = jnp.broadcast_to(
            jnp.sum(do_ref[...] * o_ref[...].astype(F32), axis=-1, keepdims=True), delta.shape)

        def probs(b, d, r, j, u):
            unit = _Unit(d, r, j)
            s = lax.dot_general(unit.queries(qr).astype(BF16), unit.keys(kcat).astype(BF16), NT_DIMS,
                                preferred_element_type=F32) + _unit_bias(bias, b, n, j)
            ps[u] = jnp.exp(s - unit.queries(lse_ref)[:, 0:1]).astype(BF16)

        def score_grads(b, d, r, j, u):
            unit = _Unit(d, r, j)
            dp = lax.dot_general(unit.queries(do_ref).astype(BF16), unit.keys(vcat).astype(BF16), NT_DIMS,
                                 preferred_element_type=F32)
            dss[u] = (ps[u].astype(F32) * (dp - unit.queries(delta)[:, 0:1])).astype(BF16)

        def input_grads(b, d, r, j, u):
            unit = _Unit(d, r, j)
            ds = dss[u]
            unit.put_queries(dq_acc, jnp.dot(ds, unit.keys(kcat).astype(BF16), preferred_element_type=F32), add=True)
            unit.put_keys(dk_acc, lax.dot_general(ds, unit.queries(qr).astype(BF16), TN_DIMS,
                                                  preferred_element_type=F32), add=True)
            unit.put_keys(dv_acc, lax.dot_general(ps[u], unit.queries(do_ref).astype(BF16), TN_DIMS,
                                                  preferred_element_type=F32), add=True)

        _attn_units([probs, score_grads, input_grads])
        dq_ref[...] = _rope_bwd(dq_acc[...] * scale, cc_ref[...], sc_ref[...]).astype(BF16)
        dkp_ref[...] = _rope_bwd(dk_acc[:, pl.ds(0, CHUNK), :], cp_ref[...], sp_ref[...]).astype(BF16)
        dko_ref[...] = _rope_bwd(dk_acc[:, pl.ds(CHUNK, CHUNK), :], cc_ref[...], sc_ref[...]).astype(BF16)
        dvp_ref[...] = dv_acc[:, pl.ds(0, CHUNK), :].astype(BF16)
        dvo_ref[...] = dv_acc[:, pl.ds(CHUNK, CHUNK), :].astype(BF16)

    blk = (SEGS, CHUNK, HEAD)
    tile = pl.BlockSpec(blk, lambda h, n: (0, n, h))
    big = pltpu.VMEM((SEGS, 2 * CHUNK, HEAD), F32)
    seg = t // SEGS
    qkv3, cos3, sin3 = _seg_view(qkv), _seg_view(cosf), _seg_view(sins)
    return pl.pallas_call(
        body, name=name, grid=(heads, t // ATT_TILE),
        in_specs=_attn_in_specs(heads) + [tile, tile, tile, _bias_spec()],
        out_specs=[tile] * 5,
        out_shape=[jax.ShapeDtypeStruct((SEGS, seg, heads * HEAD), BF16)] * 5,
        scratch_shapes=[pltpu.VMEM(blk, F32), big, big, pltpu.VMEM(blk, F32), big, big, pltpu.VMEM(blk, F32),
                        pltpu.VMEM((ATT_UNITS, CHUNK, 2 * CHUNK), BF16), pltpu.VMEM((ATT_UNITS, CHUNK, 2 * CHUNK), BF16)],
        compiler_params=_cp(("parallel", "parallel"), 60 << 20),
    )(qkv3, qkv3, qkv3, qkv3, qkv3, cos3, sin3, cos3, sin3, _seg_view(d_o), _seg_view(o), _seg_view(lse), bias)


def _attn_merge(name, dq, dk_own, dk_prev, dv_own, dv_prev):
    _, seg, hd = dq.shape
    nt = seg // CHUNK
    tw = _tile(hd, 512)

    def body(dq_ref, dko_ref, dkn_ref, dvo_ref, dvn_ref, o_ref):
        last = pl.program_id(0) == nt - 1
        part = pl.program_id(1)

        @pl.when(part == 0)
        def _():
            o_ref[...] = dq_ref[...]

        @pl.when(part == 1)
        def _():
            o_ref[...] = (dko_ref[...].astype(F32) + jnp.where(last, 0.0, dkn_ref[...].astype(F32))).astype(BF16)

        @pl.when(part == 2)
        def _():
            o_ref[...] = (dvo_ref[...].astype(F32) + jnp.where(last, 0.0, dvn_ref[...].astype(F32))).astype(BF16)

    blk = (SEGS, CHUNK, tw)

    def own(part):
        return pl.BlockSpec(blk, lambda n, p, c: (0, jnp.where(p == part, n, 0), jnp.where(p == part, c, 0)))

    def nxt(part):
        return pl.BlockSpec(blk, lambda n, p, c: (0, jnp.where(p == part, jnp.minimum(n + 1, nt - 1), 0),
                                                  jnp.where(p == part, c, 0)))

    per = hd // tw
    return pl.pallas_call(
        body, name=name, grid=(nt, 3, per), in_specs=[own(0), own(1), nxt(1), own(2), nxt(2)],
        out_specs=pl.BlockSpec(blk, lambda n, p, c: (0, n, p * per + c)),
        out_shape=jax.ShapeDtypeStruct((SEGS, seg, 3 * hd), BF16),
        compiler_params=_cp(("parallel", "parallel", "parallel"), VMEM_MM),
    )(dq, dk_own, dk_prev, dv_own, dv_prev).reshape(SEGS * seg, 3 * hd)


def _sum_parts(name, parts):
    nparts, rows, cols = parts.shape
    tr = _tile(rows, 256)

    def body(p_ref, o_ref):
        s = p_ref[0]
        for k in range(1, nparts):
            s = s + p_ref[k]
        o_ref[...] = s

    return pl.pallas_call(
        body, name=name, grid=(rows // tr,),
        in_specs=[pl.BlockSpec((nparts, tr, cols), lambda i: (0, i, 0))],
        out_specs=pl.BlockSpec((tr, cols), lambda i: (i, 0)),
        out_shape=jax.ShapeDtypeStruct((rows, cols), F32),
        compiler_params=_cp(("parallel",)))(parts)


def _rows128(a, pad_to=8):
    flat = a.reshape(-1)
    rows = -(-flat.shape[0] // 128)
    rows = -(-rows // pad_to) * pad_to
    flat = jnp.pad(flat, (0, rows * 128 - flat.shape[0]))
    return flat.reshape(rows, 128)


def _pack(arrays):
    return jnp.concatenate([_rows128(a) for a in arrays], axis=0)


def _unpack(packed, like):
    out, at = [], 0
    for a in like:
        size = 1
        for s in a.shape:
            size *= s
        rows = -(-(-(-size // 128)) // 8) * 8
        out.append(packed[at:at + rows].reshape(-1)[:size].reshape(a.shape))
        at += rows
    return out


def kernel(x, norm_mix_pre, norm_mix_post, norm_mlp_pre, norm_mlp_post, w_in_ab, w_spatial, b_spatial, conv_w, w_out_ab, w_qkv, w_o, w_up, w_down, loss_target, m_norm_mix_pre, m_norm_mix_post, m_norm_mlp_pre, m_norm_mlp_post, m_w_in_ab, m_w_spatial, m_b_spatial, m_conv_w, m_w_out_ab, m_w_qkv, m_w_o, m_w_up, m_w_down, v_norm_mix_pre, v_norm_mix_post, v_norm_mlp_pre, v_norm_mlp_post, v_w_in_ab, v_w_spatial, v_b_spatial, v_conv_w, v_w_out_ab, v_w_qkv, v_w_o, v_w_up, v_w_down):
    depth = norm_mix_pre.shape[0]
    seq, dm = x.shape[1], x.shape[2]
    h0 = x.reshape(seq, dm)
    target = loss_target.reshape(seq, dm)
    ax, ay, ac = lax.axis_index("x"), lax.axis_index("y"), lax.axis_index("c")
    my_block = 4 * ax + 2 * ay + ac
    block = jnp.reshape(my_block, (1,)).astype(jnp.int32)

    half = HEAD // 2
    inv_freq = ROPE_THETA ** (-jnp.arange(half, dtype=F32) * 2.0 / HEAD)
    ang = jnp.arange(seq, dtype=jnp.int32).astype(F32)[:, None] * inv_freq[None, :]
    ang = ang.reshape(seq // SEGS, SEGS, half).transpose(1, 0, 2).reshape(seq, half)
    cosf = jnp.concatenate([jnp.cos(ang), jnp.cos(ang)], axis=-1)
    sins = jnp.concatenate([-jnp.sin(ang), jnp.sin(ang)], axis=-1)
    band_bias = _band_bias()

    big = {"w_in_ab": w_in_ab, "w_out_ab": w_out_ab, "w_qkv": w_qkv, "w_o": w_o, "w_up": w_up, "w_down": w_down}
    use_order = []
    for l in range(depth):
        use_order += [("w_in_ab", l // 2), ("w_out_ab", l // 2)] if l % 2 == 0 else [("w_qkv", l // 2), ("w_o", l // 2)]
        use_order += [("w_up", l), ("w_down", l)]
    n_even = w_in_ab.shape[0]
    cw_rows = jnp.pad(conv_w.reshape(n_even * CONV_TAPS, conv_w.shape[2]), ((0, HALO - (n_even * CONV_TAPS) % HALO), (0, 0)))
    cw_gathered = _all_gather("ag_conv", [cw_rows])[0]
    first = [k for k in use_order if k in (("w_in_ab", 0), ("w_out_ab", 0), ("w_up", 0), ("w_down", 0))]
    rest = [k for k in use_order if k not in first]
    lands_a, sems_a = _ag_start("ag_start_first", [_cast_fill(f"cast_{nm}_{l}", big[nm], l, block) for nm, l in first],
                                cw_gathered)
    lands_b, sems_b = _ag_start("ag_start_rest", [_cast_fill(f"cast_{nm}_{l}", big[nm], l, block) for nm, l in rest],
                                lands_a[0])
    lands = dict(zip(first + rest, list(lands_a) + list(lands_b)))
    ag_sems = dict(zip(first + rest, list(sems_a) + list(sems_b)))
    passed_on, wg = [], {}

    def weight(key, after):
        pins = []
        if key not in wg:
            upto = min(use_order.index(key) + 1, len(use_order) - 1)
            for k in use_order[len(passed_on):upto + 1]:
                lands[k] = _ag_mid(f"ag_mid_{k[0]}_{k[1]}", lands[k], ag_sems[k], after)
                passed_on.append(k)
                if k != key and use_order.index(k) >= AG_PIN_FROM:
                    pins.append(lands[k])
            wg[key] = _ag_wait(f"ag_wait_{key[0]}_{key[1]}", lands[key], ag_sems[key], after)
        return wg[key], pins

    cw_all = cw_gathered[:, :n_even * CONV_TAPS].reshape(NDEV, n_even, CONV_TAPS, -1)
    cw_all = jnp.transpose(cw_all, (1, 2, 0, 3)).reshape(n_even, CONV_TAPS, -1)
    cw_full = [jnp.pad(cw_all[e], ((0, HALO - CONV_TAPS), (0, 0))) for e in range(n_even)]

    def rows_nat(blk):
        return blk.reshape(blk.shape[0] * blk.shape[1], blk.shape[2])

    saved = []
    hn = _norm_fwd("norm_first", h0, g_pre=norm_mix_pre[0][None])[0]
    h = h0
    for l in range(depth):
        s = {"h_in": h, "hn1": hn}
        if l % 2 == 0:
            e = l // 2
            w_, pins = weight(("w_in_ab", e), hn)
            proj = _mm_nn_blk(f"fwd_in_{l}", hn, w_, after=pins)
            ab = _gate_fwd(f"gate_fwd_{l}", proj, w_spatial[e], b_spatial[e].T, cw_full[e])
            w_, pins = weight(("w_out_ab", e), ab)
            mix = _mm_nn(f"fwd_out_{l}", ab, rows_nat(w_), after=pins)
            s.update(proj=proj, ab=ab)
        else:
            o_ = l // 2
            w_, pins = weight(("w_qkv", o_), hn)
            qkv = _mm_nn_blk(f"fwd_qkv_{l}", hn, w_, after=pins)
            att, lse = _attn_fwd(f"attn_fwd_{l}", qkv, cosf, sins, band_bias)
            w_, pins = weight(("w_o", o_), att)
            mix = _mm_nn(f"fwd_o_{l}", att, rows_nat(w_), after=pins)
            s.update(qkv=qkv, att=att, lse=lse)
        h1, hn2 = _norm_fwd(f"norm_mid_{l}", h, mix, norm_mix_post[l][None], norm_mlp_pre[l][None], seg_z=l % 2 == 1)
        w_, pins = weight(("w_up", l), hn2)
        act = _mm_nn_blk(f"fwd_up_{l}", hn2, w_, relu2=True, after=pins)
        w_, pins = weight(("w_down", l), act)
        f = _mm_nn(f"fwd_down_{l}", act, rows_nat(w_), after=pins)
        s.update(mix=mix, h1=h1, hn2=hn2, act=act, f=f)
        if l + 1 < depth:
            h, hn = _norm_fwd(f"norm_end_{l}", h1, f, norm_mlp_post[l][None], norm_mix_pre[l + 1][None],
                              seg_y=(l + 1) % 2 == 1)
        else:
            h = _norm_fwd(f"norm_end_{l}", h1, f, norm_mlp_post[l][None])[0]
        saved.append(s)

    d_h, loss_row = _loss_grad("loss", h, target)
    rs = {}

    def scatter(key, g):
        rs[key] = _rs_start(f"rs_start_{key[0]}_{key[1]}", g.reshape(NDEV, -1, g.shape[-1]))

    dg ={nm: [None] * depth for nm in ("norm_mix_pre", "norm_mix_post", "norm_mlp_pre", "norm_mlp_post")}
    d_ws, d_bs, d_cw = [None] * n_even, [None] * n_even, [None] * n_even
    d_hn_next = None
    for l in reversed(range(depth)):
        s = saved[l]
        if l == depth - 1:
            d_f, dg["norm_mlp_post"][l] = _norm_bwd(f"nb_end_{l}", d_h, post=(s["f"], norm_mlp_post[l][None]))
        else:
            d_h, dg["norm_mix_pre"][l + 1], d_f, dg["norm_mlp_post"][l] = _norm_bwd(
                f"nb_end_{l}", d_h, pre=(d_hn_next, saved[l + 1]["h_in"], norm_mix_pre[l + 1][None]),
                post=(s["f"], norm_mlp_post[l][None]), seg_dy=(l + 1) % 2 == 1)
        wd = rows_nat(wg[("w_down", l)])
        d_up = _mm_nt_rows(f"bwd_down_{l}", d_f, wd, act=s["act"])
        scatter(("w_down", l), _mm_tn(f"gw_down_{l}", s["act"], d_f))
        scatter(("w_up", l), _mm_tn(f"gw_up_{l}", s["hn2"], d_up, nb=w_up.shape[2]))
        d_hn2 = _mm_nt_blk(f"bwd_up_{l}", d_up, wg[("w_up", l)], after=[rs[("w_down", l)][0], rs[("w_up", l)][0]])
        d_h, dg["norm_mlp_pre"][l], d_mix, dg["norm_mix_post"][l] = _norm_bwd(
            f"nb_mid_{l}", d_h, pre=(d_hn2, s["h1"], norm_mlp_pre[l][None]),
            post=(s["mix"], norm_mix_post[l][None]), seg_z=l % 2 == 1)
        if l % 2 == 0:
            e = l // 2
            wo = rows_nat(wg[("w_out_ab", e)])
            d_ab = _mm_nt_rows(f"bwd_out_{l}", d_mix, wo)
            scatter(("w_out_ab", e), _mm_tn(f"gw_out_{l}", s["ab"], d_mix))
            d_proj, d_ws[e], d_bs[e], d_cw[e] = _gate_bwd(
                f"gate_bwd_{l}", s["proj"], d_ab, w_spatial[e], b_spatial[e].T, cw_full[e])
            scatter(("w_in_ab", e), _mm_tn(f"gw_in_{l}", s["hn1"], d_proj, nb=w_in_ab.shape[2]))
            d_hn_next = _mm_nt_blk(f"bwd_in_{l}", d_proj, wg[("w_in_ab", e)],
                                   after=[rs[("w_out_ab", e)][0], rs[("w_in_ab", e)][0]])
        else:
            o_ = l // 2
            wo = rows_nat(wg[("w_o", o_)])
            d_att = _mm_nt_rows(f"bwd_o_{l}", d_mix, wo, out_dtype=F32)
            scatter(("w_o", o_), _mm_tn(f"gw_o_{l}", s["att"], d_mix))
            parts = _attn_bwd(f"attn_bwd_{l}", s["qkv"], cosf, sins, band_bias, d_att, s["att"], s["lse"])
            d_qkv = _attn_merge(f"attn_merge_{l}", *parts)
            scatter(("w_qkv", o_), _mm_tn(f"gw_qkv_{l}", s["hn1"], d_qkv, nb=w_qkv.shape[2]))
            d_hn_next = _mm_nt_blk(f"bwd_qkv_{l}", d_qkv, wg[("w_qkv", o_)],
                                   after=[rs[("w_o", o_)][0], rs[("w_qkv", o_)][0]])
    grad_x, dg["norm_mix_pre"][0] = _norm_bwd("nb_first", d_h, pre=(d_hn_next, h0, norm_mix_pre[0][None]))

    small_g = ([jnp.concatenate(dg[nm], axis=0) for nm in dg]
               + [jnp.stack(d_ws), jnp.stack(d_bs), jnp.stack([c[:CONV_TAPS] for c in d_cw]), loss_row])
    small_land, small_send, small_recv = _ag_direct_start("ag_small_start", _fill_slot("fill_small", _pack(small_g), block))

    moments = {"w_in_ab": (m_w_in_ab, v_w_in_ab), "w_out_ab": (m_w_out_ab, v_w_out_ab), "w_qkv": (m_w_qkv, v_w_qkv),
               "w_o": (m_w_o, v_w_o), "w_up": (m_w_up, v_w_up), "w_down": (m_w_down, v_w_down)}
    out_big = {}
    behind = small_land
    for nm in ("w_o", "w_qkv", "w_down", "w_up", "w_out_ab", "w_in_ab"):
        own, landed = [], []
        for l in range(big[nm].shape[0]):
            g, land = _wait_all(f"rs_wait_{nm}_{l}", *rs[(nm, l)], behind)
            own.append(g)
            landed.append(land)
        out_big[nm] = _adamw_layers(f"adamw_{nm}", own, landed, block, big[nm], moments[nm][0], moments[nm][1])
        behind = out_big[nm][0]

    summed = _sum_parts("sum_small", _ag_direct_wait("ag_small_wait", small_land, small_send, small_recv, behind))
    g_nmp, g_nmo, g_nlp, g_nlo, g_ws, g_bs, g_cw_all, loss_sum = _unpack(summed, small_g)
    loss = loss_sum[0, 0]
    cwb = conv_w.shape[2]
    g_cw = lax.dynamic_slice_in_dim(g_cw_all, my_block * cwb, cwb, axis=2)
    small_w = [norm_mix_pre, norm_mix_post, norm_mlp_pre, norm_mlp_post, w_spatial, b_spatial, conv_w]
    small_m = [m_norm_mix_pre, m_norm_mix_post, m_norm_mlp_pre, m_norm_mlp_post, m_w_spatial, m_b_spatial, m_conv_w]
    small_v = [v_norm_mix_pre, v_norm_mix_post, v_norm_mlp_pre, v_norm_mlp_post, v_w_spatial, v_b_spatial, v_conv_w]
    small_grad = [g_nmp, g_nmo, g_nlp, g_nlo, g_ws, g_bs, g_cw]
    upd = _adamw("adamw_small", _pack(small_grad)[None], _pack(small_w), _pack(small_m), _pack(small_v))
    sg, sd, sm, sv = [_unpack(u, small_w) for u in upd]

    def outs(i_small, i_big):
        return (i_small[0], i_small[1], i_small[2], i_small[3], i_big["w_in_ab"], i_small[4], i_small[5], i_small[6],
                i_big["w_out_ab"], i_big["w_qkv"], i_big["w_o"], i_big["w_up"], i_big["w_down"])

    pick = lambda i: {nm: out_big[nm][i] for nm in big}
    return (loss, grad_x.reshape(x.shape), *outs(sg, pick(0)), *outs(sd, pick(1)), *outs(sm, pick(2)),
            *outs(sv, pick(3)))
```

```python
import functools

import jax
import jax.numpy as jnp
from jax import lax
from jax.experimental import pallas as pl
from jax.experimental.pallas import tpu as pltpu

F32 = jnp.float32
BF16 = jnp.bfloat16
MESH = pl.DeviceIdType.MESH
ANY = pl.BlockSpec(memory_space=pl.ANY)
HBM = pl.BlockSpec(memory_space=pltpu.HBM)
SEM = pl.BlockSpec(memory_space=pltpu.SEMAPHORE)
EFFECT = pltpu.SideEffectType.DATAFLOW_SIDE_EFFECTING

NDEV = 8
NCHIP = 4
RMS_EPS = 1e-6
LN_EPS = 1e-5
CHUNK = 128
HEAD = 128
ATT_TILE = 2048
ATT_UNROLL = 16
DILATIONS = (1, 4, 16)
SEGS = 16
ROPE_THETA = 10000.0
CONV_TAPS = 3
HALO = 8
HALO_BF16 = 16
GELU_C = 0.7978845608028654
GELU_A = 0.044715
ADAM_LR, ADAM_B1, ADAM_B2, ADAM_EPS, ADAM_WD, ADAM_STEP = 0.001, 0.9, 0.999, 1e-08, 0.01, 10
VMEM_MM = 52 << 20
VMEM_EW = 40 << 20


def _cp(sem=None, vmem=VMEM_EW):
    if sem is None:
        return pltpu.CompilerParams(vmem_limit_bytes=vmem)
    return pltpu.CompilerParams(dimension_semantics=sem, vmem_limit_bytes=vmem)


def _tile(n, want):
    return want if n % want == 0 else n


def _all_gather(name, shards, after=()):
    n = len(shards)
    after = list(after)

    def body(*refs):
        ins, outs = refs[:n], refs[n + len(after):2 * n + len(after)]
        send_sems, recv_sems, local_sems = refs[2 * n + len(after):]
        x, y, c = lax.axis_index("x"), lax.axis_index("y"), lax.axis_index("c")
        me, sibling = (x, y, c), (x, y, 1 - c)
        chips = [(1 - x, y), (x, 1 - y), (1 - x, 1 - y)]

        def slot(p):
            return 4 * p[0] + 2 * p[1] + p[2]

        def copy(i, k, block, to, src=None):
            dst = outs[i].at[slot(block)]
            return pltpu.make_async_remote_copy(
                src_ref=dst if src is None else src, dst_ref=dst,
                send_sem=send_sems.at[i, k], recv_sem=recv_sems.at[i, k],
                device_id=to, device_id_type=MESH)

        mine = [pltpu.make_async_copy(ins[i], outs[i].at[slot(me)], local_sems.at[i]) for i in range(n)]
        for cp in mine:
            cp.start()
        first = []
        for i in range(n):
            first.append(copy(i, 0, me, sibling, src=ins[i]))
            for j, chip in enumerate(chips):
                first.append(copy(i, 1 + j, me, (*chip, c), src=ins[i]))
        for cp in first:
            cp.start()
        passed = []
        for j, chip in enumerate(chips):
            for i in range(n):
                copy(i, 1 + j, (*chip, c), me).wait_recv()
                fwd = copy(i, 4 + j, (*chip, c), sibling)
                fwd.start()
                passed.append(fwd)
        for i in range(n):
            copy(i, 0, sibling, me).wait_recv()
            for j, chip in enumerate(chips):
                copy(i, 4 + j, (*chip, 1 - c), me).wait_recv()
        for cp in first + passed:
            cp.wait_send()
        for cp in mine:
            cp.wait()

    return pl.pallas_call(
        body, name=name,
        out_shape=[jax.ShapeDtypeStruct((NDEV,) + s.shape, s.dtype) for s in shards],
        in_specs=[ANY] * (n + len(after)), out_specs=[ANY] * n,
        scratch_shapes=[pltpu.SemaphoreType.DMA((n, 7)), pltpu.SemaphoreType.DMA((n, 7)),
                        pltpu.SemaphoreType.DMA((n,))],
    )(*shards, *after)


def _peer(x, y, c, r):
    return (1 - x if r & 4 else x, 1 - y if r & 2 else y, 1 - c if r & 1 else c)


def _slot(p):
    return 4 * p[0] + 2 * p[1] + p[2]


def _cast_fill(name, w, layer, block):
    _, rows, cols = w.shape
    tr = _tile(rows, 256)

    def body(blk_ref, w_ref, o_ref):
        o_ref[...] = w_ref[...].astype(BF16)

    return pl.pallas_call(
        body, name=name,
        grid_spec=pltpu.PrefetchScalarGridSpec(
            num_scalar_prefetch=1, grid=(rows // tr,),
            in_specs=[pl.BlockSpec((None, tr, cols), lambda i, blk: (layer, i, 0))],
            out_specs=pl.BlockSpec((None, tr, cols), lambda i, blk: (blk[0], i, 0))),
        out_shape=jax.ShapeDtypeStruct((NDEV, rows, cols), BF16),
        compiler_params=_cp(("parallel",)))(block, w)


OTHER_CHIPS = (2, 4, 6)
AG_SEMS = 6
AG_PIN_FROM = 5


def _ag_start(name, lands, after):
    n = len(lands)

    def body(*refs):
        ins, sems = refs[:n], refs[n + 1:n + 1 + AG_SEMS * n]
        x, y, c = lax.axis_index("x"), lax.axis_index("y"), lax.axis_index("c")
        mine = _slot((x, y, c))
        for i in range(n):
            send_a, *recv_a, _, recv_b = sems[AG_SEMS * i:AG_SEMS * (i + 1)]
            block = ins[i].at[mine]
            pltpu.make_async_remote_copy(src_ref=block, dst_ref=block, send_sem=send_a, recv_sem=recv_b,
                                         device_id=_peer(x, y, c, 1), device_id_type=MESH).start()
            for k, r in enumerate(OTHER_CHIPS):
                pltpu.make_async_remote_copy(src_ref=block, dst_ref=block, send_sem=send_a, recv_sem=recv_a[k],
                                             device_id=_peer(x, y, c, r), device_id_type=MESH).start()

    outs = pl.pallas_call(
        body, name=name,
        out_shape=[pltpu.SemaphoreType.DMA(())] * (AG_SEMS * n) + [pltpu.HBM(a.shape, a.dtype) for a in lands],
        in_specs=[HBM] * n + [ANY], out_specs=[SEM] * (AG_SEMS * n) + [HBM] * n,
        input_output_aliases={i: AG_SEMS * n + i for i in range(n)},
        compiler_params=pltpu.CompilerParams(has_side_effects=EFFECT),
    )(*[pltpu.with_memory_space_constraint(a, pltpu.HBM) for a in lands], after)
    return outs[AG_SEMS * n:], [tuple(outs[AG_SEMS * i:AG_SEMS * (i + 1)]) for i in range(n)]


def _ag_mid(name, land, sems, after):
    _, *recv_a, send_b, recv_b = sems

    def body(land_ref, ra0, ra1, ra2, send_b_ref, recv_b_ref, after_ref, land_out):
        x, y, c = lax.axis_index("x"), lax.axis_index("y"), lax.axis_index("c")
        sibling = _peer(x, y, c, 1)
        for arrival, r in zip((ra0, ra1, ra2), OTHER_CHIPS):
            block = land_ref.at[_slot(_peer(x, y, c, r))]
            pltpu.make_async_remote_copy(src_ref=block, dst_ref=block, send_sem=send_b_ref, recv_sem=arrival,
                                         device_id=sibling, device_id_type=MESH).wait_recv()
            pltpu.make_async_remote_copy(src_ref=block, dst_ref=block, send_sem=send_b_ref, recv_sem=recv_b_ref,
                                         device_id=sibling, device_id_type=MESH).start()

    return pl.pallas_call(
        body, name=name, out_shape=pltpu.HBM(land.shape, land.dtype),
        in_specs=[HBM] + [SEM] * 5 + [ANY], out_specs=HBM, input_output_aliases={0: 0},
        compiler_params=pltpu.CompilerParams(has_side_effects=EFFECT),
    )(land, *recv_a, send_b, recv_b, after)


def _ag_wait(name, land, sems, after):
    send_a, _, _, _, send_b, recv_b = sems

    def body(land_ref, send_a_ref, send_b_ref, recv_b_ref, after_ref, land_out):
        x, y, c = lax.axis_index("x"), lax.axis_index("y"), lax.axis_index("c")
        sibling = _peer(x, y, c, 1)
        four = land_ref.at[pl.ds(0, 1 + len(OTHER_CHIPS))]
        three = land_ref.at[pl.ds(0, len(OTHER_CHIPS))]
        first = pltpu.make_async_remote_copy(src_ref=four, dst_ref=four, send_sem=send_a_ref, recv_sem=recv_b_ref,
                                             device_id=sibling, device_id_type=MESH)
        passed = pltpu.make_async_remote_copy(src_ref=three, dst_ref=three, send_sem=send_b_ref, recv_sem=recv_b_ref,
                                              device_id=sibling, device_id_type=MESH)
        first.wait_send()
        passed.wait_send()
        first.wait_recv()

    return pl.pallas_call(
        body, name=name, out_shape=pltpu.HBM(land.shape, land.dtype),
        in_specs=[HBM, SEM, SEM, SEM, ANY], out_specs=HBM, input_output_aliases={0: 0},
        compiler_params=pltpu.CompilerParams(has_side_effects=EFFECT),
    )(land, send_a, send_b, recv_b, after)


def _fill_slot(name, rows, block):
    r, c = rows.shape

    def body(blk_ref, i_ref, o_ref):
        o_ref[...] = i_ref[...]

    return pl.pallas_call(
        body, name=name,
        grid_spec=pltpu.PrefetchScalarGridSpec(
            num_scalar_prefetch=1, grid=(1,),
            in_specs=[pl.BlockSpec((r, c), lambda i, blk: (0, 0))],
            out_specs=pl.BlockSpec((None, r, c), lambda i, blk: (blk[0], 0, 0))),
        out_shape=jax.ShapeDtypeStruct((NDEV, r, c), rows.dtype),
        compiler_params=_cp(("arbitrary",)))(block, rows)


def _ag_direct_start(name, land):
    def body(land_ref, send, recv, land_out):
        x, y, c = lax.axis_index("x"), lax.axis_index("y"), lax.axis_index("c")
        block = land_ref.at[_slot((x, y, c))]
        for r in range(1, NDEV):
            pltpu.make_async_remote_copy(src_ref=block, dst_ref=block, send_sem=send, recv_sem=recv,
                                         device_id=_peer(x, y, c, r), device_id_type=MESH).start()

    send, recv, land_thru = pl.pallas_call(
        body, name=name,
        out_shape=[pltpu.SemaphoreType.DMA(()), pltpu.SemaphoreType.DMA(()), pltpu.HBM(land.shape, land.dtype)],
        in_specs=[HBM], out_specs=[SEM, SEM, HBM], input_output_aliases={0: 2},
        compiler_params=pltpu.CompilerParams(has_side_effects=EFFECT),
    )(pltpu.with_memory_space_constraint(land, pltpu.HBM))
    return land_thru, send, recv


def _ag_direct_wait(name, land, send, recv, after):
    def body(land_ref, send_ref, recv_ref, after_ref, land_out):
        x, y, c = lax.axis_index("x"), lax.axis_index("y"), lax.axis_index("c")
        seven = land_ref.at[pl.ds(0, NDEV - 1)]
        copy = pltpu.make_async_remote_copy(src_ref=seven, dst_ref=seven, send_sem=send_ref, recv_sem=recv_ref,
                                            device_id=_peer(x, y, c, 1), device_id_type=MESH)
        copy.wait_send()
        copy.wait_recv()

    return pl.pallas_call(
        body, name=name, out_shape=pltpu.HBM(land.shape, land.dtype),
        in_specs=[HBM, SEM, SEM, ANY], out_specs=HBM, input_output_aliases={0: 0},
        compiler_params=pltpu.CompilerParams(has_side_effects=EFFECT),
    )(land, send, recv, after)


def _wait_all(name, src, land, send, recv, after):
    def body(src_ref, land_ref, send_ref, recv_ref, after_ref, src_out, land_out):
        x, y, c = lax.axis_index("x"), lax.axis_index("y"), lax.axis_index("c")
        seven = land_ref.at[pl.ds(0, NDEV - 1)]
        copy = pltpu.make_async_remote_copy(src_ref=seven, dst_ref=seven, send_sem=send_ref, recv_sem=recv_ref,
                                            device_id=_peer(x, y, c, 1), device_id_type=MESH)
        copy.wait_send()
        copy.wait_recv()

    return pl.pallas_call(
        body, name=name,
        out_shape=[pltpu.HBM(src.shape, src.dtype), pltpu.HBM(land.shape, land.dtype)],
        in_specs=[HBM, HBM, SEM, SEM, ANY], out_specs=[HBM, HBM],
        input_output_aliases={0: 0, 1: 1},
        compiler_params=pltpu.CompilerParams(has_side_effects=EFFECT),
    )(src, land, send, recv, after)


def _rs_start(name, grad):
    land = lax.empty((NDEV - 1,) + grad.shape[1:], grad.dtype)

    def body(g_ref, land_ref, send, recv, g_out, land_out):
        x, y, c = lax.axis_index("x"), lax.axis_index("y"), lax.axis_index("c")
        for r in range(1, NDEV):
            peer = _peer(x, y, c, r)
            pltpu.make_async_remote_copy(
                src_ref=g_ref.at[_slot(peer)], dst_ref=land_ref.at[r - 1], send_sem=send, recv_sem=recv,
                device_id=peer, device_id_type=MESH).start()

    send, recv, g_thru, land_thru = pl.pallas_call(
        body, name=name,
        out_shape=[pltpu.SemaphoreType.DMA(()), pltpu.SemaphoreType.DMA(()),
                   pltpu.HBM(grad.shape, grad.dtype), pltpu.HBM(land.shape, land.dtype)],
        in_specs=[HBM, HBM], out_specs=[SEM, SEM, HBM, HBM], input_output_aliases={0: 2, 1: 3},
        compiler_params=pltpu.CompilerParams(has_side_effects=EFFECT),
    )(pltpu.with_memory_space_constraint(grad, pltpu.HBM), pltpu.with_memory_space_constraint(land, pltpu.HBM))
    return g_thru, land_thru, send, recv


def _adam_math(w, g, m, v):
    m = ADAM_B1 * m + (1.0 - ADAM_B1) * g
    v = ADAM_B2 * v + (1.0 - ADAM_B2) * (g * g)
    m_hat = m / (1.0 - ADAM_B1 ** ADAM_STEP)
    v_hat = v / (1.0 - ADAM_B2 ** ADAM_STEP)
    delta = -ADAM_LR * (m_hat / (jnp.sqrt(v_hat) + ADAM_EPS) + ADAM_WD * w)
    return delta, m, v


def _adamw(name, parts, w, m, v):
    nparts, rows, cols = parts.shape
    tr = _tile(rows, 256)

    def body(p_ref, w_ref, m_ref, v_ref, g_out, d_out, m_out, v_out):
        g = p_ref[0].astype(F32)
        for k in range(1, nparts):
            g = g + p_ref[k].astype(F32)
        delta, mn, vn = _adam_math(w_ref[...], g, m_ref[...], v_ref[...])
        g_out[...] = g
        d_out[...] = delta
        m_out[...] = mn
        v_out[...] = vn

    row = pl.BlockSpec((tr, cols), lambda i: (i, 0))
    return pl.pallas_call(
        body, name=name, grid=(rows // tr,),
        in_specs=[pl.BlockSpec((nparts, tr, cols), lambda i: (0, i, 0)), row, row, row],
        out_specs=[row] * 4,
        out_shape=[jax.ShapeDtypeStruct((rows, cols), F32)] * 4,
        compiler_params=_cp(("parallel",)),
    )(parts, w, m, v)


def _adamw_layers(name, grads, lands, block, w, m, v):
    layers, rows, cols = w.shape
    nland = lands[0].shape[0]
    tr = rows
    while tr % 2 == 0 and tr > 8 and nland * tr * cols * 2 > (2 << 20):
        tr //= 2

    def body(blk_ref, *refs):
        own_refs, land_refs = refs[:layers], refs[layers:2 * layers]
        w_ref, m_ref, v_ref, g_out, d_out, m_out, v_out = refs[2 * layers:]
        layer = pl.program_id(0)
        for k in range(layers):
            @pl.when(layer == k)
            def _(k=k):
                g = own_refs[k][...].astype(F32)
                for s in range(nland):
                    g = g + land_refs[k][s].astype(F32)
                delta, mn, vn = _adam_math(w_ref[...], g, m_ref[...], v_ref[...])
                g_out[...] = g
                d_out[...] = delta
                m_out[...] = mn
                v_out[...] = vn

    def own_spec(k):
        return pl.BlockSpec((None, tr, cols), lambda l, i, blk: (blk[0], jnp.where(l == k, i, 0), 0))

    def land_spec(k):
        return pl.BlockSpec((nland, tr, cols), lambda l, i, blk: (0, jnp.where(l == k, i, 0), 0))

    row = pl.BlockSpec((None, tr, cols), lambda l, i, blk: (l, i, 0))
    return pl.pallas_call(
        body, name=name,
        grid_spec=pltpu.PrefetchScalarGridSpec(
            num_scalar_prefetch=1, grid=(layers, rows // tr),
            in_specs=[own_spec(k) for k in range(layers)] + [land_spec(k) for k in range(layers)] + [row, row, row],
            out_specs=[row] * 4),
        out_shape=[jax.ShapeDtypeStruct((layers, rows, cols), F32)] * 4,
        compiler_params=_cp(("arbitrary", "arbitrary")),
    )(block, *grads, *lands, w, m, v)


LANES = 128


def _seg_scratch(rows, d):
    return pltpu.VMEM((d // LANES, rows, LANES), F32)


def _to_segments(vals, scratch, out_ref):
    per = scratch.shape[1] // SEGS
    for c in range(scratch.shape[0]):
        cols = slice(c * LANES, (c + 1) * LANES)
        scratch[c] = vals[:, cols]
        for s in range(SEGS):
            out_ref[s, :, cols] = scratch.at[c][pl.ds(s, per, stride=SEGS), :].astype(out_ref.dtype)


def _from_segments(in_ref, scratch):
    per = scratch.shape[1] // SEGS
    for c in range(scratch.shape[0]):
        for s in range(SEGS):
            scratch.at[c][pl.ds(s, per, stride=SEGS), :] = in_ref[s, :, c * LANES:(c + 1) * LANES].astype(F32)
    return jnp.concatenate([scratch[c] for c in range(scratch.shape[0])], axis=1)


def _seg_view(a):
    return a.reshape(SEGS, a.shape[0] // SEGS, a.shape[1])


def _norm_fwd(name, h, z=None, g_post=None, g_pre=None, seg_z=False, seg_y=False):
    rows, d = h.shape
    tm = _tile(rows, 256)
    has_post, has_pre = z is not None, g_pre is not None
    nscratch = int(seg_z) + int(seg_y)

    def body(*refs):
        scratch = list(refs[len(refs) - nscratch:])
        it = iter(refs)
        hv = next(it)[...]
        if has_post:
            z_ref = next(it)
            zv = _from_segments(z_ref, scratch.pop(0)) if seg_z else z_ref[...].astype(F32)
            gp = next(it)[...]
        if has_pre:
            gq = next(it)[...]
        if has_post:
            r = lax.rsqrt(jnp.mean(zv * zv, axis=-1, keepdims=True) + RMS_EPS)
            hv = hv + (zv * r) * gp
            next(it)[...] = hv
        if has_pre:
            r = lax.rsqrt(jnp.mean(hv * hv, axis=-1, keepdims=True) + RMS_EPS)
            y = (hv * r) * gq
            if seg_y:
                _to_segments(y, scratch.pop(0), next(it))
            else:
                next(it)[...] = y.astype(BF16)

    row = pl.BlockSpec((tm, d), lambda i: (i, 0))
    seg = pl.BlockSpec((SEGS, tm // SEGS, d), lambda i: (0, i, 0))
    vec = pl.BlockSpec((1, d), lambda i: (0, 0))
    ins, in_specs, out_shape, out_specs = [h], [row], [], []
    if has_post:
        ins += [_seg_view(z) if seg_z else z, g_post]
        in_specs += [seg if seg_z else row, vec]
        out_shape.append(jax.ShapeDtypeStruct((rows, d), F32))
        out_specs.append(row)
    if has_pre:
        ins.append(g_pre)
        in_specs.append(vec)
        out_shape.append(jax.ShapeDtypeStruct((SEGS, rows // SEGS, d) if seg_y else (rows, d), BF16))
        out_specs.append(seg if seg_y else row)
    outs = pl.pallas_call(body, name=name, grid=(rows // tm,), in_specs=in_specs, out_specs=out_specs,
                          out_shape=out_shape, scratch_shapes=[_seg_scratch(tm, d)] * nscratch,
                          compiler_params=_cp(("parallel",)))(*ins)
    if seg_y:
        outs = list(outs[:-1]) + [outs[-1].reshape(rows, d)]
    return outs


def _rms_bwd_rows(x, g, dy):
    r = lax.rsqrt(jnp.mean(x * x, axis=-1, keepdims=True) + RMS_EPS)
    xn = x * r
    dg = jnp.sum(dy * xn, axis=0, keepdims=True)
    dxn = dy * g
    dx = r * (dxn - xn * jnp.mean(dxn * xn, axis=-1, keepdims=True))
    return dx, dg


def _norm_bwd(name, d_out, pre=None, post=None, seg_dy=False, seg_z=False):
    rows, d = d_out.shape
    tm = _tile(rows, 256)
    has_pre, has_post = pre is not None, post is not None
    nscratch = int(seg_dy) + 2 * int(seg_z)

    def body(*refs):
        scratch = list(refs[len(refs) - nscratch:])
        it = iter(refs)
        dres = next(it)[...]
        if has_pre:
            dy_ref = next(it)
            dy = _from_segments(dy_ref, scratch.pop(0)) if seg_dy else dy_ref[...].astype(F32)
            xp, gq = next(it)[...], next(it)[...]
        if has_post:
            z_ref = next(it)
            zv = _from_segments(z_ref, scratch.pop(0)) if seg_z else z_ref[...].astype(F32)
            gp = next(it)[...]
        first = pl.program_id(0) == 0
        if has_pre:
            dx, dg = _rms_bwd_rows(xp, gq, dy)
            dres = dres + dx
            next(it)[...] = dres
            dg_ref = next(it)

            @pl.when(first)
            def _():
                dg_ref[...] = jnp.zeros_like(dg_ref)
            dg_ref[...] += dg
        if has_post:
            dz, dg2 = _rms_bwd_rows(zv, gp, dres)
            if seg_z:
                _to_segments(dz, scratch.pop(0), next(it))
            else:
                next(it)[...] = dz.astype(BF16)
            dg2_ref = next(it)

            @pl.when(first)
            def _():
                dg2_ref[...] = jnp.zeros_like(dg2_ref)
            dg2_ref[...] += dg2

    row = pl.BlockSpec((tm, d), lambda i: (i, 0))
    seg = pl.BlockSpec((SEGS, tm // SEGS, d), lambda i: (0, i, 0))
    vec = pl.BlockSpec((1, d), lambda i: (0, 0))
    ins, in_specs, out_shape, out_specs = [d_out], [row], [], []
    if has_pre:
        d_y, x_pre, g_pre = pre
        ins += [_seg_view(d_y) if seg_dy else d_y, x_pre, g_pre]
        in_specs += [seg if seg_dy else row, row, vec]
        out_shape += [jax.ShapeDtypeStruct((rows, d), F32), jax.ShapeDtypeStruct((1, d), F32)]
        out_specs += [row, vec]
    if has_post:
        z, g_post = post
        ins += [_seg_view(z) if seg_z else z, g_post]
        in_specs += [seg if seg_z else row, vec]
        out_shape += [jax.ShapeDtypeStruct((SEGS, rows // SEGS, d) if seg_z else (rows, d), BF16),
                      jax.ShapeDtypeStruct((1, d), F32)]
        out_specs += [seg if seg_z else row, vec]
    outs = pl.pallas_call(body, name=name, grid=(rows // tm,), in_specs=in_specs, out_specs=out_specs,
                          out_shape=out_shape, scratch_shapes=[_seg_scratch(tm, d)] * nscratch,
                          compiler_params=_cp(("arbitrary",)))(*ins)
    if seg_z:
        outs = list(outs)
        outs[-2] = outs[-2].reshape(rows, d)
    return outs


def _loss_grad(name, y, target):
    rows, d = y.shape
    tm = _tile(rows, 256)

    def body(y_ref, t_ref, dy_ref, loss_ref):
        err = y_ref[...] - t_ref[...]
        dy_ref[...] = err * (1.0 / d)

        @pl.when(pl.program_id(0) == 0)
        def _():
            loss_ref[...] = jnp.zeros_like(loss_ref)
        loss_ref[...] += jnp.full(loss_ref.shape, (0.5 / d) * jnp.sum(err * err), F32)

    row = pl.BlockSpec((tm, d), lambda i: (i, 0))
    return pl.pallas_call(
        body, name=name, grid=(rows // tm,), in_specs=[row, row],
        out_specs=[row, pl.BlockSpec((1, 128), lambda i: (0, 0))],
        out_shape=[jax.ShapeDtypeStruct((rows, d), F32), jax.ShapeDtypeStruct((1, 128), F32)],
        compiler_params=_cp(("arbitrary",)))(y, target)


NT_DIMS = (((1,), (1,)), ((), ()))
TN_DIMS = (((0,), (0,)), ((), ()))


MXU_WIDTH = 256


def _blocks_per_step(nb):
    return 1 if nb % MXU_WIDTH == 0 else 2


def _mm_nn_blk(name, a, wblk, relu2=False, after=()):
    m, k = a.shape
    nb = wblk.shape[2]
    tm = _tile(m, 1024)
    per = _blocks_per_step(nb)
    after = list(after)

    def body(a_ref, w_ref, *rest):
        w = w_ref[0] if per == 1 else jnp.concatenate([w_ref[t] for t in range(per)], axis=1)
        r = jnp.dot(a_ref[...], w, preferred_element_type=F32)
        if relu2:
            rr = jnp.maximum(r, 0.0)
            r = rr * rr
        rest[-1][...] = r.astype(BF16)

    return pl.pallas_call(
        body, name=name, grid=(NDEV // per, m // tm),
        in_specs=[pl.BlockSpec((tm, k), lambda d, i: (i, 0)), pl.BlockSpec((per, k, nb), lambda d, i: (d, 0, 0))]
        + [ANY] * len(after),
        out_specs=pl.BlockSpec((tm, per * nb), lambda d, i: (i, d)),
        out_shape=jax.ShapeDtypeStruct((m, NDEV * nb), BF16),
        compiler_params=_cp(("parallel", "parallel"), VMEM_MM))(a, wblk, *after)


def _accumulate(acc, o_ref, r, step, last):
    if acc is None:
        o_ref[...] = r.astype(o_ref.dtype)
        return

    @pl.when(step == 0)
    def _():
        acc[...] = r

    @pl.when(jnp.logical_and(step > 0, step < last))
    def _():
        acc[...] += r

    @pl.when(jnp.logical_and(step > 0, step == last))
    def _():
        o_ref[...] = (acc[...] + r).astype(o_ref.dtype)


def _mm_nn(name, a, w, after=()):
    m, kb = a.shape
    n = w.shape[1]
    one_step = kb <= 2048
    tm = _tile(m, 512 if one_step else 1024)
    tk = kb if one_step else _tile(kb, 4096)
    tn = n if one_step else _tile(n, 1024)
    steps = kb // tk
    after = list(after)

    def body(a_ref, w_ref, *rest):
        o_ref, scratch = rest[len(after)], rest[len(after) + 1:]
        r = jnp.dot(a_ref[...], w_ref[...], preferred_element_type=F32)
        _accumulate(scratch[0] if scratch else None, o_ref, r, pl.program_id(2), steps - 1)

    return pl.pallas_call(
        body, name=name, grid=(m // tm, n // tn, steps),
        in_specs=[pl.BlockSpec((tm, tk), lambda i, j, s: (i, s)), pl.BlockSpec((tk, tn), lambda i, j, s: (s, j))]
        + [ANY] * len(after),
        out_specs=pl.BlockSpec((tm, tn), lambda i, j, s: (i, j)),
        out_shape=jax.ShapeDtypeStruct((m, n), BF16),
        scratch_shapes=[pltpu.VMEM((tm, tn), F32)] if steps > 1 else [],
        compiler_params=_cp(("parallel", "parallel", "arbitrary"), VMEM_MM))(a, w, *after)


def _mm_nt_rows(name, dy, w, act=None, out_dtype=BF16):
    m, n = dy.shape
    kw = w.shape[0]
    tm, tkw = _tile(m, 1024), _tile(kw, 1024)

    def body(dy_ref, w_ref, *rest):
        r = lax.dot_general(dy_ref[...], w_ref[...], NT_DIMS, preferred_element_type=F32)
        if act is None:
            rest[0][...] = r.astype(out_dtype)
        else:
            rest[1][...] = (r * (2.0 * jnp.sqrt(rest[0][...].astype(F32)))).astype(BF16)

    ins = [dy, w]
    in_specs = [pl.BlockSpec((tm, n), lambda j, i: (i, 0)), pl.BlockSpec((tkw, n), lambda j, i: (j, 0))]
    if act is not None:
        ins.append(act)
        in_specs.append(pl.BlockSpec((tm, tkw), lambda j, i: (i, j)))
    return pl.pallas_call(
        body, name=name, grid=(kw // tkw, m // tm), in_specs=in_specs,
        out_specs=pl.BlockSpec((tm, tkw), lambda j, i: (i, j)),
        out_shape=jax.ShapeDtypeStruct((m, kw), out_dtype if act is None else BF16),
        compiler_params=_cp(("parallel", "parallel"), VMEM_MM))(*ins)


def _mm_nt_blk(name, dy, wblk, after=None):
    m = dy.shape[0]
    _, kw, nb = wblk.shape
    tm, tkw, per = _tile(m, 1024), _tile(kw, 1024), 4

    extra = list(after or ())

    def body(dy_ref, w_ref, *rest):
        o_ref, acc = rest[len(extra):]
        if nb % MXU_WIDTH == 0:
            r = lax.dot_general(dy_ref[:, :nb], w_ref[0], NT_DIMS, preferred_element_type=F32)
            for t in range(1, per):
                r = r + lax.dot_general(dy_ref[:, t * nb:(t + 1) * nb], w_ref[t], NT_DIMS, preferred_element_type=F32)
        else:
            w = jnp.concatenate([w_ref[t] for t in range(per)], axis=1)
            r = lax.dot_general(dy_ref[...], w, NT_DIMS, preferred_element_type=F32)
        _accumulate(acc, o_ref, r, pl.program_id(2), NDEV // per - 1)

    return pl.pallas_call(
        body, name=name, grid=(m // tm, kw // tkw, NDEV // per),
        in_specs=[pl.BlockSpec((tm, per * nb), lambda i, j, s: (i, s)),
                  pl.BlockSpec((per, tkw, nb), lambda i, j, s: (s, j, 0))] + [ANY] * len(extra),
        out_specs=pl.BlockSpec((tm, tkw), lambda i, j, s: (i, j)),
        out_shape=jax.ShapeDtypeStruct((m, kw), BF16),
        scratch_shapes=[pltpu.VMEM((tm, tkw), F32)],
        compiler_params=_cp(("parallel", "parallel", "arbitrary"), VMEM_MM))(dy, wblk, *extra)


def _mm_tn(name, x, dy, nb=None):
    t, mx = x.shape
    n = dy.shape[1]
    tmx = _tile(mx, 512)
    per = 1 if nb is None else _blocks_per_step(nb)
    tn = per * nb if nb is not None else _tile(n, 1024)

    def body(x_ref, dy_ref, o_ref):
        r = lax.dot_general(x_ref[...], dy_ref[...], TN_DIMS, preferred_element_type=F32).astype(BF16)
        if nb is None:
            o_ref[...] = r
        else:
            for b in range(per):
                o_ref[b] = r[:, b * nb:(b + 1) * nb]

    if nb is None:
        out_shape = jax.ShapeDtypeStruct((mx, n), BF16)
        out_spec = pl.BlockSpec((tmx, tn), lambda j, i: (i, j))
    else:
        out_shape = jax.ShapeDtypeStruct((NDEV, mx, nb), BF16)
        out_spec = pl.BlockSpec((per, tmx, nb), lambda j, i: (j, i, 0))
    return pl.pallas_call(
        body, name=name, grid=(n // tn, mx // tmx),
        in_specs=[pl.BlockSpec((t, tmx), lambda j, i: (0, i)), pl.BlockSpec((t, tn), lambda j, i: (0, j))],
        out_specs=out_spec, out_shape=out_shape,
        compiler_params=_cp(("parallel", "parallel"), VMEM_MM))(x, dy)


def _gelu(x):
    return 0.5 * x * (1.0 + jnp.tanh(GELU_C * (x + GELU_A * (x * x * x))))


def _gelu_grad(x):
    t = jnp.tanh(GELU_C * (x + GELU_A * (x * x * x)))
    return 0.5 * (1.0 + t) + 0.5 * x * (1.0 - t * t) * (GELU_C * (1.0 + 3.0 * GELU_A * (x * x)))


def _layernorm(a):
    mu = jnp.mean(a, axis=-1, keepdims=True)
    ac = a - mu
    rstd = lax.rsqrt(jnp.mean(ac * ac, axis=-1, keepdims=True) + LN_EPS)
    return ac * rstd, rstd


def _shift_rows(z, halo, k):
    zr = pltpu.roll(z, k, 0)
    hr = pltpu.roll(halo, k, 0)
    row = lax.broadcasted_iota(jnp.int32, hr.shape, 0)
    top = jnp.where(row < k, hr, zr[:HALO])
    return jnp.concatenate([top, zr[HALO:]], axis=0)


def _shift_rows_up(z, halo, k):
    rows = z.shape[0]
    zr = pltpu.roll(z, rows - k, 0)
    hr = pltpu.roll(halo, HALO - k, 0)
    row = lax.broadcasted_iota(jnp.int32, hr.shape, 0)
    bot = jnp.where(row >= HALO - k, hr, zr[rows - HALO:])
    return jnp.concatenate([zr[:rows - HALO], bot], axis=0)


def _causal_mask():
    t = lax.broadcasted_iota(jnp.int32, (CHUNK, CHUNK), 0)
    s = lax.broadcasted_iota(jnp.int32, (CHUNK, CHUNK), 1)
    return s <= t


def _gate_specs(tm, rows, width):
    per = tm // HALO_BF16
    last = rows // HALO_BF16 - 1
    cur = pl.BlockSpec((tm, width), lambda i: (i, 0))
    prev = pl.BlockSpec((HALO_BF16, width), lambda i: (jnp.maximum(i * per - 1, 0), 0))
    nxt = pl.BlockSpec((HALO_BF16, width), lambda i: (jnp.minimum((i + 1) * per, last), 0))
    return cur, prev, nxt


def _cols(ref, lo, hi):
    return ref[:, lo:hi].astype(F32)


def _halo_before(ref, lo, hi):
    return ref[:, lo:hi].astype(F32)[HALO_BF16 - HALO:]


def _halo_after(ref, lo, hi):
    return ref[:, lo:hi].astype(F32)[:HALO]


def _gate_fwd(name, proj, w_s, b_st, cw):
    rows, width = proj.shape
    w = width // 5
    groups = w // CHUNK
    tm = _tile(rows, 256)
    cur, prev, _ = _gate_specs(tm, rows, width)

    def body(p_ref, h_ref, ws_ref, b_ref, cw_ref, o_ref):
        mask = _causal_mask()
        au = _gelu(_cols(p_ref, 0, w))
        vn, _ = _layernorm(_gelu(_cols(p_ref, w, 2 * w)))
        vn = vn.astype(BF16)
        for g in range(groups):
            wc = jnp.where(mask, ws_ref[g], 0.0).astype(BF16)
            cols = slice(g * CHUNK, (g + 1) * CHUNK)
            for ch in range(tm // CHUNK):
                rws = slice(ch * CHUNK, (ch + 1) * CHUNK)
                mixed = jnp.dot(wc, vn[rws, cols], preferred_element_type=F32) + b_ref[:, g:g + 1]
                o_ref[rws, cols] = (au[rws, cols] * mixed).astype(BF16)
        z = _cols(p_ref, 3 * w, 4 * w) * _cols(p_ref, 4 * w, 5 * w)
        zh = _halo_before(h_ref, 3 * w, 4 * w) * _halo_before(h_ref, 4 * w, 5 * w)
        zh = jnp.where(pl.program_id(0) == 0, 0.0, zh)
        y = cw_ref[0:1, :] * _shift_rows(z, zh, 2) + cw_ref[1:2, :] * _shift_rows(z, zh, 1) + cw_ref[2:3, :] * z
        o_ref[:, w:2 * w] = (_cols(p_ref, 2 * w, 3 * w) * y).astype(BF16)

    full = lambda a: pl.BlockSpec(a.shape, lambda i: (0,) * a.ndim)
    return pl.pallas_call(
        body, name=name, grid=(rows // tm,),
        in_specs=[cur, prev, full(w_s), full(b_st), full(cw)],
        out_specs=pl.BlockSpec((tm, 2 * w), lambda i: (i, 0)),
        out_shape=jax.ShapeDtypeStruct((rows, 2 * w), BF16),
        compiler_params=_cp(("parallel",)))(proj, proj, w_s, b_st, cw)


def _gate_bwd(name, proj, d_ab, w_s, b_st, cw):
    rows, width = proj.shape
    w = width // 5
    groups = w // CHUNK
    tm = _tile(rows, 256)
    cur, prev, nxt = _gate_specs(tm, rows, width)
    dcur, _, dnxt = _gate_specs(tm, rows, 2 * w)

    def body(p_ref, ph_ref, pn_ref, d_ref, dn_ref, ws_ref, b_ref, cw_ref, o_ref, dws_ref, dbs_ref, dcw_ref):
        i = pl.program_id(0)

        @pl.when(i == 0)
        def _():
            dws_ref[...] = jnp.zeros_like(dws_ref)
            dbs_ref[...] = jnp.zeros_like(dbs_ref)
            dcw_ref[...] = jnp.zeros_like(dcw_ref)

        mask = _causal_mask()
        u, v = _cols(p_ref, 0, w), _cols(p_ref, w, 2 * w)
        au, av = _gelu(u), _gelu(v)
        vn, rstd = _layernorm(av)
        vnb = vn.astype(BF16)
        d_a = _cols(d_ref, 0, w)
        d_mixed = (d_a * au).astype(BF16)
        ones = jnp.ones((HALO, CHUNK), BF16)
        d_vn_cols = []
        d_au_cols = []
        for g in range(groups):
            wc = jnp.where(mask, ws_ref[g], 0.0).astype(BF16)
            cols = slice(g * CHUNK, (g + 1) * CHUNK)
            dw = jnp.zeros((CHUNK, CHUNK), F32)
            db = jnp.zeros((HALO, CHUNK), F32)
            d_vn_rows, d_au_rows = [], []
            for ch in range(tm // CHUNK):
                rws = slice(ch * CHUNK, (ch + 1) * CHUNK)
                mixed = jnp.dot(wc, vnb[rws, cols], preferred_element_type=F32) + b_ref[:, g:g + 1]
                d_au_rows.append(d_a[rws, cols] * mixed)
                dm = d_mixed[rws, cols]
                dw = dw + lax.dot_general(dm, vnb[rws, cols], NT_DIMS, preferred_element_type=F32)
                db = db + lax.dot_general(ones, dm, NT_DIMS, preferred_element_type=F32)
                d_vn_rows.append(lax.dot_general(wc, dm, TN_DIMS, preferred_element_type=F32))
            dws_ref[g] += jnp.where(mask, dw, 0.0)
            dbs_ref[g:g + 1, :] += db[0:1, :]
            d_vn_cols.append(jnp.concatenate(d_vn_rows, axis=0))
            d_au_cols.append(jnp.concatenate(d_au_rows, axis=0))
        d_vn = jnp.concatenate(d_vn_cols, axis=1)
        d_au = jnp.concatenate(d_au_cols, axis=1)
        d_av = rstd * (d_vn - jnp.mean(d_vn, axis=-1, keepdims=True)
                       - vn * jnp.mean(d_vn * vn, axis=-1, keepdims=True))
        o_ref[:, 0:w] = (d_au * _gelu_grad(u)).astype(BF16)
        o_ref[:, w:2 * w] = (d_av * _gelu_grad(v)).astype(BF16)

        gb, gc, bx = _cols(p_ref, 2 * w, 3 * w), _cols(p_ref, 3 * w, 4 * w), _cols(p_ref, 4 * w, 5 * w)
        z = gc * bx
        zh = jnp.where(i == 0, 0.0, _halo_before(ph_ref, 3 * w, 4 * w) * _halo_before(ph_ref, 4 * w, 5 * w))
        z1, z2 = _shift_rows(z, zh, 1), _shift_rows(z, zh, 2)
        d_b = _cols(d_ref, w, 2 * w)
        y = cw_ref[0:1, :] * z2 + cw_ref[1:2, :] * z1 + cw_ref[2:3, :] * z
        dy = d_b * gb
        dyn = jnp.where(i == pl.num_programs(0) - 1, 0.0,
                        _halo_after(dn_ref, w, 2 * w) * _halo_after(pn_ref, 2 * w, 3 * w))
        dz = (cw_ref[2:3, :] * dy + cw_ref[1:2, :] * _shift_rows_up(dy, dyn, 1)
              + cw_ref[0:1, :] * _shift_rows_up(dy, dyn, 2))
        dcw_ref[0:1, :] += jnp.sum(dy * z2, axis=0, keepdims=True)
        dcw_ref[1:2, :] += jnp.sum(dy * z1, axis=0, keepdims=True)
        dcw_ref[2:3, :] += jnp.sum(dy * z, axis=0, keepdims=True)
        o_ref[:, 2 * w:3 * w] = (d_b * y).astype(BF16)
        o_ref[:, 3 * w:4 * w] = (dz * bx).astype(BF16)
        o_ref[:, 4 * w:5 * w] = (dz * gc).astype(BF16)

    full = lambda a: pl.BlockSpec(a.shape, lambda i: (0,) * a.ndim)
    acc = lambda shape: pl.BlockSpec(shape, lambda i: (0,) * len(shape))
    return pl.pallas_call(
        body, name=name, grid=(rows // tm,),
        in_specs=[cur, prev, nxt, dcur, dnxt, full(w_s), full(b_st), full(cw)],
        out_specs=[pl.BlockSpec((tm, width), lambda i: (i, 0)), acc((groups, CHUNK, CHUNK)),
                   acc((groups, CHUNK)), acc((HALO, w))],
        out_shape=[jax.ShapeDtypeStruct((rows, width), BF16), jax.ShapeDtypeStruct((groups, CHUNK, CHUNK), F32),
                   jax.ShapeDtypeStruct((groups, CHUNK), F32), jax.ShapeDtypeStruct((HALO, w), F32)],
        compiler_params=_cp(("arbitrary",), VMEM_MM))(proj, proj, proj, d_ab, d_ab, w_s, b_st, cw)


def _flat(x):
    return x.reshape(-1, x.shape[-1])


def _rope(t, cosf, sins):
    t2 = _flat(t)
    return (t2 * _flat(cosf) + pltpu.roll(t2, HEAD // 2, 1) * _flat(sins)).reshape(t.shape)


def _rope_bwd(dt, cosf, sins):
    d2 = _flat(dt)
    return (d2 * _flat(cosf) + pltpu.roll(d2 * _flat(sins), HEAD // 2, 1)).reshape(dt.shape)


ATT_UNITS = ATT_TILE // CHUNK


def _attn_units(phases):
    for b, d in enumerate(DILATIONS):
        blocks = ATT_TILE // (CHUNK * d)
        for visit in phases:
            for r in range(d):
                if blocks <= ATT_UNROLL:
                    for j in range(blocks):
                        visit(b, d, r, j, r * blocks + j)
                else:
                    def step(jj, carry, b=b, d=d, r=r, visit=visit, blocks=blocks):
                        for u in range(ATT_UNROLL):
                            j = jj * ATT_UNROLL + u
                            visit(b, d, r, j, r * blocks + j)
                        return carry
                    lax.fori_loop(0, blocks // ATT_UNROLL, step, 0)


class _Unit:
    def __init__(self, d, r, j):
        self.segs = [r + d * k for k in range(SEGS // d)]
        self.w = CHUNK * d // SEGS
        q0 = j * self.w
        self.q0 = q0 if isinstance(q0, int) else pl.multiple_of(q0, HALO)
        k0 = CHUNK + (j - 1) * self.w
        self.k0 = k0 if isinstance(k0, int) else pl.multiple_of(k0, HALO)

    def queries(self, ref):
        return _chunks(ref, self.segs, self.q0, self.w)

    def keys(self, ref):
        return _chunks(ref, self.segs, self.k0, 2 * self.w)

    def put_queries(self, ref, val, add=False):
        _put_chunks(ref, self.segs, self.q0, self.w, val, add)

    def put_keys(self, ref, val, add=False):
        _put_chunks(ref, self.segs, self.k0, 2 * self.w, val, add)


def _chunks(ref, segs, start, size):
    parts = [ref[s, pl.ds(start, size), :] for s in segs]
    return parts[0] if len(parts) == 1 else jnp.concatenate(parts, axis=0)


def _put_chunks(ref, segs, start, size, val, add):
    for k, s in enumerate(segs):
        piece = val[k * size:(k + 1) * size]
        if add:
            ref[s, pl.ds(start, size), :] += piece
        else:
            ref[s, pl.ds(start, size), :] = piece


def _band_bias():
    qi = lax.broadcasted_iota(jnp.int32, (CHUNK, 2 * CHUNK), 0)
    ki = lax.broadcasted_iota(jnp.int32, (CHUNK, 2 * CHUNK), 1)
    tables = []
    for d in DILATIONS:
        nseg, w = SEGS // d, CHUNK * d // SEGS
        pos_q = nseg * (qi % w) + qi // w
        pos_k = nseg * (ki % (2 * w) - w) + ki // (2 * w)
        band = (pos_q >= pos_k) & (pos_q - pos_k <= CHUNK)
        tables += [jnp.where(band, 0.0, -jnp.inf), jnp.where(band & (pos_k >= 0), 0.0, -jnp.inf)]
    return jnp.stack(tables).astype(F32)


def _bias_spec():
    return pl.BlockSpec((2 * len(DILATIONS), CHUNK, 2 * CHUNK), lambda h, n: (0, 0, 0))


def _unit_bias(bias, b, n, j):
    if isinstance(j, int) and j != 0:
        return bias[2 * b]
    return bias[2 * b + jnp.where(jnp.logical_and(n == 0, j == 0), 1, 0)]


def _attn_in_specs(heads):
    blk = (SEGS, CHUNK, HEAD)
    prev = lambda n: jnp.maximum(n - 1, 0)
    return [
        pl.BlockSpec(blk, lambda h, n: (0, n, h)),
        pl.BlockSpec(blk, lambda h, n: (0, n, heads + h)),
        pl.BlockSpec(blk, lambda h, n: (0, prev(n), heads + h)),
        pl.BlockSpec(blk, lambda h, n: (0, n, 2 * heads + h)),
        pl.BlockSpec(blk, lambda h, n: (0, prev(n), 2 * heads + h)),
        pl.BlockSpec(blk, lambda h, n: (0, n, 0)),
        pl.BlockSpec(blk, lambda h, n: (0, n, 0)),
        pl.BlockSpec(blk, lambda h, n: (0, prev(n), 0)),
        pl.BlockSpec(blk, lambda h, n: (0, prev(n), 0)),
    ]


def _attn_load(q_ref, kc_ref, kp_ref, vc_ref, vp_ref, cc_ref, sc_ref, cp_ref, sp_ref, qr, kcat, vcat):
    qr[...] = _rope(q_ref[...].astype(F32), cc_ref[...], sc_ref[...]) * (HEAD ** -0.5)
    kcat[:, pl.ds(0, CHUNK), :] = _rope(kp_ref[...].astype(F32), cp_ref[...], sp_ref[...])
    kcat[:, pl.ds(CHUNK, CHUNK), :] = _rope(kc_ref[...].astype(F32), cc_ref[...], sc_ref[...])
    vcat[:, pl.ds(0, CHUNK), :] = vp_ref[...].astype(F32)
    vcat[:, pl.ds(CHUNK, CHUNK), :] = vc_ref[...].astype(F32)


def _attn_fwd(name, qkv, cosf, sins, bias):
    t = qkv.shape[0]
    heads = qkv.shape[1] // (3 * HEAD)
    nbr = len(DILATIONS)

    def body(q_ref, kc_ref, kp_ref, vc_ref, vp_ref, cc_ref, sc_ref, cp_ref, sp_ref, bias, o_ref, lse_ref,
             qr, kcat, vcat, obr, mbr, dbr, pn):
        n = pl.program_id(1)
        _attn_load(q_ref, kc_ref, kp_ref, vc_ref, vp_ref, cc_ref, sc_ref, cp_ref, sp_ref, qr, kcat, vcat)

        def probs(b, d, r, j, u):
            unit = _Unit(d, r, j)
            s = lax.dot_general(unit.queries(qr).astype(BF16), unit.keys(kcat).astype(BF16), NT_DIMS,
                                preferred_element_type=F32) + _unit_bias(bias, b, n, j)
            mx = jnp.max(s, axis=-1, keepdims=True)
            p = jnp.exp(s - mx)
            pn[u] = p.astype(BF16)
            unit.put_queries(mbr.at[b], jnp.broadcast_to(mx, (CHUNK, HEAD)))
            unit.put_queries(dbr.at[b], jnp.broadcast_to(jnp.sum(p, axis=-1, keepdims=True), (CHUNK, HEAD)))

        def values(b, d, r, j, u):
            unit = _Unit(d, r, j)
            unit.put_queries(obr.at[b], jnp.dot(pn[u], unit.keys(vcat).astype(BF16), preferred_element_type=F32))

        _attn_units([probs, values])
        ms = [mbr[b] for b in range(nbr)]
        top = functools.reduce(jnp.maximum, ms)
        ws = [jnp.exp(m - top) for m in ms]
        tot = functools.reduce(jnp.add, [ws[b] * dbr[b] for b in range(nbr)])
        inv = 1.0 / tot
        o = (ws[0] * inv) * obr[0]
        for b in range(1, nbr):
            o = o + (ws[b] * inv) * obr[b]
        o_ref[...] = o.astype(BF16)
        lse_ref[...] = top + jnp.log(tot)

    blk = (SEGS, CHUNK, HEAD)
    keys = pltpu.VMEM((SEGS, 2 * CHUNK, HEAD), F32)
    tile = pl.BlockSpec(blk, lambda h, n: (0, n, h))
    seg = t // SEGS
    qkv3, cos3, sin3 = _seg_view(qkv), _seg_view(cosf), _seg_view(sins)
    o, lse = pl.pallas_call(
        body, name=name, grid=(heads, t // ATT_TILE), in_specs=_attn_in_specs(heads) + [_bias_spec()],
        out_specs=[tile, tile],
        out_shape=[jax.ShapeDtypeStruct((SEGS, seg, heads * HEAD), BF16),
                   jax.ShapeDtypeStruct((SEGS, seg, heads * HEAD), F32)],
        scratch_shapes=[pltpu.VMEM(blk, F32), keys, keys,
                        pltpu.VMEM((nbr,) + blk, F32), pltpu.VMEM((nbr,) + blk, F32), pltpu.VMEM((nbr,) + blk, F32),
                        pltpu.VMEM((ATT_UNITS, CHUNK, 2 * CHUNK), BF16)],
        compiler_params=_cp(("parallel", "parallel"), VMEM_MM),
    )(qkv3, qkv3, qkv3, qkv3, qkv3, cos3, sin3, cos3, sin3, bias)
    return o.reshape(t, heads * HEAD), lse.reshape(t, heads * HEAD)


def _attn_bwd(name, qkv, cosf, sins, bias, d_o, o, lse):
    t = qkv.shape[0]
    heads = qkv.shape[1] // (3 * HEAD)
    scale = HEAD ** -0.5

    def body(q_ref, kc_ref, kp_ref, vc_ref, vp_ref, cc_ref, sc_ref, cp_ref, sp_ref, do_ref, o_ref, lse_ref, bias,
             dq_ref, dko_ref, dkp_ref, dvo_ref, dvp_ref, qr, kcat, vcat, dq_acc, dk_acc, dv_acc, delta, ps, dss):
        n = pl.program_id(1)
        _attn_load(q_ref, kc_ref, kp_ref, vc_ref, vp_ref, cc_ref, sc_ref, cp_ref, sp_ref, qr, kcat, vcat)
        dq_acc[...] = jnp.zeros_like(dq_acc)
        dk_acc[...] = jnp.zeros_like(dk_acc)
        dv_acc[...] = jnp.zeros_like(dv_acc)
        delta[...] = jnp.broadcast_to(
            jnp.sum(do_ref[...] * o_ref[...].astype(F32), axis=-1, keepdims=True), delta.shape)

        def probs(b, d, r, j, u):
            unit = _Unit(d, r, j)
            s = lax.dot_general(unit.queries(qr).astype(BF16), unit.keys(kcat).astype(BF16), NT_DIMS,
                                preferred_element_type=F32) + _unit_bias(bias, b, n, j)
            ps[u] = jnp.exp(s - unit.queries(lse_ref)[:, 0:1]).astype(BF16)

        def score_grads(b, d, r, j, u):
            unit = _Unit(d, r, j)
            dp = lax.dot_general(unit.queries(do_ref).astype(BF16), unit.keys(vcat).astype(BF16), NT_DIMS,
                                 preferred_element_type=F32)
            dss[u] = (ps[u].astype(F32) * (dp - unit.queries(delta)[:, 0:1])).astype(BF16)

        def input_grads(b, d, r, j, u):
            unit = _Unit(d, r, j)
            ds = dss[u]
            unit.put_queries(dq_acc, jnp.dot(ds, unit.keys(kcat).astype(BF16), preferred_element_type=F32), add=True)
            unit.put_keys(dk_acc, lax.dot_general(ds, unit.queries(qr).astype(BF16), TN_DIMS,
                                                  preferred_element_type=F32), add=True)
            unit.put_keys(dv_acc, lax.dot_general(ps[u], unit.queries(do_ref).astype(BF16), TN_DIMS,
                                                  preferred_element_type=F32), add=True)

        _attn_units([probs, score_grads, input_grads])
        dq_ref[...] = _rope_bwd(dq_acc[...] * scale, cc_ref[...], sc_ref[...]).astype(BF16)
        dkp_ref[...] = _rope_bwd(dk_acc[:, pl.ds(0, CHUNK), :], cp_ref[...], sp_ref[...]).astype(BF16)
        dko_ref[...] = _rope_bwd(dk_acc[:, pl.ds(CHUNK, CHUNK), :], cc_ref[...], sc_ref[...]).astype(BF16)
        dvp_ref[...] = dv_acc[:, pl.ds(0, CHUNK), :].astype(BF16)
        dvo_ref[...] = dv_acc[:, pl.ds(CHUNK, CHUNK), :].astype(BF16)

    blk = (SEGS, CHUNK, HEAD)
    tile = pl.BlockSpec(blk, lambda h, n: (0, n, h))
    big = pltpu.VMEM((SEGS, 2 * CHUNK, HEAD), F32)
    seg = t // SEGS
    qkv3, cos3, sin3 = _seg_view(qkv), _seg_view(cosf), _seg_view(sins)
    return pl.pallas_call(
        body, name=name, grid=(heads, t // ATT_TILE),
        in_specs=_attn_in_specs(heads) + [tile, tile, tile, _bias_spec()],
        out_specs=[tile] * 5,
        out_shape=[jax.ShapeDtypeStruct((SEGS, seg, heads * HEAD), BF16)] * 5,
        scratch_shapes=[pltpu.VMEM(blk, F32), big, big, pltpu.VMEM(blk, F32), big, big, pltpu.VMEM(blk, F32),
                        pltpu.VMEM((ATT_UNITS, CHUNK, 2 * CHUNK), BF16), pltpu.VMEM((ATT_UNITS, CHUNK, 2 * CHUNK), BF16)],
        compiler_params=_cp(("parallel", "parallel"), 60 << 20),
    )(qkv3, qkv3, qkv3, qkv3, qkv3, cos3, sin3, cos3, sin3, _seg_view(d_o), _seg_view(o), _seg_view(lse), bias)


def _attn_merge(name, dq, dk_own, dk_prev, dv_own, dv_prev):
    _, seg, hd = dq.shape
    nt = seg // CHUNK
    tw = _tile(hd, 512)

    def body(dq_ref, dko_ref, dkn_ref, dvo_ref, dvn_ref, o_ref):
        last = pl.program_id(0) == nt - 1
        part = pl.program_id(1)

        @pl.when(part == 0)
        def _():
            o_ref[...] = dq_ref[...]

        @pl.when(part == 1)
        def _():
            o_ref[...] = (dko_ref[...].astype(F32) + jnp.where(last, 0.0, dkn_ref[...].astype(F32))).astype(BF16)

        @pl.when(part == 2)
        def _():
            o_ref[...] = (dvo_ref[...].astype(F32) + jnp.where(last, 0.0, dvn_ref[...].astype(F32))).astype(BF16)

    blk = (SEGS, CHUNK, tw)

    def own(part):
        return pl.BlockSpec(blk, lambda n, p, c: (0, jnp.where(p == part, n, 0), jnp.where(p == part, c, 0)))

    def nxt(part):
        return pl.BlockSpec(blk, lambda n, p, c: (0, jnp.where(p == part, jnp.minimum(n + 1, nt - 1), 0),
                                                  jnp.where(p == part, c, 0)))

    per = hd // tw
    return pl.pallas_call(
        body, name=name, grid=(nt, 3, per), in_specs=[own(0), own(1), nxt(1), own(2), nxt(2)],
        out_specs=pl.BlockSpec(blk, lambda n, p, c: (0, n, p * per + c)),
        out_shape=jax.ShapeDtypeStruct((SEGS, seg, 3 * hd), BF16),
        compiler_params=_cp(("parallel", "parallel", "parallel"), VMEM_MM),
    )(dq, dk_own, dk_prev, dv_own, dv_prev).reshape(SEGS * seg, 3 * hd)


def _sum_parts(name, parts):
    nparts, rows, cols = parts.shape
    tr = _tile(rows, 256)

    def body(p_ref, o_ref):
        s = p_ref[0]
        for k in range(1, nparts):
            s = s + p_ref[k]
        o_ref[...] = s

    return pl.pallas_call(
        body, name=name, grid=(rows // tr,),
        in_specs=[pl.BlockSpec((nparts, tr, cols), lambda i: (0, i, 0))],
        out_specs=pl.BlockSpec((tr, cols), lambda i: (i, 0)),
        out_shape=jax.ShapeDtypeStruct((rows, cols), F32),
        compiler_params=_cp(("parallel",)))(parts)


def _rows128(a, pad_to=8):
    flat = a.reshape(-1)
    rows = -(-flat.shape[0] // 128)
    rows = -(-rows // pad_to) * pad_to
    flat = jnp.pad(flat, (0, rows * 128 - flat.shape[0]))
    return flat.reshape(rows, 128)


def _pack(arrays):
    return jnp.concatenate([_rows128(a) for a in arrays], axis=0)


def _unpack(packed, like):
    out, at = [], 0
    for a in like:
        size = 1
        for s in a.shape:
            size *= s
        rows = -(-(-(-size // 128)) // 8) * 8
        out.append(packed[at:at + rows].reshape(-1)[:size].reshape(a.shape))
        at += rows
    return out


def kernel(x, norm_mix_pre, norm_mix_post, norm_mlp_pre, norm_mlp_post, w_in_ab, w_spatial, b_spatial, conv_w, w_out_ab, w_qkv, w_o, w_up, w_down, loss_target, m_norm_mix_pre, m_norm_mix_post, m_norm_mlp_pre, m_norm_mlp_post, m_w_in_ab, m_w_spatial, m_b_spatial, m_conv_w, m_w_out_ab, m_w_qkv, m_w_o, m_w_up, m_w_down, v_norm_mix_pre, v_norm_mix_post, v_norm_mlp_pre, v_norm_mlp_post, v_w_in_ab, v_w_spatial, v_b_spatial, v_conv_w, v_w_out_ab, v_w_qkv, v_w_o, v_w_up, v_w_down):
    depth = norm_mix_pre.shape[0]
    seq, dm = x.shape[1], x.shape[2]
    h0 = x.reshape(seq, dm)
    target = loss_target.reshape(seq, dm)
    ax, ay, ac = lax.axis_index("x"), lax.axis_index("y"), lax.axis_index("c")
    my_block = 4 * ax + 2 * ay + ac
    block = jnp.reshape(my_block, (1,)).astype(jnp.int32)

    half = HEAD // 2
    inv_freq = ROPE_THETA ** (-jnp.arange(half, dtype=F32) * 2.0 / HEAD)
    ang = jnp.arange(seq, dtype=jnp.int32).astype(F32)[:, None] * inv_freq[None, :]
    ang = ang.reshape(seq // SEGS, SEGS, half).transpose(1, 0, 2).reshape(seq, half)
    cosf = jnp.concatenate([jnp.cos(ang), jnp.cos(ang)], axis=-1)
    sins = jnp.concatenate([-jnp.sin(ang), jnp.sin(ang)], axis=-1)
    band_bias = _band_bias()

    big = {"w_in_ab": w_in_ab, "w_out_ab": w_out_ab, "w_qkv": w_qkv, "w_o": w_o, "w_up": w_up, "w_down": w_down}
    use_order = []
    for l in range(depth):
        use_order += [("w_in_ab", l // 2), ("w_out_ab", l // 2)] if l % 2 == 0 else [("w_qkv", l // 2), ("w_o", l // 2)]
        use_order += [("w_up", l), ("w_down", l)]
    n_even = w_in_ab.shape[0]
    cw_rows = jnp.pad(conv_w.reshape(n_even * CONV_TAPS, conv_w.shape[2]), ((0, HALO - (n_even * CONV_TAPS) % HALO), (0, 0)))
    cw_gathered = _all_gather("ag_conv", [cw_rows])[0]
    first = [k for k in use_order if k in (("w_in_ab", 0), ("w_out_ab", 0), ("w_up", 0), ("w_down", 0))]
    rest = [k for k in use_order if k not in first]
    lands_a, sems_a = _ag_start("ag_start_first", [_cast_fill(f"cast_{nm}_{l}", big[nm], l, block) for nm, l in first],
                                cw_gathered)
    lands_b, sems_b = _ag_start("ag_start_rest", [_cast_fill(f"cast_{nm}_{l}", big[nm], l, block) for nm, l in rest],
                                lands_a[0])
    lands = dict(zip(first + rest, list(lands_a) + list(lands_b)))
    ag_sems = dict(zip(first + rest, list(sems_a) + list(sems_b)))
    passed_on, wg = [], {}

    def weight(key, after):
        pins = []
        if key not in wg:
            upto = min(use_order.index(key) + 1, len(use_order) - 1)
            for k in use_order[len(passed_on):upto + 1]:
                lands[k] = _ag_mid(f"ag_mid_{k[0]}_{k[1]}", lands[k], ag_sems[k], after)
                passed_on.append(k)
                if k != key and use_order.index(k) >= AG_PIN_FROM:
                    pins.append(lands[k])
            wg[key] = _ag_wait(f"ag_wait_{key[0]}_{key[1]}", lands[key], ag_sems[key], after)
        return wg[key], pins

    cw_all = cw_gathered[:, :n_even * CONV_TAPS].reshape(NDEV, n_even, CONV_TAPS, -1)
    cw_all = jnp.transpose(cw_all, (1, 2, 0, 3)).reshape(n_even, CONV_TAPS, -1)
    cw_full = [jnp.pad(cw_all[e], ((0, HALO - CONV_TAPS), (0, 0))) for e in range(n_even)]

    def rows_nat(blk):
        return blk.reshape(blk.shape[0] * blk.shape[1], blk.shape[2])

    saved = []
    hn = _norm_fwd("norm_first", h0, g_pre=norm_mix_pre[0][None])[0]
    h = h0
    for l in range(depth):
        s = {"h_in": h, "hn1": hn}
        if l % 2 == 0:
            e = l // 2
            w_, pins = weight(("w_in_ab", e), hn)
            proj = _mm_nn_blk(f"fwd_in_{l}", hn, w_, after=pins)
            ab = _gate_fwd(f"gate_fwd_{l}", proj, w_spatial[e], b_spatial[e].T, cw_full[e])
            w_, pins = weight(("w_out_ab", e), ab)
            mix = _mm_nn(f"fwd_out_{l}", ab, rows_nat(w_), after=pins)
            s.update(proj=proj, ab=ab)
        else:
            o_ = l // 2
            w_, pins = weight(("w_qkv", o_), hn)
            qkv = _mm_nn_blk(f"fwd_qkv_{l}", hn, w_, after=pins)
            att, lse = _attn_fwd(f"attn_fwd_{l}", qkv, cosf, sins, band_bias)
            w_, pins = weight(("w_o", o_), att)
            mix = _mm_nn(f"fwd_o_{l}", att, rows_nat(w_), after=pins)
            s.update(qkv=qkv, att=att, lse=lse)
        h1, hn2 = _norm_fwd(f"norm_mid_{l}", h, mix, norm_mix_post[l][None], norm_mlp_pre[l][None], seg_z=l % 2 == 1)
        w_, pins = weight(("w_up", l), hn2)
        act = _mm_nn_blk(f"fwd_up_{l}", hn2, w_, relu2=True, after=pins)
        w_, pins = weight(("w_down", l), act)
        f = _mm_nn(f"fwd_down_{l}", act, rows_nat(w_), after=pins)
        s.update(mix=mix, h1=h1, hn2=hn2, act=act, f=f)
        if l + 1 < depth:
            h, hn = _norm_fwd(f"norm_end_{l}", h1, f, norm_mlp_post[l][None], norm_mix_pre[l + 1][None],
                              seg_y=(l + 1) % 2 == 1)
        else:
            h = _norm_fwd(f"norm_end_{l}", h1, f, norm_mlp_post[l][None])[0]
        saved.append(s)

    d_h, loss_row = _loss_grad("loss", h, target)
    rs = {}

    def scatter(key, g):
        rs[key] = _rs_start(f"rs_start_{key[0]}_{key[1]}", g.reshape(NDEV, -1, g.shape[-1]))

    dg ={nm: [None] * depth for nm in ("norm_mix_pre", "norm_mix_post", "norm_mlp_pre", "norm_mlp_post")}
    d_ws, d_bs, d_cw = [None] * n_even, [None] * n_even, [None] * n_even
    d_hn_next = None
    for l in reversed(range(depth)):
        s = saved[l]
        if l == depth - 1:
            d_f, dg["norm_mlp_post"][l] = _norm_bwd(f"nb_end_{l}", d_h, post=(s["f"], norm_mlp_post[l][None]))
        else:
            d_h, dg["norm_mix_pre"][l + 1], d_f, dg["norm_mlp_post"][l] = _norm_bwd(
                f"nb_end_{l}", d_h, pre=(d_hn_next, saved[l + 1]["h_in"], norm_mix_pre[l + 1][None]),
                post=(s["f"], norm_mlp_post[l][None]), seg_dy=(l + 1) % 2 == 1)
        wd = rows_nat(wg[("w_down", l)])
        d_up = _mm_nt_rows(f"bwd_down_{l}", d_f, wd, act=s["act"])
        scatter(("w_down", l), _mm_tn(f"gw_down_{l}", s["act"], d_f))
        scatter(("w_up", l), _mm_tn(f"gw_up_{l}", s["hn2"], d_up, nb=w_up.shape[2]))
        d_hn2 = _mm_nt_blk(f"bwd_up_{l}", d_up, wg[("w_up", l)], after=[rs[("w_down", l)][0], rs[("w_up", l)][0]])
        d_h, dg["norm_mlp_pre"][l], d_mix, dg["norm_mix_post"][l] = _norm_bwd(
            f"nb_mid_{l}", d_h, pre=(d_hn2, s["h1"], norm_mlp_pre[l][None]),
            post=(s["mix"], norm_mix_post[l][None]), seg_z=l % 2 == 1)
        if l % 2 == 0:
            e = l // 2
            wo = rows_nat(wg[("w_out_ab", e)])
            d_ab = _mm_nt_rows(f"bwd_out_{l}", d_mix, wo)
            scatter(("w_out_ab", e), _mm_tn(f"gw_out_{l}", s["ab"], d_mix))
            d_proj, d_ws[e], d_bs[e], d_cw[e] = _gate_bwd(
                f"gate_bwd_{l}", s["proj"], d_ab, w_spatial[e], b_spatial[e].T, cw_full[e])
            scatter(("w_in_ab", e), _mm_tn(f"gw_in_{l}", s["hn1"], d_proj, nb=w_in_ab.shape[2]))
            d_hn_next = _mm_nt_blk(f"bwd_in_{l}", d_proj, wg[("w_in_ab", e)],
                                   after=[rs[("w_out_ab", e)][0], rs[("w_in_ab", e)][0]])
        else:
            o_ = l // 2
            wo = rows_nat(wg[("w_o", o_)])
            d_att = _mm_nt_rows(f"bwd_o_{l}", d_mix, wo, out_dtype=F32)
            scatter(("w_o", o_), _mm_tn(f"gw_o_{l}", s["att"], d_mix))
            parts = _attn_bwd(f"attn_bwd_{l}", s["qkv"], cosf, sins, band_bias, d_att, s["att"], s["lse"])
            d_qkv = _attn_merge(f"attn_merge_{l}", *parts)
            scatter(("w_qkv", o_), _mm_tn(f"gw_qkv_{l}", s["hn1"], d_qkv, nb=w_qkv.shape[2]))
            d_hn_next = _mm_nt_blk(f"bwd_qkv_{l}", d_qkv, wg[("w_qkv", o_)],
                                   after=[rs[("w_o", o_)][0], rs[("w_qkv", o_)][0]])
    grad_x, dg["norm_mix_pre"][0] = _norm_bwd("nb_first", d_h, pre=(d_hn_next, h0, norm_mix_pre[0][None]))

    small_g = ([jnp.concatenate(dg[nm], axis=0) for nm in dg]
               + [jnp.stack(d_ws), jnp.stack(d_bs), jnp.stack([c[:CONV_TAPS] for c in d_cw]), loss_row])
    small_land, small_send, small_recv = _ag_direct_start("ag_small_start", _fill_slot("fill_small", _pack(small_g), block))

    moments = {"w_in_ab": (m_w_in_ab, v_w_in_ab), "w_out_ab": (m_w_out_ab, v_w_out_ab), "w_qkv": (m_w_qkv, v_w_qkv),
               "w_o": (m_w_o, v_w_o), "w_up": (m_w_up, v_w_up), "w_down": (m_w_down, v_w_down)}
    out_big = {}
    behind = small_land
    for nm in ("w_o", "w_qkv", "w_down", "w_up", "w_out_ab", "w_in_ab"):
        own, landed = [], []
        for l in range(big[nm].shape[0]):
            g, land = _wait_all(f"rs_wait_{nm}_{l}", *rs[(nm, l)], behind)
            own.append(g)
            landed.append(land)
        out_big[nm] = _adamw_layers(f"adamw_{nm}", own, landed, block, big[nm], moments[nm][0], moments[nm][1])
        behind = out_big[nm][0]

    summed = _sum_parts("sum_small", _ag_direct_wait("ag_small_wait", small_land, small_send, small_recv, behind))
    g_nmp, g_nmo, g_nlp, g_nlo, g_ws, g_bs, g_cw_all, loss_sum = _unpack(summed, small_g)
    loss = loss_sum[0, 0]
    cwb = conv_w.shape[2]
    g_cw = lax.dynamic_slice_in_dim(g_cw_all, my_block * cwb, cwb, axis=2)
    small_w = [norm_mix_pre, norm_mix_post, norm_mlp_pre, norm_mlp_post, w_spatial, b_spatial, conv_w]
    small_m = [m_norm_mix_pre, m_norm_mix_post, m_norm_mlp_pre, m_norm_mlp_post, m_w_spatial, m_b_spatial, m_conv_w]
    small_v = [v_norm_mix_pre, v_norm_mix_post, v_norm_mlp_pre, v_norm_mlp_post, v_w_spatial, v_b_spatial, v_conv_w]
    small_grad = [g_nmp, g_nmo, g_nlp, g_nlo, g_ws, g_bs, g_cw]
    upd = _adamw("adamw_small", _pack(small_grad)[None], _pack(small_w), _pack(small_m), _pack(small_v))
    sg, sd, sm, sv = [_unpack(u, small_w) for u in upd]

    def outs(i_small, i_big):
        return (i_small[0], i_small[1], i_small[2], i_small[3], i_big["w_in_ab"], i_small[4], i_small[5], i_small[6],
                i_big["w_out_ab"], i_big["w_qkv"], i_big["w_o"], i_big["w_up"], i_big["w_down"])

    pick = lambda i: {nm: out_big[nm][i] for nm in big}
    return (loss, grad_x.reshape(x.shape), *outs(sg, pick(0)), *outs(sd, pick(1)), *outs(sm, pick(2)),
            *outs(sv, pick(3)))
```

```python
import functools

import jax
import jax.numpy as jnp
from jax import lax
from jax.experimental import pallas as pl
from jax.experimental.pallas import tpu as pltpu

F32 = jnp.float32
BF16 = jnp.bfloat16
MESH = pl.DeviceIdType.MESH
ANY = pl.BlockSpec(memory_space=pl.ANY)
HBM = pl.BlockSpec(memory_space=pltpu.HBM)
SEM = pl.BlockSpec(memory_space=pltpu.SEMAPHORE)
EFFECT = pltpu.SideEffectType.DATAFLOW_SIDE_EFFECTING

NDEV = 8
RMS_EPS = 1e-6
LN_EPS = 1e-5
CHUNK = 128
HEAD = 128
ATT_TILE = 2048
ATT_UNROLL = 16
DILATIONS = (1, 4, 16)
SEGS = 16
ROPE_THETA = 10000.0
CONV_TAPS = 3
HALO = 8
HALO_BF16 = 16
GELU_C = 0.7978845608028654
GELU_A = 0.044715
ADAM_LR, ADAM_B1, ADAM_B2, ADAM_EPS, ADAM_WD, ADAM_STEP = 0.001, 0.9, 0.999, 1e-08, 0.01, 10
VMEM_MM = 52 << 20
VMEM_EW = 40 << 20


def _cp(sem=None, vmem=VMEM_EW):
    if sem is None:
        return pltpu.CompilerParams(vmem_limit_bytes=vmem)
    return pltpu.CompilerParams(dimension_semantics=sem, vmem_limit_bytes=vmem)


def _tile(n, want):
    return want if n % want == 0 else n


def _all_gather(name, shards, after=()):
    n = len(shards)
    after = list(after)

    def body(*refs):
        ins, outs = refs[:n], refs[n + len(after):2 * n + len(after)]
        send_sems, recv_sems, local_sems = refs[2 * n + len(after):]
        x, y, c = lax.axis_index("x"), lax.axis_index("y"), lax.axis_index("c")
        me, sibling = (x, y, c), (x, y, 1 - c)
        chips = [(1 - x, y), (x, 1 - y), (1 - x, 1 - y)]

        def slot(p):
            return 4 * p[0] + 2 * p[1] + p[2]

        def copy(i, k, block, to, src=None):
            dst = outs[i].at[slot(block)]
            return pltpu.make_async_remote_copy(
                src_ref=dst if src is None else src, dst_ref=dst,
                send_sem=send_sems.at[i, k], recv_sem=recv_sems.at[i, k],
                device_id=to, device_id_type=MESH)

        mine = [pltpu.make_async_copy(ins[i], outs[i].at[slot(me)], local_sems.at[i]) for i in range(n)]
        for cp in mine:
            cp.start()
        first = []
        for i in range(n):
            first.append(copy(i, 0, me, sibling, src=ins[i]))
            for j, chip in enumerate(chips):
                first.append(copy(i, 1 + j, me, (*chip, c), src=ins[i]))
        for cp in first:
            cp.start()
        passed = []
        for j, chip in enumerate(chips):
            for i in range(n):
                copy(i, 1 + j, (*chip, c), me).wait_recv()
                fwd = copy(i, 4 + j, (*chip, c), sibling)
                fwd.start()
                passed.append(fwd)
        for i in range(n):
            copy(i, 0, sibling, me).wait_recv()
            for j, chip in enumerate(chips):
                copy(i, 4 + j, (*chip, 1 - c), me).wait_recv()
        for cp in first + passed:
            cp.wait_send()
        for cp in mine:
            cp.wait()

    return pl.pallas_call(
        body, name=name,
        out_shape=[jax.ShapeDtypeStruct((NDEV,) + s.shape, s.dtype) for s in shards],
        in_specs=[ANY] * (n + len(after)), out_specs=[ANY] * n,
        scratch_shapes=[pltpu.SemaphoreType.DMA((n, 7)), pltpu.SemaphoreType.DMA((n, 7)),
                        pltpu.SemaphoreType.DMA((n,))],
    )(*shards, *after)


def _peer(x, y, c, r):
    return (1 - x if r & 4 else x, 1 - y if r & 2 else y, 1 - c if r & 1 else c)


def _slot(p):
    return 4 * p[0] + 2 * p[1] + p[2]


def _cast_fill(name, w, layer, block):
    _, rows, cols = w.shape
    tr = _tile(rows, 256)

    def body(blk_ref, w_ref, o_ref):
        o_ref[...] = w_ref[...].astype(BF16)

    return pl.pallas_call(
        body, name=name,
        grid_spec=pltpu.PrefetchScalarGridSpec(
            num_scalar_prefetch=1, grid=(rows // tr,),
            in_specs=[pl.BlockSpec((None, tr, cols), lambda i, blk: (layer, i, 0))],
            out_specs=pl.BlockSpec((None, tr, cols), lambda i, blk: (blk[0], i, 0))),
        out_shape=jax.ShapeDtypeStruct((NDEV, rows, cols), BF16),
        compiler_params=_cp(("parallel",)))(block, w)


OTHER_CHIPS = (2, 4, 6)
AG_SEMS = 6
AG_PIN_FROM = 5


def _ag_start(name, lands, after):
    n = len(lands)

    def body(*refs):
        ins, sems = refs[:n], refs[n + 1:n + 1 + AG_SEMS * n]
        x, y, c = lax.axis_index("x"), lax.axis_index("y"), lax.axis_index("c")
        mine = _slot((x, y, c))
        for i in range(n):
            send_a, *recv_a, _, recv_b = sems[AG_SEMS * i:AG_SEMS * (i + 1)]
            block = ins[i].at[mine]
            pltpu.make_async_remote_copy(src_ref=block, dst_ref=block, send_sem=send_a, recv_sem=recv_b,
                                         device_id=_peer(x, y, c, 1), device_id_type=MESH).start()
            for k, r in enumerate(OTHER_CHIPS):
                pltpu.make_async_remote_copy(src_ref=block, dst_ref=block, send_sem=send_a, recv_sem=recv_a[k],
                                             device_id=_peer(x, y, c, r), device_id_type=MESH).start()

    outs = pl.pallas_call(
        body, name=name,
        out_shape=[pltpu.SemaphoreType.DMA(())] * (AG_SEMS * n) + [pltpu.HBM(a.shape, a.dtype) for a in lands],
        in_specs=[HBM] * n + [ANY], out_specs=[SEM] * (AG_SEMS * n) + [HBM] * n,
        input_output_aliases={i: AG_SEMS * n + i for i in range(n)},
        compiler_params=pltpu.CompilerParams(has_side_effects=EFFECT),
    )(*[pltpu.with_memory_space_constraint(a, pltpu.HBM) for a in lands], after)
    return outs[AG_SEMS * n:], [tuple(outs[AG_SEMS * i:AG_SEMS * (i + 1)]) for i in range(n)]


def _ag_mid(name, land, sems, after):
    _, *recv_a, send_b, recv_b = sems

    def body(land_ref, ra0, ra1, ra2, send_b_ref, recv_b_ref, after_ref, land_out):
        x, y, c = lax.axis_index("x"), lax.axis_index("y"), lax.axis_index("c")
        sibling = _peer(x, y, c, 1)
        for arrival, r in zip((ra0, ra1, ra2), OTHER_CHIPS):
            block = land_ref.at[_slot(_peer(x, y, c, r))]
            pltpu.make_async_remote_copy(src_ref=block, dst_ref=block, send_sem=send_b_ref, recv_sem=arrival,
                                         device_id=sibling, device_id_type=MESH).wait_recv()
            pltpu.make_async_remote_copy(src_ref=block, dst_ref=block, send_sem=send_b_ref, recv_sem=recv_b_ref,
                                         device_id=sibling, device_id_type=MESH).start()

    return pl.pallas_call(
        body, name=name, out_shape=pltpu.HBM(land.shape, land.dtype),
        in_specs=[HBM] + [SEM] * 5 + [ANY], out_specs=HBM, input_output_aliases={0: 0},
        compiler_params=pltpu.CompilerParams(has_side_effects=EFFECT),
    )(land, *recv_a, send_b, recv_b, after)


def _ag_wait(name, land, sems, after):
    send_a, _, _, _, send_b, recv_b = sems

    def body(land_ref, send_a_ref, send_b_ref, recv_b_ref, after_ref, land_out):
        x, y, c = lax.axis_index("x"), lax.axis_index("y"), lax.axis_index("c")
        sibling = _peer(x, y, c, 1)
        four = land_ref.at[pl.ds(0, 1 + len(OTHER_CHIPS))]
        three = land_ref.at[pl.ds(0, len(OTHER_CHIPS))]
        first = pltpu.make_async_remote_copy(src_ref=four, dst_ref=four, send_sem=send_a_ref, recv_sem=recv_b_ref,
                                             device_id=sibling, device_id_type=MESH)
        passed = pltpu.make_async_remote_copy(src_ref=three, dst_ref=three, send_sem=send_b_ref, recv_sem=recv_b_ref,
                                              device_id=sibling, device_id_type=MESH)
        first.wait_send()
        passed.wait_send()
        first.wait_recv()

    return pl.pallas_call(
        body, name=name, out_shape=pltpu.HBM(land.shape, land.dtype),
        in_specs=[HBM, SEM, SEM, SEM, ANY], out_specs=HBM, input_output_aliases={0: 0},
        compiler_params=pltpu.CompilerParams(has_side_effects=EFFECT),
    )(land, send_a, send_b, recv_b, after)


def _fill_slot(name, rows, block):
    r, c = rows.shape

    def body(blk_ref, i_ref, o_ref):
        o_ref[...] = i_ref[...]

    return pl.pallas_call(
        body, name=name,
        grid_spec=pltpu.PrefetchScalarGridSpec(
            num_scalar_prefetch=1, grid=(1,),
            in_specs=[pl.BlockSpec((r, c), lambda i, blk: (0, 0))],
            out_specs=pl.BlockSpec((None, r, c), lambda i, blk: (blk[0], 0, 0))),
        out_shape=jax.ShapeDtypeStruct((NDEV, r, c), rows.dtype),
        compiler_params=_cp(("arbitrary",)))(block, rows)


def _ag_direct_start(name, land):
    def body(land_ref, send, recv, land_out):
        x, y, c = lax.axis_index("x"), lax.axis_index("y"), lax.axis_index("c")
        block = land_ref.at[_slot((x, y, c))]
        for r in range(1, NDEV):
            pltpu.make_async_remote_copy(src_ref=block, dst_ref=block, send_sem=send, recv_sem=recv,
                                         device_id=_peer(x, y, c, r), device_id_type=MESH).start()

    send, recv, land_thru = pl.pallas_call(
        body, name=name,
        out_shape=[pltpu.SemaphoreType.DMA(()), pltpu.SemaphoreType.DMA(()), pltpu.HBM(land.shape, land.dtype)],
        in_specs=[HBM], out_specs=[SEM, SEM, HBM], input_output_aliases={0: 2},
        compiler_params=pltpu.CompilerParams(has_side_effects=EFFECT),
    )(pltpu.with_memory_space_constraint(land, pltpu.HBM))
    return land_thru, send, recv


def _ag_direct_wait(name, land, send, recv, after):
    def body(land_ref, send_ref, recv_ref, after_ref, land_out):
        x, y, c = lax.axis_index("x"), lax.axis_index("y"), lax.axis_index("c")
        seven = land_ref.at[pl.ds(0, NDEV - 1)]
        copy = pltpu.make_async_remote_copy(src_ref=seven, dst_ref=seven, send_sem=send_ref, recv_sem=recv_ref,
                                            device_id=_peer(x, y, c, 1), device_id_type=MESH)
        copy.wait_send()
        copy.wait_recv()

    return pl.pallas_call(
        body, name=name, out_shape=pltpu.HBM(land.shape, land.dtype),
        in_specs=[HBM, SEM, SEM, ANY], out_specs=HBM, input_output_aliases={0: 0},
        compiler_params=pltpu.CompilerParams(has_side_effects=EFFECT),
    )(land, send, recv, after)


def _wait_all(name, src, land, send, recv, after):
    def body(src_ref, land_ref, send_ref, recv_ref, after_ref, src_out, land_out):
        x, y, c = lax.axis_index("x"), lax.axis_index("y"), lax.axis_index("c")
        seven = land_ref.at[pl.ds(0, NDEV - 1)]
        copy = pltpu.make_async_remote_copy(src_ref=seven, dst_ref=seven, send_sem=send_ref, recv_sem=recv_ref,
                                            device_id=_peer(x, y, c, 1), device_id_type=MESH)
        copy.wait_send()
        copy.wait_recv()

    return pl.pallas_call(
        body, name=name,
        out_shape=[pltpu.HBM(src.shape, src.dtype), pltpu.HBM(land.shape, land.dtype)],
        in_specs=[HBM, HBM, SEM, SEM, ANY], out_specs=[HBM, HBM],
        input_output_aliases={0: 0, 1: 1},
        compiler_params=pltpu.CompilerParams(has_side_effects=EFFECT),
    )(src, land, send, recv, after)


def _rs_start(name, grad):
    land = lax.empty((NDEV - 1,) + grad.shape[1:], grad.dtype)

    def body(g_ref, land_ref, send, recv, g_out, land_out):
        x, y, c = lax.axis_index("x"), lax.axis_index("y"), lax.axis_index("c")
        for r in range(1, NDEV):
            peer = _peer(x, y, c, r)
            pltpu.make_async_remote_copy(
                src_ref=g_ref.at[_slot(peer)], dst_ref=land_ref.at[r - 1], send_sem=send, recv_sem=recv,
                device_id=peer, device_id_type=MESH).start()

    send, recv, g_thru, land_thru = pl.pallas_call(
        body, name=name,
        out_shape=[pltpu.SemaphoreType.DMA(()), pltpu.SemaphoreType.DMA(()),
                   pltpu.HBM(grad.shape, grad.dtype), pltpu.HBM(land.shape, land.dtype)],
        in_specs=[HBM, HBM], out_specs=[SEM, SEM, HBM, HBM], input_output_aliases={0: 2, 1: 3},
        compiler_params=pltpu.CompilerParams(has_side_effects=EFFECT),
    )(pltpu.with_memory_space_constraint(grad, pltpu.HBM), pltpu.with_memory_space_constraint(land, pltpu.HBM))
    return g_thru, land_thru, send, recv


def _adam_math(w, g, m, v):
    m = ADAM_B1 * m + (1.0 - ADAM_B1) * g
    v = ADAM_B2 * v + (1.0 - ADAM_B2) * (g * g)
    m_hat = m / (1.0 - ADAM_B1 ** ADAM_STEP)
    v_hat = v / (1.0 - ADAM_B2 ** ADAM_STEP)
    delta = -ADAM_LR * (m_hat / (jnp.sqrt(v_hat) + ADAM_EPS) + ADAM_WD * w)
    return delta, m, v


def _adamw(name, parts, w, m, v):
    nparts, rows, cols = parts.shape
    tr = _tile(rows, 256)

    def body(p_ref, w_ref, m_ref, v_ref, g_out, d_out, m_out, v_out):
        g = p_ref[0].astype(F32)
        for k in range(1, nparts):
            g = g + p_ref[k].astype(F32)
        delta, mn, vn = _adam_math(w_ref[...], g, m_ref[...], v_ref[...])
        g_out[...] = g
        d_out[...] = delta
        m_out[...] = mn
        v_out[...] = vn

    row = pl.BlockSpec((tr, cols), lambda i: (i, 0))
    return pl.pallas_call(
        body, name=name, grid=(rows // tr,),
        in_specs=[pl.BlockSpec((nparts, tr, cols), lambda i: (0, i, 0)), row, row, row],
        out_specs=[row] * 4,
        out_shape=[jax.ShapeDtypeStruct((rows, cols), F32)] * 4,
        compiler_params=_cp(("parallel",)),
    )(parts, w, m, v)


def _adamw_layers(name, grads, lands, block, w, m, v):
    layers, rows, cols = w.shape
    nland = lands[0].shape[0]
    tr = rows
    while tr % 2 == 0 and tr > 8 and nland * tr * cols * 2 > (2 << 20):
        tr //= 2

    def body(blk_ref, *refs):
        own_refs, land_refs = refs[:layers], refs[layers:2 * layers]
        w_ref, m_ref, v_ref, g_out, d_out, m_out, v_out = refs[2 * layers:]
        layer = pl.program_id(0)
        for k in range(layers):
            @pl.when(layer == k)
            def _(k=k):
                g = own_refs[k][...].astype(F32)
                for s in range(nland):
                    g = g + land_refs[k][s].astype(F32)
                delta, mn, vn = _adam_math(w_ref[...], g, m_ref[...], v_ref[...])
                g_out[...] = g
                d_out[...] = delta
                m_out[...] = mn
                v_out[...] = vn

    def own_spec(k):
        return pl.BlockSpec((None, tr, cols), lambda l, i, blk: (blk[0], jnp.where(l == k, i, 0), 0))

    def land_spec(k):
        return pl.BlockSpec((nland, tr, cols), lambda l, i, blk: (0, jnp.where(l == k, i, 0), 0))

    row = pl.BlockSpec((None, tr, cols), lambda l, i, blk: (l, i, 0))
    return pl.pallas_call(
        body, name=name,
        grid_spec=pltpu.PrefetchScalarGridSpec(
            num_scalar_prefetch=1, grid=(layers, rows // tr),
            in_specs=[own_spec(k) for k in range(layers)] + [land_spec(k) for k in range(layers)] + [row, row, row],
            out_specs=[row] * 4),
        out_shape=[jax.ShapeDtypeStruct((layers, rows, cols), F32)] * 4,
        compiler_params=_cp(("arbitrary", "arbitrary")),
    )(block, *grads, *lands, w, m, v)


LANES = 128


def _seg_scratch(rows, d):
    return pltpu.VMEM((d // LANES, rows, LANES), F32)


def _to_segments(vals, scratch, out_ref):
    per = scratch.shape[1] // SEGS
    for c in range(scratch.shape[0]):
        cols = slice(c * LANES, (c + 1) * LANES)
        scratch[c] = vals[:, cols]
        for s in range(SEGS):
            out_ref[s, :, cols] = scratch.at[c][pl.ds(s, per, stride=SEGS), :].astype(out_ref.dtype)


def _from_segments(in_ref, scratch):
    per = scratch.shape[1] // SEGS
    for c in range(scratch.shape[0]):
        for s in range(SEGS):
            scratch.at[c][pl.ds(s, per, stride=SEGS), :] = in_ref[s, :, c * LANES:(c + 1) * LANES].astype(F32)
    return jnp.concatenate([scratch[c] for c in range(scratch.shape[0])], axis=1)


def _seg_view(a):
    return a.reshape(SEGS, a.shape[0] // SEGS, a.shape[1])


def _norm_fwd(name, h, z=None, g_post=None, g_pre=None, seg_z=False, seg_y=False):
    rows, d = h.shape
    tm = _tile(rows, 256)
    has_post, has_pre = z is not None, g_pre is not None
    nscratch = int(seg_z) + int(seg_y)

    def body(*refs):
        scratch = list(refs[len(refs) - nscratch:])
        it = iter(refs)
        hv = next(it)[...]
        if has_post:
            z_ref = next(it)
            zv = _from_segments(z_ref, scratch.pop(0)) if seg_z else z_ref[...].astype(F32)
            gp = next(it)[...]
        if has_pre:
            gq = next(it)[...]
        if has_post:
            r = lax.rsqrt(jnp.mean(zv * zv, axis=-1, keepdims=True) + RMS_EPS)
            hv = hv + (zv * r) * gp
            next(it)[...] = hv
        if has_pre:
            r = lax.rsqrt(jnp.mean(hv * hv, axis=-1, keepdims=True) + RMS_EPS)
            y = (hv * r) * gq
            if seg_y:
                _to_segments(y, scratch.pop(0), next(it))
            else:
                next(it)[...] = y.astype(BF16)

    row = pl.BlockSpec((tm, d), lambda i: (i, 0))
    seg = pl.BlockSpec((SEGS, tm // SEGS, d), lambda i: (0, i, 0))
    vec = pl.BlockSpec((1, d), lambda i: (0, 0))
    ins, in_specs, out_shape, out_specs = [h], [row], [], []
    if has_post:
        ins += [_seg_view(z) if seg_z else z, g_post]
        in_specs += [seg if seg_z else row, vec]
        out_shape.append(jax.ShapeDtypeStruct((rows, d), F32))
        out_specs.append(row)
    if has_pre:
        ins.append(g_pre)
        in_specs.append(vec)
        out_shape.append(jax.ShapeDtypeStruct((SEGS, rows // SEGS, d) if seg_y else (rows, d), BF16))
        out_specs.append(seg if seg_y else row)
    outs = pl.pallas_call(body, name=name, grid=(rows // tm,), in_specs=in_specs, out_specs=out_specs,
                          out_shape=out_shape, scratch_shapes=[_seg_scratch(tm, d)] * nscratch,
                          compiler_params=_cp(("parallel",)))(*ins)
    if seg_y:
        outs = list(outs[:-1]) + [outs[-1].reshape(rows, d)]
    return outs


def _rms_bwd_rows(x, g, dy):
    r = lax.rsqrt(jnp.mean(x * x, axis=-1, keepdims=True) + RMS_EPS)
    xn = x * r
    dg = jnp.sum(dy * xn, axis=0, keepdims=True)
    dxn = dy * g
    dx = r * (dxn - xn * jnp.mean(dxn * xn, axis=-1, keepdims=True))
    return dx, dg


def _norm_bwd(name, d_out, pre=None, post=None, seg_dy=False, seg_z=False):
    rows, d = d_out.shape
    tm = _tile(rows, 256)
    has_pre, has_post = pre is not None, post is not None
    nscratch = int(seg_dy) + 2 * int(seg_z)

    def body(*refs):
        scratch = list(refs[len(refs) - nscratch:])
        it = iter(refs)
        dres = next(it)[...]
        if has_pre:
            dy_ref = next(it)
            dy = _from_segments(dy_ref, scratch.pop(0)) if seg_dy else dy_ref[...].astype(F32)
            xp, gq = next(it)[...], next(it)[...]
        if has_post:
            z_ref = next(it)
            zv = _from_segments(z_ref, scratch.pop(0)) if seg_z else z_ref[...].astype(F32)
            gp = next(it)[...]
        first = pl.program_id(0) == 0
        if has_pre:
            dx, dg = _rms_bwd_rows(xp, gq, dy)
            dres = dres + dx
            next(it)[...] = dres
            dg_ref = next(it)

            @pl.when(first)
            def _():
                dg_ref[...] = jnp.zeros_like(dg_ref)
            dg_ref[...] += dg
        if has_post:
            dz, dg2 = _rms_bwd_rows(zv, gp, dres)
            if seg_z:
                _to_segments(dz, scratch.pop(0), next(it))
            else:
                next(it)[...] = dz.astype(BF16)
            dg2_ref = next(it)

            @pl.when(first)
            def _():
                dg2_ref[...] = jnp.zeros_like(dg2_ref)
            dg2_ref[...] += dg2

    row = pl.BlockSpec((tm, d), lambda i: (i, 0))
    seg = pl.BlockSpec((SEGS, tm // SEGS, d), lambda i: (0, i, 0))
    vec = pl.BlockSpec((1, d), lambda i: (0, 0))
    ins, in_specs, out_shape, out_specs = [d_out], [row], [], []
    if has_pre:
        d_y, x_pre, g_pre = pre
        ins += [_seg_view(d_y) if seg_dy else d_y, x_pre, g_pre]
        in_specs += [seg if seg_dy else row, row, vec]
        out_shape += [jax.ShapeDtypeStruct((rows, d), F32), jax.ShapeDtypeStruct((1, d), F32)]
        out_specs += [row, vec]
    if has_post:
        z, g_post = post
        ins += [_seg_view(z) if seg_z else z, g_post]
        in_specs += [seg if seg_z else row, vec]
        out_shape += [jax.ShapeDtypeStruct((SEGS, rows // SEGS, d) if seg_z else (rows, d), BF16),
                      jax.ShapeDtypeStruct((1, d), F32)]
        out_specs += [seg if seg_z else row, vec]
    outs = pl.pallas_call(body, name=name, grid=(rows // tm,), in_specs=in_specs, out_specs=out_specs,
                          out_shape=out_shape, scratch_shapes=[_seg_scratch(tm, d)] * nscratch,
                          compiler_params=_cp(("arbitrary",)))(*ins)
    if seg_z:
        outs = list(outs)
        outs[-2] = outs[-2].reshape(rows, d)
    return outs


def _loss_grad(name, y, target):
    rows, d = y.shape
    tm = _tile(rows, 256)

    def body(y_ref, t_ref, dy_ref, loss_ref):
        err = y_ref[...] - t_ref[...]
        dy_ref[...] = err * (1.0 / d)

        @pl.when(pl.program_id(0) == 0)
        def _():
            loss_ref[...] = jnp.zeros_like(loss_ref)
        loss_ref[...] += jnp.full(loss_ref.shape, (0.5 / d) * jnp.sum(err * err), F32)

    row = pl.BlockSpec((tm, d), lambda i: (i, 0))
    return pl.pallas_call(
        body, name=name, grid=(rows // tm,), in_specs=[row, row],
        out_specs=[row, pl.BlockSpec((1, 128), lambda i: (0, 0))],
        out_shape=[jax.ShapeDtypeStruct((rows, d), F32), jax.ShapeDtypeStruct((1, 128), F32)],
        compiler_params=_cp(("arbitrary",)))(y, target)


NT_DIMS = (((1,), (1,)), ((), ()))
TN_DIMS = (((0,), (0,)), ((), ()))


MXU_WIDTH = 256


def _blocks_per_step(nb):
    return 1 if nb % MXU_WIDTH == 0 else 2


def _mm_nn_blk(name, a, wblk, relu2=False, after=()):
    m, k = a.shape
    nb = wblk.shape[2]
    tm = _tile(m, 1024)
    per = _blocks_per_step(nb)
    after = list(after)

    def body(a_ref, w_ref, *rest):
        w = w_ref[0] if per == 1 else jnp.concatenate([w_ref[t] for t in range(per)], axis=1)
        r = jnp.dot(a_ref[...], w, preferred_element_type=F32)
        if relu2:
            rr = jnp.maximum(r, 0.0)
            rest[-2][...] = (rr * rr).astype(BF16)
            rest[-1][...] = (2.0 * rr).astype(BF16)
        else:
            rest[-1][...] = r.astype(BF16)

    out = jax.ShapeDtypeStruct((m, NDEV * nb), BF16)
    spec = pl.BlockSpec((tm, per * nb), lambda d, i: (i, d))
    return pl.pallas_call(
        body, name=name, grid=(NDEV // per, m // tm),
        in_specs=[pl.BlockSpec((tm, k), lambda d, i: (i, 0)), pl.BlockSpec((per, k, nb), lambda d, i: (d, 0, 0))]
        + [ANY] * len(after),
        out_specs=[spec, spec] if relu2 else spec,
        out_shape=[out, out] if relu2 else out,
        compiler_params=_cp(("parallel", "parallel"), VMEM_MM))(a, wblk, *after)


def _accumulate(acc, o_ref, r, step, last):
    if acc is None:
        o_ref[...] = r.astype(o_ref.dtype)
        return

    @pl.when(step == 0)
    def _():
        acc[...] = r

    @pl.when(jnp.logical_and(step > 0, step < last))
    def _():
        acc[...] += r

    @pl.when(jnp.logical_and(step > 0, step == last))
    def _():
        o_ref[...] = (acc[...] + r).astype(o_ref.dtype)


def _mm_nn(name, a, w, after=()):
    m, kb = a.shape
    n = w.shape[1]
    one_step = kb <= 2048
    tm = _tile(m, 512 if one_step else 1024)
    tk = kb if one_step else _tile(kb, 4096)
    tn = n if one_step else _tile(n, 1024)
    steps = kb // tk
    after = list(after)

    def body(a_ref, w_ref, *rest):
        o_ref, scratch = rest[len(after)], rest[len(after) + 1:]
        r = jnp.dot(a_ref[...], w_ref[...], preferred_element_type=F32)
        _accumulate(scratch[0] if scratch else None, o_ref, r, pl.program_id(2), steps - 1)

    return pl.pallas_call(
        body, name=name, grid=(m // tm, n // tn, steps),
        in_specs=[pl.BlockSpec((tm, tk), lambda i, j, s: (i, s)), pl.BlockSpec((tk, tn), lambda i, j, s: (s, j))]
        + [ANY] * len(after),
        out_specs=pl.BlockSpec((tm, tn), lambda i, j, s: (i, j)),
        out_shape=jax.ShapeDtypeStruct((m, n), BF16),
        scratch_shapes=[pltpu.VMEM((tm, tn), F32)] if steps > 1 else [],
        compiler_params=_cp(("parallel", "parallel", "arbitrary"), VMEM_MM))(a, w, *after)


def _mm_nt_rows(name, dy, w, act=None, out_dtype=BF16):
    m, n = dy.shape
    kw = w.shape[0]
    tm, tkw = _tile(m, 1024), _tile(kw, 1024)

    def body(dy_ref, w_ref, *rest):
        r = lax.dot_general(dy_ref[...], w_ref[...], NT_DIMS, preferred_element_type=F32)
        if act is None:
            rest[0][...] = r.astype(out_dtype)
        else:
            rest[1][...] = (r * rest[0][...].astype(F32)).astype(BF16)

    ins = [dy, w]
    in_specs = [pl.BlockSpec((tm, n), lambda j, i: (i, 0)), pl.BlockSpec((tkw, n), lambda j, i: (j, 0))]
    if act is not None:
        ins.append(act)
        in_specs.append(pl.BlockSpec((tm, tkw), lambda j, i: (i, j)))
    return pl.pallas_call(
        body, name=name, grid=(kw // tkw, m // tm), in_specs=in_specs,
        out_specs=pl.BlockSpec((tm, tkw), lambda j, i: (i, j)),
        out_shape=jax.ShapeDtypeStruct((m, kw), out_dtype if act is None else BF16),
        compiler_params=_cp(("parallel", "parallel"), VMEM_MM))(*ins)


def _mm_nt_blk(name, dy, wblk, after=None):
    m = dy.shape[0]
    _, kw, nb = wblk.shape
    tm, tkw, per = _tile(m, 1024), _tile(kw, 1024), 4

    extra = list(after or ())

    def body(dy_ref, w_ref, *rest):
        o_ref, acc = rest[len(extra):]
        if nb % MXU_WIDTH == 0:
            r = lax.dot_general(dy_ref[:, :nb], w_ref[0], NT_DIMS, preferred_element_type=F32)
            for t in range(1, per):
                r = r + lax.dot_general(dy_ref[:, t * nb:(t + 1) * nb], w_ref[t], NT_DIMS, preferred_element_type=F32)
        else:
            w = jnp.concatenate([w_ref[t] for t in range(per)], axis=1)
            r = lax.dot_general(dy_ref[...], w, NT_DIMS, preferred_element_type=F32)
        _accumulate(acc, o_ref, r, pl.program_id(2), NDEV // per - 1)

    return pl.pallas_call(
        body, name=name, grid=(m // tm, kw // tkw, NDEV // per),
        in_specs=[pl.BlockSpec((tm, per * nb), lambda i, j, s: (i, s)),
                  pl.BlockSpec((per, tkw, nb), lambda i, j, s: (s, j, 0))] + [ANY] * len(extra),
        out_specs=pl.BlockSpec((tm, tkw), lambda i, j, s: (i, j)),
        out_shape=jax.ShapeDtypeStruct((m, kw), BF16),
        scratch_shapes=[pltpu.VMEM((tm, tkw), F32)],
        compiler_params=_cp(("parallel", "parallel", "arbitrary"), VMEM_MM))(dy, wblk, *extra)


def _mm_tn(name, x, dy, nb=None):
    t, mx = x.shape
    n = dy.shape[1]
    tmx = _tile(mx, 512)
    per = 1 if nb is None else _blocks_per_step(nb)
    tn = per * nb if nb is not None else _tile(n, 1024)

    def body(x_ref, dy_ref, o_ref):
        r = lax.dot_general(x_ref[...], dy_ref[...], TN_DIMS, preferred_element_type=F32).astype(BF16)
        if nb is None:
            o_ref[...] = r
        else:
            for b in range(per):
                o_ref[b] = r[:, b * nb:(b + 1) * nb]

    if nb is None:
        out_shape = jax.ShapeDtypeStruct((mx, n), BF16)
        out_spec = pl.BlockSpec((tmx, tn), lambda j, i: (i, j))
    else:
        out_shape = jax.ShapeDtypeStruct((NDEV, mx, nb), BF16)
        out_spec = pl.BlockSpec((per, tmx, nb), lambda j, i: (j, i, 0))
    return pl.pallas_call(
        body, name=name, grid=(n // tn, mx // tmx),
        in_specs=[pl.BlockSpec((t, tmx), lambda j, i: (0, i)), pl.BlockSpec((t, tn), lambda j, i: (0, j))],
        out_specs=out_spec, out_shape=out_shape,
        compiler_params=_cp(("parallel", "parallel"), VMEM_MM))(x, dy)


def _gelu(x):
    return 0.5 * x * (1.0 + jnp.tanh(GELU_C * (x + GELU_A * (x * x * x))))


def _gelu_grad(x):
    t = jnp.tanh(GELU_C * (x + GELU_A * (x * x * x)))
    return 0.5 * (1.0 + t) + 0.5 * x * (1.0 - t * t) * (GELU_C * (1.0 + 3.0 * GELU_A * (x * x)))


def _layernorm(a):
    mu = jnp.mean(a, axis=-1, keepdims=True)
    ac = a - mu
    rstd = lax.rsqrt(jnp.mean(ac * ac, axis=-1, keepdims=True) + LN_EPS)
    return ac * rstd, rstd


def _shift_rows(z, halo, k):
    zr = pltpu.roll(z, k, 0)
    hr = pltpu.roll(halo, k, 0)
    row = lax.broadcasted_iota(jnp.int32, hr.shape, 0)
    top = jnp.where(row < k, hr, zr[:HALO])
    return jnp.concatenate([top, zr[HALO:]], axis=0)


def _shift_rows_up(z, halo, k):
    rows = z.shape[0]
    zr = pltpu.roll(z, rows - k, 0)
    hr = pltpu.roll(halo, HALO - k, 0)
    row = lax.broadcasted_iota(jnp.int32, hr.shape, 0)
    bot = jnp.where(row >= HALO - k, hr, zr[rows - HALO:])
    return jnp.concatenate([zr[:rows - HALO], bot], axis=0)


def _causal_mask():
    t = lax.broadcasted_iota(jnp.int32, (CHUNK, CHUNK), 0)
    s = lax.broadcasted_iota(jnp.int32, (CHUNK, CHUNK), 1)
    return s <= t


def _gate_specs(tm, rows, width):
    per = tm // HALO_BF16
    last = rows // HALO_BF16 - 1
    cur = pl.BlockSpec((tm, width), lambda i: (i, 0))
    prev = pl.BlockSpec((HALO_BF16, width), lambda i: (jnp.maximum(i * per - 1, 0), 0))
    nxt = pl.BlockSpec((HALO_BF16, width), lambda i: (jnp.minimum((i + 1) * per, last), 0))
    return cur, prev, nxt


def _cols(ref, lo, hi):
    return ref[:, lo:hi].astype(F32)


def _halo_before(ref, lo, hi):
    return ref[:, lo:hi].astype(F32)[HALO_BF16 - HALO:]


def _halo_after(ref, lo, hi):
    return ref[:, lo:hi].astype(F32)[:HALO]


def _gate_fwd(name, proj, w_s, b_st, cw):
    rows, width = proj.shape
    w = width // 5
    groups = w // CHUNK
    tm = _tile(rows, 256)
    cur, prev, _ = _gate_specs(tm, rows, width)

    def body(p_ref, h_ref, ws_ref, b_ref, cw_ref, o_ref):
        mask = _causal_mask()
        au = _gelu(_cols(p_ref, 0, w))
        vn, _ = _layernorm(_gelu(_cols(p_ref, w, 2 * w)))
        vn = vn.astype(BF16)
        for g in range(groups):
            wc = jnp.where(mask, ws_ref[g], 0.0).astype(BF16)
            cols = slice(g * CHUNK, (g + 1) * CHUNK)
            for ch in range(tm // CHUNK):
                rws = slice(ch * CHUNK, (ch + 1) * CHUNK)
                mixed = jnp.dot(wc, vn[rws, cols], preferred_element_type=F32) + b_ref[:, g:g + 1]
                o_ref[rws, cols] = (au[rws, cols] * mixed).astype(BF16)
        z = _cols(p_ref, 3 * w, 4 * w) * _cols(p_ref, 4 * w, 5 * w)
        zh = _halo_before(h_ref, 3 * w, 4 * w) * _halo_before(h_ref, 4 * w, 5 * w)
        zh = jnp.where(pl.program_id(0) == 0, 0.0, zh)
        y = cw_ref[0:1, :] * _shift_rows(z, zh, 2) + cw_ref[1:2, :] * _shift_rows(z, zh, 1) + cw_ref[2:3, :] * z
        o_ref[:, w:2 * w] = (_cols(p_ref, 2 * w, 3 * w) * y).astype(BF16)

    full = lambda a: pl.BlockSpec(a.shape, lambda i: (0,) * a.ndim)
    return pl.pallas_call(
        body, name=name, grid=(rows // tm,),
        in_specs=[cur, prev, full(w_s), full(b_st), full(cw)],
        out_specs=pl.BlockSpec((tm, 2 * w), lambda i: (i, 0)),
        out_shape=jax.ShapeDtypeStruct((rows, 2 * w), BF16),
        compiler_params=_cp(("parallel",)))(proj, proj, w_s, b_st, cw)


def _gate_bwd(name, proj, d_ab, w_s, b_st, cw):
    rows, width = proj.shape
    w = width // 5
    groups = w // CHUNK
    tm = _tile(rows, 256)
    cur, prev, nxt = _gate_specs(tm, rows, width)
    dcur, _, dnxt = _gate_specs(tm, rows, 2 * w)

    def body(p_ref, ph_ref, pn_ref, d_ref, dn_ref, ws_ref, b_ref, cw_ref, o_ref, dws_ref, dbs_ref, dcw_ref):
        i = pl.program_id(0)

        @pl.when(i == 0)
        def _():
            dws_ref[...] = jnp.zeros_like(dws_ref)
            dbs_ref[...] = jnp.zeros_like(dbs_ref)
            dcw_ref[...] = jnp.zeros_like(dcw_ref)

        mask = _causal_mask()
        u, v = _cols(p_ref, 0, w), _cols(p_ref, w, 2 * w)
        au, av = _gelu(u), _gelu(v)
        vn, rstd = _layernorm(av)
        vnb = vn.astype(BF16)
        d_a = _cols(d_ref, 0, w)
        d_mixed = (d_a * au).astype(BF16)
        ones = jnp.ones((HALO, CHUNK), BF16)
        d_vn_cols = []
        d_au_cols = []
        for g in range(groups):
            wc = jnp.where(mask, ws_ref[g], 0.0).astype(BF16)
            cols = slice(g * CHUNK, (g + 1) * CHUNK)
            dw = jnp.zeros((CHUNK, CHUNK), F32)
            db = jnp.zeros((HALO, CHUNK), F32)
            d_vn_rows, d_au_rows = [], []
            for ch in range(tm // CHUNK):
                rws = slice(ch * CHUNK, (ch + 1) * CHUNK)
                mixed = jnp.dot(wc, vnb[rws, cols], preferred_element_type=F32) + b_ref[:, g:g + 1]
                d_au_rows.append(d_a[rws, cols] * mixed)
                dm = d_mixed[rws, cols]
                dw = dw + lax.dot_general(dm, vnb[rws, cols], NT_DIMS, preferred_element_type=F32)
                db = db + lax.dot_general(ones, dm, NT_DIMS, preferred_element_type=F32)
                d_vn_rows.append(lax.dot_general(wc, dm, TN_DIMS, preferred_element_type=F32))
            dws_ref[g] += jnp.where(mask, dw, 0.0)
            dbs_ref[g:g + 1, :] += db[0:1, :]
            d_vn_cols.append(jnp.concatenate(d_vn_rows, axis=0))
            d_au_cols.append(jnp.concatenate(d_au_rows, axis=0))
        d_vn = jnp.concatenate(d_vn_cols, axis=1)
        d_au = jnp.concatenate(d_au_cols, axis=1)
        d_av = rstd * (d_vn - jnp.mean(d_vn, axis=-1, keepdims=True)
                       - vn * jnp.mean(d_vn * vn, axis=-1, keepdims=True))
        o_ref[:, 0:w] = (d_au * _gelu_grad(u)).astype(BF16)
        o_ref[:, w:2 * w] = (d_av * _gelu_grad(v)).astype(BF16)

        gb, gc, bx = _cols(p_ref, 2 * w, 3 * w), _cols(p_ref, 3 * w, 4 * w), _cols(p_ref, 4 * w, 5 * w)
        z = gc * bx
        zh = jnp.where(i == 0, 0.0, _halo_before(ph_ref, 3 * w, 4 * w) * _halo_before(ph_ref, 4 * w, 5 * w))
        z1, z2 = _shift_rows(z, zh, 1), _shift_rows(z, zh, 2)
        d_b = _cols(d_ref, w, 2 * w)
        y = cw_ref[0:1, :] * z2 + cw_ref[1:2, :] * z1 + cw_ref[2:3, :] * z
        dy = d_b * gb
        dyn = jnp.where(i == pl.num_programs(0) - 1, 0.0,
                        _halo_after(dn_ref, w, 2 * w) * _halo_after(pn_ref, 2 * w, 3 * w))
        dz = (cw_ref[2:3, :] * dy + cw_ref[1:2, :] * _shift_rows_up(dy, dyn, 1)
              + cw_ref[0:1, :] * _shift_rows_up(dy, dyn, 2))
        dcw_ref[0:1, :] += jnp.sum(dy * z2, axis=0, keepdims=True)
        dcw_ref[1:2, :] += jnp.sum(dy * z1, axis=0, keepdims=True)
        dcw_ref[2:3, :] += jnp.sum(dy * z, axis=0, keepdims=True)
        o_ref[:, 2 * w:3 * w] = (d_b * y).astype(BF16)
        o_ref[:, 3 * w:4 * w] = (dz * bx).astype(BF16)
        o_ref[:, 4 * w:5 * w] = (dz * gc).astype(BF16)

    full = lambda a: pl.BlockSpec(a.shape, lambda i: (0,) * a.ndim)
    acc = lambda shape: pl.BlockSpec(shape, lambda i: (0,) * len(shape))
    return pl.pallas_call(
        body, name=name, grid=(rows // tm,),
        in_specs=[cur, prev, nxt, dcur, dnxt, full(w_s), full(b_st), full(cw)],
        out_specs=[pl.BlockSpec((tm, width), lambda i: (i, 0)), acc((groups, CHUNK, CHUNK)),
                   acc((groups, CHUNK)), acc((HALO, w))],
        out_shape=[jax.ShapeDtypeStruct((rows, width), BF16), jax.ShapeDtypeStruct((groups, CHUNK, CHUNK), F32),
                   jax.ShapeDtypeStruct((groups, CHUNK), F32), jax.ShapeDtypeStruct((HALO, w), F32)],
        compiler_params=_cp(("arbitrary",), VMEM_MM))(proj, proj, proj, d_ab, d_ab, w_s, b_st, cw)


def _flat(x):
    return x.reshape(-1, x.shape[-1])


def _rope(t, cosf, sins):
    t2 = _flat(t)
    return (t2 * _flat(cosf) + pltpu.roll(t2, HEAD // 2, 1) * _flat(sins)).reshape(t.shape)


def _rope_bwd(dt, cosf, sins):
    d2 = _flat(dt)
    return (d2 * _flat(cosf) + pltpu.roll(d2 * _flat(sins), HEAD // 2, 1)).reshape(dt.shape)


ATT_UNITS = ATT_TILE // CHUNK


def _attn_units(phases):
    for b, d in enumerate(DILATIONS):
        blocks = ATT_TILE // (CHUNK * d)
        for visit in phases:
            for r in range(d):
                if blocks <= ATT_UNROLL:
                    for j in range(blocks):
                        visit(b, d, r, j, r * blocks + j)
                else:
                    def step(jj, carry, b=b, d=d, r=r, visit=visit, blocks=blocks):
                        for u in range(ATT_UNROLL):
                            j = jj * ATT_UNROLL + u
                            visit(b, d, r, j, r * blocks + j)
                        return carry
                    lax.fori_loop(0, blocks // ATT_UNROLL, step, 0)


class _Unit:
    def __init__(self, d, r, j):
        self.segs = [r + d * k for k in range(SEGS // d)]
        self.w = CHUNK * d // SEGS
        q0 = j * self.w
        self.q0 = q0 if isinstance(q0, int) else pl.multiple_of(q0, HALO)
        k0 = CHUNK + (j - 1) * self.w
        self.k0 = k0 if isinstance(k0, int) else pl.multiple_of(k0, HALO)

    def queries(self, ref):
        return _chunks(ref, self.segs, self.q0, self.w)

    def keys(self, ref):
        return _chunks(ref, self.segs, self.k0, 2 * self.w)

    def put_queries(self, ref, val, add=False):
        _put_chunks(ref, self.segs, self.q0, self.w, val, add)

    def put_keys(self, ref, val, add=False):
        _put_chunks(ref, self.segs, self.k0, 2 * self.w, val, add)


def _chunks(ref, segs, start, size):
    parts = [ref[s, pl.ds(start, size), :] for s in segs]
    return parts[0] if len(parts) == 1 else jnp.concatenate(parts, axis=0)


def _put_chunks(ref, segs, start, size, val, add):
    for k, s in enumerate(segs):
        piece = val[k * size:(k + 1) * size]
        if add:
            ref[s, pl.ds(start, size), :] += piece
        else:
            ref[s, pl.ds(start, size), :] = piece


def _band_bias():
    qi = lax.broadcasted_iota(jnp.int32, (CHUNK, 2 * CHUNK), 0)
    ki = lax.broadcasted_iota(jnp.int32, (CHUNK, 2 * CHUNK), 1)
    tables = []
    for d in DILATIONS:
        nseg, w = SEGS // d, CHUNK * d // SEGS
        pos_q = nseg * (qi % w) + qi // w
        pos_k = nseg * (ki % (2 * w) - w) + ki // (2 * w)
        band = (pos_q >= pos_k) & (pos_q - pos_k <= CHUNK)
        tables += [jnp.where(band, 0.0, -jnp.inf), jnp.where(band & (pos_k >= 0), 0.0, -jnp.inf)]
    return jnp.stack(tables).astype(F32)


def _bias_spec():
    return pl.BlockSpec((2 * len(DILATIONS), CHUNK, 2 * CHUNK), lambda h, n: (0, 0, 0))


def _unit_bias(bias, b, n, j):
    if isinstance(j, int) and j != 0:
        return bias[2 * b]
    return bias[2 * b + jnp.where(jnp.logical_and(n == 0, j == 0), 1, 0)]


def _attn_in_specs(heads):
    blk = (SEGS, CHUNK, HEAD)
    prev = lambda n: jnp.maximum(n - 1, 0)
    return [
        pl.BlockSpec(blk, lambda h, n: (0, n, h)),
        pl.BlockSpec(blk, lambda h, n: (0, n, heads + h)),
        pl.BlockSpec(blk, lambda h, n: (0, prev(n), heads + h)),
        pl.BlockSpec(blk, lambda h, n: (0, n, 2 * heads + h)),
        pl.BlockSpec(blk, lambda h, n: (0, prev(n), 2 * heads + h)),
        pl.BlockSpec(blk, lambda h, n: (0, n, 0)),
        pl.BlockSpec(blk, lambda h, n: (0, n, 0)),
        pl.BlockSpec(blk, lambda h, n: (0, prev(n), 0)),
        pl.BlockSpec(blk, lambda h, n: (0, prev(n), 0)),
    ]


def _attn_load(q_ref, kc_ref, kp_ref, vc_ref, vp_ref, cc_ref, sc_ref, cp_ref, sp_ref, qr, kcat, vcat):
    qr[...] = _rope(q_ref[...].astype(F32), cc_ref[...], sc_ref[...]) * (HEAD ** -0.5)
    kcat[:, pl.ds(0, CHUNK), :] = _rope(kp_ref[...].astype(F32), cp_ref[...], sp_ref[...])
    kcat[:, pl.ds(CHUNK, CHUNK), :] = _rope(kc_ref[...].astype(F32), cc_ref[...], sc_ref[...])
    vcat[:, pl.ds(0, CHUNK), :] = vp_ref[...].astype(F32)
    vcat[:, pl.ds(CHUNK, CHUNK), :] = vc_ref[...].astype(F32)


def _attn_fwd(name, qkv, cosf, sins, bias):
    t = qkv.shape[0]
    heads = qkv.shape[1] // (3 * HEAD)
    nbr = len(DILATIONS)

    def body(q_ref, kc_ref, kp_ref, vc_ref, vp_ref, cc_ref, sc_ref, cp_ref, sp_ref, bias, o_ref, lse_ref,
             qr, kcat, vcat, obr, mbr, dbr, pn):
        n = pl.program_id(1)
        _attn_load(q_ref, kc_ref, kp_ref, vc_ref, vp_ref, cc_ref, sc_ref, cp_ref, sp_ref, qr, kcat, vcat)

        def probs(b, d, r, j, u):
            unit = _Unit(d, r, j)
            s = lax.dot_general(unit.queries(qr).astype(BF16), unit.keys(kcat).astype(BF16), NT_DIMS,
                                preferred_element_type=F32) + _unit_bias(bias, b, n, j)
            mx = jnp.max(s, axis=-1, keepdims=True)
            p = jnp.exp(s - mx)
            pn[u] = p.astype(BF16)
            unit.put_queries(mbr.at[b], jnp.broadcast_to(mx, (CHUNK, HEAD)))
            unit.put_queries(dbr.at[b], jnp.broadcast_to(jnp.sum(p, axis=-1, keepdims=True), (CHUNK, HEAD)))

        def values(b, d, r, j, u):
            unit = _Unit(d, r, j)
            unit.put_queries(obr.at[b], jnp.dot(pn[u], unit.keys(vcat).astype(BF16), preferred_element_type=F32))

        _attn_units([probs, values])
        ms = [mbr[b] for b in range(nbr)]
        top = functools.reduce(jnp.maximum, ms)
        ws = [jnp.exp(m - top) for m in ms]
        tot = functools.reduce(jnp.add, [ws[b] * dbr[b] for b in range(nbr)])
        inv = 1.0 / tot
        o = (ws[0] * inv) * obr[0]
        for b in range(1, nbr):
            o = o + (ws[b] * inv) * obr[b]
        o_ref[...] = o.astype(BF16)
        lse_ref[...] = top + jnp.log(tot)

    blk = (SEGS, CHUNK, HEAD)
    keys = pltpu.VMEM((SEGS, 2 * CHUNK, HEAD), F32)
    tile = pl.BlockSpec(blk, lambda h, n: (0, n, h))
    seg = t // SEGS
    qkv3, cos3, sin3 = _seg_view(qkv), _seg_view(cosf), _seg_view(sins)
    o, lse = pl.pallas_call(
        body, name=name, grid=(heads, t // ATT_TILE), in_specs=_attn_in_specs(heads) + [_bias_spec()],
        out_specs=[tile, tile],
        out_shape=[jax.ShapeDtypeStruct((SEGS, seg, heads * HEAD), BF16),
                   jax.ShapeDtypeStruct((SEGS, seg, heads * HEAD), F32)],
        scratch_shapes=[pltpu.VMEM(blk, F32), keys, keys,
                        pltpu.VMEM((nbr,) + blk, F32), pltpu.VMEM((nbr,) + blk, F32), pltpu.VMEM((nbr,) + blk, F32),
                        pltpu.VMEM((ATT_UNITS, CHUNK, 2 * CHUNK), BF16)],
        compiler_params=_cp(("parallel", "parallel"), VMEM_MM),
    )(qkv3, qkv3, qkv3, qkv3, qkv3, cos3, sin3, cos3, sin3, bias)
    return o.reshape(t, heads * HEAD), lse.reshape(t, heads * HEAD)


def _attn_bwd(name, qkv, cosf, sins, bias, d_o, o, lse):
    t = qkv.shape[0]
    heads = qkv.shape[1] // (3 * HEAD)
    scale = HEAD ** -0.5

    def body(q_ref, kc_ref, kp_ref, vc_ref, vp_ref, cc_ref, sc_ref, cp_ref, sp_ref, do_ref, o_ref, lse_ref, bias,
             dq_ref, dko_ref, dkp_ref, dvo_ref, dvp_ref, qr, kcat, vcat, dq_acc, dk_acc, dv_acc, delta, ps, dss):
        n = pl.program_id(1)
        _attn_load(q_ref, kc_ref, kp_ref, vc_ref, vp_ref, cc_ref, sc_ref, cp_ref, sp_ref, qr, kcat, vcat)
        dq_acc[...] = jnp.zeros_like(dq_acc)
        dk_acc[...] = jnp.zeros_like(dk_acc)
        dv_acc[...] = jnp.zeros_like(dv_acc)
        delta[...] = jnp.broadcast_to(
            jnp.sum(do_ref[...] * o_ref[...].astype(F32), axis=-1, keepdims=True), delta.shape)

        def probs(b, d, r, j, u):
            unit = _Unit(d, r, j)
            s = lax.dot_general(unit.queries(qr).astype(BF16), unit.keys(kcat).astype(BF16), NT_DIMS,
                                preferred_element_type=F32) + _unit_bias(bias, b, n, j)
            ps[u] = jnp.exp(s - unit.queries(lse_ref)[:, 0:1]).astype(BF16)

        def score_grads(b, d, r, j, u):
            unit = _Unit(d, r, j)
            dp = lax.dot_general(unit.queries(do_ref).astype(BF16), unit.keys(vcat).astype(BF16), NT_DIMS,
                                 preferred_element_type=F32)
            dss[u] = (ps[u].astype(F32) * (dp - unit.queries(delta)[:, 0:1])).astype(BF16)

        def input_grads(b, d, r, j, u):
            unit = _Unit(d, r, j)
            ds = dss[u]
            unit.put_queries(dq_acc, jnp.dot(ds, unit.keys(kcat).astype(BF16), preferred_element_type=F32), add=True)
            unit.put_keys(dk_acc, lax.dot_general(ds, unit.queries(qr).astype(BF16), TN_DIMS,
                                                  preferred_element_type=F32), add=True)
            unit.put_keys(dv_acc, lax.dot_general(ps[u], unit.queries(do_ref).astype(BF16), TN_DIMS,
                                                  preferred_element_type=F32), add=True)

        _attn_units([probs, score_grads, input_grads])
        dq_ref[...] = _rope_bwd(dq_acc[...] * scale, cc_ref[...], sc_ref[...]).astype(BF16)
        dkp_ref[...] = _rope_bwd(dk_acc[:, pl.ds(0, CHUNK), :], cp_ref[...], sp_ref[...]).astype(BF16)
        dko_ref[...] = _rope_bwd(dk_acc[:, pl.ds(CHUNK, CHUNK), :], cc_ref[...], sc_ref[...]).astype(BF16)
        dvp_ref[...] = dv_acc[:, pl.ds(0, CHUNK), :].astype(BF16)
        dvo_ref[...] = dv_acc[:, pl.ds(CHUNK, CHUNK), :].astype(BF16)

    blk = (SEGS, CHUNK, HEAD)
    tile = pl.BlockSpec(blk, lambda h, n: (0, n, h))
    big = pltpu.VMEM((SEGS, 2 * CHUNK, HEAD), F32)
    seg = t // SEGS
    qkv3, cos3, sin3 = _seg_view(qkv), _seg_view(cosf), _seg_view(sins)
    return pl.pallas_call(
        body, name=name, grid=(heads, t // ATT_TILE),
        in_specs=_attn_in_specs(heads) + [tile, tile, tile, _bias_spec()],
        out_specs=[tile] * 5,
        out_shape=[jax.ShapeDtypeStruct((SEGS, seg, heads * HEAD), BF16)] * 5,
        scratch_shapes=[pltpu.VMEM(blk, F32), big, big, pltpu.VMEM(blk, F32), big, big, pltpu.VMEM(blk, F32),
                        pltpu.VMEM((ATT_UNITS, CHUNK, 2 * CHUNK), BF16), pltpu.VMEM((ATT_UNITS, CHUNK, 2 * CHUNK), BF16)],
        compiler_params=_cp(("parallel", "parallel"), 60 << 20),
    )(qkv3, qkv3, qkv3, qkv3, qkv3, cos3, sin3, cos3, sin3, _seg_view(d_o), _seg_view(o), _seg_view(lse), bias)


def _attn_merge(name, dq, dk_own, dk_prev, dv_own, dv_prev):
    _, seg, hd = dq.shape
    nt = seg // CHUNK
    tw = _tile(hd, 512)

    def body(dq_ref, dko_ref, dkn_ref, dvo_ref, dvn_ref, o_ref):
        last = pl.program_id(0) == nt - 1
        part = pl.program_id(1)

        @pl.when(part == 0)
        def _():
            o_ref[...] = dq_ref[...]

        @pl.when(part == 1)
        def _():
            o_ref[...] = (dko_ref[...].astype(F32) + jnp.where(last, 0.0, dkn_ref[...].astype(F32))).astype(BF16)

        @pl.when(part == 2)
        def _():
            o_ref[...] = (dvo_ref[...].astype(F32) + jnp.where(last, 0.0, dvn_ref[...].astype(F32))).astype(BF16)

    blk = (SEGS, CHUNK, tw)

    def own(part):
        return pl.BlockSpec(blk, lambda n, p, c: (0, jnp.where(p == part, n, 0), jnp.where(p == part, c, 0)))

    def nxt(part):
        return pl.BlockSpec(blk, lambda n, p, c: (0, jnp.where(p == part, jnp.minimum(n + 1, nt - 1), 0),
                                                  jnp.where(p == part, c, 0)))

    per = hd // tw
    return pl.pallas_call(
        body, name=name, grid=(nt, 3, per), in_specs=[own(0), own(1), nxt(1), own(2), nxt(2)],
        out_specs=pl.BlockSpec(blk, lambda n, p, c: (0, n, p * per + c)),
        out_shape=jax.ShapeDtypeStruct((SEGS, seg, 3 * hd), BF16),
        compiler_params=_cp(("parallel", "parallel", "parallel"), VMEM_MM),
    )(dq, dk_own, dk_prev, dv_own, dv_prev).reshape(SEGS * seg, 3 * hd)


def _sum_parts(name, parts):
    nparts, rows, cols = parts.shape
    tr = _tile(rows, 256)

    def body(p_ref, o_ref):
        s = p_ref[0]
        for k in range(1, nparts):
            s = s + p_ref[k]
        o_ref[...] = s

    return pl.pallas_call(
        body, name=name, grid=(rows // tr,),
        in_specs=[pl.BlockSpec((nparts, tr, cols), lambda i: (0, i, 0))],
        out_specs=pl.BlockSpec((tr, cols), lambda i: (i, 0)),
        out_shape=jax.ShapeDtypeStruct((rows, cols), F32),
        compiler_params=_cp(("parallel",)))(parts)


def _rows128(a, pad_to=8):
    flat = a.reshape(-1)
    rows = -(-flat.shape[0] // 128)
    rows = -(-rows // pad_to) * pad_to
    flat = jnp.pad(flat, (0, rows * 128 - flat.shape[0]))
    return flat.reshape(rows, 128)


def _pack(arrays):
    return jnp.concatenate([_rows128(a) for a in arrays], axis=0)


def _unpack(packed, like):
    out, at = [], 0
    for a in like:
        size = 1
        for s in a.shape:
            size *= s
        rows = -(-(-(-size // 128)) // 8) * 8
        out.append(packed[at:at + rows].reshape(-1)[:size].reshape(a.shape))
        at += rows
    return out


def kernel(x, norm_mix_pre, norm_mix_post, norm_mlp_pre, norm_mlp_post, w_in_ab, w_spatial, b_spatial, conv_w, w_out_ab, w_qkv, w_o, w_up, w_down, loss_target, m_norm_mix_pre, m_norm_mix_post, m_norm_mlp_pre, m_norm_mlp_post, m_w_in_ab, m_w_spatial, m_b_spatial, m_conv_w, m_w_out_ab, m_w_qkv, m_w_o, m_w_up, m_w_down, v_norm_mix_pre, v_norm_mix_post, v_norm_mlp_pre, v_norm_mlp_post, v_w_in_ab, v_w_spatial, v_b_spatial, v_conv_w, v_w_out_ab, v_w_qkv, v_w_o, v_w_up, v_w_down):
    depth = norm_mix_pre.shape[0]
    seq, dm = x.shape[1], x.shape[2]
    h0 = x.reshape(seq, dm)
    target = loss_target.reshape(seq, dm)
    ax, ay, ac = lax.axis_index("x"), lax.axis_index("y"), lax.axis_index("c")
    my_block = 4 * ax + 2 * ay + ac
    block = jnp.reshape(my_block, (1,)).astype(jnp.int32)

    half = HEAD // 2
    inv_freq = ROPE_THETA ** (-jnp.arange(half, dtype=F32) * 2.0 / HEAD)
    ang = jnp.arange(seq, dtype=jnp.int32).astype(F32)[:, None] * inv_freq[None, :]
    ang = ang.reshape(seq // SEGS, SEGS, half).transpose(1, 0, 2).reshape(seq, half)
    cosf = jnp.concatenate([jnp.cos(ang), jnp.cos(ang)], axis=-1)
    sins = jnp.concatenate([-jnp.sin(ang), jnp.sin(ang)], axis=-1)
    band_bias = _band_bias()

    big = {"w_in_ab": w_in_ab, "w_out_ab": w_out_ab, "w_qkv": w_qkv, "w_o": w_o, "w_up": w_up, "w_down": w_down}
    use_order = []
    for l in range(depth):
        use_order += [("w_in_ab", l // 2), ("w_out_ab", l // 2)] if l % 2 == 0 else [("w_qkv", l // 2), ("w_o", l // 2)]
        use_order += [("w_up", l), ("w_down", l)]
    n_even = w_in_ab.shape[0]
    cw_rows = jnp.pad(conv_w.reshape(n_even * CONV_TAPS, conv_w.shape[2]), ((0, HALO - (n_even * CONV_TAPS) % HALO), (0, 0)))
    cw_gathered = _all_gather("ag_conv", [cw_rows])[0]
    first = [k for k in use_order if k in (("w_in_ab", 0), ("w_out_ab", 0), ("w_up", 0), ("w_down", 0))]
    rest = [k for k in use_order if k not in first]
    lands_a, sems_a = _ag_start("ag_start_first", [_cast_fill(f"cast_{nm}_{l}", big[nm], l, block) for nm, l in first],
                                cw_gathered)
    lands_b, sems_b = _ag_start("ag_start_rest", [_cast_fill(f"cast_{nm}_{l}", big[nm], l, block) for nm, l in rest],
                                lands_a[0])
    lands = dict(zip(first + rest, list(lands_a) + list(lands_b)))
    ag_sems = dict(zip(first + rest, list(sems_a) + list(sems_b)))
    passed_on, wg = [], {}

    def weight(key, after):
        pins = []
        if key not in wg:
            upto = min(use_order.index(key) + 1, len(use_order) - 1)
            for k in use_order[len(passed_on):upto + 1]:
                lands[k] = _ag_mid(f"ag_mid_{k[0]}_{k[1]}", lands[k], ag_sems[k], after)
                passed_on.append(k)
                if k != key and use_order.index(k) >= AG_PIN_FROM:
                    pins.append(lands[k])
            wg[key] = _ag_wait(f"ag_wait_{key[0]}_{key[1]}", lands[key], ag_sems[key], after)
        return wg[key], pins

    cw_all = cw_gathered[:, :n_even * CONV_TAPS].reshape(NDEV, n_even, CONV_TAPS, -1)
    cw_all = jnp.transpose(cw_all, (1, 2, 0, 3)).reshape(n_even, CONV_TAPS, -1)
    cw_full = [jnp.pad(cw_all[e], ((0, HALO - CONV_TAPS), (0, 0))) for e in range(n_even)]

    def rows_nat(blk):
        return blk.reshape(blk.shape[0] * blk.shape[1], blk.shape[2])

    saved = []
    hn = _norm_fwd("norm_first", h0, g_pre=norm_mix_pre[0][None])[0]
    h = h0
    for l in range(depth):
        s = {"h_in": h, "hn1": hn}
        if l % 2 == 0:
            e = l // 2
            w_, pins = weight(("w_in_ab", e), hn)
            proj = _mm_nn_blk(f"fwd_in_{l}", hn, w_, after=pins)
            ab = _gate_fwd(f"gate_fwd_{l}", proj, w_spatial[e], b_spatial[e].T, cw_full[e])
            w_, pins = weight(("w_out_ab", e), ab)
            mix = _mm_nn(f"fwd_out_{l}", ab, rows_nat(w_), after=pins)
            s.update(proj=proj, ab=ab)
        else:
            o_ = l // 2
            w_, pins = weight(("w_qkv", o_), hn)
            qkv = _mm_nn_blk(f"fwd_qkv_{l}", hn, w_, after=pins)
            att, lse = _attn_fwd(f"attn_fwd_{l}", qkv, cosf, sins, band_bias)
            w_, pins = weight(("w_o", o_), att)
            mix = _mm_nn(f"fwd_o_{l}", att, rows_nat(w_), after=pins)
            s.update(qkv=qkv, att=att, lse=lse)
        h1, hn2 = _norm_fwd(f"norm_mid_{l}", h, mix, norm_mix_post[l][None], norm_mlp_pre[l][None], seg_z=l % 2 == 1)
        w_, pins = weight(("w_up", l), hn2)
        act, act_grad = _mm_nn_blk(f"fwd_up_{l}", hn2, w_, relu2=True, after=pins)
        w_, pins = weight(("w_down", l), act)
        f = _mm_nn(f"fwd_down_{l}", act, rows_nat(w_), after=pins)
        s.update(mix=mix, h1=h1, hn2=hn2, act=act, act_grad=act_grad, f=f)
        if l + 1 < depth:
            h, hn = _norm_fwd(f"norm_end_{l}", h1, f, norm_mlp_post[l][None], norm_mix_pre[l + 1][None],
                              seg_y=(l + 1) % 2 == 1)
        else:
            h = _norm_fwd(f"norm_end_{l}", h1, f, norm_mlp_post[l][None])[0]
        saved.append(s)

    d_h, loss_row = _loss_grad("loss", h, target)
    rs = {}

    def scatter(key, g):
        rs[key] = _rs_start(f"rs_start_{key[0]}_{key[1]}", g.reshape(NDEV, -1, g.shape[-1]))

    dg ={nm: [None] * depth for nm in ("norm_mix_pre", "norm_mix_post", "norm_mlp_pre", "norm_mlp_post")}
    d_ws, d_bs, d_cw = [None] * n_even, [None] * n_even, [None] * n_even
    d_hn_next = None
    for l in reversed(range(depth)):
        s = saved[l]
        if l == depth - 1:
            d_f, dg["norm_mlp_post"][l] = _norm_bwd(f"nb_end_{l}", d_h, post=(s["f"], norm_mlp_post[l][None]))
        else:
            d_h, dg["norm_mix_pre"][l + 1], d_f, dg["norm_mlp_post"][l] = _norm_bwd(
                f"nb_end_{l}", d_h, pre=(d_hn_next, saved[l + 1]["h_in"], norm_mix_pre[l + 1][None]),
                post=(s["f"], norm_mlp_post[l][None]), seg_dy=(l + 1) % 2 == 1)
        wd = rows_nat(wg[("w_down", l)])
        d_up = _mm_nt_rows(f"bwd_down_{l}", d_f, wd, act=s["act_grad"])
        scatter(("w_down", l), _mm_tn(f"gw_down_{l}", s["act"], d_f))
        scatter(("w_up", l), _mm_tn(f"gw_up_{l}", s["hn2"], d_up, nb=w_up.shape[2]))
        d_hn2 = _mm_nt_blk(f"bwd_up_{l}", d_up, wg[("w_up", l)], after=[rs[("w_down", l)][0], rs[("w_up", l)][0]])
        d_h, dg["norm_mlp_pre"][l], d_mix, dg["norm_mix_post"][l] = _norm_bwd(
            f"nb_mid_{l}", d_h, pre=(d_hn2, s["h1"], norm_mlp_pre[l][None]),
            post=(s["mix"], norm_mix_post[l][None]), seg_z=l % 2 == 1)
        if l % 2 == 0:
            e = l // 2
            wo = rows_nat(wg[("w_out_ab", e)])
            d_ab = _mm_nt_rows(f"bwd_out_{l}", d_mix, wo)
            scatter(("w_out_ab", e), _mm_tn(f"gw_out_{l}", s["ab"], d_mix))
            d_proj, d_ws[e], d_bs[e], d_cw[e] = _gate_bwd(
                f"gate_bwd_{l}", s["proj"], d_ab, w_spatial[e], b_spatial[e].T, cw_full[e])
            scatter(("w_in_ab", e), _mm_tn(f"gw_in_{l}", s["hn1"], d_proj, nb=w_in_ab.shape[2]))
            d_hn_next = _mm_nt_blk(f"bwd_in_{l}", d_proj, wg[("w_in_ab", e)],
                                   after=[rs[("w_out_ab", e)][0], rs[("w_in_ab", e)][0]])
        else:
            o_ = l // 2
            wo = rows_nat(wg[("w_o", o_)])
            d_att = _mm_nt_rows(f"bwd_o_{l}", d_mix, wo, out_dtype=F32)
            scatter(("w_o", o_), _mm_tn(f"gw_o_{l}", s["att"], d_mix))
            parts = _attn_bwd(f"attn_bwd_{l}", s["qkv"], cosf, sins, band_bias, d_att, s["att"], s["lse"])
            d_qkv = _attn_merge(f"attn_merge_{l}", *parts)
            scatter(("w_qkv", o_), _mm_tn(f"gw_qkv_{l}", s["hn1"], d_qkv, nb=w_qkv.shape[2]))
            d_hn_next = _mm_nt_blk(f"bwd_qkv_{l}", d_qkv, wg[("w_qkv", o_)],
                                   after=[rs[("w_o", o_)][0], rs[("w_qkv", o_)][0]])
    grad_x, dg["norm_mix_pre"][0] = _norm_bwd("nb_first", d_h, pre=(d_hn_next, h0, norm_mix_pre[0][None]))

    small_g = ([jnp.concatenate(dg[nm], axis=0) for nm in dg]
               + [jnp.stack(d_ws), jnp.stack(d_bs), jnp.stack([c[:CONV_TAPS] for c in d_cw]), loss_row])
    small_land, small_send, small_recv = _ag_direct_start("ag_small_start", _fill_slot("fill_small", _pack(small_g), block))

    moments = {"w_in_ab": (m_w_in_ab, v_w_in_ab), "w_out_ab": (m_w_out_ab, v_w_out_ab), "w_qkv": (m_w_qkv, v_w_qkv),
               "w_o": (m_w_o, v_w_o), "w_up": (m_w_up, v_w_up), "w_down": (m_w_down, v_w_down)}
    out_big = {}
    behind = small_land
    for nm in ("w_o", "w_qkv", "w_down", "w_up", "w_out_ab", "w_in_ab"):
        own, landed = [], []
        for l in range(big[nm].shape[0]):
            g, land = _wait_all(f"rs_wait_{nm}_{l}", *rs[(nm, l)], behind)
            own.append(g)
            landed.append(land)
        out_big[nm] = _adamw_layers(f"adamw_{nm}", own, landed, block, big[nm], moments[nm][0], moments[nm][1])
        behind = out_big[nm][0]

    summed = _sum_parts("sum_small", _ag_direct_wait("ag_small_wait", small_land, small_send, small_recv, behind))
    g_nmp, g_nmo, g_nlp, g_nlo, g_ws, g_bs, g_cw_all, loss_sum = _unpack(summed, small_g)
    loss = loss_sum[0, 0]
    cwb = conv_w.shape[2]
    g_cw = lax.dynamic_slice_in_dim(g_cw_all, my_block * cwb, cwb, axis=2)
    small_w = [norm_mix_pre, norm_mix_post, norm_mlp_pre, norm_mlp_post, w_spatial, b_spatial, conv_w]
    small_m = [m_norm_mix_pre, m_norm_mix_post, m_norm_mlp_pre, m_norm_mlp_post, m_w_spatial, m_b_spatial, m_conv_w]
    small_v = [v_norm_mix_pre, v_norm_mix_post, v_norm_mlp_pre, v_norm_mlp_post, v_w_spatial, v_b_spatial, v_conv_w]
    small_grad = [g_nmp, g_nmo, g_nlp, g_nlo, g_ws, g_bs, g_cw]
    upd = _adamw("adamw_small", _pack(small_grad)[None], _pack(small_w), _pack(small_m), _pack(small_v))
    sg, sd, sm, sv = [_unpack(u, small_w) for u in upd]

    def outs(i_small, i_big):
        return (i_small[0], i_small[1], i_small[2], i_small[3], i_big["w_in_ab"], i_small[4], i_small[5], i_small[6],
                i_big["w_out_ab"], i_big["w_qkv"], i_big["w_o"], i_big["w_up"], i_big["w_down"])

    pick = lambda i: {nm: out_big[nm][i] for nm in big}
    return (loss, grad_x.reshape(x.shape), *outs(sg, pick(0)), *outs(sd, pick(1)), *outs(sm, pick(2)),
            *outs(sv, pick(3)))
```

```python
import functools

import jax
import jax.numpy as jnp
from jax import lax
from jax.experimental import pallas as pl
from jax.experimental.pallas import tpu as pltpu

F32 = jnp.float32
BF16 = jnp.bfloat16
MESH = pl.DeviceIdType.MESH
ANY = pl.BlockSpec(memory_space=pl.ANY)
HBM = pl.BlockSpec(memory_space=pltpu.HBM)
SEM = pl.BlockSpec(memory_space=pltpu.SEMAPHORE)
EFFECT = pltpu.SideEffectType.DATAFLOW_SIDE_EFFECTING

NDEV = 8
RMS_EPS = 1e-6
LN_EPS = 1e-5
CHUNK = 128
HEAD = 128
ATT_TILE = 2048
ATT_UNROLL = 16
DILATIONS = (1, 4, 16)
SEGS = 16
ROPE_THETA = 10000.0
CONV_TAPS = 3
HALO = 8
HALO_BF16 = 16
GELU_C = 0.7978845608028654
GELU_A = 0.044715
ADAM_LR, ADAM_B1, ADAM_B2, ADAM_EPS, ADAM_WD, ADAM_STEP = 0.001, 0.9, 0.999, 1e-08, 0.01, 10
VMEM_MM = 52 << 20
VMEM_EW = 40 << 20


def _cp(sem=None, vmem=VMEM_EW):
    if sem is None:
        return pltpu.CompilerParams(vmem_limit_bytes=vmem)
    return pltpu.CompilerParams(dimension_semantics=sem, vmem_limit_bytes=vmem)


def _tile(n, want):
    return want if n % want == 0 else n


def _all_gather(name, shards, after=()):
    n = len(shards)
    after = list(after)

    def body(*refs):
        ins, outs = refs[:n], refs[n + len(after):2 * n + len(after)]
        send_sems, recv_sems, local_sems = refs[2 * n + len(after):]
        x, y, c = lax.axis_index("x"), lax.axis_index("y"), lax.axis_index("c")
        me, sibling = (x, y, c), (x, y, 1 - c)
        chips = [(1 - x, y), (x, 1 - y), (1 - x, 1 - y)]

        def slot(p):
            return 4 * p[0] + 2 * p[1] + p[2]

        def copy(i, k, block, to, src=None):
            dst = outs[i].at[slot(block)]
            return pltpu.make_async_remote_copy(
                src_ref=dst if src is None else src, dst_ref=dst,
                send_sem=send_sems.at[i, k], recv_sem=recv_sems.at[i, k],
                device_id=to, device_id_type=MESH)

        mine = [pltpu.make_async_copy(ins[i], outs[i].at[slot(me)], local_sems.at[i]) for i in range(n)]
        for cp in mine:
            cp.start()
        first = []
        for i in range(n):
            first.append(copy(i, 0, me, sibling, src=ins[i]))
            for j, chip in enumerate(chips):
                first.append(copy(i, 1 + j, me, (*chip, c), src=ins[i]))
        for cp in first:
            cp.start()
        passed = []
        for j, chip in enumerate(chips):
            for i in range(n):
                copy(i, 1 + j, (*chip, c), me).wait_recv()
                fwd = copy(i, 4 + j, (*chip, c), sibling)
                fwd.start()
                passed.append(fwd)
        for i in range(n):
            copy(i, 0, sibling, me).wait_recv()
            for j, chip in enumerate(chips):
                copy(i, 4 + j, (*chip, 1 - c), me).wait_recv()
        for cp in first + passed:
            cp.wait_send()
        for cp in mine:
            cp.wait()

    return pl.pallas_call(
        body, name=name,
        out_shape=[jax.ShapeDtypeStruct((NDEV,) + s.shape, s.dtype) for s in shards],
        in_specs=[ANY] * (n + len(after)), out_specs=[ANY] * n,
        scratch_shapes=[pltpu.SemaphoreType.DMA((n, 7)), pltpu.SemaphoreType.DMA((n, 7)),
                        pltpu.SemaphoreType.DMA((n,))],
    )(*shards, *after)


def _peer(x, y, c, r):
    return (1 - x if r & 4 else x, 1 - y if r & 2 else y, 1 - c if r & 1 else c)


def _slot(p):
    return 4 * p[0] + 2 * p[1] + p[2]


def _cast_fill(name, w, layer, block):
    _, rows, cols = w.shape
    tr = _tile(rows, 256)

    def body(blk_ref, w_ref, o_ref):
        o_ref[...] = w_ref[...].astype(BF16)

    return pl.pallas_call(
        body, name=name,
        grid_spec=pltpu.PrefetchScalarGridSpec(
            num_scalar_prefetch=1, grid=(rows // tr,),
            in_specs=[pl.BlockSpec((None, tr, cols), lambda i, blk: (layer, i, 0))],
            out_specs=pl.BlockSpec((None, tr, cols), lambda i, blk: (blk[0], i, 0))),
        out_shape=jax.ShapeDtypeStruct((NDEV, rows, cols), BF16),
        compiler_params=_cp(("parallel",)))(block, w)


OTHER_CHIPS = (2, 4, 6)
AG_SEMS = 6
AG_PIN_FROM = 5


def _ag_start(name, lands, after):
    n = len(lands)

    def body(*refs):
        ins, sems = refs[:n], refs[n + 1:n + 1 + AG_SEMS * n]
        x, y, c = lax.axis_index("x"), lax.axis_index("y"), lax.axis_index("c")
        mine = _slot((x, y, c))
        for i in range(n):
            send_a, *recv_a, _, recv_b = sems[AG_SEMS * i:AG_SEMS * (i + 1)]
            block = ins[i].at[mine]
            pltpu.make_async_remote_copy(src_ref=block, dst_ref=block, send_sem=send_a, recv_sem=recv_b,
                                         device_id=_peer(x, y, c, 1), device_id_type=MESH).start()
            for k, r in enumerate(OTHER_CHIPS):
                pltpu.make_async_remote_copy(src_ref=block, dst_ref=block, send_sem=send_a, recv_sem=recv_a[k],
                                             device_id=_peer(x, y, c, r), device_id_type=MESH).start()

    outs = pl.pallas_call(
        body, name=name,
        out_shape=[pltpu.SemaphoreType.DMA(())] * (AG_SEMS * n) + [pltpu.HBM(a.shape, a.dtype) for a in lands],
        in_specs=[HBM] * n + [ANY], out_specs=[SEM] * (AG_SEMS * n) + [HBM] * n,
        input_output_aliases={i: AG_SEMS * n + i for i in range(n)},
        compiler_params=pltpu.CompilerParams(has_side_effects=EFFECT),
    )(*[pltpu.with_memory_space_constraint(a, pltpu.HBM) for a in lands], after)
    return outs[AG_SEMS * n:], [tuple(outs[AG_SEMS * i:AG_SEMS * (i + 1)]) for i in range(n)]


def _ag_mid(name, land, sems, after):
    _, *recv_a, send_b, recv_b = sems

    def body(land_ref, ra0, ra1, ra2, send_b_ref, recv_b_ref, after_ref, land_out):
        x, y, c = lax.axis_index("x"), lax.axis_index("y"), lax.axis_index("c")
        sibling = _peer(x, y, c, 1)
        for arrival, r in zip((ra0, ra1, ra2), OTHER_CHIPS):
            block = land_ref.at[_slot(_peer(x, y, c, r))]
            pltpu.make_async_remote_copy(src_ref=block, dst_ref=block, send_sem=send_b_ref, recv_sem=arrival,
                                         device_id=sibling, device_id_type=MESH).wait_recv()
            pltpu.make_async_remote_copy(src_ref=block, dst_ref=block, send_sem=send_b_ref, recv_sem=recv_b_ref,
                                         device_id=sibling, device_id_type=MESH).start()

    return pl.pallas_call(
        body, name=name, out_shape=pltpu.HBM(land.shape, land.dtype),
        in_specs=[HBM] + [SEM] * 5 + [ANY], out_specs=HBM, input_output_aliases={0: 0},
        compiler_params=pltpu.CompilerParams(has_side_effects=EFFECT),
    )(land, *recv_a, send_b, recv_b, after)


def _ag_wait(name, land, sems, after):
    send_a, _, _, _, send_b, recv_b = sems

    def body(land_ref, send_a_ref, send_b_ref, recv_b_ref, after_ref, land_out):
        x, y, c = lax.axis_index("x"), lax.axis_index("y"), lax.axis_index("c")
        sibling = _peer(x, y, c, 1)
        four = land_ref.at[pl.ds(0, 1 + len(OTHER_CHIPS))]
        three = land_ref.at[pl.ds(0, len(OTHER_CHIPS))]
        first = pltpu.make_async_remote_copy(src_ref=four, dst_ref=four, send_sem=send_a_ref, recv_sem=recv_b_ref,
                                             device_id=sibling, device_id_type=MESH)
        passed = pltpu.make_async_remote_copy(src_ref=three, dst_ref=three, send_sem=send_b_ref, recv_sem=recv_b_ref,
                                              device_id=sibling, device_id_type=MESH)
        first.wait_send()
        passed.wait_send()
        first.wait_recv()

    return pl.pallas_call(
        body, name=name, out_shape=pltpu.HBM(land.shape, land.dtype),
        in_specs=[HBM, SEM, SEM, SEM, ANY], out_specs=HBM, input_output_aliases={0: 0},
        compiler_params=pltpu.CompilerParams(has_side_effects=EFFECT),
    )(land, send_a, send_b, recv_b, after)


def _fill_slot(name, rows, block):
    r, c = rows.shape

    def body(blk_ref, i_ref, o_ref):
        o_ref[...] = i_ref[...]

    return pl.pallas_call(
        body, name=name,
        grid_spec=pltpu.PrefetchScalarGridSpec(
            num_scalar_prefetch=1, grid=(1,),
            in_specs=[pl.BlockSpec((r, c), lambda i, blk: (0, 0))],
            out_specs=pl.BlockSpec((None, r, c), lambda i, blk: (blk[0], 0, 0))),
        out_shape=jax.ShapeDtypeStruct((NDEV, r, c), rows.dtype),
        compiler_params=_cp(("arbitrary",)))(block, rows)


def _ag_direct_start(name, land):
    def body(land_ref, send, recv, land_out):
        x, y, c = lax.axis_index("x"), lax.axis_index("y"), lax.axis_index("c")
        block = land_ref.at[_slot((x, y, c))]
        for r in range(1, NDEV):
            pltpu.make_async_remote_copy(src_ref=block, dst_ref=block, send_sem=send, recv_sem=recv,
                                         device_id=_peer(x, y, c, r), device_id_type=MESH).start()

    send, recv, land_thru = pl.pallas_call(
        body, name=name,
        out_shape=[pltpu.SemaphoreType.DMA(()), pltpu.SemaphoreType.DMA(()), pltpu.HBM(land.shape, land.dtype)],
        in_specs=[HBM], out_specs=[SEM, SEM, HBM], input_output_aliases={0: 2},
        compiler_params=pltpu.CompilerParams(has_side_effects=EFFECT),
    )(pltpu.with_memory_space_constraint(land, pltpu.HBM))
    return land_thru, send, recv


def _ag_direct_wait(name, land, send, recv, after):
    def body(land_ref, send_ref, recv_ref, after_ref, land_out):
        x, y, c = lax.axis_index("x"), lax.axis_index("y"), lax.axis_index("c")
        seven = land_ref.at[pl.ds(0, NDEV - 1)]
        copy = pltpu.make_async_remote_copy(src_ref=seven, dst_ref=seven, send_sem=send_ref, recv_sem=recv_ref,
                                            device_id=_peer(x, y, c, 1), device_id_type=MESH)
        copy.wait_send()
        copy.wait_recv()

    return pl.pallas_call(
        body, name=name, out_shape=pltpu.HBM(land.shape, land.dtype),
        in_specs=[HBM, SEM, SEM, ANY], out_specs=HBM, input_output_aliases={0: 0},
        compiler_params=pltpu.CompilerParams(has_side_effects=EFFECT),
    )(land, send, recv, after)


def _wait_all(name, src, land, send, recv, after):
    def body(src_ref, land_ref, send_ref, recv_ref, after_ref, src_out, land_out):
        x, y, c = lax.axis_index("x"), lax.axis_index("y"), lax.axis_index("c")
        seven = land_ref.at[pl.ds(0, NDEV - 1)]
        copy = pltpu.make_async_remote_copy(src_ref=seven, dst_ref=seven, send_sem=send_ref, recv_sem=recv_ref,
                                            device_id=_peer(x, y, c, 1), device_id_type=MESH)
        copy.wait_send()
        copy.wait_recv()

    return pl.pallas_call(
        body, name=name,
        out_shape=[pltpu.HBM(src.shape, src.dtype), pltpu.HBM(land.shape, land.dtype)],
        in_specs=[HBM, HBM, SEM, SEM, ANY], out_specs=[HBM, HBM],
        input_output_aliases={0: 0, 1: 1},
        compiler_params=pltpu.CompilerParams(has_side_effects=EFFECT),
    )(src, land, send, recv, after)


def _rs_start(name, grad):
    land = lax.empty((NDEV - 1,) + grad.shape[1:], grad.dtype)

    def body(g_ref, land_ref, send, recv, g_out, land_out):
        x, y, c = lax.axis_index("x"), lax.axis_index("y"), lax.axis_index("c")
        for r in range(1, NDEV):
            peer = _peer(x, y, c, r)
            pltpu.make_async_remote_copy(
                src_ref=g_ref.at[_slot(peer)], dst_ref=land_ref.at[r - 1], send_sem=send, recv_sem=recv,
                device_id=peer, device_id_type=MESH).start()

    send, recv, g_thru, land_thru = pl.pallas_call(
        body, name=name,
        out_shape=[pltpu.SemaphoreType.DMA(()), pltpu.SemaphoreType.DMA(()),
                   pltpu.HBM(grad.shape, grad.dtype), pltpu.HBM(land.shape, land.dtype)],
        in_specs=[HBM, HBM], out_specs=[SEM, SEM, HBM, HBM], input_output_aliases={0: 2, 1: 3},
        compiler_params=pltpu.CompilerParams(has_side_effects=EFFECT),
    )(pltpu.with_memory_space_constraint(grad, pltpu.HBM), pltpu.with_memory_space_constraint(land, pltpu.HBM))
    return g_thru, land_thru, send, recv


def _adam_math(w, g, m, v):
    m = ADAM_B1 * m + (1.0 - ADAM_B1) * g
    v = ADAM_B2 * v + (1.0 - ADAM_B2) * (g * g)
    m_hat = m / (1.0 - ADAM_B1 ** ADAM_STEP)
    v_hat = v / (1.0 - ADAM_B2 ** ADAM_STEP)
    delta = -ADAM_LR * (m_hat / (jnp.sqrt(v_hat) + ADAM_EPS) + ADAM_WD * w)
    return delta, m, v


def _adamw(name, parts, w, m, v):
    nparts, rows, cols = parts.shape
    tr = _tile(rows, 256)

    def body(p_ref, w_ref, m_ref, v_ref, g_out, d_out, m_out, v_out):
        g = p_ref[0].astype(F32)
        for k in range(1, nparts):
            g = g + p_ref[k].astype(F32)
        delta, mn, vn = _adam_math(w_ref[...], g, m_ref[...], v_ref[...])
        g_out[...] = g
        d_out[...] = delta
        m_out[...] = mn
        v_out[...] = vn

    row = pl.BlockSpec((tr, cols), lambda i: (i, 0))
    return pl.pallas_call(
        body, name=name, grid=(rows // tr,),
        in_specs=[pl.BlockSpec((nparts, tr, cols), lambda i: (0, i, 0)), row, row, row],
        out_specs=[row] * 4,
        out_shape=[jax.ShapeDtypeStruct((rows, cols), F32)] * 4,
        compiler_params=_cp(("parallel",)),
    )(parts, w, m, v)


def _adamw_layers(name, grads, lands, block, w, m, v):
    layers, rows, cols = w.shape
    nland = lands[0].shape[0]
    tr = rows
    while tr % 2 == 0 and tr > 8 and nland * tr * cols * 2 > (2 << 20):
        tr //= 2

    def body(blk_ref, *refs):
        own_refs, land_refs = refs[:layers], refs[layers:2 * layers]
        w_ref, m_ref, v_ref, g_out, d_out, m_out, v_out = refs[2 * layers:]
        layer = pl.program_id(0)
        for k in range(layers):
            @pl.when(layer == k)
            def _(k=k):
                g = own_refs[k][...].astype(F32)
                for s in range(nland):
                    g = g + land_refs[k][s].astype(F32)
                delta, mn, vn = _adam_math(w_ref[...], g, m_ref[...], v_ref[...])
                g_out[...] = g
                d_out[...] = delta
                m_out[...] = mn
                v_out[...] = vn

    def own_spec(k):
        return pl.BlockSpec((None, tr, cols), lambda l, i, blk: (blk[0], jnp.where(l == k, i, 0), 0))

    def land_spec(k):
        return pl.BlockSpec((nland, tr, cols), lambda l, i, blk: (0, jnp.where(l == k, i, 0), 0))

    row = pl.BlockSpec((None, tr, cols), lambda l, i, blk: (l, i, 0))
    return pl.pallas_call(
        body, name=name,
        grid_spec=pltpu.PrefetchScalarGridSpec(
            num_scalar_prefetch=1, grid=(layers, rows // tr),
            in_specs=[own_spec(k) for k in range(layers)] + [land_spec(k) for k in range(layers)] + [row, row, row],
            out_specs=[row] * 4),
        out_shape=[jax.ShapeDtypeStruct((layers, rows, cols), F32)] * 4,
        compiler_params=_cp(("arbitrary", "arbitrary")),
    )(block, *grads, *lands, w, m, v)


LANES = 128


def _seg_scratch(rows, d):
    return pltpu.VMEM((d // LANES, rows, LANES), F32)


def _to_segments(vals, scratch, out_ref):
    per = scratch.shape[1] // SEGS
    for c in range(scratch.shape[0]):
        cols = slice(c * LANES, (c + 1) * LANES)
        scratch[c] = vals[:, cols]
        for s in range(SEGS):
            out_ref[s, :, cols] = scratch.at[c][pl.ds(s, per, stride=SEGS), :].astype(out_ref.dtype)


def _from_segments(in_ref, scratch):
    per = scratch.shape[1] // SEGS
    for c in range(scratch.shape[0]):
        for s in range(SEGS):
            scratch.at[c][pl.ds(s, per, stride=SEGS), :] = in_ref[s, :, c * LANES:(c + 1) * LANES].astype(F32)
    return jnp.concatenate([scratch[c] for c in range(scratch.shape[0])], axis=1)


def _seg_view(a):
    return a.reshape(SEGS, a.shape[0] // SEGS, a.shape[1])


def _norm_fwd(name, h, z=None, g_post=None, g_pre=None, seg_z=False, seg_y=False):
    rows, d = h.shape
    tm = _tile(rows, 256)
    has_post, has_pre = z is not None, g_pre is not None
    nscratch = int(seg_z) + int(seg_y)

    def body_by_chunks(*refs):
        it = iter(refs)
        h_ref = next(it)
        if has_post:
            z_ref, gp_ref = next(it), next(it)
        if has_pre:
            gq_ref = next(it)
        hn_ref = next(it) if has_post else None
        y_ref = next(it) if has_pre else None

        def chunk(c, carry):
            rws = pl.ds(pl.multiple_of(c * HALO_BF16, HALO_BF16), HALO_BF16)
            hv = h_ref[rws, :]
            if has_post:
                zv = z_ref[rws, :].astype(F32)
                r = lax.rsqrt(jnp.mean(zv * zv, axis=-1, keepdims=True) + RMS_EPS)
                hv = hv + (zv * r) * gp_ref[...]
                hn_ref[rws, :] = hv
            if has_pre:
                r = lax.rsqrt(jnp.mean(hv * hv, axis=-1, keepdims=True) + RMS_EPS)
                y_ref[rws, :] = ((hv * r) * gq_ref[...]).astype(BF16)
            return carry

        lax.fori_loop(0, tm // HALO_BF16, chunk, 0)

    def body(*refs):
        if nscratch == 0 and tm % HALO_BF16 == 0:
            return body_by_chunks(*refs)
        scratch = list(refs[len(refs) - nscratch:])
        it = iter(refs)
        hv = next(it)[...]
        if has_post:
            z_ref = next(it)
            zv = _from_segments(z_ref, scratch.pop(0)) if seg_z else z_ref[...].astype(F32)
            gp = next(it)[...]
        if has_pre:
            gq = next(it)[...]
        if has_post:
            r = lax.rsqrt(jnp.mean(zv * zv, axis=-1, keepdims=True) + RMS_EPS)
            hv = hv + (zv * r) * gp
            next(it)[...] = hv
        if has_pre:
            r = lax.rsqrt(jnp.mean(hv * hv, axis=-1, keepdims=True) + RMS_EPS)
            y = (hv * r) * gq
            if seg_y:
                _to_segments(y, scratch.pop(0), next(it))
            else:
                next(it)[...] = y.astype(BF16)

    row = pl.BlockSpec((tm, d), lambda i: (i, 0))
    seg = pl.BlockSpec((SEGS, tm // SEGS, d), lambda i: (0, i, 0))
    vec = pl.BlockSpec((1, d), lambda i: (0, 0))
    ins, in_specs, out_shape, out_specs = [h], [row], [], []
    if has_post:
        ins += [_seg_view(z) if seg_z else z, g_post]
        in_specs += [seg if seg_z else row, vec]
        out_shape.append(jax.ShapeDtypeStruct((rows, d), F32))
        out_specs.append(row)
    if has_pre:
        ins.append(g_pre)
        in_specs.append(vec)
        out_shape.append(jax.ShapeDtypeStruct((SEGS, rows // SEGS, d) if seg_y else (rows, d), BF16))
        out_specs.append(seg if seg_y else row)
    outs = pl.pallas_call(body, name=name, grid=(rows // tm,), in_specs=in_specs, out_specs=out_specs,
                          out_shape=out_shape, scratch_shapes=[_seg_scratch(tm, d)] * nscratch,
                          compiler_params=_cp(("parallel",)))(*ins)
    if seg_y:
        outs = list(outs[:-1]) + [outs[-1].reshape(rows, d)]
    return outs


def _rms_bwd_rows(x, g, dy):
    r = lax.rsqrt(jnp.mean(x * x, axis=-1, keepdims=True) + RMS_EPS)
    xn = x * r
    dg = jnp.sum(dy * xn, axis=0, keepdims=True)
    dxn = dy * g
    dx = r * (dxn - xn * jnp.mean(dxn * xn, axis=-1, keepdims=True))
    return dx, dg


def _norm_bwd(name, d_out, pre=None, post=None, seg_dy=False, seg_z=False):
    rows, d = d_out.shape
    tm = _tile(rows, 256)
    has_pre, has_post = pre is not None, post is not None
    nscratch = int(seg_dy) + 2 * int(seg_z)

    def body(*refs):
        scratch = list(refs[len(refs) - nscratch:])
        it = iter(refs)
        dres = next(it)[...]
        if has_pre:
            dy_ref = next(it)
            dy = _from_segments(dy_ref, scratch.pop(0)) if seg_dy else dy_ref[...].astype(F32)
            xp, gq = next(it)[...], next(it)[...]
        if has_post:
            z_ref = next(it)
            zv = _from_segments(z_ref, scratch.pop(0)) if seg_z else z_ref[...].astype(F32)
            gp = next(it)[...]
        first = pl.program_id(0) == 0
        if has_pre:
            dx, dg = _rms_bwd_rows(xp, gq, dy)
            dres = dres + dx
            next(it)[...] = dres
            dg_ref = next(it)

            @pl.when(first)
            def _():
                dg_ref[...] = jnp.zeros_like(dg_ref)
            dg_ref[...] += dg
        if has_post:
            dz, dg2 = _rms_bwd_rows(zv, gp, dres)
            if seg_z:
                _to_segments(dz, scratch.pop(0), next(it))
            else:
                next(it)[...] = dz.astype(BF16)
            dg2_ref = next(it)

            @pl.when(first)
            def _():
                dg2_ref[...] = jnp.zeros_like(dg2_ref)
            dg2_ref[...] += dg2

    row = pl.BlockSpec((tm, d), lambda i: (i, 0))
    seg = pl.BlockSpec((SEGS, tm // SEGS, d), lambda i: (0, i, 0))
    vec = pl.BlockSpec((1, d), lambda i: (0, 0))
    ins, in_specs, out_shape, out_specs = [d_out], [row], [], []
    if has_pre:
        d_y, x_pre, g_pre = pre
        ins += [_seg_view(d_y) if seg_dy else d_y, x_pre, g_pre]
        in_specs += [seg if seg_dy else row, row, vec]
        out_shape += [jax.ShapeDtypeStruct((rows, d), F32), jax.ShapeDtypeStruct((1, d), F32)]
        out_specs += [row, vec]
    if has_post:
        z, g_post = post
        ins += [_seg_view(z) if seg_z else z, g_post]
        in_specs += [seg if seg_z else row, vec]
        out_shape += [jax.ShapeDtypeStruct((SEGS, rows // SEGS, d) if seg_z else (rows, d), BF16),
                      jax.ShapeDtypeStruct((1, d), F32)]
        out_specs += [seg if seg_z else row, vec]
    outs = pl.pallas_call(body, name=name, grid=(rows // tm,), in_specs=in_specs, out_specs=out_specs,
                          out_shape=out_shape, scratch_shapes=[_seg_scratch(tm, d)] * nscratch,
                          compiler_params=_cp(("arbitrary",)))(*ins)
    if seg_z:
        outs = list(outs)
        outs[-2] = outs[-2].reshape(rows, d)
    return outs


def _loss_grad(name, y, target):
    rows, d = y.shape
    tm = _tile(rows, 256)

    def body(y_ref, t_ref, dy_ref, loss_ref):
        err = y_ref[...] - t_ref[...]
        dy_ref[...] = err * (1.0 / d)

        @pl.when(pl.program_id(0) == 0)
        def _():
            loss_ref[...] = jnp.zeros_like(loss_ref)
        loss_ref[...] += jnp.full(loss_ref.shape, (0.5 / d) * jnp.sum(err * err), F32)

    row = pl.BlockSpec((tm, d), lambda i: (i, 0))
    return pl.pallas_call(
        body, name=name, grid=(rows // tm,), in_specs=[row, row],
        out_specs=[row, pl.BlockSpec((1, 128), lambda i: (0, 0))],
        out_shape=[jax.ShapeDtypeStruct((rows, d), F32), jax.ShapeDtypeStruct((1, 128), F32)],
        compiler_params=_cp(("arbitrary",)))(y, target)


NT_DIMS = (((1,), (1,)), ((), ()))
TN_DIMS = (((0,), (0,)), ((), ()))


MXU_WIDTH = 256


def _blocks_per_step(nb):
    return 1 if nb % MXU_WIDTH == 0 else 2


def _mm_nn_blk(name, a, wblk, relu2=False, after=()):
    m, k = a.shape
    nb = wblk.shape[2]
    tm = _tile(m, 1024)
    per = _blocks_per_step(nb)
    after = list(after)

    def body(a_ref, w_ref, *rest):
        w = w_ref[0] if per == 1 else jnp.concatenate([w_ref[t] for t in range(per)], axis=1)
        r = jnp.dot(a_ref[...], w, preferred_element_type=F32)
        if relu2:
            rr = jnp.maximum(r, 0.0)
            rest[-2][...] = (rr * rr).astype(BF16)
            rest[-1][...] = (2.0 * rr).astype(BF16)
        else:
            rest[-1][...] = r.astype(BF16)

    out = jax.ShapeDtypeStruct((m, NDEV * nb), BF16)
    spec = pl.BlockSpec((tm, per * nb), lambda d, i: (i, d))
    return pl.pallas_call(
        body, name=name, grid=(NDEV // per, m // tm),
        in_specs=[pl.BlockSpec((tm, k), lambda d, i: (i, 0)), pl.BlockSpec((per, k, nb), lambda d, i: (d, 0, 0))]
        + [ANY] * len(after),
        out_specs=[spec, spec] if relu2 else spec,
        out_shape=[out, out] if relu2 else out,
        compiler_params=_cp(("parallel", "parallel"), VMEM_MM))(a, wblk, *after)


def _accumulate(acc, o_ref, r, step, last):
    if acc is None:
        o_ref[...] = r.astype(o_ref.dtype)
        return

    @pl.when(step == 0)
    def _():
        acc[...] = r

    @pl.when(jnp.logical_and(step > 0, step < last))
    def _():
        acc[...] += r

    @pl.when(jnp.logical_and(step > 0, step == last))
    def _():
        o_ref[...] = (acc[...] + r).astype(o_ref.dtype)


def _mm_nn(name, a, w, after=()):
    m, kb = a.shape
    n = w.shape[1]
    one_step = kb <= 2048
    tm = _tile(m, 512 if one_step else 1024)
    tk = kb if one_step else _tile(kb, 4096)
    tn = n if one_step else _tile(n, 1024)
    steps = kb // tk
    after = list(after)

    def body(a_ref, w_ref, *rest):
        o_ref, scratch = rest[len(after)], rest[len(after) + 1:]
        r = jnp.dot(a_ref[...], w_ref[...], preferred_element_type=F32)
        _accumulate(scratch[0] if scratch else None, o_ref, r, pl.program_id(2), steps - 1)

    return pl.pallas_call(
        body, name=name, grid=(m // tm, n // tn, steps),
        in_specs=[pl.BlockSpec((tm, tk), lambda i, j, s: (i, s)), pl.BlockSpec((tk, tn), lambda i, j, s: (s, j))]
        + [ANY] * len(after),
        out_specs=pl.BlockSpec((tm, tn), lambda i, j, s: (i, j)),
        out_shape=jax.ShapeDtypeStruct((m, n), BF16),
        scratch_shapes=[pltpu.VMEM((tm, tn), F32)] if steps > 1 else [],
        compiler_params=_cp(("parallel", "parallel", "arbitrary"), VMEM_MM))(a, w, *after)


def _mm_nt_rows(name, dy, w, act=None, out_dtype=BF16):
    m, n = dy.shape
    kw = w.shape[0]
    tm, tkw = _tile(m, 1024), _tile(kw, 1024)

    def body(dy_ref, w_ref, *rest):
        r = lax.dot_general(dy_ref[...], w_ref[...], NT_DIMS, preferred_element_type=F32)
        if act is None:
            rest[0][...] = r.astype(out_dtype)
        else:
            rest[1][...] = (r * rest[0][...].astype(F32)).astype(BF16)

    ins = [dy, w]
    in_specs = [pl.BlockSpec((tm, n), lambda j, i: (i, 0)), pl.BlockSpec((tkw, n), lambda j, i: (j, 0))]
    if act is not None:
        ins.append(act)
        in_specs.append(pl.BlockSpec((tm, tkw), lambda j, i: (i, j)))
    return pl.pallas_call(
        body, name=name, grid=(kw // tkw, m // tm), in_specs=in_specs,
        out_specs=pl.BlockSpec((tm, tkw), lambda j, i: (i, j)),
        out_shape=jax.ShapeDtypeStruct((m, kw), out_dtype if act is None else BF16),
        compiler_params=_cp(("parallel", "parallel"), VMEM_MM))(*ins)


def _mm_nt_blk(name, dy, wblk, after=None):
    m = dy.shape[0]
    _, kw, nb = wblk.shape
    tm, tkw, per = _tile(m, 1024), _tile(kw, 1024), 4

    extra = list(after or ())

    def body(dy_ref, w_ref, *rest):
        o_ref, acc = rest[len(extra):]
        if nb % MXU_WIDTH == 0:
            r = lax.dot_general(dy_ref[:, :nb], w_ref[0], NT_DIMS, preferred_element_type=F32)
            for t in range(1, per):
                r = r + lax.dot_general(dy_ref[:, t * nb:(t + 1) * nb], w_ref[t], NT_DIMS, preferred_element_type=F32)
        else:
            w = jnp.concatenate([w_ref[t] for t in range(per)], axis=1)
            r = lax.dot_general(dy_ref[...], w, NT_DIMS, preferred_element_type=F32)
        _accumulate(acc, o_ref, r, pl.program_id(2), NDEV // per - 1)

    return pl.pallas_call(
        body, name=name, grid=(m // tm, kw // tkw, NDEV // per),
        in_specs=[pl.BlockSpec((tm, per * nb), lambda i, j, s: (i, s)),
                  pl.BlockSpec((per, tkw, nb), lambda i, j, s: (s, j, 0))] + [ANY] * len(extra),
        out_specs=pl.BlockSpec((tm, tkw), lambda i, j, s: (i, j)),
        out_shape=jax.ShapeDtypeStruct((m, kw), BF16),
        scratch_shapes=[pltpu.VMEM((tm, tkw), F32)],
        compiler_params=_cp(("parallel", "parallel", "arbitrary"), VMEM_MM))(dy, wblk, *extra)


def _mm_tn(name, x, dy, nb=None):
    t, mx = x.shape
    n = dy.shape[1]
    tmx = _tile(mx, 512)
    per = 1 if nb is None else _blocks_per_step(nb)
    tn = per * nb if nb is not None else _tile(n, 1024)

    def body(x_ref, dy_ref, o_ref):
        r = lax.dot_general(x_ref[...], dy_ref[...], TN_DIMS, preferred_element_type=F32).astype(BF16)
        if nb is None:
            o_ref[...] = r
        else:
            for b in range(per):
                o_ref[b] = r[:, b * nb:(b + 1) * nb]

    if nb is None:
        out_shape = jax.ShapeDtypeStruct((mx, n), BF16)
        out_spec = pl.BlockSpec((tmx, tn), lambda j, i: (i, j))
    else:
        out_shape = jax.ShapeDtypeStruct((NDEV, mx, nb), BF16)
        out_spec = pl.BlockSpec((per, tmx, nb), lambda j, i: (j, i, 0))
    return pl.pallas_call(
        body, name=name, grid=(n // tn, mx // tmx),
        in_specs=[pl.BlockSpec((t, tmx), lambda j, i: (0, i)), pl.BlockSpec((t, tn), lambda j, i: (0, j))],
        out_specs=out_spec, out_shape=out_shape,
        compiler_params=_cp(("parallel", "parallel"), VMEM_MM))(x, dy)


def _gelu(x):
    return 0.5 * x * (1.0 + jnp.tanh(GELU_C * (x + GELU_A * (x * x * x))))


def _gelu_grad(x):
    t = jnp.tanh(GELU_C * (x + GELU_A * (x * x * x)))
    return 0.5 * (1.0 + t) + 0.5 * x * (1.0 - t * t) * (GELU_C * (1.0 + 3.0 * GELU_A * (x * x)))


def _layernorm(a):
    mu = jnp.mean(a, axis=-1, keepdims=True)
    ac = a - mu
    rstd = lax.rsqrt(jnp.mean(ac * ac, axis=-1, keepdims=True) + LN_EPS)
    return ac * rstd, rstd


def _shift_rows(z, halo, k):
    zr = pltpu.roll(z, k, 0)
    hr = pltpu.roll(halo, k, 0)
    row = lax.broadcasted_iota(jnp.int32, hr.shape, 0)
    top = jnp.where(row < k, hr, zr[:HALO])
    return jnp.concatenate([top, zr[HALO:]], axis=0)


def _shift_rows_up(z, halo, k):
    rows = z.shape[0]
    zr = pltpu.roll(z, rows - k, 0)
    hr = pltpu.roll(halo, HALO - k, 0)
    row = lax.broadcasted_iota(jnp.int32, hr.shape, 0)
    bot = jnp.where(row >= HALO - k, hr, zr[rows - HALO:])
    return jnp.concatenate([zr[:rows - HALO], bot], axis=0)


def _causal_mask():
    t = lax.broadcasted_iota(jnp.int32, (CHUNK, CHUNK), 0)
    s = lax.broadcasted_iota(jnp.int32, (CHUNK, CHUNK), 1)
    return s <= t


def _gate_specs(tm, rows, width):
    per = tm // HALO_BF16
    last = rows // HALO_BF16 - 1
    cur = pl.BlockSpec((tm, width), lambda i: (i, 0))
    prev = pl.BlockSpec((HALO_BF16, width), lambda i: (jnp.maximum(i * per - 1, 0), 0))
    nxt = pl.BlockSpec((HALO_BF16, width), lambda i: (jnp.minimum((i + 1) * per, last), 0))
    return cur, prev, nxt


def _cols(ref, lo, hi):
    return ref[:, lo:hi].astype(F32)


def _halo_before(ref, lo, hi):
    return ref[:, lo:hi].astype(F32)[HALO_BF16 - HALO:]


def _halo_after(ref, lo, hi):
    return ref[:, lo:hi].astype(F32)[:HALO]


def _gate_fwd(name, proj, w_s, b_st, cw):
    rows, width = proj.shape
    w = width // 5
    groups = w // CHUNK
    tm = _tile(rows, 256)
    cur, prev, _ = _gate_specs(tm, rows, width)

    def body(p_ref, h_ref, ws_ref, b_ref, cw_ref, o_ref):
        mask = _causal_mask()
        au = _gelu(_cols(p_ref, 0, w))
        vn, _ = _layernorm(_gelu(_cols(p_ref, w, 2 * w)))
        vn = vn.astype(BF16)
        for g in range(groups):
            wc = jnp.where(mask, ws_ref[g], 0.0).astype(BF16)
            cols = slice(g * CHUNK, (g + 1) * CHUNK)
            for ch in range(tm // CHUNK):
                rws = slice(ch * CHUNK, (ch + 1) * CHUNK)
                mixed = jnp.dot(wc, vn[rws, cols], preferred_element_type=F32) + b_ref[:, g:g + 1]
                o_ref[rws, cols] = (au[rws, cols] * mixed).astype(BF16)
        z = _cols(p_ref, 3 * w, 4 * w) * _cols(p_ref, 4 * w, 5 * w)
        zh = _halo_before(h_ref, 3 * w, 4 * w) * _halo_before(h_ref, 4 * w, 5 * w)
        zh = jnp.where(pl.program_id(0) == 0, 0.0, zh)
        y = cw_ref[0:1, :] * _shift_rows(z, zh, 2) + cw_ref[1:2, :] * _shift_rows(z, zh, 1) + cw_ref[2:3, :] * z
        o_ref[:, w:2 * w] = (_cols(p_ref, 2 * w, 3 * w) * y).astype(BF16)

    full = lambda a: pl.BlockSpec(a.shape, lambda i: (0,) * a.ndim)
    return pl.pallas_call(
        body, name=name, grid=(rows // tm,),
        in_specs=[cur, prev, full(w_s), full(b_st), full(cw)],
        out_specs=pl.BlockSpec((tm, 2 * w), lambda i: (i, 0)),
        out_shape=jax.ShapeDtypeStruct((rows, 2 * w), BF16),
        compiler_params=_cp(("parallel",)))(proj, proj, w_s, b_st, cw)


def _gate_bwd(name, proj, d_ab, w_s, b_st, cw):
    rows, width = proj.shape
    w = width // 5
    groups = w // CHUNK
    tm = _tile(rows, 256)
    cur, prev, nxt = _gate_specs(tm, rows, width)
    dcur, _, dnxt = _gate_specs(tm, rows, 2 * w)

    def body(p_ref, ph_ref, pn_ref, d_ref, dn_ref, ws_ref, b_ref, cw_ref, o_ref, dws_ref, dbs_ref, dcw_ref):
        i = pl.program_id(0)

        @pl.when(i == 0)
        def _():
            dws_ref[...] = jnp.zeros_like(dws_ref)
            dbs_ref[...] = jnp.zeros_like(dbs_ref)
            dcw_ref[...] = jnp.zeros_like(dcw_ref)

        mask = _causal_mask()
        u, v = _cols(p_ref, 0, w), _cols(p_ref, w, 2 * w)
        au, av = _gelu(u), _gelu(v)
        vn, rstd = _layernorm(av)
        vnb = vn.astype(BF16)
        d_a = _cols(d_ref, 0, w)
        d_mixed = (d_a * au).astype(BF16)
        ones = jnp.ones((HALO, CHUNK), BF16)
        d_vn_cols = []
        d_au_cols = []
        for g in range(groups):
            wc = jnp.where(mask, ws_ref[g], 0.0).astype(BF16)
            cols = slice(g * CHUNK, (g + 1) * CHUNK)
            dw = jnp.zeros((CHUNK, CHUNK), F32)
            db = jnp.zeros((HALO, CHUNK), F32)
            d_vn_rows, d_au_rows = [], []
            for ch in range(tm // CHUNK):
                rws = slice(ch * CHUNK, (ch + 1) * CHUNK)
                mixed = jnp.dot(wc, vnb[rws, cols], preferred_element_type=F32) + b_ref[:, g:g + 1]
                d_au_rows.append(d_a[rws, cols] * mixed)
                dm = d_mixed[rws, cols]
                dw = dw + lax.dot_general(dm, vnb[rws, cols], NT_DIMS, preferred_element_type=F32)
                db = db + lax.dot_general(ones, dm, NT_DIMS, preferred_element_type=F32)
                d_vn_rows.append(lax.dot_general(wc, dm, TN_DIMS, preferred_element_type=F32))
            dws_ref[g] += jnp.where(mask, dw, 0.0)
            dbs_ref[g:g + 1, :] += db[0:1, :]
            d_vn_cols.append(jnp.concatenate(d_vn_rows, axis=0))
            d_au_cols.append(jnp.concatenate(d_au_rows, axis=0))
        d_vn = jnp.concatenate(d_vn_cols, axis=1)
        d_au = jnp.concatenate(d_au_cols, axis=1)
        d_av = rstd * (d_vn - jnp.mean(d_vn, axis=-1, keepdims=True)
                       - vn * jnp.mean(d_vn * vn, axis=-1, keepdims=True))
        o_ref[:, 0:w] = (d_au * _gelu_grad(u)).astype(BF16)
        o_ref[:, w:2 * w] = (d_av * _gelu_grad(v)).astype(BF16)

        gb, gc, bx = _cols(p_ref, 2 * w, 3 * w), _cols(p_ref, 3 * w, 4 * w), _cols(p_ref, 4 * w, 5 * w)
        z = gc * bx
        zh = jnp.where(i == 0, 0.0, _halo_before(ph_ref, 3 * w, 4 * w) * _halo_before(ph_ref, 4 * w, 5 * w))
        z1, z2 = _shift_rows(z, zh, 1), _shift_rows(z, zh, 2)
        d_b = _cols(d_ref, w, 2 * w)
        y = cw_ref[0:1, :] * z2 + cw_ref[1:2, :] * z1 + cw_ref[2:3, :] * z
        dy = d_b * gb
        dyn = jnp.where(i == pl.num_programs(0) - 1, 0.0,
                        _halo_after(dn_ref, w, 2 * w) * _halo_after(pn_ref, 2 * w, 3 * w))
        dz = (cw_ref[2:3, :] * dy + cw_ref[1:2, :] * _shift_rows_up(dy, dyn, 1)
              + cw_ref[0:1, :] * _shift_rows_up(dy, dyn, 2))
        dcw_ref[0:1, :] += jnp.sum(dy * z2, axis=0, keepdims=True)
        dcw_ref[1:2, :] += jnp.sum(dy * z1, axis=0, keepdims=True)
        dcw_ref[2:3, :] += jnp.sum(dy * z, axis=0, keepdims=True)
        o_ref[:, 2 * w:3 * w] = (d_b * y).astype(BF16)
        o_ref[:, 3 * w:4 * w] = (dz * bx).astype(BF16)
        o_ref[:, 4 * w:5 * w] = (dz * gc).astype(BF16)

    full = lambda a: pl.BlockSpec(a.shape, lambda i: (0,) * a.ndim)
    acc = lambda shape: pl.BlockSpec(shape, lambda i: (0,) * len(shape))
    return pl.pallas_call(
        body, name=name, grid=(rows // tm,),
        in_specs=[cur, prev, nxt, dcur, dnxt, full(w_s), full(b_st), full(cw)],
        out_specs=[pl.BlockSpec((tm, width), lambda i: (i, 0)), acc((groups, CHUNK, CHUNK)),
                   acc((groups, CHUNK)), acc((HALO, w))],
        out_shape=[jax.ShapeDtypeStruct((rows, width), BF16), jax.ShapeDtypeStruct((groups, CHUNK, CHUNK), F32),
                   jax.ShapeDtypeStruct((groups, CHUNK), F32), jax.ShapeDtypeStruct((HALO, w), F32)],
        compiler_params=_cp(("arbitrary",), VMEM_MM))(proj, proj, proj, d_ab, d_ab, w_s, b_st, cw)


def _flat(x):
    return x.reshape(-1, x.shape[-1])


def _rope(t, cosf, sins):
    t2 = _flat(t)
    return (t2 * _flat(cosf) + pltpu.roll(t2, HEAD // 2, 1) * _flat(sins)).reshape(t.shape)


def _rope_bwd(dt, cosf, sins):
    d2 = _flat(dt)
    return (d2 * _flat(cosf) + pltpu.roll(d2 * _flat(sins), HEAD // 2, 1)).reshape(dt.shape)


ATT_UNITS = ATT_TILE // CHUNK


def _attn_units(phases):
    for b, d in enumerate(DILATIONS):
        blocks = ATT_TILE // (CHUNK * d)
        for visit in phases:
            for r in range(d):
                if blocks <= ATT_UNROLL:
                    for j in range(blocks):
                        visit(b, d, r, j, r * blocks + j)
                else:
                    def step(jj, carry, b=b, d=d, r=r, visit=visit, blocks=blocks):
                        for u in range(ATT_UNROLL):
                            j = jj * ATT_UNROLL + u
                            visit(b, d, r, j, r * blocks + j)
                        return carry
                    lax.fori_loop(0, blocks // ATT_UNROLL, step, 0)


class _Unit:
    def __init__(self, d, r, j):
        self.segs = [r + d * k for k in range(SEGS // d)]
        self.w = CHUNK * d // SEGS
        q0 = j * self.w
        self.q0 = q0 if isinstance(q0, int) else pl.multiple_of(q0, HALO)
        k0 = CHUNK + (j - 1) * self.w
        self.k0 = k0 if isinstance(k0, int) else pl.multiple_of(k0, HALO)

    def queries(self, ref):
        return _chunks(ref, self.segs, self.q0, self.w)

    def keys(self, ref):
        return _chunks(ref, self.segs, self.k0, 2 * self.w)

    def put_queries(self, ref, val, add=False):
        _put_chunks(ref, self.segs, self.q0, self.w, val, add)

    def put_keys(self, ref, val, add=False):
        _put_chunks(ref, self.segs, self.k0, 2 * self.w, val, add)


def _chunks(ref, segs, start, size):
    parts = [ref[s, pl.ds(start, size), :] for s in segs]
    return parts[0] if len(parts) == 1 else jnp.concatenate(parts, axis=0)


def _put_chunks(ref, segs, start, size, val, add):
    for k, s in enumerate(segs):
        piece = val[k * size:(k + 1) * size]
        if add:
            ref[s, pl.ds(start, size), :] += piece
        else:
            ref[s, pl.ds(start, size), :] = piece


def _band_bias():
    qi = lax.broadcasted_iota(jnp.int32, (CHUNK, 2 * CHUNK), 0)
    ki = lax.broadcasted_iota(jnp.int32, (CHUNK, 2 * CHUNK), 1)
    tables = []
    for d in DILATIONS:
        nseg, w = SEGS // d, CHUNK * d // SEGS
        pos_q = nseg * (qi % w) + qi // w
        pos_k = nseg * (ki % (2 * w) - w) + ki // (2 * w)
        band = (pos_q >= pos_k) & (pos_q - pos_k <= CHUNK)
        tables += [jnp.where(band, 0.0, -jnp.inf), jnp.where(band & (pos_k >= 0), 0.0, -jnp.inf)]
    return jnp.stack(tables).astype(F32)


def _bias_spec():
    return pl.BlockSpec((2 * len(DILATIONS), CHUNK, 2 * CHUNK), lambda h, n: (0, 0, 0))


def _unit_bias(bias, b, n, j):
    if isinstance(j, int) and j != 0:
        return bias[2 * b]
    return bias[2 * b + jnp.where(jnp.logical_and(n == 0, j == 0), 1, 0)]


def _attn_in_specs(heads):
    blk = (SEGS, CHUNK, HEAD)
    prev = lambda n: jnp.maximum(n - 1, 0)
    return [
        pl.BlockSpec(blk, lambda h, n: (0, n, h)),
        pl.BlockSpec(blk, lambda h, n: (0, n, heads + h)),
        pl.BlockSpec(blk, lambda h, n: (0, prev(n), heads + h)),
        pl.BlockSpec(blk, lambda h, n: (0, n, 2 * heads + h)),
        pl.BlockSpec(blk, lambda h, n: (0, prev(n), 2 * heads + h)),
        pl.BlockSpec(blk, lambda h, n: (0, n, 0)),
        pl.BlockSpec(blk, lambda h, n: (0, n, 0)),
        pl.BlockSpec(blk, lambda h, n: (0, prev(n), 0)),
        pl.BlockSpec(blk, lambda h, n: (0, prev(n), 0)),
    ]


def _attn_load(q_ref, kc_ref, kp_ref, vc_ref, vp_ref, cc_ref, sc_ref, cp_ref, sp_ref, qr, kcat, vcat):
    qr[...] = _rope(q_ref[...].astype(F32), cc_ref[...], sc_ref[...]) * (HEAD ** -0.5)
    kcat[:, pl.ds(0, CHUNK), :] = _rope(kp_ref[...].astype(F32), cp_ref[...], sp_ref[...])
    kcat[:, pl.ds(CHUNK, CHUNK), :] = _rope(kc_ref[...].astype(F32), cc_ref[...], sc_ref[...])
    vcat[:, pl.ds(0, CHUNK), :] = vp_ref[...].astype(F32)
    vcat[:, pl.ds(CHUNK, CHUNK), :] = vc_ref[...].astype(F32)


def _attn_fwd(name, qkv, cosf, sins, bias):
    t = qkv.shape[0]
    heads = qkv.shape[1] // (3 * HEAD)
    nbr = len(DILATIONS)

    def body(q_ref, kc_ref, kp_ref, vc_ref, vp_ref, cc_ref, sc_ref, cp_ref, sp_ref, bias, o_ref, lse_ref,
             qr, kcat, vcat, obr, mbr, dbr, pn):
        n = pl.program_id(1)
        _attn_load(q_ref, kc_ref, kp_ref, vc_ref, vp_ref, cc_ref, sc_ref, cp_ref, sp_ref, qr, kcat, vcat)

        def probs(b, d, r, j, u):
            unit = _Unit(d, r, j)
            s = lax.dot_general(unit.queries(qr).astype(BF16), unit.keys(kcat).astype(BF16), NT_DIMS,
                                preferred_element_type=F32) + _unit_bias(bias, b, n, j)
            mx = jnp.max(s, axis=-1, keepdims=True)
            p = jnp.exp(s - mx)
            pn[u] = p.astype(BF16)
            unit.put_queries(mbr.at[b], jnp.broadcast_to(mx, (CHUNK, HEAD)))
            unit.put_queries(dbr.at[b], jnp.broadcast_to(jnp.sum(p, axis=-1, keepdims=True), (CHUNK, HEAD)))

        def values(b, d, r, j, u):
            unit = _Unit(d, r, j)
            unit.put_queries(obr.at[b], jnp.dot(pn[u], unit.keys(vcat).astype(BF16), preferred_element_type=F32))

        _attn_units([probs, values])
        ms = [mbr[b] for b in range(nbr)]
        top = functools.reduce(jnp.maximum, ms)
        ws = [jnp.exp(m - top) for m in ms]
        tot = functools.reduce(jnp.add, [ws[b] * dbr[b] for b in range(nbr)])
        inv = 1.0 / tot
        o = (ws[0] * inv) * obr[0]
        for b in range(1, nbr):
            o = o + (ws[b] * inv) * obr[b]
        o_ref[...] = o.astype(BF16)
        lse_ref[...] = top + jnp.log(tot)

    blk = (SEGS, CHUNK, HEAD)
    keys = pltpu.VMEM((SEGS, 2 * CHUNK, HEAD), F32)
    tile = pl.BlockSpec(blk, lambda h, n: (0, n, h))
    seg = t // SEGS
    qkv3, cos3, sin3 = _seg_view(qkv), _seg_view(cosf), _seg_view(sins)
    o, lse = pl.pallas_call(
        body, name=name, grid=(heads, t // ATT_TILE), in_specs=_attn_in_specs(heads) + [_bias_spec()],
        out_specs=[tile, tile],
        out_shape=[jax.ShapeDtypeStruct((SEGS, seg, heads * HEAD), BF16),
                   jax.ShapeDtypeStruct((SEGS, seg, heads * HEAD), F32)],
        scratch_shapes=[pltpu.VMEM(blk, F32), keys, keys,
                        pltpu.VMEM((nbr,) + blk, F32), pltpu.VMEM((nbr,) + blk, F32), pltpu.VMEM((nbr,) + blk, F32),
                        pltpu.VMEM((ATT_UNITS, CHUNK, 2 * CHUNK), BF16)],
        compiler_params=_cp(("parallel", "parallel"), VMEM_MM),
    )(qkv3, qkv3, qkv3, qkv3, qkv3, cos3, sin3, cos3, sin3, bias)
    return o.reshape(t, heads * HEAD), lse.reshape(t, heads * HEAD)


def _attn_bwd(name, qkv, cosf, sins, bias, d_o, o, lse):
    t = qkv.shape[0]
    heads = qkv.shape[1] // (3 * HEAD)
    scale = HEAD ** -0.5

    def body(q_ref, kc_ref, kp_ref, vc_ref, vp_ref, cc_ref, sc_ref, cp_ref, sp_ref, do_ref, o_ref, lse_ref, bias,
             dq_ref, dko_ref, dkp_ref, dvo_ref, dvp_ref, qr, kcat, vcat, dq_acc, dk_acc, dv_acc, delta, ps, dss):
        n = pl.program_id(1)
        _attn_load(q_ref, kc_ref, kp_ref, vc_ref, vp_ref, cc_ref, sc_ref, cp_ref, sp_ref, qr, kcat, vcat)
        dq_acc[...] = jnp.zeros_like(dq_acc)
        dk_acc[...] = jnp.zeros_like(dk_acc)
        dv_acc[...] = jnp.zeros_like(dv_acc)
        delta[...] = jnp.broadcast_to(
            jnp.sum(do_ref[...] * o_ref[...].astype(F32), axis=-1, keepdims=True), delta.shape)

        def probs(b, d, r, j, u):
            unit = _Unit(d, r, j)
            s = lax.dot_general(unit.queries(qr).astype(BF16), unit.keys(kcat).astype(BF16), NT_DIMS,
                                preferred_element_type=F32) + _unit_bias(bias, b, n, j)
            ps[u] = jnp.exp(s - unit.queries(lse_ref)[:, 0:1]).astype(BF16)

        def score_grads(b, d, r, j, u):
            unit = _Unit(d, r, j)
            dp = lax.dot_general(unit.queries(do_ref).astype(BF16), unit.keys(vcat).astype(BF16), NT_DIMS,
                                 preferred_element_type=F32)
            dss[u] = (ps[u].astype(F32) * (dp - unit.queries(delta)[:, 0:1])).astype(BF16)

        def input_grads(b, d, r, j, u):
            unit = _Unit(d, r, j)
            ds = dss[u]
            unit.put_queries(dq_acc, jnp.dot(ds, unit.keys(kcat).astype(BF16), preferred_element_type=F32), add=True)
            unit.put_keys(dk_acc, lax.dot_general(ds, unit.queries(qr).astype(BF16), TN_DIMS,
                                                  preferred_element_type=F32), add=True)
            unit.put_keys(dv_acc, lax.dot_general(ps[u], unit.queries(do_ref).astype(BF16), TN_DIMS,
                                                  preferred_element_type=F32), add=True)

        _attn_units([probs, score_grads, input_grads])
        dq_ref[...] = _rope_bwd(dq_acc[...] * scale, cc_ref[...], sc_ref[...]).astype(BF16)
        dkp_ref[...] = _rope_bwd(dk_acc[:, pl.ds(0, CHUNK), :], cp_ref[...], sp_ref[...]).astype(BF16)
        dko_ref[...] = _rope_bwd(dk_acc[:, pl.ds(CHUNK, CHUNK), :], cc_ref[...], sc_ref[...]).astype(BF16)
        dvp_ref[...] = dv_acc[:, pl.ds(0, CHUNK), :].astype(BF16)
        dvo_ref[...] = dv_acc[:, pl.ds(CHUNK, CHUNK), :].astype(BF16)

    blk = (SEGS, CHUNK, HEAD)
    tile = pl.BlockSpec(blk, lambda h, n: (0, n, h))
    big = pltpu.VMEM((SEGS, 2 * CHUNK, HEAD), F32)
    seg = t // SEGS
    qkv3, cos3, sin3 = _seg_view(qkv), _seg_view(cosf), _seg_view(sins)
    return pl.pallas_call(
        body, name=name, grid=(heads, t // ATT_TILE),
        in_specs=_attn_in_specs(heads) + [tile, tile, tile, _bias_spec()],
        out_specs=[tile] * 5,
        out_shape=[jax.ShapeDtypeStruct((SEGS, seg, heads * HEAD), BF16)] * 5,
        scratch_shapes=[pltpu.VMEM(blk, F32), big, big, pltpu.VMEM(blk, F32), big, big, pltpu.VMEM(blk, F32),
                        pltpu.VMEM((ATT_UNITS, CHUNK, 2 * CHUNK), BF16), pltpu.VMEM((ATT_UNITS, CHUNK, 2 * CHUNK), BF16)],
        compiler_params=_cp(("parallel", "parallel"), 60 << 20),
    )(qkv3, qkv3, qkv3, qkv3, qkv3, cos3, sin3, cos3, sin3, _seg_view(d_o), _seg_view(o), _seg_view(lse), bias)


def _attn_merge(name, dq, dk_own, dk_prev, dv_own, dv_prev):
    _, seg, hd = dq.shape
    nt = seg // CHUNK
    tw = _tile(hd, 512)

    def body(dq_ref, dko_ref, dkn_ref, dvo_ref, dvn_ref, o_ref):
        last = pl.program_id(0) == nt - 1
        part = pl.program_id(1)

        @pl.when(part == 0)
        def _():
            o_ref[...] = dq_ref[...]

        @pl.when(part == 1)
        def _():
            o_ref[...] = (dko_ref[...].astype(F32) + jnp.where(last, 0.0, dkn_ref[...].astype(F32))).astype(BF16)

        @pl.when(part == 2)
        def _():
            o_ref[...] = (dvo_ref[...].astype(F32) + jnp.where(last, 0.0, dvn_ref[...].astype(F32))).astype(BF16)

    blk = (SEGS, CHUNK, tw)

    def own(part):
        return pl.BlockSpec(blk, lambda n, p, c: (0, jnp.where(p == part, n, 0), jnp.where(p == part, c, 0)))

    def nxt(part):
        return pl.BlockSpec(blk, lambda n, p, c: (0, jnp.where(p == part, jnp.minimum(n + 1, nt - 1), 0),
                                                  jnp.where(p == part, c, 0)))

    per = hd // tw
    return pl.pallas_call(
        body, name=name, grid=(nt, 3, per), in_specs=[own(0), own(1), nxt(1), own(2), nxt(2)],
        out_specs=pl.BlockSpec(blk, lambda n, p, c: (0, n, p * per + c)),
        out_shape=jax.ShapeDtypeStruct((SEGS, seg, 3 * hd), BF16),
        compiler_params=_cp(("parallel", "parallel", "parallel"), VMEM_MM),
    )(dq, dk_own, dk_prev, dv_own, dv_prev).reshape(SEGS * seg, 3 * hd)


def _sum_parts(name, parts):
    nparts, rows, cols = parts.shape
    tr = _tile(rows, 256)

    def body(p_ref, o_ref):
        s = p_ref[0]
        for k in range(1, nparts):
            s = s + p_ref[k]
        o_ref[...] = s

    return pl.pallas_call(
        body, name=name, grid=(rows // tr,),
        in_specs=[pl.BlockSpec((nparts, tr, cols), lambda i: (0, i, 0))],
        out_specs=pl.BlockSpec((tr, cols), lambda i: (i, 0)),
        out_shape=jax.ShapeDtypeStruct((rows, cols), F32),
        compiler_params=_cp(("parallel",)))(parts)


def _rows128(a, pad_to=8):
    flat = a.reshape(-1)
    rows = -(-flat.shape[0] // 128)
    rows = -(-rows // pad_to) * pad_to
    flat = jnp.pad(flat, (0, rows * 128 - flat.shape[0]))
    return flat.reshape(rows, 128)


def _pack(arrays):
    return jnp.concatenate([_rows128(a) for a in arrays], axis=0)


def _unpack(packed, like):
    out, at = [], 0
    for a in like:
        size = 1
        for s in a.shape:
            size *= s
        rows = -(-(-(-size // 128)) // 8) * 8
        out.append(packed[at:at + rows].reshape(-1)[:size].reshape(a.shape))
        at += rows
    return out


def kernel(x, norm_mix_pre, norm_mix_post, norm_mlp_pre, norm_mlp_post, w_in_ab, w_spatial, b_spatial, conv_w, w_out_ab, w_qkv, w_o, w_up, w_down, loss_target, m_norm_mix_pre, m_norm_mix_post, m_norm_mlp_pre, m_norm_mlp_post, m_w_in_ab, m_w_spatial, m_b_spatial, m_conv_w, m_w_out_ab, m_w_qkv, m_w_o, m_w_up, m_w_down, v_norm_mix_pre, v_norm_mix_post, v_norm_mlp_pre, v_norm_mlp_post, v_w_in_ab, v_w_spatial, v_b_spatial, v_conv_w, v_w_out_ab, v_w_qkv, v_w_o, v_w_up, v_w_down):
    depth = norm_mix_pre.shape[0]
    seq, dm = x.shape[1], x.shape[2]
    h0 = x.reshape(seq, dm)
    target = loss_target.reshape(seq, dm)
    ax, ay, ac = lax.axis_index("x"), lax.axis_index("y"), lax.axis_index("c")
    my_block = 4 * ax + 2 * ay + ac
    block = jnp.reshape(my_block, (1,)).astype(jnp.int32)

    half = HEAD // 2
    inv_freq = ROPE_THETA ** (-jnp.arange(half, dtype=F32) * 2.0 / HEAD)
    ang = jnp.arange(seq, dtype=jnp.int32).astype(F32)[:, None] * inv_freq[None, :]
    ang = ang.reshape(seq // SEGS, SEGS, half).transpose(1, 0, 2).reshape(seq, half)
    cosf = jnp.concatenate([jnp.cos(ang), jnp.cos(ang)], axis=-1)
    sins = jnp.concatenate([-jnp.sin(ang), jnp.sin(ang)], axis=-1)
    band_bias = _band_bias()

    big = {"w_in_ab": w_in_ab, "w_out_ab": w_out_ab, "w_qkv": w_qkv, "w_o": w_o, "w_up": w_up, "w_down": w_down}
    use_order = []
    for l in range(depth):
        use_order += [("w_in_ab", l // 2), ("w_out_ab", l // 2)] if l % 2 == 0 else [("w_qkv", l // 2), ("w_o", l // 2)]
        use_order += [("w_up", l), ("w_down", l)]
    n_even = w_in_ab.shape[0]
    cw_rows = jnp.pad(conv_w.reshape(n_even * CONV_TAPS, conv_w.shape[2]), ((0, HALO - (n_even * CONV_TAPS) % HALO), (0, 0)))
    cw_gathered = _all_gather("ag_conv", [cw_rows])[0]
    first = [k for k in use_order if k in (("w_in_ab", 0), ("w_out_ab", 0), ("w_up", 0), ("w_down", 0))]
    rest = [k for k in use_order if k not in first]
    lands_a, sems_a = _ag_start("ag_start_first", [_cast_fill(f"cast_{nm}_{l}", big[nm], l, block) for nm, l in first],
                                cw_gathered)
    lands_b, sems_b = _ag_start("ag_start_rest", [_cast_fill(f"cast_{nm}_{l}", big[nm], l, block) for nm, l in rest],
                                lands_a[0])
    lands = dict(zip(first + rest, list(lands_a) + list(lands_b)))
    ag_sems = dict(zip(first + rest, list(sems_a) + list(sems_b)))
    passed_on, wg = [], {}

    def weight(key, after):
        pins = []
        if key not in wg:
            upto = min(use_order.index(key) + 1, len(use_order) - 1)
            for k in use_order[len(passed_on):upto + 1]:
                lands[k] = _ag_mid(f"ag_mid_{k[0]}_{k[1]}", lands[k], ag_sems[k], after)
                passed_on.append(k)
                if k != key and use_order.index(k) >= AG_PIN_FROM:
                    pins.append(lands[k])
            wg[key] = _ag_wait(f"ag_wait_{key[0]}_{key[1]}", lands[key], ag_sems[key], after)
        return wg[key], pins

    cw_all = cw_gathered[:, :n_even * CONV_TAPS].reshape(NDEV, n_even, CONV_TAPS, -1)
    cw_all = jnp.transpose(cw_all, (1, 2, 0, 3)).reshape(n_even, CONV_TAPS, -1)
    cw_full = [jnp.pad(cw_all[e], ((0, HALO - CONV_TAPS), (0, 0))) for e in range(n_even)]

    def rows_nat(blk):
        return blk.reshape(blk.shape[0] * blk.shape[1], blk.shape[2])

    saved = []
    hn = _norm_fwd("norm_first", h0, g_pre=norm_mix_pre[0][None])[0]
    h = h0
    for l in range(depth):
        s = {"h_in": h, "hn1": hn}
        if l % 2 == 0:
            e = l // 2
            w_, pins = weight(("w_in_ab", e), hn)
            proj = _mm_nn_blk(f"fwd_in_{l}", hn, w_, after=pins)
            ab = _gate_fwd(f"gate_fwd_{l}", proj, w_spatial[e], b_spatial[e].T, cw_full[e])
            w_, pins = weight(("w_out_ab", e), ab)
            mix = _mm_nn(f"fwd_out_{l}", ab, rows_nat(w_), after=pins)
            s.update(proj=proj, ab=ab)
        else:
            o_ = l // 2
            w_, pins = weight(("w_qkv", o_), hn)
            qkv = _mm_nn_blk(f"fwd_qkv_{l}", hn, w_, after=pins)
            att, lse = _attn_fwd(f"attn_fwd_{l}", qkv, cosf, sins, band_bias)
            w_, pins = weight(("w_o", o_), att)
            mix = _mm_nn(f"fwd_o_{l}", att, rows_nat(w_), after=pins)
            s.update(qkv=qkv, att=att, lse=lse)
        h1, hn2 = _norm_fwd(f"norm_mid_{l}", h, mix, norm_mix_post[l][None], norm_mlp_pre[l][None], seg_z=l % 2 == 1)
        w_, pins = weight(("w_up", l), hn2)
        act, act_grad = _mm_nn_blk(f"fwd_up_{l}", hn2, w_, relu2=True, after=pins)
        w_, pins = weight(("w_down", l), act)
        f = _mm_nn(f"fwd_down_{l}", act, rows_nat(w_), after=pins)
        s.update(mix=mix, h1=h1, hn2=hn2, act=act, act_grad=act_grad, f=f)
        if l + 1 < depth:
            h, hn = _norm_fwd(f"norm_end_{l}", h1, f, norm_mlp_post[l][None], norm_mix_pre[l + 1][None],
                              seg_y=(l + 1) % 2 == 1)
        else:
            h = _norm_fwd(f"norm_end_{l}", h1, f, norm_mlp_post[l][None])[0]
        saved.append(s)

    d_h, loss_row = _loss_grad("loss", h, target)
    rs = {}

    def scatter(key, g):
        rs[key] = _rs_start(f"rs_start_{key[0]}_{key[1]}", g.reshape(NDEV, -1, g.shape[-1]))

    dg ={nm: [None] * depth for nm in ("norm_mix_pre", "norm_mix_post", "norm_mlp_pre", "norm_mlp_post")}
    d_ws, d_bs, d_cw = [None] * n_even, [None] * n_even, [None] * n_even
    d_hn_next = None
    for l in reversed(range(depth)):
        s = saved[l]
        if l == depth - 1:
            d_f, dg["norm_mlp_post"][l] = _norm_bwd(f"nb_end_{l}", d_h, post=(s["f"], norm_mlp_post[l][None]))
        else:
            d_h, dg["norm_mix_pre"][l + 1], d_f, dg["norm_mlp_post"][l] = _norm_bwd(
                f"nb_end_{l}", d_h, pre=(d_hn_next, saved[l + 1]["h_in"], norm_mix_pre[l + 1][None]),
                post=(s["f"], norm_mlp_post[l][None]), seg_dy=(l + 1) % 2 == 1)
        wd = rows_nat(wg[("w_down", l)])
        d_up = _mm_nt_rows(f"bwd_down_{l}", d_f, wd, act=s["act_grad"])
        scatter(("w_down", l), _mm_tn(f"gw_down_{l}", s["act"], d_f))
        scatter(("w_up", l), _mm_tn(f"gw_up_{l}", s["hn2"], d_up, nb=w_up.shape[2]))
        d_hn2 = _mm_nt_blk(f"bwd_up_{l}", d_up, wg[("w_up", l)], after=[rs[("w_down", l)][0], rs[("w_up", l)][0]])
        d_h, dg["norm_mlp_pre"][l], d_mix, dg["norm_mix_post"][l] = _norm_bwd(
            f"nb_mid_{l}", d_h, pre=(d_hn2, s["h1"], norm_mlp_pre[l][None]),
            post=(s["mix"], norm_mix_post[l][None]), seg_z=l % 2 == 1)
        if l % 2 == 0:
            e = l // 2
            wo = rows_nat(wg[("w_out_ab", e)])
            d_ab = _mm_nt_rows(f"bwd_out_{l}", d_mix, wo)
            scatter(("w_out_ab", e), _mm_tn(f"gw_out_{l}", s["ab"], d_mix))
            d_proj, d_ws[e], d_bs[e], d_cw[e] = _gate_bwd(
                f"gate_bwd_{l}", s["proj"], d_ab, w_spatial[e], b_spatial[e].T, cw_full[e])
            scatter(("w_in_ab", e), _mm_tn(f"gw_in_{l}", s["hn1"], d_proj, nb=w_in_ab.shape[2]))
            d_hn_next = _mm_nt_blk(f"bwd_in_{l}", d_proj, wg[("w_in_ab", e)],
                                   after=[rs[("w_out_ab", e)][0], rs[("w_in_ab", e)][0]])
        else:
            o_ = l // 2
            wo = rows_nat(wg[("w_o", o_)])
            d_att = _mm_nt_rows(f"bwd_o_{l}", d_mix, wo, out_dtype=F32)
            scatter(("w_o", o_), _mm_tn(f"gw_o_{l}", s["att"], d_mix))
            parts = _attn_bwd(f"attn_bwd_{l}", s["qkv"], cosf, sins, band_bias, d_att, s["att"], s["lse"])
            d_qkv = _attn_merge(f"attn_merge_{l}", *parts)
            scatter(("w_qkv", o_), _mm_tn(f"gw_qkv_{l}", s["hn1"], d_qkv, nb=w_qkv.shape[2]))
            d_hn_next = _mm_nt_blk(f"bwd_qkv_{l}", d_qkv, wg[("w_qkv", o_)],
                                   after=[rs[("w_o", o_)][0], rs[("w_qkv", o_)][0]])
    grad_x, dg["norm_mix_pre"][0] = _norm_bwd("nb_first", d_h, pre=(d_hn_next, h0, norm_mix_pre[0][None]))

    small_g = ([jnp.concatenate(dg[nm], axis=0) for nm in dg]
               + [jnp.stack(d_ws), jnp.stack(d_bs), jnp.stack([c[:CONV_TAPS] for c in d_cw]), loss_row])
    small_land, small_send, small_recv = _ag_direct_start("ag_small_start", _fill_slot("fill_small", _pack(small_g), block))

    moments = {"w_in_ab": (m_w_in_ab, v_w_in_ab), "w_out_ab": (m_w_out_ab, v_w_out_ab), "w_qkv": (m_w_qkv, v_w_qkv),
               "w_o": (m_w_o, v_w_o), "w_up": (m_w_up, v_w_up), "w_down": (m_w_down, v_w_down)}
    out_big = {}
    behind = small_land
    for nm in ("w_o", "w_qkv", "w_down", "w_up", "w_out_ab", "w_in_ab"):
        own, landed = [], []
        for l in range(big[nm].shape[0]):
            g, land = _wait_all(f"rs_wait_{nm}_{l}", *rs[(nm, l)], behind)
            own.append(g)
            landed.append(land)
        out_big[nm] = _adamw_layers(f"adamw_{nm}", own, landed, block, big[nm], moments[nm][0], moments[nm][1])
        behind = out_big[nm][0]

    summed = _sum_parts("sum_small", _ag_direct_wait("ag_small_wait", small_land, small_send, small_recv, behind))
    g_nmp, g_nmo, g_nlp, g_nlo, g_ws, g_bs, g_cw_all, loss_sum = _unpack(summed, small_g)
    loss = loss_sum[0, 0]
    cwb = conv_w.shape[2]
    g_cw = lax.dynamic_slice_in_dim(g_cw_all, my_block * cwb, cwb, axis=2)
    small_w = [norm_mix_pre, norm_mix_post, norm_mlp_pre, norm_mlp_post, w_spatial, b_spatial, conv_w]
    small_m = [m_norm_mix_pre, m_norm_mix_post, m_norm_mlp_pre, m_norm_mlp_post, m_w_spatial, m_b_spatial, m_conv_w]
    small_v = [v_norm_mix_pre, v_norm_mix_post, v_norm_mlp_pre, v_norm_mlp_post, v_w_spatial, v_b_spatial, v_conv_w]
    small_grad = [g_nmp, g_nmo, g_nlp, g_nlo, g_ws, g_bs, g_cw]
    upd = _adamw("adamw_small", _pack(small_grad)[None], _pack(small_w), _pack(small_m), _pack(small_v))
    sg, sd, sm, sv = [_unpack(u, small_w) for u in upd]

    def outs(i_small, i_big):
        return (i_small[0], i_small[1], i_small[2], i_small[3], i_big["w_in_ab"], i_small[4], i_small[5], i_small[6],
                i_big["w_out_ab"], i_big["w_qkv"], i_big["w_o"], i_big["w_up"], i_big["w_down"])

    pick = lambda i: {nm: out_big[nm][i] for nm in big}
    return (loss, grad_x.reshape(x.shape), *outs(sg, pick(0)), *outs(sd, pick(1)), *outs(sm, pick(2)),
            *outs(sv, pick(3)))
```
